```python
import math
import jax
import jax.numpy as jnp
from jax import lax
import numpy as np

D_MODEL = 1024
BATCH = 16
SEQ = 256
DEPTH = 2
DEC_BATCH = 2
DEC_SEQ = 1024
PAST_LEN = 256

GRID_W = 64
N_EVEN = (DEPTH + 1) // 2
N_ODD = DEPTH // 2
W_A = D_MODEL // 2
HD_A = 64
H_A = W_A // (2 * HD_A)
W_B = D_MODEL - W_A
HD_B = 128
H_B = W_B // HD_B
AB_SPLITS = (W_A, 2 * W_A, 3 * W_A, 3 * W_A + 2 * W_B, 3 * W_A + 3 * W_B, 3 * W_A + 4 * W_B)
AB_PROJ = 3 * W_A + 4 * W_B + 4 * H_B
ROPE_BASE = 10000.0
Q_BLOCK = 128
CHUNK = 64
HY_ORDER = 2
HY_PROJ = (HY_ORDER + 1) * D_MODEL
HY_BANDS = 8
HY_EMB = 1 + 2 * HY_BANDS
HY_FH = 64
HY_TARGET = 1e-2
HY_FAST_PCT = 0.3
HY_SLOW_PCT = 1.5
N_EXPERTS = 32
TOP_K = 4
D_FF = D_MODEL
SWIGLU_ALPHA = 1.702
SWIGLU_LIMIT = 7.0
ADA_CHUNKS = 6
EPS = 1e-6
NEG = -1e30
F32 = jnp.float32

kernel_name = 'hybrid_diffattn_mlstm_hyena_moe_denoise_step'


def rms_norm(x, g):
    xf = x.astype(F32)
    y = xf * lax.rsqrt(jnp.mean(xf * xf, axis=-1, keepdims=True) + EPS)
    return (y * g.astype(F32)).astype(x.dtype)


def ada_mod(cond, w, b):
    m = jax.nn.silu(cond) @ w + b
    return jnp.split(m[:, None, :], ADA_CHUNKS, axis=-1)


def modulate(x, g, shift, scale):
    return rms_norm(x, g) * (1.0 + scale) + shift


def dwconv3(x, w, b):
    xp = jnp.pad(x, ((0, 0), (1, 1), (0, 0)))
    return xp[:, :-2] * w[0] + xp[:, 1:-1] * w[1] + xp[:, 2:] * w[2] + b


def grid_positions(L):
    rows = L // GRID_W
    row = jnp.repeat(jnp.arange(rows), GRID_W)
    col = jnp.tile(jnp.arange(GRID_W), rows)
    return row, col


def axial_rope(x):
    L = x.shape[1]
    row, col = grid_positions(L)
    half = HD_A // 2
    nf = half // 2
    inv = ROPE_BASE ** (-jnp.arange(nf, dtype=F32) / nf)
    bshape = (1, L) + (1,) * (x.ndim - 3) + (nf,)

    def rot(xp, pos):
        ang = pos.astype(F32)[:, None] * inv
        cos = jnp.cos(ang).reshape(bshape)
        sin = jnp.sin(ang).reshape(bshape)
        x1 = xp[..., :nf].astype(F32)
        x2 = xp[..., nf:].astype(F32)
        return jnp.concatenate([x1 * cos - x2 * sin, x1 * sin + x2 * cos], axis=-1)

    out = jnp.concatenate([rot(x[..., :half], row), rot(x[..., half:], col)], axis=-1)
    return out.astype(x.dtype)


def diff_attention(q, k, v, lam, sub_g, lam_init):
    b, lq = q.shape[0], q.shape[1]
    nb = lq // Q_BLOCK
    qb = jnp.moveaxis(q.reshape(b, nb, Q_BLOCK, H_A, 2, HD_A), 1, 0)
    scale = HD_A ** -0.5

    def block(qblk):
        s = jnp.einsum('bqhcd,bkhcd->bhcqk', qblk, k).astype(F32) * scale
        p = jax.nn.softmax(s, axis=-1)
        w = p[:, :, 0] - lam * p[:, :, 1]
        return jnp.einsum('bhqk,bkhe->bqhe', w.astype(v.dtype), v)

    o = jnp.moveaxis(lax.map(block, qb), 0, 1).reshape(b, lq, H_A, 2 * HD_A)
    o = rms_norm(o, sub_g) * (1.0 - lam_init)
    return o.reshape(b, lq, W_A)


def mlstm_chunkwise(q, k, v, i_pre, f_pre, c0, n0, m0):
    b, L, h, d = q.shape
    nc = L // CHUNK

    def chunks(a):
        a = a.astype(F32).reshape((b, nc, CHUNK, h) + a.shape[3:])
        return jnp.moveaxis(a, (1, 3), (0, 2))

    qc = chunks(q) * (d ** -0.5)
    kc = chunks(k)
    vc = chunks(v)
    ic = chunks(i_pre)
    lfc = chunks(jax.nn.log_sigmoid(f_pre.astype(F32)))
    mask = jnp.tril(jnp.ones((CHUNK, CHUNK), dtype=bool))

    def step(carry, xs):
        cm, nm, mm = carry
        qt, kt, vt, it, lft = xs
        bcum = jnp.cumsum(lft, axis=-1)
        dmat = jnp.where(mask, bcum[..., :, None] - bcum[..., None, :] + it[..., None, :], NEG)
        inter = bcum + mm[..., None]
        m_t = jnp.maximum(inter, jnp.max(dmat, axis=-1))
        s = jnp.einsum('bhtd,bhsd->bhts', qt, kt) * jnp.exp(dmat - m_t[..., None])
        w_inter = jnp.exp(inter - m_t)
        num = jnp.einsum('bhts,bhse->bhte', s, vt) + w_inter[..., None] * jnp.einsum('bhed,bhtd->bhte', cm, qt)
        nq = jnp.sum(s, axis=-1) + w_inter * jnp.einsum('bhd,bhtd->bht', nm, qt)
        hout = num / jnp.maximum(jnp.abs(nq), jnp.exp(-m_t))[..., None]
        b_last = bcum[..., -1]
        g = b_last[..., None] - bcum + it
        m_new = jnp.maximum(b_last + mm, jnp.max(g, axis=-1))
        w_c = jnp.exp(b_last + mm - m_new)
        w_k = jnp.exp(g - m_new[..., None])
        c_new = w_c[..., None, None] * cm + jnp.einsum('bhse,bhsd->bhed', vt * w_k[..., None], kt)
        n_new = w_c[..., None] * nm + jnp.einsum('bhs,bhsd->bhd', w_k, kt)
        return (c_new, n_new, m_new), hout

    init = (c0.astype(F32), n0.astype(F32), m0.astype(F32))
    final, hs = lax.scan(step, init, (qc, kc, vc, ic, lfc))
    hs = jnp.moveaxis(hs, (0, 2), (1, 3)).reshape(b, L, h, d)
    return hs.astype(q.dtype), final


def mlstm_bidir(q, k, v, gates, c0, n0, m0):
    def flip(a):
        return jnp.flip(a, axis=1)
    h_f, (cf, nf_, mf) = mlstm_chunkwise(q, k, v, gates[:, :, 0, 0], gates[:, :, 0, 1], c0[:, 0], n0[:, 0], m0[:, 0])
    h_b, (cb, nb_, mb) = mlstm_chunkwise(flip(q), flip(k), flip(v), flip(gates[:, :, 1, 0]), flip(gates[:, :, 1, 1]), c0[:, 1], n0[:, 1], m0[:, 1])
    h = h_f + flip(h_b)
    return h, (jnp.stack([cf, cb], axis=1), jnp.stack([nf_, nb_], axis=1), jnp.stack([mf, mb], axis=1))


def ab_mixer(h, w_in, w_out, qn_g, kn_g, lam_p, sub_g, conv_w, conv_b, gate_b, hn_g, lam_init, ctx=None):
    b, L, _ = h.shape
    dq, dk, dv, mqk, mv, mo, mg = jnp.split(h @ w_in, AB_SPLITS, axis=-1)
    q = rms_norm(dq.reshape(b, L, H_A, 2, HD_A), qn_g)
    k = rms_norm(dk.reshape(b, L, H_A, 2, HD_A), kn_g)
    v = dv.reshape(b, L, H_A, 2 * HD_A)
    lp = lam_p.astype(F32)
    lam = jnp.exp(jnp.sum(lp[0] * lp[1])) - jnp.exp(jnp.sum(lp[2] * lp[3])) + lam_init
    mq, mk = jnp.split(jax.nn.silu(dwconv3(mqk, conv_w, conv_b)), 2, axis=-1)
    mq = mq.reshape(b, L, H_B, HD_B)
    mk = mk.reshape(b, L, H_B, HD_B)
    mv = mv.reshape(b, L, H_B, HD_B)
    gates = mg.reshape(b, L, 2, 2, H_B).astype(F32) + gate_b.astype(F32)
    if ctx is None:
        k_all, v_all = k, v
        c0 = jnp.zeros((b, 2, H_B, HD_B, HD_B), F32)
        n0 = jnp.zeros((b, 2, H_B, HD_B), F32)
        m0 = jnp.zeros((b, 2, H_B), F32)
    else:
        ck, cv, c0, n0, m0 = ctx
        q = axial_rope(q)
        k = axial_rope(k)
        k_all = jnp.concatenate([ck.astype(k.dtype), k], axis=1)
        v_all = jnp.concatenate([cv.astype(v.dtype), v], axis=1)
    o_a = diff_attention(q, k_all, v_all, lam, sub_g, lam_init)
    o_b, state = mlstm_bidir(mq, mk, mv, gates, c0, n0, m0)
    o_b = rms_norm(o_b, hn_g.reshape(H_B, HD_B)).reshape(b, L, W_B) * jax.nn.sigmoid(mo)
    out = jnp.concatenate([o_a, o_b.astype(o_a.dtype)], axis=-1) @ w_out
    return out, (k, v) + state


def hyena_filters(L, w1, b1, fr1, w2, b2, fr2, w3):
    t = jnp.linspace(0.0, 1.0, L, dtype=F32)
    wpos = 2.0 * math.pi * jnp.arange(L, dtype=F32) / L
    fb = jnp.linspace(1e-4, HY_BANDS - 1, HY_BANDS, dtype=F32)
    z = wpos[:, None] * fb
    feats = jnp.concatenate([t[:, None], jnp.cos(z), -jnp.sin(z)], axis=-1)
    hdn = jnp.sin(fr1 * (feats @ w1 + b1))
    hdn = jnp.sin(fr2 * (hdn @ w2 + b2))
    filt = (hdn @ w3).astype(F32).reshape(L, HY_ORDER, 2, D_MODEL)
    deltas = jnp.abs(jnp.linspace(math.log(HY_TARGET) / HY_SLOW_PCT, math.log(HY_TARGET) / HY_FAST_PCT, D_MODEL, dtype=F32))
    filt = filt * jnp.exp(-t[:, None] * deltas)[:, None, None, :]
    k_full = jnp.concatenate([filt[:, :, 0], jnp.zeros((1, HY_ORDER, D_MODEL), F32), filt[:0:-1, :, 1]], axis=0)
    return jnp.fft.rfft(k_full, axis=0)


def long_conv(u, kf, bias):
    L = u.shape[1]
    uf = u.astype(F32)
    U = jnp.fft.rfft(uf, n=2 * L, axis=1)
    y = jnp.fft.irfft(U * kf[None], n=2 * L, axis=1)[:, :L]
    return (y + uf * bias.astype(F32)).astype(u.dtype)


def hyena_mixer(h, w_in, w_out, conv_w, conv_b, f_w1, f_b1, f_fr1, f_w2, f_b2, f_fr2, f_w3, bias):
    L = h.shape[1]
    v, x1, x2 = jnp.split(dwconv3(h @ w_in, conv_w, conv_b), HY_ORDER + 1, axis=-1)
    kf = hyena_filters(L, f_w1, f_b1, f_fr1, f_w2, f_b2, f_fr2, f_w3)
    z = x1 * long_conv(v, kf[:, 0], bias[0])
    y = x2 * long_conv(z, kf[:, 1], bias[1])
    return y @ w_out


def moe_ffn(h, router_w, router_b, w1, b1, w2, b2):
    bsz, L, _ = h.shape
    t = h.reshape(bsz * L, D_MODEL)
    logits = (t @ router_w + router_b).astype(F32)
    vals, idx = lax.top_k(logits, TOP_K)
    wts = jax.nn.softmax(vals, axis=-1)
    dense = jnp.sum(jax.nn.one_hot(idx, N_EXPERTS, dtype=F32) * wts[..., None], axis=1)

    def expert(acc, xs):
        w1e, b1e, w2e, b2e, ge = xs
        a = t @ w1e + b1e
        glu = jnp.minimum(a[:, 0::2], SWIGLU_LIMIT)
        lin = jnp.clip(a[:, 1::2], -SWIGLU_LIMIT, SWIGLU_LIMIT)
        hid = glu * jax.nn.sigmoid(SWIGLU_ALPHA * glu) * (lin + 1.0)
        return acc + ge[:, None] * (hid @ w2e + b2e).astype(F32), None

    acc, _ = lax.scan(expert, jnp.zeros((bsz * L, D_MODEL), F32), (w1, b1, w2, b2, dense.T))
    return acc.reshape(bsz, L, D_MODEL).astype(h.dtype)


def setup_inputs(seed: int = 0) -> dict:
    key = jax.random.key(seed)
    keys = iter(jax.random.split(key, 64))

    def nrm(shape, scale):
        return jax.random.normal(next(keys), shape, F32) * scale

    def gain(shape):
        return 1.0 + nrm(shape, 0.02)

    d_in = D_MODEL ** -0.5
    gate_bias_base = jnp.stack([jnp.zeros((H_B,), F32), jnp.linspace(3.0, 6.0, H_B, dtype=F32)])
    return {
        'x_prompt': nrm((BATCH, SEQ, D_MODEL), 1.0),
        'x_sample': nrm((DEC_BATCH, DEC_SEQ, D_MODEL), 1.0),
        'cache_attn_k': nrm((DEC_BATCH, N_EVEN, PAST_LEN, H_A, 2, HD_A), 1.0),
        'cache_attn_v': nrm((DEC_BATCH, N_EVEN, PAST_LEN, H_A, 2 * HD_A), 1.0),
        'state_mlstm_C': nrm((DEC_BATCH, N_EVEN, 2, H_B, HD_B, HD_B), 0.5),
        'state_mlstm_n': nrm((DEC_BATCH, N_EVEN, 2, H_B, HD_B), 0.5),
        'state_mlstm_m': nrm((DEC_BATCH, N_EVEN, 2, H_B), 1.0),
        'c': nrm((DEC_BATCH, D_MODEL), 1.0),
        'c_ctx': nrm((D_MODEL,), 1.0),
        'ada_w': nrm((DEPTH, D_MODEL, ADA_CHUNKS * D_MODEL), 0.5 * d_in),
        'ada_b': nrm((DEPTH, ADA_CHUNKS * D_MODEL), 0.02),
        'norm_mix_g': gain((DEPTH, D_MODEL)),
        'norm_ffn_g': gain((DEPTH, D_MODEL)),
        'ab_w_in': nrm((N_EVEN, D_MODEL, AB_PROJ), d_in),
        'ab_w_out': nrm((N_EVEN, D_MODEL, D_MODEL), d_in),
        'da_qnorm_g': gain((N_EVEN, HD_A)),
        'da_knorm_g': gain((N_EVEN, HD_A)),
        'da_lambda': nrm((N_EVEN, 4, HD_A), 0.1),
        'da_subnorm_g': gain((N_EVEN, 2 * HD_A)),
        'ml_conv_w': nrm((N_EVEN, 3, 2 * W_B), 0.5),
        'ml_conv_b': nrm((N_EVEN, 2 * W_B), 0.02),
        'ml_gate_b': gate_bias_base[None, None] + nrm((N_EVEN, 2, 2, H_B), 0.1),
        'ml_headnorm_g': gain((N_EVEN, W_B)),
        'hy_w_in': nrm((N_ODD, D_MODEL, HY_PROJ), d_in),
        'hy_w_out': nrm((N_ODD, D_MODEL, D_MODEL), d_in),
        'hy_conv_w': nrm((N_ODD, 3, HY_PROJ), 0.5),
        'hy_conv_b': nrm((N_ODD, HY_PROJ), 0.02),
        'hy_f_w1': nrm((N_ODD, HY_EMB, HY_FH), HY_EMB ** -0.5),
        'hy_f_b1': nrm((N_ODD, HY_FH), 0.02),
        'hy_f_freq1': gain((N_ODD, HY_FH)),
        'hy_f_w2': nrm((N_ODD, HY_FH, HY_FH), HY_FH ** -0.5),
        'hy_f_b2': nrm((N_ODD, HY_FH), 0.02),
        'hy_f_freq2': gain((N_ODD, HY_FH)),
        'hy_f_w3': nrm((N_ODD, HY_FH, HY_ORDER * 2 * D_MODEL), 0.05 * HY_FH ** -0.5),
        'hy_bias': nrm((N_ODD, HY_ORDER, D_MODEL), 0.1),
        'router_w': nrm((DEPTH, D_MODEL, N_EXPERTS), d_in),
        'router_b': nrm((DEPTH, N_EXPERTS), 0.01),
        'moe_w1': nrm((DEPTH, N_EXPERTS, D_MODEL, 2 * D_FF), d_in),
        'moe_b1': nrm((DEPTH, N_EXPERTS, 2 * D_FF), 0.02),
        'moe_w2': nrm((DEPTH, N_EXPERTS, D_FF, D_MODEL), D_FF ** -0.5),
        'moe_b2': nrm((DEPTH, N_EXPERTS, D_MODEL), 0.02),
    }


def reference(x_prompt, x_sample, cache_attn_k, cache_attn_v, state_mlstm_C, state_mlstm_n, state_mlstm_m, c, c_ctx, ada_w, ada_b, norm_mix_g, norm_ffn_g, ab_w_in, ab_w_out, da_qnorm_g, da_knorm_g, da_lambda, da_subnorm_g, ml_conv_w, ml_conv_b, ml_gate_b, ml_headnorm_g, hy_w_in, hy_w_out, hy_conv_w, hy_conv_b, hy_f_w1, hy_f_b1, hy_f_freq1, hy_f_w2, hy_f_b2, hy_f_freq2, hy_f_w3, hy_bias, router_w, router_b, moe_w1, moe_b1, moe_w2, moe_b2):
    y_p = x_prompt
    y_s = x_sample
    new_k, new_v, new_c, new_n, new_m = [], [], [], [], []
    for layer in range(DEPTH):
        sh1p, sc1p, g1p, sh2p, sc2p, g2p = ada_mod(c_ctx[None, :], ada_w[layer], ada_b[layer])
        sh1s, sc1s, g1s, sh2s, sc2s, g2s = ada_mod(c, ada_w[layer], ada_b[layer])
        hp = modulate(y_p, norm_mix_g[layer], sh1p, sc1p)
        hs = modulate(y_s, norm_mix_g[layer], sh1s, sc1s)
        if layer % 2 == 0:
            e = layer // 2
            lam_init = 0.8 - 0.6 * math.exp(-0.3 * layer)
            prm = (ab_w_in[e], ab_w_out[e], da_qnorm_g[e], da_knorm_g[e], da_lambda[e], da_subnorm_g[e], ml_conv_w[e], ml_conv_b[e], ml_gate_b[e], ml_headnorm_g[e], lam_init)
            mp, (kc, vc, cc, ncx, mcx) = ab_mixer(hp, *prm)
            ms, _ = ab_mixer(hs, *prm, ctx=(cache_attn_k[:, e], cache_attn_v[:, e], state_mlstm_C[:, e], state_mlstm_n[:, e], state_mlstm_m[:, e]))
            new_k.append(kc)
            new_v.append(vc)
            new_c.append(cc)
            new_n.append(ncx)
            new_m.append(mcx)
        else:
            o = layer // 2
            prm = (hy_w_in[o], hy_w_out[o], hy_conv_w[o], hy_conv_b[o], hy_f_w1[o], hy_f_b1[o], hy_f_freq1[o], hy_f_w2[o], hy_f_b2[o], hy_f_freq2[o], hy_f_w3[o], hy_bias[o])
            mp = hyena_mixer(hp, *prm)
            ms = hyena_mixer(hs, *prm)
        y_p = y_p + g1p * mp
        y_s = y_s + g1s * ms
        moe_prm = (router_w[layer], router_b[layer], moe_w1[layer], moe_b1[layer], moe_w2[layer], moe_b2[layer])
        y_p = y_p + g2p * moe_ffn(modulate(y_p, norm_ffn_g[layer], sh2p, sc2p), *moe_prm)
        y_s = y_s + g2s * moe_ffn(modulate(y_s, norm_ffn_g[layer], sh2s, sc2s), *moe_prm)
    new_attn_k = jnp.stack(new_k, axis=1)
    new_attn_v = jnp.stack(new_v, axis=1)
    new_mlstm_C = jnp.stack(new_c, axis=1)
    new_mlstm_n = jnp.stack(new_n, axis=1)
    new_mlstm_m = jnp.stack(new_m, axis=1)
    return (y_p, y_s, new_attn_k, new_attn_v, new_mlstm_C, new_mlstm_n, new_mlstm_m)
```

```python
import functools
import math

import numpy as np
import jax
import jax.numpy as jnp
from jax import lax
from jax.experimental import pallas as pl
from jax.experimental.pallas import tpu as pltpu

D_MODEL = 1024
BATCH = 16
SEQ = 256
DEPTH = 2
DEC_BATCH = 2
DEC_SEQ = 1024
PAST_LEN = 256
GRID_W = 64
W_A = D_MODEL // 2
HD_A = 64
H_A = W_A // (2 * HD_A)
W_B = D_MODEL - W_A
HD_B = 128
H_B = W_B // HD_B
AB_PROJ = 3 * W_A + 4 * W_B + 4 * H_B
ROPE_BASE = 10000.0
CHUNK = 64
HY_ORDER = 2
HY_PROJ = (HY_ORDER + 1) * D_MODEL
HY_BANDS = 8
HY_FH = 64
HY_TARGET = 1e-2
HY_FAST_PCT = 0.3
HY_SLOW_PCT = 1.5
N_EXPERTS = 32
TOP_K = 4
D_FF = D_MODEL
SWIGLU_ALPHA = 1.702
SWIGLU_LIMIT = 7.0
ADA_CHUNKS = 6
EPS = 1e-6
NEG = -1e30
F32 = jnp.float32
BF16 = jnp.bfloat16

T_P = BATCH * SEQ
T_S = DEC_BATCH * DEC_SEQ
T_ALL = T_P + T_S
ROW_TILE = 256
N_ROW_TILES = T_ALL // ROW_TILE
P_TILES = T_P // ROW_TILE
S_TILES_PER_BATCH = DEC_SEQ // ROW_TILE
COND_ROWS = 8
MOE_TILE = 256
N_ASSIGN = T_ALL * TOP_K
MOE_ROWS = N_ASSIGN + N_EXPERTS * MOE_TILE
MOE_TILES = MOE_ROWS // MOE_TILE
VMEM_LIMIT = 56 * 1024 * 1024
HIGHEST = lax.Precision.HIGHEST


def _cparams(n_axes):
    return pltpu.CompilerParams(dimension_semantics=("arbitrary",) * n_axes,
                                vmem_limit_bytes=VMEM_LIMIT)


def _bdot(a, b):
    return jnp.dot(a.astype(BF16), b.astype(BF16), preferred_element_type=F32)


def _cond_row(i):
    return jnp.where(i < P_TILES, 0, 1 + (i - P_TILES) // S_TILES_PER_BATCH)


def _ada_chunk(ada_ref, row, j):
    return ada_ref[pl.ds(row, 1), j * D_MODEL:(j + 1) * D_MODEL]


def _modulate(x, g, shift, scale):
    ms = jnp.mean(x * x, axis=-1, keepdims=True)
    return (x * lax.rsqrt(ms + EPS) * g) * (1.0 + scale) + shift


def _sigmoid(x):
    return 1.0 / (1.0 + jnp.exp(-x))


def _silu(x):
    return x * _sigmoid(x)


def _log_sigmoid(x):
    return jnp.minimum(x, 0.0) - jnp.log(1.0 + jnp.exp(-jnp.abs(x)))


def _dwconv3(x, w, b):
    n = x.shape[0]
    row = lax.broadcasted_iota(jnp.int32, x.shape, 0)
    prev = jnp.where(row == 0, 0.0, pltpu.roll(x, 1, 0))
    nxt = jnp.where(row == n - 1, 0.0, pltpu.roll(x, n - 1, 0))
    return prev * w[0:1] + x * w[1:2] + nxt * w[2:3] + b


def _ada_kernel(cond_ref, w_ref, b_ref, o_ref):
    c = _silu(cond_ref[...])
    o_ref[...] = _bdot(c, w_ref[...]) + b_ref[...]


def _ada_table(cond, ada_w, ada_b):
    tn = 1536
    return pl.pallas_call(
        _ada_kernel,
        grid=(DEPTH, ADA_CHUNKS * D_MODEL // tn),
        in_specs=[
            pl.BlockSpec((COND_ROWS, D_MODEL), lambda l, j: (0, 0)),
            pl.BlockSpec((None, D_MODEL, tn), lambda l, j: (l, 0, j)),
            pl.BlockSpec((None, 1, tn), lambda l, j: (l, 0, j)),
        ],
        out_specs=pl.BlockSpec((None, COND_ROWS, tn), lambda l, j: (l, 0, j)),
        out_shape=jax.ShapeDtypeStruct((DEPTH, COND_ROWS, ADA_CHUNKS * D_MODEL), F32),
        compiler_params=_cparams(2),
    )(cond, ada_w, ada_b.reshape(DEPTH, 1, ADA_CHUNKS * D_MODEL))


def _proj_kernel(splits, x_ref, ada_ref, g_ref, w_ref, *rest):
    out_refs, wbf_ref = rest[:-1], rest[-1]
    i = pl.program_id(0)

    @pl.when(i == 0)
    def _():
        wbf_ref[...] = w_ref[...].astype(BF16)

    row = _cond_row(i)
    h = _modulate(x_ref[...], g_ref[...], _ada_chunk(ada_ref, row, 0), _ada_chunk(ada_ref, row, 1))
    h = h.astype(BF16)
    lo = 0
    for o_ref, width in zip(out_refs, splits):
        o_ref[...] = jnp.dot(h, wbf_ref[:, lo:lo + width], preferred_element_type=F32)
        lo += width


def _modulated_proj(y, ada_l, g, w, splits):
    n = w.shape[1]
    return pl.pallas_call(
        functools.partial(_proj_kernel, splits),
        grid=(N_ROW_TILES,),
        in_specs=[
            pl.BlockSpec((ROW_TILE, D_MODEL), lambda i: (i, 0)),
            pl.BlockSpec((COND_ROWS, ADA_CHUNKS * D_MODEL), lambda i: (0, 0)),
            pl.BlockSpec((1, D_MODEL), lambda i: (0, 0)),
            pl.BlockSpec((D_MODEL, n), lambda i: (0, 0), pipeline_mode=pl.Buffered(1)),
        ],
        out_specs=[pl.BlockSpec((ROW_TILE, s), lambda i: (i, 0)) for s in splits],
        out_shape=[jax.ShapeDtypeStruct((T_ALL, s), F32) for s in splits],
        scratch_shapes=[pltpu.VMEM((D_MODEL, n), BF16)],
        compiler_params=_cparams(1),
    )(y, ada_l, g.reshape(1, D_MODEL), w)


def _out_proj_kernel(n_in, gate_chunk, *refs):
    x_refs = refs[:n_in]
    y_ref, ada_ref, w_ref, o_ref, wbf_ref = refs[n_in:]
    i = pl.program_id(0)

    @pl.when(i == 0)
    def _():
        wbf_ref[...] = w_ref[...].astype(BF16)

    acc = None
    lo = 0
    for x_ref in x_refs:
        k = x_ref.shape[1]
        part = jnp.dot(x_ref[...].astype(BF16), wbf_ref[lo:lo + k, :], preferred_element_type=F32)
        acc = part if acc is None else acc + part
        lo += k
    gate = _ada_chunk(ada_ref, _cond_row(i), gate_chunk)
    o_ref[...] = y_ref[...] + gate * acc


def _out_proj_residual(xs, y, ada_l, w, gate_chunk):
    return pl.pallas_call(
        functools.partial(_out_proj_kernel, len(xs), gate_chunk),
        grid=(N_ROW_TILES,),
        in_specs=[pl.BlockSpec((ROW_TILE, x.shape[1]), lambda i: (i, 0)) for x in xs] + [
            pl.BlockSpec((ROW_TILE, D_MODEL), lambda i: (i, 0)),
            pl.BlockSpec((COND_ROWS, ADA_CHUNKS * D_MODEL), lambda i: (0, 0)),
            pl.BlockSpec((D_MODEL, D_MODEL), lambda i: (0, 0), pipeline_mode=pl.Buffered(1)),
        ],
        out_specs=pl.BlockSpec((ROW_TILE, D_MODEL), lambda i: (i, 0)),
        out_shape=jax.ShapeDtypeStruct((T_ALL, D_MODEL), F32),
        scratch_shapes=[pltpu.VMEM((D_MODEL, D_MODEL), BF16)],
        compiler_params=_cparams(1),
    )(*xs, y, ada_l, w)


def _subhead_norm(x, g2):
    lane = lax.broadcasted_iota(jnp.int32, x.shape, 1)
    first = lane < HD_A
    xx = x * x
    s0 = jnp.sum(jnp.where(first, xx, 0.0), axis=-1, keepdims=True)
    s1 = jnp.sum(jnp.where(first, 0.0, xx), axis=-1, keepdims=True)
    r = jnp.where(first, lax.rsqrt(s0 / HD_A + EPS), lax.rsqrt(s1 / HD_A + EPS))
    return x * r * g2


def _rope(x, cos, sin):
    quarter = HD_A // 4
    lane = lax.broadcasted_iota(jnp.int32, x.shape, 1)
    lower = (lane % (2 * quarter)) < quarter
    swapped = jnp.where(lower, pltpu.roll(x, 2 * HD_A - quarter, 1), pltpu.roll(x, quarter, 1))
    return x * cos + swapped * sin


def _attn_kernel(lam_init, has_ctx, *refs):
    if has_ctx:
        (q_ref, k_ref, v_ref, ck_ref, cv_ref, cq_ref, sq_ref, ckk_ref, skk_ref,
         qg_ref, kg_ref, lp_ref, sg_ref, o_ref) = refs
    else:
        q_ref, k_ref, v_ref, qg_ref, kg_ref, lp_ref, sg_ref, o_ref, kn_ref = refs
    q = _subhead_norm(q_ref[...], qg_ref[...])
    k = _subhead_norm(k_ref[...], kg_ref[...])
    v = v_ref[...]
    if has_ctx:
        q = _rope(q, cq_ref[...], sq_ref[...])
        k = _rope(k, ckk_ref[...], skk_ref[...])
        k = jnp.concatenate([ck_ref[...], k], axis=0)
        v = jnp.concatenate([cv_ref[...], v], axis=0)
    else:
        kn_ref[...] = k
    lp = lp_ref[...]
    lam = (jnp.exp(jnp.sum(lp[0:1] * lp[1:2], axis=-1, keepdims=True))
           - jnp.exp(jnp.sum(lp[2:3] * lp[3:4], axis=-1, keepdims=True)) + lam_init)
    scale = HD_A ** -0.5
    probs = []
    for c in range(2):
        qc = q[:, c * HD_A:(c + 1) * HD_A].astype(BF16)
        kc = k[:, c * HD_A:(c + 1) * HD_A].astype(BF16)
        s = lax.dot_general(qc, kc, (((1,), (1,)), ((), ())), preferred_element_type=F32) * scale
        e = jnp.exp(s - jnp.max(s, axis=-1, keepdims=True))
        probs.append(e / jnp.sum(e, axis=-1, keepdims=True))
    w = probs[0] - lam * probs[1]
    o = _bdot(w, v)
    ms = jnp.mean(o * o, axis=-1, keepdims=True)
    o_ref[...] = (o * lax.rsqrt(ms + EPS) * sg_ref[...]) * (1.0 - lam_init)


def _attention_prompt(qkv, qg2, kg2, lam_p, sub_g, lam_init):
    nh = H_A
    head = 2 * HD_A
    small = [
        pl.BlockSpec((1, head), lambda b, h: (0, 0)),
        pl.BlockSpec((1, head), lambda b, h: (0, 0)),
        pl.BlockSpec((4, HD_A), lambda b, h: (0, 0)),
        pl.BlockSpec((1, head), lambda b, h: (0, 0)),
    ]
    return pl.pallas_call(
        functools.partial(_attn_kernel, lam_init, False),
        grid=(BATCH, nh),
        in_specs=[
            pl.BlockSpec((SEQ, head), lambda b, h: (b, h)),
            pl.BlockSpec((SEQ, head), lambda b, h: (b, nh + h)),
            pl.BlockSpec((SEQ, head), lambda b, h: (b, 2 * nh + h)),
        ] + small,
        out_specs=[pl.BlockSpec((SEQ, head), lambda b, h: (b, h)),
                   pl.BlockSpec((SEQ, head), lambda b, h: (b, h))],
        out_shape=[jax.ShapeDtypeStruct((T_P, W_A), F32), jax.ShapeDtypeStruct((T_P, W_A), F32)],
        compiler_params=_cparams(2),
    )(qkv, qkv, qkv, qg2, kg2, lam_p, sub_g)


def _attention_sample(qkv, cache_k, cache_v, cos, sin, qg2, kg2, lam_p, sub_g, lam_init):
    nh = H_A
    head = 2 * HD_A
    tq = ROW_TILE
    nq = DEC_SEQ // tq
    q_off = T_P // tq
    k_off = T_P // DEC_SEQ
    small = [
        pl.BlockSpec((1, head), lambda b, h, i: (0, 0)),
        pl.BlockSpec((1, head), lambda b, h, i: (0, 0)),
        pl.BlockSpec((4, HD_A), lambda b, h, i: (0, 0)),
        pl.BlockSpec((1, head), lambda b, h, i: (0, 0)),
    ]
    return pl.pallas_call(
        functools.partial(_attn_kernel, lam_init, True),
        grid=(DEC_BATCH, nh, nq),
        in_specs=[
            pl.BlockSpec((tq, head), lambda b, h, i: (q_off + b * nq + i, h)),
            pl.BlockSpec((DEC_SEQ, head), lambda b, h, i: (k_off + b, nh + h)),
            pl.BlockSpec((DEC_SEQ, head), lambda b, h, i: (k_off + b, 2 * nh + h)),
            pl.BlockSpec((None, PAST_LEN, head), lambda b, h, i: (b, 0, h)),
            pl.BlockSpec((None, PAST_LEN, head), lambda b, h, i: (b, 0, h)),
            pl.BlockSpec((tq, head), lambda b, h, i: (i, 0)),
            pl.BlockSpec((tq, head), lambda b, h, i: (i, 0)),
            pl.BlockSpec((DEC_SEQ, head), lambda b, h, i: (0, 0)),
            pl.BlockSpec((DEC_SEQ, head), lambda b, h, i: (0, 0)),
        ] + small,
        out_specs=pl.BlockSpec((tq, head), lambda b, h, i: (b * nq + i, h)),
        out_shape=jax.ShapeDtypeStruct((T_S, W_A), F32),
        compiler_params=_cparams(3),
    )(qkv, qkv, qkv, cache_k, cache_v, cos, sin, cos, sin, qg2, kg2, lam_p, sub_g)


def _rope_tables():
    half = HD_A // 2
    nf = half // 2
    inv = ROPE_BASE ** (-np.arange(nf, dtype=np.float32) / nf)
    pos = np.arange(DEC_SEQ)
    row = (pos // GRID_W).astype(np.float32)
    col = (pos % GRID_W).astype(np.float32)
    ang_r = (row[:, None] * inv).astype(np.float32)
    ang_c = (col[:, None] * inv).astype(np.float32)
    ang = np.concatenate([ang_r, ang_r, ang_c, ang_c], axis=1)
    sign = np.concatenate([-np.ones(nf), np.ones(nf), -np.ones(nf), np.ones(nf)]).astype(np.float32)
    cos = np.cos(ang.astype(np.float64)).astype(np.float32)
    sin = (np.sin(ang.astype(np.float64)) * sign).astype(np.float32)
    return jnp.asarray(np.tile(cos, (1, 2))), jnp.asarray(np.tile(sin, (1, 2)))


def _mlstm_kernel(seq, has_ctx, *refs):
    if has_ctx:
        (q_ref, k_ref, cwq_ref, cwk_ref, cbq_ref, cbk_ref, v_ref, mo_ref, g_ref, gt_ref,
         gb_ref, gbt_ref, hn_ref, c0_ref, n0_ref, m0_ref, o_ref, qs_ref, ks_ref, hs_ref) = refs
    else:
        (q_ref, k_ref, cwq_ref, cwk_ref, cbq_ref, cbk_ref, v_ref, mo_ref, g_ref, gt_ref,
         gb_ref, gbt_ref, hn_ref, o_ref, c_out_ref, n_out_ref, m_out_ref, qs_ref, ks_ref, hs_ref) = refs
    head = pl.program_id(1)
    nc = seq // CHUNK
    qs_ref[...] = _silu(_dwconv3(q_ref[...], cwq_ref[...], cbq_ref[...])) * (HD_B ** -0.5)
    ks_ref[...] = _silu(_dwconv3(k_ref[...], cwk_ref[...], cbk_ref[...]))

    t_idx = lax.broadcasted_iota(jnp.int32, (CHUNK, CHUNK), 0)
    s_idx = lax.broadcasted_iota(jnp.int32, (CHUNK, CHUNK), 1)
    lane16 = lax.broadcasted_iota(jnp.int32, (CHUNK, 4 * H_B), 1)

    def gate_col(gc, idx):
        return jnp.sum(jnp.where(lane16 == idx, gc, 0.0), axis=-1, keepdims=True)

    for d in range(2):
        i_idx = d * 2 * H_B + head
        f_idx = i_idx + H_B
        mask = (s_idx <= t_idx) if d == 0 else (s_idx >= t_idx)
        mask_t = (t_idx <= s_idx) if d == 0 else (t_idx >= s_idx)

        def body(j, carry, d=d, i_idx=i_idx, f_idx=f_idx, mask=mask, mask_t=mask_t):
            cm, nm, mm = carry
            c = j if d == 0 else nc - 1 - j
            r0 = pl.multiple_of(c * CHUNK, CHUNK)
            qt = qs_ref[pl.ds(r0, CHUNK), :]
            kt = ks_ref[pl.ds(r0, CHUNK), :]
            vt = v_ref[pl.ds(r0, CHUNK), :]
            gc = g_ref[pl.ds(r0, CHUNK), :] + gb_ref[...]
            i_col = gate_col(gc, i_idx)
            lf_col = _log_sigmoid(gate_col(gc, f_idx))
            i_row = gt_ref[c, pl.ds(i_idx, 1), :] + gbt_ref[pl.ds(i_idx, 1), :]
            lf_row = _log_sigmoid(gt_ref[c, pl.ds(f_idx, 1), :] + gbt_ref[pl.ds(f_idx, 1), :])
            b_col = jnp.sum(jnp.where(mask, lf_row, 0.0), axis=-1, keepdims=True)
            b_row = jnp.sum(jnp.where(mask_t, lf_col, 0.0), axis=0, keepdims=True)
            b_last = jnp.sum(lf_row, axis=-1, keepdims=True)
            dmat = jnp.where(mask, b_col - b_row + i_row, NEG)
            inter = b_col + mm
            m_t = jnp.maximum(inter, jnp.max(dmat, axis=-1, keepdims=True))
            qk = lax.dot_general(qt.astype(BF16), kt.astype(BF16), (((1,), (1,)), ((), ())),
                                 preferred_element_type=F32)
            s = qk * jnp.exp(dmat - m_t)
            w_inter = jnp.exp(inter - m_t)
            cq = lax.dot_general(qt.astype(BF16), cm.astype(BF16), (((1,), (1,)), ((), ())),
                                 preferred_element_type=F32)
            num = _bdot(s, vt) + w_inter * cq
            nq = jnp.sum(s, axis=-1, keepdims=True) + w_inter * jnp.sum(qt * nm, axis=-1, keepdims=True)
            hout = num / jnp.maximum(jnp.abs(nq), jnp.exp(-m_t))
            if d == 0:
                hs_ref[pl.ds(r0, CHUNK), :] = hout
            else:
                hs_ref[pl.ds(r0, CHUNK), :] += hout
            g_colv = b_last - b_col + i_col
            g_rowv = b_last - b_row + i_row
            m_new = jnp.maximum(b_last + mm, jnp.max(g_rowv, axis=-1, keepdims=True))
            w_c = jnp.exp(b_last + mm - m_new)
            w_k = jnp.exp(g_colv - m_new)
            vw = (vt * w_k).astype(BF16)
            c_new = w_c * cm + lax.dot_general(vw, kt.astype(BF16), (((0,), (0,)), ((), ())),
                                               preferred_element_type=F32)
            n_new = w_c * nm + jnp.sum(kt * w_k, axis=0, keepdims=True)
            return c_new, n_new, m_new

        if has_ctx:
            m0 = jnp.sum(jnp.where(lax.broadcasted_iota(jnp.int32, (1, H_B), 1) == head,
                                   m0_ref[pl.ds(d, 1), :], 0.0), axis=-1, keepdims=True)
            init = (c0_ref[d], n0_ref[d], m0)
        else:
            init = (jnp.zeros((HD_B, HD_B), F32), jnp.zeros((1, HD_B), F32), jnp.zeros((1, 1), F32))
        cm, nm, mm = lax.fori_loop(0, nc, body, init)
        if not has_ctx:
            c_out_ref[d] = cm
            n_out_ref[d] = nm
            m_out_ref[d] = jnp.broadcast_to(mm, (1, HD_B))

    h = hs_ref[...]
    ms = jnp.mean(h * h, axis=-1, keepdims=True)
    o_ref[...] = (h * lax.rsqrt(ms + EPS) * hn_ref[...]) * _sigmoid(mo_ref[...])


def _mlstm(mqk, mv, mo, mg, gt, conv_w, conv_b, gate_b, hn_g, *, seq, nbatch, row_off, ctx=None):
    nh = H_B
    has_ctx = ctx is not None
    blk = lambda col: pl.BlockSpec((seq, HD_B), lambda b, h, col=col: (row_off + b, col * nh + h))
    in_specs = [
        blk(0), blk(1),
        pl.BlockSpec((3, HD_B), lambda b, h: (0, h)),
        pl.BlockSpec((3, HD_B), lambda b, h: (0, nh + h)),
        pl.BlockSpec((1, HD_B), lambda b, h: (0, h)),
        pl.BlockSpec((1, HD_B), lambda b, h: (0, nh + h)),
        blk(0), blk(0),
        pl.BlockSpec((seq, 4 * nh), lambda b, h: (row_off + b, 0)),
        pl.BlockSpec((None, seq // CHUNK, 4 * nh, CHUNK), lambda b, h: (b, 0, 0, 0)),
        pl.BlockSpec((1, 4 * nh), lambda b, h: (0, 0)),
        pl.BlockSpec((4 * nh, 1), lambda b, h: (0, 0)),
        pl.BlockSpec((1, HD_B), lambda b, h: (0, h)),
    ]
    args = [mqk, mqk, conv_w, conv_w, conv_b, conv_b, mv, mo, mg, gt,
            gate_b.reshape(1, 4 * nh), gate_b.reshape(4 * nh, 1), hn_g.reshape(1, W_B)]
    o_spec = pl.BlockSpec((seq, HD_B), lambda b, h: (b, h))
    o_shape = jax.ShapeDtypeStruct((nbatch * seq, W_B), F32)
    if has_ctx:
        c0, n0, m0 = ctx
        in_specs += [
            pl.BlockSpec((None, 2, None, HD_B, HD_B), lambda b, h: (b, 0, h, 0, 0)),
            pl.BlockSpec((None, 2, None, 1, HD_B), lambda b, h: (b, 0, h, 0, 0)),
            pl.BlockSpec((None, 2, nh), lambda b, h: (b, 0, 0)),
        ]
        args += [c0, n0.reshape(nbatch, 2, nh, 1, HD_B), m0]
        out_specs, out_shape = o_spec, o_shape
    else:
        out_specs = [
            o_spec,
            pl.BlockSpec((None, 2, None, HD_B, HD_B), lambda b, h: (b, 0, h, 0, 0)),
            pl.BlockSpec((None, 2, None, 1, HD_B), lambda b, h: (b, 0, h, 0, 0)),
            pl.BlockSpec((None, 2, None, 1, HD_B), lambda b, h: (b, 0, h, 0, 0)),
        ]
        out_shape = [
            o_shape,
            jax.ShapeDtypeStruct((nbatch, 2, nh, HD_B, HD_B), F32),
            jax.ShapeDtypeStruct((nbatch, 2, nh, 1, HD_B), F32),
            jax.ShapeDtypeStruct((nbatch, 2, nh, 1, HD_B), F32),
        ]
    return pl.pallas_call(
        functools.partial(_mlstm_kernel, seq, has_ctx),
        grid=(nbatch, nh),
        in_specs=in_specs,
        out_specs=out_specs,
        out_shape=out_shape,
        scratch_shapes=[pltpu.VMEM((seq, HD_B), F32)] * 3,
        compiler_params=_cparams(2),
    )(*args)


def _dft_mats(L):
    f = np.arange(L)[:, None]
    j = np.arange(L)[None, :]
    ang = 2.0 * np.pi * ((f * j) % (2 * L)) / (2 * L)
    cm = np.cos(ang)
    sm = np.sin(ang)
    alt = (1.0 - 2.0 * (np.arange(L) % 2))
    fwd_b = -sm
    fwd_b[0, :] = alt
    fwd = np.concatenate([cm, fwd_b], axis=0)
    wgt = np.where(np.arange(L) == 0, 1.0, 2.0)[None, :]
    inv_a = cm.T * wgt
    inv_b = -2.0 * sm.T
    inv_b[:, 0] = alt
    inv = np.concatenate([inv_a, inv_b], axis=1) / (2 * L)
    return jnp.asarray(fwd.astype(np.float32)), jnp.asarray(inv.astype(np.float32))


def _hyena_feats(L):
    t = np.linspace(0.0, 1.0, L, dtype=np.float32)
    wpos = (2.0 * math.pi * np.arange(L, dtype=np.float32) / L).astype(np.float32)
    fb = np.linspace(1e-4, HY_BANDS - 1, HY_BANDS, dtype=np.float32)
    z = (wpos[:, None] * fb).astype(np.float32)
    feats = np.concatenate([t[:, None], np.cos(z), -np.sin(z)], axis=-1).astype(np.float32)
    deltas = np.abs(np.linspace(math.log(HY_TARGET) / HY_SLOW_PCT, math.log(HY_TARGET) / HY_FAST_PCT,
                                D_MODEL, dtype=np.float32))
    decay = np.exp(-t[:, None] * deltas).astype(np.float32)
    return jnp.asarray(feats), jnp.asarray(decay)


def _filter_kernel(L, feats_ref, w1_ref, b1_ref, fr1_ref, w2_ref, b2_ref, fr2_ref, w3f_ref, w3b_ref,
                   decay_ref, fwd_ref, o_ref, hdn_ref):
    @pl.when((pl.program_id(0) == 0) & (pl.program_id(1) == 0))
    def _():
        h1 = jnp.sin(fr1_ref[...] * (jnp.dot(feats_ref[...], w1_ref[...], precision=HIGHEST,
                                             preferred_element_type=F32) + b1_ref[...]))
        hdn_ref[...] = jnp.sin(fr2_ref[...] * (jnp.dot(h1, w2_ref[...], precision=HIGHEST,
                                                       preferred_element_type=F32) + b2_ref[...]))

    hdn = hdn_ref[...]
    decay = decay_ref[...]
    f_fwd = jnp.dot(hdn, w3f_ref[...], precision=HIGHEST, preferred_element_type=F32) * decay
    f_bwd = jnp.dot(hdn, w3b_ref[...], precision=HIGHEST, preferred_element_type=F32) * decay
    row = lax.broadcasted_iota(jnp.int32, f_bwd.shape, 0)
    f_bwd = jnp.where(row == 0, 0.0, f_bwd)
    fwd = fwd_ref[...]
    p = _bdot(fwd, f_fwd)
    q = _bdot(fwd, f_bwd)
    first = row == 0
    o_ref[0:L, :] = p[0:L] + q[0:L]
    o_ref[L:2 * L, :] = p[L:2 * L] + jnp.where(first, q[L:2 * L], -q[L:2 * L])


def _hyena_filter_spectrum(L, fwd_bf, w1, b1, fr1, w2, b2, fr2, w3):
    feats, decay = _hyena_feats(L)
    td = 512
    nd = D_MODEL // td
    emb = feats.shape[1]
    vec = lambda a: a.reshape(1, HY_FH)
    full = lambda shape: pl.BlockSpec(shape, lambda o, j: (0, 0))
    return pl.pallas_call(
        functools.partial(_filter_kernel, L),
        grid=(HY_ORDER, nd),
        in_specs=[
            full((L, emb)), full((emb, HY_FH)), full((1, HY_FH)), full((1, HY_FH)),
            full((HY_FH, HY_FH)), full((1, HY_FH)), full((1, HY_FH)),
            pl.BlockSpec((HY_FH, td), lambda o, j: (0, o * 2 * nd + j)),
            pl.BlockSpec((HY_FH, td), lambda o, j: (0, o * 2 * nd + nd + j)),
            pl.BlockSpec((L, td), lambda o, j: (0, j)),
            full((2 * L, L)),
        ],
        out_specs=pl.BlockSpec((2 * L, td), lambda o, j: (0, o * nd + j)),
        out_shape=jax.ShapeDtypeStruct((2 * L, HY_ORDER * D_MODEL), F32),
        scratch_shapes=[pltpu.VMEM((L, HY_FH), F32)],
        compiler_params=_cparams(2),
    )(feats, w1, vec(b1), vec(fr1), w2, vec(b2), vec(fr2), w3, w3, decay, fwd_bf)


def _spectral_conv(u, fwd, inv, kspec, L):
    uf = jnp.dot(fwd, u.astype(BF16), preferred_element_type=F32)
    ua, ub = uf[0:L], uf[L:2 * L]
    ka, kb = kspec[0:L], kspec[L:2 * L]
    first = lax.broadcasted_iota(jnp.int32, ua.shape, 0) == 0
    ya = ua * ka - jnp.where(first, 0.0, ub * kb)
    yb = jnp.where(first, ub * kb, ua * kb + ub * ka)
    y = jnp.concatenate([ya, yb], axis=0).astype(BF16)
    return jnp.dot(inv, y, preferred_element_type=F32)


def _hyena_kernel(L, zv_ref, z1_ref, z2_ref, cwv_ref, cw1_ref, cw2_ref, cbv_ref, cb1_ref, cb2_ref,
                  fwd_ref, inv_ref, k0_ref, k1_ref, bias0_ref, bias1_ref, o_ref):
    fwd = fwd_ref[...]
    inv = inv_ref[...]
    v = _dwconv3(zv_ref[...], cwv_ref[...], cbv_ref[...])
    x1 = _dwconv3(z1_ref[...], cw1_ref[...], cb1_ref[...])
    x2 = _dwconv3(z2_ref[...], cw2_ref[...], cb2_ref[...])
    z = x1 * (_spectral_conv(v, fwd, inv, k0_ref[...], L) + v * bias0_ref[...])
    o_ref[...] = x2 * (_spectral_conv(z, fwd, inv, k1_ref[...], L) + z * bias1_ref[...])


def _hyena_core(zproj, conv_w, conv_b, fwd_bf, inv_bf, kspec, bias, *, seq, nbatch, row_off, td):
    nd = D_MODEL // td
    zblk = lambda part: pl.BlockSpec((seq, td), lambda b, j, part=part: (row_off + b, part * nd + j))
    cwblk = lambda part: pl.BlockSpec((3, td), lambda b, j, part=part: (0, part * nd + j))
    cbblk = lambda part: pl.BlockSpec((1, td), lambda b, j, part=part: (0, part * nd + j))
    return pl.pallas_call(
        functools.partial(_hyena_kernel, seq),
        grid=(nbatch, nd),
        in_specs=[
            zblk(0), zblk(1), zblk(2), cwblk(0), cwblk(1), cwblk(2), cbblk(0), cbblk(1), cbblk(2),
            pl.BlockSpec((2 * seq, seq), lambda b, j: (0, 0), pipeline_mode=pl.Buffered(1)),
            pl.BlockSpec((seq, 2 * seq), lambda b, j: (0, 0), pipeline_mode=pl.Buffered(1)),
            pl.BlockSpec((2 * seq, td), lambda b, j: (0, j)),
            pl.BlockSpec((2 * seq, td), lambda b, j: (0, nd + j)),
            pl.BlockSpec((None, 1, td), lambda b, j: (0, 0, j)),
            pl.BlockSpec((None, 1, td), lambda b, j: (1, 0, j)),
        ],
        out_specs=pl.BlockSpec((seq, td), lambda b, j: (b, j)),
        out_shape=jax.ShapeDtypeStruct((nbatch * seq, D_MODEL), F32),
        compiler_params=_cparams(2),
    )(zproj, zproj, zproj, conv_w, conv_w, conv_w, conv_b, conv_b, conv_b,
      fwd_bf, inv_bf, kspec, kspec, bias.reshape(HY_ORDER, 1, D_MODEL), bias.reshape(HY_ORDER, 1, D_MODEL))


def _router_kernel(x_ref, ada_ref, g_ref, rw_ref, rb_ref, h_ref, idx_ref, wt_ref):
    i = pl.program_id(0)
    row = _cond_row(i)
    h = _modulate(x_ref[...], g_ref[...], _ada_chunk(ada_ref, row, 3), _ada_chunk(ada_ref, row, 4))
    h_ref[...] = h.astype(BF16)
    logits = jnp.dot(h, rw_ref[...], precision=HIGHEST, preferred_element_type=F32) + rb_ref[...]
    lane = lax.broadcasted_iota(jnp.int32, logits.shape, 1)
    slot = lax.broadcasted_iota(jnp.int32, (logits.shape[0], TOP_K), 1)
    vals = jnp.zeros((logits.shape[0], TOP_K), F32)
    idxs = jnp.zeros((logits.shape[0], TOP_K), jnp.int32)
    cur = logits
    for k in range(TOP_K):
        m = jnp.max(cur, axis=-1, keepdims=True)
        a = jnp.min(jnp.where(cur == m, lane, N_EXPERTS), axis=-1, keepdims=True)
        vals = jnp.where(slot == k, m, vals)
        idxs = jnp.where(slot == k, a, idxs)
        cur = jnp.where(lane == a, -jnp.inf, cur)
    e = jnp.exp(vals - vals[:, 0:1])
    wt_ref[...] = e / jnp.sum(e, axis=-1, keepdims=True)
    idx_ref[...] = idxs


def _router(y, ada_l, g, router_w, router_b):
    return pl.pallas_call(
        _router_kernel,
        grid=(N_ROW_TILES,),
        in_specs=[
            pl.BlockSpec((ROW_TILE, D_MODEL), lambda i: (i, 0)),
            pl.BlockSpec((COND_ROWS, ADA_CHUNKS * D_MODEL), lambda i: (0, 0)),
            pl.BlockSpec((1, D_MODEL), lambda i: (0, 0)),
            pl.BlockSpec((D_MODEL, N_EXPERTS), lambda i: (0, 0)),
            pl.BlockSpec((1, N_EXPERTS), lambda i: (0, 0)),
        ],
        out_specs=[
            pl.BlockSpec((ROW_TILE, D_MODEL), lambda i: (i, 0)),
            pl.BlockSpec((ROW_TILE, TOP_K), lambda i: (i, 0)),
            pl.BlockSpec((ROW_TILE, TOP_K), lambda i: (i, 0)),
        ],
        out_shape=[
            jax.ShapeDtypeStruct((T_ALL, D_MODEL), BF16),
            jax.ShapeDtypeStruct((T_ALL, TOP_K), jnp.int32),
            jax.ShapeDtypeStruct((T_ALL, TOP_K), F32),
        ],
        compiler_params=_cparams(1),
    )(y, ada_l, g.reshape(1, D_MODEL), router_w, router_b.reshape(1, N_EXPERTS))


def _deinterleave_matrix():
    s = np.zeros((256, 256), np.float32)
    j = np.arange(128)
    s[2 * j, j] = 1.0
    s[2 * j + 1, 128 + j] = 1.0
    return jnp.asarray(s)


def _expert_kernel(te_ref, tf_ref, nu_ref, x_ref, w1_ref, b1_ref, w2_ref, b2_ref, wt_ref, s_ref,
                   o_ref, w1p_ref, w2p_ref):
    i = pl.program_id(0)
    half = 128

    @pl.when(tf_ref[i] == 1)
    def _():
        s = s_ref[...].astype(BF16)
        for c in range(2 * D_FF // 256):
            blk = jnp.dot(w1_ref[:, c * 256:(c + 1) * 256].astype(BF16), s, preferred_element_type=F32)
            w1p_ref[:, c * half:(c + 1) * half] = blk[:, :half].astype(BF16)
            w1p_ref[:, D_FF + c * half:D_FF + (c + 1) * half] = blk[:, half:].astype(BF16)
        w2p_ref[...] = w2_ref[...].astype(BF16)

    @pl.when(i < nu_ref[0])
    def _():
        a = jnp.dot(x_ref[...], w1p_ref[...], preferred_element_type=F32) + b1_ref[...]
        glu = jnp.minimum(a[:, :D_FF], SWIGLU_LIMIT)
        lin = jnp.clip(a[:, D_FF:], -SWIGLU_LIMIT, SWIGLU_LIMIT)
        hid = glu * _sigmoid(SWIGLU_ALPHA * glu) * (lin + 1.0)
        out = jnp.dot(hid.astype(BF16), w2p_ref[...], preferred_element_type=F32) + b2_ref[...]
        o_ref[...] = out * wt_ref[...]

    @pl.when(i >= nu_ref[0])
    def _():
        o_ref[...] = jnp.zeros_like(o_ref)


def _experts(x_sorted, w_sorted, tile_expert, tile_first, n_used, w1, b1p, w2, b2):
    grid_spec = pltpu.PrefetchScalarGridSpec(
        num_scalar_prefetch=3,
        grid=(MOE_TILES,),
        in_specs=[
            pl.BlockSpec((MOE_TILE, D_MODEL), lambda i, te, tf, nu: (i, 0)),
            pl.BlockSpec((None, D_MODEL, 2 * D_FF), lambda i, te, tf, nu: (te[i], 0, 0)),
            pl.BlockSpec((None, 1, 2 * D_FF), lambda i, te, tf, nu: (te[i], 0, 0)),
            pl.BlockSpec((None, D_FF, D_MODEL), lambda i, te, tf, nu: (te[i], 0, 0)),
            pl.BlockSpec((None, 1, D_MODEL), lambda i, te, tf, nu: (te[i], 0, 0)),
            pl.BlockSpec((MOE_TILE, 1), lambda i, te, tf, nu: (i, 0)),
            pl.BlockSpec((256, 256), lambda i, te, tf, nu: (0, 0)),
        ],
        out_specs=pl.BlockSpec((MOE_TILE, D_MODEL), lambda i, te, tf, nu: (i, 0)),
        scratch_shapes=[pltpu.VMEM((D_MODEL, 2 * D_FF), BF16), pltpu.VMEM((D_FF, D_MODEL), BF16)],
    )
    return pl.pallas_call(
        _expert_kernel,
        grid_spec=grid_spec,
        out_shape=jax.ShapeDtypeStruct((MOE_ROWS, D_MODEL), F32),
        compiler_params=_cparams(1),
    )(tile_expert, tile_first, n_used, x_sorted, w1, b1p, w2, b2, w_sorted, _deinterleave_matrix())


def _combine_kernel(y_ref, g_ref, ada_ref, o_ref):
    gate = _ada_chunk(ada_ref, _cond_row(pl.program_id(0)), 5)
    acc = g_ref[:, 0:D_MODEL]
    for k in range(1, TOP_K):
        acc = acc + g_ref[:, k * D_MODEL:(k + 1) * D_MODEL]
    o_ref[...] = y_ref[...] + gate * acc


def _combine(y, gathered, ada_l):
    return pl.pallas_call(
        _combine_kernel,
        grid=(N_ROW_TILES,),
        in_specs=[
            pl.BlockSpec((ROW_TILE, D_MODEL), lambda i: (i, 0)),
            pl.BlockSpec((ROW_TILE, TOP_K * D_MODEL), lambda i: (i, 0)),
            pl.BlockSpec((COND_ROWS, ADA_CHUNKS * D_MODEL), lambda i: (0, 0)),
        ],
        out_specs=pl.BlockSpec((ROW_TILE, D_MODEL), lambda i: (i, 0)),
        out_shape=jax.ShapeDtypeStruct((T_ALL, D_MODEL), F32),
        compiler_params=_cparams(1),
    )(y, gathered, ada_l)


def _routing_plan(idx):
    eid = idx.reshape(-1)
    order = jnp.argsort(eid, stable=True).astype(jnp.int32)
    e_sorted = eid[order]
    counts = jnp.sum(eid[:, None] == jnp.arange(N_EXPERTS, dtype=jnp.int32)[None, :], axis=0).astype(jnp.int32)
    padded = ((counts + MOE_TILE - 1) // MOE_TILE) * MOE_TILE
    ends = jnp.cumsum(padded)
    starts = ends - padded
    cstarts = jnp.cumsum(counts) - counts
    dest = starts[e_sorted] + (jnp.arange(N_ASSIGN, dtype=jnp.int32) - cstarts[e_sorted])
    src_token = jnp.zeros((MOE_ROWS,), jnp.int32).at[dest].set(order // TOP_K)
    pos = jnp.zeros((N_ASSIGN,), jnp.int32).at[order].set(dest)
    tile_start = jnp.arange(MOE_TILES, dtype=jnp.int32) * MOE_TILE
    tile_expert = jnp.minimum(jnp.searchsorted(ends, tile_start, side="right"), N_EXPERTS - 1).astype(jnp.int32)
    prev = jnp.concatenate([jnp.full((1,), -1, jnp.int32), tile_expert[:-1]])
    tile_first = (tile_expert != prev).astype(jnp.int32)
    n_used = (ends[-1:] // MOE_TILE).astype(jnp.int32)
    return order, dest, src_token, pos, tile_expert, tile_first, n_used


def _moe(y, ada_l, g, router_w, router_b, w1, b1, w2, b2):
    h, idx, wts = _router(y, ada_l, g, router_w, router_b)
    order, dest, src_token, pos, tile_expert, tile_first, n_used = _routing_plan(idx)
    w_sorted = jnp.zeros((MOE_ROWS,), F32).at[dest].set(wts.reshape(-1)[order]).reshape(MOE_ROWS, 1)
    x_sorted = jnp.take(h, src_token, axis=0)
    b1p = jnp.concatenate([b1[:, 0::2], b1[:, 1::2]], axis=-1).reshape(N_EXPERTS, 1, 2 * D_FF)
    out_sorted = _experts(x_sorted, w_sorted, tile_expert, tile_first, n_used, w1, b1p, w2,
                          b2.reshape(N_EXPERTS, 1, D_MODEL))
    gathered = jnp.take(out_sorted, pos, axis=0).reshape(T_ALL, TOP_K * D_MODEL)
    return _combine(y, gathered, ada_l)


def _chunked_gates_t(mg_stream, nbatch, seq):
    return mg_stream.reshape(nbatch, seq // CHUNK, CHUNK, 4 * H_B).transpose(0, 1, 3, 2)


def kernel(x_prompt, x_sample, cache_attn_k, cache_attn_v, state_mlstm_C, state_mlstm_n, state_mlstm_m, c, c_ctx, ada_w, ada_b, norm_mix_g, norm_ffn_g, ab_w_in, ab_w_out, da_qnorm_g, da_knorm_g, da_lambda, da_subnorm_g, ml_conv_w, ml_conv_b, ml_gate_b, ml_headnorm_g, hy_w_in, hy_w_out, hy_conv_w, hy_conv_b, hy_f_w1, hy_f_b1, hy_f_freq1, hy_f_w2, hy_f_b2, hy_f_freq2, hy_f_w3, hy_bias, router_w, router_b, moe_w1, moe_b1, moe_w2, moe_b2):
    y = jnp.concatenate([x_prompt.reshape(T_P, D_MODEL), x_sample.reshape(T_S, D_MODEL)], axis=0)
    cond = jnp.concatenate([c_ctx[None, :], c, jnp.zeros((COND_ROWS - 1 - DEC_BATCH, D_MODEL), F32)], axis=0)
    ada = _ada_table(cond, ada_w, ada_b)
    new_k, new_v, new_c, new_n, new_m = [], [], [], [], []
    for layer in range(DEPTH):
        ada_l = ada[layer]
        if layer % 2 == 0:
            e = layer // 2
            lam_init = 0.8 - 0.6 * math.exp(-0.3 * layer)
            qkv, mqk, mv, mo, mg = _modulated_proj(
                y, ada_l, norm_mix_g[layer], ab_w_in[e], (3 * W_A, 2 * W_B, W_B, W_B, 4 * H_B))
            qg2 = jnp.tile(da_qnorm_g[e], 2).reshape(1, 2 * HD_A)
            kg2 = jnp.tile(da_knorm_g[e], 2).reshape(1, 2 * HD_A)
            sub_g = da_subnorm_g[e].reshape(1, 2 * HD_A)
            oa_p, k_norm = _attention_prompt(qkv, qg2, kg2, da_lambda[e], sub_g, lam_init)
            cos, sin = _rope_tables()
            oa_s = _attention_sample(
                qkv, cache_attn_k[:, e].reshape(DEC_BATCH, PAST_LEN, W_A),
                cache_attn_v[:, e].reshape(DEC_BATCH, PAST_LEN, W_A), cos, sin,
                qg2, kg2, da_lambda[e], sub_g, lam_init)
            gt_p = _chunked_gates_t(mg[:T_P], BATCH, SEQ)
            gt_s = _chunked_gates_t(mg[T_P:], DEC_BATCH, DEC_SEQ)
            ob_p, c_new, n_new, m_new = _mlstm(
                mqk, mv, mo, mg, gt_p, ml_conv_w[e], ml_conv_b[e].reshape(1, 2 * W_B), ml_gate_b[e],
                ml_headnorm_g[e], seq=SEQ, nbatch=BATCH, row_off=0)
            ob_s = _mlstm(
                mqk, mv, mo, mg, gt_s, ml_conv_w[e], ml_conv_b[e].reshape(1, 2 * W_B), ml_gate_b[e],
                ml_headnorm_g[e], seq=DEC_SEQ, nbatch=DEC_BATCH, row_off=T_P // DEC_SEQ,
                ctx=(state_mlstm_C[:, e], state_mlstm_n[:, e], state_mlstm_m[:, e]))
            o_a = jnp.concatenate([oa_p, oa_s], axis=0)
            o_b = jnp.concatenate([ob_p, ob_s], axis=0)
            y = _out_proj_residual([o_a, o_b], y, ada_l, ab_w_out[e], 2)
            new_k.append(k_norm.reshape(BATCH, SEQ, H_A, 2, HD_A))
            new_v.append(qkv[:T_P, 2 * W_A:].reshape(BATCH, SEQ, H_A, 2 * HD_A))
            new_c.append(c_new)
            new_n.append(n_new.reshape(BATCH, 2, H_B, HD_B))
            new_m.append(m_new[..., 0, 0])
        else:
            o = layer // 2
            (zproj,) = _modulated_proj(y, ada_l, norm_mix_g[layer], hy_w_in[o], (HY_PROJ,))
            cores = []
            for seq, nbatch, row_off, td in ((SEQ, BATCH, 0, 512), (DEC_SEQ, DEC_BATCH, T_P // DEC_SEQ, 256)):
                fwd, inv = _dft_mats(seq)
                fwd_bf, inv_bf = fwd.astype(BF16), inv.astype(BF16)
                kspec = _hyena_filter_spectrum(seq, fwd_bf, hy_f_w1[o], hy_f_b1[o], hy_f_freq1[o], hy_f_w2[o],
                                               hy_f_b2[o], hy_f_freq2[o], hy_f_w3[o])
                cores.append(_hyena_core(zproj, hy_conv_w[o], hy_conv_b[o].reshape(1, HY_PROJ), fwd_bf, inv_bf,
                                         kspec, hy_bias[o], seq=seq, nbatch=nbatch, row_off=row_off, td=td))
            y = _out_proj_residual([jnp.concatenate(cores, axis=0)], y, ada_l, hy_w_out[o], 2)
        y = _moe(y, ada_l, norm_ffn_g[layer], router_w[layer], router_b[layer],
                 moe_w1[layer], moe_b1[layer], moe_w2[layer], moe_b2[layer])
    y_p = y[:T_P].reshape(BATCH, SEQ, D_MODEL)
    y_s = y[T_P:].reshape(DEC_BATCH, DEC_SEQ, D_MODEL)
    return (y_p, y_s, jnp.stack(new_k, axis=1), jnp.stack(new_v, axis=1), jnp.stack(new_c, axis=1),
            jnp.stack(new_n, axis=1), jnp.stack(new_m, axis=1))
```

```python
import functools
import math

import numpy as np
import jax
import jax.numpy as jnp
from jax import lax
from jax.experimental import pallas as pl
from jax.experimental.pallas import tpu as pltpu

D_MODEL = 1024
BATCH = 16
SEQ = 256
DEPTH = 2
DEC_BATCH = 2
DEC_SEQ = 1024
PAST_LEN = 256
GRID_W = 64
W_A = D_MODEL // 2
HD_A = 64
H_A = W_A // (2 * HD_A)
W_B = D_MODEL - W_A
HD_B = 128
H_B = W_B // HD_B
AB_PROJ = 3 * W_A + 4 * W_B + 4 * H_B
ROPE_BASE = 10000.0
CHUNK = 64
HY_ORDER = 2
HY_PROJ = (HY_ORDER + 1) * D_MODEL
HY_BANDS = 8
HY_FH = 64
HY_TARGET = 1e-2
HY_FAST_PCT = 0.3
HY_SLOW_PCT = 1.5
N_EXPERTS = 32
TOP_K = 4
D_FF = D_MODEL
SWIGLU_ALPHA = 1.702
SWIGLU_LIMIT = 7.0
ADA_CHUNKS = 6
EPS = 1e-6
NEG = -1e30
F32 = jnp.float32
BF16 = jnp.bfloat16

T_P = BATCH * SEQ
T_S = DEC_BATCH * DEC_SEQ
T_ALL = T_P + T_S
ROW_TILE = 256
N_ROW_TILES = T_ALL // ROW_TILE
P_TILES = T_P // ROW_TILE
S_TILES_PER_BATCH = DEC_SEQ // ROW_TILE
COND_ROWS = 8
MOE_TILE = 256
N_ASSIGN = T_ALL * TOP_K
MOE_ROWS = N_ASSIGN + N_EXPERTS * MOE_TILE
MOE_TILES = MOE_ROWS // MOE_TILE
SPARE_ROW = T_ALL
ACC_ROWS = T_ALL + 8
SCATTER_GROUP = 8
W1_DMA_CHUNKS = 4
W2_DMA_CHUNKS = 2
VMEM_LIMIT = 56 * 1024 * 1024
HIGHEST = lax.Precision.HIGHEST


def _cparams(n_axes):
    return pltpu.CompilerParams(dimension_semantics=("arbitrary",) * n_axes,
                                vmem_limit_bytes=VMEM_LIMIT)


def _bdot(a, b):
    return jnp.dot(a.astype(BF16), b.astype(BF16), preferred_element_type=F32)


def _cond_row(i):
    return jnp.where(i < P_TILES, 0, 1 + (i - P_TILES) // S_TILES_PER_BATCH)


def _ada_chunk(ada_ref, row, j):
    return ada_ref[pl.ds(row, 1), j * D_MODEL:(j + 1) * D_MODEL]


def _modulate(x, g, shift, scale):
    ms = jnp.mean(x * x, axis=-1, keepdims=True)
    return (x * lax.rsqrt(ms + EPS) * g) * (1.0 + scale) + shift


def _sigmoid(x):
    return 1.0 / (1.0 + jnp.exp(-x))


def _silu(x):
    return x * _sigmoid(x)


def _log_sigmoid(x):
    return jnp.minimum(x, 0.0) - jnp.log(1.0 + jnp.exp(-jnp.abs(x)))


def _dwconv3(x, w, b):
    n = x.shape[0]
    row = lax.broadcasted_iota(jnp.int32, x.shape, 0)
    prev = jnp.where(row == 0, 0.0, pltpu.roll(x, 1, 0))
    nxt = jnp.where(row == n - 1, 0.0, pltpu.roll(x, n - 1, 0))
    return prev * w[0:1] + x * w[1:2] + nxt * w[2:3] + b


def _ada_kernel(cond_ref, w_ref, b_ref, o_ref):
    c = _silu(cond_ref[...])
    o_ref[...] = _bdot(c, w_ref[...]) + b_ref[...]


def _ada_table(cond, ada_w, ada_b):
    tn = 1536
    return pl.pallas_call(
        _ada_kernel,
        grid=(DEPTH, ADA_CHUNKS * D_MODEL // tn),
        in_specs=[
            pl.BlockSpec((COND_ROWS, D_MODEL), lambda l, j: (0, 0)),
            pl.BlockSpec((None, D_MODEL, tn), lambda l, j: (l, 0, j)),
            pl.BlockSpec((None, 1, tn), lambda l, j: (l, 0, j)),
        ],
        out_specs=pl.BlockSpec((None, COND_ROWS, tn), lambda l, j: (l, 0, j)),
        out_shape=jax.ShapeDtypeStruct((DEPTH, COND_ROWS, ADA_CHUNKS * D_MODEL), F32),
        compiler_params=_cparams(2),
    )(cond, ada_w, ada_b.reshape(DEPTH, 1, ADA_CHUNKS * D_MODEL))


def _proj_kernel(splits, x_ref, ada_ref, g_ref, w_ref, *rest):
    out_refs, wbf_ref = rest[:-1], rest[-1]
    i = pl.program_id(0)

    @pl.when(i == 0)
    def _():
        wbf_ref[...] = w_ref[...].astype(BF16)

    row = _cond_row(i)
    h = _modulate(x_ref[...], g_ref[...], _ada_chunk(ada_ref, row, 0), _ada_chunk(ada_ref, row, 1))
    h = h.astype(BF16)
    lo = 0
    for o_ref, width in zip(out_refs, splits):
        o_ref[...] = jnp.dot(h, wbf_ref[:, lo:lo + width], preferred_element_type=F32)
        lo += width


def _modulated_proj(y, ada_l, g, w, splits):
    n = w.shape[1]
    return pl.pallas_call(
        functools.partial(_proj_kernel, splits),
        grid=(N_ROW_TILES,),
        in_specs=[
            pl.BlockSpec((ROW_TILE, D_MODEL), lambda i: (i, 0)),
            pl.BlockSpec((COND_ROWS, ADA_CHUNKS * D_MODEL), lambda i: (0, 0)),
            pl.BlockSpec((1, D_MODEL), lambda i: (0, 0)),
            pl.BlockSpec((D_MODEL, n), lambda i: (0, 0), pipeline_mode=pl.Buffered(1)),
        ],
        out_specs=[pl.BlockSpec((ROW_TILE, s), lambda i: (i, 0)) for s in splits],
        out_shape=[jax.ShapeDtypeStruct((T_ALL, s), F32) for s in splits],
        scratch_shapes=[pltpu.VMEM((D_MODEL, n), BF16)],
        compiler_params=_cparams(1),
    )(y, ada_l, g.reshape(1, D_MODEL), w)


def _out_proj_kernel(n_in, gate_chunk, *refs):
    x_refs = refs[:n_in]
    y_ref, ada_ref, w_ref, o_ref, wbf_ref = refs[n_in:]
    i = pl.program_id(0)

    @pl.when(i == 0)
    def _():
        wbf_ref[...] = w_ref[...].astype(BF16)

    acc = None
    lo = 0
    for x_ref in x_refs:
        k = x_ref.shape[1]
        part = jnp.dot(x_ref[...].astype(BF16), wbf_ref[lo:lo + k, :], preferred_element_type=F32)
        acc = part if acc is None else acc + part
        lo += k
    gate = _ada_chunk(ada_ref, _cond_row(i), gate_chunk)
    o_ref[...] = y_ref[...] + gate * acc


def _out_proj_residual(xs, y, ada_l, w, gate_chunk):
    return pl.pallas_call(
        functools.partial(_out_proj_kernel, len(xs), gate_chunk),
        grid=(N_ROW_TILES,),
        in_specs=[pl.BlockSpec((ROW_TILE, x.shape[1]), lambda i: (i, 0)) for x in xs] + [
            pl.BlockSpec((ROW_TILE, D_MODEL), lambda i: (i, 0)),
            pl.BlockSpec((COND_ROWS, ADA_CHUNKS * D_MODEL), lambda i: (0, 0)),
            pl.BlockSpec((D_MODEL, D_MODEL), lambda i: (0, 0), pipeline_mode=pl.Buffered(1)),
        ],
        out_specs=pl.BlockSpec((ROW_TILE, D_MODEL), lambda i: (i, 0)),
        out_shape=jax.ShapeDtypeStruct((T_ALL, D_MODEL), F32),
        scratch_shapes=[pltpu.VMEM((D_MODEL, D_MODEL), BF16)],
        compiler_params=_cparams(1),
    )(*xs, y, ada_l, w)


def _subhead_norm(x, g2):
    lane = lax.broadcasted_iota(jnp.int32, x.shape, 1)
    first = lane < HD_A
    xx = x * x
    s0 = jnp.sum(jnp.where(first, xx, 0.0), axis=-1, keepdims=True)
    s1 = jnp.sum(jnp.where(first, 0.0, xx), axis=-1, keepdims=True)
    r = jnp.where(first, lax.rsqrt(s0 / HD_A + EPS), lax.rsqrt(s1 / HD_A + EPS))
    return x * r * g2


def _rope(x, cos, sin):
    quarter = HD_A // 4
    lane = lax.broadcasted_iota(jnp.int32, x.shape, 1)
    lower = (lane % (2 * quarter)) < quarter
    swapped = jnp.where(lower, pltpu.roll(x, 2 * HD_A - quarter, 1), pltpu.roll(x, quarter, 1))
    return x * cos + swapped * sin


def _attn_kernel(lam_init, has_ctx, *refs):
    if has_ctx:
        (q_ref, k_ref, v_ref, ck_ref, cv_ref, cq_ref, sq_ref, ckk_ref, skk_ref,
         qg_ref, kg_ref, lp_ref, sg_ref, o_ref) = refs
    else:
        q_ref, k_ref, v_ref, qg_ref, kg_ref, lp_ref, sg_ref, o_ref, kn_ref = refs
    q = _subhead_norm(q_ref[...], qg_ref[...])
    k = _subhead_norm(k_ref[...], kg_ref[...])
    v = v_ref[...]
    if has_ctx:
        q = _rope(q, cq_ref[...], sq_ref[...])
        k = _rope(k, ckk_ref[...], skk_ref[...])
        k = jnp.concatenate([ck_ref[...], k], axis=0)
        v = jnp.concatenate([cv_ref[...], v], axis=0)
    else:
        kn_ref[...] = k
    lp = lp_ref[...]
    lam = (jnp.exp(jnp.sum(lp[0:1] * lp[1:2], axis=-1, keepdims=True))
           - jnp.exp(jnp.sum(lp[2:3] * lp[3:4], axis=-1, keepdims=True)) + lam_init)
    scale = HD_A ** -0.5
    probs = []
    for c in range(2):
        qc = q[:, c * HD_A:(c + 1) * HD_A].astype(BF16)
        kc = k[:, c * HD_A:(c + 1) * HD_A].astype(BF16)
        s = lax.dot_general(qc, kc, (((1,), (1,)), ((), ())), preferred_element_type=F32) * scale
        e = jnp.exp(s - jnp.max(s, axis=-1, keepdims=True))
        probs.append(e / jnp.sum(e, axis=-1, keepdims=True))
    w = probs[0] - lam * probs[1]
    o = _bdot(w, v)
    ms = jnp.mean(o * o, axis=-1, keepdims=True)
    o_ref[...] = (o * lax.rsqrt(ms + EPS) * sg_ref[...]) * (1.0 - lam_init)


def _attention_prompt(qkv, qg2, kg2, lam_p, sub_g, lam_init):
    nh = H_A
    head = 2 * HD_A
    small = [
        pl.BlockSpec((1, head), lambda b, h: (0, 0)),
        pl.BlockSpec((1, head), lambda b, h: (0, 0)),
        pl.BlockSpec((4, HD_A), lambda b, h: (0, 0)),
        pl.BlockSpec((1, head), lambda b, h: (0, 0)),
    ]
    return pl.pallas_call(
        functools.partial(_attn_kernel, lam_init, False),
        grid=(BATCH, nh),
        in_specs=[
            pl.BlockSpec((SEQ, head), lambda b, h: (b, h)),
            pl.BlockSpec((SEQ, head), lambda b, h: (b, nh + h)),
            pl.BlockSpec((SEQ, head), lambda b, h: (b, 2 * nh + h)),
        ] + small,
        out_specs=[pl.BlockSpec((SEQ, head), lambda b, h: (b, h)),
                   pl.BlockSpec((SEQ, head), lambda b, h: (b, h))],
        out_shape=[jax.ShapeDtypeStruct((T_P, W_A), F32), jax.ShapeDtypeStruct((T_P, W_A), F32)],
        compiler_params=_cparams(2),
    )(qkv, qkv, qkv, qg2, kg2, lam_p, sub_g)


def _attention_sample(qkv, cache_k, cache_v, cos, sin, qg2, kg2, lam_p, sub_g, lam_init):
    nh = H_A
    head = 2 * HD_A
    tq = ROW_TILE
    nq = DEC_SEQ // tq
    q_off = T_P // tq
    k_off = T_P // DEC_SEQ
    small = [
        pl.BlockSpec((1, head), lambda b, h, i: (0, 0)),
        pl.BlockSpec((1, head), lambda b, h, i: (0, 0)),
        pl.BlockSpec((4, HD_A), lambda b, h, i: (0, 0)),
        pl.BlockSpec((1, head), lambda b, h, i: (0, 0)),
    ]
    return pl.pallas_call(
        functools.partial(_attn_kernel, lam_init, True),
        grid=(DEC_BATCH, nh, nq),
        in_specs=[
            pl.BlockSpec((tq, head), lambda b, h, i: (q_off + b * nq + i, h)),
            pl.BlockSpec((DEC_SEQ, head), lambda b, h, i: (k_off + b, nh + h)),
            pl.BlockSpec((DEC_SEQ, head), lambda b, h, i: (k_off + b, 2 * nh + h)),
            pl.BlockSpec((None, PAST_LEN, head), lambda b, h, i: (b, 0, h)),
            pl.BlockSpec((None, PAST_LEN, head), lambda b, h, i: (b, 0, h)),
            pl.BlockSpec((tq, head), lambda b, h, i: (i, 0)),
            pl.BlockSpec((tq, head), lambda b, h, i: (i, 0)),
            pl.BlockSpec((DEC_SEQ, head), lambda b, h, i: (0, 0)),
            pl.BlockSpec((DEC_SEQ, head), lambda b, h, i: (0, 0)),
        ] + small,
        out_specs=pl.BlockSpec((tq, head), lambda b, h, i: (b * nq + i, h)),
        out_shape=jax.ShapeDtypeStruct((T_S, W_A), F32),
        compiler_params=_cparams(3),
    )(qkv, qkv, qkv, cache_k, cache_v, cos, sin, cos, sin, qg2, kg2, lam_p, sub_g)


def _rope_tables():
    half = HD_A // 2
    nf = half // 2
    inv = ROPE_BASE ** (-np.arange(nf, dtype=np.float32) / nf)
    pos = np.arange(DEC_SEQ)
    row = (pos // GRID_W).astype(np.float32)
    col = (pos % GRID_W).astype(np.float32)
    ang_r = (row[:, None] * inv).astype(np.float32)
    ang_c = (col[:, None] * inv).astype(np.float32)
    ang = np.concatenate([ang_r, ang_r, ang_c, ang_c], axis=1)
    sign = np.concatenate([-np.ones(nf), np.ones(nf), -np.ones(nf), np.ones(nf)]).astype(np.float32)
    cos = np.cos(ang.astype(np.float64)).astype(np.float32)
    sin = (np.sin(ang.astype(np.float64)) * sign).astype(np.float32)
    return jnp.asarray(np.tile(cos, (1, 2))), jnp.asarray(np.tile(sin, (1, 2)))


def _mlstm_kernel(seq, has_ctx, *refs):
    if has_ctx:
        (q_ref, k_ref, cwq_ref, cwk_ref, cbq_ref, cbk_ref, v_ref, mo_ref, g_ref, gt_ref,
         gb_ref, gbt_ref, hn_ref, c0_ref, n0_ref, m0_ref, o_ref, qs_ref, ks_ref, hs_ref) = refs
    else:
        (q_ref, k_ref, cwq_ref, cwk_ref, cbq_ref, cbk_ref, v_ref, mo_ref, g_ref, gt_ref,
         gb_ref, gbt_ref, hn_ref, o_ref, c_out_ref, n_out_ref, m_out_ref, qs_ref, ks_ref, hs_ref) = refs
    head = pl.program_id(1)
    nc = seq // CHUNK
    qs_ref[...] = _silu(_dwconv3(q_ref[...], cwq_ref[...], cbq_ref[...])) * (HD_B ** -0.5)
    ks_ref[...] = _silu(_dwconv3(k_ref[...], cwk_ref[...], cbk_ref[...]))

    t_idx = lax.broadcasted_iota(jnp.int32, (CHUNK, CHUNK), 0)
    s_idx = lax.broadcasted_iota(jnp.int32, (CHUNK, CHUNK), 1)
    lane16 = lax.broadcasted_iota(jnp.int32, (CHUNK, 4 * H_B), 1)

    def gate_col(gc, idx):
        return jnp.sum(jnp.where(lane16 == idx, gc, 0.0), axis=-1, keepdims=True)

    for d in range(2):
        i_idx = d * 2 * H_B + head
        f_idx = i_idx + H_B
        mask = (s_idx <= t_idx) if d == 0 else (s_idx >= t_idx)
        mask_t = (t_idx <= s_idx) if d == 0 else (t_idx >= s_idx)

        def body(j, carry, d=d, i_idx=i_idx, f_idx=f_idx, mask=mask, mask_t=mask_t):
            cm, nm, mm = carry
            c = j if d == 0 else nc - 1 - j
            r0 = pl.multiple_of(c * CHUNK, CHUNK)
            qt = qs_ref[pl.ds(r0, CHUNK), :]
            kt = ks_ref[pl.ds(r0, CHUNK), :]
            vt = v_ref[pl.ds(r0, CHUNK), :]
            gc = g_ref[pl.ds(r0, CHUNK), :] + gb_ref[...]
            i_col = gate_col(gc, i_idx)
            lf_col = _log_sigmoid(gate_col(gc, f_idx))
            i_row = gt_ref[c, pl.ds(i_idx, 1), :] + gbt_ref[pl.ds(i_idx, 1), :]
            lf_row = _log_sigmoid(gt_ref[c, pl.ds(f_idx, 1), :] + gbt_ref[pl.ds(f_idx, 1), :])
            b_col = jnp.sum(jnp.where(mask, lf_row, 0.0), axis=-1, keepdims=True)
            b_row = jnp.sum(jnp.where(mask_t, lf_col, 0.0), axis=0, keepdims=True)
            b_last = jnp.sum(lf_row, axis=-1, keepdims=True)
            dmat = jnp.where(mask, b_col - b_row + i_row, NEG)
            inter = b_col + mm
            m_t = jnp.maximum(inter, jnp.max(dmat, axis=-1, keepdims=True))
            qk = lax.dot_general(qt.astype(BF16), kt.astype(BF16), (((1,), (1,)), ((), ())),
                                 preferred_element_type=F32)
            s = qk * jnp.exp(dmat - m_t)
            w_inter = jnp.exp(inter - m_t)
            cq = lax.dot_general(qt.astype(BF16), cm.astype(BF16), (((1,), (1,)), ((), ())),
                                 preferred_element_type=F32)
            num = _bdot(s, vt) + w_inter * cq
            nq = jnp.sum(s, axis=-1, keepdims=True) + w_inter * jnp.sum(qt * nm, axis=-1, keepdims=True)
            hout = num / jnp.maximum(jnp.abs(nq), jnp.exp(-m_t))
            if d == 0:
                hs_ref[pl.ds(r0, CHUNK), :] = hout
            else:
                hs_ref[pl.ds(r0, CHUNK), :] += hout
            g_colv = b_last - b_col + i_col
            g_rowv = b_last - b_row + i_row
            m_new = jnp.maximum(b_last + mm, jnp.max(g_rowv, axis=-1, keepdims=True))
            w_c = jnp.exp(b_last + mm - m_new)
            w_k = jnp.exp(g_colv - m_new)
            vw = (vt * w_k).astype(BF16)
            c_new = w_c * cm + lax.dot_general(vw, kt.astype(BF16), (((0,), (0,)), ((), ())),
                                               preferred_element_type=F32)
            n_new = w_c * nm + jnp.sum(kt * w_k, axis=0, keepdims=True)
            return c_new, n_new, m_new

        if has_ctx:
            m0 = jnp.sum(jnp.where(lax.broadcasted_iota(jnp.int32, (1, H_B), 1) == head,
                                   m0_ref[pl.ds(d, 1), :], 0.0), axis=-1, keepdims=True)
            init = (c0_ref[d], n0_ref[d], m0)
        else:
            init = (jnp.zeros((HD_B, HD_B), F32), jnp.zeros((1, HD_B), F32), jnp.zeros((1, 1), F32))
        cm, nm, mm = lax.fori_loop(0, nc, body, init)
        if not has_ctx:
            c_out_ref[d] = cm
            n_out_ref[d] = nm
            m_out_ref[d] = jnp.broadcast_to(mm, (1, HD_B))

    h = hs_ref[...]
    ms = jnp.mean(h * h, axis=-1, keepdims=True)
    o_ref[...] = (h * lax.rsqrt(ms + EPS) * hn_ref[...]) * _sigmoid(mo_ref[...])


def _mlstm(mqk, mv, mo, mg, gt, conv_w, conv_b, gate_b, hn_g, *, seq, nbatch, row_off, ctx=None):
    nh = H_B
    has_ctx = ctx is not None
    blk = lambda col: pl.BlockSpec((seq, HD_B), lambda b, h, col=col: (row_off + b, col * nh + h))
    in_specs = [
        blk(0), blk(1),
        pl.BlockSpec((3, HD_B), lambda b, h: (0, h)),
        pl.BlockSpec((3, HD_B), lambda b, h: (0, nh + h)),
        pl.BlockSpec((1, HD_B), lambda b, h: (0, h)),
        pl.BlockSpec((1, HD_B), lambda b, h: (0, nh + h)),
        blk(0), blk(0),
        pl.BlockSpec((seq, 4 * nh), lambda b, h: (row_off + b, 0)),
        pl.BlockSpec((None, seq // CHUNK, 4 * nh, CHUNK), lambda b, h: (b, 0, 0, 0)),
        pl.BlockSpec((1, 4 * nh), lambda b, h: (0, 0)),
        pl.BlockSpec((4 * nh, 1), lambda b, h: (0, 0)),
        pl.BlockSpec((1, HD_B), lambda b, h: (0, h)),
    ]
    args = [mqk, mqk, conv_w, conv_w, conv_b, conv_b, mv, mo, mg, gt,
            gate_b.reshape(1, 4 * nh), gate_b.reshape(4 * nh, 1), hn_g.reshape(1, W_B)]
    o_spec = pl.BlockSpec((seq, HD_B), lambda b, h: (b, h))
    o_shape = jax.ShapeDtypeStruct((nbatch * seq, W_B), F32)
    if has_ctx:
        c0, n0, m0 = ctx
        in_specs += [
            pl.BlockSpec((None, 2, None, HD_B, HD_B), lambda b, h: (b, 0, h, 0, 0)),
            pl.BlockSpec((None, 2, None, 1, HD_B), lambda b, h: (b, 0, h, 0, 0)),
            pl.BlockSpec((None, 2, nh), lambda b, h: (b, 0, 0)),
        ]
        args += [c0, n0.reshape(nbatch, 2, nh, 1, HD_B), m0]
        out_specs, out_shape = o_spec, o_shape
    else:
        out_specs = [
            o_spec,
            pl.BlockSpec((None, 2, None, HD_B, HD_B), lambda b, h: (b, 0, h, 0, 0)),
            pl.BlockSpec((None, 2, None, 1, HD_B), lambda b, h: (b, 0, h, 0, 0)),
            pl.BlockSpec((None, 2, None, 1, HD_B), lambda b, h: (b, 0, h, 0, 0)),
        ]
        out_shape = [
            o_shape,
            jax.ShapeDtypeStruct((nbatch, 2, nh, HD_B, HD_B), F32),
            jax.ShapeDtypeStruct((nbatch, 2, nh, 1, HD_B), F32),
            jax.ShapeDtypeStruct((nbatch, 2, nh, 1, HD_B), F32),
        ]
    return pl.pallas_call(
        functools.partial(_mlstm_kernel, seq, has_ctx),
        grid=(nbatch, nh),
        in_specs=in_specs,
        out_specs=out_specs,
        out_shape=out_shape,
        scratch_shapes=[pltpu.VMEM((seq, HD_B), F32)] * 3,
        compiler_params=_cparams(2),
    )(*args)


def _dft_mats(L):
    f = np.arange(L)[:, None]
    j = np.arange(L)[None, :]
    ang = 2.0 * np.pi * ((f * j) % (2 * L)) / (2 * L)
    cm = np.cos(ang)
    sm = np.sin(ang)
    alt = (1.0 - 2.0 * (np.arange(L) % 2))
    fwd_b = -sm
    fwd_b[0, :] = alt
    fwd = np.concatenate([cm, fwd_b], axis=0)
    wgt = np.where(np.arange(L) == 0, 1.0, 2.0)[None, :]
    inv_a = cm.T * wgt
    inv_b = -2.0 * sm.T
    inv_b[:, 0] = alt
    inv = np.concatenate([inv_a, inv_b], axis=1) / (2 * L)
    return jnp.asarray(fwd.astype(np.float32)), jnp.asarray(inv.astype(np.float32))


def _hyena_feats(L):
    t = np.linspace(0.0, 1.0, L, dtype=np.float32)
    wpos = (2.0 * math.pi * np.arange(L, dtype=np.float32) / L).astype(np.float32)
    fb = np.linspace(1e-4, HY_BANDS - 1, HY_BANDS, dtype=np.float32)
    z = (wpos[:, None] * fb).astype(np.float32)
    feats = np.concatenate([t[:, None], np.cos(z), -np.sin(z)], axis=-1).astype(np.float32)
    deltas = np.abs(np.linspace(math.log(HY_TARGET) / HY_SLOW_PCT, math.log(HY_TARGET) / HY_FAST_PCT,
                                D_MODEL, dtype=np.float32))
    decay = np.exp(-t[:, None] * deltas).astype(np.float32)
    return jnp.asarray(feats), jnp.asarray(decay)


def _filter_kernel(L, feats_ref, w1_ref, b1_ref, fr1_ref, w2_ref, b2_ref, fr2_ref, w3f_ref, w3b_ref,
                   decay_ref, fwd_ref, o_ref, hdn_ref):
    @pl.when((pl.program_id(0) == 0) & (pl.program_id(1) == 0))
    def _():
        h1 = jnp.sin(fr1_ref[...] * (jnp.dot(feats_ref[...], w1_ref[...], precision=HIGHEST,
                                             preferred_element_type=F32) + b1_ref[...]))
        hdn_ref[...] = jnp.sin(fr2_ref[...] * (jnp.dot(h1, w2_ref[...], precision=HIGHEST,
                                                       preferred_element_type=F32) + b2_ref[...]))

    hdn = hdn_ref[...]
    decay = decay_ref[...]
    f_fwd = jnp.dot(hdn, w3f_ref[...], precision=HIGHEST, preferred_element_type=F32) * decay
    f_bwd = jnp.dot(hdn, w3b_ref[...], precision=HIGHEST, preferred_element_type=F32) * decay
    row = lax.broadcasted_iota(jnp.int32, f_bwd.shape, 0)
    f_bwd = jnp.where(row == 0, 0.0, f_bwd)
    fwd = fwd_ref[...]
    p = _bdot(fwd, f_fwd)
    q = _bdot(fwd, f_bwd)
    first = row == 0
    o_ref[0:L, :] = p[0:L] + q[0:L]
    o_ref[L:2 * L, :] = p[L:2 * L] + jnp.where(first, q[L:2 * L], -q[L:2 * L])


def _hyena_filter_spectrum(L, fwd_bf, w1, b1, fr1, w2, b2, fr2, w3):
    feats, decay = _hyena_feats(L)
    td = 512
    nd = D_MODEL // td
    emb = feats.shape[1]
    vec = lambda a: a.reshape(1, HY_FH)
    full = lambda shape: pl.BlockSpec(shape, lambda o, j: (0, 0))
    return pl.pallas_call(
        functools.partial(_filter_kernel, L),
        grid=(HY_ORDER, nd),
        in_specs=[
            full((L, emb)), full((emb, HY_FH)), full((1, HY_FH)), full((1, HY_FH)),
            full((HY_FH, HY_FH)), full((1, HY_FH)), full((1, HY_FH)),
            pl.BlockSpec((HY_FH, td), lambda o, j: (0, o * 2 * nd + j)),
            pl.BlockSpec((HY_FH, td), lambda o, j: (0, o * 2 * nd + nd + j)),
            pl.BlockSpec((L, td), lambda o, j: (0, j)),
            full((2 * L, L)),
        ],
        out_specs=pl.BlockSpec((2 * L, td), lambda o, j: (0, o * nd + j)),
        out_shape=jax.ShapeDtypeStruct((2 * L, HY_ORDER * D_MODEL), F32),
        scratch_shapes=[pltpu.VMEM((L, HY_FH), F32)],
        compiler_params=_cparams(2),
    )(feats, w1, vec(b1), vec(fr1), w2, vec(b2), vec(fr2), w3, w3, decay, fwd_bf)


def _spectral_conv(u, fwd, inv, kspec, L):
    uf = jnp.dot(fwd, u.astype(BF16), preferred_element_type=F32)
    ua, ub = uf[0:L], uf[L:2 * L]
    ka, kb = kspec[0:L], kspec[L:2 * L]
    first = lax.broadcasted_iota(jnp.int32, ua.shape, 0) == 0
    ya = ua * ka - jnp.where(first, 0.0, ub * kb)
    yb = jnp.where(first, ub * kb, ua * kb + ub * ka)
    y = jnp.concatenate([ya, yb], axis=0).astype(BF16)
    return jnp.dot(inv, y, preferred_element_type=F32)


def _hyena_kernel(L, zv_ref, z1_ref, z2_ref, cwv_ref, cw1_ref, cw2_ref, cbv_ref, cb1_ref, cb2_ref,
                  fwd_ref, inv_ref, k0_ref, k1_ref, bias0_ref, bias1_ref, o_ref):
    fwd = fwd_ref[...]
    inv = inv_ref[...]
    v = _dwconv3(zv_ref[...], cwv_ref[...], cbv_ref[...])
    x1 = _dwconv3(z1_ref[...], cw1_ref[...], cb1_ref[...])
    x2 = _dwconv3(z2_ref[...], cw2_ref[...], cb2_ref[...])
    z = x1 * (_spectral_conv(v, fwd, inv, k0_ref[...], L) + v * bias0_ref[...])
    o_ref[...] = x2 * (_spectral_conv(z, fwd, inv, k1_ref[...], L) + z * bias1_ref[...])


def _hyena_core(zproj, conv_w, conv_b, fwd_bf, inv_bf, kspec, bias, *, seq, nbatch, row_off, td):
    nd = D_MODEL // td
    zblk = lambda part: pl.BlockSpec((seq, td), lambda b, j, part=part: (row_off + b, part * nd + j))
    cwblk = lambda part: pl.BlockSpec((3, td), lambda b, j, part=part: (0, part * nd + j))
    cbblk = lambda part: pl.BlockSpec((1, td), lambda b, j, part=part: (0, part * nd + j))
    return pl.pallas_call(
        functools.partial(_hyena_kernel, seq),
        grid=(nbatch, nd),
        in_specs=[
            zblk(0), zblk(1), zblk(2), cwblk(0), cwblk(1), cwblk(2), cbblk(0), cbblk(1), cbblk(2),
            pl.BlockSpec((2 * seq, seq), lambda b, j: (0, 0), pipeline_mode=pl.Buffered(1)),
            pl.BlockSpec((seq, 2 * seq), lambda b, j: (0, 0), pipeline_mode=pl.Buffered(1)),
            pl.BlockSpec((2 * seq, td), lambda b, j: (0, j)),
            pl.BlockSpec((2 * seq, td), lambda b, j: (0, nd + j)),
            pl.BlockSpec((None, 1, td), lambda b, j: (0, 0, j)),
            pl.BlockSpec((None, 1, td), lambda b, j: (1, 0, j)),
        ],
        out_specs=pl.BlockSpec((seq, td), lambda b, j: (b, j)),
        out_shape=jax.ShapeDtypeStruct((nbatch * seq, D_MODEL), F32),
        compiler_params=_cparams(2),
    )(zproj, zproj, zproj, conv_w, conv_w, conv_w, conv_b, conv_b, conv_b,
      fwd_bf, inv_bf, kspec, kspec, bias.reshape(HY_ORDER, 1, D_MODEL), bias.reshape(HY_ORDER, 1, D_MODEL))


def _router_kernel(x_ref, ada_ref, g_ref, rw_ref, rb_ref, h_ref, idx_ref, wt_ref):
    i = pl.program_id(0)
    row = _cond_row(i)
    h = _modulate(x_ref[...], g_ref[...], _ada_chunk(ada_ref, row, 3), _ada_chunk(ada_ref, row, 4))
    h_ref[...] = h.astype(BF16)
    logits = jnp.dot(h, rw_ref[...], precision=HIGHEST, preferred_element_type=F32) + rb_ref[...]
    lane = lax.broadcasted_iota(jnp.int32, logits.shape, 1)
    slot = lax.broadcasted_iota(jnp.int32, (logits.shape[0], TOP_K), 1)
    vals = jnp.zeros((logits.shape[0], TOP_K), F32)
    idxs = jnp.zeros((logits.shape[0], TOP_K), jnp.int32)
    cur = logits
    for k in range(TOP_K):
        m = jnp.max(cur, axis=-1, keepdims=True)
        a = jnp.min(jnp.where(cur == m, lane, N_EXPERTS), axis=-1, keepdims=True)
        vals = jnp.where(slot == k, m, vals)
        idxs = jnp.where(slot == k, a, idxs)
        cur = jnp.where(lane == a, -jnp.inf, cur)
    e = jnp.exp(vals - vals[:, 0:1])
    wt_ref[...] = e / jnp.sum(e, axis=-1, keepdims=True)
    idx_ref[...] = idxs


def _router(y, ada_l, g, router_w, router_b):
    return pl.pallas_call(
        _router_kernel,
        grid=(N_ROW_TILES,),
        in_specs=[
            pl.BlockSpec((ROW_TILE, D_MODEL), lambda i: (i, 0)),
            pl.BlockSpec((COND_ROWS, ADA_CHUNKS * D_MODEL), lambda i: (0, 0)),
            pl.BlockSpec((1, D_MODEL), lambda i: (0, 0)),
            pl.BlockSpec((D_MODEL, N_EXPERTS), lambda i: (0, 0)),
            pl.BlockSpec((1, N_EXPERTS), lambda i: (0, 0)),
        ],
        out_specs=[
            pl.BlockSpec((ROW_TILE, D_MODEL), lambda i: (i, 0)),
            pl.BlockSpec((ROW_TILE, TOP_K), lambda i: (i, 0)),
            pl.BlockSpec((ROW_TILE, TOP_K), lambda i: (i, 0)),
        ],
        out_shape=[
            jax.ShapeDtypeStruct((T_ALL, D_MODEL), BF16),
            jax.ShapeDtypeStruct((T_ALL, TOP_K), jnp.int32),
            jax.ShapeDtypeStruct((T_ALL, TOP_K), F32),
        ],
        compiler_params=_cparams(1),
    )(y, ada_l, g.reshape(1, D_MODEL), router_w, router_b.reshape(1, N_EXPERTS))


def _deinterleave_matrix():
    s = np.zeros((256, 256), np.float32)
    j = np.arange(128)
    s[2 * j, j] = 1.0
    s[2 * j + 1, 128 + j] = 1.0
    return jnp.asarray(s)


def _weight_copies(layer, e, w1_hbm, w2_hbm, w1s_ref, w2s_ref, sem):
    copies = []
    r1 = D_MODEL // W1_DMA_CHUNKS
    for c in range(W1_DMA_CHUNKS):
        copies.append(pltpu.make_async_copy(w1_hbm.at[layer, e, pl.ds(c * r1, r1)],
                                            w1s_ref.at[pl.ds(c * r1, r1)], sem.at[c]))
    r2 = D_FF // W2_DMA_CHUNKS
    for c in range(W2_DMA_CHUNKS):
        copies.append(pltpu.make_async_copy(w2_hbm.at[layer, e, pl.ds(c * r2, r2)],
                                            w2s_ref.at[pl.ds(c * r2, r2)], sem.at[W1_DMA_CHUNKS + c]))
    return copies


def _expert_kernel(layer, te_ref, tf_ref, ne_ref, nu_ref, src_ref, x_ref, b1_ref, b2_ref, wt_ref, s_ref,
                   w1_hbm, w2_hbm, o_hbm, w1s_ref, w2s_ref, w1p_ref, w2p_ref, acc_ref, out_ref, wsem, osem):
    i = pl.program_id(0)
    half = 128
    copies = functools.partial(_weight_copies, layer, w1_hbm=w1_hbm, w2_hbm=w2_hbm,
                               w1s_ref=w1s_ref, w2s_ref=w2s_ref, sem=wsem)

    @pl.when(i == 0)
    def _():
        acc_ref[...] = jnp.zeros_like(acc_ref)
        for cp in copies(te_ref[0]):
            cp.start()

    @pl.when(tf_ref[i] == 1)
    def _():
        for cp in copies(te_ref[i]):
            cp.wait()
        s = s_ref[...].astype(BF16)
        for c in range(2 * D_FF // 256):
            blk = jnp.dot(w1s_ref[:, c * 256:(c + 1) * 256].astype(BF16), s, preferred_element_type=F32)
            w1p_ref[:, c * half:(c + 1) * half] = blk[:, :half].astype(BF16)
            w1p_ref[:, D_FF + c * half:D_FF + (c + 1) * half] = blk[:, half:].astype(BF16)
        w2p_ref[...] = w2s_ref[...].astype(BF16)

        @pl.when(ne_ref[i] >= 0)
        def _():
            for cp in copies(ne_ref[i]):
                cp.start()

    @pl.when(i < nu_ref[0])
    def _():
        a = jnp.dot(x_ref[...], w1p_ref[...], preferred_element_type=F32) + b1_ref[...]
        glu = jnp.minimum(a[:, :D_FF], SWIGLU_LIMIT)
        lin = jnp.clip(a[:, D_FF:], -SWIGLU_LIMIT, SWIGLU_LIMIT)
        hid = glu * _sigmoid(SWIGLU_ALPHA * glu) * (lin + 1.0)
        out = jnp.dot(hid.astype(BF16), w2p_ref[...], preferred_element_type=F32) + b2_ref[...]
        out_ref[...] = out * wt_ref[...]
        base = i * MOE_TILE

        def group(gi, carry):
            r0 = pl.multiple_of(gi * SCATTER_GROUP, SCATTER_GROUP)
            toks = [src_ref[base + r0 + g] for g in range(SCATTER_GROUP)]
            cur = [acc_ref[pl.ds(toks[g], 1), :] for g in range(SCATTER_GROUP)]
            add = [out_ref[pl.ds(r0 + g, 1), :] for g in range(SCATTER_GROUP)]
            for g in range(SCATTER_GROUP):
                acc_ref[pl.ds(toks[g], 1), :] = cur[g] + add[g]
            return carry

        lax.fori_loop(0, MOE_TILE // SCATTER_GROUP, group, 0)

    @pl.when(i == pl.num_programs(0) - 1)
    def _():
        cp = pltpu.make_async_copy(acc_ref.at[pl.ds(0, T_ALL)], o_hbm, osem)
        cp.start()
        cp.wait()


def _experts(layer, x_sorted, w_sorted, plan, w1, b1p, w2, b2):
    tile_expert, tile_first, next_expert, n_used, src = plan
    grid_spec = pltpu.PrefetchScalarGridSpec(
        num_scalar_prefetch=5,
        grid=(MOE_TILES,),
        in_specs=[
            pl.BlockSpec((MOE_TILE, D_MODEL), lambda i, te, *_: (i, 0)),
            pl.BlockSpec((None, None, 1, 2 * D_FF), lambda i, te, *_: (layer, te[i], 0, 0)),
            pl.BlockSpec((None, None, 1, D_MODEL), lambda i, te, *_: (layer, te[i], 0, 0)),
            pl.BlockSpec((MOE_TILE, 1), lambda i, te, *_: (i, 0)),
            pl.BlockSpec((256, 256), lambda i, te, *_: (0, 0)),
            pl.BlockSpec(memory_space=pl.ANY),
            pl.BlockSpec(memory_space=pl.ANY),
        ],
        out_specs=pl.BlockSpec(memory_space=pl.ANY),
        scratch_shapes=[
            pltpu.VMEM((D_MODEL, 2 * D_FF), F32),
            pltpu.VMEM((D_FF, D_MODEL), F32),
            pltpu.VMEM((D_MODEL, 2 * D_FF), BF16),
            pltpu.VMEM((D_FF, D_MODEL), BF16),
            pltpu.VMEM((ACC_ROWS, D_MODEL), F32),
            pltpu.VMEM((MOE_TILE, D_MODEL), F32),
            pltpu.SemaphoreType.DMA((W1_DMA_CHUNKS + W2_DMA_CHUNKS,)),
            pltpu.SemaphoreType.DMA(()),
        ],
    )
    return pl.pallas_call(
        functools.partial(_expert_kernel, layer),
        grid_spec=grid_spec,
        out_shape=jax.ShapeDtypeStruct((T_ALL, D_MODEL), F32),
        compiler_params=_cparams(1),
    )(tile_expert, tile_first, next_expert, n_used, src, x_sorted, b1p, b2, w_sorted,
      _deinterleave_matrix(), w1, w2)


def _combine_kernel(y_ref, a_ref, ada_ref, o_ref):
    gate = _ada_chunk(ada_ref, _cond_row(pl.program_id(0)), 5)
    o_ref[...] = y_ref[...] + gate * a_ref[...]


def _combine(y, acc, ada_l):
    return pl.pallas_call(
        _combine_kernel,
        grid=(N_ROW_TILES,),
        in_specs=[
            pl.BlockSpec((ROW_TILE, D_MODEL), lambda i: (i, 0)),
            pl.BlockSpec((ROW_TILE, D_MODEL), lambda i: (i, 0)),
            pl.BlockSpec((COND_ROWS, ADA_CHUNKS * D_MODEL), lambda i: (0, 0)),
        ],
        out_specs=pl.BlockSpec((ROW_TILE, D_MODEL), lambda i: (i, 0)),
        out_shape=jax.ShapeDtypeStruct((T_ALL, D_MODEL), F32),
        compiler_params=_cparams(1),
    )(y, acc, ada_l)


def _routing_plan(idx, wts):
    eid = idx.reshape(-1)
    order = jnp.argsort(eid, stable=True).astype(jnp.int32)
    experts = jnp.arange(N_EXPERTS, dtype=jnp.int32)
    counts = jnp.sum(eid[:, None] == experts[None, :], axis=0).astype(jnp.int32)
    ntiles = (counts + MOE_TILE - 1) // MOE_TILE
    tile_end = jnp.cumsum(ntiles).astype(jnp.int32)
    tile_begin = tile_end - ntiles
    cstarts = (jnp.cumsum(counts) - counts).astype(jnp.int32)
    n_used = tile_end[-1]
    tile = jnp.arange(MOE_TILES, dtype=jnp.int32)
    te = jnp.minimum(jnp.sum(tile[:, None] >= tile_end[None, :], axis=1), N_EXPERTS - 1).astype(jnp.int32)
    used = tile < n_used
    prev = jnp.concatenate([jnp.full((1,), -1, jnp.int32), te[:-1]])
    first = (te != prev) & used
    nxt = tile_end[te]
    next_expert = jnp.where(first & (nxt < n_used), te[jnp.minimum(nxt, MOE_TILES - 1)], -1).astype(jnp.int32)
    off = (tile - tile_begin[te])[:, None] * MOE_TILE + jnp.arange(MOE_TILE, dtype=jnp.int32)[None, :]
    valid = (off < counts[te][:, None]) & used[:, None]
    assign = order[jnp.clip(cstarts[te][:, None] + off, 0, N_ASSIGN - 1)]
    token = assign // TOP_K
    src = jnp.where(valid, token, SPARE_ROW).reshape(MOE_ROWS).astype(jnp.int32)
    gather_row = jnp.where(valid, token, 0).reshape(MOE_ROWS)
    w_sorted = jnp.where(valid, wts.reshape(-1)[assign], 0.0).reshape(MOE_ROWS, 1)
    plan = (te, first.astype(jnp.int32), next_expert, n_used.reshape(1), src)
    return plan, gather_row, w_sorted


def _moe(layer, y, ada_l, g, router_w, router_b, w1, b1p, w2, b2):
    h, idx, wts = _router(y, ada_l, g, router_w, router_b)
    plan, gather_row, w_sorted = _routing_plan(idx, wts)
    x_sorted = jnp.take(h, gather_row, axis=0)
    acc = _experts(layer, x_sorted, w_sorted, plan, w1, b1p, w2, b2)
    return _combine(y, acc, ada_l)


def _chunked_gates_t(mg_stream, nbatch, seq):
    return mg_stream.reshape(nbatch, seq // CHUNK, CHUNK, 4 * H_B).transpose(0, 1, 3, 2)


def kernel(x_prompt, x_sample, cache_attn_k, cache_attn_v, state_mlstm_C, state_mlstm_n, state_mlstm_m, c, c_ctx, ada_w, ada_b, norm_mix_g, norm_ffn_g, ab_w_in, ab_w_out, da_qnorm_g, da_knorm_g, da_lambda, da_subnorm_g, ml_conv_w, ml_conv_b, ml_gate_b, ml_headnorm_g, hy_w_in, hy_w_out, hy_conv_w, hy_conv_b, hy_f_w1, hy_f_b1, hy_f_freq1, hy_f_w2, hy_f_b2, hy_f_freq2, hy_f_w3, hy_bias, router_w, router_b, moe_w1, moe_b1, moe_w2, moe_b2):
    y = jnp.concatenate([x_prompt.reshape(T_P, D_MODEL), x_sample.reshape(T_S, D_MODEL)], axis=0)
    cond = jnp.concatenate([c_ctx[None, :], c, jnp.zeros((COND_ROWS - 1 - DEC_BATCH, D_MODEL), F32)], axis=0)
    ada = _ada_table(cond, ada_w, ada_b)
    b1p = jnp.concatenate([moe_b1[..., 0::2], moe_b1[..., 1::2]], axis=-1).reshape(DEPTH, N_EXPERTS, 1, 2 * D_FF)
    b2r = moe_b2.reshape(DEPTH, N_EXPERTS, 1, D_MODEL)
    new_k, new_v, new_c, new_n, new_m = [], [], [], [], []
    for layer in range(DEPTH):
        ada_l = ada[layer]
        if layer % 2 == 0:
            e = layer // 2
            lam_init = 0.8 - 0.6 * math.exp(-0.3 * layer)
            qkv, mqk, mv, mo, mg = _modulated_proj(
                y, ada_l, norm_mix_g[layer], ab_w_in[e], (3 * W_A, 2 * W_B, W_B, W_B, 4 * H_B))
            qg2 = jnp.tile(da_qnorm_g[e], 2).reshape(1, 2 * HD_A)
            kg2 = jnp.tile(da_knorm_g[e], 2).reshape(1, 2 * HD_A)
            sub_g = da_subnorm_g[e].reshape(1, 2 * HD_A)
            oa_p, k_norm = _attention_prompt(qkv, qg2, kg2, da_lambda[e], sub_g, lam_init)
            cos, sin = _rope_tables()
            oa_s = _attention_sample(
                qkv, cache_attn_k[:, e].reshape(DEC_BATCH, PAST_LEN, W_A),
                cache_attn_v[:, e].reshape(DEC_BATCH, PAST_LEN, W_A), cos, sin,
                qg2, kg2, da_lambda[e], sub_g, lam_init)
            gt_p = _chunked_gates_t(mg[:T_P], BATCH, SEQ)
            gt_s = _chunked_gates_t(mg[T_P:], DEC_BATCH, DEC_SEQ)
            ob_p, c_new, n_new, m_new = _mlstm(
                mqk, mv, mo, mg, gt_p, ml_conv_w[e], ml_conv_b[e].reshape(1, 2 * W_B), ml_gate_b[e],
                ml_headnorm_g[e], seq=SEQ, nbatch=BATCH, row_off=0)
            ob_s = _mlstm(
                mqk, mv, mo, mg, gt_s, ml_conv_w[e], ml_conv_b[e].reshape(1, 2 * W_B), ml_gate_b[e],
                ml_headnorm_g[e], seq=DEC_SEQ, nbatch=DEC_BATCH, row_off=T_P // DEC_SEQ,
                ctx=(state_mlstm_C[:, e], state_mlstm_n[:, e], state_mlstm_m[:, e]))
            o_a = jnp.concatenate([oa_p, oa_s], axis=0)
            o_b = jnp.concatenate([ob_p, ob_s], axis=0)
            y = _out_proj_residual([o_a, o_b], y, ada_l, ab_w_out[e], 2)
            new_k.append(k_norm.reshape(BATCH, SEQ, H_A, 2, HD_A))
            new_v.append(qkv[:T_P, 2 * W_A:].reshape(BATCH, SEQ, H_A, 2 * HD_A))
            new_c.append(c_new)
            new_n.append(n_new.reshape(BATCH, 2, H_B, HD_B))
            new_m.append(m_new[..., 0, 0])
        else:
            o = layer // 2
            (zproj,) = _modulated_proj(y, ada_l, norm_mix_g[layer], hy_w_in[o], (HY_PROJ,))
            cores = []
            for seq, nbatch, row_off, td in ((SEQ, BATCH, 0, 512), (DEC_SEQ, DEC_BATCH, T_P // DEC_SEQ, 256)):
                fwd, inv = _dft_mats(seq)
                fwd_bf, inv_bf = fwd.astype(BF16), inv.astype(BF16)
                kspec = _hyena_filter_spectrum(seq, fwd_bf, hy_f_w1[o], hy_f_b1[o], hy_f_freq1[o], hy_f_w2[o],
                                               hy_f_b2[o], hy_f_freq2[o], hy_f_w3[o])
                cores.append(_hyena_core(zproj, hy_conv_w[o], hy_conv_b[o].reshape(1, HY_PROJ), fwd_bf, inv_bf,
                                         kspec, hy_bias[o], seq=seq, nbatch=nbatch, row_off=row_off, td=td))
            y = _out_proj_residual([jnp.concatenate(cores, axis=0)], y, ada_l, hy_w_out[o], 2)
        y = _moe(layer, y, ada_l, norm_ffn_g[layer], router_w[layer], router_b[layer],
                 moe_w1, b1p, moe_w2, b2r)
    y_p = y[:T_P].reshape(BATCH, SEQ, D_MODEL)
    y_s = y[T_P:].reshape(DEC_BATCH, DEC_SEQ, D_MODEL)
    return (y_p, y_s, jnp.stack(new_k, axis=1), jnp.stack(new_v, axis=1), jnp.stack(new_c, axis=1),
            jnp.stack(new_n, axis=1), jnp.stack(new_m, axis=1))
```

```python
import functools
import math

import numpy as np
import jax
import jax.numpy as jnp
from jax import lax
from jax.experimental import pallas as pl
from jax.experimental.pallas import tpu as pltpu

D_MODEL = 1024
BATCH = 16
SEQ = 256
DEPTH = 2
DEC_BATCH = 2
DEC_SEQ = 1024
PAST_LEN = 256
GRID_W = 64
W_A = D_MODEL // 2
HD_A = 64
H_A = W_A // (2 * HD_A)
W_B = D_MODEL - W_A
HD_B = 128
H_B = W_B // HD_B
AB_PROJ = 3 * W_A + 4 * W_B + 4 * H_B
ROPE_BASE = 10000.0
CHUNK = 64
HY_ORDER = 2
HY_PROJ = (HY_ORDER + 1) * D_MODEL
HY_BANDS = 8
HY_FH = 64
HY_TARGET = 1e-2
HY_FAST_PCT = 0.3
HY_SLOW_PCT = 1.5
N_EXPERTS = 32
TOP_K = 4
D_FF = D_MODEL
SWIGLU_ALPHA = 1.702
SWIGLU_LIMIT = 7.0
ADA_CHUNKS = 6
EPS = 1e-6
NEG = -1e30
F32 = jnp.float32
BF16 = jnp.bfloat16

T_P = BATCH * SEQ
T_S = DEC_BATCH * DEC_SEQ
T_ALL = T_P + T_S
ROW_TILE = 256
N_ROW_TILES = T_ALL // ROW_TILE
P_TILES = T_P // ROW_TILE
S_TILES_PER_BATCH = DEC_SEQ // ROW_TILE
COND_ROWS = 8
MOE_TILE = 256
N_ASSIGN = T_ALL * TOP_K
MOE_ROWS = N_ASSIGN + N_EXPERTS * MOE_TILE
MOE_TILES = MOE_ROWS // MOE_TILE
SPARE_ROW = T_ALL
ACC_ROWS = T_ALL + 8
SCATTER_GROUP = 8
W1_DMA_CHUNKS = 4
W2_DMA_CHUNKS = 2
VMEM_LIMIT = 56 * 1024 * 1024
HIGHEST = lax.Precision.HIGHEST


def _cparams(n_axes):
    return pltpu.CompilerParams(dimension_semantics=("arbitrary",) * n_axes,
                                vmem_limit_bytes=VMEM_LIMIT)


def _bdot(a, b):
    return jnp.dot(a.astype(BF16), b.astype(BF16), preferred_element_type=F32)


def _cond_row(i):
    return jnp.where(i < P_TILES, 0, 1 + (i - P_TILES) // S_TILES_PER_BATCH)


def _ada_chunk(ada_ref, row, j):
    return ada_ref[pl.ds(row, 1), j * D_MODEL:(j + 1) * D_MODEL]


def _modulate(x, g, shift, scale):
    ms = jnp.mean(x * x, axis=-1, keepdims=True)
    return (x * lax.rsqrt(ms + EPS) * g) * (1.0 + scale) + shift


def _sigmoid(x):
    return 1.0 / (1.0 + jnp.exp(-x))


def _silu(x):
    return x * _sigmoid(x)


def _log_sigmoid(x):
    return jnp.minimum(x, 0.0) - jnp.log(1.0 + jnp.exp(-jnp.abs(x)))


def _dwconv3(x, w, b):
    n = x.shape[0]
    row = lax.broadcasted_iota(jnp.int32, x.shape, 0)
    prev = jnp.where(row == 0, 0.0, pltpu.roll(x, 1, 0))
    nxt = jnp.where(row == n - 1, 0.0, pltpu.roll(x, n - 1, 0))
    return prev * w[0:1] + x * w[1:2] + nxt * w[2:3] + b


def _ada_kernel(cond_ref, w_ref, b_ref, o_ref):
    c = _silu(cond_ref[...])
    o_ref[...] = _bdot(c, w_ref[...]) + b_ref[...]


def _ada_table(cond, ada_w, ada_b):
    tn = 1536
    return pl.pallas_call(
        _ada_kernel,
        grid=(DEPTH, ADA_CHUNKS * D_MODEL // tn),
        in_specs=[
            pl.BlockSpec((COND_ROWS, D_MODEL), lambda l, j: (0, 0)),
            pl.BlockSpec((None, D_MODEL, tn), lambda l, j: (l, 0, j)),
            pl.BlockSpec((None, 1, tn), lambda l, j: (l, 0, j)),
        ],
        out_specs=pl.BlockSpec((None, COND_ROWS, tn), lambda l, j: (l, 0, j)),
        out_shape=jax.ShapeDtypeStruct((DEPTH, COND_ROWS, ADA_CHUNKS * D_MODEL), F32),
        compiler_params=_cparams(2),
    )(cond, ada_w, ada_b.reshape(DEPTH, 1, ADA_CHUNKS * D_MODEL))


def _proj_kernel(splits, x_ref, ada_ref, g_ref, w_ref, *rest):
    out_refs, wbf_ref = rest[:-1], rest[-1]
    i = pl.program_id(0)

    @pl.when(i == 0)
    def _():
        wbf_ref[...] = w_ref[...].astype(BF16)

    row = _cond_row(i)
    h = _modulate(x_ref[...], g_ref[...], _ada_chunk(ada_ref, row, 0), _ada_chunk(ada_ref, row, 1))
    h = h.astype(BF16)
    lo = 0
    for o_ref, width in zip(out_refs, splits):
        o_ref[...] = jnp.dot(h, wbf_ref[:, lo:lo + width], preferred_element_type=F32)
        lo += width


def _modulated_proj(y, ada_l, g, w, splits):
    n = w.shape[1]
    return pl.pallas_call(
        functools.partial(_proj_kernel, splits),
        grid=(N_ROW_TILES,),
        in_specs=[
            pl.BlockSpec((ROW_TILE, D_MODEL), lambda i: (i, 0)),
            pl.BlockSpec((COND_ROWS, ADA_CHUNKS * D_MODEL), lambda i: (0, 0)),
            pl.BlockSpec((1, D_MODEL), lambda i: (0, 0)),
            pl.BlockSpec((D_MODEL, n), lambda i: (0, 0), pipeline_mode=pl.Buffered(1)),
        ],
        out_specs=[pl.BlockSpec((ROW_TILE, s), lambda i: (i, 0)) for s in splits],
        out_shape=[jax.ShapeDtypeStruct((T_ALL, s), F32) for s in splits],
        scratch_shapes=[pltpu.VMEM((D_MODEL, n), BF16)],
        compiler_params=_cparams(1),
    )(y, ada_l, g.reshape(1, D_MODEL), w)


def _out_proj_kernel(n_in, gate_chunk, *refs):
    x_refs = refs[:n_in]
    y_ref, ada_ref, w_ref, o_ref, wbf_ref = refs[n_in:]
    i = pl.program_id(0)

    @pl.when(i == 0)
    def _():
        wbf_ref[...] = w_ref[...].astype(BF16)

    acc = None
    lo = 0
    for x_ref in x_refs:
        k = x_ref.shape[1]
        part = jnp.dot(x_ref[...].astype(BF16), wbf_ref[lo:lo + k, :], preferred_element_type=F32)
        acc = part if acc is None else acc + part
        lo += k
    gate = _ada_chunk(ada_ref, _cond_row(i), gate_chunk)
    o_ref[...] = y_ref[...] + gate * acc


def _out_proj_residual(xs, y, ada_l, w, gate_chunk):
    return pl.pallas_call(
        functools.partial(_out_proj_kernel, len(xs), gate_chunk),
        grid=(N_ROW_TILES,),
        in_specs=[pl.BlockSpec((ROW_TILE, x.shape[1]), lambda i: (i, 0)) for x in xs] + [
            pl.BlockSpec((ROW_TILE, D_MODEL), lambda i: (i, 0)),
            pl.BlockSpec((COND_ROWS, ADA_CHUNKS * D_MODEL), lambda i: (0, 0)),
            pl.BlockSpec((D_MODEL, D_MODEL), lambda i: (0, 0), pipeline_mode=pl.Buffered(1)),
        ],
        out_specs=pl.BlockSpec((ROW_TILE, D_MODEL), lambda i: (i, 0)),
        out_shape=jax.ShapeDtypeStruct((T_ALL, D_MODEL), F32),
        scratch_shapes=[pltpu.VMEM((D_MODEL, D_MODEL), BF16)],
        compiler_params=_cparams(1),
    )(*xs, y, ada_l, w)


def _subhead_norm(x, g2):
    lane = lax.broadcasted_iota(jnp.int32, x.shape, 1)
    first = lane < HD_A
    xx = x * x
    s0 = jnp.sum(jnp.where(first, xx, 0.0), axis=-1, keepdims=True)
    s1 = jnp.sum(jnp.where(first, 0.0, xx), axis=-1, keepdims=True)
    r = jnp.where(first, lax.rsqrt(s0 / HD_A + EPS), lax.rsqrt(s1 / HD_A + EPS))
    return x * r * g2


def _rope(x, cos, sin):
    quarter = HD_A // 4
    lane = lax.broadcasted_iota(jnp.int32, x.shape, 1)
    lower = (lane % (2 * quarter)) < quarter
    swapped = jnp.where(lower, pltpu.roll(x, 2 * HD_A - quarter, 1), pltpu.roll(x, quarter, 1))
    return x * cos + swapped * sin


def _attn_kernel(lam_init, has_ctx, *refs):
    if has_ctx:
        (q_ref, k_ref, v_ref, ck_ref, cv_ref, cq_ref, sq_ref, ckk_ref, skk_ref,
         qg_ref, kg_ref, lp_ref, sg_ref, o_ref) = refs
    else:
        q_ref, k_ref, v_ref, qg_ref, kg_ref, lp_ref, sg_ref, o_ref, kn_ref = refs
    q = _subhead_norm(q_ref[...], qg_ref[...])
    k = _subhead_norm(k_ref[...], kg_ref[...])
    v = v_ref[...]
    if has_ctx:
        q = _rope(q, cq_ref[...], sq_ref[...])
        k = _rope(k, ckk_ref[...], skk_ref[...])
        k = jnp.concatenate([ck_ref[...], k], axis=0)
        v = jnp.concatenate([cv_ref[...], v], axis=0)
    else:
        kn_ref[...] = k
    lp = lp_ref[...]
    lam = (jnp.exp(jnp.sum(lp[0:1] * lp[1:2], axis=-1, keepdims=True))
           - jnp.exp(jnp.sum(lp[2:3] * lp[3:4], axis=-1, keepdims=True)) + lam_init)
    scale = HD_A ** -0.5
    probs = []
    for c in range(2):
        qc = q[:, c * HD_A:(c + 1) * HD_A].astype(BF16)
        kc = k[:, c * HD_A:(c + 1) * HD_A].astype(BF16)
        s = lax.dot_general(qc, kc, (((1,), (1,)), ((), ())), preferred_element_type=F32) * scale
        e = jnp.exp(s - jnp.max(s, axis=-1, keepdims=True))
        probs.append(e / jnp.sum(e, axis=-1, keepdims=True))
    w = probs[0] - lam * probs[1]
    o = _bdot(w, v)
    ms = jnp.mean(o * o, axis=-1, keepdims=True)
    o_ref[...] = (o * lax.rsqrt(ms + EPS) * sg_ref[...]) * (1.0 - lam_init)


def _attention_prompt(qkv, qg2, kg2, lam_p, sub_g, lam_init):
    nh = H_A
    head = 2 * HD_A
    small = [
        pl.BlockSpec((1, head), lambda b, h: (0, 0)),
        pl.BlockSpec((1, head), lambda b, h: (0, 0)),
        pl.BlockSpec((4, HD_A), lambda b, h: (0, 0)),
        pl.BlockSpec((1, head), lambda b, h: (0, 0)),
    ]
    return pl.pallas_call(
        functools.partial(_attn_kernel, lam_init, False),
        grid=(BATCH, nh),
        in_specs=[
            pl.BlockSpec((SEQ, head), lambda b, h: (b, h)),
            pl.BlockSpec((SEQ, head), lambda b, h: (b, nh + h)),
            pl.BlockSpec((SEQ, head), lambda b, h: (b, 2 * nh + h)),
        ] + small,
        out_specs=[pl.BlockSpec((SEQ, head), lambda b, h: (b, h)),
                   pl.BlockSpec((SEQ, head), lambda b, h: (b, h))],
        out_shape=[jax.ShapeDtypeStruct((T_P, W_A), F32), jax.ShapeDtypeStruct((T_P, W_A), F32)],
        compiler_params=_cparams(2),
    )(qkv, qkv, qkv, qg2, kg2, lam_p, sub_g)


def _attention_sample(qkv, cache_k, cache_v, cos, sin, qg2, kg2, lam_p, sub_g, lam_init):
    nh = H_A
    head = 2 * HD_A
    tq = ROW_TILE
    nq = DEC_SEQ // tq
    q_off = T_P // tq
    k_off = T_P // DEC_SEQ
    small = [
        pl.BlockSpec((1, head), lambda b, h, i: (0, 0)),
        pl.BlockSpec((1, head), lambda b, h, i: (0, 0)),
        pl.BlockSpec((4, HD_A), lambda b, h, i: (0, 0)),
        pl.BlockSpec((1, head), lambda b, h, i: (0, 0)),
    ]
    return pl.pallas_call(
        functools.partial(_attn_kernel, lam_init, True),
        grid=(DEC_BATCH, nh, nq),
        in_specs=[
            pl.BlockSpec((tq, head), lambda b, h, i: (q_off + b * nq + i, h)),
            pl.BlockSpec((DEC_SEQ, head), lambda b, h, i: (k_off + b, nh + h)),
            pl.BlockSpec((DEC_SEQ, head), lambda b, h, i: (k_off + b, 2 * nh + h)),
            pl.BlockSpec((None, PAST_LEN, head), lambda b, h, i: (b, 0, h)),
            pl.BlockSpec((None, PAST_LEN, head), lambda b, h, i: (b, 0, h)),
            pl.BlockSpec((tq, head), lambda b, h, i: (i, 0)),
            pl.BlockSpec((tq, head), lambda b, h, i: (i, 0)),
            pl.BlockSpec((DEC_SEQ, head), lambda b, h, i: (0, 0)),
            pl.BlockSpec((DEC_SEQ, head), lambda b, h, i: (0, 0)),
        ] + small,
        out_specs=pl.BlockSpec((tq, head), lambda b, h, i: (b * nq + i, h)),
        out_shape=jax.ShapeDtypeStruct((T_S, W_A), F32),
        compiler_params=_cparams(3),
    )(qkv, qkv, qkv, cache_k, cache_v, cos, sin, cos, sin, qg2, kg2, lam_p, sub_g)


def _rope_tables():
    half = HD_A // 2
    nf = half // 2
    inv = ROPE_BASE ** (-np.arange(nf, dtype=np.float32) / nf)
    pos = np.arange(DEC_SEQ)
    row = (pos // GRID_W).astype(np.float32)
    col = (pos % GRID_W).astype(np.float32)
    ang_r = (row[:, None] * inv).astype(np.float32)
    ang_c = (col[:, None] * inv).astype(np.float32)
    ang = np.concatenate([ang_r, ang_r, ang_c, ang_c], axis=1)
    sign = np.concatenate([-np.ones(nf), np.ones(nf), -np.ones(nf), np.ones(nf)]).astype(np.float32)
    cos = np.cos(ang.astype(np.float64)).astype(np.float32)
    sin = (np.sin(ang.astype(np.float64)) * sign).astype(np.float32)
    return jnp.asarray(np.tile(cos, (1, 2))), jnp.asarray(np.tile(sin, (1, 2)))


def _mlstm_kernel(seq, has_ctx, *refs):
    if has_ctx:
        (q_ref, k_ref, cwq_ref, cwk_ref, cbq_ref, cbk_ref, v_ref, mo_ref, g_ref, gt_ref,
         gb_ref, gbt_ref, hn_ref, c0_ref, n0_ref, m0_ref, o_ref,
         qs_ref, ks_ref, hf_ref, hb_ref, cs_ref, gl_ref, gtl_ref) = refs
    else:
        (q_ref, k_ref, cwq_ref, cwk_ref, cbq_ref, cbk_ref, v_ref, mo_ref, g_ref, gt_ref,
         gb_ref, gbt_ref, hn_ref, o_ref, c_out_ref, n_out_ref, m_out_ref,
         qs_ref, ks_ref, hf_ref, hb_ref, cs_ref, gl_ref, gtl_ref) = refs
    nc = seq // CHUNK
    qs_ref[...] = _silu(_dwconv3(q_ref[...], cwq_ref[...], cbq_ref[...])) * (HD_B ** -0.5)
    ks_ref[...] = _silu(_dwconv3(k_ref[...], cwk_ref[...], cbk_ref[...]))
    gate = g_ref[...] + gb_ref[...]
    col = lax.broadcasted_iota(jnp.int32, gate.shape, 1)
    gl_ref[...] = jnp.where(col % (2 * H_B) >= H_B, _log_sigmoid(gate), gate)
    gate_t = gt_ref[...] + gbt_ref[...]
    row = lax.broadcasted_iota(jnp.int32, gate_t.shape, 1)
    gtl_ref[...] = jnp.where(row % (2 * H_B) >= H_B, _log_sigmoid(gate_t), gate_t)

    t_idx = lax.broadcasted_iota(jnp.int32, (CHUNK, CHUNK), 0)
    s_idx = lax.broadcasted_iota(jnp.int32, (CHUNK, CHUNK), 1)
    chains = [(d, h) for d in range(2) for h in range(H_B)]

    def chunk_step(d, h, c, cm, nm, mm):
        i_idx = d * 2 * H_B + h
        f_idx = i_idx + H_B
        mask = (s_idx <= t_idx) if d == 0 else (s_idx >= t_idx)
        mask_t = (t_idx <= s_idx) if d == 0 else (t_idx >= s_idx)
        r0 = pl.multiple_of(c * CHUNK, CHUNK)
        cols = slice(h * HD_B, (h + 1) * HD_B)
        qt = qs_ref[pl.ds(r0, CHUNK), cols]
        kt = ks_ref[pl.ds(r0, CHUNK), cols]
        vt = v_ref[pl.ds(r0, CHUNK), cols]
        gc = gl_ref[pl.ds(r0, CHUNK), :]
        i_col = gc[:, i_idx:i_idx + 1]
        lf_col = gc[:, f_idx:f_idx + 1]
        i_row = gtl_ref[c, i_idx:i_idx + 1, :]
        lf_row = gtl_ref[c, f_idx:f_idx + 1, :]
        b_col = jnp.sum(jnp.where(mask, lf_row, 0.0), axis=-1, keepdims=True)
        b_row = jnp.sum(jnp.where(mask_t, lf_col, 0.0), axis=0, keepdims=True)
        b_last = jnp.sum(lf_row, axis=-1, keepdims=True)
        dmat = jnp.where(mask, b_col - b_row + i_row, NEG)
        inter = b_col + mm
        m_t = jnp.maximum(inter, jnp.max(dmat, axis=-1, keepdims=True))
        qk = lax.dot_general(qt.astype(BF16), kt.astype(BF16), (((1,), (1,)), ((), ())),
                             preferred_element_type=F32)
        s = qk * jnp.exp(dmat - m_t)
        w_inter = jnp.exp(inter - m_t)
        cq = lax.dot_general(qt.astype(BF16), cm.astype(BF16), (((1,), (1,)), ((), ())),
                             preferred_element_type=F32)
        num = _bdot(s, vt) + w_inter * cq
        nq = jnp.sum(s, axis=-1, keepdims=True) + w_inter * jnp.sum(qt * nm, axis=-1, keepdims=True)
        hdir_ref = hf_ref if d == 0 else hb_ref
        hdir_ref[pl.ds(r0, CHUNK), cols] = num / jnp.maximum(jnp.abs(nq), jnp.exp(-m_t))
        g_colv = b_last - b_col + i_col
        g_rowv = b_last - b_row + i_row
        m_new = jnp.maximum(b_last + mm, jnp.max(g_rowv, axis=-1, keepdims=True))
        w_c = jnp.exp(b_last + mm - m_new)
        w_k = jnp.exp(g_colv - m_new)
        vw = (vt * w_k).astype(BF16)
        c_new = w_c * cm + lax.dot_general(vw, kt.astype(BF16), (((0,), (0,)), ((), ())),
                                           preferred_element_type=F32)
        n_new = w_c * nm + jnp.sum(kt * w_k, axis=0, keepdims=True)
        return c_new, n_new, m_new

    for n, (d, h) in enumerate(chains):
        if has_ctx:
            cs_ref[n] = c0_ref[d, h]
        else:
            cs_ref[n] = jnp.zeros((HD_B, HD_B), F32)

    def body(j, carry):
        new = []
        for n, (d, h) in enumerate(chains):
            nm, mm = carry[n]
            c = j if d == 0 else nc - 1 - j
            c_new, n_new, m_new = chunk_step(d, h, c, cs_ref[n], nm, mm)
            cs_ref[n] = c_new
            new.append((n_new, m_new))
        return tuple(new)

    if has_ctx:
        init = tuple((n0_ref[d, h], m0_ref[d:d + 1, h:h + 1]) for d, h in chains)
    else:
        init = tuple((jnp.zeros((1, HD_B), F32), jnp.zeros((1, 1), F32)) for _ in chains)
    final = lax.fori_loop(0, nc, body, init)
    if not has_ctx:
        for n, (d, h) in enumerate(chains):
            c_out_ref[d, h] = cs_ref[n]
            n_out_ref[d, h] = final[n][0]
            m_out_ref[d, h] = jnp.broadcast_to(final[n][1], (1, HD_B))

    for h in range(H_B):
        cols = slice(h * HD_B, (h + 1) * HD_B)
        hh = hf_ref[:, cols] + hb_ref[:, cols]
        ms = jnp.mean(hh * hh, axis=-1, keepdims=True)
        o_ref[:, cols] = (hh * lax.rsqrt(ms + EPS) * hn_ref[:, cols]) * _sigmoid(mo_ref[:, cols])


def _mlstm(mqk, mv, mo, mg, gt, conv_w, conv_b, gate_b, hn_g, *, seq, nbatch, row_off, ctx=None):
    nh = H_B
    has_ctx = ctx is not None
    blk = lambda col: pl.BlockSpec((seq, W_B), lambda b, col=col: (row_off + b, col))
    in_specs = [
        blk(0), blk(1),
        pl.BlockSpec((3, W_B), lambda b: (0, 0)),
        pl.BlockSpec((3, W_B), lambda b: (0, 1)),
        pl.BlockSpec((1, W_B), lambda b: (0, 0)),
        pl.BlockSpec((1, W_B), lambda b: (0, 1)),
        blk(0), blk(0),
        pl.BlockSpec((seq, 4 * nh), lambda b: (row_off + b, 0)),
        pl.BlockSpec((None, seq // CHUNK, 4 * nh, CHUNK), lambda b: (b, 0, 0, 0)),
        pl.BlockSpec((1, 4 * nh), lambda b: (0, 0)),
        pl.BlockSpec((4 * nh, 1), lambda b: (0, 0)),
        pl.BlockSpec((1, W_B), lambda b: (0, 0)),
    ]
    args = [mqk, mqk, conv_w, conv_w, conv_b, conv_b, mv, mo, mg, gt,
            gate_b.reshape(1, 4 * nh), gate_b.reshape(4 * nh, 1), hn_g.reshape(1, W_B)]
    o_spec = pl.BlockSpec((seq, W_B), lambda b: (b, 0))
    o_shape = jax.ShapeDtypeStruct((nbatch * seq, W_B), F32)
    state_blk = lambda rows: pl.BlockSpec((None, 2, nh, rows, HD_B), lambda b: (b, 0, 0, 0, 0))
    if has_ctx:
        c0, n0, m0 = ctx
        in_specs += [state_blk(HD_B), state_blk(1), pl.BlockSpec((None, 2, nh), lambda b: (b, 0, 0))]
        args += [c0, n0.reshape(nbatch, 2, nh, 1, HD_B), m0]
        out_specs, out_shape = o_spec, o_shape
    else:
        out_specs = [o_spec, state_blk(HD_B), state_blk(1), state_blk(1)]
        out_shape = [
            o_shape,
            jax.ShapeDtypeStruct((nbatch, 2, nh, HD_B, HD_B), F32),
            jax.ShapeDtypeStruct((nbatch, 2, nh, 1, HD_B), F32),
            jax.ShapeDtypeStruct((nbatch, 2, nh, 1, HD_B), F32),
        ]
    return pl.pallas_call(
        functools.partial(_mlstm_kernel, seq, has_ctx),
        grid=(nbatch,),
        in_specs=in_specs,
        out_specs=out_specs,
        out_shape=out_shape,
        scratch_shapes=[pltpu.VMEM((seq, W_B), F32)] * 4 + [
            pltpu.VMEM((2 * nh, HD_B, HD_B), F32),
            pltpu.VMEM((seq, 4 * nh), F32),
            pltpu.VMEM((seq // CHUNK, 4 * nh, CHUNK), F32),
        ],
        compiler_params=_cparams(1),
    )(*args)


def _dft_mats(L):
    f = np.arange(L)[:, None]
    j = np.arange(L)[None, :]
    ang = 2.0 * np.pi * ((f * j) % (2 * L)) / (2 * L)
    cm = np.cos(ang)
    sm = np.sin(ang)
    alt = (1.0 - 2.0 * (np.arange(L) % 2))
    fwd_b = -sm
    fwd_b[0, :] = alt
    fwd = np.concatenate([cm, fwd_b], axis=0)
    wgt = np.where(np.arange(L) == 0, 1.0, 2.0)[None, :]
    inv_a = cm.T * wgt
    inv_b = -2.0 * sm.T
    inv_b[:, 0] = alt
    inv = np.concatenate([inv_a, inv_b], axis=1) / (2 * L)
    return jnp.asarray(fwd.astype(np.float32)), jnp.asarray(inv.astype(np.float32))


def _hyena_feats(L):
    t = np.linspace(0.0, 1.0, L, dtype=np.float32)
    wpos = (2.0 * math.pi * np.arange(L, dtype=np.float32) / L).astype(np.float32)
    fb = np.linspace(1e-4, HY_BANDS - 1, HY_BANDS, dtype=np.float32)
    z = (wpos[:, None] * fb).astype(np.float32)
    feats = np.concatenate([t[:, None], np.cos(z), -np.sin(z)], axis=-1).astype(np.float32)
    deltas = np.abs(np.linspace(math.log(HY_TARGET) / HY_SLOW_PCT, math.log(HY_TARGET) / HY_FAST_PCT,
                                D_MODEL, dtype=np.float32))
    decay = np.exp(-t[:, None] * deltas).astype(np.float32)
    return jnp.asarray(feats), jnp.asarray(decay)


def _filter_kernel(L, feats_ref, w1_ref, b1_ref, fr1_ref, w2_ref, b2_ref, fr2_ref, w3f_ref, w3b_ref,
                   decay_ref, fwd_ref, o_ref, hdn_ref):
    @pl.when((pl.program_id(0) == 0) & (pl.program_id(1) == 0))
    def _():
        h1 = jnp.sin(fr1_ref[...] * (jnp.dot(feats_ref[...], w1_ref[...], precision=HIGHEST,
                                             preferred_element_type=F32) + b1_ref[...]))
        hdn_ref[...] = jnp.sin(fr2_ref[...] * (jnp.dot(h1, w2_ref[...], precision=HIGHEST,
                                                       preferred_element_type=F32) + b2_ref[...]))

    hdn = hdn_ref[...]
    decay = decay_ref[...]
    f_fwd = jnp.dot(hdn, w3f_ref[...], precision=HIGHEST, preferred_element_type=F32) * decay
    f_bwd = jnp.dot(hdn, w3b_ref[...], precision=HIGHEST, preferred_element_type=F32) * decay
    row = lax.broadcasted_iota(jnp.int32, f_bwd.shape, 0)
    f_bwd = jnp.where(row == 0, 0.0, f_bwd)
    fwd = fwd_ref[...]
    p = _bdot(fwd, f_fwd)
    q = _bdot(fwd, f_bwd)
    first = row == 0
    o_ref[0:L, :] = p[0:L] + q[0:L]
    o_ref[L:2 * L, :] = p[L:2 * L] + jnp.where(first, q[L:2 * L], -q[L:2 * L])


def _hyena_filter_spectrum(L, fwd_bf, w1, b1, fr1, w2, b2, fr2, w3):
    feats, decay = _hyena_feats(L)
    td = 512
    nd = D_MODEL // td
    emb = feats.shape[1]
    vec = lambda a: a.reshape(1, HY_FH)
    full = lambda shape: pl.BlockSpec(shape, lambda o, j: (0, 0))
    return pl.pallas_call(
        functools.partial(_filter_kernel, L),
        grid=(HY_ORDER, nd),
        in_specs=[
            full((L, emb)), full((emb, HY_FH)), full((1, HY_FH)), full((1, HY_FH)),
            full((HY_FH, HY_FH)), full((1, HY_FH)), full((1, HY_FH)),
            pl.BlockSpec((HY_FH, td), lambda o, j: (0, o * 2 * nd + j)),
            pl.BlockSpec((HY_FH, td), lambda o, j: (0, o * 2 * nd + nd + j)),
            pl.BlockSpec((L, td), lambda o, j: (0, j)),
            full((2 * L, L)),
        ],
        out_specs=pl.BlockSpec((2 * L, td), lambda o, j: (0, o * nd + j)),
        out_shape=jax.ShapeDtypeStruct((2 * L, HY_ORDER * D_MODEL), F32),
        scratch_shapes=[pltpu.VMEM((L, HY_FH), F32)],
        compiler_params=_cparams(2),
    )(feats, w1, vec(b1), vec(fr1), w2, vec(b2), vec(fr2), w3, w3, decay, fwd_bf)


def _spectral_conv(u, fwd, inv, kspec, L):
    uf = jnp.dot(fwd, u.astype(BF16), preferred_element_type=F32)
    ua, ub = uf[0:L], uf[L:2 * L]
    ka, kb = kspec[0:L], kspec[L:2 * L]
    first = lax.broadcasted_iota(jnp.int32, ua.shape, 0) == 0
    ya = ua * ka - jnp.where(first, 0.0, ub * kb)
    yb = jnp.where(first, ub * kb, ua * kb + ub * ka)
    y = jnp.concatenate([ya, yb], axis=0).astype(BF16)
    return jnp.dot(inv, y, preferred_element_type=F32)


def _hyena_kernel(L, zv_ref, z1_ref, z2_ref, cwv_ref, cw1_ref, cw2_ref, cbv_ref, cb1_ref, cb2_ref,
                  fwd_ref, inv_ref, k0_ref, k1_ref, bias0_ref, bias1_ref, o_ref):
    fwd = fwd_ref[...]
    inv = inv_ref[...]
    v = _dwconv3(zv_ref[...], cwv_ref[...], cbv_ref[...])
    x1 = _dwconv3(z1_ref[...], cw1_ref[...], cb1_ref[...])
    x2 = _dwconv3(z2_ref[...], cw2_ref[...], cb2_ref[...])
    z = x1 * (_spectral_conv(v, fwd, inv, k0_ref[...], L) + v * bias0_ref[...])
    o_ref[...] = x2 * (_spectral_conv(z, fwd, inv, k1_ref[...], L) + z * bias1_ref[...])


def _hyena_core(zproj, conv_w, conv_b, fwd_bf, inv_bf, kspec, bias, *, seq, nbatch, row_off, td):
    nd = D_MODEL // td
    zblk = lambda part: pl.BlockSpec((seq, td), lambda b, j, part=part: (row_off + b, part * nd + j))
    cwblk = lambda part: pl.BlockSpec((3, td), lambda b, j, part=part: (0, part * nd + j))
    cbblk = lambda part: pl.BlockSpec((1, td), lambda b, j, part=part: (0, part * nd + j))
    return pl.pallas_call(
        functools.partial(_hyena_kernel, seq),
        grid=(nbatch, nd),
        in_specs=[
            zblk(0), zblk(1), zblk(2), cwblk(0), cwblk(1), cwblk(2), cbblk(0), cbblk(1), cbblk(2),
            pl.BlockSpec((2 * seq, seq), lambda b, j: (0, 0), pipeline_mode=pl.Buffered(1)),
            pl.BlockSpec((seq, 2 * seq), lambda b, j: (0, 0), pipeline_mode=pl.Buffered(1)),
            pl.BlockSpec((2 * seq, td), lambda b, j: (0, j)),
            pl.BlockSpec((2 * seq, td), lambda b, j: (0, nd + j)),
            pl.BlockSpec((None, 1, td), lambda b, j: (0, 0, j)),
            pl.BlockSpec((None, 1, td), lambda b, j: (1, 0, j)),
        ],
        out_specs=pl.BlockSpec((seq, td), lambda b, j: (b, j)),
        out_shape=jax.ShapeDtypeStruct((nbatch * seq, D_MODEL), F32),
        compiler_params=_cparams(2),
    )(zproj, zproj, zproj, conv_w, conv_w, conv_w, conv_b, conv_b, conv_b,
      fwd_bf, inv_bf, kspec, kspec, bias.reshape(HY_ORDER, 1, D_MODEL), bias.reshape(HY_ORDER, 1, D_MODEL))


def _router_kernel(x_ref, ada_ref, g_ref, rw_ref, rb_ref, h_ref, idx_ref, wt_ref):
    i = pl.program_id(0)
    row = _cond_row(i)
    h = _modulate(x_ref[...], g_ref[...], _ada_chunk(ada_ref, row, 3), _ada_chunk(ada_ref, row, 4))
    h_ref[...] = h.astype(BF16)
    logits = jnp.dot(h, rw_ref[...], precision=HIGHEST, preferred_element_type=F32) + rb_ref[...]
    lane = lax.broadcasted_iota(jnp.int32, logits.shape, 1)
    slot = lax.broadcasted_iota(jnp.int32, (logits.shape[0], TOP_K), 1)
    vals = jnp.zeros((logits.shape[0], TOP_K), F32)
    idxs = jnp.zeros((logits.shape[0], TOP_K), jnp.int32)
    cur = logits
    for k in range(TOP_K):
        m = jnp.max(cur, axis=-1, keepdims=True)
        a = jnp.min(jnp.where(cur == m, lane, N_EXPERTS), axis=-1, keepdims=True)
        vals = jnp.where(slot == k, m, vals)
        idxs = jnp.where(slot == k, a, idxs)
        cur = jnp.where(lane == a, -jnp.inf, cur)
    e = jnp.exp(vals - vals[:, 0:1])
    wt_ref[...] = e / jnp.sum(e, axis=-1, keepdims=True)
    idx_ref[...] = idxs


def _router(y, ada_l, g, router_w, router_b):
    return pl.pallas_call(
        _router_kernel,
        grid=(N_ROW_TILES,),
        in_specs=[
            pl.BlockSpec((ROW_TILE, D_MODEL), lambda i: (i, 0)),
            pl.BlockSpec((COND_ROWS, ADA_CHUNKS * D_MODEL), lambda i: (0, 0)),
            pl.BlockSpec((1, D_MODEL), lambda i: (0, 0)),
            pl.BlockSpec((D_MODEL, N_EXPERTS), lambda i: (0, 0)),
            pl.BlockSpec((1, N_EXPERTS), lambda i: (0, 0)),
        ],
        out_specs=[
            pl.BlockSpec((ROW_TILE, D_MODEL), lambda i: (i, 0)),
            pl.BlockSpec((ROW_TILE, TOP_K), lambda i: (i, 0)),
            pl.BlockSpec((ROW_TILE, TOP_K), lambda i: (i, 0)),
        ],
        out_shape=[
            jax.ShapeDtypeStruct((T_ALL, D_MODEL), BF16),
            jax.ShapeDtypeStruct((T_ALL, TOP_K), jnp.int32),
            jax.ShapeDtypeStruct((T_ALL, TOP_K), F32),
        ],
        compiler_params=_cparams(1),
    )(y, ada_l, g.reshape(1, D_MODEL), router_w, router_b.reshape(1, N_EXPERTS))


def _deinterleave_matrix():
    s = np.zeros((256, 256), np.float32)
    j = np.arange(128)
    s[2 * j, j] = 1.0
    s[2 * j + 1, 128 + j] = 1.0
    return jnp.asarray(s)


def _weight_copies(layer, e, w1_hbm, w2_hbm, w1s_ref, w2s_ref, sem):
    copies = []
    r1 = D_MODEL // W1_DMA_CHUNKS
    for c in range(W1_DMA_CHUNKS):
        copies.append(pltpu.make_async_copy(w1_hbm.at[layer, e, pl.ds(c * r1, r1)],
                                            w1s_ref.at[pl.ds(c * r1, r1)], sem.at[c]))
    r2 = D_FF // W2_DMA_CHUNKS
    for c in range(W2_DMA_CHUNKS):
        copies.append(pltpu.make_async_copy(w2_hbm.at[layer, e, pl.ds(c * r2, r2)],
                                            w2s_ref.at[pl.ds(c * r2, r2)], sem.at[W1_DMA_CHUNKS + c]))
    return copies


def _expert_kernel(layer, te_ref, tf_ref, ne_ref, nu_ref, src_ref, x_ref, b1_ref, b2_ref, wt_ref, s_ref,
                   w1_hbm, w2_hbm, o_hbm, w1s_ref, w2s_ref, w1p_ref, w2p_ref, acc_ref, out_ref, wsem, osem):
    i = pl.program_id(0)
    half = 128
    copies = functools.partial(_weight_copies, layer, w1_hbm=w1_hbm, w2_hbm=w2_hbm,
                               w1s_ref=w1s_ref, w2s_ref=w2s_ref, sem=wsem)

    @pl.when(i == 0)
    def _():
        acc_ref[...] = jnp.zeros_like(acc_ref)
        out_ref[...] = jnp.zeros_like(out_ref)
        for cp in copies(te_ref[0]):
            cp.start()

    @pl.when(tf_ref[i] == 1)
    def _():
        for cp in copies(te_ref[i]):
            cp.wait()
        s = s_ref[...].astype(BF16)
        for c in range(2 * D_FF // 256):
            blk = jnp.dot(w1s_ref[:, c * 256:(c + 1) * 256].astype(BF16), s, preferred_element_type=F32)
            w1p_ref[:, c * half:(c + 1) * half] = blk[:, :half].astype(BF16)
            w1p_ref[:, D_FF + c * half:D_FF + (c + 1) * half] = blk[:, half:].astype(BF16)
        w2p_ref[...] = w2s_ref[...].astype(BF16)

        @pl.when(ne_ref[i] >= 0)
        def _():
            for cp in copies(ne_ref[i]):
                cp.start()

    @pl.when(i <= nu_ref[0])
    def _():
        base = i * MOE_TILE
        prev = (i + 1) % 2
        for r0 in range(0, MOE_TILE, SCATTER_GROUP):
            toks = [src_ref[base + r0 + g] for g in range(SCATTER_GROUP)]
            cur = [acc_ref[pl.ds(toks[g], 1), :] for g in range(SCATTER_GROUP)]
            add = [out_ref[prev, r0 + g:r0 + g + 1, :] for g in range(SCATTER_GROUP)]
            for g in range(SCATTER_GROUP):
                acc_ref[pl.ds(toks[g], 1), :] = cur[g] + add[g]
        a = jnp.dot(x_ref[...], w1p_ref[...], preferred_element_type=F32) + b1_ref[...]
        glu = jnp.minimum(a[:, :D_FF], SWIGLU_LIMIT)
        lin = jnp.clip(a[:, D_FF:], -SWIGLU_LIMIT, SWIGLU_LIMIT)
        hid = glu * _sigmoid(SWIGLU_ALPHA * glu) * (lin + 1.0)
        out = jnp.dot(hid.astype(BF16), w2p_ref[...], preferred_element_type=F32) + b2_ref[...]
        out_ref[i % 2] = out * wt_ref[...]

    @pl.when(i == pl.num_programs(0) - 1)
    def _():
        cp = pltpu.make_async_copy(acc_ref.at[pl.ds(0, T_ALL)], o_hbm, osem)
        cp.start()
        cp.wait()


def _experts(layer, x_sorted, w_sorted, plan, w1, b1p, w2, b2):
    tile_expert, tile_first, next_expert, n_used, src = plan
    grid_spec = pltpu.PrefetchScalarGridSpec(
        num_scalar_prefetch=5,
        grid=(MOE_TILES,),
        in_specs=[
            pl.BlockSpec((MOE_TILE, D_MODEL), lambda i, te, *_: (i, 0)),
            pl.BlockSpec((None, None, 1, 2 * D_FF), lambda i, te, *_: (layer, te[i], 0, 0)),
            pl.BlockSpec((None, None, 1, D_MODEL), lambda i, te, *_: (layer, te[i], 0, 0)),
            pl.BlockSpec((MOE_TILE, 1), lambda i, te, *_: (i, 0)),
            pl.BlockSpec((256, 256), lambda i, te, *_: (0, 0)),
            pl.BlockSpec(memory_space=pl.ANY),
            pl.BlockSpec(memory_space=pl.ANY),
        ],
        out_specs=pl.BlockSpec(memory_space=pl.ANY),
        scratch_shapes=[
            pltpu.VMEM((D_MODEL, 2 * D_FF), F32),
            pltpu.VMEM((D_FF, D_MODEL), F32),
            pltpu.VMEM((D_MODEL, 2 * D_FF), BF16),
            pltpu.VMEM((D_FF, D_MODEL), BF16),
            pltpu.VMEM((ACC_ROWS, D_MODEL), F32),
            pltpu.VMEM((2, MOE_TILE, D_MODEL), F32),
            pltpu.SemaphoreType.DMA((W1_DMA_CHUNKS + W2_DMA_CHUNKS,)),
            pltpu.SemaphoreType.DMA(()),
        ],
    )
    return pl.pallas_call(
        functools.partial(_expert_kernel, layer),
        grid_spec=grid_spec,
        out_shape=jax.ShapeDtypeStruct((T_ALL, D_MODEL), F32),
        compiler_params=_cparams(1),
    )(tile_expert, tile_first, next_expert, n_used, src, x_sorted, b1p, b2, w_sorted,
      _deinterleave_matrix(), w1, w2)


def _combine_kernel(y_ref, a_ref, ada_ref, o_ref):
    gate = _ada_chunk(ada_ref, _cond_row(pl.program_id(0)), 5)
    o_ref[...] = y_ref[...] + gate * a_ref[...]


def _combine(y, acc, ada_l):
    return pl.pallas_call(
        _combine_kernel,
        grid=(N_ROW_TILES,),
        in_specs=[
            pl.BlockSpec((ROW_TILE, D_MODEL), lambda i: (i, 0)),
            pl.BlockSpec((ROW_TILE, D_MODEL), lambda i: (i, 0)),
            pl.BlockSpec((COND_ROWS, ADA_CHUNKS * D_MODEL), lambda i: (0, 0)),
        ],
        out_specs=pl.BlockSpec((ROW_TILE, D_MODEL), lambda i: (i, 0)),
        out_shape=jax.ShapeDtypeStruct((T_ALL, D_MODEL), F32),
        compiler_params=_cparams(1),
    )(y, acc, ada_l)


def _routing_plan(idx, wts):
    eid = idx.reshape(-1)
    order = jnp.argsort(eid, stable=True).astype(jnp.int32)
    experts = jnp.arange(N_EXPERTS, dtype=jnp.int32)
    counts = jnp.sum(eid[:, None] == experts[None, :], axis=0).astype(jnp.int32)
    ntiles = (counts + MOE_TILE - 1) // MOE_TILE
    tile_end = jnp.cumsum(ntiles).astype(jnp.int32)
    tile_begin = tile_end - ntiles
    cstarts = (jnp.cumsum(counts) - counts).astype(jnp.int32)
    n_used = tile_end[-1]
    tile = jnp.arange(MOE_TILES, dtype=jnp.int32)
    te = jnp.minimum(jnp.sum(tile[:, None] >= tile_end[None, :], axis=1), N_EXPERTS - 1).astype(jnp.int32)
    used = tile < n_used
    prev = jnp.concatenate([jnp.full((1,), -1, jnp.int32), te[:-1]])
    first = (te != prev) & used
    nxt = tile_end[te]
    next_expert = jnp.where(first & (nxt < n_used), te[jnp.minimum(nxt, MOE_TILES - 1)], -1).astype(jnp.int32)
    off = (tile - tile_begin[te])[:, None] * MOE_TILE + jnp.arange(MOE_TILE, dtype=jnp.int32)[None, :]
    valid = (off < counts[te][:, None]) & used[:, None]
    assign = order[jnp.clip(cstarts[te][:, None] + off, 0, N_ASSIGN - 1)]
    token = assign // TOP_K
    src = jnp.where(valid, token, SPARE_ROW).reshape(MOE_ROWS).astype(jnp.int32)
    src = jnp.concatenate([jnp.full((MOE_TILE,), SPARE_ROW, jnp.int32), src])
    gather_row = jnp.where(valid, token, 0).reshape(MOE_ROWS)
    w_sorted = jnp.where(valid, wts.reshape(-1)[assign], 0.0).reshape(MOE_ROWS, 1)
    plan = (te, first.astype(jnp.int32), next_expert, n_used.reshape(1), src)
    return plan, gather_row, w_sorted


def _moe(layer, y, ada_l, g, router_w, router_b, w1, b1p, w2, b2):
    h, idx, wts = _router(y, ada_l, g, router_w, router_b)
    plan, gather_row, w_sorted = _routing_plan(idx, wts)
    x_sorted = jnp.take(h, gather_row, axis=0, mode="clip")
    acc = _experts(layer, x_sorted, w_sorted, plan, w1, b1p, w2, b2)
    return _combine(y, acc, ada_l)


def _chunked_gates_t(mg_stream, nbatch, seq):
    return mg_stream.reshape(nbatch, seq // CHUNK, CHUNK, 4 * H_B).transpose(0, 1, 3, 2)


def kernel(x_prompt, x_sample, cache_attn_k, cache_attn_v, state_mlstm_C, state_mlstm_n, state_mlstm_m, c, c_ctx, ada_w, ada_b, norm_mix_g, norm_ffn_g, ab_w_in, ab_w_out, da_qnorm_g, da_knorm_g, da_lambda, da_subnorm_g, ml_conv_w, ml_conv_b, ml_gate_b, ml_headnorm_g, hy_w_in, hy_w_out, hy_conv_w, hy_conv_b, hy_f_w1, hy_f_b1, hy_f_freq1, hy_f_w2, hy_f_b2, hy_f_freq2, hy_f_w3, hy_bias, router_w, router_b, moe_w1, moe_b1, moe_w2, moe_b2):
    y = jnp.concatenate([x_prompt.reshape(T_P, D_MODEL), x_sample.reshape(T_S, D_MODEL)], axis=0)
    cond = jnp.concatenate([c_ctx[None, :], c, jnp.zeros((COND_ROWS - 1 - DEC_BATCH, D_MODEL), F32)], axis=0)
    ada = _ada_table(cond, ada_w, ada_b)
    b1p = jnp.concatenate([moe_b1[..., 0::2], moe_b1[..., 1::2]], axis=-1).reshape(DEPTH, N_EXPERTS, 1, 2 * D_FF)
    b2r = moe_b2.reshape(DEPTH, N_EXPERTS, 1, D_MODEL)
    new_k, new_v, new_c, new_n, new_m = [], [], [], [], []
    for layer in range(DEPTH):
        ada_l = ada[layer]
        if layer % 2 == 0:
            e = layer // 2
            lam_init = 0.8 - 0.6 * math.exp(-0.3 * layer)
            qkv, mqk, mv, mo, mg = _modulated_proj(
                y, ada_l, norm_mix_g[layer], ab_w_in[e], (3 * W_A, 2 * W_B, W_B, W_B, 4 * H_B))
            qg2 = jnp.tile(da_qnorm_g[e], 2).reshape(1, 2 * HD_A)
            kg2 = jnp.tile(da_knorm_g[e], 2).reshape(1, 2 * HD_A)
            sub_g = da_subnorm_g[e].reshape(1, 2 * HD_A)
            oa_p, k_norm = _attention_prompt(qkv, qg2, kg2, da_lambda[e], sub_g, lam_init)
            cos, sin = _rope_tables()
            oa_s = _attention_sample(
                qkv, cache_attn_k[:, e].reshape(DEC_BATCH, PAST_LEN, W_A),
                cache_attn_v[:, e].reshape(DEC_BATCH, PAST_LEN, W_A), cos, sin,
                qg2, kg2, da_lambda[e], sub_g, lam_init)
            gt_p = _chunked_gates_t(mg[:T_P], BATCH, SEQ)
            gt_s = _chunked_gates_t(mg[T_P:], DEC_BATCH, DEC_SEQ)
            ob_p, c_new, n_new, m_new = _mlstm(
                mqk, mv, mo, mg, gt_p, ml_conv_w[e], ml_conv_b[e].reshape(1, 2 * W_B), ml_gate_b[e],
                ml_headnorm_g[e], seq=SEQ, nbatch=BATCH, row_off=0)
            ob_s = _mlstm(
                mqk, mv, mo, mg, gt_s, ml_conv_w[e], ml_conv_b[e].reshape(1, 2 * W_B), ml_gate_b[e],
                ml_headnorm_g[e], seq=DEC_SEQ, nbatch=DEC_BATCH, row_off=T_P // DEC_SEQ,
                ctx=(state_mlstm_C[:, e], state_mlstm_n[:, e], state_mlstm_m[:, e]))
            o_a = jnp.concatenate([oa_p, oa_s], axis=0)
            o_b = jnp.concatenate([ob_p, ob_s], axis=0)
            y = _out_proj_residual([o_a, o_b], y, ada_l, ab_w_out[e], 2)
            new_k.append(k_norm.reshape(BATCH, SEQ, H_A, 2, HD_A))
            new_v.append(qkv[:T_P, 2 * W_A:].reshape(BATCH, SEQ, H_A, 2 * HD_A))
            new_c.append(c_new)
            new_n.append(n_new.reshape(BATCH, 2, H_B, HD_B))
            new_m.append(m_new[..., 0, 0])
        else:
            o = layer // 2
            (zproj,) = _modulated_proj(y, ada_l, norm_mix_g[layer], hy_w_in[o], (HY_PROJ,))
            cores = []
            for seq, nbatch, row_off, td in ((SEQ, BATCH, 0, 512), (DEC_SEQ, DEC_BATCH, T_P // DEC_SEQ, 256)):
                fwd, inv = _dft_mats(seq)
                fwd_bf, inv_bf = fwd.astype(BF16), inv.astype(BF16)
                kspec = _hyena_filter_spectrum(seq, fwd_bf, hy_f_w1[o], hy_f_b1[o], hy_f_freq1[o], hy_f_w2[o],
                                               hy_f_b2[o], hy_f_freq2[o], hy_f_w3[o])
                cores.append(_hyena_core(zproj, hy_conv_w[o], hy_conv_b[o].reshape(1, HY_PROJ), fwd_bf, inv_bf,
                                         kspec, hy_bias[o], seq=seq, nbatch=nbatch, row_off=row_off, td=td))
            y = _out_proj_residual([jnp.concatenate(cores, axis=0)], y, ada_l, hy_w_out[o], 2)
        y = _moe(layer, y, ada_l, norm_ffn_g[layer], router_w[layer], router_b[layer],
                 moe_w1, b1p, moe_w2, b2r)
    y_p = y[:T_P].reshape(BATCH, SEQ, D_MODEL)
    y_s = y[T_P:].reshape(DEC_BATCH, DEC_SEQ, D_MODEL)
    return (y_p, y_s, jnp.stack(new_k, axis=1), jnp.stack(new_v, axis=1), jnp.stack(new_c, axis=1),
            jnp.stack(new_n, axis=1), jnp.stack(new_m, axis=1))
```

```python
import functools
import math

import numpy as np
import jax
import jax.numpy as jnp
from jax import lax
from jax.experimental import pallas as pl
from jax.experimental.pallas import tpu as pltpu

D_MODEL = 1024
BATCH = 16
SEQ = 256
DEPTH = 2
DEC_BATCH = 2
DEC_SEQ = 1024
PAST_LEN = 256
GRID_W = 64
W_A = D_MODEL // 2
HD_A = 64
H_A = W_A // (2 * HD_A)
W_B = D_MODEL - W_A
HD_B = 128
H_B = W_B // HD_B
AB_PROJ = 3 * W_A + 4 * W_B + 4 * H_B
ROPE_BASE = 10000.0
CHUNK = 64
HY_ORDER = 2
HY_PROJ = (HY_ORDER + 1) * D_MODEL
HY_BANDS = 8
HY_FH = 64
HY_TARGET = 1e-2
HY_FAST_PCT = 0.3
HY_SLOW_PCT = 1.5
N_EXPERTS = 32
TOP_K = 4
D_FF = D_MODEL
SWIGLU_ALPHA = 1.702
SWIGLU_LIMIT = 7.0
ADA_CHUNKS = 6
EPS = 1e-6
NEG = -1e30
F32 = jnp.float32
BF16 = jnp.bfloat16

T_P = BATCH * SEQ
T_S = DEC_BATCH * DEC_SEQ
T_ALL = T_P + T_S
ROW_TILE = 256
N_ROW_TILES = T_ALL // ROW_TILE
P_TILES = T_P // ROW_TILE
S_TILES_PER_BATCH = DEC_SEQ // ROW_TILE
COND_ROWS = 8
MOE_TILE = 256
N_ASSIGN = T_ALL * TOP_K
MOE_ROWS = N_ASSIGN + N_EXPERTS * MOE_TILE
MOE_TILES = MOE_ROWS // MOE_TILE
SPARE_ROW = T_ALL
ACC_ROWS = T_ALL + 8
SCATTER_GROUP = 8
W1_DMA_CHUNKS = 8
W2_DMA_CHUNKS = 4
VMEM_LIMIT = 56 * 1024 * 1024
HIGHEST = lax.Precision.HIGHEST


def _cparams(n_axes):
    return pltpu.CompilerParams(dimension_semantics=("arbitrary",) * n_axes,
                                vmem_limit_bytes=VMEM_LIMIT)


def _bdot(a, b):
    return jnp.dot(a.astype(BF16), b.astype(BF16), preferred_element_type=F32)


def _cond_row(i):
    return jnp.where(i < P_TILES, 0, 1 + (i - P_TILES) // S_TILES_PER_BATCH)


def _ada_chunk(ada_ref, row, j):
    return ada_ref[pl.ds(row, 1), j * D_MODEL:(j + 1) * D_MODEL]


def _modulate(x, g, shift, scale):
    ms = jnp.mean(x * x, axis=-1, keepdims=True)
    return (x * lax.rsqrt(ms + EPS) * g) * (1.0 + scale) + shift


def _sigmoid(x):
    return 1.0 / (1.0 + jnp.exp(-x))


def _silu(x):
    return x * _sigmoid(x)


def _log_sigmoid(x):
    return jnp.minimum(x, 0.0) - jnp.log(1.0 + jnp.exp(-jnp.abs(x)))


def _dwconv3(x, w, b):
    n = x.shape[0]
    row = lax.broadcasted_iota(jnp.int32, x.shape, 0)
    prev = jnp.where(row == 0, 0.0, pltpu.roll(x, 1, 0))
    nxt = jnp.where(row == n - 1, 0.0, pltpu.roll(x, n - 1, 0))
    return prev * w[0:1] + x * w[1:2] + nxt * w[2:3] + b


def _ada_kernel(cond_ref, w_ref, b_ref, o_ref):
    c = _silu(cond_ref[...])
    o_ref[...] = _bdot(c, w_ref[...]) + b_ref[...]


def _ada_table(cond, ada_w, ada_b):
    tn = 1536
    return pl.pallas_call(
        _ada_kernel,
        grid=(DEPTH, ADA_CHUNKS * D_MODEL // tn),
        in_specs=[
            pl.BlockSpec((COND_ROWS, D_MODEL), lambda l, j: (0, 0)),
            pl.BlockSpec((None, D_MODEL, tn), lambda l, j: (l, 0, j)),
            pl.BlockSpec((None, 1, tn), lambda l, j: (l, 0, j)),
        ],
        out_specs=pl.BlockSpec((None, COND_ROWS, tn), lambda l, j: (l, 0, j)),
        out_shape=jax.ShapeDtypeStruct((DEPTH, COND_ROWS, ADA_CHUNKS * D_MODEL), F32),
        compiler_params=_cparams(2),
    )(cond, ada_w, ada_b.reshape(DEPTH, 1, ADA_CHUNKS * D_MODEL))


def _proj_kernel(splits, x_ref, ada_ref, g_ref, w_ref, *rest):
    out_refs, wbf_ref = rest[:-1], rest[-1]
    i = pl.program_id(0)

    @pl.when(i == 0)
    def _():
        wbf_ref[...] = w_ref[...].astype(BF16)

    row = _cond_row(i)
    h = _modulate(x_ref[...], g_ref[...], _ada_chunk(ada_ref, row, 0), _ada_chunk(ada_ref, row, 1))
    h = h.astype(BF16)
    lo = 0
    for o_ref, width in zip(out_refs, splits):
        o_ref[...] = jnp.dot(h, wbf_ref[:, lo:lo + width], preferred_element_type=F32)
        lo += width


def _modulated_proj(y, ada_l, g, w, splits):
    n = w.shape[1]
    return pl.pallas_call(
        functools.partial(_proj_kernel, splits),
        grid=(N_ROW_TILES,),
        in_specs=[
            pl.BlockSpec((ROW_TILE, D_MODEL), lambda i: (i, 0)),
            pl.BlockSpec((COND_ROWS, ADA_CHUNKS * D_MODEL), lambda i: (0, 0)),
            pl.BlockSpec((1, D_MODEL), lambda i: (0, 0)),
            pl.BlockSpec((D_MODEL, n), lambda i: (0, 0), pipeline_mode=pl.Buffered(1)),
        ],
        out_specs=[pl.BlockSpec((ROW_TILE, s), lambda i: (i, 0)) for s in splits],
        out_shape=[jax.ShapeDtypeStruct((T_ALL, s), F32) for s in splits],
        scratch_shapes=[pltpu.VMEM((D_MODEL, n), BF16)],
        compiler_params=_cparams(1),
    )(y, ada_l, g.reshape(1, D_MODEL), w)


def _out_proj_kernel(n_in, gate_chunk, *refs):
    x_refs = refs[:2 * n_in]
    y_ref, ada_ref, w_ref, o_ref, wbf_ref = refs[2 * n_in:]
    i = pl.program_id(0)

    @pl.when(i == 0)
    def _():
        wbf_ref[...] = w_ref[...].astype(BF16)

    acc = None
    lo = 0
    for xp_ref, xs_ref in zip(x_refs[0::2], x_refs[1::2]):
        k = xp_ref.shape[1]
        x = jnp.where(i < P_TILES, xp_ref[...], xs_ref[...])
        part = jnp.dot(x.astype(BF16), wbf_ref[lo:lo + k, :], preferred_element_type=F32)
        acc = part if acc is None else acc + part
        lo += k
    gate = _ada_chunk(ada_ref, _cond_row(i), gate_chunk)
    o_ref[...] = y_ref[...] + gate * acc


def _out_proj_residual(xs, y, ada_l, w, gate_chunk):
    x_specs = []
    for xp, _ in xs:
        x_specs.append(pl.BlockSpec((ROW_TILE, xp.shape[1]), lambda i: (jnp.minimum(i, P_TILES - 1), 0)))
        x_specs.append(pl.BlockSpec((ROW_TILE, xp.shape[1]), lambda i: (jnp.maximum(i - P_TILES, 0), 0)))
    return pl.pallas_call(
        functools.partial(_out_proj_kernel, len(xs), gate_chunk),
        grid=(N_ROW_TILES,),
        in_specs=x_specs + [
            pl.BlockSpec((ROW_TILE, D_MODEL), lambda i: (i, 0)),
            pl.BlockSpec((COND_ROWS, ADA_CHUNKS * D_MODEL), lambda i: (0, 0)),
            pl.BlockSpec((D_MODEL, D_MODEL), lambda i: (0, 0), pipeline_mode=pl.Buffered(1)),
        ],
        out_specs=pl.BlockSpec((ROW_TILE, D_MODEL), lambda i: (i, 0)),
        out_shape=jax.ShapeDtypeStruct((T_ALL, D_MODEL), F32),
        scratch_shapes=[pltpu.VMEM((D_MODEL, D_MODEL), BF16)],
        compiler_params=_cparams(1),
    )(*[a for pair in xs for a in pair], y, ada_l, w)


def _subhead_norm(x, g2):
    lane = lax.broadcasted_iota(jnp.int32, x.shape, 1)
    first = lane < HD_A
    xx = x * x
    s0 = jnp.sum(jnp.where(first, xx, 0.0), axis=-1, keepdims=True)
    s1 = jnp.sum(jnp.where(first, 0.0, xx), axis=-1, keepdims=True)
    r = jnp.where(first, lax.rsqrt(s0 / HD_A + EPS), lax.rsqrt(s1 / HD_A + EPS))
    return x * r * g2


def _rope(x, cos, sin):
    quarter = HD_A // 4
    lane = lax.broadcasted_iota(jnp.int32, x.shape, 1)
    lower = (lane % (2 * quarter)) < quarter
    swapped = jnp.where(lower, pltpu.roll(x, 2 * HD_A - quarter, 1), pltpu.roll(x, quarter, 1))
    return x * cos + swapped * sin


def _attn_kernel(lam_init, has_ctx, *refs):
    if has_ctx:
        (q_ref, k_ref, v_ref, ck_ref, cv_ref, cq_ref, sq_ref, ckk_ref, skk_ref,
         qg_ref, kg_ref, lp_ref, sg_ref, o_ref) = refs
    else:
        q_ref, k_ref, v_ref, qg_ref, kg_ref, lp_ref, sg_ref, o_ref, kn_ref = refs
    q = _subhead_norm(q_ref[...], qg_ref[...])
    k = _subhead_norm(k_ref[...], kg_ref[...])
    v = v_ref[...]
    if has_ctx:
        q = _rope(q, cq_ref[...], sq_ref[...])
        k = _rope(k, ckk_ref[...], skk_ref[...])
        k = jnp.concatenate([ck_ref[...], k], axis=0)
        v = jnp.concatenate([cv_ref[...], v], axis=0)
    else:
        kn_ref[...] = k
    lp = lp_ref[...]
    lam = (jnp.exp(jnp.sum(lp[0:1] * lp[1:2], axis=-1, keepdims=True))
           - jnp.exp(jnp.sum(lp[2:3] * lp[3:4], axis=-1, keepdims=True)) + lam_init)
    scale = HD_A ** -0.5
    probs = []
    for c in range(2):
        qc = q[:, c * HD_A:(c + 1) * HD_A].astype(BF16)
        kc = k[:, c * HD_A:(c + 1) * HD_A].astype(BF16)
        s = lax.dot_general(qc, kc, (((1,), (1,)), ((), ())), preferred_element_type=F32) * scale
        e = jnp.exp(s - jnp.max(s, axis=-1, keepdims=True))
        probs.append(e / jnp.sum(e, axis=-1, keepdims=True))
    w = probs[0] - lam * probs[1]
    o = _bdot(w, v)
    ms = jnp.mean(o * o, axis=-1, keepdims=True)
    o_ref[...] = (o * lax.rsqrt(ms + EPS) * sg_ref[...]) * (1.0 - lam_init)


def _attention_prompt(qkv, qg2, kg2, lam_p, sub_g, lam_init):
    nh = H_A
    head = 2 * HD_A
    small = [
        pl.BlockSpec((1, head), lambda b, h: (0, 0)),
        pl.BlockSpec((1, head), lambda b, h: (0, 0)),
        pl.BlockSpec((4, HD_A), lambda b, h: (0, 0)),
        pl.BlockSpec((1, head), lambda b, h: (0, 0)),
    ]
    return pl.pallas_call(
        functools.partial(_attn_kernel, lam_init, False),
        grid=(BATCH, nh),
        in_specs=[
            pl.BlockSpec((SEQ, head), lambda b, h: (b, h)),
            pl.BlockSpec((SEQ, head), lambda b, h: (b, nh + h)),
            pl.BlockSpec((SEQ, head), lambda b, h: (b, 2 * nh + h)),
        ] + small,
        out_specs=[pl.BlockSpec((SEQ, head), lambda b, h: (b, h)),
                   pl.BlockSpec((SEQ, head), lambda b, h: (b, h))],
        out_shape=[jax.ShapeDtypeStruct((T_P, W_A), F32), jax.ShapeDtypeStruct((T_P, W_A), F32)],
        compiler_params=_cparams(2),
    )(qkv, qkv, qkv, qg2, kg2, lam_p, sub_g)


def _attention_sample(qkv, cache_k, cache_v, cos, sin, qg2, kg2, lam_p, sub_g, lam_init):
    nh = H_A
    head = 2 * HD_A
    tq = ROW_TILE
    nq = DEC_SEQ // tq
    q_off = T_P // tq
    k_off = T_P // DEC_SEQ
    small = [
        pl.BlockSpec((1, head), lambda b, h, i: (0, 0)),
        pl.BlockSpec((1, head), lambda b, h, i: (0, 0)),
        pl.BlockSpec((4, HD_A), lambda b, h, i: (0, 0)),
        pl.BlockSpec((1, head), lambda b, h, i: (0, 0)),
    ]
    return pl.pallas_call(
        functools.partial(_attn_kernel, lam_init, True),
        grid=(DEC_BATCH, nh, nq),
        in_specs=[
            pl.BlockSpec((tq, head), lambda b, h, i: (q_off + b * nq + i, h)),
            pl.BlockSpec((DEC_SEQ, head), lambda b, h, i: (k_off + b, nh + h)),
            pl.BlockSpec((DEC_SEQ, head), lambda b, h, i: (k_off + b, 2 * nh + h)),
            pl.BlockSpec((None, PAST_LEN, head), lambda b, h, i: (b, 0, h)),
            pl.BlockSpec((None, PAST_LEN, head), lambda b, h, i: (b, 0, h)),
            pl.BlockSpec((tq, head), lambda b, h, i: (i, 0)),
            pl.BlockSpec((tq, head), lambda b, h, i: (i, 0)),
            pl.BlockSpec((DEC_SEQ, head), lambda b, h, i: (0, 0)),
            pl.BlockSpec((DEC_SEQ, head), lambda b, h, i: (0, 0)),
        ] + small,
        out_specs=pl.BlockSpec((tq, head), lambda b, h, i: (b * nq + i, h)),
        out_shape=jax.ShapeDtypeStruct((T_S, W_A), F32),
        compiler_params=_cparams(3),
    )(qkv, qkv, qkv, cache_k, cache_v, cos, sin, cos, sin, qg2, kg2, lam_p, sub_g)


def _rope_tables():
    half = HD_A // 2
    nf = half // 2
    inv = ROPE_BASE ** (-np.arange(nf, dtype=np.float32) / nf)
    pos = np.arange(DEC_SEQ)
    row = (pos // GRID_W).astype(np.float32)
    col = (pos % GRID_W).astype(np.float32)
    ang_r = (row[:, None] * inv).astype(np.float32)
    ang_c = (col[:, None] * inv).astype(np.float32)
    ang = np.concatenate([ang_r, ang_r, ang_c, ang_c], axis=1)
    sign = np.concatenate([-np.ones(nf), np.ones(nf), -np.ones(nf), np.ones(nf)]).astype(np.float32)
    cos = np.cos(ang.astype(np.float64)).astype(np.float32)
    sin = (np.sin(ang.astype(np.float64)) * sign).astype(np.float32)
    return jnp.asarray(np.tile(cos, (1, 2))), jnp.asarray(np.tile(sin, (1, 2)))


def _mlstm_kernel(seq, has_ctx, *refs):
    if has_ctx:
        (q_ref, k_ref, cwq_ref, cwk_ref, cbq_ref, cbk_ref, v_ref, mo_ref, g_ref, gt_ref,
         gb_ref, gbt_ref, hn_ref, c0_ref, n0_ref, m0_ref, o_ref,
         qs_ref, ks_ref, hf_ref, hb_ref, cs_ref, gl_ref, gtl_ref) = refs
    else:
        (q_ref, k_ref, cwq_ref, cwk_ref, cbq_ref, cbk_ref, v_ref, mo_ref, g_ref, gt_ref,
         gb_ref, gbt_ref, hn_ref, o_ref, c_out_ref, n_out_ref, m_out_ref,
         qs_ref, ks_ref, hf_ref, hb_ref, cs_ref, gl_ref, gtl_ref) = refs
    nc = seq // CHUNK
    qs_ref[...] = _silu(_dwconv3(q_ref[...], cwq_ref[...], cbq_ref[...])) * (HD_B ** -0.5)
    ks_ref[...] = _silu(_dwconv3(k_ref[...], cwk_ref[...], cbk_ref[...]))
    gate = g_ref[...] + gb_ref[...]
    col = lax.broadcasted_iota(jnp.int32, gate.shape, 1)
    gl_ref[...] = jnp.where(col % (2 * H_B) >= H_B, _log_sigmoid(gate), gate)
    gate_t = gt_ref[...] + gbt_ref[...]
    row = lax.broadcasted_iota(jnp.int32, gate_t.shape, 1)
    gtl_ref[...] = jnp.where(row % (2 * H_B) >= H_B, _log_sigmoid(gate_t), gate_t)

    t_idx = lax.broadcasted_iota(jnp.int32, (CHUNK, CHUNK), 0)
    s_idx = lax.broadcasted_iota(jnp.int32, (CHUNK, CHUNK), 1)
    chains = [(d, h) for d in range(2) for h in range(H_B)]

    def chunk_step(d, h, c, cm, nm, mm):
        i_idx = d * 2 * H_B + h
        f_idx = i_idx + H_B
        mask = (s_idx <= t_idx) if d == 0 else (s_idx >= t_idx)
        mask_t = (t_idx <= s_idx) if d == 0 else (t_idx >= s_idx)
        r0 = pl.multiple_of(c * CHUNK, CHUNK)
        cols = slice(h * HD_B, (h + 1) * HD_B)
        qt = qs_ref[pl.ds(r0, CHUNK), cols]
        kt = ks_ref[pl.ds(r0, CHUNK), cols]
        vt = v_ref[pl.ds(r0, CHUNK), cols]
        gc = gl_ref[pl.ds(r0, CHUNK), :]
        i_col = gc[:, i_idx:i_idx + 1]
        lf_col = gc[:, f_idx:f_idx + 1]
        i_row = gtl_ref[c, i_idx:i_idx + 1, :]
        lf_row = gtl_ref[c, f_idx:f_idx + 1, :]
        b_col = jnp.sum(jnp.where(mask, lf_row, 0.0), axis=-1, keepdims=True)
        b_row = jnp.sum(jnp.where(mask_t, lf_col, 0.0), axis=0, keepdims=True)
        b_last = jnp.sum(lf_row, axis=-1, keepdims=True)
        dmat = jnp.where(mask, b_col - b_row + i_row, NEG)
        inter = b_col + mm
        m_t = jnp.maximum(inter, jnp.max(dmat, axis=-1, keepdims=True))
        qk = lax.dot_general(qt.astype(BF16), kt.astype(BF16), (((1,), (1,)), ((), ())),
                             preferred_element_type=F32)
        s = qk * jnp.exp(dmat - m_t)
        w_inter = jnp.exp(inter - m_t)
        cq = lax.dot_general(qt.astype(BF16), cm.astype(BF16), (((1,), (1,)), ((), ())),
                             preferred_element_type=F32)
        num = _bdot(s, vt) + w_inter * cq
        nq = jnp.sum(s, axis=-1, keepdims=True) + w_inter * jnp.sum(qt * nm, axis=-1, keepdims=True)
        hdir_ref = hf_ref if d == 0 else hb_ref
        hdir_ref[pl.ds(r0, CHUNK), cols] = num / jnp.maximum(jnp.abs(nq), jnp.exp(-m_t))
        g_colv = b_last - b_col + i_col
        g_rowv = b_last - b_row + i_row
        m_new = jnp.maximum(b_last + mm, jnp.max(g_rowv, axis=-1, keepdims=True))
        w_c = jnp.exp(b_last + mm - m_new)
        w_k = jnp.exp(g_colv - m_new)
        vw = (vt * w_k).astype(BF16)
        c_new = w_c * cm + lax.dot_general(vw, kt.astype(BF16), (((0,), (0,)), ((), ())),
                                           preferred_element_type=F32)
        n_new = w_c * nm + jnp.sum(kt * w_k, axis=0, keepdims=True)
        return c_new, n_new, m_new

    for n, (d, h) in enumerate(chains):
        if has_ctx:
            cs_ref[n] = c0_ref[d, h]
        else:
            cs_ref[n] = jnp.zeros((HD_B, HD_B), F32)

    def body(j, carry):
        new = []
        for n, (d, h) in enumerate(chains):
            nm, mm = carry[n]
            c = j if d == 0 else nc - 1 - j
            c_new, n_new, m_new = chunk_step(d, h, c, cs_ref[n], nm, mm)
            cs_ref[n] = c_new
            new.append((n_new, m_new))
        return tuple(new)

    if has_ctx:
        init = tuple((n0_ref[d, h], m0_ref[d:d + 1, h:h + 1]) for d, h in chains)
    else:
        init = tuple((jnp.zeros((1, HD_B), F32), jnp.zeros((1, 1), F32)) for _ in chains)
    final = lax.fori_loop(0, nc, body, init)
    if not has_ctx:
        for n, (d, h) in enumerate(chains):
            c_out_ref[d, h] = cs_ref[n]
            n_out_ref[d, h] = final[n][0]
            m_out_ref[d, h] = jnp.broadcast_to(final[n][1], (1, HD_B))

    for h in range(H_B):
        cols = slice(h * HD_B, (h + 1) * HD_B)
        hh = hf_ref[:, cols] + hb_ref[:, cols]
        ms = jnp.mean(hh * hh, axis=-1, keepdims=True)
        o_ref[:, cols] = (hh * lax.rsqrt(ms + EPS) * hn_ref[:, cols]) * _sigmoid(mo_ref[:, cols])


def _mlstm(mqk, mv, mo, mg, gt, conv_w, conv_b, gate_b, hn_g, *, seq, nbatch, row_off, ctx=None):
    nh = H_B
    has_ctx = ctx is not None
    blk = lambda col: pl.BlockSpec((seq, W_B), lambda b, col=col: (row_off + b, col))
    in_specs = [
        blk(0), blk(1),
        pl.BlockSpec((3, W_B), lambda b: (0, 0)),
        pl.BlockSpec((3, W_B), lambda b: (0, 1)),
        pl.BlockSpec((1, W_B), lambda b: (0, 0)),
        pl.BlockSpec((1, W_B), lambda b: (0, 1)),
        blk(0), blk(0),
        pl.BlockSpec((seq, 4 * nh), lambda b: (row_off + b, 0)),
        pl.BlockSpec((None, seq // CHUNK, 4 * nh, CHUNK), lambda b: (b, 0, 0, 0)),
        pl.BlockSpec((1, 4 * nh), lambda b: (0, 0)),
        pl.BlockSpec((4 * nh, 1), lambda b: (0, 0)),
        pl.BlockSpec((1, W_B), lambda b: (0, 0)),
    ]
    args = [mqk, mqk, conv_w, conv_w, conv_b, conv_b, mv, mo, mg, gt,
            gate_b.reshape(1, 4 * nh), gate_b.reshape(4 * nh, 1), hn_g.reshape(1, W_B)]
    o_spec = pl.BlockSpec((seq, W_B), lambda b: (b, 0))
    o_shape = jax.ShapeDtypeStruct((nbatch * seq, W_B), F32)
    state_blk = lambda rows: pl.BlockSpec((None, 2, nh, rows, HD_B), lambda b: (b, 0, 0, 0, 0))
    if has_ctx:
        c0, n0, m0 = ctx
        in_specs += [state_blk(HD_B), state_blk(1), pl.BlockSpec((None, 2, nh), lambda b: (b, 0, 0))]
        args += [c0, n0.reshape(nbatch, 2, nh, 1, HD_B), m0]
        out_specs, out_shape = o_spec, o_shape
    else:
        out_specs = [o_spec, state_blk(HD_B), state_blk(1), state_blk(1)]
        out_shape = [
            o_shape,
            jax.ShapeDtypeStruct((nbatch, 2, nh, HD_B, HD_B), F32),
            jax.ShapeDtypeStruct((nbatch, 2, nh, 1, HD_B), F32),
            jax.ShapeDtypeStruct((nbatch, 2, nh, 1, HD_B), F32),
        ]
    return pl.pallas_call(
        functools.partial(_mlstm_kernel, seq, has_ctx),
        grid=(nbatch,),
        in_specs=in_specs,
        out_specs=out_specs,
        out_shape=out_shape,
        scratch_shapes=[pltpu.VMEM((seq, W_B), F32)] * 4 + [
            pltpu.VMEM((2 * nh, HD_B, HD_B), F32),
            pltpu.VMEM((seq, 4 * nh), F32),
            pltpu.VMEM((seq // CHUNK, 4 * nh, CHUNK), F32),
        ],
        compiler_params=_cparams(1),
    )(*args)


def _dft_mats(L):
    f = np.arange(L)[:, None]
    j = np.arange(L)[None, :]
    ang = 2.0 * np.pi * ((f * j) % (2 * L)) / (2 * L)
    cm = np.cos(ang)
    sm = np.sin(ang)
    alt = (1.0 - 2.0 * (np.arange(L) % 2))
    fwd_b = -sm
    fwd_b[0, :] = alt
    fwd = np.concatenate([cm, fwd_b], axis=0)
    wgt = np.where(np.arange(L) == 0, 1.0, 2.0)[None, :]
    inv_a = cm.T * wgt
    inv_b = -2.0 * sm.T
    inv_b[:, 0] = alt
    inv = np.concatenate([inv_a, inv_b], axis=1) / (2 * L)
    return jnp.asarray(fwd.astype(np.float32)), jnp.asarray(inv.astype(np.float32))


def _hyena_feats(L):
    t = np.linspace(0.0, 1.0, L, dtype=np.float32)
    wpos = (2.0 * math.pi * np.arange(L, dtype=np.float32) / L).astype(np.float32)
    fb = np.linspace(1e-4, HY_BANDS - 1, HY_BANDS, dtype=np.float32)
    z = (wpos[:, None] * fb).astype(np.float32)
    feats = np.concatenate([t[:, None], np.cos(z), -np.sin(z)], axis=-1).astype(np.float32)
    deltas = np.abs(np.linspace(math.log(HY_TARGET) / HY_SLOW_PCT, math.log(HY_TARGET) / HY_FAST_PCT,
                                D_MODEL, dtype=np.float32))
    decay = np.exp(-t[:, None] * deltas).astype(np.float32)
    return jnp.asarray(feats), jnp.asarray(decay)


def _filter_kernel(L, feats_ref, w1_ref, b1_ref, fr1_ref, w2_ref, b2_ref, fr2_ref, w3f_ref, w3b_ref,
                   decay_ref, fwd_ref, o_ref, hdn_ref):
    @pl.when((pl.program_id(0) == 0) & (pl.program_id(1) == 0))
    def _():
        h1 = jnp.sin(fr1_ref[...] * (jnp.dot(feats_ref[...], w1_ref[...], precision=HIGHEST,
                                             preferred_element_type=F32) + b1_ref[...]))
        hdn_ref[...] = jnp.sin(fr2_ref[...] * (jnp.dot(h1, w2_ref[...], precision=HIGHEST,
                                                       preferred_element_type=F32) + b2_ref[...]))

    hdn = hdn_ref[...]
    decay = decay_ref[...]
    f_fwd = jnp.dot(hdn, w3f_ref[...], precision=HIGHEST, preferred_element_type=F32) * decay
    f_bwd = jnp.dot(hdn, w3b_ref[...], precision=HIGHEST, preferred_element_type=F32) * decay
    row = lax.broadcasted_iota(jnp.int32, f_bwd.shape, 0)
    f_bwd = jnp.where(row == 0, 0.0, f_bwd)
    fwd = fwd_ref[...]
    p = _bdot(fwd, f_fwd)
    q = _bdot(fwd, f_bwd)
    first = row == 0
    o_ref[0:L, :] = p[0:L] + q[0:L]
    o_ref[L:2 * L, :] = p[L:2 * L] + jnp.where(first, q[L:2 * L], -q[L:2 * L])


def _hyena_filter_spectrum(L, fwd_bf, w1, b1, fr1, w2, b2, fr2, w3):
    feats, decay = _hyena_feats(L)
    td = 512
    nd = D_MODEL // td
    emb = feats.shape[1]
    vec = lambda a: a.reshape(1, HY_FH)
    full = lambda shape: pl.BlockSpec(shape, lambda o, j: (0, 0))
    return pl.pallas_call(
        functools.partial(_filter_kernel, L),
        grid=(HY_ORDER, nd),
        in_specs=[
            full((L, emb)), full((emb, HY_FH)), full((1, HY_FH)), full((1, HY_FH)),
            full((HY_FH, HY_FH)), full((1, HY_FH)), full((1, HY_FH)),
            pl.BlockSpec((HY_FH, td), lambda o, j: (0, o * 2 * nd + j)),
            pl.BlockSpec((HY_FH, td), lambda o, j: (0, o * 2 * nd + nd + j)),
            pl.BlockSpec((L, td), lambda o, j: (0, j)),
            full((2 * L, L)),
        ],
        out_specs=pl.BlockSpec((2 * L, td), lambda o, j: (0, o * nd + j)),
        out_shape=jax.ShapeDtypeStruct((2 * L, HY_ORDER * D_MODEL), F32),
        scratch_shapes=[pltpu.VMEM((L, HY_FH), F32)],
        compiler_params=_cparams(2),
    )(feats, w1, vec(b1), vec(fr1), w2, vec(b2), vec(fr2), w3, w3, decay, fwd_bf)


def _spectral_conv(u, fwd, inv, kspec, L):
    uf = jnp.dot(fwd, u.astype(BF16), preferred_element_type=F32)
    ua, ub = uf[0:L], uf[L:2 * L]
    ka, kb = kspec[0:L], kspec[L:2 * L]
    first = lax.broadcasted_iota(jnp.int32, ua.shape, 0) == 0
    ya = ua * ka - jnp.where(first, 0.0, ub * kb)
    yb = jnp.where(first, ub * kb, ua * kb + ub * ka)
    y = jnp.concatenate([ya, yb], axis=0).astype(BF16)
    return jnp.dot(inv, y, preferred_element_type=F32)


def _hyena_kernel(L, zv_ref, z1_ref, z2_ref, cwv_ref, cw1_ref, cw2_ref, cbv_ref, cb1_ref, cb2_ref,
                  fwd_ref, inv_ref, k0_ref, k1_ref, bias0_ref, bias1_ref, o_ref):
    fwd = fwd_ref[...]
    inv = inv_ref[...]
    v = _dwconv3(zv_ref[...], cwv_ref[...], cbv_ref[...])
    x1 = _dwconv3(z1_ref[...], cw1_ref[...], cb1_ref[...])
    x2 = _dwconv3(z2_ref[...], cw2_ref[...], cb2_ref[...])
    z = x1 * (_spectral_conv(v, fwd, inv, k0_ref[...], L) + v * bias0_ref[...])
    o_ref[...] = x2 * (_spectral_conv(z, fwd, inv, k1_ref[...], L) + z * bias1_ref[...])


def _hyena_core(zproj, conv_w, conv_b, fwd_bf, inv_bf, kspec, bias, *, seq, nbatch, row_off, td):
    nd = D_MODEL // td
    zblk = lambda part: pl.BlockSpec((seq, td), lambda b, j, part=part: (row_off + b, part * nd + j))
    cwblk = lambda part: pl.BlockSpec((3, td), lambda b, j, part=part: (0, part * nd + j))
    cbblk = lambda part: pl.BlockSpec((1, td), lambda b, j, part=part: (0, part * nd + j))
    return pl.pallas_call(
        functools.partial(_hyena_kernel, seq),
        grid=(nbatch, nd),
        in_specs=[
            zblk(0), zblk(1), zblk(2), cwblk(0), cwblk(1), cwblk(2), cbblk(0), cbblk(1), cbblk(2),
            pl.BlockSpec((2 * seq, seq), lambda b, j: (0, 0), pipeline_mode=pl.Buffered(1)),
            pl.BlockSpec((seq, 2 * seq), lambda b, j: (0, 0), pipeline_mode=pl.Buffered(1)),
            pl.BlockSpec((2 * seq, td), lambda b, j: (0, j)),
            pl.BlockSpec((2 * seq, td), lambda b, j: (0, nd + j)),
            pl.BlockSpec((None, 1, td), lambda b, j: (0, 0, j)),
            pl.BlockSpec((None, 1, td), lambda b, j: (1, 0, j)),
        ],
        out_specs=pl.BlockSpec((seq, td), lambda b, j: (b, j)),
        out_shape=jax.ShapeDtypeStruct((nbatch * seq, D_MODEL), F32),
        compiler_params=_cparams(2),
    )(zproj, zproj, zproj, conv_w, conv_w, conv_w, conv_b, conv_b, conv_b,
      fwd_bf, inv_bf, kspec, kspec, bias.reshape(HY_ORDER, 1, D_MODEL), bias.reshape(HY_ORDER, 1, D_MODEL))


def _router_kernel(x_ref, ada_ref, g_ref, rw_ref, rb_ref, h_ref, idx_ref, wt_ref):
    i = pl.program_id(0)
    row = _cond_row(i)
    h = _modulate(x_ref[...], g_ref[...], _ada_chunk(ada_ref, row, 3), _ada_chunk(ada_ref, row, 4))
    h_ref[...] = h.astype(BF16)
    logits = lax.dot_general(rw_ref[...], h, (((1,), (1,)), ((), ())), precision=HIGHEST,
                             preferred_element_type=F32) + rb_ref[...]
    expert = lax.broadcasted_iota(jnp.int32, logits.shape, 0)
    slot = lax.broadcasted_iota(jnp.int32, (TOP_K, logits.shape[1]), 0)
    vals = jnp.zeros((TOP_K, logits.shape[1]), F32)
    idxs = jnp.zeros((TOP_K, logits.shape[1]), jnp.int32)
    cur = logits
    for k in range(TOP_K):
        m = jnp.max(cur, axis=0, keepdims=True)
        a = jnp.min(jnp.where(cur == m, expert, N_EXPERTS), axis=0, keepdims=True)
        vals = jnp.where(slot == k, m, vals)
        idxs = jnp.where(slot == k, a, idxs)
        cur = jnp.where(expert == a, -jnp.inf, cur)
    e = jnp.exp(vals - vals[0:1])
    wt_ref[...] = e / jnp.sum(e, axis=0, keepdims=True)
    idx_ref[...] = idxs


def _router(y, ada_l, g, router_w, router_b):
    return pl.pallas_call(
        _router_kernel,
        grid=(N_ROW_TILES,),
        in_specs=[
            pl.BlockSpec((ROW_TILE, D_MODEL), lambda i: (i, 0)),
            pl.BlockSpec((COND_ROWS, ADA_CHUNKS * D_MODEL), lambda i: (0, 0)),
            pl.BlockSpec((1, D_MODEL), lambda i: (0, 0)),
            pl.BlockSpec((N_EXPERTS, D_MODEL), lambda i: (0, 0)),
            pl.BlockSpec((N_EXPERTS, 1), lambda i: (0, 0)),
        ],
        out_specs=[
            pl.BlockSpec((ROW_TILE, D_MODEL), lambda i: (i, 0)),
            pl.BlockSpec((TOP_K, ROW_TILE), lambda i: (0, i)),
            pl.BlockSpec((TOP_K, ROW_TILE), lambda i: (0, i)),
        ],
        out_shape=[
            jax.ShapeDtypeStruct((T_ALL, D_MODEL), BF16),
            jax.ShapeDtypeStruct((TOP_K, T_ALL), jnp.int32),
            jax.ShapeDtypeStruct((TOP_K, T_ALL), F32),
        ],
        compiler_params=_cparams(1),
    )(y, ada_l, g.reshape(1, D_MODEL), router_w.T, router_b.reshape(N_EXPERTS, 1))


def _deinterleave_matrix():
    s = np.zeros((256, 256), np.float32)
    j = np.arange(128)
    s[2 * j, j] = 1.0
    s[2 * j + 1, 128 + j] = 1.0
    return jnp.asarray(s)


def _weight_copies(layer, e, w1_hbm, w2_hbm, w1s_ref, w2s_ref, sem):
    copies = []
    r1 = D_MODEL // W1_DMA_CHUNKS
    for c in range(W1_DMA_CHUNKS):
        copies.append(pltpu.make_async_copy(w1_hbm.at[layer, e, pl.ds(c * r1, r1)],
                                            w1s_ref.at[pl.ds(c * r1, r1)], sem.at[c]))
    r2 = D_FF // W2_DMA_CHUNKS
    for c in range(W2_DMA_CHUNKS):
        copies.append(pltpu.make_async_copy(w2_hbm.at[layer, e, pl.ds(c * r2, r2)],
                                            w2s_ref.at[pl.ds(c * r2, r2)], sem.at[W1_DMA_CHUNKS + c]))
    return copies


def _expert_kernel(layer, te_ref, tf_ref, ne_ref, nu_ref, src_ref, x_ref, b1_ref, b2_ref, wt_ref, s_ref,
                   w1_hbm, w2_hbm, o_hbm, w1s_ref, w2s_ref, w1p_ref, w2p_ref, acc_ref, out_ref, wsem, osem):
    i = pl.program_id(0)
    half = 128
    copies = functools.partial(_weight_copies, layer, w1_hbm=w1_hbm, w2_hbm=w2_hbm,
                               w1s_ref=w1s_ref, w2s_ref=w2s_ref, sem=wsem)

    @pl.when(i == 0)
    def _():
        acc_ref[...] = jnp.zeros_like(acc_ref)
        out_ref[...] = jnp.zeros_like(out_ref)
        for cp in copies(te_ref[0]):
            cp.start()

    @pl.when(tf_ref[i] == 1)
    def _():
        for cp in copies(te_ref[i]):
            cp.wait()
        s = s_ref[...].astype(BF16)
        for c in range(2 * D_FF // 256):
            blk = jnp.dot(w1s_ref[:, c * 256:(c + 1) * 256].astype(BF16), s, preferred_element_type=F32)
            w1p_ref[:, c * half:(c + 1) * half] = blk[:, :half].astype(BF16)
            w1p_ref[:, D_FF + c * half:D_FF + (c + 1) * half] = blk[:, half:].astype(BF16)
        w2p_ref[...] = w2s_ref[...].astype(BF16)

        @pl.when(ne_ref[i] >= 0)
        def _():
            for cp in copies(ne_ref[i]):
                cp.start()

    @pl.when(i <= nu_ref[0])
    def _():
        base = i * MOE_TILE
        prev = (i + 1) % 2
        for r0 in range(0, MOE_TILE, SCATTER_GROUP):
            toks = [src_ref[base + r0 + g] for g in range(SCATTER_GROUP)]
            cur = [acc_ref[pl.ds(toks[g], 1), :] for g in range(SCATTER_GROUP)]
            add = [out_ref[prev, r0 + g:r0 + g + 1, :] for g in range(SCATTER_GROUP)]
            for g in range(SCATTER_GROUP):
                acc_ref[pl.ds(toks[g], 1), :] = cur[g] + add[g]
        a = jnp.dot(x_ref[...], w1p_ref[...], preferred_element_type=F32) + b1_ref[...]
        glu = jnp.minimum(a[:, :D_FF], SWIGLU_LIMIT)
        lin = jnp.clip(a[:, D_FF:], -SWIGLU_LIMIT, SWIGLU_LIMIT)
        hid = glu * _sigmoid(SWIGLU_ALPHA * glu) * (lin + 1.0)
        out = jnp.dot(hid.astype(BF16), w2p_ref[...], preferred_element_type=F32) + b2_ref[...]
        out_ref[i % 2] = out * wt_ref[...]

    @pl.when(i == pl.num_programs(0) - 1)
    def _():
        cp = pltpu.make_async_copy(acc_ref.at[pl.ds(0, T_ALL)], o_hbm, osem)
        cp.start()
        cp.wait()


def _experts(layer, x_sorted, w_sorted, plan, w1, b1p, w2, b2):
    tile_expert, tile_first, next_expert, n_used, src = plan
    grid_spec = pltpu.PrefetchScalarGridSpec(
        num_scalar_prefetch=5,
        grid=(MOE_TILES,),
        in_specs=[
            pl.BlockSpec((MOE_TILE, D_MODEL), lambda i, te, *_: (i, 0)),
            pl.BlockSpec((None, None, 1, 2 * D_FF), lambda i, te, *_: (layer, te[i], 0, 0)),
            pl.BlockSpec((None, None, 1, D_MODEL), lambda i, te, *_: (layer, te[i], 0, 0)),
            pl.BlockSpec((MOE_TILE, 1), lambda i, te, *_: (i, 0)),
            pl.BlockSpec((256, 256), lambda i, te, *_: (0, 0)),
            pl.BlockSpec(memory_space=pl.ANY),
            pl.BlockSpec(memory_space=pl.ANY),
        ],
        out_specs=pl.BlockSpec(memory_space=pl.ANY),
        scratch_shapes=[
            pltpu.VMEM((D_MODEL, 2 * D_FF), F32),
            pltpu.VMEM((D_FF, D_MODEL), F32),
            pltpu.VMEM((D_MODEL, 2 * D_FF), BF16),
            pltpu.VMEM((D_FF, D_MODEL), BF16),
            pltpu.VMEM((ACC_ROWS, D_MODEL), F32),
            pltpu.VMEM((2, MOE_TILE, D_MODEL), F32),
            pltpu.SemaphoreType.DMA((W1_DMA_CHUNKS + W2_DMA_CHUNKS,)),
            pltpu.SemaphoreType.DMA(()),
        ],
    )
    return pl.pallas_call(
        functools.partial(_expert_kernel, layer),
        grid_spec=grid_spec,
        out_shape=jax.ShapeDtypeStruct((T_ALL, D_MODEL), F32),
        compiler_params=_cparams(1),
    )(tile_expert, tile_first, next_expert, n_used, src, x_sorted, b1p, b2, w_sorted,
      _deinterleave_matrix(), w1, w2)


def _combine_kernel(first_tile, y_ref, a_ref, ada_ref, o_ref):
    gate = _ada_chunk(ada_ref, _cond_row(first_tile + pl.program_id(0)), 5)
    o_ref[...] = y_ref[...] + gate * a_ref[...]


def _combine(y, acc, ada_l, first_tile=0, n_tiles=N_ROW_TILES):
    return pl.pallas_call(
        functools.partial(_combine_kernel, first_tile),
        grid=(n_tiles,),
        in_specs=[
            pl.BlockSpec((ROW_TILE, D_MODEL), lambda i: (first_tile + i, 0)),
            pl.BlockSpec((ROW_TILE, D_MODEL), lambda i: (first_tile + i, 0)),
            pl.BlockSpec((COND_ROWS, ADA_CHUNKS * D_MODEL), lambda i: (0, 0)),
        ],
        out_specs=pl.BlockSpec((ROW_TILE, D_MODEL), lambda i: (i, 0)),
        out_shape=jax.ShapeDtypeStruct((n_tiles * ROW_TILE, D_MODEL), F32),
        compiler_params=_cparams(1),
    )(y, acc, ada_l)


def _routing_plan(idx, wts):
    eid = idx.reshape(-1)
    order = jnp.argsort(eid, stable=True).astype(jnp.int32)
    experts = jnp.arange(N_EXPERTS, dtype=jnp.int32)
    counts = jnp.sum(eid[:, None] == experts[None, :], axis=0).astype(jnp.int32)
    ntiles = (counts + MOE_TILE - 1) // MOE_TILE
    tile_end = jnp.cumsum(ntiles).astype(jnp.int32)
    tile_begin = tile_end - ntiles
    cstarts = (jnp.cumsum(counts) - counts).astype(jnp.int32)
    n_used = tile_end[-1]
    tile = jnp.arange(MOE_TILES, dtype=jnp.int32)
    te = jnp.minimum(jnp.sum(tile[:, None] >= tile_end[None, :], axis=1), N_EXPERTS - 1).astype(jnp.int32)
    used = tile < n_used
    prev = jnp.concatenate([jnp.full((1,), -1, jnp.int32), te[:-1]])
    first = (te != prev) & used
    nxt = tile_end[te]
    next_expert = jnp.where(first & (nxt < n_used), te[jnp.minimum(nxt, MOE_TILES - 1)], -1).astype(jnp.int32)
    off = (tile - tile_begin[te])[:, None] * MOE_TILE + jnp.arange(MOE_TILE, dtype=jnp.int32)[None, :]
    valid = (off < counts[te][:, None]) & used[:, None]
    assign = order[jnp.clip(cstarts[te][:, None] + off, 0, N_ASSIGN - 1)]
    token = assign // TOP_K
    src = jnp.where(valid, token, SPARE_ROW).reshape(MOE_ROWS).astype(jnp.int32)
    src = jnp.concatenate([jnp.full((MOE_TILE,), SPARE_ROW, jnp.int32), src])
    gather_row = jnp.where(valid, token, 0).reshape(MOE_ROWS)
    w_sorted = jnp.where(valid, wts.reshape(-1)[assign], 0.0).reshape(MOE_ROWS, 1)
    plan = (te, first.astype(jnp.int32), next_expert, n_used.reshape(1), src)
    return plan, gather_row, w_sorted


def _moe(layer, y, ada_l, g, router_w, router_b, w1, b1p, w2, b2):
    h, idx_t, wts_t = _router(y, ada_l, g, router_w, router_b)
    plan, gather_row, w_sorted = _routing_plan(idx_t.T, wts_t.T)
    x_sorted = jnp.take(h, gather_row, axis=0, mode="clip")
    acc = _experts(layer, x_sorted, w_sorted, plan, w1, b1p, w2, b2)
    if layer == DEPTH - 1:
        return (_combine(y, acc, ada_l, 0, P_TILES), _combine(y, acc, ada_l, P_TILES, N_ROW_TILES - P_TILES))
    return _combine(y, acc, ada_l)


def _chunked_gates_t(mg_stream, nbatch, seq):
    return mg_stream.reshape(nbatch, seq // CHUNK, CHUNK, 4 * H_B).transpose(0, 1, 3, 2)


def kernel(x_prompt, x_sample, cache_attn_k, cache_attn_v, state_mlstm_C, state_mlstm_n, state_mlstm_m, c, c_ctx, ada_w, ada_b, norm_mix_g, norm_ffn_g, ab_w_in, ab_w_out, da_qnorm_g, da_knorm_g, da_lambda, da_subnorm_g, ml_conv_w, ml_conv_b, ml_gate_b, ml_headnorm_g, hy_w_in, hy_w_out, hy_conv_w, hy_conv_b, hy_f_w1, hy_f_b1, hy_f_freq1, hy_f_w2, hy_f_b2, hy_f_freq2, hy_f_w3, hy_bias, router_w, router_b, moe_w1, moe_b1, moe_w2, moe_b2):
    y = jnp.concatenate([x_prompt.reshape(T_P, D_MODEL), x_sample.reshape(T_S, D_MODEL)], axis=0)
    cond = jnp.concatenate([c_ctx[None, :], c, jnp.zeros((COND_ROWS - 1 - DEC_BATCH, D_MODEL), F32)], axis=0)
    ada = _ada_table(cond, ada_w, ada_b)
    b1p = jnp.concatenate([moe_b1[..., 0::2], moe_b1[..., 1::2]], axis=-1).reshape(DEPTH, N_EXPERTS, 1, 2 * D_FF)
    b2r = moe_b2.reshape(DEPTH, N_EXPERTS, 1, D_MODEL)
    new_k, new_v, new_c, new_n, new_m = [], [], [], [], []
    for layer in range(DEPTH):
        ada_l = ada[layer]
        if layer % 2 == 0:
            e = layer // 2
            lam_init = 0.8 - 0.6 * math.exp(-0.3 * layer)
            qkv, mqk, mv, mo, mg = _modulated_proj(
                y, ada_l, norm_mix_g[layer], ab_w_in[e], (3 * W_A, 2 * W_B, W_B, W_B, 4 * H_B))
            qg2 = jnp.tile(da_qnorm_g[e], 2).reshape(1, 2 * HD_A)
            kg2 = jnp.tile(da_knorm_g[e], 2).reshape(1, 2 * HD_A)
            sub_g = da_subnorm_g[e].reshape(1, 2 * HD_A)
            oa_p, k_norm = _attention_prompt(qkv, qg2, kg2, da_lambda[e], sub_g, lam_init)
            cos, sin = _rope_tables()
            oa_s = _attention_sample(
                qkv, cache_attn_k[:, e].reshape(DEC_BATCH, PAST_LEN, W_A),
                cache_attn_v[:, e].reshape(DEC_BATCH, PAST_LEN, W_A), cos, sin,
                qg2, kg2, da_lambda[e], sub_g, lam_init)
            gt_p = _chunked_gates_t(mg[:T_P], BATCH, SEQ)
            gt_s = _chunked_gates_t(mg[T_P:], DEC_BATCH, DEC_SEQ)
            ob_p, c_new, n_new, m_new = _mlstm(
                mqk, mv, mo, mg, gt_p, ml_conv_w[e], ml_conv_b[e].reshape(1, 2 * W_B), ml_gate_b[e],
                ml_headnorm_g[e], seq=SEQ, nbatch=BATCH, row_off=0)
            ob_s = _mlstm(
                mqk, mv, mo, mg, gt_s, ml_conv_w[e], ml_conv_b[e].reshape(1, 2 * W_B), ml_gate_b[e],
                ml_headnorm_g[e], seq=DEC_SEQ, nbatch=DEC_BATCH, row_off=T_P // DEC_SEQ,
                ctx=(state_mlstm_C[:, e], state_mlstm_n[:, e], state_mlstm_m[:, e]))
            y = _out_proj_residual([(oa_p, oa_s), (ob_p, ob_s)], y, ada_l, ab_w_out[e], 2)
            new_k.append(k_norm.reshape(BATCH, SEQ, H_A, 2, HD_A))
            new_v.append(qkv[:T_P, 2 * W_A:].reshape(BATCH, SEQ, H_A, 2 * HD_A))
            new_c.append(c_new)
            new_n.append(n_new.reshape(BATCH, 2, H_B, HD_B))
            new_m.append(m_new[..., 0, 0])
        else:
            o = layer // 2
            (zproj,) = _modulated_proj(y, ada_l, norm_mix_g[layer], hy_w_in[o], (HY_PROJ,))
            cores = []
            for seq, nbatch, row_off, td in ((SEQ, BATCH, 0, 512), (DEC_SEQ, DEC_BATCH, T_P // DEC_SEQ, 256)):
                fwd, inv = _dft_mats(seq)
                fwd_bf, inv_bf = fwd.astype(BF16), inv.astype(BF16)
                kspec = _hyena_filter_spectrum(seq, fwd_bf, hy_f_w1[o], hy_f_b1[o], hy_f_freq1[o], hy_f_w2[o],
                                               hy_f_b2[o], hy_f_freq2[o], hy_f_w3[o])
                cores.append(_hyena_core(zproj, hy_conv_w[o], hy_conv_b[o].reshape(1, HY_PROJ), fwd_bf, inv_bf,
                                         kspec, hy_bias[o], seq=seq, nbatch=nbatch, row_off=row_off, td=td))
            y = _out_proj_residual([tuple(cores)], y, ada_l, hy_w_out[o], 2)
        y = _moe(layer, y, ada_l, norm_ffn_g[layer], router_w[layer], router_b[layer],
                 moe_w1, b1p, moe_w2, b2r)
    y_p = y[0].reshape(BATCH, SEQ, D_MODEL)
    y_s = y[1].reshape(DEC_BATCH, DEC_SEQ, D_MODEL)
    return (y_p, y_s, jnp.stack(new_k, axis=1), jnp.stack(new_v, axis=1), jnp.stack(new_c, axis=1),
            jnp.stack(new_n, axis=1), jnp.stack(new_m, axis=1))
```

```python
import functools
import math

import numpy as np
import jax
import jax.numpy as jnp
from jax import lax
from jax.experimental import pallas as pl
from jax.experimental.pallas import tpu as pltpu

D_MODEL = 1024
BATCH = 16
SEQ = 256
DEPTH = 2
DEC_BATCH = 2
DEC_SEQ = 1024
PAST_LEN = 256
GRID_W = 64
W_A = D_MODEL // 2
HD_A = 64
H_A = W_A // (2 * HD_A)
W_B = D_MODEL - W_A
HD_B = 128
H_B = W_B // HD_B
AB_PROJ = 3 * W_A + 4 * W_B + 4 * H_B
ROPE_BASE = 10000.0
CHUNK = 64
HY_ORDER = 2
HY_PROJ = (HY_ORDER + 1) * D_MODEL
HY_BANDS = 8
HY_FH = 64
HY_TARGET = 1e-2
HY_FAST_PCT = 0.3
HY_SLOW_PCT = 1.5
N_EXPERTS = 32
TOP_K = 4
D_FF = D_MODEL
SWIGLU_ALPHA = 1.702
SWIGLU_LIMIT = 7.0
ADA_CHUNKS = 6
EPS = 1e-6
NEG = -1e30
F32 = jnp.float32
BF16 = jnp.bfloat16

T_P = BATCH * SEQ
T_S = DEC_BATCH * DEC_SEQ
T_ALL = T_P + T_S
ROW_TILE = 256
N_ROW_TILES = T_ALL // ROW_TILE
P_TILES = T_P // ROW_TILE
S_TILES_PER_BATCH = DEC_SEQ // ROW_TILE
COND_ROWS = 8
MOE_TILE = 256
N_ASSIGN = T_ALL * TOP_K
MOE_ROWS = N_ASSIGN + N_EXPERTS * MOE_TILE
MOE_TILES = MOE_ROWS // MOE_TILE
SPARE_ROW = T_ALL
ACC_ROWS = T_ALL + 8
SCATTER_GROUP = 8
W1_DMA_CHUNKS = 8
W2_DMA_CHUNKS = 4
VMEM_LIMIT = 56 * 1024 * 1024
HIGHEST = lax.Precision.HIGHEST


def _cparams(n_axes):
    return pltpu.CompilerParams(dimension_semantics=("arbitrary",) * n_axes,
                                vmem_limit_bytes=VMEM_LIMIT)


def _bdot(a, b):
    return jnp.dot(a.astype(BF16), b.astype(BF16), preferred_element_type=F32)


def _cond_row(i):
    return jnp.where(i < P_TILES, 0, 1 + (i - P_TILES) // S_TILES_PER_BATCH)


def _ada_chunk(ada_ref, row, j):
    return ada_ref[pl.ds(row, 1), j * D_MODEL:(j + 1) * D_MODEL]


def _modulate(x, g, shift, scale):
    ms = jnp.mean(x * x, axis=-1, keepdims=True)
    return (x * lax.rsqrt(ms + EPS) * g) * (1.0 + scale) + shift


def _sigmoid(x):
    return 1.0 / (1.0 + jnp.exp(-x))


def _silu(x):
    return x * _sigmoid(x)


def _log_sigmoid(x):
    return jnp.minimum(x, 0.0) - jnp.log(1.0 + jnp.exp(-jnp.abs(x)))


def _dwconv3(x, w, b):
    n = x.shape[0]
    row = lax.broadcasted_iota(jnp.int32, x.shape, 0)
    prev = jnp.where(row == 0, 0.0, pltpu.roll(x, 1, 0))
    nxt = jnp.where(row == n - 1, 0.0, pltpu.roll(x, n - 1, 0))
    return prev * w[0:1] + x * w[1:2] + nxt * w[2:3] + b


def _ada_kernel(cond_ref, w_ref, b_ref, o_ref):
    c = _silu(cond_ref[...])
    o_ref[...] = _bdot(c, w_ref[...]) + b_ref[...]


def _ada_table(cond, ada_w, ada_b):
    tn = 1536
    return pl.pallas_call(
        _ada_kernel,
        grid=(DEPTH, ADA_CHUNKS * D_MODEL // tn),
        in_specs=[
            pl.BlockSpec((COND_ROWS, D_MODEL), lambda l, j: (0, 0)),
            pl.BlockSpec((None, D_MODEL, tn), lambda l, j: (l, 0, j)),
            pl.BlockSpec((None, 1, tn), lambda l, j: (l, 0, j)),
        ],
        out_specs=pl.BlockSpec((None, COND_ROWS, tn), lambda l, j: (l, 0, j)),
        out_shape=jax.ShapeDtypeStruct((DEPTH, COND_ROWS, ADA_CHUNKS * D_MODEL), F32),
        compiler_params=_cparams(2),
    )(cond, ada_w, ada_b.reshape(DEPTH, 1, ADA_CHUNKS * D_MODEL))


def _proj_kernel(splits, x_ref, ada_ref, g_ref, w_ref, *rest):
    out_refs, wbf_ref = rest[:-1], rest[-1]
    i = pl.program_id(0)

    @pl.when(i == 0)
    def _():
        wbf_ref[...] = w_ref[...].astype(BF16)

    row = _cond_row(i)
    h = _modulate(x_ref[...], g_ref[...], _ada_chunk(ada_ref, row, 0), _ada_chunk(ada_ref, row, 1))
    h = h.astype(BF16)
    lo = 0
    for o_ref, width in zip(out_refs, splits):
        o_ref[...] = jnp.dot(h, wbf_ref[:, lo:lo + width], preferred_element_type=F32)
        lo += width


def _modulated_proj(y, ada_l, g, w, splits):
    n = w.shape[1]
    return pl.pallas_call(
        functools.partial(_proj_kernel, splits),
        grid=(N_ROW_TILES,),
        in_specs=[
            pl.BlockSpec((ROW_TILE, D_MODEL), lambda i: (i, 0)),
            pl.BlockSpec((COND_ROWS, ADA_CHUNKS * D_MODEL), lambda i: (0, 0)),
            pl.BlockSpec((1, D_MODEL), lambda i: (0, 0)),
            pl.BlockSpec((D_MODEL, n), lambda i: (0, 0), pipeline_mode=pl.Buffered(1)),
        ],
        out_specs=[pl.BlockSpec((ROW_TILE, s), lambda i: (i, 0)) for s in splits],
        out_shape=[jax.ShapeDtypeStruct((T_ALL, s), F32) for s in splits],
        scratch_shapes=[pltpu.VMEM((D_MODEL, n), BF16)],
        compiler_params=_cparams(1),
    )(y, ada_l, g.reshape(1, D_MODEL), w)


def _out_proj_kernel(n_in, gate_chunk, *refs):
    x_refs = refs[:2 * n_in]
    y_ref, ada_ref, w_ref, o_ref, wbf_ref = refs[2 * n_in:]
    i = pl.program_id(0)

    @pl.when(i == 0)
    def _():
        wbf_ref[...] = w_ref[...].astype(BF16)

    acc = None
    lo = 0
    for xp_ref, xs_ref in zip(x_refs[0::2], x_refs[1::2]):
        k = xp_ref.shape[1]
        x = jnp.where(i < P_TILES, xp_ref[...], xs_ref[...])
        part = jnp.dot(x.astype(BF16), wbf_ref[lo:lo + k, :], preferred_element_type=F32)
        acc = part if acc is None else acc + part
        lo += k
    gate = _ada_chunk(ada_ref, _cond_row(i), gate_chunk)
    o_ref[...] = y_ref[...] + gate * acc


def _out_proj_residual(xs, y, ada_l, w, gate_chunk):
    x_specs = []
    for xp, _ in xs:
        x_specs.append(pl.BlockSpec((ROW_TILE, xp.shape[1]), lambda i: (jnp.minimum(i, P_TILES - 1), 0)))
        x_specs.append(pl.BlockSpec((ROW_TILE, xp.shape[1]), lambda i: (jnp.maximum(i - P_TILES, 0), 0)))
    return pl.pallas_call(
        functools.partial(_out_proj_kernel, len(xs), gate_chunk),
        grid=(N_ROW_TILES,),
        in_specs=x_specs + [
            pl.BlockSpec((ROW_TILE, D_MODEL), lambda i: (i, 0)),
            pl.BlockSpec((COND_ROWS, ADA_CHUNKS * D_MODEL), lambda i: (0, 0)),
            pl.BlockSpec((D_MODEL, D_MODEL), lambda i: (0, 0), pipeline_mode=pl.Buffered(1)),
        ],
        out_specs=pl.BlockSpec((ROW_TILE, D_MODEL), lambda i: (i, 0)),
        out_shape=jax.ShapeDtypeStruct((T_ALL, D_MODEL), F32),
        scratch_shapes=[pltpu.VMEM((D_MODEL, D_MODEL), BF16)],
        compiler_params=_cparams(1),
    )(*[a for pair in xs for a in pair], y, ada_l, w)


def _subhead_norm(x, g2):
    lane = lax.broadcasted_iota(jnp.int32, x.shape, 1)
    first = lane < HD_A
    xx = x * x
    s0 = jnp.sum(jnp.where(first, xx, 0.0), axis=-1, keepdims=True)
    s1 = jnp.sum(jnp.where(first, 0.0, xx), axis=-1, keepdims=True)
    r = jnp.where(first, lax.rsqrt(s0 / HD_A + EPS), lax.rsqrt(s1 / HD_A + EPS))
    return x * r * g2


def _rope(x, cos, sin):
    quarter = HD_A // 4
    lane = lax.broadcasted_iota(jnp.int32, x.shape, 1)
    lower = (lane % (2 * quarter)) < quarter
    swapped = jnp.where(lower, pltpu.roll(x, 2 * HD_A - quarter, 1), pltpu.roll(x, quarter, 1))
    return x * cos + swapped * sin


def _attn_kernel(lam_init, has_ctx, *refs):
    if has_ctx:
        (q_ref, k_ref, v_ref, ck_ref, cv_ref, cq_ref, sq_ref, ckk_ref, skk_ref,
         qg_ref, kg_ref, lp_ref, sg_ref, o_ref) = refs
    else:
        q_ref, k_ref, v_ref, qg_ref, kg_ref, lp_ref, sg_ref, o_ref, kn_ref = refs
    q = _subhead_norm(q_ref[...], qg_ref[...])
    k = _subhead_norm(k_ref[...], kg_ref[...])
    v = v_ref[...]
    if has_ctx:
        q = _rope(q, cq_ref[...], sq_ref[...])
        k = _rope(k, ckk_ref[...], skk_ref[...])
        k = jnp.concatenate([ck_ref[...], k], axis=0)
        v = jnp.concatenate([cv_ref[...], v], axis=0)
    else:
        kn_ref[...] = k
    lp = lp_ref[...]
    lam = (jnp.exp(jnp.sum(lp[0:1] * lp[1:2], axis=-1, keepdims=True))
           - jnp.exp(jnp.sum(lp[2:3] * lp[3:4], axis=-1, keepdims=True)) + lam_init)
    scale = HD_A ** -0.5
    probs = []
    for c in range(2):
        qc = q[:, c * HD_A:(c + 1) * HD_A].astype(BF16)
        kc = k[:, c * HD_A:(c + 1) * HD_A].astype(BF16)
        s = lax.dot_general(qc, kc, (((1,), (1,)), ((), ())), preferred_element_type=F32) * scale
        e = jnp.exp(s - jnp.max(s, axis=-1, keepdims=True))
        probs.append(e / jnp.sum(e, axis=-1, keepdims=True))
    w = probs[0] - lam * probs[1]
    o = _bdot(w, v)
    ms = jnp.mean(o * o, axis=-1, keepdims=True)
    o_ref[...] = (o * lax.rsqrt(ms + EPS) * sg_ref[...]) * (1.0 - lam_init)


def _attention_prompt(qkv, qg2, kg2, lam_p, sub_g, lam_init):
    nh = H_A
    head = 2 * HD_A
    small = [
        pl.BlockSpec((1, head), lambda b, h: (0, 0)),
        pl.BlockSpec((1, head), lambda b, h: (0, 0)),
        pl.BlockSpec((4, HD_A), lambda b, h: (0, 0)),
        pl.BlockSpec((1, head), lambda b, h: (0, 0)),
    ]
    return pl.pallas_call(
        functools.partial(_attn_kernel, lam_init, False),
        grid=(BATCH, nh),
        in_specs=[
            pl.BlockSpec((SEQ, head), lambda b, h: (b, h)),
            pl.BlockSpec((SEQ, head), lambda b, h: (b, nh + h)),
            pl.BlockSpec((SEQ, head), lambda b, h: (b, 2 * nh + h)),
        ] + small,
        out_specs=[pl.BlockSpec((SEQ, head), lambda b, h: (b, h)),
                   pl.BlockSpec((SEQ, head), lambda b, h: (b, h))],
        out_shape=[jax.ShapeDtypeStruct((T_P, W_A), F32), jax.ShapeDtypeStruct((T_P, W_A), F32)],
        compiler_params=_cparams(2),
    )(qkv, qkv, qkv, qg2, kg2, lam_p, sub_g)


def _attention_sample(qkv, cache_k, cache_v, cos, sin, qg2, kg2, lam_p, sub_g, lam_init):
    nh = H_A
    head = 2 * HD_A
    tq = ROW_TILE
    nq = DEC_SEQ // tq
    q_off = T_P // tq
    k_off = T_P // DEC_SEQ
    small = [
        pl.BlockSpec((1, head), lambda b, h, i: (0, 0)),
        pl.BlockSpec((1, head), lambda b, h, i: (0, 0)),
        pl.BlockSpec((4, HD_A), lambda b, h, i: (0, 0)),
        pl.BlockSpec((1, head), lambda b, h, i: (0, 0)),
    ]
    return pl.pallas_call(
        functools.partial(_attn_kernel, lam_init, True),
        grid=(DEC_BATCH, nh, nq),
        in_specs=[
            pl.BlockSpec((tq, head), lambda b, h, i: (q_off + b * nq + i, h)),
            pl.BlockSpec((DEC_SEQ, head), lambda b, h, i: (k_off + b, nh + h)),
            pl.BlockSpec((DEC_SEQ, head), lambda b, h, i: (k_off + b, 2 * nh + h)),
            pl.BlockSpec((None, PAST_LEN, head), lambda b, h, i: (b, 0, h)),
            pl.BlockSpec((None, PAST_LEN, head), lambda b, h, i: (b, 0, h)),
            pl.BlockSpec((tq, head), lambda b, h, i: (i, 0)),
            pl.BlockSpec((tq, head), lambda b, h, i: (i, 0)),
            pl.BlockSpec((DEC_SEQ, head), lambda b, h, i: (0, 0)),
            pl.BlockSpec((DEC_SEQ, head), lambda b, h, i: (0, 0)),
        ] + small,
        out_specs=pl.BlockSpec((tq, head), lambda b, h, i: (b * nq + i, h)),
        out_shape=jax.ShapeDtypeStruct((T_S, W_A), F32),
        compiler_params=_cparams(3),
    )(qkv, qkv, qkv, cache_k, cache_v, cos, sin, cos, sin, qg2, kg2, lam_p, sub_g)


def _rope_tables():
    half = HD_A // 2
    nf = half // 2
    inv = ROPE_BASE ** (-np.arange(nf, dtype=np.float32) / nf)
    pos = np.arange(DEC_SEQ)
    row = (pos // GRID_W).astype(np.float32)
    col = (pos % GRID_W).astype(np.float32)
    ang_r = (row[:, None] * inv).astype(np.float32)
    ang_c = (col[:, None] * inv).astype(np.float32)
    ang = np.concatenate([ang_r, ang_r, ang_c, ang_c], axis=1)
    sign = np.concatenate([-np.ones(nf), np.ones(nf), -np.ones(nf), np.ones(nf)]).astype(np.float32)
    cos = np.cos(ang.astype(np.float64)).astype(np.float32)
    sin = (np.sin(ang.astype(np.float64)) * sign).astype(np.float32)
    return jnp.asarray(np.tile(cos, (1, 2))), jnp.asarray(np.tile(sin, (1, 2)))


def _mlstm_kernel(seq, has_ctx, *refs):
    if has_ctx:
        (q_ref, k_ref, cwq_ref, cwk_ref, cbq_ref, cbk_ref, v_ref, mo_ref, gi_ref, gf_ref,
         gbi_ref, gbf_ref, hn_ref, c0_ref, n0_ref, m0_ref, o_ref,
         qs_ref, ks_ref, hf_ref, hb_ref, cs_ref, rrow_ref, col_ref, wc_ref) = refs
    else:
        (q_ref, k_ref, cwq_ref, cwk_ref, cbq_ref, cbk_ref, v_ref, mo_ref, gi_ref, gf_ref,
         gbi_ref, gbf_ref, hn_ref, o_ref, c_out_ref, n_out_ref, m_out_ref,
         qs_ref, ks_ref, hf_ref, hb_ref, cs_ref, rrow_ref, col_ref, wc_ref) = refs
    nc = seq // CHUNK
    n_chain = 2 * H_B
    chains = [(d, h) for d in range(2) for h in range(H_B)]
    qs_ref[...] = _silu(_dwconv3(q_ref[...], cwq_ref[...], cbq_ref[...])) * (HD_B ** -0.5)
    ks_ref[...] = _silu(_dwconv3(k_ref[...], cwk_ref[...], cbk_ref[...]))

    rows = nc * n_chain
    lane = lax.broadcasted_iota(jnp.int32, (rows, 2 * CHUNK), 1)
    forward = lax.broadcasted_iota(jnp.int32, (rows, 2 * CHUNK), 0) % n_chain < H_B
    valid = lane < CHUNK

    def scan(x, op, fill):
        pre, suf = x, x
        sh = 1
        while sh < CHUNK:
            pre = op(pre, jnp.where(lane >= sh, pltpu.roll(pre, sh, 1), fill))
            suf = op(suf, jnp.where(lane + sh < CHUNK, pltpu.roll(suf, 2 * CHUNK - sh, 1), fill))
            sh *= 2
        return jnp.where(forward, pre, suf)

    gate_i = (gi_ref[...] + gbi_ref[...]).reshape(rows, 2 * CHUNK)
    lf = jnp.where(valid, _log_sigmoid(gf_ref[...] + gbf_ref[...]).reshape(rows, 2 * CHUNK), 0.0)
    b = scan(lf, jnp.add, 0.0)
    cmax = scan(jnp.where(valid, gate_i - b, -jnp.inf), jnp.maximum, -jnp.inf)
    b_last = jnp.sum(lf, axis=1, keepdims=True)
    g = b_last - b + gate_i
    g_max = jnp.max(jnp.where(valid, g, -jnp.inf), axis=1, keepdims=True)
    mm = m0_ref[...] if has_ctx else jnp.zeros((n_chain, 1), F32)
    mm_seq = []
    for p in range(nc):
        mm_seq.append(mm)
        seg = slice(p * n_chain, (p + 1) * n_chain)
        mm = jnp.maximum(b_last[seg] + mm, g_max[seg])
    mm_final = mm
    mm_prev = jnp.concatenate(mm_seq, axis=0)
    mm_next = jnp.concatenate(mm_seq[1:] + [mm_final], axis=0)
    m_t = jnp.maximum(b + mm_prev, b + cmax)
    rrow_ref[...] = (b - gate_i).reshape(nc, n_chain, 2 * CHUNK)
    wc_ref[...] = jnp.exp(b_last + mm_prev - mm_next).reshape(nc, n_chain, 1)
    per_row = [b, m_t, jnp.exp(b + mm_prev - m_t), jnp.exp(-m_t), jnp.exp(g - mm_next)]
    for j, arr in enumerate(per_row):
        by_time = arr.T
        for p in range(nc):
            col_ref[p, :, j * n_chain:(j + 1) * n_chain] = by_time[0:CHUNK, p * n_chain:(p + 1) * n_chain]

    t_idx = lax.broadcasted_iota(jnp.int32, (CHUNK, CHUNK), 0)
    s_idx = lax.broadcasted_iota(jnp.int32, (CHUNK, CHUNK), 1)
    for n, (d, h) in enumerate(chains):
        cs_ref[n] = c0_ref[d, h] if has_ctx else jnp.zeros((HD_B, HD_B), F32)

    def out_step(p, n_states):
        cols = col_ref[p]
        rrows = rrow_ref[p]
        wcs = wc_ref[p]
        new_states = []
        for n, (d, h) in enumerate(chains):
            c = p if d == 0 else nc - 1 - p
            r0 = pl.multiple_of(c * CHUNK, CHUNK)
            hcols = slice(h * HD_B, (h + 1) * HD_B)
            qt = qs_ref[pl.ds(r0, CHUNK), hcols]
            kt = ks_ref[pl.ds(r0, CHUNK), hcols]
            vt = v_ref[pl.ds(r0, CHUNK), hcols]
            b_col, m_t, w_inter, e_inv, w_k = (cols[:, j * n_chain + n:j * n_chain + n + 1] for j in range(5))
            mask = (s_idx <= t_idx) if d == 0 else (s_idx >= t_idx)
            decay = jnp.exp(jnp.where(mask, b_col - rrows[n:n + 1, 0:CHUNK], NEG) - m_t)
            qk = lax.dot_general(qt.astype(BF16), kt.astype(BF16), (((1,), (1,)), ((), ())),
                                 preferred_element_type=F32)
            s = qk * decay
            cm = cs_ref[n]
            nm = n_states[n]
            cq = lax.dot_general(qt.astype(BF16), cm.astype(BF16), (((1,), (1,)), ((), ())),
                                 preferred_element_type=F32)
            num = _bdot(s, vt) + w_inter * cq
            nq = jnp.sum(s, axis=-1, keepdims=True) + w_inter * jnp.sum(qt * nm, axis=-1, keepdims=True)
            hdir_ref = hf_ref if d == 0 else hb_ref
            hdir_ref[pl.ds(r0, CHUNK), hcols] = num / jnp.maximum(jnp.abs(nq), e_inv)
            w_c = wcs[n:n + 1, :]
            vw = (vt * w_k).astype(BF16)
            cs_ref[n] = w_c * cm + lax.dot_general(vw, kt.astype(BF16), (((0,), (0,)), ((), ())),
                                                   preferred_element_type=F32)
            new_states.append(w_c * nm + jnp.sum(kt * w_k, axis=0, keepdims=True))
        return tuple(new_states)

    if has_ctx:
        n_init = tuple(n0_ref[d, h] for d, h in chains)
    else:
        n_init = tuple(jnp.zeros((1, HD_B), F32) for _ in chains)
    n_final = lax.fori_loop(0, nc, out_step, n_init)
    if not has_ctx:
        for n, (d, h) in enumerate(chains):
            c_out_ref[d, h] = cs_ref[n]
            n_out_ref[d, h] = n_final[n]
            m_out_ref[d, h] = jnp.broadcast_to(mm_final[n:n + 1, :], (1, HD_B))

    for h in range(H_B):
        hcols = slice(h * HD_B, (h + 1) * HD_B)
        hh = hf_ref[:, hcols] + hb_ref[:, hcols]
        ms = jnp.mean(hh * hh, axis=-1, keepdims=True)
        o_ref[:, hcols] = (hh * lax.rsqrt(ms + EPS) * hn_ref[:, hcols]) * _sigmoid(mo_ref[:, hcols])


def _mlstm(mqk, mv, mo, mg_stream, conv_w, conv_b, gate_b, hn_g, *, seq, nbatch, row_off, ctx=None):
    nh = H_B
    nc = seq // CHUNK
    has_ctx = ctx is not None
    gt = mg_stream.reshape(nbatch, nc, CHUNK, 2, 2, nh).transpose(0, 1, 3, 4, 5, 2)
    pad = ((0, 0), (0, 0), (0, 0), (0, CHUNK))
    gates = [jnp.pad(jnp.concatenate([gt[:, :, 0, j], gt[:, ::-1, 1, j]], axis=2), pad) for j in range(2)]
    gate_bias = [jnp.concatenate([gate_b[0, j], gate_b[1, j]]).reshape(2 * nh, 1) for j in range(2)]
    blk = lambda col: pl.BlockSpec((seq, W_B), lambda b, col=col: (row_off + b, col))
    gate_blk = pl.BlockSpec((None, nc, 2 * nh, 2 * CHUNK), lambda b: (b, 0, 0, 0))
    in_specs = [
        blk(0), blk(1),
        pl.BlockSpec((3, W_B), lambda b: (0, 0)),
        pl.BlockSpec((3, W_B), lambda b: (0, 1)),
        pl.BlockSpec((1, W_B), lambda b: (0, 0)),
        pl.BlockSpec((1, W_B), lambda b: (0, 1)),
        blk(0), blk(0),
        gate_blk, gate_blk,
        pl.BlockSpec((2 * nh, 1), lambda b: (0, 0)),
        pl.BlockSpec((2 * nh, 1), lambda b: (0, 0)),
        pl.BlockSpec((1, W_B), lambda b: (0, 0)),
    ]
    args = [mqk, mqk, conv_w, conv_w, conv_b, conv_b, mv, mo, gates[0], gates[1],
            gate_bias[0], gate_bias[1], hn_g.reshape(1, W_B)]
    o_spec = pl.BlockSpec((seq, W_B), lambda b: (b, 0))
    o_shape = jax.ShapeDtypeStruct((nbatch * seq, W_B), F32)
    state_blk = lambda rows: pl.BlockSpec((None, 2, nh, rows, HD_B), lambda b: (b, 0, 0, 0, 0))
    if has_ctx:
        c0, n0, m0 = ctx
        in_specs += [state_blk(HD_B), state_blk(1), pl.BlockSpec((None, 2 * nh, 1), lambda b: (b, 0, 0))]
        args += [c0, n0.reshape(nbatch, 2, nh, 1, HD_B), m0.reshape(nbatch, 2 * nh, 1)]
        out_specs, out_shape = o_spec, o_shape
    else:
        out_specs = [o_spec, state_blk(HD_B), state_blk(1), state_blk(1)]
        out_shape = [
            o_shape,
            jax.ShapeDtypeStruct((nbatch, 2, nh, HD_B, HD_B), F32),
            jax.ShapeDtypeStruct((nbatch, 2, nh, 1, HD_B), F32),
            jax.ShapeDtypeStruct((nbatch, 2, nh, 1, HD_B), F32),
        ]
    return pl.pallas_call(
        functools.partial(_mlstm_kernel, seq, has_ctx),
        grid=(nbatch,),
        in_specs=in_specs,
        out_specs=out_specs,
        out_shape=out_shape,
        scratch_shapes=[pltpu.VMEM((seq, W_B), F32)] * 4 + [
            pltpu.VMEM((2 * nh, HD_B, HD_B), F32),
            pltpu.VMEM((nc, 2 * nh, 2 * CHUNK), F32),
            pltpu.VMEM((nc, CHUNK, 5 * 2 * nh), F32),
            pltpu.VMEM((nc, 2 * nh, 1), F32),
        ],
        compiler_params=_cparams(1),
    )(*args)


def _dft_mats(L):
    f = np.arange(L)[:, None]
    j = np.arange(L)[None, :]
    ang = 2.0 * np.pi * ((f * j) % (2 * L)) / (2 * L)
    cm = np.cos(ang)
    sm = np.sin(ang)
    alt = (1.0 - 2.0 * (np.arange(L) % 2))
    fwd_b = -sm
    fwd_b[0, :] = alt
    fwd = np.concatenate([cm, fwd_b], axis=0)
    wgt = np.where(np.arange(L) == 0, 1.0, 2.0)[None, :]
    inv_a = cm.T * wgt
    inv_b = -2.0 * sm.T
    inv_b[:, 0] = alt
    inv = np.concatenate([inv_a, inv_b], axis=1) / (2 * L)
    return jnp.asarray(fwd.astype(np.float32)), jnp.asarray(inv.astype(np.float32))


def _hyena_feats(L):
    t = np.linspace(0.0, 1.0, L, dtype=np.float32)
    wpos = (2.0 * math.pi * np.arange(L, dtype=np.float32) / L).astype(np.float32)
    fb = np.linspace(1e-4, HY_BANDS - 1, HY_BANDS, dtype=np.float32)
    z = (wpos[:, None] * fb).astype(np.float32)
    feats = np.concatenate([t[:, None], np.cos(z), -np.sin(z)], axis=-1).astype(np.float32)
    deltas = np.abs(np.linspace(math.log(HY_TARGET) / HY_SLOW_PCT, math.log(HY_TARGET) / HY_FAST_PCT,
                                D_MODEL, dtype=np.float32))
    decay = np.exp(-t[:, None] * deltas).astype(np.float32)
    return jnp.asarray(feats), jnp.asarray(decay)


def _filter_kernel(L, feats_ref, w1_ref, b1_ref, fr1_ref, w2_ref, b2_ref, fr2_ref, w3f_ref, w3b_ref,
                   decay_ref, fwd_ref, o_ref, hdn_ref):
    @pl.when((pl.program_id(0) == 0) & (pl.program_id(1) == 0))
    def _():
        h1 = jnp.sin(fr1_ref[...] * (jnp.dot(feats_ref[...], w1_ref[...], precision=HIGHEST,
                                             preferred_element_type=F32) + b1_ref[...]))
        hdn_ref[...] = jnp.sin(fr2_ref[...] * (jnp.dot(h1, w2_ref[...], precision=HIGHEST,
                                                       preferred_element_type=F32) + b2_ref[...]))

    hdn = hdn_ref[...]
    decay = decay_ref[...]
    f_fwd = jnp.dot(hdn, w3f_ref[...], precision=HIGHEST, preferred_element_type=F32) * decay
    f_bwd = jnp.dot(hdn, w3b_ref[...], precision=HIGHEST, preferred_element_type=F32) * decay
    row = lax.broadcasted_iota(jnp.int32, f_bwd.shape, 0)
    f_bwd = jnp.where(row == 0, 0.0, f_bwd)
    fwd = fwd_ref[...]
    p = _bdot(fwd, f_fwd)
    q = _bdot(fwd, f_bwd)
    first = row == 0
    o_ref[0:L, :] = p[0:L] + q[0:L]
    o_ref[L:2 * L, :] = p[L:2 * L] + jnp.where(first, q[L:2 * L], -q[L:2 * L])


def _hyena_filter_spectrum(L, fwd_bf, w1, b1, fr1, w2, b2, fr2, w3):
    feats, decay = _hyena_feats(L)
    td = 512
    nd = D_MODEL // td
    emb = feats.shape[1]
    vec = lambda a: a.reshape(1, HY_FH)
    full = lambda shape: pl.BlockSpec(shape, lambda o, j: (0, 0))
    return pl.pallas_call(
        functools.partial(_filter_kernel, L),
        grid=(HY_ORDER, nd),
        in_specs=[
            full((L, emb)), full((emb, HY_FH)), full((1, HY_FH)), full((1, HY_FH)),
            full((HY_FH, HY_FH)), full((1, HY_FH)), full((1, HY_FH)),
            pl.BlockSpec((HY_FH, td), lambda o, j: (0, o * 2 * nd + j)),
            pl.BlockSpec((HY_FH, td), lambda o, j: (0, o * 2 * nd + nd + j)),
            pl.BlockSpec((L, td), lambda o, j: (0, j)),
            full((2 * L, L)),
        ],
        out_specs=pl.BlockSpec((2 * L, td), lambda o, j: (0, o * nd + j)),
        out_shape=jax.ShapeDtypeStruct((2 * L, HY_ORDER * D_MODEL), F32),
        scratch_shapes=[pltpu.VMEM((L, HY_FH), F32)],
        compiler_params=_cparams(2),
    )(feats, w1, vec(b1), vec(fr1), w2, vec(b2), vec(fr2), w3, w3, decay, fwd_bf)


def _spectral_conv(u, fwd, inv, kspec, L):
    uf = jnp.dot(fwd, u.astype(BF16), preferred_element_type=F32)
    ua, ub = uf[0:L], uf[L:2 * L]
    ka, kb = kspec[0:L], kspec[L:2 * L]
    first = lax.broadcasted_iota(jnp.int32, ua.shape, 0) == 0
    ya = ua * ka - jnp.where(first, 0.0, ub * kb)
    yb = jnp.where(first, ub * kb, ua * kb + ub * ka)
    y = jnp.concatenate([ya, yb], axis=0).astype(BF16)
    return jnp.dot(inv, y, preferred_element_type=F32)


def _hyena_kernel(L, zv_ref, z1_ref, z2_ref, cwv_ref, cw1_ref, cw2_ref, cbv_ref, cb1_ref, cb2_ref,
                  fwd_ref, inv_ref, k0_ref, k1_ref, bias0_ref, bias1_ref, o_ref):
    fwd = fwd_ref[...]
    inv = inv_ref[...]
    v = _dwconv3(zv_ref[...], cwv_ref[...], cbv_ref[...])
    x1 = _dwconv3(z1_ref[...], cw1_ref[...], cb1_ref[...])
    x2 = _dwconv3(z2_ref[...], cw2_ref[...], cb2_ref[...])
    z = x1 * (_spectral_conv(v, fwd, inv, k0_ref[...], L) + v * bias0_ref[...])
    o_ref[...] = x2 * (_spectral_conv(z, fwd, inv, k1_ref[...], L) + z * bias1_ref[...])


def _hyena_core(zproj, conv_w, conv_b, fwd_bf, inv_bf, kspec, bias, *, seq, nbatch, row_off, td):
    nd = D_MODEL // td
    zblk = lambda part: pl.BlockSpec((seq, td), lambda b, j, part=part: (row_off + b, part * nd + j))
    cwblk = lambda part: pl.BlockSpec((3, td), lambda b, j, part=part: (0, part * nd + j))
    cbblk = lambda part: pl.BlockSpec((1, td), lambda b, j, part=part: (0, part * nd + j))
    return pl.pallas_call(
        functools.partial(_hyena_kernel, seq),
        grid=(nbatch, nd),
        in_specs=[
            zblk(0), zblk(1), zblk(2), cwblk(0), cwblk(1), cwblk(2), cbblk(0), cbblk(1), cbblk(2),
            pl.BlockSpec((2 * seq, seq), lambda b, j: (0, 0), pipeline_mode=pl.Buffered(1)),
            pl.BlockSpec((seq, 2 * seq), lambda b, j: (0, 0), pipeline_mode=pl.Buffered(1)),
            pl.BlockSpec((2 * seq, td), lambda b, j: (0, j)),
            pl.BlockSpec((2 * seq, td), lambda b, j: (0, nd + j)),
            pl.BlockSpec((None, 1, td), lambda b, j: (0, 0, j)),
            pl.BlockSpec((None, 1, td), lambda b, j: (1, 0, j)),
        ],
        out_specs=pl.BlockSpec((seq, td), lambda b, j: (b, j)),
        out_shape=jax.ShapeDtypeStruct((nbatch * seq, D_MODEL), F32),
        compiler_params=_cparams(2),
    )(zproj, zproj, zproj, conv_w, conv_w, conv_w, conv_b, conv_b, conv_b,
      fwd_bf, inv_bf, kspec, kspec, bias.reshape(HY_ORDER, 1, D_MODEL), bias.reshape(HY_ORDER, 1, D_MODEL))


def _router_kernel(x_ref, ada_ref, g_ref, rw_ref, rb_ref, h_ref, idx_ref, wt_ref):
    i = pl.program_id(0)
    row = _cond_row(i)
    h = _modulate(x_ref[...], g_ref[...], _ada_chunk(ada_ref, row, 3), _ada_chunk(ada_ref, row, 4))
    h_ref[...] = h.astype(BF16)
    logits = lax.dot_general(rw_ref[...], h, (((1,), (1,)), ((), ())), precision=HIGHEST,
                             preferred_element_type=F32) + rb_ref[...]
    expert = lax.broadcasted_iota(jnp.int32, logits.shape, 0)
    slot = lax.broadcasted_iota(jnp.int32, (TOP_K, logits.shape[1]), 0)
    vals = jnp.zeros((TOP_K, logits.shape[1]), F32)
    idxs = jnp.zeros((TOP_K, logits.shape[1]), jnp.int32)
    cur = logits
    for k in range(TOP_K):
        m = jnp.max(cur, axis=0, keepdims=True)
        a = jnp.min(jnp.where(cur == m, expert, N_EXPERTS), axis=0, keepdims=True)
        vals = jnp.where(slot == k, m, vals)
        idxs = jnp.where(slot == k, a, idxs)
        cur = jnp.where(expert == a, -jnp.inf, cur)
    e = jnp.exp(vals - vals[0:1])
    wt_ref[...] = e / jnp.sum(e, axis=0, keepdims=True)
    idx_ref[...] = idxs


def _router(y, ada_l, g, router_w, router_b):
    return pl.pallas_call(
        _router_kernel,
        grid=(N_ROW_TILES,),
        in_specs=[
            pl.BlockSpec((ROW_TILE, D_MODEL), lambda i: (i, 0)),
            pl.BlockSpec((COND_ROWS, ADA_CHUNKS * D_MODEL), lambda i: (0, 0)),
            pl.BlockSpec((1, D_MODEL), lambda i: (0, 0)),
            pl.BlockSpec((N_EXPERTS, D_MODEL), lambda i: (0, 0)),
            pl.BlockSpec((N_EXPERTS, 1), lambda i: (0, 0)),
        ],
        out_specs=[
            pl.BlockSpec((ROW_TILE, D_MODEL), lambda i: (i, 0)),
            pl.BlockSpec((TOP_K, ROW_TILE), lambda i: (0, i)),
            pl.BlockSpec((TOP_K, ROW_TILE), lambda i: (0, i)),
        ],
        out_shape=[
            jax.ShapeDtypeStruct((T_ALL, D_MODEL), BF16),
            jax.ShapeDtypeStruct((TOP_K, T_ALL), jnp.int32),
            jax.ShapeDtypeStruct((TOP_K, T_ALL), F32),
        ],
        compiler_params=_cparams(1),
    )(y, ada_l, g.reshape(1, D_MODEL), router_w.T, router_b.reshape(N_EXPERTS, 1))


def _deinterleave_matrix():
    s = np.zeros((256, 256), np.float32)
    j = np.arange(128)
    s[2 * j, j] = 1.0
    s[2 * j + 1, 128 + j] = 1.0
    return jnp.asarray(s)


def _weight_copies(layer, e, w1_hbm, w2_hbm, w1s_ref, w2s_ref, sem):
    copies = []
    r1 = D_MODEL // W1_DMA_CHUNKS
    for c in range(W1_DMA_CHUNKS):
        copies.append(pltpu.make_async_copy(w1_hbm.at[layer, e, pl.ds(c * r1, r1)],
                                            w1s_ref.at[pl.ds(c * r1, r1)], sem.at[c]))
    r2 = D_FF // W2_DMA_CHUNKS
    for c in range(W2_DMA_CHUNKS):
        copies.append(pltpu.make_async_copy(w2_hbm.at[layer, e, pl.ds(c * r2, r2)],
                                            w2s_ref.at[pl.ds(c * r2, r2)], sem.at[W1_DMA_CHUNKS + c]))
    return copies


def _expert_kernel(layer, te_ref, tf_ref, ne_ref, nu_ref, src_ref, x_ref, b1_ref, b2_ref, wt_ref, s_ref,
                   w1_hbm, w2_hbm, o_hbm, w1s_ref, w2s_ref, w1p_ref, w2p_ref, acc_ref, out_ref, wsem, osem):
    i = pl.program_id(0)
    half = 128
    copies = functools.partial(_weight_copies, layer, w1_hbm=w1_hbm, w2_hbm=w2_hbm,
                               w1s_ref=w1s_ref, w2s_ref=w2s_ref, sem=wsem)

    @pl.when(i == 0)
    def _():
        acc_ref[...] = jnp.zeros_like(acc_ref)
        out_ref[...] = jnp.zeros_like(out_ref)
        for cp in copies(te_ref[0]):
            cp.start()

    @pl.when(tf_ref[i] == 1)
    def _():
        for cp in copies(te_ref[i]):
            cp.wait()
        s = s_ref[...].astype(BF16)
        for c in range(2 * D_FF // 256):
            blk = jnp.dot(w1s_ref[:, c * 256:(c + 1) * 256].astype(BF16), s, preferred_element_type=F32)
            w1p_ref[:, c * half:(c + 1) * half] = blk[:, :half].astype(BF16)
            w1p_ref[:, D_FF + c * half:D_FF + (c + 1) * half] = blk[:, half:].astype(BF16)
        w2p_ref[...] = w2s_ref[...].astype(BF16)

        @pl.when(ne_ref[i] >= 0)
        def _():
            for cp in copies(ne_ref[i]):
                cp.start()

    @pl.when(i <= nu_ref[0])
    def _():
        base = i * MOE_TILE
        prev = (i + 1) % 2
        for r0 in range(0, MOE_TILE, SCATTER_GROUP):
            toks = [src_ref[base + r0 + g] for g in range(SCATTER_GROUP)]
            cur = [acc_ref[pl.ds(toks[g], 1), :] for g in range(SCATTER_GROUP)]
            add = [out_ref[prev, r0 + g:r0 + g + 1, :] for g in range(SCATTER_GROUP)]
            for g in range(SCATTER_GROUP):
                acc_ref[pl.ds(toks[g], 1), :] = cur[g] + add[g]
        a = jnp.dot(x_ref[...], w1p_ref[...], preferred_element_type=F32) + b1_ref[...]
        glu = jnp.minimum(a[:, :D_FF], SWIGLU_LIMIT)
        lin = jnp.clip(a[:, D_FF:], -SWIGLU_LIMIT, SWIGLU_LIMIT)
        hid = glu * _sigmoid(SWIGLU_ALPHA * glu) * (lin + 1.0)
        out = jnp.dot(hid.astype(BF16), w2p_ref[...], preferred_element_type=F32) + b2_ref[...]
        out_ref[i % 2] = out * wt_ref[...]

    @pl.when(i == pl.num_programs(0) - 1)
    def _():
        cp = pltpu.make_async_copy(acc_ref.at[pl.ds(0, T_ALL)], o_hbm, osem)
        cp.start()
        cp.wait()


def _experts(layer, x_sorted, w_sorted, plan, w1, b1p, w2, b2):
    tile_expert, tile_first, next_expert, n_used, src = plan
    grid_spec = pltpu.PrefetchScalarGridSpec(
        num_scalar_prefetch=5,
        grid=(MOE_TILES,),
        in_specs=[
            pl.BlockSpec((MOE_TILE, D_MODEL), lambda i, te, *_: (i, 0)),
            pl.BlockSpec((None, None, 1, 2 * D_FF), lambda i, te, *_: (layer, te[i], 0, 0)),
            pl.BlockSpec((None, None, 1, D_MODEL), lambda i, te, *_: (layer, te[i], 0, 0)),
            pl.BlockSpec((MOE_TILE, 1), lambda i, te, *_: (i, 0)),
            pl.BlockSpec((256, 256), lambda i, te, *_: (0, 0)),
            pl.BlockSpec(memory_space=pl.ANY),
            pl.BlockSpec(memory_space=pl.ANY),
        ],
        out_specs=pl.BlockSpec(memory_space=pl.ANY),
        scratch_shapes=[
            pltpu.VMEM((D_MODEL, 2 * D_FF), F32),
            pltpu.VMEM((D_FF, D_MODEL), F32),
            pltpu.VMEM((D_MODEL, 2 * D_FF), BF16),
            pltpu.VMEM((D_FF, D_MODEL), BF16),
            pltpu.VMEM((ACC_ROWS, D_MODEL), F32),
            pltpu.VMEM((2, MOE_TILE, D_MODEL), F32),
            pltpu.SemaphoreType.DMA((W1_DMA_CHUNKS + W2_DMA_CHUNKS,)),
            pltpu.SemaphoreType.DMA(()),
        ],
    )
    return pl.pallas_call(
        functools.partial(_expert_kernel, layer),
        grid_spec=grid_spec,
        out_shape=jax.ShapeDtypeStruct((T_ALL, D_MODEL), F32),
        compiler_params=_cparams(1),
    )(tile_expert, tile_first, next_expert, n_used, src, x_sorted, b1p, b2, w_sorted,
      _deinterleave_matrix(), w1, w2)


def _combine_kernel(first_tile, y_ref, a_ref, ada_ref, o_ref):
    gate = _ada_chunk(ada_ref, _cond_row(first_tile + pl.program_id(0)), 5)
    o_ref[...] = y_ref[...] + gate * a_ref[...]


def _combine(y, acc, ada_l, first_tile=0, n_tiles=N_ROW_TILES):
    return pl.pallas_call(
        functools.partial(_combine_kernel, first_tile),
        grid=(n_tiles,),
        in_specs=[
            pl.BlockSpec((ROW_TILE, D_MODEL), lambda i: (first_tile + i, 0)),
            pl.BlockSpec((ROW_TILE, D_MODEL), lambda i: (first_tile + i, 0)),
            pl.BlockSpec((COND_ROWS, ADA_CHUNKS * D_MODEL), lambda i: (0, 0)),
        ],
        out_specs=pl.BlockSpec((ROW_TILE, D_MODEL), lambda i: (i, 0)),
        out_shape=jax.ShapeDtypeStruct((n_tiles * ROW_TILE, D_MODEL), F32),
        compiler_params=_cparams(1),
    )(y, acc, ada_l)


def _routing_plan(idx, wts):
    eid = idx.reshape(-1)
    order = jnp.argsort(eid, stable=True).astype(jnp.int32)
    experts = jnp.arange(N_EXPERTS, dtype=jnp.int32)
    counts = jnp.sum(eid[:, None] == experts[None, :], axis=0).astype(jnp.int32)
    ntiles = (counts + MOE_TILE - 1) // MOE_TILE
    tile_end = jnp.cumsum(ntiles).astype(jnp.int32)
    tile_begin = tile_end - ntiles
    cstarts = (jnp.cumsum(counts) - counts).astype(jnp.int32)
    n_used = tile_end[-1]
    tile = jnp.arange(MOE_TILES, dtype=jnp.int32)
    te = jnp.minimum(jnp.sum(tile[:, None] >= tile_end[None, :], axis=1), N_EXPERTS - 1).astype(jnp.int32)
    used = tile < n_used
    prev = jnp.concatenate([jnp.full((1,), -1, jnp.int32), te[:-1]])
    first = (te != prev) & used
    nxt = tile_end[te]
    next_expert = jnp.where(first & (nxt < n_used), te[jnp.minimum(nxt, MOE_TILES - 1)], -1).astype(jnp.int32)
    off = (tile - tile_begin[te])[:, None] * MOE_TILE + jnp.arange(MOE_TILE, dtype=jnp.int32)[None, :]
    valid = (off < counts[te][:, None]) & used[:, None]
    assign = order[jnp.clip(cstarts[te][:, None] + off, 0, N_ASSIGN - 1)]
    token = assign // TOP_K
    src = jnp.where(valid, token, SPARE_ROW).reshape(MOE_ROWS).astype(jnp.int32)
    src = jnp.concatenate([jnp.full((MOE_TILE,), SPARE_ROW, jnp.int32), src])
    gather_row = jnp.where(valid, token, 0).reshape(MOE_ROWS)
    w_sorted = jnp.where(valid, wts.reshape(-1)[assign], 0.0).reshape(MOE_ROWS, 1)
    plan = (te, first.astype(jnp.int32), next_expert, n_used.reshape(1), src)
    return plan, gather_row, w_sorted


def _moe(layer, y, ada_l, g, router_w, router_b, w1, b1p, w2, b2):
    h, idx_t, wts_t = _router(y, ada_l, g, router_w, router_b)
    plan, gather_row, w_sorted = _routing_plan(idx_t.T, wts_t.T)
    x_sorted = jnp.take(h, gather_row, axis=0, mode="clip")
    acc = _experts(layer, x_sorted, w_sorted, plan, w1, b1p, w2, b2)
    if layer == DEPTH - 1:
        return (_combine(y, acc, ada_l, 0, P_TILES), _combine(y, acc, ada_l, P_TILES, N_ROW_TILES - P_TILES))
    return _combine(y, acc, ada_l)


def kernel(x_prompt, x_sample, cache_attn_k, cache_attn_v, state_mlstm_C, state_mlstm_n, state_mlstm_m, c, c_ctx, ada_w, ada_b, norm_mix_g, norm_ffn_g, ab_w_in, ab_w_out, da_qnorm_g, da_knorm_g, da_lambda, da_subnorm_g, ml_conv_w, ml_conv_b, ml_gate_b, ml_headnorm_g, hy_w_in, hy_w_out, hy_conv_w, hy_conv_b, hy_f_w1, hy_f_b1, hy_f_freq1, hy_f_w2, hy_f_b2, hy_f_freq2, hy_f_w3, hy_bias, router_w, router_b, moe_w1, moe_b1, moe_w2, moe_b2):
    y = jnp.concatenate([x_prompt.reshape(T_P, D_MODEL), x_sample.reshape(T_S, D_MODEL)], axis=0)
    cond = jnp.concatenate([c_ctx[None, :], c, jnp.zeros((COND_ROWS - 1 - DEC_BATCH, D_MODEL), F32)], axis=0)
    ada = _ada_table(cond, ada_w, ada_b)
    b1p = jnp.concatenate([moe_b1[..., 0::2], moe_b1[..., 1::2]], axis=-1).reshape(DEPTH, N_EXPERTS, 1, 2 * D_FF)
    b2r = moe_b2.reshape(DEPTH, N_EXPERTS, 1, D_MODEL)
    new_k, new_v, new_c, new_n, new_m = [], [], [], [], []
    for layer in range(DEPTH):
        ada_l = ada[layer]
        if layer % 2 == 0:
            e = layer // 2
            lam_init = 0.8 - 0.6 * math.exp(-0.3 * layer)
            qkv, mqk, mv, mo, mg = _modulated_proj(
                y, ada_l, norm_mix_g[layer], ab_w_in[e], (3 * W_A, 2 * W_B, W_B, W_B, 4 * H_B))
            qg2 = jnp.tile(da_qnorm_g[e], 2).reshape(1, 2 * HD_A)
            kg2 = jnp.tile(da_knorm_g[e], 2).reshape(1, 2 * HD_A)
            sub_g = da_subnorm_g[e].reshape(1, 2 * HD_A)
            oa_p, k_norm = _attention_prompt(qkv, qg2, kg2, da_lambda[e], sub_g, lam_init)
            cos, sin = _rope_tables()
            oa_s = _attention_sample(
                qkv, cache_attn_k[:, e].reshape(DEC_BATCH, PAST_LEN, W_A),
                cache_attn_v[:, e].reshape(DEC_BATCH, PAST_LEN, W_A), cos, sin,
                qg2, kg2, da_lambda[e], sub_g, lam_init)
            ob_p, c_new, n_new, m_new = _mlstm(
                mqk, mv, mo, mg[:T_P], ml_conv_w[e], ml_conv_b[e].reshape(1, 2 * W_B), ml_gate_b[e],
                ml_headnorm_g[e], seq=SEQ, nbatch=BATCH, row_off=0)
            ob_s = _mlstm(
                mqk, mv, mo, mg[T_P:], ml_conv_w[e], ml_conv_b[e].reshape(1, 2 * W_B), ml_gate_b[e],
                ml_headnorm_g[e], seq=DEC_SEQ, nbatch=DEC_BATCH, row_off=T_P // DEC_SEQ,
                ctx=(state_mlstm_C[:, e], state_mlstm_n[:, e], state_mlstm_m[:, e]))
            y = _out_proj_residual([(oa_p, oa_s), (ob_p, ob_s)], y, ada_l, ab_w_out[e], 2)
            new_k.append(k_norm.reshape(BATCH, SEQ, H_A, 2, HD_A))
            new_v.append(qkv[:T_P, 2 * W_A:].reshape(BATCH, SEQ, H_A, 2 * HD_A))
            new_c.append(c_new)
            new_n.append(n_new.reshape(BATCH, 2, H_B, HD_B))
            new_m.append(m_new[..., 0, 0])
        else:
            o = layer // 2
            (zproj,) = _modulated_proj(y, ada_l, norm_mix_g[layer], hy_w_in[o], (HY_PROJ,))
            cores = []
            for seq, nbatch, row_off, td in ((SEQ, BATCH, 0, 512), (DEC_SEQ, DEC_BATCH, T_P // DEC_SEQ, 256)):
                fwd, inv = _dft_mats(seq)
                fwd_bf, inv_bf = fwd.astype(BF16), inv.astype(BF16)
                kspec = _hyena_filter_spectrum(seq, fwd_bf, hy_f_w1[o], hy_f_b1[o], hy_f_freq1[o], hy_f_w2[o],
                                               hy_f_b2[o], hy_f_freq2[o], hy_f_w3[o])
                cores.append(_hyena_core(zproj, hy_conv_w[o], hy_conv_b[o].reshape(1, HY_PROJ), fwd_bf, inv_bf,
                                         kspec, hy_bias[o], seq=seq, nbatch=nbatch, row_off=row_off, td=td))
            y = _out_proj_residual([tuple(cores)], y, ada_l, hy_w_out[o], 2)
        y = _moe(layer, y, ada_l, norm_ffn_g[layer], router_w[layer], router_b[layer],
                 moe_w1, b1p, moe_w2, b2r)
    y_p = y[0].reshape(BATCH, SEQ, D_MODEL)
    y_s = y[1].reshape(DEC_BATCH, DEC_SEQ, D_MODEL)
    return (y_p, y_s, jnp.stack(new_k, axis=1), jnp.stack(new_v, axis=1), jnp.stack(new_c, axis=1),
            jnp.stack(new_n, axis=1), jnp.stack(new_m, axis=1))
```

```python
import functools
import math

import numpy as np
import jax
import jax.numpy as jnp
from jax import lax
from jax.experimental import pallas as pl
from jax.experimental.pallas import tpu as pltpu

D_MODEL = 1024
BATCH = 16
SEQ = 256
DEPTH = 2
DEC_BATCH = 2
DEC_SEQ = 1024
PAST_LEN = 256
GRID_W = 64
W_A = D_MODEL // 2
HD_A = 64
H_A = W_A // (2 * HD_A)
W_B = D_MODEL - W_A
HD_B = 128
H_B = W_B // HD_B
AB_PROJ = 3 * W_A + 4 * W_B + 4 * H_B
ROPE_BASE = 10000.0
CHUNK = 64
HY_ORDER = 2
HY_PROJ = (HY_ORDER + 1) * D_MODEL
HY_BANDS = 8
HY_FH = 64
HY_TARGET = 1e-2
HY_FAST_PCT = 0.3
HY_SLOW_PCT = 1.5
N_EXPERTS = 32
TOP_K = 4
D_FF = D_MODEL
SWIGLU_ALPHA = 1.702
SWIGLU_LIMIT = 7.0
ADA_CHUNKS = 6
EPS = 1e-6
NEG = -1e30
F32 = jnp.float32
BF16 = jnp.bfloat16

T_P = BATCH * SEQ
T_S = DEC_BATCH * DEC_SEQ
T_ALL = T_P + T_S
ROW_TILE = 256
N_ROW_TILES = T_ALL // ROW_TILE
P_TILES = T_P // ROW_TILE
S_TILES_PER_BATCH = DEC_SEQ // ROW_TILE
COND_ROWS = 8
MOE_TILE = 256
N_ASSIGN = T_ALL * TOP_K
MOE_ROWS = N_ASSIGN + N_EXPERTS * MOE_TILE
MOE_TILES = MOE_ROWS // MOE_TILE
X_ALIGN = 16
X_ROWS = N_ASSIGN + N_EXPERTS * X_ALIGN + MOE_TILE
SPARE_ROW = T_ALL
ACC_ROWS = T_ALL + 8
SCATTER_GROUP = 8
W1_DMA_CHUNKS = 8
W2_DMA_CHUNKS = 4
VMEM_LIMIT = 56 * 1024 * 1024
HIGHEST = lax.Precision.HIGHEST


def _cparams(n_axes):
    return pltpu.CompilerParams(dimension_semantics=("arbitrary",) * n_axes,
                                vmem_limit_bytes=VMEM_LIMIT)


def _bdot(a, b):
    return jnp.dot(a.astype(BF16), b.astype(BF16), preferred_element_type=F32)


def _cond_row(i):
    return jnp.where(i < P_TILES, 0, 1 + (i - P_TILES) // S_TILES_PER_BATCH)


def _ada_chunk(ada_ref, row, j):
    return ada_ref[pl.ds(row, 1), j * D_MODEL:(j + 1) * D_MODEL]


def _modulate(x, g, shift, scale):
    ms = jnp.mean(x * x, axis=-1, keepdims=True)
    return (x * lax.rsqrt(ms + EPS) * g) * (1.0 + scale) + shift


def _sigmoid(x):
    return 1.0 / (1.0 + jnp.exp(-x))


def _silu(x):
    return x * _sigmoid(x)


def _log_sigmoid(x):
    return jnp.minimum(x, 0.0) - jnp.log(1.0 + jnp.exp(-jnp.abs(x)))


def _dwconv3(x, w, b):
    n = x.shape[0]
    row = lax.broadcasted_iota(jnp.int32, x.shape, 0)
    prev = jnp.where(row == 0, 0.0, pltpu.roll(x, 1, 0))
    nxt = jnp.where(row == n - 1, 0.0, pltpu.roll(x, n - 1, 0))
    return prev * w[0:1] + x * w[1:2] + nxt * w[2:3] + b


def _ada_kernel(cond_ref, w_ref, b_ref, o_ref):
    c = _silu(cond_ref[...])
    o_ref[...] = _bdot(c, w_ref[...]) + b_ref[...]


def _ada_table(cond, ada_w, ada_b):
    tn = 1536
    return pl.pallas_call(
        _ada_kernel,
        grid=(DEPTH, ADA_CHUNKS * D_MODEL // tn),
        in_specs=[
            pl.BlockSpec((COND_ROWS, D_MODEL), lambda l, j: (0, 0)),
            pl.BlockSpec((None, D_MODEL, tn), lambda l, j: (l, 0, j)),
            pl.BlockSpec((None, 1, tn), lambda l, j: (l, 0, j)),
        ],
        out_specs=pl.BlockSpec((None, COND_ROWS, tn), lambda l, j: (l, 0, j)),
        out_shape=jax.ShapeDtypeStruct((DEPTH, COND_ROWS, ADA_CHUNKS * D_MODEL), F32),
        compiler_params=_cparams(2),
    )(cond, ada_w, ada_b.reshape(DEPTH, 1, ADA_CHUNKS * D_MODEL))


def _proj_kernel(splits, x_ref, ada_ref, g_ref, w_ref, *rest):
    out_refs, wbf_ref = rest[:-1], rest[-1]
    i = pl.program_id(0)

    @pl.when(i == 0)
    def _():
        wbf_ref[...] = w_ref[...].astype(BF16)

    row = _cond_row(i)
    h = _modulate(x_ref[...], g_ref[...], _ada_chunk(ada_ref, row, 0), _ada_chunk(ada_ref, row, 1))
    h = h.astype(BF16)
    lo = 0
    for o_ref, width in zip(out_refs, splits):
        o_ref[...] = jnp.dot(h, wbf_ref[:, lo:lo + width], preferred_element_type=F32)
        lo += width


def _modulated_proj(y, ada_l, g, w, splits):
    n = w.shape[1]
    return pl.pallas_call(
        functools.partial(_proj_kernel, splits),
        grid=(N_ROW_TILES,),
        in_specs=[
            pl.BlockSpec((ROW_TILE, D_MODEL), lambda i: (i, 0)),
            pl.BlockSpec((COND_ROWS, ADA_CHUNKS * D_MODEL), lambda i: (0, 0)),
            pl.BlockSpec((1, D_MODEL), lambda i: (0, 0)),
            pl.BlockSpec((D_MODEL, n), lambda i: (0, 0), pipeline_mode=pl.Buffered(1)),
        ],
        out_specs=[pl.BlockSpec((ROW_TILE, s), lambda i: (i, 0)) for s in splits],
        out_shape=[jax.ShapeDtypeStruct((T_ALL, s), F32) for s in splits],
        scratch_shapes=[pltpu.VMEM((D_MODEL, n), BF16)],
        compiler_params=_cparams(1),
    )(y, ada_l, g.reshape(1, D_MODEL), w)


def _out_proj_kernel(n_in, gate_chunk, *refs):
    x_refs = refs[:2 * n_in]
    y_ref, ada_ref, w_ref, o_ref, wbf_ref = refs[2 * n_in:]
    i = pl.program_id(0)

    @pl.when(i == 0)
    def _():
        wbf_ref[...] = w_ref[...].astype(BF16)

    acc = None
    lo = 0
    for xp_ref, xs_ref in zip(x_refs[0::2], x_refs[1::2]):
        k = xp_ref.shape[1]
        x = jnp.where(i < P_TILES, xp_ref[...], xs_ref[...])
        part = jnp.dot(x.astype(BF16), wbf_ref[lo:lo + k, :], preferred_element_type=F32)
        acc = part if acc is None else acc + part
        lo += k
    gate = _ada_chunk(ada_ref, _cond_row(i), gate_chunk)
    o_ref[...] = y_ref[...] + gate * acc


def _out_proj_residual(xs, y, ada_l, w, gate_chunk):
    x_specs = []
    for xp, _ in xs:
        x_specs.append(pl.BlockSpec((ROW_TILE, xp.shape[1]), lambda i: (jnp.minimum(i, P_TILES - 1), 0)))
        x_specs.append(pl.BlockSpec((ROW_TILE, xp.shape[1]), lambda i: (jnp.maximum(i - P_TILES, 0), 0)))
    return pl.pallas_call(
        functools.partial(_out_proj_kernel, len(xs), gate_chunk),
        grid=(N_ROW_TILES,),
        in_specs=x_specs + [
            pl.BlockSpec((ROW_TILE, D_MODEL), lambda i: (i, 0)),
            pl.BlockSpec((COND_ROWS, ADA_CHUNKS * D_MODEL), lambda i: (0, 0)),
            pl.BlockSpec((D_MODEL, D_MODEL), lambda i: (0, 0), pipeline_mode=pl.Buffered(1)),
        ],
        out_specs=pl.BlockSpec((ROW_TILE, D_MODEL), lambda i: (i, 0)),
        out_shape=jax.ShapeDtypeStruct((T_ALL, D_MODEL), F32),
        scratch_shapes=[pltpu.VMEM((D_MODEL, D_MODEL), BF16)],
        compiler_params=_cparams(1),
    )(*[a for pair in xs for a in pair], y, ada_l, w)


def _subhead_norm(x, g2):
    lane = lax.broadcasted_iota(jnp.int32, x.shape, 1)
    first = lane < HD_A
    xx = x * x
    s0 = jnp.sum(jnp.where(first, xx, 0.0), axis=-1, keepdims=True)
    s1 = jnp.sum(jnp.where(first, 0.0, xx), axis=-1, keepdims=True)
    r = jnp.where(first, lax.rsqrt(s0 / HD_A + EPS), lax.rsqrt(s1 / HD_A + EPS))
    return x * r * g2


def _rope(x, cos, sin):
    quarter = HD_A // 4
    lane = lax.broadcasted_iota(jnp.int32, x.shape, 1)
    lower = (lane % (2 * quarter)) < quarter
    swapped = jnp.where(lower, pltpu.roll(x, 2 * HD_A - quarter, 1), pltpu.roll(x, quarter, 1))
    return x * cos + swapped * sin


def _attn_kernel(lam_init, has_ctx, *refs):
    if has_ctx:
        (q_ref, k_ref, v_ref, ck_ref, cv_ref, cq_ref, sq_ref, ckk_ref, skk_ref,
         qg_ref, kg_ref, lp_ref, sg_ref, o_ref, kall_ref, vall_ref) = refs
    else:
        q_ref, k_ref, v_ref, qg_ref, kg_ref, lp_ref, sg_ref, o_ref, kn_ref = refs
    q = _subhead_norm(q_ref[...], qg_ref[...])
    if has_ctx:
        q = _rope(q, cq_ref[...], sq_ref[...])

        @pl.when(pl.program_id(2) == 0)
        def _():
            kall_ref[0:PAST_LEN, :] = ck_ref[...].astype(BF16)
            vall_ref[0:PAST_LEN, :] = cv_ref[...].astype(BF16)
            k_new = _rope(_subhead_norm(k_ref[...], kg_ref[...]), ckk_ref[...], skk_ref[...])
            kall_ref[PAST_LEN:, :] = k_new.astype(BF16)
            vall_ref[PAST_LEN:, :] = v_ref[...].astype(BF16)

        k = kall_ref[...]
        v = vall_ref[...]
    else:
        k = _subhead_norm(k_ref[...], kg_ref[...])
        v = v_ref[...]
        kn_ref[...] = k
    lp = lp_ref[...]
    lam = (jnp.exp(jnp.sum(lp[0:1] * lp[1:2], axis=-1, keepdims=True))
           - jnp.exp(jnp.sum(lp[2:3] * lp[3:4], axis=-1, keepdims=True)) + lam_init)
    scale = HD_A ** -0.5
    probs = []
    for c in range(2):
        qc = q[:, c * HD_A:(c + 1) * HD_A].astype(BF16)
        kc = k[:, c * HD_A:(c + 1) * HD_A].astype(BF16)
        s = lax.dot_general(qc, kc, (((1,), (1,)), ((), ())), preferred_element_type=F32) * scale
        e = jnp.exp(s - jnp.max(s, axis=-1, keepdims=True))
        probs.append(e / jnp.sum(e, axis=-1, keepdims=True))
    w = probs[0] - lam * probs[1]
    o = _bdot(w, v)
    ms = jnp.mean(o * o, axis=-1, keepdims=True)
    o_ref[...] = (o * lax.rsqrt(ms + EPS) * sg_ref[...]) * (1.0 - lam_init)


def _attention_prompt(qkv, qg2, kg2, lam_p, sub_g, lam_init):
    nh = H_A
    head = 2 * HD_A
    small = [
        pl.BlockSpec((1, head), lambda b, h: (0, 0)),
        pl.BlockSpec((1, head), lambda b, h: (0, 0)),
        pl.BlockSpec((4, HD_A), lambda b, h: (0, 0)),
        pl.BlockSpec((1, head), lambda b, h: (0, 0)),
    ]
    return pl.pallas_call(
        functools.partial(_attn_kernel, lam_init, False),
        grid=(BATCH, nh),
        in_specs=[
            pl.BlockSpec((SEQ, head), lambda b, h: (b, h)),
            pl.BlockSpec((SEQ, head), lambda b, h: (b, nh + h)),
            pl.BlockSpec((SEQ, head), lambda b, h: (b, 2 * nh + h)),
        ] + small,
        out_specs=[pl.BlockSpec((SEQ, head), lambda b, h: (b, h)),
                   pl.BlockSpec((SEQ, head), lambda b, h: (b, h))],
        out_shape=[jax.ShapeDtypeStruct((T_P, W_A), F32), jax.ShapeDtypeStruct((T_P, W_A), F32)],
        compiler_params=_cparams(2),
    )(qkv, qkv, qkv, qg2, kg2, lam_p, sub_g)


def _attention_sample(qkv, cache_k, cache_v, cos, sin, qg2, kg2, lam_p, sub_g, lam_init):
    nh = H_A
    head = 2 * HD_A
    tq = ROW_TILE
    nq = DEC_SEQ // tq
    q_off = T_P // tq
    k_off = T_P // DEC_SEQ
    small = [
        pl.BlockSpec((1, head), lambda b, h, i: (0, 0)),
        pl.BlockSpec((1, head), lambda b, h, i: (0, 0)),
        pl.BlockSpec((4, HD_A), lambda b, h, i: (0, 0)),
        pl.BlockSpec((1, head), lambda b, h, i: (0, 0)),
    ]
    return pl.pallas_call(
        functools.partial(_attn_kernel, lam_init, True),
        grid=(DEC_BATCH, nh, nq),
        in_specs=[
            pl.BlockSpec((tq, head), lambda b, h, i: (q_off + b * nq + i, h)),
            pl.BlockSpec((DEC_SEQ, head), lambda b, h, i: (k_off + b, nh + h)),
            pl.BlockSpec((DEC_SEQ, head), lambda b, h, i: (k_off + b, 2 * nh + h)),
            pl.BlockSpec((None, PAST_LEN, head), lambda b, h, i: (b, 0, h)),
            pl.BlockSpec((None, PAST_LEN, head), lambda b, h, i: (b, 0, h)),
            pl.BlockSpec((tq, head), lambda b, h, i: (i, 0)),
            pl.BlockSpec((tq, head), lambda b, h, i: (i, 0)),
            pl.BlockSpec((DEC_SEQ, head), lambda b, h, i: (0, 0)),
            pl.BlockSpec((DEC_SEQ, head), lambda b, h, i: (0, 0)),
        ] + small,
        out_specs=pl.BlockSpec((tq, head), lambda b, h, i: (b * nq + i, h)),
        out_shape=jax.ShapeDtypeStruct((T_S, W_A), F32),
        scratch_shapes=[pltpu.VMEM((PAST_LEN + DEC_SEQ, head), BF16)] * 2,
        compiler_params=_cparams(3),
    )(qkv, qkv, qkv, cache_k, cache_v, cos, sin, cos, sin, qg2, kg2, lam_p, sub_g)


def _rope_tables():
    half = HD_A // 2
    nf = half // 2
    inv = ROPE_BASE ** (-np.arange(nf, dtype=np.float32) / nf)
    pos = np.arange(DEC_SEQ)
    row = (pos // GRID_W).astype(np.float32)
    col = (pos % GRID_W).astype(np.float32)
    ang_r = (row[:, None] * inv).astype(np.float32)
    ang_c = (col[:, None] * inv).astype(np.float32)
    ang = np.concatenate([ang_r, ang_r, ang_c, ang_c], axis=1)
    sign = np.concatenate([-np.ones(nf), np.ones(nf), -np.ones(nf), np.ones(nf)]).astype(np.float32)
    cos = np.cos(ang.astype(np.float64)).astype(np.float32)
    sin = (np.sin(ang.astype(np.float64)) * sign).astype(np.float32)
    return jnp.asarray(np.tile(cos, (1, 2))), jnp.asarray(np.tile(sin, (1, 2)))


def _mlstm_kernel(seq, has_ctx, *refs):
    if has_ctx:
        (q_ref, k_ref, cwq_ref, cwk_ref, cbq_ref, cbk_ref, v_ref, mo_ref, gi_ref, gf_ref,
         gbi_ref, gbf_ref, hn_ref, c0_ref, n0_ref, m0_ref, o_ref,
         qs_ref, ks_ref, hf_ref, hb_ref, cs_ref, rrow_ref, col_ref, wc_ref) = refs
    else:
        (q_ref, k_ref, cwq_ref, cwk_ref, cbq_ref, cbk_ref, v_ref, mo_ref, gi_ref, gf_ref,
         gbi_ref, gbf_ref, hn_ref, o_ref, c_out_ref, n_out_ref, m_out_ref,
         qs_ref, ks_ref, hf_ref, hb_ref, cs_ref, rrow_ref, col_ref, wc_ref) = refs
    nc = seq // CHUNK
    n_chain = 2 * H_B
    chains = [(d, h) for d in range(2) for h in range(H_B)]
    qs_ref[...] = _silu(_dwconv3(q_ref[...], cwq_ref[...], cbq_ref[...])) * (HD_B ** -0.5)
    ks_ref[...] = _silu(_dwconv3(k_ref[...], cwk_ref[...], cbk_ref[...]))

    rows = nc * n_chain
    lane = lax.broadcasted_iota(jnp.int32, (rows, 2 * CHUNK), 1)
    forward = lax.broadcasted_iota(jnp.int32, (rows, 2 * CHUNK), 0) % n_chain < H_B
    valid = lane < CHUNK

    def scan(x, op, fill):
        pre, suf = x, x
        sh = 1
        while sh < CHUNK:
            pre = op(pre, jnp.where(lane >= sh, pltpu.roll(pre, sh, 1), fill))
            suf = op(suf, jnp.where(lane + sh < CHUNK, pltpu.roll(suf, 2 * CHUNK - sh, 1), fill))
            sh *= 2
        return jnp.where(forward, pre, suf)

    gate_i = (gi_ref[...] + gbi_ref[...]).reshape(rows, 2 * CHUNK)
    lf = jnp.where(valid, _log_sigmoid(gf_ref[...] + gbf_ref[...]).reshape(rows, 2 * CHUNK), 0.0)
    b = scan(lf, jnp.add, 0.0)
    cmax = scan(jnp.where(valid, gate_i - b, -jnp.inf), jnp.maximum, -jnp.inf)
    b_last = jnp.sum(lf, axis=1, keepdims=True)
    g = b_last - b + gate_i
    g_max = jnp.max(jnp.where(valid, g, -jnp.inf), axis=1, keepdims=True)
    mm = m0_ref[...] if has_ctx else jnp.zeros((n_chain, 1), F32)
    mm_seq = []
    for p in range(nc):
        mm_seq.append(mm)
        seg = slice(p * n_chain, (p + 1) * n_chain)
        mm = jnp.maximum(b_last[seg] + mm, g_max[seg])
    mm_final = mm
    mm_prev = jnp.concatenate(mm_seq, axis=0)
    mm_next = jnp.concatenate(mm_seq[1:] + [mm_final], axis=0)
    m_t = jnp.maximum(b + mm_prev, b + cmax)
    rrow_ref[...] = (b - gate_i).reshape(nc, n_chain, 2 * CHUNK)
    wc_ref[...] = jnp.exp(b_last + mm_prev - mm_next).reshape(nc, n_chain, 1)
    per_row = [b, m_t, jnp.exp(b + mm_prev - m_t), jnp.exp(-m_t), jnp.exp(g - mm_next)]
    for j, arr in enumerate(per_row):
        by_time = arr.T
        for p in range(nc):
            col_ref[p, :, j * n_chain:(j + 1) * n_chain] = by_time[0:CHUNK, p * n_chain:(p + 1) * n_chain]

    t_idx = lax.broadcasted_iota(jnp.int32, (CHUNK, CHUNK), 0)
    s_idx = lax.broadcasted_iota(jnp.int32, (CHUNK, CHUNK), 1)
    for n, (d, h) in enumerate(chains):
        cs_ref[n] = c0_ref[d, h] if has_ctx else jnp.zeros((HD_B, HD_B), F32)

    def out_step(p, n_states):
        cols = col_ref[p]
        rrows = rrow_ref[p]
        wcs = wc_ref[p]
        new_states = []
        for n, (d, h) in enumerate(chains):
            c = p if d == 0 else nc - 1 - p
            r0 = pl.multiple_of(c * CHUNK, CHUNK)
            hcols = slice(h * HD_B, (h + 1) * HD_B)
            qt = qs_ref[pl.ds(r0, CHUNK), hcols]
            kt = ks_ref[pl.ds(r0, CHUNK), hcols]
            vt = v_ref[pl.ds(r0, CHUNK), hcols]
            b_col, m_t, w_inter, e_inv, w_k = (cols[:, j * n_chain + n:j * n_chain + n + 1] for j in range(5))
            mask = (s_idx <= t_idx) if d == 0 else (s_idx >= t_idx)
            decay = jnp.exp(jnp.where(mask, b_col - rrows[n:n + 1, 0:CHUNK], NEG) - m_t)
            qk = lax.dot_general(qt.astype(BF16), kt.astype(BF16), (((1,), (1,)), ((), ())),
                                 preferred_element_type=F32)
            s = qk * decay
            cm = cs_ref[n]
            nm = n_states[n]
            cq = lax.dot_general(qt.astype(BF16), cm.astype(BF16), (((1,), (1,)), ((), ())),
                                 preferred_element_type=F32)
            num = _bdot(s, vt) + w_inter * cq
            nq = jnp.sum(s, axis=-1, keepdims=True) + w_inter * jnp.sum(qt * nm, axis=-1, keepdims=True)
            hdir_ref = hf_ref if d == 0 else hb_ref
            hdir_ref[pl.ds(r0, CHUNK), hcols] = num / jnp.maximum(jnp.abs(nq), e_inv)
            w_c = wcs[n:n + 1, :]
            vw = (vt * w_k).astype(BF16)
            cs_ref[n] = w_c * cm + lax.dot_general(vw, kt.astype(BF16), (((0,), (0,)), ((), ())),
                                                   preferred_element_type=F32)
            new_states.append(w_c * nm + jnp.sum(kt * w_k, axis=0, keepdims=True))
        return tuple(new_states)

    if has_ctx:
        n_init = tuple(n0_ref[d, h] for d, h in chains)
    else:
        n_init = tuple(jnp.zeros((1, HD_B), F32) for _ in chains)
    n_final = lax.fori_loop(0, nc, out_step, n_init)
    if not has_ctx:
        for n, (d, h) in enumerate(chains):
            c_out_ref[d, h] = cs_ref[n]
            n_out_ref[d, h] = n_final[n]
            m_out_ref[d, h] = jnp.broadcast_to(mm_final[n:n + 1, :], (1, HD_B))

    for h in range(H_B):
        hcols = slice(h * HD_B, (h + 1) * HD_B)
        hh = hf_ref[:, hcols] + hb_ref[:, hcols]
        ms = jnp.mean(hh * hh, axis=-1, keepdims=True)
        o_ref[:, hcols] = (hh * lax.rsqrt(ms + EPS) * hn_ref[:, hcols]) * _sigmoid(mo_ref[:, hcols])


def _mlstm(mqk, mv, mo, mg_stream, conv_w, conv_b, gate_b, hn_g, *, seq, nbatch, row_off, ctx=None):
    nh = H_B
    nc = seq // CHUNK
    has_ctx = ctx is not None
    gt = mg_stream.reshape(nbatch, nc, CHUNK, 2, 2, nh).transpose(0, 1, 3, 4, 5, 2)
    pad = ((0, 0), (0, 0), (0, 0), (0, CHUNK))
    gates = [jnp.pad(jnp.concatenate([gt[:, :, 0, j], gt[:, ::-1, 1, j]], axis=2), pad) for j in range(2)]
    gate_bias = [jnp.concatenate([gate_b[0, j], gate_b[1, j]]).reshape(2 * nh, 1) for j in range(2)]
    blk = lambda col: pl.BlockSpec((seq, W_B), lambda b, col=col: (row_off + b, col))
    gate_blk = pl.BlockSpec((None, nc, 2 * nh, 2 * CHUNK), lambda b: (b, 0, 0, 0))
    in_specs = [
        blk(0), blk(1),
        pl.BlockSpec((3, W_B), lambda b: (0, 0)),
        pl.BlockSpec((3, W_B), lambda b: (0, 1)),
        pl.BlockSpec((1, W_B), lambda b: (0, 0)),
        pl.BlockSpec((1, W_B), lambda b: (0, 1)),
        blk(0), blk(0),
        gate_blk, gate_blk,
        pl.BlockSpec((2 * nh, 1), lambda b: (0, 0)),
        pl.BlockSpec((2 * nh, 1), lambda b: (0, 0)),
        pl.BlockSpec((1, W_B), lambda b: (0, 0)),
    ]
    args = [mqk, mqk, conv_w, conv_w, conv_b, conv_b, mv, mo, gates[0], gates[1],
            gate_bias[0], gate_bias[1], hn_g.reshape(1, W_B)]
    o_spec = pl.BlockSpec((seq, W_B), lambda b: (b, 0))
    o_shape = jax.ShapeDtypeStruct((nbatch * seq, W_B), F32)
    state_blk = lambda rows: pl.BlockSpec((None, 2, nh, rows, HD_B), lambda b: (b, 0, 0, 0, 0))
    if has_ctx:
        c0, n0, m0 = ctx
        in_specs += [state_blk(HD_B), state_blk(1), pl.BlockSpec((None, 2 * nh, 1), lambda b: (b, 0, 0))]
        args += [c0, n0.reshape(nbatch, 2, nh, 1, HD_B), m0.reshape(nbatch, 2 * nh, 1)]
        out_specs, out_shape = o_spec, o_shape
    else:
        out_specs = [o_spec, state_blk(HD_B), state_blk(1), state_blk(1)]
        out_shape = [
            o_shape,
            jax.ShapeDtypeStruct((nbatch, 2, nh, HD_B, HD_B), F32),
            jax.ShapeDtypeStruct((nbatch, 2, nh, 1, HD_B), F32),
            jax.ShapeDtypeStruct((nbatch, 2, nh, 1, HD_B), F32),
        ]
    return pl.pallas_call(
        functools.partial(_mlstm_kernel, seq, has_ctx),
        grid=(nbatch,),
        in_specs=in_specs,
        out_specs=out_specs,
        out_shape=out_shape,
        scratch_shapes=[pltpu.VMEM((seq, W_B), F32)] * 4 + [
            pltpu.VMEM((2 * nh, HD_B, HD_B), F32),
            pltpu.VMEM((nc, 2 * nh, 2 * CHUNK), F32),
            pltpu.VMEM((nc, CHUNK, 5 * 2 * nh), F32),
            pltpu.VMEM((nc, 2 * nh, 1), F32),
        ],
        compiler_params=_cparams(1),
    )(*args)


def _dft_mats(L):
    f = np.arange(L)[:, None]
    j = np.arange(L)[None, :]
    ang = 2.0 * np.pi * ((f * j) % (2 * L)) / (2 * L)
    cm = np.cos(ang)
    sm = np.sin(ang)
    alt = (1.0 - 2.0 * (np.arange(L) % 2))
    fwd_b = -sm
    fwd_b[0, :] = alt
    fwd = np.concatenate([cm, fwd_b], axis=0)
    wgt = np.where(np.arange(L) == 0, 1.0, 2.0)[None, :]
    inv_a = cm.T * wgt
    inv_b = -2.0 * sm.T
    inv_b[:, 0] = alt
    inv = np.concatenate([inv_a, inv_b], axis=1) / (2 * L)
    return jnp.asarray(fwd.astype(np.float32)), jnp.asarray(inv.astype(np.float32))


def _hyena_feats(L):
    t = np.linspace(0.0, 1.0, L, dtype=np.float32)
    wpos = (2.0 * math.pi * np.arange(L, dtype=np.float32) / L).astype(np.float32)
    fb = np.linspace(1e-4, HY_BANDS - 1, HY_BANDS, dtype=np.float32)
    z = (wpos[:, None] * fb).astype(np.float32)
    feats = np.concatenate([t[:, None], np.cos(z), -np.sin(z)], axis=-1).astype(np.float32)
    deltas = np.abs(np.linspace(math.log(HY_TARGET) / HY_SLOW_PCT, math.log(HY_TARGET) / HY_FAST_PCT,
                                D_MODEL, dtype=np.float32))
    decay = np.exp(-t[:, None] * deltas).astype(np.float32)
    return jnp.asarray(feats), jnp.asarray(decay)


def _filter_kernel(L, feats_ref, w1_ref, b1_ref, fr1_ref, w2_ref, b2_ref, fr2_ref, w3f_ref, w3b_ref,
                   decay_ref, fwd_ref, o_ref, hdn_ref):
    @pl.when((pl.program_id(0) == 0) & (pl.program_id(1) == 0))
    def _():
        h1 = jnp.sin(fr1_ref[...] * (jnp.dot(feats_ref[...], w1_ref[...], precision=HIGHEST,
                                             preferred_element_type=F32) + b1_ref[...]))
        hdn_ref[...] = jnp.sin(fr2_ref[...] * (jnp.dot(h1, w2_ref[...], precision=HIGHEST,
                                                       preferred_element_type=F32) + b2_ref[...]))

    hdn = hdn_ref[...]
    decay = decay_ref[...]
    f_fwd = jnp.dot(hdn, w3f_ref[...], precision=HIGHEST, preferred_element_type=F32) * decay
    f_bwd = jnp.dot(hdn, w3b_ref[...], precision=HIGHEST, preferred_element_type=F32) * decay
    row = lax.broadcasted_iota(jnp.int32, f_bwd.shape, 0)
    f_bwd = jnp.where(row == 0, 0.0, f_bwd)
    fwd = fwd_ref[...]
    p = _bdot(fwd, f_fwd)
    q = _bdot(fwd, f_bwd)
    first = row == 0
    o_ref[0:L, :] = p[0:L] + q[0:L]
    o_ref[L:2 * L, :] = p[L:2 * L] + jnp.where(first, q[L:2 * L], -q[L:2 * L])


def _hyena_filter_spectrum(L, fwd_bf, w1, b1, fr1, w2, b2, fr2, w3):
    feats, decay = _hyena_feats(L)
    td = 512
    nd = D_MODEL // td
    emb = feats.shape[1]
    vec = lambda a: a.reshape(1, HY_FH)
    full = lambda shape: pl.BlockSpec(shape, lambda o, j: (0, 0))
    return pl.pallas_call(
        functools.partial(_filter_kernel, L),
        grid=(HY_ORDER, nd),
        in_specs=[
            full((L, emb)), full((emb, HY_FH)), full((1, HY_FH)), full((1, HY_FH)),
            full((HY_FH, HY_FH)), full((1, HY_FH)), full((1, HY_FH)),
            pl.BlockSpec((HY_FH, td), lambda o, j: (0, o * 2 * nd + j)),
            pl.BlockSpec((HY_FH, td), lambda o, j: (0, o * 2 * nd + nd + j)),
            pl.BlockSpec((L, td), lambda o, j: (0, j)),
            full((2 * L, L)),
        ],
        out_specs=pl.BlockSpec((2 * L, td), lambda o, j: (0, o * nd + j)),
        out_shape=jax.ShapeDtypeStruct((2 * L, HY_ORDER * D_MODEL), F32),
        scratch_shapes=[pltpu.VMEM((L, HY_FH), F32)],
        compiler_params=_cparams(2),
    )(feats, w1, vec(b1), vec(fr1), w2, vec(b2), vec(fr2), w3, w3, decay, fwd_bf)


def _spectral_conv(u, fwd, inv, kspec, L):
    uf = jnp.dot(fwd, u.astype(BF16), preferred_element_type=F32)
    ua, ub = uf[0:L], uf[L:2 * L]
    ka, kb = kspec[0:L], kspec[L:2 * L]
    first = lax.broadcasted_iota(jnp.int32, ua.shape, 0) == 0
    ya = ua * ka - jnp.where(first, 0.0, ub * kb)
    yb = jnp.where(first, ub * kb, ua * kb + ub * ka)
    y = jnp.concatenate([ya, yb], axis=0).astype(BF16)
    return jnp.dot(inv, y, preferred_element_type=F32)


def _hyena_kernel(L, zv_ref, z1_ref, z2_ref, cwv_ref, cw1_ref, cw2_ref, cbv_ref, cb1_ref, cb2_ref,
                  fwd_ref, inv_ref, k0_ref, k1_ref, bias0_ref, bias1_ref, o_ref):
    fwd = fwd_ref[...]
    inv = inv_ref[...]
    v = _dwconv3(zv_ref[...], cwv_ref[...], cbv_ref[...])
    x1 = _dwconv3(z1_ref[...], cw1_ref[...], cb1_ref[...])
    x2 = _dwconv3(z2_ref[...], cw2_ref[...], cb2_ref[...])
    z = x1 * (_spectral_conv(v, fwd, inv, k0_ref[...], L) + v * bias0_ref[...])
    o_ref[...] = x2 * (_spectral_conv(z, fwd, inv, k1_ref[...], L) + z * bias1_ref[...])


def _hyena_core(zproj, conv_w, conv_b, fwd_bf, inv_bf, kspec, bias, *, seq, nbatch, row_off, td):
    nd = D_MODEL // td
    zblk = lambda part: pl.BlockSpec((seq, td), lambda b, j, part=part: (row_off + b, part * nd + j))
    cwblk = lambda part: pl.BlockSpec((3, td), lambda b, j, part=part: (0, part * nd + j))
    cbblk = lambda part: pl.BlockSpec((1, td), lambda b, j, part=part: (0, part * nd + j))
    return pl.pallas_call(
        functools.partial(_hyena_kernel, seq),
        grid=(nbatch, nd),
        in_specs=[
            zblk(0), zblk(1), zblk(2), cwblk(0), cwblk(1), cwblk(2), cbblk(0), cbblk(1), cbblk(2),
            pl.BlockSpec((2 * seq, seq), lambda b, j: (0, 0), pipeline_mode=pl.Buffered(1)),
            pl.BlockSpec((seq, 2 * seq), lambda b, j: (0, 0), pipeline_mode=pl.Buffered(1)),
            pl.BlockSpec((2 * seq, td), lambda b, j: (0, j)),
            pl.BlockSpec((2 * seq, td), lambda b, j: (0, nd + j)),
            pl.BlockSpec((None, 1, td), lambda b, j: (0, 0, j)),
            pl.BlockSpec((None, 1, td), lambda b, j: (1, 0, j)),
        ],
        out_specs=pl.BlockSpec((seq, td), lambda b, j: (b, j)),
        out_shape=jax.ShapeDtypeStruct((nbatch * seq, D_MODEL), F32),
        compiler_params=_cparams(2),
    )(zproj, zproj, zproj, conv_w, conv_w, conv_w, conv_b, conv_b, conv_b,
      fwd_bf, inv_bf, kspec, kspec, bias.reshape(HY_ORDER, 1, D_MODEL), bias.reshape(HY_ORDER, 1, D_MODEL))


def _router_kernel(x_ref, ada_ref, g_ref, rw_ref, rb_ref, h_ref, idx_ref, wt_ref):
    i = pl.program_id(0)
    row = _cond_row(i)
    h = _modulate(x_ref[...], g_ref[...], _ada_chunk(ada_ref, row, 3), _ada_chunk(ada_ref, row, 4))
    h_ref[...] = h.astype(BF16)
    logits = lax.dot_general(rw_ref[...], h, (((1,), (1,)), ((), ())), precision=HIGHEST,
                             preferred_element_type=F32) + rb_ref[...]
    expert = lax.broadcasted_iota(jnp.int32, logits.shape, 0)
    slot = lax.broadcasted_iota(jnp.int32, (TOP_K, logits.shape[1]), 0)
    vals = jnp.zeros((TOP_K, logits.shape[1]), F32)
    idxs = jnp.zeros((TOP_K, logits.shape[1]), jnp.int32)
    cur = logits
    for k in range(TOP_K):
        m = jnp.max(cur, axis=0, keepdims=True)
        a = jnp.min(jnp.where(cur == m, expert, N_EXPERTS), axis=0, keepdims=True)
        vals = jnp.where(slot == k, m, vals)
        idxs = jnp.where(slot == k, a, idxs)
        cur = jnp.where(expert == a, -jnp.inf, cur)
    e = jnp.exp(vals - vals[0:1])
    wt_ref[...] = e / jnp.sum(e, axis=0, keepdims=True)
    idx_ref[...] = idxs


def _router(y, ada_l, g, router_w, router_b):
    return pl.pallas_call(
        _router_kernel,
        grid=(N_ROW_TILES,),
        in_specs=[
            pl.BlockSpec((ROW_TILE, D_MODEL), lambda i: (i, 0)),
            pl.BlockSpec((COND_ROWS, ADA_CHUNKS * D_MODEL), lambda i: (0, 0)),
            pl.BlockSpec((1, D_MODEL), lambda i: (0, 0)),
            pl.BlockSpec((N_EXPERTS, D_MODEL), lambda i: (0, 0)),
            pl.BlockSpec((N_EXPERTS, 1), lambda i: (0, 0)),
        ],
        out_specs=[
            pl.BlockSpec((ROW_TILE, D_MODEL), lambda i: (i, 0)),
            pl.BlockSpec((TOP_K, ROW_TILE), lambda i: (0, i)),
            pl.BlockSpec((TOP_K, ROW_TILE), lambda i: (0, i)),
        ],
        out_shape=[
            jax.ShapeDtypeStruct((T_ALL, D_MODEL), BF16),
            jax.ShapeDtypeStruct((TOP_K, T_ALL), jnp.int32),
            jax.ShapeDtypeStruct((TOP_K, T_ALL), F32),
        ],
        compiler_params=_cparams(1),
    )(y, ada_l, g.reshape(1, D_MODEL), router_w.T, router_b.reshape(N_EXPERTS, 1))


def _deinterleave_matrix():
    s = np.zeros((256, 256), np.float32)
    j = np.arange(128)
    s[2 * j, j] = 1.0
    s[2 * j + 1, 128 + j] = 1.0
    return jnp.asarray(s)


def _weight_copies(layer, e, w1_hbm, w2_hbm, w1s_ref, w2s_ref, sem):
    copies = []
    r1 = D_MODEL // W1_DMA_CHUNKS
    for c in range(W1_DMA_CHUNKS):
        copies.append(pltpu.make_async_copy(w1_hbm.at[layer, e, pl.ds(c * r1, r1)],
                                            w1s_ref.at[pl.ds(c * r1, r1)], sem.at[c]))
    r2 = D_FF // W2_DMA_CHUNKS
    for c in range(W2_DMA_CHUNKS):
        copies.append(pltpu.make_async_copy(w2_hbm.at[layer, e, pl.ds(c * r2, r2)],
                                            w2s_ref.at[pl.ds(c * r2, r2)], sem.at[W1_DMA_CHUNKS + c]))
    return copies


def _expert_kernel(layer, te_ref, tf_ref, ne_ref, nu_ref, src_ref, xo_ref, x_ref, b1_ref, b2_ref, wt_ref, s_ref,
                   w1_hbm, w2_hbm, o_hbm, w1s_ref, w2s_ref, w1p_ref, w2p_ref, acc_ref, out_ref, wsem, osem):
    i = pl.program_id(0)
    half = 128
    copies = functools.partial(_weight_copies, layer, w1_hbm=w1_hbm, w2_hbm=w2_hbm,
                               w1s_ref=w1s_ref, w2s_ref=w2s_ref, sem=wsem)

    @pl.when(i == 0)
    def _():
        acc_ref[...] = jnp.zeros_like(acc_ref)
        out_ref[...] = jnp.zeros_like(out_ref)
        for cp in copies(te_ref[0]):
            cp.start()

    @pl.when(tf_ref[i] == 1)
    def _():
        for cp in copies(te_ref[i]):
            cp.wait()
        s = s_ref[...].astype(BF16)
        for c in range(2 * D_FF // 256):
            blk = jnp.dot(w1s_ref[:, c * 256:(c + 1) * 256].astype(BF16), s, preferred_element_type=F32)
            w1p_ref[:, c * half:(c + 1) * half] = blk[:, :half].astype(BF16)
            w1p_ref[:, D_FF + c * half:D_FF + (c + 1) * half] = blk[:, half:].astype(BF16)
        w2p_ref[...] = w2s_ref[...].astype(BF16)

        @pl.when(ne_ref[i] >= 0)
        def _():
            for cp in copies(ne_ref[i]):
                cp.start()

    @pl.when(i <= nu_ref[0])
    def _():
        base = i * MOE_TILE
        prev = (i + 1) % 2
        for r0 in range(0, MOE_TILE, SCATTER_GROUP):
            toks = [src_ref[base + r0 + g] for g in range(SCATTER_GROUP)]
            cur = [acc_ref[pl.ds(toks[g], 1), :] for g in range(SCATTER_GROUP)]
            add = [out_ref[prev, r0 + g:r0 + g + 1, :] for g in range(SCATTER_GROUP)]
            for g in range(SCATTER_GROUP):
                acc_ref[pl.ds(toks[g], 1), :] = cur[g] + add[g]
        a = jnp.dot(x_ref[...], w1p_ref[...], preferred_element_type=F32) + b1_ref[...]
        glu = jnp.minimum(a[:, :D_FF], SWIGLU_LIMIT)
        lin = jnp.clip(a[:, D_FF:], -SWIGLU_LIMIT, SWIGLU_LIMIT)
        hid = glu * _sigmoid(SWIGLU_ALPHA * glu) * (lin + 1.0)
        out = jnp.dot(hid.astype(BF16), w2p_ref[...], preferred_element_type=F32) + b2_ref[...]
        out_ref[i % 2] = out * wt_ref[...]

    @pl.when(i == pl.num_programs(0) - 1)
    def _():
        cp = pltpu.make_async_copy(acc_ref.at[pl.ds(0, T_ALL)], o_hbm, osem)
        cp.start()
        cp.wait()


def _experts(layer, x_sorted, w_sorted, plan, w1, b1p, w2, b2):
    tile_expert, tile_first, next_expert, n_used, src, x_off = plan
    grid_spec = pltpu.PrefetchScalarGridSpec(
        num_scalar_prefetch=6,
        grid=(MOE_TILES,),
        in_specs=[
            pl.BlockSpec((pl.Element(MOE_TILE), pl.Element(D_MODEL)),
                         lambda i, te, tf, ne, nu, src, xo: (pl.multiple_of(xo[i], X_ALIGN), 0)),
            pl.BlockSpec((None, None, 1, 2 * D_FF), lambda i, te, *_: (layer, te[i], 0, 0)),
            pl.BlockSpec((None, None, 1, D_MODEL), lambda i, te, *_: (layer, te[i], 0, 0)),
            pl.BlockSpec((MOE_TILE, 1), lambda i, te, *_: (i, 0)),
            pl.BlockSpec((256, 256), lambda i, te, *_: (0, 0)),
            pl.BlockSpec(memory_space=pl.ANY),
            pl.BlockSpec(memory_space=pl.ANY),
        ],
        out_specs=pl.BlockSpec(memory_space=pl.ANY),
        scratch_shapes=[
            pltpu.VMEM((D_MODEL, 2 * D_FF), F32),
            pltpu.VMEM((D_FF, D_MODEL), F32),
            pltpu.VMEM((D_MODEL, 2 * D_FF), BF16),
            pltpu.VMEM((D_FF, D_MODEL), BF16),
            pltpu.VMEM((ACC_ROWS, D_MODEL), F32),
            pltpu.VMEM((2, MOE_TILE, D_MODEL), F32),
            pltpu.SemaphoreType.DMA((W1_DMA_CHUNKS + W2_DMA_CHUNKS,)),
            pltpu.SemaphoreType.DMA(()),
        ],
    )
    return pl.pallas_call(
        functools.partial(_expert_kernel, layer),
        grid_spec=grid_spec,
        out_shape=jax.ShapeDtypeStruct((T_ALL, D_MODEL), F32),
        compiler_params=_cparams(1),
    )(tile_expert, tile_first, next_expert, n_used, src, x_off, x_sorted, b1p, b2, w_sorted,
      _deinterleave_matrix(), w1, w2)


def _combine_kernel(first_tile, y_ref, a_ref, ada_ref, o_ref):
    gate = _ada_chunk(ada_ref, _cond_row(first_tile + pl.program_id(0)), 5)
    o_ref[...] = y_ref[...] + gate * a_ref[...]


def _combine(y, acc, ada_l, first_tile=0, n_tiles=N_ROW_TILES):
    return pl.pallas_call(
        functools.partial(_combine_kernel, first_tile),
        grid=(n_tiles,),
        in_specs=[
            pl.BlockSpec((ROW_TILE, D_MODEL), lambda i: (first_tile + i, 0)),
            pl.BlockSpec((ROW_TILE, D_MODEL), lambda i: (first_tile + i, 0)),
            pl.BlockSpec((COND_ROWS, ADA_CHUNKS * D_MODEL), lambda i: (0, 0)),
        ],
        out_specs=pl.BlockSpec((ROW_TILE, D_MODEL), lambda i: (i, 0)),
        out_shape=jax.ShapeDtypeStruct((n_tiles * ROW_TILE, D_MODEL), F32),
        compiler_params=_cparams(1),
    )(y, acc, ada_l)


def _routing_plan(idx, wts):
    eid = idx.reshape(-1)
    order = jnp.argsort(eid, stable=True).astype(jnp.int32)
    experts = jnp.arange(N_EXPERTS, dtype=jnp.int32)
    counts = jnp.sum(eid[:, None] == experts[None, :], axis=0).astype(jnp.int32)
    ntiles = (counts + MOE_TILE - 1) // MOE_TILE
    tile_end = jnp.cumsum(ntiles).astype(jnp.int32)
    tile_begin = tile_end - ntiles
    cstarts = (jnp.cumsum(counts) - counts).astype(jnp.int32)
    n_used = tile_end[-1]
    tile = jnp.arange(MOE_TILES, dtype=jnp.int32)
    te = jnp.minimum(jnp.sum(tile[:, None] >= tile_end[None, :], axis=1), N_EXPERTS - 1).astype(jnp.int32)
    used = tile < n_used
    prev = jnp.concatenate([jnp.full((1,), -1, jnp.int32), te[:-1]])
    first = (te != prev) & used
    nxt = tile_end[te]
    next_expert = jnp.where(first & (nxt < n_used), te[jnp.minimum(nxt, MOE_TILES - 1)], -1).astype(jnp.int32)
    off = (tile - tile_begin[te])[:, None] * MOE_TILE + jnp.arange(MOE_TILE, dtype=jnp.int32)[None, :]
    valid = (off < counts[te][:, None]) & used[:, None]
    assign = order[jnp.clip(cstarts[te][:, None] + off, 0, N_ASSIGN - 1)]
    token = assign // TOP_K
    src = jnp.where(valid, token, SPARE_ROW).reshape(MOE_ROWS).astype(jnp.int32)
    src = jnp.concatenate([jnp.full((MOE_TILE,), SPARE_ROW, jnp.int32), src])
    w_sorted = jnp.where(valid, wts.reshape(-1)[assign], 0.0).reshape(MOE_ROWS, 1)
    seg = ((counts + X_ALIGN - 1) // X_ALIGN) * X_ALIGN
    seg_end = jnp.cumsum(seg).astype(jnp.int32)
    seg_begin = seg_end - seg
    x_off = jnp.where(used, seg_begin[te] + (tile - tile_begin[te]) * MOE_TILE, 0).astype(jnp.int32)
    xrow = jnp.arange(X_ROWS, dtype=jnp.int32)
    xe = jnp.minimum(jnp.sum(xrow[:, None] >= seg_end[None, :], axis=1), N_EXPERTS - 1)
    xoffset = xrow - seg_begin[xe]
    xassign = order[jnp.clip(cstarts[xe] + xoffset, 0, N_ASSIGN - 1)]
    gather_row = jnp.where(xoffset < counts[xe], xassign // TOP_K, 0)
    plan = (te, first.astype(jnp.int32), next_expert, n_used.reshape(1), src, x_off)
    return plan, gather_row, w_sorted


def _moe(layer, y, ada_l, g, router_w, router_b, w1, b1p, w2, b2):
    h, idx_t, wts_t = _router(y, ada_l, g, router_w, router_b)
    plan, gather_row, w_sorted = _routing_plan(idx_t.T, wts_t.T)
    x_sorted = jnp.take(h, gather_row, axis=0, mode="clip")
    acc = _experts(layer, x_sorted, w_sorted, plan, w1, b1p, w2, b2)
    if layer == DEPTH - 1:
        return (_combine(y, acc, ada_l, 0, P_TILES), _combine(y, acc, ada_l, P_TILES, N_ROW_TILES - P_TILES))
    return _combine(y, acc, ada_l)


def kernel(x_prompt, x_sample, cache_attn_k, cache_attn_v, state_mlstm_C, state_mlstm_n, state_mlstm_m, c, c_ctx, ada_w, ada_b, norm_mix_g, norm_ffn_g, ab_w_in, ab_w_out, da_qnorm_g, da_knorm_g, da_lambda, da_subnorm_g, ml_conv_w, ml_conv_b, ml_gate_b, ml_headnorm_g, hy_w_in, hy_w_out, hy_conv_w, hy_conv_b, hy_f_w1, hy_f_b1, hy_f_freq1, hy_f_w2, hy_f_b2, hy_f_freq2, hy_f_w3, hy_bias, router_w, router_b, moe_w1, moe_b1, moe_w2, moe_b2):
    y = jnp.concatenate([x_prompt.reshape(T_P, D_MODEL), x_sample.reshape(T_S, D_MODEL)], axis=0)
    cond = jnp.concatenate([c_ctx[None, :], c, jnp.zeros((COND_ROWS - 1 - DEC_BATCH, D_MODEL), F32)], axis=0)
    ada = _ada_table(cond, ada_w, ada_b)
    b1p = moe_b1.reshape(DEPTH, N_EXPERTS, D_FF, 2).swapaxes(2, 3).reshape(DEPTH, N_EXPERTS, 1, 2 * D_FF)
    b2r = moe_b2.reshape(DEPTH, N_EXPERTS, 1, D_MODEL)
    new_k, new_v, new_c, new_n, new_m = [], [], [], [], []
    for layer in range(DEPTH):
        ada_l = ada[layer]
        if layer % 2 == 0:
            e = layer // 2
            lam_init = 0.8 - 0.6 * math.exp(-0.3 * layer)
            qkv, mqk, mv, mo, mg = _modulated_proj(
                y, ada_l, norm_mix_g[layer], ab_w_in[e], (3 * W_A, 2 * W_B, W_B, W_B, 4 * H_B))
            qg2 = jnp.tile(da_qnorm_g[e], 2).reshape(1, 2 * HD_A)
            kg2 = jnp.tile(da_knorm_g[e], 2).reshape(1, 2 * HD_A)
            sub_g = da_subnorm_g[e].reshape(1, 2 * HD_A)
            oa_p, k_norm = _attention_prompt(qkv, qg2, kg2, da_lambda[e], sub_g, lam_init)
            cos, sin = _rope_tables()
            oa_s = _attention_sample(
                qkv, cache_attn_k[:, e].reshape(DEC_BATCH, PAST_LEN, W_A),
                cache_attn_v[:, e].reshape(DEC_BATCH, PAST_LEN, W_A), cos, sin,
                qg2, kg2, da_lambda[e], sub_g, lam_init)
            ob_p, c_new, n_new, m_new = _mlstm(
                mqk, mv, mo, mg[:T_P], ml_conv_w[e], ml_conv_b[e].reshape(1, 2 * W_B), ml_gate_b[e],
                ml_headnorm_g[e], seq=SEQ, nbatch=BATCH, row_off=0)
            ob_s = _mlstm(
                mqk, mv, mo, mg[T_P:], ml_conv_w[e], ml_conv_b[e].reshape(1, 2 * W_B), ml_gate_b[e],
                ml_headnorm_g[e], seq=DEC_SEQ, nbatch=DEC_BATCH, row_off=T_P // DEC_SEQ,
                ctx=(state_mlstm_C[:, e], state_mlstm_n[:, e], state_mlstm_m[:, e]))
            y = _out_proj_residual([(oa_p, oa_s), (ob_p, ob_s)], y, ada_l, ab_w_out[e], 2)
            new_k.append(k_norm.reshape(BATCH, SEQ, H_A, 2, HD_A))
            new_v.append(qkv[:T_P, 2 * W_A:].reshape(BATCH, SEQ, H_A, 2 * HD_A))
            new_c.append(c_new)
            new_n.append(n_new.reshape(BATCH, 2, H_B, HD_B))
            new_m.append(m_new[..., 0, 0])
        else:
            o = layer // 2
            (zproj,) = _modulated_proj(y, ada_l, norm_mix_g[layer], hy_w_in[o], (HY_PROJ,))
            cores = []
            for seq, nbatch, row_off, td in ((SEQ, BATCH, 0, 512), (DEC_SEQ, DEC_BATCH, T_P // DEC_SEQ, 256)):
                fwd, inv = _dft_mats(seq)
                fwd_bf, inv_bf = fwd.astype(BF16), inv.astype(BF16)
                kspec = _hyena_filter_spectrum(seq, fwd_bf, hy_f_w1[o], hy_f_b1[o], hy_f_freq1[o], hy_f_w2[o],
                                               hy_f_b2[o], hy_f_freq2[o], hy_f_w3[o])
                cores.append(_hyena_core(zproj, hy_conv_w[o], hy_conv_b[o].reshape(1, HY_PROJ), fwd_bf, inv_bf,
                                         kspec, hy_bias[o], seq=seq, nbatch=nbatch, row_off=row_off, td=td))
            y = _out_proj_residual([tuple(cores)], y, ada_l, hy_w_out[o], 2)
        y = _moe(layer, y, ada_l, norm_ffn_g[layer], router_w[layer], router_b[layer],
                 moe_w1, b1p, moe_w2, b2r)
    y_p = y[0].reshape(BATCH, SEQ, D_MODEL)
    y_s = y[1].reshape(DEC_BATCH, DEC_SEQ, D_MODEL)
    return (y_p, y_s, jnp.stack(new_k, axis=1), jnp.stack(new_v, axis=1), jnp.stack(new_c, axis=1),
            jnp.stack(new_n, axis=1), jnp.stack(new_m, axis=1))
```

```python
import functools
import math

import numpy as np
import jax
import jax.numpy as jnp
from jax import lax
from jax.experimental import pallas as pl
from jax.experimental.pallas import tpu as pltpu

D_MODEL = 1024
BATCH = 16
SEQ = 256
DEPTH = 2
DEC_BATCH = 2
DEC_SEQ = 1024
PAST_LEN = 256
GRID_W = 64
W_A = D_MODEL // 2
HD_A = 64
H_A = W_A // (2 * HD_A)
W_B = D_MODEL - W_A
HD_B = 128
H_B = W_B // HD_B
AB_PROJ = 3 * W_A + 4 * W_B + 4 * H_B
ROPE_BASE = 10000.0
CHUNK = 64
HY_ORDER = 2
HY_PROJ = (HY_ORDER + 1) * D_MODEL
HY_BANDS = 8
HY_FH = 64
HY_TARGET = 1e-2
HY_FAST_PCT = 0.3
HY_SLOW_PCT = 1.5
N_EXPERTS = 32
TOP_K = 4
D_FF = D_MODEL
SWIGLU_ALPHA = 1.702
SWIGLU_LIMIT = 7.0
ADA_CHUNKS = 6
EPS = 1e-6
NEG = -1e30
F32 = jnp.float32
BF16 = jnp.bfloat16

T_P = BATCH * SEQ
T_S = DEC_BATCH * DEC_SEQ
T_ALL = T_P + T_S
ROW_TILE = 256
N_ROW_TILES = T_ALL // ROW_TILE
P_TILES = T_P // ROW_TILE
S_TILES_PER_BATCH = DEC_SEQ // ROW_TILE
COND_ROWS = 8
MOE_TILE = 256
N_ASSIGN = T_ALL * TOP_K
MOE_ROWS = N_ASSIGN + N_EXPERTS * MOE_TILE
MOE_TILES = MOE_ROWS // MOE_TILE
X_ALIGN = 16
X_ROWS = N_ASSIGN + N_EXPERTS * X_ALIGN + MOE_TILE
SPARE_ROW = T_ALL
ACC_ROWS = T_ALL + 8
SCATTER_GROUP = 8
W1_DMA_CHUNKS = 8
W2_DMA_CHUNKS = 4
VMEM_LIMIT = 56 * 1024 * 1024
HIGHEST = lax.Precision.HIGHEST


def _cparams(n_axes):
    return pltpu.CompilerParams(dimension_semantics=("arbitrary",) * n_axes,
                                vmem_limit_bytes=VMEM_LIMIT)


def _bdot(a, b):
    return jnp.dot(a.astype(BF16), b.astype(BF16), preferred_element_type=F32)


def _cond_row(i):
    return jnp.where(i < P_TILES, 0, 1 + (i - P_TILES) // S_TILES_PER_BATCH)


def _ada_chunk(ada_ref, row, j):
    return ada_ref[pl.ds(row, 1), j * D_MODEL:(j + 1) * D_MODEL]


def _modulate(x, g, shift, scale):
    ms = jnp.mean(x * x, axis=-1, keepdims=True)
    return (x * lax.rsqrt(ms + EPS) * g) * (1.0 + scale) + shift


def _sigmoid(x):
    return 1.0 / (1.0 + jnp.exp(-x))


def _silu(x):
    return x * _sigmoid(x)


def _log_sigmoid(x):
    return jnp.minimum(x, 0.0) - jnp.log(1.0 + jnp.exp(-jnp.abs(x)))


def _dwconv3(x, w, b):
    n = x.shape[0]
    row = lax.broadcasted_iota(jnp.int32, x.shape, 0)
    prev = jnp.where(row == 0, 0.0, pltpu.roll(x, 1, 0))
    nxt = jnp.where(row == n - 1, 0.0, pltpu.roll(x, n - 1, 0))
    return prev * w[0:1] + x * w[1:2] + nxt * w[2:3] + b


def _ada_kernel(cond_ref, w_ref, b_ref, o_ref):
    c = _silu(cond_ref[...])
    o_ref[...] = _bdot(c, w_ref[...]) + b_ref[...]


def _ada_table(cond, ada_w, ada_b):
    tn = 1536
    return pl.pallas_call(
        _ada_kernel,
        grid=(DEPTH, ADA_CHUNKS * D_MODEL // tn),
        in_specs=[
            pl.BlockSpec((COND_ROWS, D_MODEL), lambda l, j: (0, 0)),
            pl.BlockSpec((None, D_MODEL, tn), lambda l, j: (l, 0, j)),
            pl.BlockSpec((None, 1, tn), lambda l, j: (l, 0, j)),
        ],
        out_specs=pl.BlockSpec((None, COND_ROWS, tn), lambda l, j: (l, 0, j)),
        out_shape=jax.ShapeDtypeStruct((DEPTH, COND_ROWS, ADA_CHUNKS * D_MODEL), F32),
        compiler_params=_cparams(2),
    )(cond, ada_w, ada_b.reshape(DEPTH, 1, ADA_CHUNKS * D_MODEL))


def _proj_kernel(splits, x_ref, ada_ref, g_ref, w_ref, *rest):
    out_refs, wbf_ref = rest[:-1], rest[-1]
    i = pl.program_id(0)

    @pl.when(i == 0)
    def _():
        wbf_ref[...] = w_ref[...].astype(BF16)

    row = _cond_row(i)
    h = _modulate(x_ref[...], g_ref[...], _ada_chunk(ada_ref, row, 0), _ada_chunk(ada_ref, row, 1))
    h = h.astype(BF16)
    lo = 0
    for o_ref, width in zip(out_refs, splits):
        o_ref[...] = jnp.dot(h, wbf_ref[:, lo:lo + width], preferred_element_type=F32)
        lo += width


def _modulated_proj(y, ada_l, g, w, splits):
    n = w.shape[1]
    return pl.pallas_call(
        functools.partial(_proj_kernel, splits),
        grid=(N_ROW_TILES,),
        in_specs=[
            pl.BlockSpec((ROW_TILE, D_MODEL), lambda i: (i, 0)),
            pl.BlockSpec((COND_ROWS, ADA_CHUNKS * D_MODEL), lambda i: (0, 0)),
            pl.BlockSpec((1, D_MODEL), lambda i: (0, 0)),
            pl.BlockSpec((D_MODEL, n), lambda i: (0, 0), pipeline_mode=pl.Buffered(1)),
        ],
        out_specs=[pl.BlockSpec((ROW_TILE, s), lambda i: (i, 0)) for s in splits],
        out_shape=[jax.ShapeDtypeStruct((T_ALL, s), F32) for s in splits],
        scratch_shapes=[pltpu.VMEM((D_MODEL, n), BF16)],
        compiler_params=_cparams(1),
    )(y, ada_l, g.reshape(1, D_MODEL), w)


def _out_proj_kernel(n_in, gate_chunk, *refs):
    x_refs = refs[:2 * n_in]
    y_ref, ada_ref, w_ref, o_ref, wbf_ref = refs[2 * n_in:]
    i = pl.program_id(0)

    @pl.when(i == 0)
    def _():
        wbf_ref[...] = w_ref[...].astype(BF16)

    acc = None
    lo = 0
    for xp_ref, xs_ref in zip(x_refs[0::2], x_refs[1::2]):
        k = xp_ref.shape[1]
        x = jnp.where(i < P_TILES, xp_ref[...], xs_ref[...])
        part = jnp.dot(x.astype(BF16), wbf_ref[lo:lo + k, :], preferred_element_type=F32)
        acc = part if acc is None else acc + part
        lo += k
    gate = _ada_chunk(ada_ref, _cond_row(i), gate_chunk)
    o_ref[...] = y_ref[...] + gate * acc


def _out_proj_residual(xs, y, ada_l, w, gate_chunk):
    x_specs = []
    for xp, _ in xs:
        x_specs.append(pl.BlockSpec((ROW_TILE, xp.shape[1]), lambda i: (jnp.minimum(i, P_TILES - 1), 0)))
        x_specs.append(pl.BlockSpec((ROW_TILE, xp.shape[1]), lambda i: (jnp.maximum(i - P_TILES, 0), 0)))
    return pl.pallas_call(
        functools.partial(_out_proj_kernel, len(xs), gate_chunk),
        grid=(N_ROW_TILES,),
        in_specs=x_specs + [
            pl.BlockSpec((ROW_TILE, D_MODEL), lambda i: (i, 0)),
            pl.BlockSpec((COND_ROWS, ADA_CHUNKS * D_MODEL), lambda i: (0, 0)),
            pl.BlockSpec((D_MODEL, D_MODEL), lambda i: (0, 0), pipeline_mode=pl.Buffered(1)),
        ],
        out_specs=pl.BlockSpec((ROW_TILE, D_MODEL), lambda i: (i, 0)),
        out_shape=jax.ShapeDtypeStruct((T_ALL, D_MODEL), F32),
        scratch_shapes=[pltpu.VMEM((D_MODEL, D_MODEL), BF16)],
        compiler_params=_cparams(1),
    )(*[a for pair in xs for a in pair], y, ada_l, w)


def _subhead_norm(x, g2):
    lane = lax.broadcasted_iota(jnp.int32, x.shape, 1)
    first = lane < HD_A
    xx = x * x
    s0 = jnp.sum(jnp.where(first, xx, 0.0), axis=-1, keepdims=True)
    s1 = jnp.sum(jnp.where(first, 0.0, xx), axis=-1, keepdims=True)
    r = jnp.where(first, lax.rsqrt(s0 / HD_A + EPS), lax.rsqrt(s1 / HD_A + EPS))
    return x * r * g2


def _rope(x, cos, sin):
    quarter = HD_A // 4
    lane = lax.broadcasted_iota(jnp.int32, x.shape, 1)
    lower = (lane % (2 * quarter)) < quarter
    swapped = jnp.where(lower, pltpu.roll(x, 2 * HD_A - quarter, 1), pltpu.roll(x, quarter, 1))
    return x * cos + swapped * sin


def _attn_kernel(lam_init, has_ctx, *refs):
    if has_ctx:
        (q_ref, k_ref, v_ref, ck_ref, cv_ref, cq_ref, sq_ref, ckk_ref, skk_ref,
         qg_ref, kg_ref, lp_ref, sg_ref, o_ref, kall_ref, vall_ref) = refs
    else:
        q_ref, k_ref, v_ref, qg_ref, kg_ref, lp_ref, sg_ref, o_ref, kn_ref = refs
    q = _subhead_norm(q_ref[...], qg_ref[...])
    if has_ctx:
        q = _rope(q, cq_ref[...], sq_ref[...])

        @pl.when(pl.program_id(2) == 0)
        def _():
            kall_ref[0:PAST_LEN, :] = ck_ref[...].astype(BF16)
            vall_ref[0:PAST_LEN, :] = cv_ref[...].astype(BF16)
            k_new = _rope(_subhead_norm(k_ref[...], kg_ref[...]), ckk_ref[...], skk_ref[...])
            kall_ref[PAST_LEN:, :] = k_new.astype(BF16)
            vall_ref[PAST_LEN:, :] = v_ref[...].astype(BF16)

        k = kall_ref[...]
        v = vall_ref[...]
    else:
        k = _subhead_norm(k_ref[...], kg_ref[...])
        v = v_ref[...]
        kn_ref[...] = k
    lp = lp_ref[...]
    lam = (jnp.exp(jnp.sum(lp[0:1] * lp[1:2], axis=-1, keepdims=True))
           - jnp.exp(jnp.sum(lp[2:3] * lp[3:4], axis=-1, keepdims=True)) + lam_init)
    scale = HD_A ** -0.5
    probs = []
    for c in range(2):
        qc = q[:, c * HD_A:(c + 1) * HD_A].astype(BF16)
        kc = k[:, c * HD_A:(c + 1) * HD_A].astype(BF16)
        s = lax.dot_general(qc, kc, (((1,), (1,)), ((), ())), preferred_element_type=F32) * scale
        e = jnp.exp(s - jnp.max(s, axis=-1, keepdims=True))
        probs.append(e / jnp.sum(e, axis=-1, keepdims=True))
    w = probs[0] - lam * probs[1]
    o = _bdot(w, v)
    ms = jnp.mean(o * o, axis=-1, keepdims=True)
    o_ref[...] = (o * lax.rsqrt(ms + EPS) * sg_ref[...]) * (1.0 - lam_init)


def _attention_prompt(qkv, qg2, kg2, lam_p, sub_g, lam_init):
    nh = H_A
    head = 2 * HD_A
    small = [
        pl.BlockSpec((1, head), lambda b, h: (0, 0)),
        pl.BlockSpec((1, head), lambda b, h: (0, 0)),
        pl.BlockSpec((4, HD_A), lambda b, h: (0, 0)),
        pl.BlockSpec((1, head), lambda b, h: (0, 0)),
    ]
    return pl.pallas_call(
        functools.partial(_attn_kernel, lam_init, False),
        grid=(BATCH, nh),
        in_specs=[
            pl.BlockSpec((SEQ, head), lambda b, h: (b, h)),
            pl.BlockSpec((SEQ, head), lambda b, h: (b, nh + h)),
            pl.BlockSpec((SEQ, head), lambda b, h: (b, 2 * nh + h)),
        ] + small,
        out_specs=[pl.BlockSpec((SEQ, head), lambda b, h: (b, h)),
                   pl.BlockSpec((SEQ, head), lambda b, h: (b, h))],
        out_shape=[jax.ShapeDtypeStruct((T_P, W_A), F32), jax.ShapeDtypeStruct((T_P, W_A), F32)],
        compiler_params=_cparams(2),
    )(qkv, qkv, qkv, qg2, kg2, lam_p, sub_g)


def _attention_sample(qkv, cache_k, cache_v, cos, sin, qg2, kg2, lam_p, sub_g, lam_init):
    nh = H_A
    head = 2 * HD_A
    tq = ROW_TILE
    nq = DEC_SEQ // tq
    q_off = T_P // tq
    k_off = T_P // DEC_SEQ
    small = [
        pl.BlockSpec((1, head), lambda b, h, i: (0, 0)),
        pl.BlockSpec((1, head), lambda b, h, i: (0, 0)),
        pl.BlockSpec((4, HD_A), lambda b, h, i: (0, 0)),
        pl.BlockSpec((1, head), lambda b, h, i: (0, 0)),
    ]
    return pl.pallas_call(
        functools.partial(_attn_kernel, lam_init, True),
        grid=(DEC_BATCH, nh, nq),
        in_specs=[
            pl.BlockSpec((tq, head), lambda b, h, i: (q_off + b * nq + i, h)),
            pl.BlockSpec((DEC_SEQ, head), lambda b, h, i: (k_off + b, nh + h)),
            pl.BlockSpec((DEC_SEQ, head), lambda b, h, i: (k_off + b, 2 * nh + h)),
            pl.BlockSpec((None, PAST_LEN, head), lambda b, h, i: (b, 0, h)),
            pl.BlockSpec((None, PAST_LEN, head), lambda b, h, i: (b, 0, h)),
            pl.BlockSpec((tq, head), lambda b, h, i: (i, 0)),
            pl.BlockSpec((tq, head), lambda b, h, i: (i, 0)),
            pl.BlockSpec((DEC_SEQ, head), lambda b, h, i: (0, 0)),
            pl.BlockSpec((DEC_SEQ, head), lambda b, h, i: (0, 0)),
        ] + small,
        out_specs=pl.BlockSpec((tq, head), lambda b, h, i: (b * nq + i, h)),
        out_shape=jax.ShapeDtypeStruct((T_S, W_A), F32),
        scratch_shapes=[pltpu.VMEM((PAST_LEN + DEC_SEQ, head), BF16)] * 2,
        compiler_params=_cparams(3),
    )(qkv, qkv, qkv, cache_k, cache_v, cos, sin, cos, sin, qg2, kg2, lam_p, sub_g)


def _rope_tables():
    half = HD_A // 2
    nf = half // 2
    inv = ROPE_BASE ** (-np.arange(nf, dtype=np.float32) / nf)
    pos = np.arange(DEC_SEQ)
    row = (pos // GRID_W).astype(np.float32)
    col = (pos % GRID_W).astype(np.float32)
    ang_r = (row[:, None] * inv).astype(np.float32)
    ang_c = (col[:, None] * inv).astype(np.float32)
    ang = np.concatenate([ang_r, ang_r, ang_c, ang_c], axis=1)
    sign = np.concatenate([-np.ones(nf), np.ones(nf), -np.ones(nf), np.ones(nf)]).astype(np.float32)
    cos = np.cos(ang.astype(np.float64)).astype(np.float32)
    sin = (np.sin(ang.astype(np.float64)) * sign).astype(np.float32)
    return jnp.asarray(np.tile(cos, (1, 2))), jnp.asarray(np.tile(sin, (1, 2)))


def _mlstm_kernel(seq, has_ctx, *refs):
    if has_ctx:
        (q_ref, k_ref, cwq_ref, cwk_ref, cbq_ref, cbk_ref, v_ref, mo_ref, gi_ref, gf_ref,
         gbi_ref, gbf_ref, hn_ref, c0_ref, n0_ref, m0_ref, o_ref,
         qs_ref, ks_ref, hf_ref, hb_ref, cs_ref, rrow_ref, col_ref, wc_ref) = refs
    else:
        (q_ref, k_ref, cwq_ref, cwk_ref, cbq_ref, cbk_ref, v_ref, mo_ref, gi_ref, gf_ref,
         gbi_ref, gbf_ref, hn_ref, o_ref, c_out_ref, n_out_ref, m_out_ref,
         qs_ref, ks_ref, hf_ref, hb_ref, cs_ref, rrow_ref, col_ref, wc_ref) = refs
    nc = seq // CHUNK
    n_chain = 2 * H_B
    chains = [(d, h) for d in range(2) for h in range(H_B)]
    qs_ref[...] = _silu(_dwconv3(q_ref[...], cwq_ref[...], cbq_ref[...])) * (HD_B ** -0.5)
    ks_ref[...] = _silu(_dwconv3(k_ref[...], cwk_ref[...], cbk_ref[...]))

    rows = nc * n_chain
    lane = lax.broadcasted_iota(jnp.int32, (rows, 2 * CHUNK), 1)
    forward = lax.broadcasted_iota(jnp.int32, (rows, 2 * CHUNK), 0) % n_chain < H_B
    valid = lane < CHUNK

    def scan(x, op, fill):
        pre, suf = x, x
        sh = 1
        while sh < CHUNK:
            pre = op(pre, jnp.where(lane >= sh, pltpu.roll(pre, sh, 1), fill))
            suf = op(suf, jnp.where(lane + sh < CHUNK, pltpu.roll(suf, 2 * CHUNK - sh, 1), fill))
            sh *= 2
        return jnp.where(forward, pre, suf)

    gate_i = (gi_ref[...] + gbi_ref[...]).reshape(rows, 2 * CHUNK)
    lf = jnp.where(valid, _log_sigmoid(gf_ref[...] + gbf_ref[...]).reshape(rows, 2 * CHUNK), 0.0)
    b = scan(lf, jnp.add, 0.0)
    cmax = scan(jnp.where(valid, gate_i - b, -jnp.inf), jnp.maximum, -jnp.inf)
    b_last = jnp.sum(lf, axis=1, keepdims=True)
    g = b_last - b + gate_i
    g_max = jnp.max(jnp.where(valid, g, -jnp.inf), axis=1, keepdims=True)
    mm = m0_ref[...] if has_ctx else jnp.zeros((n_chain, 1), F32)
    mm_seq = []
    for p in range(nc):
        mm_seq.append(mm)
        seg = slice(p * n_chain, (p + 1) * n_chain)
        mm = jnp.maximum(b_last[seg] + mm, g_max[seg])
    mm_final = mm
    mm_prev = jnp.concatenate(mm_seq, axis=0)
    mm_next = jnp.concatenate(mm_seq[1:] + [mm_final], axis=0)
    m_t = jnp.maximum(b + mm_prev, b + cmax)
    rrow_ref[...] = (b - gate_i).reshape(nc, n_chain, 2 * CHUNK)
    wc_ref[...] = jnp.exp(b_last + mm_prev - mm_next).reshape(nc, n_chain, 1)
    per_row = [b, m_t, jnp.exp(b + mm_prev - m_t), jnp.exp(-m_t), jnp.exp(g - mm_next)]
    for j, arr in enumerate(per_row):
        by_time = arr.T
        for p in range(nc):
            col_ref[p, :, j * n_chain:(j + 1) * n_chain] = by_time[0:CHUNK, p * n_chain:(p + 1) * n_chain]

    t_idx = lax.broadcasted_iota(jnp.int32, (CHUNK, CHUNK), 0)
    s_idx = lax.broadcasted_iota(jnp.int32, (CHUNK, CHUNK), 1)
    for n, (d, h) in enumerate(chains):
        cs_ref[n] = c0_ref[d, h] if has_ctx else jnp.zeros((HD_B, HD_B), F32)

    def out_step(p, n_states):
        cols = col_ref[p]
        rrows = rrow_ref[p]
        wcs = wc_ref[p]
        new_states = []
        for n, (d, h) in enumerate(chains):
            c = p if d == 0 else nc - 1 - p
            r0 = pl.multiple_of(c * CHUNK, CHUNK)
            hcols = slice(h * HD_B, (h + 1) * HD_B)
            qt = qs_ref[pl.ds(r0, CHUNK), hcols]
            kt = ks_ref[pl.ds(r0, CHUNK), hcols]
            vt = v_ref[pl.ds(r0, CHUNK), hcols]
            b_col, m_t, w_inter, e_inv, w_k = (cols[:, j * n_chain + n:j * n_chain + n + 1] for j in range(5))
            mask = (s_idx <= t_idx) if d == 0 else (s_idx >= t_idx)
            decay = jnp.exp(jnp.where(mask, b_col - rrows[n:n + 1, 0:CHUNK], NEG) - m_t)
            qk = lax.dot_general(qt.astype(BF16), kt.astype(BF16), (((1,), (1,)), ((), ())),
                                 preferred_element_type=F32)
            s = qk * decay
            cm = cs_ref[n]
            nm = n_states[n]
            cq = lax.dot_general(qt.astype(BF16), cm.astype(BF16), (((1,), (1,)), ((), ())),
                                 preferred_element_type=F32)
            num = _bdot(s, vt) + w_inter * cq
            nq = jnp.sum(s, axis=-1, keepdims=True) + w_inter * jnp.sum(qt * nm, axis=-1, keepdims=True)
            hdir_ref = hf_ref if d == 0 else hb_ref
            hdir_ref[pl.ds(r0, CHUNK), hcols] = num / jnp.maximum(jnp.abs(nq), e_inv)
            w_c = wcs[n:n + 1, :]
            vw = (vt * w_k).astype(BF16)
            cs_ref[n] = w_c * cm + lax.dot_general(vw, kt.astype(BF16), (((0,), (0,)), ((), ())),
                                                   preferred_element_type=F32)
            new_states.append(w_c * nm + jnp.sum(kt * w_k, axis=0, keepdims=True))
        return tuple(new_states)

    if has_ctx:
        n_init = tuple(n0_ref[d, h] for d, h in chains)
    else:
        n_init = tuple(jnp.zeros((1, HD_B), F32) for _ in chains)
    n_final = lax.fori_loop(0, nc, out_step, n_init)
    if not has_ctx:
        for n, (d, h) in enumerate(chains):
            c_out_ref[d, h] = cs_ref[n]
            n_out_ref[d, h] = n_final[n]
            m_out_ref[d, h] = jnp.broadcast_to(mm_final[n:n + 1, :], (1, HD_B))

    for h in range(H_B):
        hcols = slice(h * HD_B, (h + 1) * HD_B)
        hh = hf_ref[:, hcols] + hb_ref[:, hcols]
        ms = jnp.mean(hh * hh, axis=-1, keepdims=True)
        o_ref[:, hcols] = (hh * lax.rsqrt(ms + EPS) * hn_ref[:, hcols]) * _sigmoid(mo_ref[:, hcols])


def _mlstm(mqk, mv, mo, mg_stream, conv_w, conv_b, gate_b, hn_g, *, seq, nbatch, row_off, ctx=None):
    nh = H_B
    nc = seq // CHUNK
    has_ctx = ctx is not None
    gt = mg_stream.reshape(nbatch, nc, CHUNK, 2, 2, nh).transpose(0, 1, 3, 4, 5, 2)
    pad = ((0, 0), (0, 0), (0, 0), (0, CHUNK))
    gates = [jnp.pad(jnp.concatenate([gt[:, :, 0, j], gt[:, ::-1, 1, j]], axis=2), pad) for j in range(2)]
    gate_bias = [jnp.concatenate([gate_b[0, j], gate_b[1, j]]).reshape(2 * nh, 1) for j in range(2)]
    blk = lambda col: pl.BlockSpec((seq, W_B), lambda b, col=col: (row_off + b, col))
    gate_blk = pl.BlockSpec((None, nc, 2 * nh, 2 * CHUNK), lambda b: (b, 0, 0, 0))
    in_specs = [
        blk(0), blk(1),
        pl.BlockSpec((3, W_B), lambda b: (0, 0)),
        pl.BlockSpec((3, W_B), lambda b: (0, 1)),
        pl.BlockSpec((1, W_B), lambda b: (0, 0)),
        pl.BlockSpec((1, W_B), lambda b: (0, 1)),
        blk(0), blk(0),
        gate_blk, gate_blk,
        pl.BlockSpec((2 * nh, 1), lambda b: (0, 0)),
        pl.BlockSpec((2 * nh, 1), lambda b: (0, 0)),
        pl.BlockSpec((1, W_B), lambda b: (0, 0)),
    ]
    args = [mqk, mqk, conv_w, conv_w, conv_b, conv_b, mv, mo, gates[0], gates[1],
            gate_bias[0], gate_bias[1], hn_g.reshape(1, W_B)]
    o_spec = pl.BlockSpec((seq, W_B), lambda b: (b, 0))
    o_shape = jax.ShapeDtypeStruct((nbatch * seq, W_B), F32)
    state_blk = lambda rows: pl.BlockSpec((None, 2, nh, rows, HD_B), lambda b: (b, 0, 0, 0, 0))
    if has_ctx:
        c0, n0, m0 = ctx
        in_specs += [state_blk(HD_B), state_blk(1), pl.BlockSpec((None, 2 * nh, 1), lambda b: (b, 0, 0))]
        args += [c0, n0.reshape(nbatch, 2, nh, 1, HD_B), m0.reshape(nbatch, 2 * nh, 1)]
        out_specs, out_shape = o_spec, o_shape
    else:
        out_specs = [o_spec, state_blk(HD_B), state_blk(1), state_blk(1)]
        out_shape = [
            o_shape,
            jax.ShapeDtypeStruct((nbatch, 2, nh, HD_B, HD_B), F32),
            jax.ShapeDtypeStruct((nbatch, 2, nh, 1, HD_B), F32),
            jax.ShapeDtypeStruct((nbatch, 2, nh, 1, HD_B), F32),
        ]
    return pl.pallas_call(
        functools.partial(_mlstm_kernel, seq, has_ctx),
        grid=(nbatch,),
        in_specs=in_specs,
        out_specs=out_specs,
        out_shape=out_shape,
        scratch_shapes=[pltpu.VMEM((seq, W_B), F32)] * 4 + [
            pltpu.VMEM((2 * nh, HD_B, HD_B), F32),
            pltpu.VMEM((nc, 2 * nh, 2 * CHUNK), F32),
            pltpu.VMEM((nc, CHUNK, 5 * 2 * nh), F32),
            pltpu.VMEM((nc, 2 * nh, 1), F32),
        ],
        compiler_params=_cparams(1),
    )(*args)


def _dft_mats(L):
    f = np.arange(L)[:, None]
    j = np.arange(L)[None, :]
    ang = 2.0 * np.pi * ((f * j) % (2 * L)) / (2 * L)
    cm = np.cos(ang)
    sm = np.sin(ang)
    alt = (1.0 - 2.0 * (np.arange(L) % 2))
    fwd_b = -sm
    fwd_b[0, :] = alt
    fwd = np.concatenate([cm, fwd_b], axis=0)
    wgt = np.where(np.arange(L) == 0, 1.0, 2.0)[None, :]
    inv_a = cm.T * wgt
    inv_b = -2.0 * sm.T
    inv_b[:, 0] = alt
    inv = np.concatenate([inv_a, inv_b], axis=1) / (2 * L)
    return jnp.asarray(fwd.astype(np.float32)), jnp.asarray(inv.astype(np.float32))


def _hyena_feats(L):
    t = np.linspace(0.0, 1.0, L, dtype=np.float32)
    wpos = (2.0 * math.pi * np.arange(L, dtype=np.float32) / L).astype(np.float32)
    fb = np.linspace(1e-4, HY_BANDS - 1, HY_BANDS, dtype=np.float32)
    z = (wpos[:, None] * fb).astype(np.float32)
    feats = np.concatenate([t[:, None], np.cos(z), -np.sin(z)], axis=-1).astype(np.float32)
    deltas = np.abs(np.linspace(math.log(HY_TARGET) / HY_SLOW_PCT, math.log(HY_TARGET) / HY_FAST_PCT,
                                D_MODEL, dtype=np.float32))
    decay = np.exp(-t[:, None] * deltas).astype(np.float32)
    return jnp.asarray(feats), jnp.asarray(decay)


def _filter_kernel(L, feats_ref, w1_ref, b1_ref, fr1_ref, w2_ref, b2_ref, fr2_ref, w3f_ref, w3b_ref,
                   decay_ref, fwd_ref, o_ref, hdn_ref):
    @pl.when((pl.program_id(0) == 0) & (pl.program_id(1) == 0))
    def _():
        h1 = jnp.sin(fr1_ref[...] * (jnp.dot(feats_ref[...], w1_ref[...], precision=HIGHEST,
                                             preferred_element_type=F32) + b1_ref[...]))
        hdn_ref[...] = jnp.sin(fr2_ref[...] * (jnp.dot(h1, w2_ref[...], precision=HIGHEST,
                                                       preferred_element_type=F32) + b2_ref[...]))

    hdn = hdn_ref[...]
    decay = decay_ref[...]
    f_fwd = jnp.dot(hdn, w3f_ref[...], precision=HIGHEST, preferred_element_type=F32) * decay
    f_bwd = jnp.dot(hdn, w3b_ref[...], precision=HIGHEST, preferred_element_type=F32) * decay
    row = lax.broadcasted_iota(jnp.int32, f_bwd.shape, 0)
    f_bwd = jnp.where(row == 0, 0.0, f_bwd)
    fwd = fwd_ref[...]
    p = _bdot(fwd, f_fwd)
    q = _bdot(fwd, f_bwd)
    first = row == 0
    o_ref[0:L, :] = p[0:L] + q[0:L]
    o_ref[L:2 * L, :] = p[L:2 * L] + jnp.where(first, q[L:2 * L], -q[L:2 * L])


def _hyena_filter_spectrum(L, fwd_bf, w1, b1, fr1, w2, b2, fr2, w3):
    feats, decay = _hyena_feats(L)
    td = 512
    nd = D_MODEL // td
    emb = feats.shape[1]
    vec = lambda a: a.reshape(1, HY_FH)
    full = lambda shape: pl.BlockSpec(shape, lambda o, j: (0, 0))
    return pl.pallas_call(
        functools.partial(_filter_kernel, L),
        grid=(HY_ORDER, nd),
        in_specs=[
            full((L, emb)), full((emb, HY_FH)), full((1, HY_FH)), full((1, HY_FH)),
            full((HY_FH, HY_FH)), full((1, HY_FH)), full((1, HY_FH)),
            pl.BlockSpec((HY_FH, td), lambda o, j: (0, o * 2 * nd + j)),
            pl.BlockSpec((HY_FH, td), lambda o, j: (0, o * 2 * nd + nd + j)),
            pl.BlockSpec((L, td), lambda o, j: (0, j)),
            full((2 * L, L)),
        ],
        out_specs=pl.BlockSpec((2 * L, td), lambda o, j: (0, o * nd + j)),
        out_shape=jax.ShapeDtypeStruct((2 * L, HY_ORDER * D_MODEL), F32),
        scratch_shapes=[pltpu.VMEM((L, HY_FH), F32)],
        compiler_params=_cparams(2),
    )(feats, w1, vec(b1), vec(fr1), w2, vec(b2), vec(fr2), w3, w3, decay, fwd_bf)


def _spectral_conv(u, fwd, inv, kspec, L):
    uf = jnp.dot(fwd, u.astype(BF16), preferred_element_type=F32)
    ua, ub = uf[0:L], uf[L:2 * L]
    ka, kb = kspec[0:L], kspec[L:2 * L]
    first = lax.broadcasted_iota(jnp.int32, ua.shape, 0) == 0
    ya = ua * ka - jnp.where(first, 0.0, ub * kb)
    yb = jnp.where(first, ub * kb, ua * kb + ub * ka)
    y = jnp.concatenate([ya, yb], axis=0).astype(BF16)
    return jnp.dot(inv, y, preferred_element_type=F32)


def _hyena_kernel(L, zv_ref, z1_ref, z2_ref, cwv_ref, cw1_ref, cw2_ref, cbv_ref, cb1_ref, cb2_ref,
                  fwd_ref, inv_ref, k0_ref, k1_ref, bias0_ref, bias1_ref, o_ref):
    fwd = fwd_ref[...]
    inv = inv_ref[...]
    v = _dwconv3(zv_ref[...], cwv_ref[...], cbv_ref[...])
    x1 = _dwconv3(z1_ref[...], cw1_ref[...], cb1_ref[...])
    x2 = _dwconv3(z2_ref[...], cw2_ref[...], cb2_ref[...])
    z = x1 * (_spectral_conv(v, fwd, inv, k0_ref[...], L) + v * bias0_ref[...])
    o_ref[...] = x2 * (_spectral_conv(z, fwd, inv, k1_ref[...], L) + z * bias1_ref[...])


def _hyena_core(zproj, conv_w, conv_b, fwd_bf, inv_bf, kspec, bias, *, seq, nbatch, row_off, td):
    nd = D_MODEL // td
    zblk = lambda part: pl.BlockSpec((seq, td), lambda b, j, part=part: (row_off + b, part * nd + j))
    cwblk = lambda part: pl.BlockSpec((3, td), lambda b, j, part=part: (0, part * nd + j))
    cbblk = lambda part: pl.BlockSpec((1, td), lambda b, j, part=part: (0, part * nd + j))
    return pl.pallas_call(
        functools.partial(_hyena_kernel, seq),
        grid=(nbatch, nd),
        in_specs=[
            zblk(0), zblk(1), zblk(2), cwblk(0), cwblk(1), cwblk(2), cbblk(0), cbblk(1), cbblk(2),
            pl.BlockSpec((2 * seq, seq), lambda b, j: (0, 0), pipeline_mode=pl.Buffered(1)),
            pl.BlockSpec((seq, 2 * seq), lambda b, j: (0, 0), pipeline_mode=pl.Buffered(1)),
            pl.BlockSpec((2 * seq, td), lambda b, j: (0, j)),
            pl.BlockSpec((2 * seq, td), lambda b, j: (0, nd + j)),
            pl.BlockSpec((None, 1, td), lambda b, j: (0, 0, j)),
            pl.BlockSpec((None, 1, td), lambda b, j: (1, 0, j)),
        ],
        out_specs=pl.BlockSpec((seq, td), lambda b, j: (b, j)),
        out_shape=jax.ShapeDtypeStruct((nbatch * seq, D_MODEL), F32),
        compiler_params=_cparams(2),
    )(zproj, zproj, zproj, conv_w, conv_w, conv_w, conv_b, conv_b, conv_b,
      fwd_bf, inv_bf, kspec, kspec, bias.reshape(HY_ORDER, 1, D_MODEL), bias.reshape(HY_ORDER, 1, D_MODEL))


def _router_kernel(x_ref, ada_ref, g_ref, rw_ref, rb_ref, h_ref, idx_ref, wt_ref):
    i = pl.program_id(0)
    row = _cond_row(i)
    h = _modulate(x_ref[...], g_ref[...], _ada_chunk(ada_ref, row, 3), _ada_chunk(ada_ref, row, 4))
    h_ref[...] = h.astype(BF16)
    logits = lax.dot_general(rw_ref[...], h, (((1,), (1,)), ((), ())), precision=HIGHEST,
                             preferred_element_type=F32) + rb_ref[...]
    expert = lax.broadcasted_iota(jnp.int32, logits.shape, 0)
    slot = lax.broadcasted_iota(jnp.int32, (TOP_K, logits.shape[1]), 0)
    vals = jnp.zeros((TOP_K, logits.shape[1]), F32)
    idxs = jnp.zeros((TOP_K, logits.shape[1]), jnp.int32)
    cur = logits
    for k in range(TOP_K):
        m = jnp.max(cur, axis=0, keepdims=True)
        a = jnp.min(jnp.where(cur == m, expert, N_EXPERTS), axis=0, keepdims=True)
        vals = jnp.where(slot == k, m, vals)
        idxs = jnp.where(slot == k, a, idxs)
        cur = jnp.where(expert == a, -jnp.inf, cur)
    e = jnp.exp(vals - vals[0:1])
    wt_ref[...] = e / jnp.sum(e, axis=0, keepdims=True)
    idx_ref[...] = idxs


def _router(y, ada_l, g, router_w, router_b):
    return pl.pallas_call(
        _router_kernel,
        grid=(N_ROW_TILES,),
        in_specs=[
            pl.BlockSpec((ROW_TILE, D_MODEL), lambda i: (i, 0)),
            pl.BlockSpec((COND_ROWS, ADA_CHUNKS * D_MODEL), lambda i: (0, 0)),
            pl.BlockSpec((1, D_MODEL), lambda i: (0, 0)),
            pl.BlockSpec((N_EXPERTS, D_MODEL), lambda i: (0, 0)),
            pl.BlockSpec((N_EXPERTS, 1), lambda i: (0, 0)),
        ],
        out_specs=[
            pl.BlockSpec((ROW_TILE, D_MODEL), lambda i: (i, 0)),
            pl.BlockSpec((TOP_K, ROW_TILE), lambda i: (0, i)),
            pl.BlockSpec((TOP_K, ROW_TILE), lambda i: (0, i)),
        ],
        out_shape=[
            jax.ShapeDtypeStruct((T_ALL, D_MODEL), BF16),
            jax.ShapeDtypeStruct((TOP_K, T_ALL), jnp.int32),
            jax.ShapeDtypeStruct((TOP_K, T_ALL), F32),
        ],
        compiler_params=_cparams(1),
    )(y, ada_l, g.reshape(1, D_MODEL), router_w.T, router_b.reshape(N_EXPERTS, 1))


def _deinterleave_matrix():
    s = np.zeros((256, 256), np.float32)
    j = np.arange(128)
    s[2 * j, j] = 1.0
    s[2 * j + 1, 128 + j] = 1.0
    return jnp.asarray(s)


def _weight_copies(layer, e, w1_hbm, w2_hbm, w1s_ref, w2s_ref, sem):
    copies = []
    r1 = D_MODEL // W1_DMA_CHUNKS
    for c in range(W1_DMA_CHUNKS):
        copies.append(pltpu.make_async_copy(w1_hbm.at[layer, e, pl.ds(c * r1, r1)],
                                            w1s_ref.at[pl.ds(c * r1, r1)], sem.at[c]))
    r2 = D_FF // W2_DMA_CHUNKS
    for c in range(W2_DMA_CHUNKS):
        copies.append(pltpu.make_async_copy(w2_hbm.at[layer, e, pl.ds(c * r2, r2)],
                                            w2s_ref.at[pl.ds(c * r2, r2)], sem.at[W1_DMA_CHUNKS + c]))
    return copies


def _expert_kernel(layer, te_ref, tf_ref, ne_ref, nu_ref, src_ref, xo_ref, x_ref, b1_ref, b2_ref, wt_ref, s_ref,
                   w1_hbm, w2_hbm, o_hbm, w1s_ref, w2s_ref, w1p_ref, w2p_ref, acc_ref, out_ref, wsem, osem):
    i = pl.program_id(0)
    half = 128
    copies = functools.partial(_weight_copies, layer, w1_hbm=w1_hbm, w2_hbm=w2_hbm,
                               w1s_ref=w1s_ref, w2s_ref=w2s_ref, sem=wsem)

    @pl.when(i == 0)
    def _():
        acc_ref[...] = jnp.zeros_like(acc_ref)
        out_ref[...] = jnp.zeros_like(out_ref)
        for cp in copies(te_ref[0]):
            cp.start()

    @pl.when(tf_ref[i] == 1)
    def _():
        for cp in copies(te_ref[i]):
            cp.wait()
        s = s_ref[...].astype(BF16)
        for c in range(2 * D_FF // 256):
            blk = jnp.dot(w1s_ref[:, c * 256:(c + 1) * 256].astype(BF16), s, preferred_element_type=F32)
            w1p_ref[:, c * half:(c + 1) * half] = blk[:, :half].astype(BF16)
            w1p_ref[:, D_FF + c * half:D_FF + (c + 1) * half] = blk[:, half:].astype(BF16)
        w2p_ref[...] = w2s_ref[...].astype(BF16)

        @pl.when(ne_ref[i] >= 0)
        def _():
            for cp in copies(ne_ref[i]):
                cp.start()

    @pl.when(i <= nu_ref[0])
    def _():
        base = i * MOE_TILE
        prev = (i + 1) % 2
        for r0 in range(0, MOE_TILE, SCATTER_GROUP):
            toks = [src_ref[base + r0 + g] for g in range(SCATTER_GROUP)]
            cur = [acc_ref[pl.ds(toks[g], 1), :] for g in range(SCATTER_GROUP)]
            add = [out_ref[prev, r0 + g:r0 + g + 1, :] for g in range(SCATTER_GROUP)]
            for g in range(SCATTER_GROUP):
                acc_ref[pl.ds(toks[g], 1), :] = cur[g] + add[g]
        a = jnp.dot(x_ref[...], w1p_ref[...], preferred_element_type=F32) + b1_ref[...]
        glu = jnp.minimum(a[:, :D_FF], SWIGLU_LIMIT)
        lin = jnp.clip(a[:, D_FF:], -SWIGLU_LIMIT, SWIGLU_LIMIT)
        hid = glu * _sigmoid(SWIGLU_ALPHA * glu) * (lin + 1.0)
        out = jnp.dot(hid.astype(BF16), w2p_ref[...], preferred_element_type=F32) + b2_ref[...]
        out_ref[i % 2] = out * wt_ref[...]

    @pl.when(i == pl.num_programs(0) - 1)
    def _():
        cp = pltpu.make_async_copy(acc_ref.at[pl.ds(0, T_ALL)], o_hbm, osem)
        cp.start()
        cp.wait()


def _experts(layer, x_sorted, w_sorted, plan, w1, b1p, w2, b2):
    tile_expert, tile_first, next_expert, n_used, src, x_off = plan
    grid_spec = pltpu.PrefetchScalarGridSpec(
        num_scalar_prefetch=6,
        grid=(MOE_TILES,),
        in_specs=[
            pl.BlockSpec((pl.Element(MOE_TILE), pl.Element(D_MODEL)),
                         lambda i, te, tf, ne, nu, src, xo: (pl.multiple_of(xo[i], X_ALIGN), 0)),
            pl.BlockSpec((None, None, 1, 2 * D_FF), lambda i, te, *_: (layer, te[i], 0, 0)),
            pl.BlockSpec((None, None, 1, D_MODEL), lambda i, te, *_: (layer, te[i], 0, 0)),
            pl.BlockSpec((MOE_TILE, 1), lambda i, te, *_: (i, 0)),
            pl.BlockSpec((256, 256), lambda i, te, *_: (0, 0)),
            pl.BlockSpec(memory_space=pl.ANY),
            pl.BlockSpec(memory_space=pl.ANY),
        ],
        out_specs=pl.BlockSpec(memory_space=pl.ANY),
        scratch_shapes=[
            pltpu.VMEM((D_MODEL, 2 * D_FF), F32),
            pltpu.VMEM((D_FF, D_MODEL), F32),
            pltpu.VMEM((D_MODEL, 2 * D_FF), BF16),
            pltpu.VMEM((D_FF, D_MODEL), BF16),
            pltpu.VMEM((ACC_ROWS, D_MODEL), F32),
            pltpu.VMEM((2, MOE_TILE, D_MODEL), F32),
            pltpu.SemaphoreType.DMA((W1_DMA_CHUNKS + W2_DMA_CHUNKS,)),
            pltpu.SemaphoreType.DMA(()),
        ],
    )
    return pl.pallas_call(
        functools.partial(_expert_kernel, layer),
        grid_spec=grid_spec,
        out_shape=jax.ShapeDtypeStruct((T_ALL, D_MODEL), F32),
        compiler_params=_cparams(1),
    )(tile_expert, tile_first, next_expert, n_used, src, x_off, x_sorted, b1p, b2, w_sorted,
      _deinterleave_matrix(), w1, w2)


def _combine_kernel(first_tile, y_ref, a_ref, ada_ref, o_ref):
    gate = _ada_chunk(ada_ref, _cond_row(first_tile + pl.program_id(0)), 5)
    o_ref[...] = y_ref[...] + gate * a_ref[...]


def _combine(y, acc, ada_l, first_tile=0, n_tiles=N_ROW_TILES):
    return pl.pallas_call(
        functools.partial(_combine_kernel, first_tile),
        grid=(n_tiles,),
        in_specs=[
            pl.BlockSpec((ROW_TILE, D_MODEL), lambda i: (first_tile + i, 0)),
            pl.BlockSpec((ROW_TILE, D_MODEL), lambda i: (first_tile + i, 0)),
            pl.BlockSpec((COND_ROWS, ADA_CHUNKS * D_MODEL), lambda i: (0, 0)),
        ],
        out_specs=pl.BlockSpec((ROW_TILE, D_MODEL), lambda i: (i, 0)),
        out_shape=jax.ShapeDtypeStruct((n_tiles * ROW_TILE, D_MODEL), F32),
        compiler_params=_cparams(1),
    )(y, acc, ada_l)


def _routing_plan(idx, wts):
    eid = idx.reshape(-1)
    order = jnp.argsort(eid, stable=True).astype(jnp.int32)
    experts = jnp.arange(N_EXPERTS, dtype=jnp.int32)
    counts = jnp.sum(eid[:, None] == experts[None, :], axis=0).astype(jnp.int32)
    ntiles = (counts + MOE_TILE - 1) // MOE_TILE
    tile_end = jnp.cumsum(ntiles).astype(jnp.int32)
    tile_begin = tile_end - ntiles
    cstarts = (jnp.cumsum(counts) - counts).astype(jnp.int32)
    n_used = tile_end[-1]
    tile = jnp.arange(MOE_TILES, dtype=jnp.int32)
    te = jnp.minimum(jnp.sum(tile[:, None] >= tile_end[None, :], axis=1), N_EXPERTS - 1).astype(jnp.int32)
    used = tile < n_used
    prev = jnp.concatenate([jnp.full((1,), -1, jnp.int32), te[:-1]])
    first = (te != prev) & used
    nxt = tile_end[te]
    next_expert = jnp.where(first & (nxt < n_used), te[jnp.minimum(nxt, MOE_TILES - 1)], -1).astype(jnp.int32)
    off = (tile - tile_begin[te])[:, None] * MOE_TILE + jnp.arange(MOE_TILE, dtype=jnp.int32)[None, :]
    valid = (off < counts[te][:, None]) & used[:, None]
    assign = order[jnp.clip(cstarts[te][:, None] + off, 0, N_ASSIGN - 1)]
    token = assign // TOP_K
    src = jnp.where(valid, token, SPARE_ROW).reshape(MOE_ROWS).astype(jnp.int32)
    src = jnp.concatenate([jnp.full((MOE_TILE,), SPARE_ROW, jnp.int32), src])
    w_sorted = jnp.where(valid, wts.reshape(-1)[assign], 0.0).reshape(MOE_ROWS, 1)
    seg = ((counts + X_ALIGN - 1) // X_ALIGN) * X_ALIGN
    seg_end = jnp.cumsum(seg).astype(jnp.int32)
    seg_begin = seg_end - seg
    x_off = jnp.where(used, seg_begin[te] + (tile - tile_begin[te]) * MOE_TILE, 0).astype(jnp.int32)
    group = jnp.arange(X_ROWS // X_ALIGN, dtype=jnp.int32) * X_ALIGN
    ge = jnp.minimum(jnp.sum(group[:, None] >= seg_end[None, :], axis=1), N_EXPERTS - 1)
    xoffset = (group - seg_begin[ge])[:, None] + jnp.arange(X_ALIGN, dtype=jnp.int32)[None, :]
    xassign = order[jnp.clip(cstarts[ge][:, None] + xoffset, 0, N_ASSIGN - 1)]
    gather_row = jnp.where(xoffset < counts[ge][:, None], xassign // TOP_K, 0).reshape(X_ROWS)
    plan = (te, first.astype(jnp.int32), next_expert, n_used.reshape(1), src, x_off)
    return plan, gather_row, w_sorted


def _moe(layer, y, ada_l, g, router_w, router_b, w1, b1p, w2, b2):
    h, idx_t, wts_t = _router(y, ada_l, g, router_w, router_b)
    plan, gather_row, w_sorted = _routing_plan(idx_t.T, wts_t.T)
    x_sorted = jnp.take(h, gather_row, axis=0, mode="clip")
    acc = _experts(layer, x_sorted, w_sorted, plan, w1, b1p, w2, b2)
    if layer == DEPTH - 1:
        return (_combine(y, acc, ada_l, 0, P_TILES), _combine(y, acc, ada_l, P_TILES, N_ROW_TILES - P_TILES))
    return _combine(y, acc, ada_l)


def kernel(x_prompt, x_sample, cache_attn_k, cache_attn_v, state_mlstm_C, state_mlstm_n, state_mlstm_m, c, c_ctx, ada_w, ada_b, norm_mix_g, norm_ffn_g, ab_w_in, ab_w_out, da_qnorm_g, da_knorm_g, da_lambda, da_subnorm_g, ml_conv_w, ml_conv_b, ml_gate_b, ml_headnorm_g, hy_w_in, hy_w_out, hy_conv_w, hy_conv_b, hy_f_w1, hy_f_b1, hy_f_freq1, hy_f_w2, hy_f_b2, hy_f_freq2, hy_f_w3, hy_bias, router_w, router_b, moe_w1, moe_b1, moe_w2, moe_b2):
    y = jnp.concatenate([x_prompt.reshape(T_P, D_MODEL), x_sample.reshape(T_S, D_MODEL)], axis=0)
    cond = jnp.concatenate([c_ctx[None, :], c, jnp.zeros((COND_ROWS - 1 - DEC_BATCH, D_MODEL), F32)], axis=0)
    ada = _ada_table(cond, ada_w, ada_b)
    b1p = moe_b1.reshape(DEPTH, N_EXPERTS, D_FF, 2).swapaxes(2, 3).reshape(DEPTH, N_EXPERTS, 1, 2 * D_FF)
    b2r = moe_b2.reshape(DEPTH, N_EXPERTS, 1, D_MODEL)
    new_k, new_v, new_c, new_n, new_m = [], [], [], [], []
    for layer in range(DEPTH):
        ada_l = ada[layer]
        if layer % 2 == 0:
            e = layer // 2
            lam_init = 0.8 - 0.6 * math.exp(-0.3 * layer)
            qkv, mqk, mv, mo, mg = _modulated_proj(
                y, ada_l, norm_mix_g[layer], ab_w_in[e], (3 * W_A, 2 * W_B, W_B, W_B, 4 * H_B))
            qg2 = jnp.tile(da_qnorm_g[e], 2).reshape(1, 2 * HD_A)
            kg2 = jnp.tile(da_knorm_g[e], 2).reshape(1, 2 * HD_A)
            sub_g = da_subnorm_g[e].reshape(1, 2 * HD_A)
            oa_p, k_norm = _attention_prompt(qkv, qg2, kg2, da_lambda[e], sub_g, lam_init)
            cos, sin = _rope_tables()
            oa_s = _attention_sample(
                qkv, cache_attn_k[:, e].reshape(DEC_BATCH, PAST_LEN, W_A),
                cache_attn_v[:, e].reshape(DEC_BATCH, PAST_LEN, W_A), cos, sin,
                qg2, kg2, da_lambda[e], sub_g, lam_init)
            ob_p, c_new, n_new, m_new = _mlstm(
                mqk, mv, mo, mg[:T_P], ml_conv_w[e], ml_conv_b[e].reshape(1, 2 * W_B), ml_gate_b[e],
                ml_headnorm_g[e], seq=SEQ, nbatch=BATCH, row_off=0)
            ob_s = _mlstm(
                mqk, mv, mo, mg[T_P:], ml_conv_w[e], ml_conv_b[e].reshape(1, 2 * W_B), ml_gate_b[e],
                ml_headnorm_g[e], seq=DEC_SEQ, nbatch=DEC_BATCH, row_off=T_P // DEC_SEQ,
                ctx=(state_mlstm_C[:, e], state_mlstm_n[:, e], state_mlstm_m[:, e]))
            y = _out_proj_residual([(oa_p, oa_s), (ob_p, ob_s)], y, ada_l, ab_w_out[e], 2)
            new_k.append(k_norm.reshape(BATCH, SEQ, H_A, 2, HD_A))
            new_v.append(qkv[:T_P, 2 * W_A:].reshape(BATCH, SEQ, H_A, 2 * HD_A))
            new_c.append(c_new)
            new_n.append(n_new.reshape(BATCH, 2, H_B, HD_B))
            new_m.append(m_new[..., 0, 0])
        else:
            o = layer // 2
            (zproj,) = _modulated_proj(y, ada_l, norm_mix_g[layer], hy_w_in[o], (HY_PROJ,))
            cores = []
            for seq, nbatch, row_off, td in ((SEQ, BATCH, 0, 512), (DEC_SEQ, DEC_BATCH, T_P // DEC_SEQ, 256)):
                fwd, inv = _dft_mats(seq)
                fwd_bf, inv_bf = fwd.astype(BF16), inv.astype(BF16)
                kspec = _hyena_filter_spectrum(seq, fwd_bf, hy_f_w1[o], hy_f_b1[o], hy_f_freq1[o], hy_f_w2[o],
                                               hy_f_b2[o], hy_f_freq2[o], hy_f_w3[o])
                cores.append(_hyena_core(zproj, hy_conv_w[o], hy_conv_b[o].reshape(1, HY_PROJ), fwd_bf, inv_bf,
                                         kspec, hy_bias[o], seq=seq, nbatch=nbatch, row_off=row_off, td=td))
            y = _out_proj_residual([tuple(cores)], y, ada_l, hy_w_out[o], 2)
        y = _moe(layer, y, ada_l, norm_ffn_g[layer], router_w[layer], router_b[layer],
                 moe_w1, b1p, moe_w2, b2r)
    y_p = y[0].reshape(BATCH, SEQ, D_MODEL)
    y_s = y[1].reshape(DEC_BATCH, DEC_SEQ, D_MODEL)
    return (y_p, y_s, jnp.stack(new_k, axis=1), jnp.stack(new_v, axis=1), jnp.stack(new_c, axis=1),
            jnp.stack(new_n, axis=1), jnp.stack(new_m, axis=1))
```

```python
import functools
import math

import numpy as np
import jax
import jax.numpy as jnp
from jax import lax
from jax.experimental import pallas as pl
from jax.experimental.pallas import tpu as pltpu

D_MODEL = 1024
BATCH = 16
SEQ = 256
DEPTH = 2
DEC_BATCH = 2
DEC_SEQ = 1024
PAST_LEN = 256
GRID_W = 64
W_A = D_MODEL // 2
HD_A = 64
H_A = W_A // (2 * HD_A)
W_B = D_MODEL - W_A
HD_B = 128
H_B = W_B // HD_B
AB_PROJ = 3 * W_A + 4 * W_B + 4 * H_B
ROPE_BASE = 10000.0
CHUNK = 64
HY_ORDER = 2
HY_PROJ = (HY_ORDER + 1) * D_MODEL
HY_BANDS = 8
HY_FH = 64
HY_TARGET = 1e-2
HY_FAST_PCT = 0.3
HY_SLOW_PCT = 1.5
N_EXPERTS = 32
TOP_K = 4
D_FF = D_MODEL
SWIGLU_ALPHA = 1.702
SWIGLU_LIMIT = 7.0
ADA_CHUNKS = 6
EPS = 1e-6
NEG = -1e30
F32 = jnp.float32
BF16 = jnp.bfloat16

T_P = BATCH * SEQ
T_S = DEC_BATCH * DEC_SEQ
T_ALL = T_P + T_S
ROW_TILE = 256
N_ROW_TILES = T_ALL // ROW_TILE
P_TILES = T_P // ROW_TILE
S_TILES_PER_BATCH = DEC_SEQ // ROW_TILE
COND_ROWS = 8
MOE_TILE = 256
N_ASSIGN = T_ALL * TOP_K
MOE_ROWS = N_ASSIGN + N_EXPERTS * MOE_TILE
MOE_TILES = MOE_ROWS // MOE_TILE
X_ALIGN = 16
X_ROWS = N_ASSIGN + N_EXPERTS * X_ALIGN + MOE_TILE
SPARE_ROW = T_ALL
ACC_ROWS = T_ALL + 8
SCATTER_GROUP = 8
W1_DMA_CHUNKS = 8
W2_DMA_CHUNKS = 4
VMEM_LIMIT = 56 * 1024 * 1024
HIGHEST = lax.Precision.HIGHEST


def _cparams(n_axes):
    return pltpu.CompilerParams(dimension_semantics=("arbitrary",) * n_axes,
                                vmem_limit_bytes=VMEM_LIMIT)


def _bdot(a, b):
    return jnp.dot(a.astype(BF16), b.astype(BF16), preferred_element_type=F32)


def _cond_row(i):
    return jnp.where(i < P_TILES, 0, 1 + (i - P_TILES) // S_TILES_PER_BATCH)


def _ada_chunk(ada_ref, row, j):
    return ada_ref[pl.ds(row, 1), j * D_MODEL:(j + 1) * D_MODEL]


def _modulate(x, g, shift, scale):
    ms = jnp.mean(x * x, axis=-1, keepdims=True)
    return (x * lax.rsqrt(ms + EPS) * g) * (1.0 + scale) + shift


def _sigmoid(x):
    return 1.0 / (1.0 + jnp.exp(-x))


def _silu(x):
    return x * _sigmoid(x)


def _log_sigmoid(x):
    return jnp.minimum(x, 0.0) - jnp.log(1.0 + jnp.exp(-jnp.abs(x)))


def _dwconv3(x, w, b):
    n = x.shape[0]
    row = lax.broadcasted_iota(jnp.int32, x.shape, 0)
    prev = jnp.where(row == 0, 0.0, pltpu.roll(x, 1, 0))
    nxt = jnp.where(row == n - 1, 0.0, pltpu.roll(x, n - 1, 0))
    return prev * w[0:1] + x * w[1:2] + nxt * w[2:3] + b


def _ada_kernel(cond_ref, w_ref, b_ref, o_ref):
    c = _silu(cond_ref[...])
    o_ref[...] = _bdot(c, w_ref[...]) + b_ref[...]


def _ada_table(cond, ada_w, ada_b):
    tn = 1536
    return pl.pallas_call(
        _ada_kernel,
        grid=(DEPTH, ADA_CHUNKS * D_MODEL // tn),
        in_specs=[
            pl.BlockSpec((COND_ROWS, D_MODEL), lambda l, j: (0, 0)),
            pl.BlockSpec((None, D_MODEL, tn), lambda l, j: (l, 0, j)),
            pl.BlockSpec((None, 1, tn), lambda l, j: (l, 0, j)),
        ],
        out_specs=pl.BlockSpec((None, COND_ROWS, tn), lambda l, j: (l, 0, j)),
        out_shape=jax.ShapeDtypeStruct((DEPTH, COND_ROWS, ADA_CHUNKS * D_MODEL), F32),
        compiler_params=_cparams(2),
    )(cond, ada_w, ada_b.reshape(DEPTH, 1, ADA_CHUNKS * D_MODEL))


def _stream_specs(y):
    if isinstance(y, tuple):
        return [pl.BlockSpec((ROW_TILE, D_MODEL), lambda i: (jnp.minimum(i, P_TILES - 1), 0)),
                pl.BlockSpec((ROW_TILE, D_MODEL), lambda i: (jnp.maximum(i - P_TILES, 0), 0))], list(y)
    return [pl.BlockSpec((ROW_TILE, D_MODEL), lambda i: (i, 0))], [y]


def _stream_tile(y_refs, i):
    if len(y_refs) == 2:
        return jnp.where(i < P_TILES, y_refs[0][...], y_refs[1][...])
    return y_refs[0][...]


def _proj_kernel(splits, n_y, *refs):
    y_refs = refs[:n_y]
    ada_ref, g_ref, w_ref = refs[n_y:n_y + 3]
    out_refs, wbf_ref = refs[n_y + 3:-1], refs[-1]
    i = pl.program_id(0)

    @pl.when(i == 0)
    def _():
        wbf_ref[...] = w_ref[...].astype(BF16)

    row = _cond_row(i)
    h = _modulate(_stream_tile(y_refs, i), g_ref[...], _ada_chunk(ada_ref, row, 0), _ada_chunk(ada_ref, row, 1))
    h = h.astype(BF16)
    lo = 0
    for o_ref, width in zip(out_refs, splits):
        o_ref[...] = jnp.dot(h, wbf_ref[:, lo:lo + width], preferred_element_type=F32)
        lo += width


def _modulated_proj(y, ada_l, g, w, splits):
    n = w.shape[1]
    y_specs, y_args = _stream_specs(y)
    return pl.pallas_call(
        functools.partial(_proj_kernel, splits, len(y_args)),
        grid=(N_ROW_TILES,),
        in_specs=y_specs + [
            pl.BlockSpec((COND_ROWS, ADA_CHUNKS * D_MODEL), lambda i: (0, 0)),
            pl.BlockSpec((1, D_MODEL), lambda i: (0, 0)),
            pl.BlockSpec((D_MODEL, n), lambda i: (0, 0), pipeline_mode=pl.Buffered(1)),
        ],
        out_specs=[pl.BlockSpec((ROW_TILE, s), lambda i: (i, 0)) for s in splits],
        out_shape=[jax.ShapeDtypeStruct((T_ALL, s), F32) for s in splits],
        scratch_shapes=[pltpu.VMEM((D_MODEL, n), BF16)],
        compiler_params=_cparams(1),
    )(*y_args, ada_l, g.reshape(1, D_MODEL), w)


def _out_proj_kernel(n_in, n_y, gate_chunk, *refs):
    x_refs = refs[:2 * n_in]
    y_refs = refs[2 * n_in:2 * n_in + n_y]
    ada_ref, w_ref, o_ref, wbf_ref = refs[2 * n_in + n_y:]
    i = pl.program_id(0)

    @pl.when(i == 0)
    def _():
        wbf_ref[...] = w_ref[...].astype(BF16)

    acc = None
    lo = 0
    for xp_ref, xs_ref in zip(x_refs[0::2], x_refs[1::2]):
        k = xp_ref.shape[1]
        x = jnp.where(i < P_TILES, xp_ref[...], xs_ref[...])
        part = jnp.dot(x.astype(BF16), wbf_ref[lo:lo + k, :], preferred_element_type=F32)
        acc = part if acc is None else acc + part
        lo += k
    gate = _ada_chunk(ada_ref, _cond_row(i), gate_chunk)
    o_ref[...] = _stream_tile(y_refs, i) + gate * acc


def _out_proj_residual(xs, y, ada_l, w, gate_chunk):
    y_specs, y_args = _stream_specs(y)
    x_specs = []
    for xp, _ in xs:
        x_specs.append(pl.BlockSpec((ROW_TILE, xp.shape[1]), lambda i: (jnp.minimum(i, P_TILES - 1), 0)))
        x_specs.append(pl.BlockSpec((ROW_TILE, xp.shape[1]), lambda i: (jnp.maximum(i - P_TILES, 0), 0)))
    return pl.pallas_call(
        functools.partial(_out_proj_kernel, len(xs), len(y_args), gate_chunk),
        grid=(N_ROW_TILES,),
        in_specs=x_specs + y_specs + [
            pl.BlockSpec((COND_ROWS, ADA_CHUNKS * D_MODEL), lambda i: (0, 0)),
            pl.BlockSpec((D_MODEL, D_MODEL), lambda i: (0, 0), pipeline_mode=pl.Buffered(1)),
        ],
        out_specs=pl.BlockSpec((ROW_TILE, D_MODEL), lambda i: (i, 0)),
        out_shape=jax.ShapeDtypeStruct((T_ALL, D_MODEL), F32),
        scratch_shapes=[pltpu.VMEM((D_MODEL, D_MODEL), BF16)],
        compiler_params=_cparams(1),
    )(*[a for pair in xs for a in pair], *y_args, ada_l, w)


def _subhead_norm(x, g2):
    lane = lax.broadcasted_iota(jnp.int32, x.shape, 1)
    first = lane < HD_A
    xx = x * x
    s0 = jnp.sum(jnp.where(first, xx, 0.0), axis=-1, keepdims=True)
    s1 = jnp.sum(jnp.where(first, 0.0, xx), axis=-1, keepdims=True)
    r = jnp.where(first, lax.rsqrt(s0 / HD_A + EPS), lax.rsqrt(s1 / HD_A + EPS))
    return x * r * g2


def _rope(x, cos, sin):
    quarter = HD_A // 4
    lane = lax.broadcasted_iota(jnp.int32, x.shape, 1)
    lower = (lane % (2 * quarter)) < quarter
    swapped = jnp.where(lower, pltpu.roll(x, 2 * HD_A - quarter, 1), pltpu.roll(x, quarter, 1))
    return x * cos + swapped * sin


def _attn_kernel(lam_init, has_ctx, *refs):
    if has_ctx:
        (q_ref, k_ref, v_ref, ck_ref, cv_ref, cq_ref, sq_ref, ckk_ref, skk_ref,
         qg_ref, kg_ref, lp_ref, sg_ref, o_ref, kall_ref, vall_ref) = refs
    else:
        q_ref, k_ref, v_ref, qg_ref, kg_ref, lp_ref, sg_ref, o_ref, kn_ref = refs
    lp = lp_ref[...]
    lam = (jnp.exp(jnp.sum(lp[0:1] * lp[1:2], axis=-1, keepdims=True))
           - jnp.exp(jnp.sum(lp[2:3] * lp[3:4], axis=-1, keepdims=True)) + lam_init)

    def attend(q, k, v):
        probs = []
        for c in range(2):
            qc = q[:, c * HD_A:(c + 1) * HD_A].astype(BF16)
            kc = k[:, c * HD_A:(c + 1) * HD_A].astype(BF16)
            s = lax.dot_general(qc, kc, (((1,), (1,)), ((), ())), preferred_element_type=F32) * (HD_A ** -0.5)
            e = jnp.exp(s - jnp.max(s, axis=-1, keepdims=True))
            probs.append(e / jnp.sum(e, axis=-1, keepdims=True))
        o = _bdot(probs[0] - lam * probs[1], v)
        ms = jnp.mean(o * o, axis=-1, keepdims=True)
        return (o * lax.rsqrt(ms + EPS) * sg_ref[...]) * (1.0 - lam_init)

    if not has_ctx:
        for h in range(H_A):
            cols = slice(h * 2 * HD_A, (h + 1) * 2 * HD_A)
            k = _subhead_norm(k_ref[:, cols], kg_ref[...])
            kn_ref[:, cols] = k
            o_ref[:, cols] = attend(_subhead_norm(q_ref[:, cols], qg_ref[...]), k, v_ref[:, cols])
        return

    @pl.when(pl.program_id(2) == 0)
    def _():
        kall_ref[0:PAST_LEN, :] = ck_ref[...].astype(BF16)
        vall_ref[0:PAST_LEN, :] = cv_ref[...].astype(BF16)
        k_new = _rope(_subhead_norm(k_ref[...], kg_ref[...]), ckk_ref[...], skk_ref[...])
        kall_ref[PAST_LEN:, :] = k_new.astype(BF16)
        vall_ref[PAST_LEN:, :] = v_ref[...].astype(BF16)

    q = _rope(_subhead_norm(q_ref[...], qg_ref[...]), cq_ref[...], sq_ref[...])
    o_ref[...] = attend(q, kall_ref[...], vall_ref[...])


def _attention_prompt(qkv, qg2, kg2, lam_p, sub_g, lam_init):
    head = 2 * HD_A
    small = [
        pl.BlockSpec((1, head), lambda b: (0, 0)),
        pl.BlockSpec((1, head), lambda b: (0, 0)),
        pl.BlockSpec((4, HD_A), lambda b: (0, 0)),
        pl.BlockSpec((1, head), lambda b: (0, 0)),
    ]
    return pl.pallas_call(
        functools.partial(_attn_kernel, lam_init, False),
        grid=(BATCH,),
        in_specs=[
            pl.BlockSpec((SEQ, W_A), lambda b: (b, 0)),
            pl.BlockSpec((SEQ, W_A), lambda b: (b, 1)),
            pl.BlockSpec((SEQ, W_A), lambda b: (b, 2)),
        ] + small,
        out_specs=[pl.BlockSpec((SEQ, W_A), lambda b: (b, 0)),
                   pl.BlockSpec((SEQ, W_A), lambda b: (b, 0))],
        out_shape=[jax.ShapeDtypeStruct((T_P, W_A), F32), jax.ShapeDtypeStruct((T_P, W_A), F32)],
        compiler_params=_cparams(1),
    )(qkv, qkv, qkv, qg2, kg2, lam_p, sub_g)


def _attention_sample(qkv, cache_k, cache_v, cos, sin, qg2, kg2, lam_p, sub_g, lam_init):
    nh = H_A
    head = 2 * HD_A
    tq = ROW_TILE
    nq = DEC_SEQ // tq
    q_off = T_P // tq
    k_off = T_P // DEC_SEQ
    small = [
        pl.BlockSpec((1, head), lambda b, h, i: (0, 0)),
        pl.BlockSpec((1, head), lambda b, h, i: (0, 0)),
        pl.BlockSpec((4, HD_A), lambda b, h, i: (0, 0)),
        pl.BlockSpec((1, head), lambda b, h, i: (0, 0)),
    ]
    return pl.pallas_call(
        functools.partial(_attn_kernel, lam_init, True),
        grid=(DEC_BATCH, nh, nq),
        in_specs=[
            pl.BlockSpec((tq, head), lambda b, h, i: (q_off + b * nq + i, h)),
            pl.BlockSpec((DEC_SEQ, head), lambda b, h, i: (k_off + b, nh + h)),
            pl.BlockSpec((DEC_SEQ, head), lambda b, h, i: (k_off + b, 2 * nh + h)),
            pl.BlockSpec((None, PAST_LEN, head), lambda b, h, i: (b, 0, h)),
            pl.BlockSpec((None, PAST_LEN, head), lambda b, h, i: (b, 0, h)),
            pl.BlockSpec((tq, head), lambda b, h, i: (i, 0)),
            pl.BlockSpec((tq, head), lambda b, h, i: (i, 0)),
            pl.BlockSpec((DEC_SEQ, head), lambda b, h, i: (0, 0)),
            pl.BlockSpec((DEC_SEQ, head), lambda b, h, i: (0, 0)),
        ] + small,
        out_specs=pl.BlockSpec((tq, head), lambda b, h, i: (b * nq + i, h)),
        out_shape=jax.ShapeDtypeStruct((T_S, W_A), F32),
        scratch_shapes=[pltpu.VMEM((PAST_LEN + DEC_SEQ, head), BF16)] * 2,
        compiler_params=_cparams(3),
    )(qkv, qkv, qkv, cache_k, cache_v, cos, sin, cos, sin, qg2, kg2, lam_p, sub_g)


def _rope_tables():
    half = HD_A // 2
    nf = half // 2
    inv = ROPE_BASE ** (-np.arange(nf, dtype=np.float32) / nf)
    pos = np.arange(DEC_SEQ)
    row = (pos // GRID_W).astype(np.float32)
    col = (pos % GRID_W).astype(np.float32)
    ang_r = (row[:, None] * inv).astype(np.float32)
    ang_c = (col[:, None] * inv).astype(np.float32)
    ang = np.concatenate([ang_r, ang_r, ang_c, ang_c], axis=1)
    sign = np.concatenate([-np.ones(nf), np.ones(nf), -np.ones(nf), np.ones(nf)]).astype(np.float32)
    cos = np.cos(ang.astype(np.float64)).astype(np.float32)
    sin = (np.sin(ang.astype(np.float64)) * sign).astype(np.float32)
    return jnp.asarray(np.tile(cos, (1, 2))), jnp.asarray(np.tile(sin, (1, 2)))


def _mlstm_kernel(seq, has_ctx, *refs):
    if has_ctx:
        (q_ref, k_ref, cwq_ref, cwk_ref, cbq_ref, cbk_ref, v_ref, mo_ref, gi_ref, gf_ref,
         gbi_ref, gbf_ref, hn_ref, c0_ref, n0_ref, m0_ref, o_ref,
         qs_ref, ks_ref, hf_ref, hb_ref, cs_ref, rrow_ref, col_ref, wc_ref) = refs
    else:
        (q_ref, k_ref, cwq_ref, cwk_ref, cbq_ref, cbk_ref, v_ref, mo_ref, gi_ref, gf_ref,
         gbi_ref, gbf_ref, hn_ref, o_ref, c_out_ref, n_out_ref, m_out_ref,
         qs_ref, ks_ref, hf_ref, hb_ref, cs_ref, rrow_ref, col_ref, wc_ref) = refs
    nc = seq // CHUNK
    n_chain = 2 * H_B
    chains = [(d, h) for d in range(2) for h in range(H_B)]
    qs_ref[...] = _silu(_dwconv3(q_ref[...], cwq_ref[...], cbq_ref[...])) * (HD_B ** -0.5)
    ks_ref[...] = _silu(_dwconv3(k_ref[...], cwk_ref[...], cbk_ref[...]))

    rows = nc * n_chain
    lane = lax.broadcasted_iota(jnp.int32, (rows, 2 * CHUNK), 1)
    forward = lax.broadcasted_iota(jnp.int32, (rows, 2 * CHUNK), 0) % n_chain < H_B
    valid = lane < CHUNK

    def scan(x, op, fill):
        pre, suf = x, x
        sh = 1
        while sh < CHUNK:
            pre = op(pre, jnp.where(lane >= sh, pltpu.roll(pre, sh, 1), fill))
            suf = op(suf, jnp.where(lane + sh < CHUNK, pltpu.roll(suf, 2 * CHUNK - sh, 1), fill))
            sh *= 2
        return jnp.where(forward, pre, suf)

    gate_i = (gi_ref[...] + gbi_ref[...]).reshape(rows, 2 * CHUNK)
    lf = jnp.where(valid, _log_sigmoid(gf_ref[...] + gbf_ref[...]).reshape(rows, 2 * CHUNK), 0.0)
    b = scan(lf, jnp.add, 0.0)
    cmax = scan(jnp.where(valid, gate_i - b, -jnp.inf), jnp.maximum, -jnp.inf)
    b_last = jnp.sum(lf, axis=1, keepdims=True)
    g = b_last - b + gate_i
    g_max = jnp.max(jnp.where(valid, g, -jnp.inf), axis=1, keepdims=True)
    mm = m0_ref[...] if has_ctx else jnp.zeros((n_chain, 1), F32)
    mm_seq = []
    for p in range(nc):
        mm_seq.append(mm)
        seg = slice(p * n_chain, (p + 1) * n_chain)
        mm = jnp.maximum(b_last[seg] + mm, g_max[seg])
    mm_final = mm
    mm_prev = jnp.concatenate(mm_seq, axis=0)
    mm_next = jnp.concatenate(mm_seq[1:] + [mm_final], axis=0)
    m_t = jnp.maximum(b + mm_prev, b + cmax)
    rrow_ref[...] = (b - gate_i).reshape(nc, n_chain, 2 * CHUNK)
    wc_ref[...] = jnp.exp(b_last + mm_prev - mm_next).reshape(nc, n_chain, 1)
    per_row = [b, m_t, jnp.exp(b + mm_prev - m_t), jnp.exp(-m_t), jnp.exp(g - mm_next)]
    for j, arr in enumerate(per_row):
        by_time = arr.T
        for p in range(nc):
            col_ref[p, :, j * n_chain:(j + 1) * n_chain] = by_time[0:CHUNK, p * n_chain:(p + 1) * n_chain]

    t_idx = lax.broadcasted_iota(jnp.int32, (CHUNK, CHUNK), 0)
    s_idx = lax.broadcasted_iota(jnp.int32, (CHUNK, CHUNK), 1)
    for n, (d, h) in enumerate(chains):
        cs_ref[n] = c0_ref[d, h] if has_ctx else jnp.zeros((HD_B, HD_B), F32)

    def out_step(p, n_states):
        cols = col_ref[p]
        rrows = rrow_ref[p]
        wcs = wc_ref[p]
        new_states = []
        for n, (d, h) in enumerate(chains):
            c = p if d == 0 else nc - 1 - p
            r0 = pl.multiple_of(c * CHUNK, CHUNK)
            hcols = slice(h * HD_B, (h + 1) * HD_B)
            qt = qs_ref[pl.ds(r0, CHUNK), hcols]
            kt = ks_ref[pl.ds(r0, CHUNK), hcols]
            vt = v_ref[pl.ds(r0, CHUNK), hcols]
            b_col, m_t, w_inter, e_inv, w_k = (cols[:, j * n_chain + n:j * n_chain + n + 1] for j in range(5))
            mask = (s_idx <= t_idx) if d == 0 else (s_idx >= t_idx)
            decay = jnp.exp(jnp.where(mask, b_col - rrows[n:n + 1, 0:CHUNK], NEG) - m_t)
            qk = lax.dot_general(qt.astype(BF16), kt.astype(BF16), (((1,), (1,)), ((), ())),
                                 preferred_element_type=F32)
            s = qk * decay
            cm = cs_ref[n]
            nm = n_states[n]
            cq = lax.dot_general(qt.astype(BF16), cm.astype(BF16), (((1,), (1,)), ((), ())),
                                 preferred_element_type=F32)
            num = _bdot(s, vt) + w_inter * cq
            nq = jnp.sum(s, axis=-1, keepdims=True) + w_inter * jnp.sum(qt * nm, axis=-1, keepdims=True)
            hdir_ref = hf_ref if d == 0 else hb_ref
            hdir_ref[pl.ds(r0, CHUNK), hcols] = num / jnp.maximum(jnp.abs(nq), e_inv)
            w_c = wcs[n:n + 1, :]
            vw = (vt * w_k).astype(BF16)
            cs_ref[n] = w_c * cm + lax.dot_general(vw, kt.astype(BF16), (((0,), (0,)), ((), ())),
                                                   preferred_element_type=F32)
            new_states.append(w_c * nm + jnp.sum(kt * w_k, axis=0, keepdims=True))
        return tuple(new_states)

    if has_ctx:
        n_init = tuple(n0_ref[d, h] for d, h in chains)
    else:
        n_init = tuple(jnp.zeros((1, HD_B), F32) for _ in chains)
    n_final = lax.fori_loop(0, nc, out_step, n_init)
    if not has_ctx:
        for n, (d, h) in enumerate(chains):
            c_out_ref[d, h] = cs_ref[n]
            n_out_ref[d, h] = n_final[n]
            m_out_ref[d, h] = jnp.broadcast_to(mm_final[n:n + 1, :], (1, HD_B))

    for h in range(H_B):
        hcols = slice(h * HD_B, (h + 1) * HD_B)
        hh = hf_ref[:, hcols] + hb_ref[:, hcols]
        ms = jnp.mean(hh * hh, axis=-1, keepdims=True)
        o_ref[:, hcols] = (hh * lax.rsqrt(ms + EPS) * hn_ref[:, hcols]) * _sigmoid(mo_ref[:, hcols])


def _mlstm(mqk, mv, mo, mg_stream, conv_w, conv_b, gate_b, hn_g, *, seq, nbatch, row_off, ctx=None):
    nh = H_B
    nc = seq // CHUNK
    has_ctx = ctx is not None
    gt = mg_stream.reshape(nbatch, nc, CHUNK, 2, 2, nh).transpose(0, 1, 3, 4, 5, 2)
    pad = ((0, 0), (0, 0), (0, 0), (0, CHUNK))
    gates = [jnp.pad(jnp.concatenate([gt[:, :, 0, j], gt[:, ::-1, 1, j]], axis=2), pad) for j in range(2)]
    gate_bias = [jnp.concatenate([gate_b[0, j], gate_b[1, j]]).reshape(2 * nh, 1) for j in range(2)]
    blk = lambda col: pl.BlockSpec((seq, W_B), lambda b, col=col: (row_off + b, col))
    gate_blk = pl.BlockSpec((None, nc, 2 * nh, 2 * CHUNK), lambda b: (b, 0, 0, 0))
    in_specs = [
        blk(0), blk(1),
        pl.BlockSpec((3, W_B), lambda b: (0, 0)),
        pl.BlockSpec((3, W_B), lambda b: (0, 1)),
        pl.BlockSpec((1, W_B), lambda b: (0, 0)),
        pl.BlockSpec((1, W_B), lambda b: (0, 1)),
        blk(0), blk(0),
        gate_blk, gate_blk,
        pl.BlockSpec((2 * nh, 1), lambda b: (0, 0)),
        pl.BlockSpec((2 * nh, 1), lambda b: (0, 0)),
        pl.BlockSpec((1, W_B), lambda b: (0, 0)),
    ]
    args = [mqk, mqk, conv_w, conv_w, conv_b, conv_b, mv, mo, gates[0], gates[1],
            gate_bias[0], gate_bias[1], hn_g.reshape(1, W_B)]
    o_spec = pl.BlockSpec((seq, W_B), lambda b: (b, 0))
    o_shape = jax.ShapeDtypeStruct((nbatch * seq, W_B), F32)
    state_blk = lambda rows: pl.BlockSpec((None, 2, nh, rows, HD_B), lambda b: (b, 0, 0, 0, 0))
    if has_ctx:
        c0, n0, m0 = ctx
        in_specs += [state_blk(HD_B), state_blk(1), pl.BlockSpec((None, 2 * nh, 1), lambda b: (b, 0, 0))]
        args += [c0, n0.reshape(nbatch, 2, nh, 1, HD_B), m0.reshape(nbatch, 2 * nh, 1)]
        out_specs, out_shape = o_spec, o_shape
    else:
        out_specs = [o_spec, state_blk(HD_B), state_blk(1), state_blk(1)]
        out_shape = [
            o_shape,
            jax.ShapeDtypeStruct((nbatch, 2, nh, HD_B, HD_B), F32),
            jax.ShapeDtypeStruct((nbatch, 2, nh, 1, HD_B), F32),
            jax.ShapeDtypeStruct((nbatch, 2, nh, 1, HD_B), F32),
        ]
    return pl.pallas_call(
        functools.partial(_mlstm_kernel, seq, has_ctx),
        grid=(nbatch,),
        in_specs=in_specs,
        out_specs=out_specs,
        out_shape=out_shape,
        scratch_shapes=[pltpu.VMEM((seq, W_B), F32)] * 4 + [
            pltpu.VMEM((2 * nh, HD_B, HD_B), F32),
            pltpu.VMEM((nc, 2 * nh, 2 * CHUNK), F32),
            pltpu.VMEM((nc, CHUNK, 5 * 2 * nh), F32),
            pltpu.VMEM((nc, 2 * nh, 1), F32),
        ],
        compiler_params=_cparams(1),
    )(*args)


def _dft_mats(L):
    f = np.arange(L)[:, None]
    j = np.arange(L)[None, :]
    ang = 2.0 * np.pi * ((f * j) % (2 * L)) / (2 * L)
    cm = np.cos(ang)
    sm = np.sin(ang)
    alt = (1.0 - 2.0 * (np.arange(L) % 2))
    fwd_b = -sm
    fwd_b[0, :] = alt
    fwd = np.concatenate([cm, fwd_b], axis=0)
    wgt = np.where(np.arange(L) == 0, 1.0, 2.0)[None, :]
    inv_a = cm.T * wgt
    inv_b = -2.0 * sm.T
    inv_b[:, 0] = alt
    inv = np.concatenate([inv_a, inv_b], axis=1) / (2 * L)
    return jnp.asarray(fwd.astype(np.float32)), jnp.asarray(inv.astype(np.float32))


def _hyena_feats(L):
    t = np.linspace(0.0, 1.0, L, dtype=np.float32)
    wpos = (2.0 * math.pi * np.arange(L, dtype=np.float32) / L).astype(np.float32)
    fb = np.linspace(1e-4, HY_BANDS - 1, HY_BANDS, dtype=np.float32)
    z = (wpos[:, None] * fb).astype(np.float32)
    feats = np.concatenate([t[:, None], np.cos(z), -np.sin(z)], axis=-1).astype(np.float32)
    deltas = np.abs(np.linspace(math.log(HY_TARGET) / HY_SLOW_PCT, math.log(HY_TARGET) / HY_FAST_PCT,
                                D_MODEL, dtype=np.float32))
    decay = np.exp(-t[:, None] * deltas).astype(np.float32)
    return jnp.asarray(feats), jnp.asarray(decay)


def _filter_kernel(L, feats_ref, w1_ref, b1_ref, fr1_ref, w2_ref, b2_ref, fr2_ref, w3f_ref, w3b_ref,
                   decay_ref, fwd_ref, o_ref, hdn_ref):
    @pl.when((pl.program_id(0) == 0) & (pl.program_id(1) == 0))
    def _():
        h1 = jnp.sin(fr1_ref[...] * (jnp.dot(feats_ref[...], w1_ref[...], precision=HIGHEST,
                                             preferred_element_type=F32) + b1_ref[...]))
        hdn_ref[...] = jnp.sin(fr2_ref[...] * (jnp.dot(h1, w2_ref[...], precision=HIGHEST,
                                                       preferred_element_type=F32) + b2_ref[...]))

    hdn = hdn_ref[...]
    decay = decay_ref[...]
    f_fwd = jnp.dot(hdn, w3f_ref[...], precision=HIGHEST, preferred_element_type=F32) * decay
    f_bwd = jnp.dot(hdn, w3b_ref[...], precision=HIGHEST, preferred_element_type=F32) * decay
    row = lax.broadcasted_iota(jnp.int32, f_bwd.shape, 0)
    f_bwd = jnp.where(row == 0, 0.0, f_bwd)
    fwd = fwd_ref[...]
    p = _bdot(fwd, f_fwd)
    q = _bdot(fwd, f_bwd)
    first = row == 0
    o_ref[0:L, :] = p[0:L] + q[0:L]
    o_ref[L:2 * L, :] = p[L:2 * L] + jnp.where(first, q[L:2 * L], -q[L:2 * L])


def _hyena_filter_spectrum(L, fwd_bf, w1, b1, fr1, w2, b2, fr2, w3):
    feats, decay = _hyena_feats(L)
    td = 512
    nd = D_MODEL // td
    emb = feats.shape[1]
    vec = lambda a: a.reshape(1, HY_FH)
    full = lambda shape: pl.BlockSpec(shape, lambda o, j: (0, 0))
    return pl.pallas_call(
        functools.partial(_filter_kernel, L),
        grid=(HY_ORDER, nd),
        in_specs=[
            full((L, emb)), full((emb, HY_FH)), full((1, HY_FH)), full((1, HY_FH)),
            full((HY_FH, HY_FH)), full((1, HY_FH)), full((1, HY_FH)),
            pl.BlockSpec((HY_FH, td), lambda o, j: (0, o * 2 * nd + j)),
            pl.BlockSpec((HY_FH, td), lambda o, j: (0, o * 2 * nd + nd + j)),
            pl.BlockSpec((L, td), lambda o, j: (0, j)),
            full((2 * L, L)),
        ],
        out_specs=pl.BlockSpec((2 * L, td), lambda o, j: (0, o * nd + j)),
        out_shape=jax.ShapeDtypeStruct((2 * L, HY_ORDER * D_MODEL), F32),
        scratch_shapes=[pltpu.VMEM((L, HY_FH), F32)],
        compiler_params=_cparams(2),
    )(feats, w1, vec(b1), vec(fr1), w2, vec(b2), vec(fr2), w3, w3, decay, fwd_bf)


def _spectral_conv(u, fwd, inv, kspec, L):
    uf = jnp.dot(fwd, u.astype(BF16), preferred_element_type=F32)
    ua, ub = uf[0:L], uf[L:2 * L]
    ka, kb = kspec[0:L], kspec[L:2 * L]
    first = lax.broadcasted_iota(jnp.int32, ua.shape, 0) == 0
    ya = ua * ka - jnp.where(first, 0.0, ub * kb)
    yb = jnp.where(first, ub * kb, ua * kb + ub * ka)
    y = jnp.concatenate([ya, yb], axis=0).astype(BF16)
    return jnp.dot(inv, y, preferred_element_type=F32)


def _hyena_kernel(L, zv_ref, z1_ref, z2_ref, cwv_ref, cw1_ref, cw2_ref, cbv_ref, cb1_ref, cb2_ref,
                  fwd_ref, inv_ref, k0_ref, k1_ref, bias0_ref, bias1_ref, o_ref):
    fwd = fwd_ref[...]
    inv = inv_ref[...]
    v = _dwconv3(zv_ref[...], cwv_ref[...], cbv_ref[...])
    x1 = _dwconv3(z1_ref[...], cw1_ref[...], cb1_ref[...])
    x2 = _dwconv3(z2_ref[...], cw2_ref[...], cb2_ref[...])
    z = x1 * (_spectral_conv(v, fwd, inv, k0_ref[...], L) + v * bias0_ref[...])
    o_ref[...] = x2 * (_spectral_conv(z, fwd, inv, k1_ref[...], L) + z * bias1_ref[...])


def _hyena_core(zproj, conv_w, conv_b, fwd_bf, inv_bf, kspec, bias, *, seq, nbatch, row_off, td):
    nd = D_MODEL // td
    zblk = lambda part: pl.BlockSpec((seq, td), lambda b, j, part=part: (row_off + b, part * nd + j))
    cwblk = lambda part: pl.BlockSpec((3, td), lambda b, j, part=part: (0, part * nd + j))
    cbblk = lambda part: pl.BlockSpec((1, td), lambda b, j, part=part: (0, part * nd + j))
    return pl.pallas_call(
        functools.partial(_hyena_kernel, seq),
        grid=(nbatch, nd),
        in_specs=[
            zblk(0), zblk(1), zblk(2), cwblk(0), cwblk(1), cwblk(2), cbblk(0), cbblk(1), cbblk(2),
            pl.BlockSpec((2 * seq, seq), lambda b, j: (0, 0), pipeline_mode=pl.Buffered(1)),
            pl.BlockSpec((seq, 2 * seq), lambda b, j: (0, 0), pipeline_mode=pl.Buffered(1)),
            pl.BlockSpec((2 * seq, td), lambda b, j: (0, j)),
            pl.BlockSpec((2 * seq, td), lambda b, j: (0, nd + j)),
            pl.BlockSpec((None, 1, td), lambda b, j: (0, 0, j)),
            pl.BlockSpec((None, 1, td), lambda b, j: (1, 0, j)),
        ],
        out_specs=pl.BlockSpec((seq, td), lambda b, j: (b, j)),
        out_shape=jax.ShapeDtypeStruct((nbatch * seq, D_MODEL), F32),
        compiler_params=_cparams(2),
    )(zproj, zproj, zproj, conv_w, conv_w, conv_w, conv_b, conv_b, conv_b,
      fwd_bf, inv_bf, kspec, kspec, bias.reshape(HY_ORDER, 1, D_MODEL), bias.reshape(HY_ORDER, 1, D_MODEL))


def _router_kernel(x_ref, ada_ref, g_ref, rw_ref, rb_ref, h_ref, idx_ref, wt_ref):
    i = pl.program_id(0)
    row = _cond_row(i)
    h = _modulate(x_ref[...], g_ref[...], _ada_chunk(ada_ref, row, 3), _ada_chunk(ada_ref, row, 4))
    h_ref[...] = h.astype(BF16)
    logits = lax.dot_general(rw_ref[...], h, (((1,), (1,)), ((), ())), precision=HIGHEST,
                             preferred_element_type=F32) + rb_ref[...]
    expert = lax.broadcasted_iota(jnp.int32, logits.shape, 0)
    slot = lax.broadcasted_iota(jnp.int32, (TOP_K, logits.shape[1]), 0)
    vals = jnp.zeros((TOP_K, logits.shape[1]), F32)
    idxs = jnp.zeros((TOP_K, logits.shape[1]), jnp.int32)
    cur = logits
    for k in range(TOP_K):
        m = jnp.max(cur, axis=0, keepdims=True)
        a = jnp.min(jnp.where(cur == m, expert, N_EXPERTS), axis=0, keepdims=True)
        vals = jnp.where(slot == k, m, vals)
        idxs = jnp.where(slot == k, a, idxs)
        cur = jnp.where(expert == a, -jnp.inf, cur)
    e = jnp.exp(vals - vals[0:1])
    wt_ref[...] = e / jnp.sum(e, axis=0, keepdims=True)
    idx_ref[...] = idxs


def _router(y, ada_l, g, router_w, router_b):
    return pl.pallas_call(
        _router_kernel,
        grid=(N_ROW_TILES,),
        in_specs=[
            pl.BlockSpec((ROW_TILE, D_MODEL), lambda i: (i, 0)),
            pl.BlockSpec((COND_ROWS, ADA_CHUNKS * D_MODEL), lambda i: (0, 0)),
            pl.BlockSpec((1, D_MODEL), lambda i: (0, 0)),
            pl.BlockSpec((N_EXPERTS, D_MODEL), lambda i: (0, 0)),
            pl.BlockSpec((N_EXPERTS, 1), lambda i: (0, 0)),
        ],
        out_specs=[
            pl.BlockSpec((ROW_TILE, D_MODEL), lambda i: (i, 0)),
            pl.BlockSpec((TOP_K, ROW_TILE), lambda i: (0, i)),
            pl.BlockSpec((TOP_K, ROW_TILE), lambda i: (0, i)),
        ],
        out_shape=[
            jax.ShapeDtypeStruct((T_ALL, D_MODEL), BF16),
            jax.ShapeDtypeStruct((TOP_K, T_ALL), jnp.int32),
            jax.ShapeDtypeStruct((TOP_K, T_ALL), F32),
        ],
        compiler_params=_cparams(1),
    )(y, ada_l, g.reshape(1, D_MODEL), router_w.T, router_b.reshape(N_EXPERTS, 1))


def _deinterleave_matrix():
    s = np.zeros((256, 256), np.float32)
    j = np.arange(128)
    s[2 * j, j] = 1.0
    s[2 * j + 1, 128 + j] = 1.0
    return jnp.asarray(s)


def _weight_copies(layer, e, w1_hbm, w2_hbm, w1s_ref, w2s_ref, sem):
    copies = []
    r1 = D_MODEL // W1_DMA_CHUNKS
    for c in range(W1_DMA_CHUNKS):
        copies.append(pltpu.make_async_copy(w1_hbm.at[layer, e, pl.ds(c * r1, r1)],
                                            w1s_ref.at[pl.ds(c * r1, r1)], sem.at[c]))
    r2 = D_FF // W2_DMA_CHUNKS
    for c in range(W2_DMA_CHUNKS):
        copies.append(pltpu.make_async_copy(w2_hbm.at[layer, e, pl.ds(c * r2, r2)],
                                            w2s_ref.at[pl.ds(c * r2, r2)], sem.at[W1_DMA_CHUNKS + c]))
    return copies


def _expert_kernel(layer, te_ref, tf_ref, ne_ref, nu_ref, src_ref, xo_ref, x_ref, b1_ref, b2_ref, wt_ref, s_ref,
                   w1_hbm, w2_hbm, o_hbm, w1s_ref, w2s_ref, w1p_ref, w2p_ref, acc_ref, out_ref, wsem, osem):
    i = pl.program_id(0)
    half = 128
    copies = functools.partial(_weight_copies, layer, w1_hbm=w1_hbm, w2_hbm=w2_hbm,
                               w1s_ref=w1s_ref, w2s_ref=w2s_ref, sem=wsem)

    @pl.when(i == 0)
    def _():
        acc_ref[...] = jnp.zeros_like(acc_ref)
        out_ref[...] = jnp.zeros_like(out_ref)
        for cp in copies(te_ref[0]):
            cp.start()

    @pl.when(tf_ref[i] == 1)
    def _():
        for cp in copies(te_ref[i]):
            cp.wait()
        s = s_ref[...].astype(BF16)
        for c in range(2 * D_FF // 256):
            blk = jnp.dot(w1s_ref[:, c * 256:(c + 1) * 256].astype(BF16), s, preferred_element_type=F32)
            w1p_ref[:, c * half:(c + 1) * half] = blk[:, :half].astype(BF16)
            w1p_ref[:, D_FF + c * half:D_FF + (c + 1) * half] = blk[:, half:].astype(BF16)
        w2p_ref[...] = w2s_ref[...].astype(BF16)

        @pl.when(ne_ref[i] >= 0)
        def _():
            for cp in copies(ne_ref[i]):
                cp.start()

    @pl.when(i <= nu_ref[0])
    def _():
        base = i * MOE_TILE
        prev = (i + 1) % 2
        for r0 in range(0, MOE_TILE, SCATTER_GROUP):
            toks = [src_ref[base + r0 + g] for g in range(SCATTER_GROUP)]
            cur = [acc_ref[pl.ds(toks[g], 1), :] for g in range(SCATTER_GROUP)]
            add = [out_ref[prev, r0 + g:r0 + g + 1, :] for g in range(SCATTER_GROUP)]
            for g in range(SCATTER_GROUP):
                acc_ref[pl.ds(toks[g], 1), :] = cur[g] + add[g]
        a = jnp.dot(x_ref[...], w1p_ref[...], preferred_element_type=F32) + b1_ref[...]
        glu = jnp.minimum(a[:, :D_FF], SWIGLU_LIMIT)
        lin = jnp.clip(a[:, D_FF:], -SWIGLU_LIMIT, SWIGLU_LIMIT)
        hid = glu * _sigmoid(SWIGLU_ALPHA * glu) * (lin + 1.0)
        out = jnp.dot(hid.astype(BF16), w2p_ref[...], preferred_element_type=F32) + b2_ref[...]
        out_ref[i % 2] = out * wt_ref[...]

    @pl.when(i == pl.num_programs(0) - 1)
    def _():
        cp = pltpu.make_async_copy(acc_ref.at[pl.ds(0, T_ALL)], o_hbm, osem)
        cp.start()
        cp.wait()


def _experts(layer, x_sorted, w_sorted, plan, w1, b1p, w2, b2):
    tile_expert, tile_first, next_expert, n_used, src, x_off = plan
    grid_spec = pltpu.PrefetchScalarGridSpec(
        num_scalar_prefetch=6,
        grid=(MOE_TILES,),
        in_specs=[
            pl.BlockSpec((pl.Element(MOE_TILE), pl.Element(D_MODEL)),
                         lambda i, te, tf, ne, nu, src, xo: (pl.multiple_of(xo[i], X_ALIGN), 0)),
            pl.BlockSpec((None, None, 1, 2 * D_FF), lambda i, te, *_: (layer, te[i], 0, 0)),
            pl.BlockSpec((None, None, 1, D_MODEL), lambda i, te, *_: (layer, te[i], 0, 0)),
            pl.BlockSpec((MOE_TILE, 1), lambda i, te, *_: (i, 0)),
            pl.BlockSpec((256, 256), lambda i, te, *_: (0, 0)),
            pl.BlockSpec(memory_space=pl.ANY),
            pl.BlockSpec(memory_space=pl.ANY),
        ],
        out_specs=pl.BlockSpec(memory_space=pl.ANY),
        scratch_shapes=[
            pltpu.VMEM((D_MODEL, 2 * D_FF), F32),
            pltpu.VMEM((D_FF, D_MODEL), F32),
            pltpu.VMEM((D_MODEL, 2 * D_FF), BF16),
            pltpu.VMEM((D_FF, D_MODEL), BF16),
            pltpu.VMEM((ACC_ROWS, D_MODEL), F32),
            pltpu.VMEM((2, MOE_TILE, D_MODEL), F32),
            pltpu.SemaphoreType.DMA((W1_DMA_CHUNKS + W2_DMA_CHUNKS,)),
            pltpu.SemaphoreType.DMA(()),
        ],
    )
    return pl.pallas_call(
        functools.partial(_expert_kernel, layer),
        grid_spec=grid_spec,
        out_shape=jax.ShapeDtypeStruct((T_ALL, D_MODEL), F32),
        compiler_params=_cparams(1),
    )(tile_expert, tile_first, next_expert, n_used, src, x_off, x_sorted, b1p, b2, w_sorted,
      _deinterleave_matrix(), w1, w2)


def _combine_kernel(first_tile, y_ref, a_ref, ada_ref, o_ref):
    gate = _ada_chunk(ada_ref, _cond_row(first_tile + pl.program_id(0)), 5)
    o_ref[...] = y_ref[...] + gate * a_ref[...]


def _combine(y, acc, ada_l, first_tile=0, n_tiles=N_ROW_TILES):
    return pl.pallas_call(
        functools.partial(_combine_kernel, first_tile),
        grid=(n_tiles,),
        in_specs=[
            pl.BlockSpec((ROW_TILE, D_MODEL), lambda i: (first_tile + i, 0)),
            pl.BlockSpec((ROW_TILE, D_MODEL), lambda i: (first_tile + i, 0)),
            pl.BlockSpec((COND_ROWS, ADA_CHUNKS * D_MODEL), lambda i: (0, 0)),
        ],
        out_specs=pl.BlockSpec((ROW_TILE, D_MODEL), lambda i: (i, 0)),
        out_shape=jax.ShapeDtypeStruct((n_tiles * ROW_TILE, D_MODEL), F32),
        compiler_params=_cparams(1),
    )(y, acc, ada_l)


def _routing_plan(idx, wts):
    eid = idx.reshape(-1)
    order = jnp.argsort(eid, stable=True).astype(jnp.int32)
    experts = jnp.arange(N_EXPERTS, dtype=jnp.int32)
    counts = jnp.sum(eid[:, None] == experts[None, :], axis=0).astype(jnp.int32)
    ntiles = (counts + MOE_TILE - 1) // MOE_TILE
    tile_end = jnp.cumsum(ntiles).astype(jnp.int32)
    tile_begin = tile_end - ntiles
    cstarts = (jnp.cumsum(counts) - counts).astype(jnp.int32)
    n_used = tile_end[-1]
    tile = jnp.arange(MOE_TILES, dtype=jnp.int32)
    te = jnp.minimum(jnp.sum(tile[:, None] >= tile_end[None, :], axis=1), N_EXPERTS - 1).astype(jnp.int32)
    used = tile < n_used
    prev = jnp.concatenate([jnp.full((1,), -1, jnp.int32), te[:-1]])
    first = (te != prev) & used

    def pick(onehot, table):
        return jnp.sum(jnp.where(onehot, table[None, :], 0), axis=1).astype(jnp.int32)

    tile_is = te[:, None] == experts[None, :]
    later = (experts[None, :] > experts[:, None]) & (ntiles[None, :] > 0)
    following = jnp.min(jnp.where(later, experts[None, :], N_EXPERTS), axis=1)
    following = jnp.where(following < N_EXPERTS, following, -1)
    next_expert = jnp.where(first, pick(tile_is, following), -1).astype(jnp.int32)
    tile_in_expert = tile - pick(tile_is, tile_begin)
    off = tile_in_expert[:, None] * MOE_TILE + jnp.arange(MOE_TILE, dtype=jnp.int32)[None, :]
    valid = (off < pick(tile_is, counts)[:, None]) & used[:, None]
    assign = order[jnp.clip(pick(tile_is, cstarts)[:, None] + off, 0, N_ASSIGN - 1)]
    token = assign // TOP_K
    src = jnp.where(valid, token, SPARE_ROW).reshape(MOE_ROWS).astype(jnp.int32)
    src = jnp.concatenate([jnp.full((MOE_TILE,), SPARE_ROW, jnp.int32), src])
    w_sorted = jnp.where(valid, wts.reshape(-1)[assign], 0.0).reshape(MOE_ROWS, 1)
    seg = ((counts + X_ALIGN - 1) // X_ALIGN) * X_ALIGN
    seg_end = jnp.cumsum(seg).astype(jnp.int32)
    seg_begin = seg_end - seg
    x_off = jnp.where(used, pick(tile_is, seg_begin) + tile_in_expert * MOE_TILE, 0).astype(jnp.int32)
    group = jnp.arange(X_ROWS // X_ALIGN, dtype=jnp.int32) * X_ALIGN
    group_is = (group[:, None] >= seg_begin[None, :]) & (group[:, None] < seg_end[None, :])
    xoffset = (group - pick(group_is, seg_begin))[:, None] + jnp.arange(X_ALIGN, dtype=jnp.int32)[None, :]
    xassign = order[jnp.clip(pick(group_is, cstarts)[:, None] + xoffset, 0, N_ASSIGN - 1)]
    gather_row = jnp.where(xoffset < pick(group_is, counts)[:, None], xassign // TOP_K, 0).reshape(X_ROWS)
    plan = (te, first.astype(jnp.int32), next_expert, n_used.reshape(1), src, x_off)
    return plan, gather_row, w_sorted


def _moe(layer, y, ada_l, g, router_w, router_b, w1, b1p, w2, b2):
    h, idx_t, wts_t = _router(y, ada_l, g, router_w, router_b)
    plan, gather_row, w_sorted = _routing_plan(idx_t.T, wts_t.T)
    x_sorted = jnp.take(h, gather_row, axis=0, mode="clip")
    acc = _experts(layer, x_sorted, w_sorted, plan, w1, b1p, w2, b2)
    if layer == DEPTH - 1:
        return (_combine(y, acc, ada_l, 0, P_TILES), _combine(y, acc, ada_l, P_TILES, N_ROW_TILES - P_TILES))
    return _combine(y, acc, ada_l)


def kernel(x_prompt, x_sample, cache_attn_k, cache_attn_v, state_mlstm_C, state_mlstm_n, state_mlstm_m, c, c_ctx, ada_w, ada_b, norm_mix_g, norm_ffn_g, ab_w_in, ab_w_out, da_qnorm_g, da_knorm_g, da_lambda, da_subnorm_g, ml_conv_w, ml_conv_b, ml_gate_b, ml_headnorm_g, hy_w_in, hy_w_out, hy_conv_w, hy_conv_b, hy_f_w1, hy_f_b1, hy_f_freq1, hy_f_w2, hy_f_b2, hy_f_freq2, hy_f_w3, hy_bias, router_w, router_b, moe_w1, moe_b1, moe_w2, moe_b2):
    y = (x_prompt.reshape(T_P, D_MODEL), x_sample.reshape(T_S, D_MODEL))
    cond = jnp.concatenate([c_ctx[None, :], c, jnp.zeros((COND_ROWS - 1 - DEC_BATCH, D_MODEL), F32)], axis=0)
    ada = _ada_table(cond, ada_w, ada_b)
    b1p = moe_b1.reshape(DEPTH, N_EXPERTS, D_FF, 2).swapaxes(2, 3).reshape(DEPTH, N_EXPERTS, 1, 2 * D_FF)
    b2r = moe_b2.reshape(DEPTH, N_EXPERTS, 1, D_MODEL)
    new_k, new_v, new_c, new_n, new_m = [], [], [], [], []
    for layer in range(DEPTH):
        ada_l = ada[layer]
        if layer % 2 == 0:
            e = layer // 2
            lam_init = 0.8 - 0.6 * math.exp(-0.3 * layer)
            qkv, mqk, mv, mo, mg = _modulated_proj(
                y, ada_l, norm_mix_g[layer], ab_w_in[e], (3 * W_A, 2 * W_B, W_B, W_B, 4 * H_B))
            qg2 = jnp.tile(da_qnorm_g[e], 2).reshape(1, 2 * HD_A)
            kg2 = jnp.tile(da_knorm_g[e], 2).reshape(1, 2 * HD_A)
            sub_g = da_subnorm_g[e].reshape(1, 2 * HD_A)
            oa_p, k_norm = _attention_prompt(qkv, qg2, kg2, da_lambda[e], sub_g, lam_init)
            cos, sin = _rope_tables()
            oa_s = _attention_sample(
                qkv, cache_attn_k[:, e].reshape(DEC_BATCH, PAST_LEN, W_A),
                cache_attn_v[:, e].reshape(DEC_BATCH, PAST_LEN, W_A), cos, sin,
                qg2, kg2, da_lambda[e], sub_g, lam_init)
            ob_p, c_new, n_new, m_new = _mlstm(
                mqk, mv, mo, mg[:T_P], ml_conv_w[e], ml_conv_b[e].reshape(1, 2 * W_B), ml_gate_b[e],
                ml_headnorm_g[e], seq=SEQ, nbatch=BATCH, row_off=0)
            ob_s = _mlstm(
                mqk, mv, mo, mg[T_P:], ml_conv_w[e], ml_conv_b[e].reshape(1, 2 * W_B), ml_gate_b[e],
                ml_headnorm_g[e], seq=DEC_SEQ, nbatch=DEC_BATCH, row_off=T_P // DEC_SEQ,
                ctx=(state_mlstm_C[:, e], state_mlstm_n[:, e], state_mlstm_m[:, e]))
            y = _out_proj_residual([(oa_p, oa_s), (ob_p, ob_s)], y, ada_l, ab_w_out[e], 2)
            new_k.append(k_norm.reshape(BATCH, SEQ, H_A, 2, HD_A))
            new_v.append(qkv[:T_P, 2 * W_A:].reshape(BATCH, SEQ, H_A, 2 * HD_A))
            new_c.append(c_new)
            new_n.append(n_new.reshape(BATCH, 2, H_B, HD_B))
            new_m.append(m_new[..., 0, 0])
        else:
            o = layer // 2
            (zproj,) = _modulated_proj(y, ada_l, norm_mix_g[layer], hy_w_in[o], (HY_PROJ,))
            cores = []
            for seq, nbatch, row_off, td in ((SEQ, BATCH, 0, 512), (DEC_SEQ, DEC_BATCH, T_P // DEC_SEQ, 256)):
                fwd, inv = _dft_mats(seq)
                fwd_bf, inv_bf = fwd.astype(BF16), inv.astype(BF16)
                kspec = _hyena_filter_spectrum(seq, fwd_bf, hy_f_w1[o], hy_f_b1[o], hy_f_freq1[o], hy_f_w2[o],
                                               hy_f_b2[o], hy_f_freq2[o], hy_f_w3[o])
                cores.append(_hyena_core(zproj, hy_conv_w[o], hy_conv_b[o].reshape(1, HY_PROJ), fwd_bf, inv_bf,
                                         kspec, hy_bias[o], seq=seq, nbatch=nbatch, row_off=row_off, td=td))
            y = _out_proj_residual([tuple(cores)], y, ada_l, hy_w_out[o], 2)
        y = _moe(layer, y, ada_l, norm_ffn_g[layer], router_w[layer], router_b[layer],
                 moe_w1, b1p, moe_w2, b2r)
    y_p = y[0].reshape(BATCH, SEQ, D_MODEL)
    y_s = y[1].reshape(DEC_BATCH, DEC_SEQ, D_MODEL)
    return (y_p, y_s, jnp.stack(new_k, axis=1), jnp.stack(new_v, axis=1), jnp.stack(new_c, axis=1),
            jnp.stack(new_n, axis=1), jnp.stack(new_m, axis=1))
```

```python
import functools
import math

import numpy as np
import jax
import jax.numpy as jnp
from jax import lax
from jax.experimental import pallas as pl
from jax.experimental.pallas import tpu as pltpu

D_MODEL = 1024
BATCH = 16
SEQ = 256
DEPTH = 2
DEC_BATCH = 2
DEC_SEQ = 1024
PAST_LEN = 256
GRID_W = 64
W_A = D_MODEL // 2
HD_A = 64
H_A = W_A // (2 * HD_A)
W_B = D_MODEL - W_A
HD_B = 128
H_B = W_B // HD_B
AB_PROJ = 3 * W_A + 4 * W_B + 4 * H_B
ROPE_BASE = 10000.0
CHUNK = 64
HY_ORDER = 2
HY_PROJ = (HY_ORDER + 1) * D_MODEL
HY_BANDS = 8
HY_FH = 64
HY_TARGET = 1e-2
HY_FAST_PCT = 0.3
HY_SLOW_PCT = 1.5
N_EXPERTS = 32
TOP_K = 4
D_FF = D_MODEL
SWIGLU_ALPHA = 1.702
SWIGLU_LIMIT = 7.0
ADA_CHUNKS = 6
EPS = 1e-6
NEG = -1e30
F32 = jnp.float32
BF16 = jnp.bfloat16

T_P = BATCH * SEQ
T_S = DEC_BATCH * DEC_SEQ
T_ALL = T_P + T_S
ROW_TILE = 256
N_ROW_TILES = T_ALL // ROW_TILE
P_TILES = T_P // ROW_TILE
S_TILES_PER_BATCH = DEC_SEQ // ROW_TILE
COND_ROWS = 8
MOE_TILE = 256
N_ASSIGN = T_ALL * TOP_K
MOE_ROWS = N_ASSIGN + N_EXPERTS * MOE_TILE
MOE_TILES = MOE_ROWS // MOE_TILE
X_ALIGN = 16
X_ROWS = N_ASSIGN + N_EXPERTS * X_ALIGN + MOE_TILE
SPARE_ROW = T_ALL
ACC_ROWS = T_ALL + 8
SCATTER_GROUP = 8
W1_DMA_CHUNKS = 8
W2_DMA_CHUNKS = 4
LANES = 128
ROW_CHUNKS = D_MODEL // LANES
VMEM_LIMIT = 56 * 1024 * 1024
HIGHEST = lax.Precision.HIGHEST


def _cparams(n_axes):
    return pltpu.CompilerParams(dimension_semantics=("arbitrary",) * n_axes,
                                vmem_limit_bytes=VMEM_LIMIT)


def _bdot(a, b):
    return jnp.dot(a.astype(BF16), b.astype(BF16), preferred_element_type=F32)


def _cond_row(i):
    return jnp.where(i < P_TILES, 0, 1 + (i - P_TILES) // S_TILES_PER_BATCH)


def _ada_chunk(ada_ref, row, j):
    return ada_ref[pl.ds(row, 1), j * D_MODEL:(j + 1) * D_MODEL]


def _modulate(x, g, shift, scale):
    ms = jnp.mean(x * x, axis=-1, keepdims=True)
    return (x * lax.rsqrt(ms + EPS) * g) * (1.0 + scale) + shift


def _sigmoid(x):
    return 1.0 / (1.0 + jnp.exp(-x))


def _silu(x):
    return x * _sigmoid(x)


def _log_sigmoid(x):
    return jnp.minimum(x, 0.0) - jnp.log(1.0 + jnp.exp(-jnp.abs(x)))


def _dwconv3(x, w, b):
    n = x.shape[0]
    row = lax.broadcasted_iota(jnp.int32, x.shape, 0)
    prev = jnp.where(row == 0, 0.0, pltpu.roll(x, 1, 0))
    nxt = jnp.where(row == n - 1, 0.0, pltpu.roll(x, n - 1, 0))
    return prev * w[0:1] + x * w[1:2] + nxt * w[2:3] + b


def _ada_kernel(cond_ref, w_ref, b_ref, o_ref):
    c = _silu(cond_ref[...])
    o_ref[...] = _bdot(c, w_ref[...]) + b_ref[...]


def _ada_table(cond, ada_w, ada_b):
    tn = 1536
    return pl.pallas_call(
        _ada_kernel,
        grid=(DEPTH, ADA_CHUNKS * D_MODEL // tn),
        in_specs=[
            pl.BlockSpec((COND_ROWS, D_MODEL), lambda l, j: (0, 0)),
            pl.BlockSpec((None, D_MODEL, tn), lambda l, j: (l, 0, j)),
            pl.BlockSpec((None, 1, tn), lambda l, j: (l, 0, j)),
        ],
        out_specs=pl.BlockSpec((None, COND_ROWS, tn), lambda l, j: (l, 0, j)),
        out_shape=jax.ShapeDtypeStruct((DEPTH, COND_ROWS, ADA_CHUNKS * D_MODEL), F32),
        compiler_params=_cparams(2),
    )(cond, ada_w, ada_b.reshape(DEPTH, 1, ADA_CHUNKS * D_MODEL))


def _stream_specs(y):
    if isinstance(y, tuple):
        return [pl.BlockSpec((ROW_TILE, D_MODEL), lambda i: (jnp.minimum(i, P_TILES - 1), 0)),
                pl.BlockSpec((ROW_TILE, D_MODEL), lambda i: (jnp.maximum(i - P_TILES, 0), 0))], list(y)
    return [pl.BlockSpec((ROW_TILE, D_MODEL), lambda i: (i, 0))], [y]


def _stream_tile(y_refs, i):
    if len(y_refs) == 2:
        return jnp.where(i < P_TILES, y_refs[0][...], y_refs[1][...])
    return y_refs[0][...]


def _proj_kernel(splits, n_y, *refs):
    y_refs = refs[:n_y]
    ada_ref, g_ref, w_ref = refs[n_y:n_y + 3]
    out_refs, wbf_ref = refs[n_y + 3:-1], refs[-1]
    i = pl.program_id(0)

    @pl.when(i == 0)
    def _():
        wbf_ref[...] = w_ref[...].astype(BF16)

    row = _cond_row(i)
    h = _modulate(_stream_tile(y_refs, i), g_ref[...], _ada_chunk(ada_ref, row, 0), _ada_chunk(ada_ref, row, 1))
    h = h.astype(BF16)
    lo = 0
    for o_ref, width in zip(out_refs, splits):
        o_ref[...] = jnp.dot(h, wbf_ref[:, lo:lo + width], preferred_element_type=F32)
        lo += width


def _modulated_proj(y, ada_l, g, w, splits):
    n = w.shape[1]
    y_specs, y_args = _stream_specs(y)
    return pl.pallas_call(
        functools.partial(_proj_kernel, splits, len(y_args)),
        grid=(N_ROW_TILES,),
        in_specs=y_specs + [
            pl.BlockSpec((COND_ROWS, ADA_CHUNKS * D_MODEL), lambda i: (0, 0)),
            pl.BlockSpec((1, D_MODEL), lambda i: (0, 0)),
            pl.BlockSpec((D_MODEL, n), lambda i: (0, 0), pipeline_mode=pl.Buffered(1)),
        ],
        out_specs=[pl.BlockSpec((ROW_TILE, s), lambda i: (i, 0)) for s in splits],
        out_shape=[jax.ShapeDtypeStruct((T_ALL, s), F32) for s in splits],
        scratch_shapes=[pltpu.VMEM((D_MODEL, n), BF16)],
        compiler_params=_cparams(1),
    )(*y_args, ada_l, g.reshape(1, D_MODEL), w)


def _out_proj_kernel(n_in, n_y, gate_chunk, *refs):
    x_refs = refs[:2 * n_in]
    y_refs = refs[2 * n_in:2 * n_in + n_y]
    ada_ref, w_ref, o_ref, wbf_ref = refs[2 * n_in + n_y:]
    i = pl.program_id(0)

    @pl.when(i == 0)
    def _():
        wbf_ref[...] = w_ref[...].astype(BF16)

    acc = None
    lo = 0
    for xp_ref, xs_ref in zip(x_refs[0::2], x_refs[1::2]):
        k = xp_ref.shape[1]
        x = jnp.where(i < P_TILES, xp_ref[...], xs_ref[...])
        part = jnp.dot(x.astype(BF16), wbf_ref[lo:lo + k, :], preferred_element_type=F32)
        acc = part if acc is None else acc + part
        lo += k
    gate = _ada_chunk(ada_ref, _cond_row(i), gate_chunk)
    o_ref[...] = _stream_tile(y_refs, i) + gate * acc


def _out_proj_residual(xs, y, ada_l, w, gate_chunk):
    y_specs, y_args = _stream_specs(y)
    x_specs = []
    for xp, _ in xs:
        x_specs.append(pl.BlockSpec((ROW_TILE, xp.shape[1]), lambda i: (jnp.minimum(i, P_TILES - 1), 0)))
        x_specs.append(pl.BlockSpec((ROW_TILE, xp.shape[1]), lambda i: (jnp.maximum(i - P_TILES, 0), 0)))
    return pl.pallas_call(
        functools.partial(_out_proj_kernel, len(xs), len(y_args), gate_chunk),
        grid=(N_ROW_TILES,),
        in_specs=x_specs + y_specs + [
            pl.BlockSpec((COND_ROWS, ADA_CHUNKS * D_MODEL), lambda i: (0, 0)),
            pl.BlockSpec((D_MODEL, D_MODEL), lambda i: (0, 0), pipeline_mode=pl.Buffered(1)),
        ],
        out_specs=pl.BlockSpec((ROW_TILE, D_MODEL), lambda i: (i, 0)),
        out_shape=jax.ShapeDtypeStruct((T_ALL, D_MODEL), F32),
        scratch_shapes=[pltpu.VMEM((D_MODEL, D_MODEL), BF16)],
        compiler_params=_cparams(1),
    )(*[a for pair in xs for a in pair], *y_args, ada_l, w)


def _subhead_norm(x, g2):
    lane = lax.broadcasted_iota(jnp.int32, x.shape, 1)
    first = lane < HD_A
    xx = x * x
    s0 = jnp.sum(jnp.where(first, xx, 0.0), axis=-1, keepdims=True)
    s1 = jnp.sum(jnp.where(first, 0.0, xx), axis=-1, keepdims=True)
    r = jnp.where(first, lax.rsqrt(s0 / HD_A + EPS), lax.rsqrt(s1 / HD_A + EPS))
    return x * r * g2


def _rope(x, cos, sin):
    quarter = HD_A // 4
    lane = lax.broadcasted_iota(jnp.int32, x.shape, 1)
    lower = (lane % (2 * quarter)) < quarter
    swapped = jnp.where(lower, pltpu.roll(x, 2 * HD_A - quarter, 1), pltpu.roll(x, quarter, 1))
    return x * cos + swapped * sin


def _attn_kernel(lam_init, has_ctx, *refs):
    if has_ctx:
        (q_ref, k_ref, v_ref, ck_ref, cv_ref, cq_ref, sq_ref, ckk_ref, skk_ref,
         qg_ref, kg_ref, lp_ref, sg_ref, o_ref, kall_ref, vall_ref) = refs
    else:
        q_ref, k_ref, v_ref, qg_ref, kg_ref, lp_ref, sg_ref, o_ref, kn_ref = refs
    lp = lp_ref[...]
    lam = (jnp.exp(jnp.sum(lp[0:1] * lp[1:2], axis=-1, keepdims=True))
           - jnp.exp(jnp.sum(lp[2:3] * lp[3:4], axis=-1, keepdims=True)) + lam_init)

    def attend(q, k, v):
        probs = []
        for c in range(2):
            qc = q[:, c * HD_A:(c + 1) * HD_A].astype(BF16)
            kc = k[:, c * HD_A:(c + 1) * HD_A].astype(BF16)
            s = lax.dot_general(qc, kc, (((1,), (1,)), ((), ())), preferred_element_type=F32) * (HD_A ** -0.5)
            e = jnp.exp(s - jnp.max(s, axis=-1, keepdims=True))
            probs.append(e / jnp.sum(e, axis=-1, keepdims=True))
        o = _bdot(probs[0] - lam * probs[1], v)
        ms = jnp.mean(o * o, axis=-1, keepdims=True)
        return (o * lax.rsqrt(ms + EPS) * sg_ref[...]) * (1.0 - lam_init)

    if not has_ctx:
        for h in range(H_A):
            cols = slice(h * 2 * HD_A, (h + 1) * 2 * HD_A)
            k = _subhead_norm(k_ref[:, cols], kg_ref[...])
            kn_ref[:, cols] = k
            o_ref[:, cols] = attend(_subhead_norm(q_ref[:, cols], qg_ref[...]), k, v_ref[:, cols])
        return

    @pl.when(pl.program_id(2) == 0)
    def _():
        kall_ref[0:PAST_LEN, :] = ck_ref[...].astype(BF16)
        vall_ref[0:PAST_LEN, :] = cv_ref[...].astype(BF16)
        k_new = _rope(_subhead_norm(k_ref[...], kg_ref[...]), ckk_ref[...], skk_ref[...])
        kall_ref[PAST_LEN:, :] = k_new.astype(BF16)
        vall_ref[PAST_LEN:, :] = v_ref[...].astype(BF16)

    q = _rope(_subhead_norm(q_ref[...], qg_ref[...]), cq_ref[...], sq_ref[...])
    o_ref[...] = attend(q, kall_ref[...], vall_ref[...])


def _attention_prompt(qkv, qg2, kg2, lam_p, sub_g, lam_init):
    head = 2 * HD_A
    small = [
        pl.BlockSpec((1, head), lambda b: (0, 0)),
        pl.BlockSpec((1, head), lambda b: (0, 0)),
        pl.BlockSpec((4, HD_A), lambda b: (0, 0)),
        pl.BlockSpec((1, head), lambda b: (0, 0)),
    ]
    return pl.pallas_call(
        functools.partial(_attn_kernel, lam_init, False),
        grid=(BATCH,),
        in_specs=[
            pl.BlockSpec((SEQ, W_A), lambda b: (b, 0)),
            pl.BlockSpec((SEQ, W_A), lambda b: (b, 1)),
            pl.BlockSpec((SEQ, W_A), lambda b: (b, 2)),
        ] + small,
        out_specs=[pl.BlockSpec((SEQ, W_A), lambda b: (b, 0)),
                   pl.BlockSpec((SEQ, W_A), lambda b: (b, 0))],
        out_shape=[jax.ShapeDtypeStruct((T_P, W_A), F32), jax.ShapeDtypeStruct((T_P, W_A), F32)],
        compiler_params=_cparams(1),
    )(qkv, qkv, qkv, qg2, kg2, lam_p, sub_g)


def _attention_sample(qkv, cache_k, cache_v, cos, sin, qg2, kg2, lam_p, sub_g, lam_init):
    nh = H_A
    head = 2 * HD_A
    tq = ROW_TILE
    nq = DEC_SEQ // tq
    q_off = T_P // tq
    k_off = T_P // DEC_SEQ
    small = [
        pl.BlockSpec((1, head), lambda b, h, i: (0, 0)),
        pl.BlockSpec((1, head), lambda b, h, i: (0, 0)),
        pl.BlockSpec((4, HD_A), lambda b, h, i: (0, 0)),
        pl.BlockSpec((1, head), lambda b, h, i: (0, 0)),
    ]
    return pl.pallas_call(
        functools.partial(_attn_kernel, lam_init, True),
        grid=(DEC_BATCH, nh, nq),
        in_specs=[
            pl.BlockSpec((tq, head), lambda b, h, i: (q_off + b * nq + i, h)),
            pl.BlockSpec((DEC_SEQ, head), lambda b, h, i: (k_off + b, nh + h)),
            pl.BlockSpec((DEC_SEQ, head), lambda b, h, i: (k_off + b, 2 * nh + h)),
            pl.BlockSpec((None, PAST_LEN, head), lambda b, h, i: (b, 0, h)),
            pl.BlockSpec((None, PAST_LEN, head), lambda b, h, i: (b, 0, h)),
            pl.BlockSpec((tq, head), lambda b, h, i: (i, 0)),
            pl.BlockSpec((tq, head), lambda b, h, i: (i, 0)),
            pl.BlockSpec((DEC_SEQ, head), lambda b, h, i: (0, 0)),
            pl.BlockSpec((DEC_SEQ, head), lambda b, h, i: (0, 0)),
        ] + small,
        out_specs=pl.BlockSpec((tq, head), lambda b, h, i: (b * nq + i, h)),
        out_shape=jax.ShapeDtypeStruct((T_S, W_A), F32),
        scratch_shapes=[pltpu.VMEM((PAST_LEN + DEC_SEQ, head), BF16)] * 2,
        compiler_params=_cparams(3),
    )(qkv, qkv, qkv, cache_k, cache_v, cos, sin, cos, sin, qg2, kg2, lam_p, sub_g)


def _rope_tables():
    half = HD_A // 2
    nf = half // 2
    inv = ROPE_BASE ** (-np.arange(nf, dtype=np.float32) / nf)
    pos = np.arange(DEC_SEQ)
    row = (pos // GRID_W).astype(np.float32)
    col = (pos % GRID_W).astype(np.float32)
    ang_r = (row[:, None] * inv).astype(np.float32)
    ang_c = (col[:, None] * inv).astype(np.float32)
    ang = np.concatenate([ang_r, ang_r, ang_c, ang_c], axis=1)
    sign = np.concatenate([-np.ones(nf), np.ones(nf), -np.ones(nf), np.ones(nf)]).astype(np.float32)
    cos = np.cos(ang.astype(np.float64)).astype(np.float32)
    sin = (np.sin(ang.astype(np.float64)) * sign).astype(np.float32)
    return jnp.asarray(np.tile(cos, (1, 2))), jnp.asarray(np.tile(sin, (1, 2)))


def _mlstm_kernel(seq, has_ctx, *refs):
    if has_ctx:
        (q_ref, k_ref, cwq_ref, cwk_ref, cbq_ref, cbk_ref, v_ref, mo_ref, gi_ref, gf_ref,
         gbi_ref, gbf_ref, hn_ref, c0_ref, n0_ref, m0_ref, o_ref,
         qs_ref, ks_ref, hf_ref, hb_ref, cs_ref, rrow_ref, col_ref, wc_ref) = refs
    else:
        (q_ref, k_ref, cwq_ref, cwk_ref, cbq_ref, cbk_ref, v_ref, mo_ref, gi_ref, gf_ref,
         gbi_ref, gbf_ref, hn_ref, o_ref, c_out_ref, n_out_ref, m_out_ref,
         qs_ref, ks_ref, hf_ref, hb_ref, cs_ref, rrow_ref, col_ref, wc_ref) = refs
    nc = seq // CHUNK
    n_chain = 2 * H_B
    chains = [(d, h) for d in range(2) for h in range(H_B)]
    qs_ref[...] = _silu(_dwconv3(q_ref[...], cwq_ref[...], cbq_ref[...])) * (HD_B ** -0.5)
    ks_ref[...] = _silu(_dwconv3(k_ref[...], cwk_ref[...], cbk_ref[...]))

    rows = nc * n_chain
    lane = lax.broadcasted_iota(jnp.int32, (rows, 2 * CHUNK), 1)
    forward = lax.broadcasted_iota(jnp.int32, (rows, 2 * CHUNK), 0) % n_chain < H_B
    valid = lane < CHUNK

    def scan(x, op, fill):
        pre, suf = x, x
        sh = 1
        while sh < CHUNK:
            pre = op(pre, jnp.where(lane >= sh, pltpu.roll(pre, sh, 1), fill))
            suf = op(suf, jnp.where(lane + sh < CHUNK, pltpu.roll(suf, 2 * CHUNK - sh, 1), fill))
            sh *= 2
        return jnp.where(forward, pre, suf)

    gate_i = (gi_ref[...] + gbi_ref[...]).reshape(rows, 2 * CHUNK)
    lf = jnp.where(valid, _log_sigmoid(gf_ref[...] + gbf_ref[...]).reshape(rows, 2 * CHUNK), 0.0)
    b = scan(lf, jnp.add, 0.0)
    cmax = scan(jnp.where(valid, gate_i - b, -jnp.inf), jnp.maximum, -jnp.inf)
    b_last = jnp.sum(lf, axis=1, keepdims=True)
    g = b_last - b + gate_i
    g_max = jnp.max(jnp.where(valid, g, -jnp.inf), axis=1, keepdims=True)
    mm = m0_ref[...] if has_ctx else jnp.zeros((n_chain, 1), F32)
    mm_seq = []
    for p in range(nc):
        mm_seq.append(mm)
        seg = slice(p * n_chain, (p + 1) * n_chain)
        mm = jnp.maximum(b_last[seg] + mm, g_max[seg])
    mm_final = mm
    mm_prev = jnp.concatenate(mm_seq, axis=0)
    mm_next = jnp.concatenate(mm_seq[1:] + [mm_final], axis=0)
    m_t = jnp.maximum(b + mm_prev, b + cmax)
    rrow_ref[...] = (b - gate_i).reshape(nc, n_chain, 2 * CHUNK)
    wc_ref[...] = jnp.exp(b_last + mm_prev - mm_next).reshape(nc, n_chain, 1)
    per_row = [b, m_t, jnp.exp(b + mm_prev - m_t), jnp.exp(-m_t), jnp.exp(g - mm_next)]
    for j, arr in enumerate(per_row):
        by_time = arr.T
        for p in range(nc):
            col_ref[p, :, j * n_chain:(j + 1) * n_chain] = by_time[0:CHUNK, p * n_chain:(p + 1) * n_chain]

    t_idx = lax.broadcasted_iota(jnp.int32, (CHUNK, CHUNK), 0)
    s_idx = lax.broadcasted_iota(jnp.int32, (CHUNK, CHUNK), 1)
    for n, (d, h) in enumerate(chains):
        cs_ref[n] = c0_ref[d, h] if has_ctx else jnp.zeros((HD_B, HD_B), F32)

    def out_step(p, n_states):
        cols = col_ref[p]
        rrows = rrow_ref[p]
        wcs = wc_ref[p]
        new_states = []
        for n, (d, h) in enumerate(chains):
            c = p if d == 0 else nc - 1 - p
            r0 = pl.multiple_of(c * CHUNK, CHUNK)
            hcols = slice(h * HD_B, (h + 1) * HD_B)
            qt = qs_ref[pl.ds(r0, CHUNK), hcols]
            kt = ks_ref[pl.ds(r0, CHUNK), hcols]
            vt = v_ref[pl.ds(r0, CHUNK), hcols]
            b_col, m_t, w_inter, e_inv, w_k = (cols[:, j * n_chain + n:j * n_chain + n + 1] for j in range(5))
            mask = (s_idx <= t_idx) if d == 0 else (s_idx >= t_idx)
            decay = jnp.exp(jnp.where(mask, b_col - rrows[n:n + 1, 0:CHUNK], NEG) - m_t)
            qk = lax.dot_general(qt.astype(BF16), kt.astype(BF16), (((1,), (1,)), ((), ())),
                                 preferred_element_type=F32)
            s = qk * decay
            cm = cs_ref[n]
            nm = n_states[n]
            cq = lax.dot_general(qt.astype(BF16), cm.astype(BF16), (((1,), (1,)), ((), ())),
                                 preferred_element_type=F32)
            num = _bdot(s, vt) + w_inter * cq
            nq = jnp.sum(s, axis=-1, keepdims=True) + w_inter * jnp.sum(qt * nm, axis=-1, keepdims=True)
            hdir_ref = hf_ref if d == 0 else hb_ref
            hdir_ref[pl.ds(r0, CHUNK), hcols] = num / jnp.maximum(jnp.abs(nq), e_inv)
            w_c = wcs[n:n + 1, :]
            vw = (vt * w_k).astype(BF16)
            cs_ref[n] = w_c * cm + lax.dot_general(vw, kt.astype(BF16), (((0,), (0,)), ((), ())),
                                                   preferred_element_type=F32)
            new_states.append(w_c * nm + jnp.sum(kt * w_k, axis=0, keepdims=True))
        return tuple(new_states)

    if has_ctx:
        n_init = tuple(n0_ref[d, h] for d, h in chains)
    else:
        n_init = tuple(jnp.zeros((1, HD_B), F32) for _ in chains)
    n_final = lax.fori_loop(0, nc, out_step, n_init)
    if not has_ctx:
        for n, (d, h) in enumerate(chains):
            c_out_ref[d, h] = cs_ref[n]
            n_out_ref[d, h] = n_final[n]
            m_out_ref[d, h] = jnp.broadcast_to(mm_final[n:n + 1, :], (1, HD_B))

    for h in range(H_B):
        hcols = slice(h * HD_B, (h + 1) * HD_B)
        hh = hf_ref[:, hcols] + hb_ref[:, hcols]
        ms = jnp.mean(hh * hh, axis=-1, keepdims=True)
        o_ref[:, hcols] = (hh * lax.rsqrt(ms + EPS) * hn_ref[:, hcols]) * _sigmoid(mo_ref[:, hcols])


def _mlstm(mqk, mv, mo, mg_stream, conv_w, conv_b, gate_b, hn_g, *, seq, nbatch, row_off, ctx=None):
    nh = H_B
    nc = seq // CHUNK
    has_ctx = ctx is not None
    gt = mg_stream.reshape(nbatch, nc, CHUNK, 2, 2, nh).transpose(0, 1, 3, 4, 5, 2)
    pad = ((0, 0), (0, 0), (0, 0), (0, CHUNK))
    gates = [jnp.pad(jnp.concatenate([gt[:, :, 0, j], gt[:, ::-1, 1, j]], axis=2), pad) for j in range(2)]
    gate_bias = [jnp.concatenate([gate_b[0, j], gate_b[1, j]]).reshape(2 * nh, 1) for j in range(2)]
    blk = lambda col: pl.BlockSpec((seq, W_B), lambda b, col=col: (row_off + b, col))
    gate_blk = pl.BlockSpec((None, nc, 2 * nh, 2 * CHUNK), lambda b: (b, 0, 0, 0))
    in_specs = [
        blk(0), blk(1),
        pl.BlockSpec((3, W_B), lambda b: (0, 0)),
        pl.BlockSpec((3, W_B), lambda b: (0, 1)),
        pl.BlockSpec((1, W_B), lambda b: (0, 0)),
        pl.BlockSpec((1, W_B), lambda b: (0, 1)),
        blk(0), blk(0),
        gate_blk, gate_blk,
        pl.BlockSpec((2 * nh, 1), lambda b: (0, 0)),
        pl.BlockSpec((2 * nh, 1), lambda b: (0, 0)),
        pl.BlockSpec((1, W_B), lambda b: (0, 0)),
    ]
    args = [mqk, mqk, conv_w, conv_w, conv_b, conv_b, mv, mo, gates[0], gates[1],
            gate_bias[0], gate_bias[1], hn_g.reshape(1, W_B)]
    o_spec = pl.BlockSpec((seq, W_B), lambda b: (b, 0))
    o_shape = jax.ShapeDtypeStruct((nbatch * seq, W_B), F32)
    state_blk = lambda rows: pl.BlockSpec((None, 2, nh, rows, HD_B), lambda b: (b, 0, 0, 0, 0))
    if has_ctx:
        c0, n0, m0 = ctx
        in_specs += [state_blk(HD_B), state_blk(1), pl.BlockSpec((None, 2 * nh, 1), lambda b: (b, 0, 0))]
        args += [c0, n0.reshape(nbatch, 2, nh, 1, HD_B), m0.reshape(nbatch, 2 * nh, 1)]
        out_specs, out_shape = o_spec, o_shape
    else:
        out_specs = [o_spec, state_blk(HD_B), state_blk(1), state_blk(1)]
        out_shape = [
            o_shape,
            jax.ShapeDtypeStruct((nbatch, 2, nh, HD_B, HD_B), F32),
            jax.ShapeDtypeStruct((nbatch, 2, nh, 1, HD_B), F32),
            jax.ShapeDtypeStruct((nbatch, 2, nh, 1, HD_B), F32),
        ]
    return pl.pallas_call(
        functools.partial(_mlstm_kernel, seq, has_ctx),
        grid=(nbatch,),
        in_specs=in_specs,
        out_specs=out_specs,
        out_shape=out_shape,
        scratch_shapes=[pltpu.VMEM((seq, W_B), F32)] * 4 + [
            pltpu.VMEM((2 * nh, HD_B, HD_B), F32),
            pltpu.VMEM((nc, 2 * nh, 2 * CHUNK), F32),
            pltpu.VMEM((nc, CHUNK, 5 * 2 * nh), F32),
            pltpu.VMEM((nc, 2 * nh, 1), F32),
        ],
        compiler_params=_cparams(1),
    )(*args)


def _dft_mats(L):
    f = np.arange(L)[:, None]
    j = np.arange(L)[None, :]
    ang = 2.0 * np.pi * ((f * j) % (2 * L)) / (2 * L)
    cm = np.cos(ang)
    sm = np.sin(ang)
    alt = (1.0 - 2.0 * (np.arange(L) % 2))
    fwd_b = -sm
    fwd_b[0, :] = alt
    fwd = np.concatenate([cm, fwd_b], axis=0)
    wgt = np.where(np.arange(L) == 0, 1.0, 2.0)[None, :]
    inv_a = cm.T * wgt
    inv_b = -2.0 * sm.T
    inv_b[:, 0] = alt
    inv = np.concatenate([inv_a, inv_b], axis=1) / (2 * L)
    return jnp.asarray(fwd.astype(np.float32)), jnp.asarray(inv.astype(np.float32))


def _hyena_feats(L):
    t = np.linspace(0.0, 1.0, L, dtype=np.float32)
    wpos = (2.0 * math.pi * np.arange(L, dtype=np.float32) / L).astype(np.float32)
    fb = np.linspace(1e-4, HY_BANDS - 1, HY_BANDS, dtype=np.float32)
    z = (wpos[:, None] * fb).astype(np.float32)
    feats = np.concatenate([t[:, None], np.cos(z), -np.sin(z)], axis=-1).astype(np.float32)
    deltas = np.abs(np.linspace(math.log(HY_TARGET) / HY_SLOW_PCT, math.log(HY_TARGET) / HY_FAST_PCT,
                                D_MODEL, dtype=np.float32))
    decay = np.exp(-t[:, None] * deltas).astype(np.float32)
    return jnp.asarray(feats), jnp.asarray(decay)


def _filter_kernel(L, feats_ref, w1_ref, b1_ref, fr1_ref, w2_ref, b2_ref, fr2_ref, w3f_ref, w3b_ref,
                   decay_ref, fwd_ref, o_ref, hdn_ref):
    @pl.when((pl.program_id(0) == 0) & (pl.program_id(1) == 0))
    def _():
        h1 = jnp.sin(fr1_ref[...] * (jnp.dot(feats_ref[...], w1_ref[...], precision=HIGHEST,
                                             preferred_element_type=F32) + b1_ref[...]))
        hdn_ref[...] = jnp.sin(fr2_ref[...] * (jnp.dot(h1, w2_ref[...], precision=HIGHEST,
                                                       preferred_element_type=F32) + b2_ref[...]))

    hdn = hdn_ref[...]
    decay = decay_ref[...]
    f_fwd = jnp.dot(hdn, w3f_ref[...], precision=HIGHEST, preferred_element_type=F32) * decay
    f_bwd = jnp.dot(hdn, w3b_ref[...], precision=HIGHEST, preferred_element_type=F32) * decay
    row = lax.broadcasted_iota(jnp.int32, f_bwd.shape, 0)
    f_bwd = jnp.where(row == 0, 0.0, f_bwd)
    fwd = fwd_ref[...]
    p = _bdot(fwd, f_fwd)
    q = _bdot(fwd, f_bwd)
    first = row == 0
    o_ref[0:L, :] = p[0:L] + q[0:L]
    o_ref[L:2 * L, :] = p[L:2 * L] + jnp.where(first, q[L:2 * L], -q[L:2 * L])


def _hyena_filter_spectrum(L, fwd_bf, w1, b1, fr1, w2, b2, fr2, w3):
    feats, decay = _hyena_feats(L)
    td = 512
    nd = D_MODEL // td
    emb = feats.shape[1]
    vec = lambda a: a.reshape(1, HY_FH)
    full = lambda shape: pl.BlockSpec(shape, lambda o, j: (0, 0))
    return pl.pallas_call(
        functools.partial(_filter_kernel, L),
        grid=(HY_ORDER, nd),
        in_specs=[
            full((L, emb)), full((emb, HY_FH)), full((1, HY_FH)), full((1, HY_FH)),
            full((HY_FH, HY_FH)), full((1, HY_FH)), full((1, HY_FH)),
            pl.BlockSpec((HY_FH, td), lambda o, j: (0, o * 2 * nd + j)),
            pl.BlockSpec((HY_FH, td), lambda o, j: (0, o * 2 * nd + nd + j)),
            pl.BlockSpec((L, td), lambda o, j: (0, j)),
            full((2 * L, L)),
        ],
        out_specs=pl.BlockSpec((2 * L, td), lambda o, j: (0, o * nd + j)),
        out_shape=jax.ShapeDtypeStruct((2 * L, HY_ORDER * D_MODEL), F32),
        scratch_shapes=[pltpu.VMEM((L, HY_FH), F32)],
        compiler_params=_cparams(2),
    )(feats, w1, vec(b1), vec(fr1), w2, vec(b2), vec(fr2), w3, w3, decay, fwd_bf)


def _spectral_conv(u, fwd, inv, kspec, L):
    uf = jnp.dot(fwd, u.astype(BF16), preferred_element_type=F32)
    ua, ub = uf[0:L], uf[L:2 * L]
    ka, kb = kspec[0:L], kspec[L:2 * L]
    first = lax.broadcasted_iota(jnp.int32, ua.shape, 0) == 0
    ya = ua * ka - jnp.where(first, 0.0, ub * kb)
    yb = jnp.where(first, ub * kb, ua * kb + ub * ka)
    y = jnp.concatenate([ya, yb], axis=0).astype(BF16)
    return jnp.dot(inv, y, preferred_element_type=F32)


def _hyena_kernel(L, zv_ref, z1_ref, z2_ref, cwv_ref, cw1_ref, cw2_ref, cbv_ref, cb1_ref, cb2_ref,
                  fwd_ref, inv_ref, k0_ref, k1_ref, bias0_ref, bias1_ref, o_ref):
    fwd = fwd_ref[...]
    inv = inv_ref[...]
    v = _dwconv3(zv_ref[...], cwv_ref[...], cbv_ref[...])
    x1 = _dwconv3(z1_ref[...], cw1_ref[...], cb1_ref[...])
    x2 = _dwconv3(z2_ref[...], cw2_ref[...], cb2_ref[...])
    z = x1 * (_spectral_conv(v, fwd, inv, k0_ref[...], L) + v * bias0_ref[...])
    o_ref[...] = x2 * (_spectral_conv(z, fwd, inv, k1_ref[...], L) + z * bias1_ref[...])


def _hyena_core(zproj, conv_w, conv_b, fwd_bf, inv_bf, kspec, bias, *, seq, nbatch, row_off, td):
    nd = D_MODEL // td
    zblk = lambda part: pl.BlockSpec((seq, td), lambda b, j, part=part: (row_off + b, part * nd + j))
    cwblk = lambda part: pl.BlockSpec((3, td), lambda b, j, part=part: (0, part * nd + j))
    cbblk = lambda part: pl.BlockSpec((1, td), lambda b, j, part=part: (0, part * nd + j))
    return pl.pallas_call(
        functools.partial(_hyena_kernel, seq),
        grid=(nbatch, nd),
        in_specs=[
            zblk(0), zblk(1), zblk(2), cwblk(0), cwblk(1), cwblk(2), cbblk(0), cbblk(1), cbblk(2),
            pl.BlockSpec((2 * seq, seq), lambda b, j: (0, 0), pipeline_mode=pl.Buffered(1)),
            pl.BlockSpec((seq, 2 * seq), lambda b, j: (0, 0), pipeline_mode=pl.Buffered(1)),
            pl.BlockSpec((2 * seq, td), lambda b, j: (0, j)),
            pl.BlockSpec((2 * seq, td), lambda b, j: (0, nd + j)),
            pl.BlockSpec((None, 1, td), lambda b, j: (0, 0, j)),
            pl.BlockSpec((None, 1, td), lambda b, j: (1, 0, j)),
        ],
        out_specs=pl.BlockSpec((seq, td), lambda b, j: (b, j)),
        out_shape=jax.ShapeDtypeStruct((nbatch * seq, D_MODEL), F32),
        compiler_params=_cparams(2),
    )(zproj, zproj, zproj, conv_w, conv_w, conv_w, conv_b, conv_b, conv_b,
      fwd_bf, inv_bf, kspec, kspec, bias.reshape(HY_ORDER, 1, D_MODEL), bias.reshape(HY_ORDER, 1, D_MODEL))


def _router_kernel(x_ref, ada_ref, g_ref, rw_ref, rb_ref, h_ref, idx_ref, wt_ref):
    i = pl.program_id(0)
    row = _cond_row(i)
    h = _modulate(x_ref[...], g_ref[...], _ada_chunk(ada_ref, row, 3), _ada_chunk(ada_ref, row, 4))
    h_ref[...] = h.astype(BF16)
    logits = lax.dot_general(rw_ref[...], h, (((1,), (1,)), ((), ())), precision=HIGHEST,
                             preferred_element_type=F32) + rb_ref[...]
    expert = lax.broadcasted_iota(jnp.int32, logits.shape, 0)
    slot = lax.broadcasted_iota(jnp.int32, (TOP_K, logits.shape[1]), 0)
    vals = jnp.zeros((TOP_K, logits.shape[1]), F32)
    idxs = jnp.zeros((TOP_K, logits.shape[1]), jnp.int32)
    cur = logits
    for k in range(TOP_K):
        m = jnp.max(cur, axis=0, keepdims=True)
        a = jnp.min(jnp.where(cur == m, expert, N_EXPERTS), axis=0, keepdims=True)
        vals = jnp.where(slot == k, m, vals)
        idxs = jnp.where(slot == k, a, idxs)
        cur = jnp.where(expert == a, -jnp.inf, cur)
    e = jnp.exp(vals - vals[0:1])
    wt_ref[...] = e / jnp.sum(e, axis=0, keepdims=True)
    idx_ref[...] = idxs


def _router(y, ada_l, g, router_w, router_b):
    return pl.pallas_call(
        _router_kernel,
        grid=(N_ROW_TILES,),
        in_specs=[
            pl.BlockSpec((ROW_TILE, D_MODEL), lambda i: (i, 0)),
            pl.BlockSpec((COND_ROWS, ADA_CHUNKS * D_MODEL), lambda i: (0, 0)),
            pl.BlockSpec((1, D_MODEL), lambda i: (0, 0)),
            pl.BlockSpec((N_EXPERTS, D_MODEL), lambda i: (0, 0)),
            pl.BlockSpec((N_EXPERTS, 1), lambda i: (0, 0)),
        ],
        out_specs=[
            pl.BlockSpec((ROW_TILE, D_MODEL), lambda i: (i, 0)),
            pl.BlockSpec((TOP_K, ROW_TILE), lambda i: (0, i)),
            pl.BlockSpec((TOP_K, ROW_TILE), lambda i: (0, i)),
        ],
        out_shape=[
            jax.ShapeDtypeStruct((T_ALL, D_MODEL), BF16),
            jax.ShapeDtypeStruct((TOP_K, T_ALL), jnp.int32),
            jax.ShapeDtypeStruct((TOP_K, T_ALL), F32),
        ],
        compiler_params=_cparams(1),
    )(y, ada_l, g.reshape(1, D_MODEL), router_w.T, router_b.reshape(N_EXPERTS, 1))


def _deinterleave_matrix():
    s = np.zeros((256, 256), np.float32)
    j = np.arange(128)
    s[2 * j, j] = 1.0
    s[2 * j + 1, 128 + j] = 1.0
    return jnp.asarray(s)


def _weight_copies(layer, e, w1_hbm, w2_hbm, w1s_ref, w2s_ref, sem):
    copies = []
    r1 = D_MODEL // W1_DMA_CHUNKS
    for c in range(W1_DMA_CHUNKS):
        copies.append(pltpu.make_async_copy(w1_hbm.at[layer, e, pl.ds(c * r1, r1)],
                                            w1s_ref.at[pl.ds(c * r1, r1)], sem.at[c]))
    r2 = D_FF // W2_DMA_CHUNKS
    for c in range(W2_DMA_CHUNKS):
        copies.append(pltpu.make_async_copy(w2_hbm.at[layer, e, pl.ds(c * r2, r2)],
                                            w2s_ref.at[pl.ds(c * r2, r2)], sem.at[W1_DMA_CHUNKS + c]))
    return copies


def _expert_kernel(layer, te_ref, tf_ref, ne_ref, nu_ref, src_ref, xo_ref, x_ref, b1_ref, b2_ref, wt_ref, s_ref,
                   w1_hbm, w2_hbm, o_hbm, w1s_ref, w2s_ref, w1p_ref, w2p_ref, acc_ref, out_ref, wsem, osem):
    i = pl.program_id(0)
    half = 128
    copies = functools.partial(_weight_copies, layer, w1_hbm=w1_hbm, w2_hbm=w2_hbm,
                               w1s_ref=w1s_ref, w2s_ref=w2s_ref, sem=wsem)

    @pl.when(i == 0)
    def _():
        acc_ref[...] = jnp.zeros_like(acc_ref)
        out_ref[...] = jnp.zeros_like(out_ref)
        for cp in copies(te_ref[0]):
            cp.start()

    @pl.when(tf_ref[i] == 1)
    def _():
        for cp in copies(te_ref[i]):
            cp.wait()
        s = s_ref[...].astype(BF16)
        for c in range(2 * D_FF // 256):
            blk = jnp.dot(w1s_ref[:, c * 256:(c + 1) * 256].astype(BF16), s, preferred_element_type=F32)
            w1p_ref[:, c * half:(c + 1) * half] = blk[:, :half].astype(BF16)
            w1p_ref[:, D_FF + c * half:D_FF + (c + 1) * half] = blk[:, half:].astype(BF16)
        w2p_ref[...] = w2s_ref[...].astype(BF16)

        @pl.when(ne_ref[i] >= 0)
        def _():
            for cp in copies(ne_ref[i]):
                cp.start()

    @pl.when(i <= nu_ref[0])
    def _():
        base = i * MOE_TILE
        prev = (i + 1) % 2
        for r0 in range(0, MOE_TILE, SCATTER_GROUP):
            toks = [pl.multiple_of(src_ref[base + r0 + g] * ROW_CHUNKS, ROW_CHUNKS) for g in range(SCATTER_GROUP)]
            cur = [acc_ref[pl.ds(toks[g], ROW_CHUNKS), :] for g in range(SCATTER_GROUP)]
            add = [out_ref[prev, (r0 + g) * ROW_CHUNKS:(r0 + g + 1) * ROW_CHUNKS, :] for g in range(SCATTER_GROUP)]
            for g in range(SCATTER_GROUP):
                acc_ref[pl.ds(toks[g], ROW_CHUNKS), :] = cur[g] + add[g]
        a = jnp.dot(x_ref[...], w1p_ref[...], preferred_element_type=F32) + b1_ref[...]
        glu = jnp.minimum(a[:, :D_FF], SWIGLU_LIMIT)
        lin = jnp.clip(a[:, D_FF:], -SWIGLU_LIMIT, SWIGLU_LIMIT)
        hid = glu * _sigmoid(SWIGLU_ALPHA * glu) * (lin + 1.0)
        out = (jnp.dot(hid.astype(BF16), w2p_ref[...], preferred_element_type=F32) + b2_ref[...]) * wt_ref[...]
        cur_buf = i % 2
        for j in range(ROW_CHUNKS):
            out_ref[cur_buf, pl.ds(j, MOE_TILE, stride=ROW_CHUNKS), :] = out[:, j * LANES:(j + 1) * LANES]

    @pl.when(i == pl.num_programs(0) - 1)
    def _():
        cp = pltpu.make_async_copy(acc_ref.at[pl.ds(0, T_ALL * ROW_CHUNKS)], o_hbm, osem)
        cp.start()
        cp.wait()


def _experts(layer, x_sorted, w_sorted, plan, w1, b1p, w2, b2):
    tile_expert, tile_first, next_expert, n_used, src, x_off = plan
    grid_spec = pltpu.PrefetchScalarGridSpec(
        num_scalar_prefetch=6,
        grid=(MOE_TILES,),
        in_specs=[
            pl.BlockSpec((pl.Element(MOE_TILE), pl.Element(D_MODEL)),
                         lambda i, te, tf, ne, nu, src, xo: (pl.multiple_of(xo[i], X_ALIGN), 0)),
            pl.BlockSpec((None, None, 1, 2 * D_FF), lambda i, te, *_: (layer, te[i], 0, 0)),
            pl.BlockSpec((None, None, 1, D_MODEL), lambda i, te, *_: (layer, te[i], 0, 0)),
            pl.BlockSpec((MOE_TILE, 1), lambda i, te, *_: (i, 0)),
            pl.BlockSpec((256, 256), lambda i, te, *_: (0, 0)),
            pl.BlockSpec(memory_space=pl.ANY),
            pl.BlockSpec(memory_space=pl.ANY),
        ],
        out_specs=pl.BlockSpec(memory_space=pl.ANY),
        scratch_shapes=[
            pltpu.VMEM((D_MODEL, 2 * D_FF), F32),
            pltpu.VMEM((D_FF, D_MODEL), F32),
            pltpu.VMEM((D_MODEL, 2 * D_FF), BF16),
            pltpu.VMEM((D_FF, D_MODEL), BF16),
            pltpu.VMEM((ACC_ROWS * ROW_CHUNKS, LANES), F32),
            pltpu.VMEM((2, MOE_TILE * ROW_CHUNKS, LANES), F32),
            pltpu.SemaphoreType.DMA((W1_DMA_CHUNKS + W2_DMA_CHUNKS,)),
            pltpu.SemaphoreType.DMA(()),
        ],
    )
    return pl.pallas_call(
        functools.partial(_expert_kernel, layer),
        grid_spec=grid_spec,
        out_shape=jax.ShapeDtypeStruct((T_ALL * ROW_CHUNKS, LANES), F32),
        compiler_params=_cparams(1),
    )(tile_expert, tile_first, next_expert, n_used, src, x_off, x_sorted, b1p, b2, w_sorted,
      _deinterleave_matrix(), w1, w2)


def _combine_kernel(first_tile, y_ref, a_ref, ada_ref, o_ref):
    gate = _ada_chunk(ada_ref, _cond_row(first_tile + pl.program_id(0)), 5)
    acc = jnp.concatenate([a_ref[pl.ds(j, ROW_TILE, stride=ROW_CHUNKS), :] for j in range(ROW_CHUNKS)], axis=1)
    o_ref[...] = y_ref[...] + gate * acc


def _combine(y, acc, ada_l, first_tile=0, n_tiles=N_ROW_TILES):
    return pl.pallas_call(
        functools.partial(_combine_kernel, first_tile),
        grid=(n_tiles,),
        in_specs=[
            pl.BlockSpec((ROW_TILE, D_MODEL), lambda i: (first_tile + i, 0)),
            pl.BlockSpec((ROW_TILE * ROW_CHUNKS, LANES), lambda i: (first_tile + i, 0)),
            pl.BlockSpec((COND_ROWS, ADA_CHUNKS * D_MODEL), lambda i: (0, 0)),
        ],
        out_specs=pl.BlockSpec((ROW_TILE, D_MODEL), lambda i: (i, 0)),
        out_shape=jax.ShapeDtypeStruct((n_tiles * ROW_TILE, D_MODEL), F32),
        compiler_params=_cparams(1),
    )(y, acc, ada_l)


def _routing_plan(idx, wts):
    eid = idx.reshape(-1)
    order = jnp.argsort(eid, stable=True).astype(jnp.int32)
    experts = jnp.arange(N_EXPERTS, dtype=jnp.int32)
    counts = jnp.sum(eid[:, None] == experts[None, :], axis=0).astype(jnp.int32)
    ntiles = (counts + MOE_TILE - 1) // MOE_TILE
    tile_end = jnp.cumsum(ntiles).astype(jnp.int32)
    tile_begin = tile_end - ntiles
    cstarts = (jnp.cumsum(counts) - counts).astype(jnp.int32)
    n_used = tile_end[-1]
    tile = jnp.arange(MOE_TILES, dtype=jnp.int32)
    te = jnp.minimum(jnp.sum(tile[:, None] >= tile_end[None, :], axis=1), N_EXPERTS - 1).astype(jnp.int32)
    used = tile < n_used
    prev = jnp.concatenate([jnp.full((1,), -1, jnp.int32), te[:-1]])
    first = (te != prev) & used

    def pick(onehot, table):
        return jnp.sum(jnp.where(onehot, table[None, :], 0), axis=1).astype(jnp.int32)

    tile_is = te[:, None] == experts[None, :]
    later = (experts[None, :] > experts[:, None]) & (ntiles[None, :] > 0)
    following = jnp.min(jnp.where(later, experts[None, :], N_EXPERTS), axis=1)
    following = jnp.where(following < N_EXPERTS, following, -1)
    next_expert = jnp.where(first, pick(tile_is, following), -1).astype(jnp.int32)
    tile_in_expert = tile - pick(tile_is, tile_begin)
    off = tile_in_expert[:, None] * MOE_TILE + jnp.arange(MOE_TILE, dtype=jnp.int32)[None, :]
    valid = (off < pick(tile_is, counts)[:, None]) & used[:, None]
    assign = order[jnp.clip(pick(tile_is, cstarts)[:, None] + off, 0, N_ASSIGN - 1)]
    token = assign // TOP_K
    src = jnp.where(valid, token, SPARE_ROW).reshape(MOE_ROWS).astype(jnp.int32)
    src = jnp.concatenate([jnp.full((MOE_TILE,), SPARE_ROW, jnp.int32), src])
    w_sorted = jnp.where(valid, wts.reshape(-1)[assign], 0.0).reshape(MOE_ROWS, 1)
    seg = ((counts + X_ALIGN - 1) // X_ALIGN) * X_ALIGN
    seg_end = jnp.cumsum(seg).astype(jnp.int32)
    seg_begin = seg_end - seg
    x_off = jnp.where(used, pick(tile_is, seg_begin) + tile_in_expert * MOE_TILE, 0).astype(jnp.int32)
    group = jnp.arange(X_ROWS // X_ALIGN, dtype=jnp.int32) * X_ALIGN
    group_is = (group[:, None] >= seg_begin[None, :]) & (group[:, None] < seg_end[None, :])
    xoffset = (group - pick(group_is, seg_begin))[:, None] + jnp.arange(X_ALIGN, dtype=jnp.int32)[None, :]
    xassign = order[jnp.clip(pick(group_is, cstarts)[:, None] + xoffset, 0, N_ASSIGN - 1)]
    gather_row = jnp.where(xoffset < pick(group_is, counts)[:, None], xassign // TOP_K, 0).reshape(X_ROWS)
    plan = (te, first.astype(jnp.int32), next_expert, n_used.reshape(1), src, x_off)
    return plan, gather_row, w_sorted


def _moe(layer, y, ada_l, g, router_w, router_b, w1, b1p, w2, b2):
    h, idx_t, wts_t = _router(y, ada_l, g, router_w, router_b)
    plan, gather_row, w_sorted = _routing_plan(idx_t.T, wts_t.T)
    x_sorted = jnp.take(h, gather_row, axis=0, mode="clip")
    acc = _experts(layer, x_sorted, w_sorted, plan, w1, b1p, w2, b2)
    if layer == DEPTH - 1:
        return (_combine(y, acc, ada_l, 0, P_TILES), _combine(y, acc, ada_l, P_TILES, N_ROW_TILES - P_TILES))
    return _combine(y, acc, ada_l)


def kernel(x_prompt, x_sample, cache_attn_k, cache_attn_v, state_mlstm_C, state_mlstm_n, state_mlstm_m, c, c_ctx, ada_w, ada_b, norm_mix_g, norm_ffn_g, ab_w_in, ab_w_out, da_qnorm_g, da_knorm_g, da_lambda, da_subnorm_g, ml_conv_w, ml_conv_b, ml_gate_b, ml_headnorm_g, hy_w_in, hy_w_out, hy_conv_w, hy_conv_b, hy_f_w1, hy_f_b1, hy_f_freq1, hy_f_w2, hy_f_b2, hy_f_freq2, hy_f_w3, hy_bias, router_w, router_b, moe_w1, moe_b1, moe_w2, moe_b2):
    y = (x_prompt.reshape(T_P, D_MODEL), x_sample.reshape(T_S, D_MODEL))
    cond = jnp.concatenate([c_ctx[None, :], c, jnp.zeros((COND_ROWS - 1 - DEC_BATCH, D_MODEL), F32)], axis=0)
    ada = _ada_table(cond, ada_w, ada_b)
    b1p = moe_b1.reshape(DEPTH, N_EXPERTS, D_FF, 2).swapaxes(2, 3).reshape(DEPTH, N_EXPERTS, 1, 2 * D_FF)
    b2r = moe_b2.reshape(DEPTH, N_EXPERTS, 1, D_MODEL)
    new_k, new_v, new_c, new_n, new_m = [], [], [], [], []
    for layer in range(DEPTH):
        ada_l = ada[layer]
        if layer % 2 == 0:
            e = layer // 2
            lam_init = 0.8 - 0.6 * math.exp(-0.3 * layer)
            qkv, mqk, mv, mo, mg = _modulated_proj(
                y, ada_l, norm_mix_g[layer], ab_w_in[e], (3 * W_A, 2 * W_B, W_B, W_B, 4 * H_B))
            qg2 = jnp.tile(da_qnorm_g[e], 2).reshape(1, 2 * HD_A)
            kg2 = jnp.tile(da_knorm_g[e], 2).reshape(1, 2 * HD_A)
            sub_g = da_subnorm_g[e].reshape(1, 2 * HD_A)
            oa_p, k_norm = _attention_prompt(qkv, qg2, kg2, da_lambda[e], sub_g, lam_init)
            cos, sin = _rope_tables()
            oa_s = _attention_sample(
                qkv, cache_attn_k[:, e].reshape(DEC_BATCH, PAST_LEN, W_A),
                cache_attn_v[:, e].reshape(DEC_BATCH, PAST_LEN, W_A), cos, sin,
                qg2, kg2, da_lambda[e], sub_g, lam_init)
            ob_p, c_new, n_new, m_new = _mlstm(
                mqk, mv, mo, mg[:T_P], ml_conv_w[e], ml_conv_b[e].reshape(1, 2 * W_B), ml_gate_b[e],
                ml_headnorm_g[e], seq=SEQ, nbatch=BATCH, row_off=0)
            ob_s = _mlstm(
                mqk, mv, mo, mg[T_P:], ml_conv_w[e], ml_conv_b[e].reshape(1, 2 * W_B), ml_gate_b[e],
                ml_headnorm_g[e], seq=DEC_SEQ, nbatch=DEC_BATCH, row_off=T_P // DEC_SEQ,
                ctx=(state_mlstm_C[:, e], state_mlstm_n[:, e], state_mlstm_m[:, e]))
            y = _out_proj_residual([(oa_p, oa_s), (ob_p, ob_s)], y, ada_l, ab_w_out[e], 2)
            new_k.append(k_norm.reshape(BATCH, SEQ, H_A, 2, HD_A))
            new_v.append(qkv[:T_P, 2 * W_A:].reshape(BATCH, SEQ, H_A, 2 * HD_A))
            new_c.append(c_new)
            new_n.append(n_new.reshape(BATCH, 2, H_B, HD_B))
            new_m.append(m_new[..., 0, 0])
        else:
            o = layer // 2
            (zproj,) = _modulated_proj(y, ada_l, norm_mix_g[layer], hy_w_in[o], (HY_PROJ,))
            cores = []
            for seq, nbatch, row_off, td in ((SEQ, BATCH, 0, 512), (DEC_SEQ, DEC_BATCH, T_P // DEC_SEQ, 256)):
                fwd, inv = _dft_mats(seq)
                fwd_bf, inv_bf = fwd.astype(BF16), inv.astype(BF16)
                kspec = _hyena_filter_spectrum(seq, fwd_bf, hy_f_w1[o], hy_f_b1[o], hy_f_freq1[o], hy_f_w2[o],
                                               hy_f_b2[o], hy_f_freq2[o], hy_f_w3[o])
                cores.append(_hyena_core(zproj, hy_conv_w[o], hy_conv_b[o].reshape(1, HY_PROJ), fwd_bf, inv_bf,
                                         kspec, hy_bias[o], seq=seq, nbatch=nbatch, row_off=row_off, td=td))
            y = _out_proj_residual([tuple(cores)], y, ada_l, hy_w_out[o], 2)
        y = _moe(layer, y, ada_l, norm_ffn_g[layer], router_w[layer], router_b[layer],
                 moe_w1, b1p, moe_w2, b2r)
    y_p = y[0].reshape(BATCH, SEQ, D_MODEL)
    y_s = y[1].reshape(DEC_BATCH, DEC_SEQ, D_MODEL)
    return (y_p, y_s, jnp.stack(new_k, axis=1), jnp.stack(new_v, axis=1), jnp.stack(new_c, axis=1),
            jnp.stack(new_n, axis=1), jnp.stack(new_m, axis=1))
```

```python
import functools
import math

import numpy as np
import jax
import jax.numpy as jnp
from jax import lax
from jax.experimental import pallas as pl
from jax.experimental.pallas import tpu as pltpu

D_MODEL = 1024
BATCH = 16
SEQ = 256
DEPTH = 2
DEC_BATCH = 2
DEC_SEQ = 1024
PAST_LEN = 256
GRID_W = 64
W_A = D_MODEL // 2
HD_A = 64
H_A = W_A // (2 * HD_A)
W_B = D_MODEL - W_A
HD_B = 128
H_B = W_B // HD_B
AB_PROJ = 3 * W_A + 4 * W_B + 4 * H_B
ROPE_BASE = 10000.0
CHUNK = 64
HY_ORDER = 2
HY_PROJ = (HY_ORDER + 1) * D_MODEL
HY_BANDS = 8
HY_FH = 64
HY_TARGET = 1e-2
HY_FAST_PCT = 0.3
HY_SLOW_PCT = 1.5
N_EXPERTS = 32
TOP_K = 4
D_FF = D_MODEL
SWIGLU_ALPHA = 1.702
SWIGLU_LIMIT = 7.0
ADA_CHUNKS = 6
EPS = 1e-6
NEG = -1e30
F32 = jnp.float32
BF16 = jnp.bfloat16

T_P = BATCH * SEQ
T_S = DEC_BATCH * DEC_SEQ
T_ALL = T_P + T_S
ROW_TILE = 256
N_ROW_TILES = T_ALL // ROW_TILE
P_TILES = T_P // ROW_TILE
S_TILES_PER_BATCH = DEC_SEQ // ROW_TILE
COND_ROWS = 8
MOE_TILE = 256
N_ASSIGN = T_ALL * TOP_K
MOE_ROWS = N_ASSIGN + N_EXPERTS * MOE_TILE
MOE_TILES = MOE_ROWS // MOE_TILE
X_ALIGN = 16
GATHER_ROWS = 512
X_ROWS = -(-(N_ASSIGN + N_EXPERTS * X_ALIGN + MOE_TILE) // GATHER_ROWS) * GATHER_ROWS
SPARE_ROW = T_ALL
ACC_ROWS = T_ALL + 8
SCATTER_GROUP = 8
W1_DMA_CHUNKS = 8
W2_DMA_CHUNKS = 4
LANES = 128
ROW_CHUNKS = D_MODEL // LANES
VMEM_LIMIT = 56 * 1024 * 1024
HIGHEST = lax.Precision.HIGHEST


def _cparams(n_axes):
    return pltpu.CompilerParams(dimension_semantics=("arbitrary",) * n_axes,
                                vmem_limit_bytes=VMEM_LIMIT)


def _bdot(a, b):
    return jnp.dot(a.astype(BF16), b.astype(BF16), preferred_element_type=F32)


def _cond_row(i):
    return jnp.where(i < P_TILES, 0, 1 + (i - P_TILES) // S_TILES_PER_BATCH)


def _ada_chunk(ada_ref, row, j):
    return ada_ref[pl.ds(row, 1), j * D_MODEL:(j + 1) * D_MODEL]


def _modulate(x, g, shift, scale):
    ms = jnp.mean(x * x, axis=-1, keepdims=True)
    return (x * lax.rsqrt(ms + EPS) * g) * (1.0 + scale) + shift


def _sigmoid(x):
    return 1.0 / (1.0 + jnp.exp(-x))


def _silu(x):
    return x * _sigmoid(x)


def _log_sigmoid(x):
    return jnp.minimum(x, 0.0) - jnp.log(1.0 + jnp.exp(-jnp.abs(x)))


def _dwconv3(x, w, b):
    n = x.shape[0]
    row = lax.broadcasted_iota(jnp.int32, x.shape, 0)
    prev = jnp.where(row == 0, 0.0, pltpu.roll(x, 1, 0))
    nxt = jnp.where(row == n - 1, 0.0, pltpu.roll(x, n - 1, 0))
    return prev * w[0:1] + x * w[1:2] + nxt * w[2:3] + b


def _ada_kernel(cond_ref, w_ref, b_ref, o_ref):
    c = _silu(cond_ref[...])
    o_ref[...] = _bdot(c, w_ref[...]) + b_ref[...]


def _ada_table(cond, ada_w, ada_b):
    tn = 1536
    return pl.pallas_call(
        _ada_kernel,
        grid=(DEPTH, ADA_CHUNKS * D_MODEL // tn),
        in_specs=[
            pl.BlockSpec((COND_ROWS, D_MODEL), lambda l, j: (0, 0)),
            pl.BlockSpec((None, D_MODEL, tn), lambda l, j: (l, 0, j)),
            pl.BlockSpec((None, 1, tn), lambda l, j: (l, 0, j)),
        ],
        out_specs=pl.BlockSpec((None, COND_ROWS, tn), lambda l, j: (l, 0, j)),
        out_shape=jax.ShapeDtypeStruct((DEPTH, COND_ROWS, ADA_CHUNKS * D_MODEL), F32),
        compiler_params=_cparams(2),
    )(cond, ada_w, ada_b.reshape(DEPTH, 1, ADA_CHUNKS * D_MODEL))


def _stream_specs(y):
    if isinstance(y, tuple):
        return [pl.BlockSpec((ROW_TILE, D_MODEL), lambda i: (jnp.minimum(i, P_TILES - 1), 0)),
                pl.BlockSpec((ROW_TILE, D_MODEL), lambda i: (jnp.maximum(i - P_TILES, 0), 0))], list(y)
    return [pl.BlockSpec((ROW_TILE, D_MODEL), lambda i: (i, 0))], [y]


def _stream_tile(y_refs, i):
    if len(y_refs) == 2:
        return jnp.where(i < P_TILES, y_refs[0][...], y_refs[1][...])
    return y_refs[0][...]


def _proj_kernel(splits, n_y, *refs):
    y_refs = refs[:n_y]
    ada_ref, g_ref, w_ref = refs[n_y:n_y + 3]
    out_refs, wbf_ref = refs[n_y + 3:-1], refs[-1]
    i = pl.program_id(0)

    @pl.when(i == 0)
    def _():
        wbf_ref[...] = w_ref[...].astype(BF16)

    row = _cond_row(i)
    h = _modulate(_stream_tile(y_refs, i), g_ref[...], _ada_chunk(ada_ref, row, 0), _ada_chunk(ada_ref, row, 1))
    h = h.astype(BF16)
    lo = 0
    for o_ref, width in zip(out_refs, splits):
        o_ref[...] = jnp.dot(h, wbf_ref[:, lo:lo + width], preferred_element_type=F32)
        lo += width


def _modulated_proj(y, ada_l, g, w, splits):
    n = w.shape[1]
    y_specs, y_args = _stream_specs(y)
    return pl.pallas_call(
        functools.partial(_proj_kernel, splits, len(y_args)),
        grid=(N_ROW_TILES,),
        in_specs=y_specs + [
            pl.BlockSpec((COND_ROWS, ADA_CHUNKS * D_MODEL), lambda i: (0, 0)),
            pl.BlockSpec((1, D_MODEL), lambda i: (0, 0)),
            pl.BlockSpec((D_MODEL, n), lambda i: (0, 0), pipeline_mode=pl.Buffered(1)),
        ],
        out_specs=[pl.BlockSpec((ROW_TILE, s), lambda i: (i, 0)) for s in splits],
        out_shape=[jax.ShapeDtypeStruct((T_ALL, s), F32) for s in splits],
        scratch_shapes=[pltpu.VMEM((D_MODEL, n), BF16)],
        compiler_params=_cparams(1),
    )(*y_args, ada_l, g.reshape(1, D_MODEL), w)


def _out_proj_kernel(n_in, n_y, gate_chunk, *refs):
    x_refs = refs[:2 * n_in]
    y_refs = refs[2 * n_in:2 * n_in + n_y]
    ada_ref, w_ref, o_ref, wbf_ref = refs[2 * n_in + n_y:]
    i = pl.program_id(0)

    @pl.when(i == 0)
    def _():
        wbf_ref[...] = w_ref[...].astype(BF16)

    acc = None
    lo = 0
    for xp_ref, xs_ref in zip(x_refs[0::2], x_refs[1::2]):
        k = xp_ref.shape[1]
        x = jnp.where(i < P_TILES, xp_ref[...], xs_ref[...])
        part = jnp.dot(x.astype(BF16), wbf_ref[lo:lo + k, :], preferred_element_type=F32)
        acc = part if acc is None else acc + part
        lo += k
    gate = _ada_chunk(ada_ref, _cond_row(i), gate_chunk)
    o_ref[...] = _stream_tile(y_refs, i) + gate * acc


def _out_proj_residual(xs, y, ada_l, w, gate_chunk):
    y_specs, y_args = _stream_specs(y)
    x_specs = []
    for xp, _ in xs:
        x_specs.append(pl.BlockSpec((ROW_TILE, xp.shape[1]), lambda i: (jnp.minimum(i, P_TILES - 1), 0)))
        x_specs.append(pl.BlockSpec((ROW_TILE, xp.shape[1]), lambda i: (jnp.maximum(i - P_TILES, 0), 0)))
    return pl.pallas_call(
        functools.partial(_out_proj_kernel, len(xs), len(y_args), gate_chunk),
        grid=(N_ROW_TILES,),
        in_specs=x_specs + y_specs + [
            pl.BlockSpec((COND_ROWS, ADA_CHUNKS * D_MODEL), lambda i: (0, 0)),
            pl.BlockSpec((D_MODEL, D_MODEL), lambda i: (0, 0), pipeline_mode=pl.Buffered(1)),
        ],
        out_specs=pl.BlockSpec((ROW_TILE, D_MODEL), lambda i: (i, 0)),
        out_shape=jax.ShapeDtypeStruct((T_ALL, D_MODEL), F32),
        scratch_shapes=[pltpu.VMEM((D_MODEL, D_MODEL), BF16)],
        compiler_params=_cparams(1),
    )(*[a for pair in xs for a in pair], *y_args, ada_l, w)


def _subhead_norm(x, g2):
    lane = lax.broadcasted_iota(jnp.int32, x.shape, 1)
    first = lane < HD_A
    xx = x * x
    s0 = jnp.sum(jnp.where(first, xx, 0.0), axis=-1, keepdims=True)
    s1 = jnp.sum(jnp.where(first, 0.0, xx), axis=-1, keepdims=True)
    r = jnp.where(first, lax.rsqrt(s0 / HD_A + EPS), lax.rsqrt(s1 / HD_A + EPS))
    return x * r * g2


def _rope(x, cos, sin):
    quarter = HD_A // 4
    lane = lax.broadcasted_iota(jnp.int32, x.shape, 1)
    lower = (lane % (2 * quarter)) < quarter
    swapped = jnp.where(lower, pltpu.roll(x, 2 * HD_A - quarter, 1), pltpu.roll(x, quarter, 1))
    return x * cos + swapped * sin


def _attn_kernel(lam_init, has_ctx, *refs):
    if has_ctx:
        (q_ref, k_ref, v_ref, ck_ref, cv_ref, cq_ref, sq_ref, ckk_ref, skk_ref,
         qg_ref, kg_ref, lp_ref, sg_ref, o_ref, kall_ref, vall_ref) = refs
    else:
        q_ref, k_ref, v_ref, qg_ref, kg_ref, lp_ref, sg_ref, o_ref, kn_ref = refs
    lp = lp_ref[...]
    lam = (jnp.exp(jnp.sum(lp[0:1] * lp[1:2], axis=-1, keepdims=True))
           - jnp.exp(jnp.sum(lp[2:3] * lp[3:4], axis=-1, keepdims=True)) + lam_init)

    def attend(q, k, v):
        probs = []
        for c in range(2):
            qc = q[:, c * HD_A:(c + 1) * HD_A].astype(BF16)
            kc = k[:, c * HD_A:(c + 1) * HD_A].astype(BF16)
            s = lax.dot_general(qc, kc, (((1,), (1,)), ((), ())), preferred_element_type=F32) * (HD_A ** -0.5)
            e = jnp.exp(s - jnp.max(s, axis=-1, keepdims=True))
            probs.append(e / jnp.sum(e, axis=-1, keepdims=True))
        o = _bdot(probs[0] - lam * probs[1], v)
        ms = jnp.mean(o * o, axis=-1, keepdims=True)
        return (o * lax.rsqrt(ms + EPS) * sg_ref[...]) * (1.0 - lam_init)

    if not has_ctx:
        for h in range(H_A):
            cols = slice(h * 2 * HD_A, (h + 1) * 2 * HD_A)
            k = _subhead_norm(k_ref[:, cols], kg_ref[...])
            kn_ref[:, cols] = k
            o_ref[:, cols] = attend(_subhead_norm(q_ref[:, cols], qg_ref[...]), k, v_ref[:, cols])
        return

    @pl.when(pl.program_id(2) == 0)
    def _():
        kall_ref[0:PAST_LEN, :] = ck_ref[...].astype(BF16)
        vall_ref[0:PAST_LEN, :] = cv_ref[...].astype(BF16)
        k_new = _rope(_subhead_norm(k_ref[...], kg_ref[...]), ckk_ref[...], skk_ref[...])
        kall_ref[PAST_LEN:, :] = k_new.astype(BF16)
        vall_ref[PAST_LEN:, :] = v_ref[...].astype(BF16)

    q = _rope(_subhead_norm(q_ref[...], qg_ref[...]), cq_ref[...], sq_ref[...])
    o_ref[...] = attend(q, kall_ref[...], vall_ref[...])


def _attention_prompt(qkv, qg2, kg2, lam_p, sub_g, lam_init):
    head = 2 * HD_A
    small = [
        pl.BlockSpec((1, head), lambda b: (0, 0)),
        pl.BlockSpec((1, head), lambda b: (0, 0)),
        pl.BlockSpec((4, HD_A), lambda b: (0, 0)),
        pl.BlockSpec((1, head), lambda b: (0, 0)),
    ]
    return pl.pallas_call(
        functools.partial(_attn_kernel, lam_init, False),
        grid=(BATCH,),
        in_specs=[
            pl.BlockSpec((SEQ, W_A), lambda b: (b, 0)),
            pl.BlockSpec((SEQ, W_A), lambda b: (b, 1)),
            pl.BlockSpec((SEQ, W_A), lambda b: (b, 2)),
        ] + small,
        out_specs=[pl.BlockSpec((SEQ, W_A), lambda b: (b, 0)),
                   pl.BlockSpec((SEQ, W_A), lambda b: (b, 0))],
        out_shape=[jax.ShapeDtypeStruct((T_P, W_A), F32), jax.ShapeDtypeStruct((T_P, W_A), F32)],
        compiler_params=_cparams(1),
    )(qkv, qkv, qkv, qg2, kg2, lam_p, sub_g)


def _attention_sample(qkv, cache_k, cache_v, cos, sin, qg2, kg2, lam_p, sub_g, lam_init):
    nh = H_A
    head = 2 * HD_A
    tq = ROW_TILE
    nq = DEC_SEQ // tq
    q_off = T_P // tq
    k_off = T_P // DEC_SEQ
    small = [
        pl.BlockSpec((1, head), lambda b, h, i: (0, 0)),
        pl.BlockSpec((1, head), lambda b, h, i: (0, 0)),
        pl.BlockSpec((4, HD_A), lambda b, h, i: (0, 0)),
        pl.BlockSpec((1, head), lambda b, h, i: (0, 0)),
    ]
    return pl.pallas_call(
        functools.partial(_attn_kernel, lam_init, True),
        grid=(DEC_BATCH, nh, nq),
        in_specs=[
            pl.BlockSpec((tq, head), lambda b, h, i: (q_off + b * nq + i, h)),
            pl.BlockSpec((DEC_SEQ, head), lambda b, h, i: (k_off + b, nh + h)),
            pl.BlockSpec((DEC_SEQ, head), lambda b, h, i: (k_off + b, 2 * nh + h)),
            pl.BlockSpec((None, PAST_LEN, head), lambda b, h, i: (b, 0, h)),
            pl.BlockSpec((None, PAST_LEN, head), lambda b, h, i: (b, 0, h)),
            pl.BlockSpec((tq, head), lambda b, h, i: (i, 0)),
            pl.BlockSpec((tq, head), lambda b, h, i: (i, 0)),
            pl.BlockSpec((DEC_SEQ, head), lambda b, h, i: (0, 0)),
            pl.BlockSpec((DEC_SEQ, head), lambda b, h, i: (0, 0)),
        ] + small,
        out_specs=pl.BlockSpec((tq, head), lambda b, h, i: (b * nq + i, h)),
        out_shape=jax.ShapeDtypeStruct((T_S, W_A), F32),
        scratch_shapes=[pltpu.VMEM((PAST_LEN + DEC_SEQ, head), BF16)] * 2,
        compiler_params=_cparams(3),
    )(qkv, qkv, qkv, cache_k, cache_v, cos, sin, cos, sin, qg2, kg2, lam_p, sub_g)


def _rope_tables():
    half = HD_A // 2
    nf = half // 2
    inv = ROPE_BASE ** (-np.arange(nf, dtype=np.float32) / nf)
    pos = np.arange(DEC_SEQ)
    row = (pos // GRID_W).astype(np.float32)
    col = (pos % GRID_W).astype(np.float32)
    ang_r = (row[:, None] * inv).astype(np.float32)
    ang_c = (col[:, None] * inv).astype(np.float32)
    ang = np.concatenate([ang_r, ang_r, ang_c, ang_c], axis=1)
    sign = np.concatenate([-np.ones(nf), np.ones(nf), -np.ones(nf), np.ones(nf)]).astype(np.float32)
    cos = np.cos(ang.astype(np.float64)).astype(np.float32)
    sin = (np.sin(ang.astype(np.float64)) * sign).astype(np.float32)
    return jnp.asarray(np.tile(cos, (1, 2))), jnp.asarray(np.tile(sin, (1, 2)))


def _mlstm_kernel(seq, has_ctx, *refs):
    if has_ctx:
        (q_ref, k_ref, cwq_ref, cwk_ref, cbq_ref, cbk_ref, v_ref, mo_ref, gi_ref, gf_ref,
         gbi_ref, gbf_ref, hn_ref, c0_ref, n0_ref, m0_ref, o_ref,
         qs_ref, ks_ref, hf_ref, hb_ref, cs_ref, rrow_ref, col_ref, wc_ref) = refs
    else:
        (q_ref, k_ref, cwq_ref, cwk_ref, cbq_ref, cbk_ref, v_ref, mo_ref, gi_ref, gf_ref,
         gbi_ref, gbf_ref, hn_ref, o_ref, c_out_ref, n_out_ref, m_out_ref,
         qs_ref, ks_ref, hf_ref, hb_ref, cs_ref, rrow_ref, col_ref, wc_ref) = refs
    nc = seq // CHUNK
    n_chain = 2 * H_B
    chains = [(d, h) for d in range(2) for h in range(H_B)]
    qs_ref[...] = _silu(_dwconv3(q_ref[...], cwq_ref[...], cbq_ref[...])) * (HD_B ** -0.5)
    ks_ref[...] = _silu(_dwconv3(k_ref[...], cwk_ref[...], cbk_ref[...]))

    rows = nc * n_chain
    lane = lax.broadcasted_iota(jnp.int32, (rows, 2 * CHUNK), 1)
    forward = lax.broadcasted_iota(jnp.int32, (rows, 2 * CHUNK), 0) % n_chain < H_B
    valid = lane < CHUNK

    def scan(x, op, fill):
        pre, suf = x, x
        sh = 1
        while sh < CHUNK:
            pre = op(pre, jnp.where(lane >= sh, pltpu.roll(pre, sh, 1), fill))
            suf = op(suf, jnp.where(lane + sh < CHUNK, pltpu.roll(suf, 2 * CHUNK - sh, 1), fill))
            sh *= 2
        return jnp.where(forward, pre, suf)

    gate_i = (gi_ref[...] + gbi_ref[...]).reshape(rows, 2 * CHUNK)
    lf = jnp.where(valid, _log_sigmoid(gf_ref[...] + gbf_ref[...]).reshape(rows, 2 * CHUNK), 0.0)
    b = scan(lf, jnp.add, 0.0)
    cmax = scan(jnp.where(valid, gate_i - b, -jnp.inf), jnp.maximum, -jnp.inf)
    b_last = jnp.sum(lf, axis=1, keepdims=True)
    g = b_last - b + gate_i
    g_max = jnp.max(jnp.where(valid, g, -jnp.inf), axis=1, keepdims=True)
    mm = m0_ref[...] if has_ctx else jnp.zeros((n_chain, 1), F32)
    mm_seq = []
    for p in range(nc):
        mm_seq.append(mm)
        seg = slice(p * n_chain, (p + 1) * n_chain)
        mm = jnp.maximum(b_last[seg] + mm, g_max[seg])
    mm_final = mm
    mm_prev = jnp.concatenate(mm_seq, axis=0)
    mm_next = jnp.concatenate(mm_seq[1:] + [mm_final], axis=0)
    m_t = jnp.maximum(b + mm_prev, b + cmax)
    rrow_ref[...] = (b - gate_i).reshape(nc, n_chain, 2 * CHUNK)
    wc_ref[...] = jnp.exp(b_last + mm_prev - mm_next).reshape(nc, n_chain, 1)
    per_row = [b, m_t, jnp.exp(b + mm_prev - m_t), jnp.exp(-m_t), jnp.exp(g - mm_next)]
    for j, arr in enumerate(per_row):
        by_time = arr.T
        for p in range(nc):
            col_ref[p, :, j * n_chain:(j + 1) * n_chain] = by_time[0:CHUNK, p * n_chain:(p + 1) * n_chain]

    t_idx = lax.broadcasted_iota(jnp.int32, (CHUNK, CHUNK), 0)
    s_idx = lax.broadcasted_iota(jnp.int32, (CHUNK, CHUNK), 1)
    for n, (d, h) in enumerate(chains):
        cs_ref[n] = c0_ref[d, h] if has_ctx else jnp.zeros((HD_B, HD_B), F32)

    def out_step(p, n_states):
        cols = col_ref[p]
        rrows = rrow_ref[p]
        wcs = wc_ref[p]
        new_states = []
        for n, (d, h) in enumerate(chains):
            c = p if d == 0 else nc - 1 - p
            r0 = pl.multiple_of(c * CHUNK, CHUNK)
            hcols = slice(h * HD_B, (h + 1) * HD_B)
            qt = qs_ref[pl.ds(r0, CHUNK), hcols]
            kt = ks_ref[pl.ds(r0, CHUNK), hcols]
            vt = v_ref[pl.ds(r0, CHUNK), hcols]
            b_col, m_t, w_inter, e_inv, w_k = (cols[:, j * n_chain + n:j * n_chain + n + 1] for j in range(5))
            mask = (s_idx <= t_idx) if d == 0 else (s_idx >= t_idx)
            decay = jnp.exp(jnp.where(mask, b_col - rrows[n:n + 1, 0:CHUNK], NEG) - m_t)
            qk = lax.dot_general(qt.astype(BF16), kt.astype(BF16), (((1,), (1,)), ((), ())),
                                 preferred_element_type=F32)
            s = qk * decay
            cm = cs_ref[n]
            nm = n_states[n]
            cq = lax.dot_general(qt.astype(BF16), cm.astype(BF16), (((1,), (1,)), ((), ())),
                                 preferred_element_type=F32)
            num = _bdot(s, vt) + w_inter * cq
            nq = jnp.sum(s, axis=-1, keepdims=True) + w_inter * jnp.sum(qt * nm, axis=-1, keepdims=True)
            hdir_ref = hf_ref if d == 0 else hb_ref
            hdir_ref[pl.ds(r0, CHUNK), hcols] = num / jnp.maximum(jnp.abs(nq), e_inv)
            w_c = wcs[n:n + 1, :]
            vw = (vt * w_k).astype(BF16)
            cs_ref[n] = w_c * cm + lax.dot_general(vw, kt.astype(BF16), (((0,), (0,)), ((), ())),
                                                   preferred_element_type=F32)
            new_states.append(w_c * nm + jnp.sum(kt * w_k, axis=0, keepdims=True))
        return tuple(new_states)

    if has_ctx:
        n_init = tuple(n0_ref[d, h] for d, h in chains)
    else:
        n_init = tuple(jnp.zeros((1, HD_B), F32) for _ in chains)
    n_final = lax.fori_loop(0, nc, out_step, n_init)
    if not has_ctx:
        for n, (d, h) in enumerate(chains):
            c_out_ref[d, h] = cs_ref[n]
            n_out_ref[d, h] = n_final[n]
            m_out_ref[d, h] = jnp.broadcast_to(mm_final[n:n + 1, :], (1, HD_B))

    for h in range(H_B):
        hcols = slice(h * HD_B, (h + 1) * HD_B)
        hh = hf_ref[:, hcols] + hb_ref[:, hcols]
        ms = jnp.mean(hh * hh, axis=-1, keepdims=True)
        o_ref[:, hcols] = (hh * lax.rsqrt(ms + EPS) * hn_ref[:, hcols]) * _sigmoid(mo_ref[:, hcols])


def _mlstm(mqk, mv, mo, mg_stream, conv_w, conv_b, gate_b, hn_g, *, seq, nbatch, row_off, ctx=None):
    nh = H_B
    nc = seq // CHUNK
    has_ctx = ctx is not None
    gt = mg_stream.reshape(nbatch, nc, CHUNK, 2, 2, nh).transpose(0, 1, 3, 4, 5, 2)
    pad = ((0, 0), (0, 0), (0, 0), (0, CHUNK))
    gates = [jnp.pad(jnp.concatenate([gt[:, :, 0, j], gt[:, ::-1, 1, j]], axis=2), pad) for j in range(2)]
    gate_bias = [jnp.concatenate([gate_b[0, j], gate_b[1, j]]).reshape(2 * nh, 1) for j in range(2)]
    blk = lambda col: pl.BlockSpec((seq, W_B), lambda b, col=col: (row_off + b, col))
    gate_blk = pl.BlockSpec((None, nc, 2 * nh, 2 * CHUNK), lambda b: (b, 0, 0, 0))
    in_specs = [
        blk(0), blk(1),
        pl.BlockSpec((3, W_B), lambda b: (0, 0)),
        pl.BlockSpec((3, W_B), lambda b: (0, 1)),
        pl.BlockSpec((1, W_B), lambda b: (0, 0)),
        pl.BlockSpec((1, W_B), lambda b: (0, 1)),
        blk(0), blk(0),
        gate_blk, gate_blk,
        pl.BlockSpec((2 * nh, 1), lambda b: (0, 0)),
        pl.BlockSpec((2 * nh, 1), lambda b: (0, 0)),
        pl.BlockSpec((1, W_B), lambda b: (0, 0)),
    ]
    args = [mqk, mqk, conv_w, conv_w, conv_b, conv_b, mv, mo, gates[0], gates[1],
            gate_bias[0], gate_bias[1], hn_g.reshape(1, W_B)]
    o_spec = pl.BlockSpec((seq, W_B), lambda b: (b, 0))
    o_shape = jax.ShapeDtypeStruct((nbatch * seq, W_B), F32)
    state_blk = lambda rows: pl.BlockSpec((None, 2, nh, rows, HD_B), lambda b: (b, 0, 0, 0, 0))
    if has_ctx:
        c0, n0, m0 = ctx
        in_specs += [state_blk(HD_B), state_blk(1), pl.BlockSpec((None, 2 * nh, 1), lambda b: (b, 0, 0))]
        args += [c0, n0.reshape(nbatch, 2, nh, 1, HD_B), m0.reshape(nbatch, 2 * nh, 1)]
        out_specs, out_shape = o_spec, o_shape
    else:
        out_specs = [o_spec, state_blk(HD_B), state_blk(1), state_blk(1)]
        out_shape = [
            o_shape,
            jax.ShapeDtypeStruct((nbatch, 2, nh, HD_B, HD_B), F32),
            jax.ShapeDtypeStruct((nbatch, 2, nh, 1, HD_B), F32),
            jax.ShapeDtypeStruct((nbatch, 2, nh, 1, HD_B), F32),
        ]
    return pl.pallas_call(
        functools.partial(_mlstm_kernel, seq, has_ctx),
        grid=(nbatch,),
        in_specs=in_specs,
        out_specs=out_specs,
        out_shape=out_shape,
        scratch_shapes=[pltpu.VMEM((seq, W_B), F32)] * 4 + [
            pltpu.VMEM((2 * nh, HD_B, HD_B), F32),
            pltpu.VMEM((nc, 2 * nh, 2 * CHUNK), F32),
            pltpu.VMEM((nc, CHUNK, 5 * 2 * nh), F32),
            pltpu.VMEM((nc, 2 * nh, 1), F32),
        ],
        compiler_params=_cparams(1),
    )(*args)


def _dft_mats(L):
    f = np.arange(L)[:, None]
    j = np.arange(L)[None, :]
    ang = 2.0 * np.pi * ((f * j) % (2 * L)) / (2 * L)
    cm = np.cos(ang)
    sm = np.sin(ang)
    alt = (1.0 - 2.0 * (np.arange(L) % 2))
    fwd_b = -sm
    fwd_b[0, :] = alt
    fwd = np.concatenate([cm, fwd_b], axis=0)
    wgt = np.where(np.arange(L) == 0, 1.0, 2.0)[None, :]
    inv_a = cm.T * wgt
    inv_b = -2.0 * sm.T
    inv_b[:, 0] = alt
    inv = np.concatenate([inv_a, inv_b], axis=1) / (2 * L)
    return jnp.asarray(fwd.astype(np.float32)), jnp.asarray(inv.astype(np.float32))


def _hyena_feats(L):
    t = np.linspace(0.0, 1.0, L, dtype=np.float32)
    wpos = (2.0 * math.pi * np.arange(L, dtype=np.float32) / L).astype(np.float32)
    fb = np.linspace(1e-4, HY_BANDS - 1, HY_BANDS, dtype=np.float32)
    z = (wpos[:, None] * fb).astype(np.float32)
    feats = np.concatenate([t[:, None], np.cos(z), -np.sin(z)], axis=-1).astype(np.float32)
    deltas = np.abs(np.linspace(math.log(HY_TARGET) / HY_SLOW_PCT, math.log(HY_TARGET) / HY_FAST_PCT,
                                D_MODEL, dtype=np.float32))
    decay = np.exp(-t[:, None] * deltas).astype(np.float32)
    return jnp.asarray(feats), jnp.asarray(decay)


def _filter_kernel(L, feats_ref, w1_ref, b1_ref, fr1_ref, w2_ref, b2_ref, fr2_ref, w3f_ref, w3b_ref,
                   decay_ref, fwd_ref, o_ref, hdn_ref):
    @pl.when((pl.program_id(0) == 0) & (pl.program_id(1) == 0))
    def _():
        h1 = jnp.sin(fr1_ref[...] * (jnp.dot(feats_ref[...], w1_ref[...], precision=HIGHEST,
                                             preferred_element_type=F32) + b1_ref[...]))
        hdn_ref[...] = jnp.sin(fr2_ref[...] * (jnp.dot(h1, w2_ref[...], precision=HIGHEST,
                                                       preferred_element_type=F32) + b2_ref[...]))

    hdn = hdn_ref[...]
    decay = decay_ref[...]
    f_fwd = jnp.dot(hdn, w3f_ref[...], precision=HIGHEST, preferred_element_type=F32) * decay
    f_bwd = jnp.dot(hdn, w3b_ref[...], precision=HIGHEST, preferred_element_type=F32) * decay
    row = lax.broadcasted_iota(jnp.int32, f_bwd.shape, 0)
    f_bwd = jnp.where(row == 0, 0.0, f_bwd)
    fwd = fwd_ref[...]
    p = _bdot(fwd, f_fwd)
    q = _bdot(fwd, f_bwd)
    first = row == 0
    o_ref[0:L, :] = p[0:L] + q[0:L]
    o_ref[L:2 * L, :] = p[L:2 * L] + jnp.where(first, q[L:2 * L], -q[L:2 * L])


def _hyena_filter_spectrum(L, fwd_bf, w1, b1, fr1, w2, b2, fr2, w3):
    feats, decay = _hyena_feats(L)
    td = 512
    nd = D_MODEL // td
    emb = feats.shape[1]
    vec = lambda a: a.reshape(1, HY_FH)
    full = lambda shape: pl.BlockSpec(shape, lambda o, j: (0, 0))
    return pl.pallas_call(
        functools.partial(_filter_kernel, L),
        grid=(HY_ORDER, nd),
        in_specs=[
            full((L, emb)), full((emb, HY_FH)), full((1, HY_FH)), full((1, HY_FH)),
            full((HY_FH, HY_FH)), full((1, HY_FH)), full((1, HY_FH)),
            pl.BlockSpec((HY_FH, td), lambda o, j: (0, o * 2 * nd + j)),
            pl.BlockSpec((HY_FH, td), lambda o, j: (0, o * 2 * nd + nd + j)),
            pl.BlockSpec((L, td), lambda o, j: (0, j)),
            full((2 * L, L)),
        ],
        out_specs=pl.BlockSpec((2 * L, td), lambda o, j: (0, o * nd + j)),
        out_shape=jax.ShapeDtypeStruct((2 * L, HY_ORDER * D_MODEL), F32),
        scratch_shapes=[pltpu.VMEM((L, HY_FH), F32)],
        compiler_params=_cparams(2),
    )(feats, w1, vec(b1), vec(fr1), w2, vec(b2), vec(fr2), w3, w3, decay, fwd_bf)


def _spectral_conv(u, fwd, inv, kspec, L):
    uf = jnp.dot(fwd, u.astype(BF16), preferred_element_type=F32)
    ua, ub = uf[0:L], uf[L:2 * L]
    ka, kb = kspec[0:L], kspec[L:2 * L]
    first = lax.broadcasted_iota(jnp.int32, ua.shape, 0) == 0
    ya = ua * ka - jnp.where(first, 0.0, ub * kb)
    yb = jnp.where(first, ub * kb, ua * kb + ub * ka)
    y = jnp.concatenate([ya, yb], axis=0).astype(BF16)
    return jnp.dot(inv, y, preferred_element_type=F32)


def _hyena_kernel(L, zv_ref, z1_ref, z2_ref, cwv_ref, cw1_ref, cw2_ref, cbv_ref, cb1_ref, cb2_ref,
                  fwd_ref, inv_ref, k0_ref, k1_ref, bias0_ref, bias1_ref, o_ref):
    fwd = fwd_ref[...]
    inv = inv_ref[...]
    v = _dwconv3(zv_ref[...], cwv_ref[...], cbv_ref[...])
    x1 = _dwconv3(z1_ref[...], cw1_ref[...], cb1_ref[...])
    x2 = _dwconv3(z2_ref[...], cw2_ref[...], cb2_ref[...])
    z = x1 * (_spectral_conv(v, fwd, inv, k0_ref[...], L) + v * bias0_ref[...])
    o_ref[...] = x2 * (_spectral_conv(z, fwd, inv, k1_ref[...], L) + z * bias1_ref[...])


def _hyena_core(zproj, conv_w, conv_b, fwd_bf, inv_bf, kspec, bias, *, seq, nbatch, row_off, td):
    nd = D_MODEL // td
    zblk = lambda part: pl.BlockSpec((seq, td), lambda b, j, part=part: (row_off + b, part * nd + j))
    cwblk = lambda part: pl.BlockSpec((3, td), lambda b, j, part=part: (0, part * nd + j))
    cbblk = lambda part: pl.BlockSpec((1, td), lambda b, j, part=part: (0, part * nd + j))
    return pl.pallas_call(
        functools.partial(_hyena_kernel, seq),
        grid=(nbatch, nd),
        in_specs=[
            zblk(0), zblk(1), zblk(2), cwblk(0), cwblk(1), cwblk(2), cbblk(0), cbblk(1), cbblk(2),
            pl.BlockSpec((2 * seq, seq), lambda b, j: (0, 0), pipeline_mode=pl.Buffered(1)),
            pl.BlockSpec((seq, 2 * seq), lambda b, j: (0, 0), pipeline_mode=pl.Buffered(1)),
            pl.BlockSpec((2 * seq, td), lambda b, j: (0, j)),
            pl.BlockSpec((2 * seq, td), lambda b, j: (0, nd + j)),
            pl.BlockSpec((None, 1, td), lambda b, j: (0, 0, j)),
            pl.BlockSpec((None, 1, td), lambda b, j: (1, 0, j)),
        ],
        out_specs=pl.BlockSpec((seq, td), lambda b, j: (b, j)),
        out_shape=jax.ShapeDtypeStruct((nbatch * seq, D_MODEL), F32),
        compiler_params=_cparams(2),
    )(zproj, zproj, zproj, conv_w, conv_w, conv_w, conv_b, conv_b, conv_b,
      fwd_bf, inv_bf, kspec, kspec, bias.reshape(HY_ORDER, 1, D_MODEL), bias.reshape(HY_ORDER, 1, D_MODEL))


def _router_kernel(x_ref, ada_ref, g_ref, rw_ref, rb_ref, h_ref, idx_ref, wt_ref):
    i = pl.program_id(0)
    row = _cond_row(i)
    h = _modulate(x_ref[...], g_ref[...], _ada_chunk(ada_ref, row, 3), _ada_chunk(ada_ref, row, 4))
    for j in range(ROW_CHUNKS):
        h_ref[pl.ds(j, ROW_TILE, stride=ROW_CHUNKS), :] = h[:, j * LANES:(j + 1) * LANES]
    logits = lax.dot_general(rw_ref[...], h, (((1,), (1,)), ((), ())), precision=HIGHEST,
                             preferred_element_type=F32) + rb_ref[...]
    expert = lax.broadcasted_iota(jnp.int32, logits.shape, 0)
    slot = lax.broadcasted_iota(jnp.int32, (TOP_K, logits.shape[1]), 0)
    vals = jnp.zeros((TOP_K, logits.shape[1]), F32)
    idxs = jnp.zeros((TOP_K, logits.shape[1]), jnp.int32)
    cur = logits
    for k in range(TOP_K):
        m = jnp.max(cur, axis=0, keepdims=True)
        a = jnp.min(jnp.where(cur == m, expert, N_EXPERTS), axis=0, keepdims=True)
        vals = jnp.where(slot == k, m, vals)
        idxs = jnp.where(slot == k, a, idxs)
        cur = jnp.where(expert == a, -jnp.inf, cur)
    e = jnp.exp(vals - vals[0:1])
    wt_ref[...] = e / jnp.sum(e, axis=0, keepdims=True)
    idx_ref[...] = idxs


def _router(y, ada_l, g, router_w, router_b):
    return pl.pallas_call(
        _router_kernel,
        grid=(N_ROW_TILES,),
        in_specs=[
            pl.BlockSpec((ROW_TILE, D_MODEL), lambda i: (i, 0)),
            pl.BlockSpec((COND_ROWS, ADA_CHUNKS * D_MODEL), lambda i: (0, 0)),
            pl.BlockSpec((1, D_MODEL), lambda i: (0, 0)),
            pl.BlockSpec((N_EXPERTS, D_MODEL), lambda i: (0, 0)),
            pl.BlockSpec((N_EXPERTS, 1), lambda i: (0, 0)),
        ],
        out_specs=[
            pl.BlockSpec((ROW_TILE * ROW_CHUNKS, LANES), lambda i: (i, 0)),
            pl.BlockSpec((TOP_K, ROW_TILE), lambda i: (0, i)),
            pl.BlockSpec((TOP_K, ROW_TILE), lambda i: (0, i)),
        ],
        out_shape=[
            jax.ShapeDtypeStruct((T_ALL * ROW_CHUNKS, LANES), F32),
            jax.ShapeDtypeStruct((TOP_K, T_ALL), jnp.int32),
            jax.ShapeDtypeStruct((TOP_K, T_ALL), F32),
        ],
        compiler_params=_cparams(1),
    )(y, ada_l, g.reshape(1, D_MODEL), router_w.T, router_b.reshape(N_EXPERTS, 1))


def _dispatch_kernel(rows_ref, h_ref, o_ref):
    base = pl.program_id(0) * GATHER_ROWS
    for r in range(GATHER_ROWS):
        t = pl.multiple_of(rows_ref[base + r] * ROW_CHUNKS, ROW_CHUNKS)
        o_ref[r * ROW_CHUNKS:(r + 1) * ROW_CHUNKS, :] = h_ref[pl.ds(t, ROW_CHUNKS), :]


def _dispatch(h_tiles, gather_row):
    grid_spec = pltpu.PrefetchScalarGridSpec(
        num_scalar_prefetch=1,
        grid=(X_ROWS // GATHER_ROWS,),
        in_specs=[pl.BlockSpec((T_ALL * ROW_CHUNKS, LANES), lambda i, rows: (0, 0), pipeline_mode=pl.Buffered(1))],
        out_specs=pl.BlockSpec((GATHER_ROWS * ROW_CHUNKS, LANES), lambda i, rows: (i, 0)),
    )
    return pl.pallas_call(
        _dispatch_kernel,
        grid_spec=grid_spec,
        out_shape=jax.ShapeDtypeStruct((X_ROWS * ROW_CHUNKS, LANES), F32),
        compiler_params=_cparams(1),
    )(gather_row, h_tiles)


def _deinterleave_matrix():
    s = np.zeros((256, 256), np.float32)
    j = np.arange(128)
    s[2 * j, j] = 1.0
    s[2 * j + 1, 128 + j] = 1.0
    return jnp.asarray(s)


def _weight_copies(layer, e, w1_hbm, w2_hbm, w1s_ref, w2s_ref, sem):
    copies = []
    r1 = D_MODEL // W1_DMA_CHUNKS
    for c in range(W1_DMA_CHUNKS):
        copies.append(pltpu.make_async_copy(w1_hbm.at[layer, e, pl.ds(c * r1, r1)],
                                            w1s_ref.at[pl.ds(c * r1, r1)], sem.at[c]))
    r2 = D_FF // W2_DMA_CHUNKS
    for c in range(W2_DMA_CHUNKS):
        copies.append(pltpu.make_async_copy(w2_hbm.at[layer, e, pl.ds(c * r2, r2)],
                                            w2s_ref.at[pl.ds(c * r2, r2)], sem.at[W1_DMA_CHUNKS + c]))
    return copies


def _expert_kernel(layer, te_ref, tf_ref, ne_ref, nu_ref, src_ref, xo_ref, x_ref, b1_ref, b2_ref, wt_ref, s_ref,
                   w1_hbm, w2_hbm, o_hbm, w1s_ref, w2s_ref, w1p_ref, w2p_ref, acc_ref, out_ref, wsem, osem):
    i = pl.program_id(0)
    half = 128
    copies = functools.partial(_weight_copies, layer, w1_hbm=w1_hbm, w2_hbm=w2_hbm,
                               w1s_ref=w1s_ref, w2s_ref=w2s_ref, sem=wsem)

    @pl.when(i == 0)
    def _():
        acc_ref[...] = jnp.zeros_like(acc_ref)
        out_ref[...] = jnp.zeros_like(out_ref)
        for cp in copies(te_ref[0]):
            cp.start()

    @pl.when(tf_ref[i] == 1)
    def _():
        for cp in copies(te_ref[i]):
            cp.wait()
        s = s_ref[...].astype(BF16)
        for c in range(2 * D_FF // 256):
            blk = jnp.dot(w1s_ref[:, c * 256:(c + 1) * 256].astype(BF16), s, preferred_element_type=F32)
            w1p_ref[:, c * half:(c + 1) * half] = blk[:, :half].astype(BF16)
            w1p_ref[:, D_FF + c * half:D_FF + (c + 1) * half] = blk[:, half:].astype(BF16)
        w2p_ref[...] = w2s_ref[...].astype(BF16)

        @pl.when(ne_ref[i] >= 0)
        def _():
            for cp in copies(ne_ref[i]):
                cp.start()

    @pl.when(i <= nu_ref[0])
    def _():
        base = i * MOE_TILE
        prev = (i + 1) % 2
        for r0 in range(0, MOE_TILE, SCATTER_GROUP):
            toks = [pl.multiple_of(src_ref[base + r0 + g] * ROW_CHUNKS, ROW_CHUNKS) for g in range(SCATTER_GROUP)]
            cur = [acc_ref[pl.ds(toks[g], ROW_CHUNKS), :] for g in range(SCATTER_GROUP)]
            add = [out_ref[prev, (r0 + g) * ROW_CHUNKS:(r0 + g + 1) * ROW_CHUNKS, :] for g in range(SCATTER_GROUP)]
            for g in range(SCATTER_GROUP):
                acc_ref[pl.ds(toks[g], ROW_CHUNKS), :] = cur[g] + add[g]
        x = jnp.concatenate([x_ref[pl.ds(j, MOE_TILE, stride=ROW_CHUNKS), :] for j in range(ROW_CHUNKS)], axis=1)
        a = jnp.dot(x.astype(BF16), w1p_ref[...], preferred_element_type=F32) + b1_ref[...]
        glu = jnp.minimum(a[:, :D_FF], SWIGLU_LIMIT)
        lin = jnp.clip(a[:, D_FF:], -SWIGLU_LIMIT, SWIGLU_LIMIT)
        hid = glu * _sigmoid(SWIGLU_ALPHA * glu) * (lin + 1.0)
        out = (jnp.dot(hid.astype(BF16), w2p_ref[...], preferred_element_type=F32) + b2_ref[...]) * wt_ref[...]
        cur_buf = i % 2
        for j in range(ROW_CHUNKS):
            out_ref[cur_buf, pl.ds(j, MOE_TILE, stride=ROW_CHUNKS), :] = out[:, j * LANES:(j + 1) * LANES]

    @pl.when(i == pl.num_programs(0) - 1)
    def _():
        cp = pltpu.make_async_copy(acc_ref.at[pl.ds(0, T_ALL * ROW_CHUNKS)], o_hbm, osem)
        cp.start()
        cp.wait()


def _experts(layer, x_sorted, w_sorted, plan, w1, b1p, w2, b2):
    tile_expert, tile_first, next_expert, n_used, src, x_off = plan
    grid_spec = pltpu.PrefetchScalarGridSpec(
        num_scalar_prefetch=6,
        grid=(MOE_TILES,),
        in_specs=[
            pl.BlockSpec((pl.Element(MOE_TILE * ROW_CHUNKS), pl.Element(LANES)),
                         lambda i, te, tf, ne, nu, src, xo: (pl.multiple_of(xo[i] * ROW_CHUNKS, X_ALIGN * ROW_CHUNKS), 0)),
            pl.BlockSpec((None, None, 1, 2 * D_FF), lambda i, te, *_: (layer, te[i], 0, 0)),
            pl.BlockSpec((None, None, 1, D_MODEL), lambda i, te, *_: (layer, te[i], 0, 0)),
            pl.BlockSpec((MOE_TILE, 1), lambda i, te, *_: (i, 0)),
            pl.BlockSpec((256, 256), lambda i, te, *_: (0, 0)),
            pl.BlockSpec(memory_space=pl.ANY),
            pl.BlockSpec(memory_space=pl.ANY),
        ],
        out_specs=pl.BlockSpec(memory_space=pl.ANY),
        scratch_shapes=[
            pltpu.VMEM((D_MODEL, 2 * D_FF), F32),
            pltpu.VMEM((D_FF, D_MODEL), F32),
            pltpu.VMEM((D_MODEL, 2 * D_FF), BF16),
            pltpu.VMEM((D_FF, D_MODEL), BF16),
            pltpu.VMEM((ACC_ROWS * ROW_CHUNKS, LANES), F32),
            pltpu.VMEM((2, MOE_TILE * ROW_CHUNKS, LANES), F32),
            pltpu.SemaphoreType.DMA((W1_DMA_CHUNKS + W2_DMA_CHUNKS,)),
            pltpu.SemaphoreType.DMA(()),
        ],
    )
    return pl.pallas_call(
        functools.partial(_expert_kernel, layer),
        grid_spec=grid_spec,
        out_shape=jax.ShapeDtypeStruct((T_ALL * ROW_CHUNKS, LANES), F32),
        compiler_params=_cparams(1),
    )(tile_expert, tile_first, next_expert, n_used, src, x_off, x_sorted, b1p, b2, w_sorted,
      _deinterleave_matrix(), w1, w2)


def _combine_kernel(first_tile, y_ref, a_ref, ada_ref, o_ref):
    gate = _ada_chunk(ada_ref, _cond_row(first_tile + pl.program_id(0)), 5)
    acc = jnp.concatenate([a_ref[pl.ds(j, ROW_TILE, stride=ROW_CHUNKS), :] for j in range(ROW_CHUNKS)], axis=1)
    o_ref[...] = y_ref[...] + gate * acc


def _combine(y, acc, ada_l, first_tile=0, n_tiles=N_ROW_TILES):
    return pl.pallas_call(
        functools.partial(_combine_kernel, first_tile),
        grid=(n_tiles,),
        in_specs=[
            pl.BlockSpec((ROW_TILE, D_MODEL), lambda i: (first_tile + i, 0)),
            pl.BlockSpec((ROW_TILE * ROW_CHUNKS, LANES), lambda i: (first_tile + i, 0)),
            pl.BlockSpec((COND_ROWS, ADA_CHUNKS * D_MODEL), lambda i: (0, 0)),
        ],
        out_specs=pl.BlockSpec((ROW_TILE, D_MODEL), lambda i: (i, 0)),
        out_shape=jax.ShapeDtypeStruct((n_tiles * ROW_TILE, D_MODEL), F32),
        compiler_params=_cparams(1),
    )(y, acc, ada_l)


def _routing_plan(idx, wts):
    eid = idx.reshape(-1)
    order = jnp.argsort(eid, stable=True).astype(jnp.int32)
    experts = jnp.arange(N_EXPERTS, dtype=jnp.int32)
    counts = jnp.sum(eid[:, None] == experts[None, :], axis=0).astype(jnp.int32)
    ntiles = (counts + MOE_TILE - 1) // MOE_TILE
    tile_end = jnp.cumsum(ntiles).astype(jnp.int32)
    tile_begin = tile_end - ntiles
    cstarts = (jnp.cumsum(counts) - counts).astype(jnp.int32)
    n_used = tile_end[-1]
    tile = jnp.arange(MOE_TILES, dtype=jnp.int32)
    te = jnp.minimum(jnp.sum(tile[:, None] >= tile_end[None, :], axis=1), N_EXPERTS - 1).astype(jnp.int32)
    used = tile < n_used
    prev = jnp.concatenate([jnp.full((1,), -1, jnp.int32), te[:-1]])
    first = (te != prev) & used

    def pick(onehot, table):
        return jnp.sum(jnp.where(onehot, table[None, :], 0), axis=1).astype(jnp.int32)

    tile_is = te[:, None] == experts[None, :]
    later = (experts[None, :] > experts[:, None]) & (ntiles[None, :] > 0)
    following = jnp.min(jnp.where(later, experts[None, :], N_EXPERTS), axis=1)
    following = jnp.where(following < N_EXPERTS, following, -1)
    next_expert = jnp.where(first, pick(tile_is, following), -1).astype(jnp.int32)
    tile_in_expert = tile - pick(tile_is, tile_begin)
    off = tile_in_expert[:, None] * MOE_TILE + jnp.arange(MOE_TILE, dtype=jnp.int32)[None, :]
    valid = (off < pick(tile_is, counts)[:, None]) & used[:, None]
    assign = order[jnp.clip(pick(tile_is, cstarts)[:, None] + off, 0, N_ASSIGN - 1)]
    token = assign // TOP_K
    src = jnp.where(valid, token, SPARE_ROW).reshape(MOE_ROWS).astype(jnp.int32)
    src = jnp.concatenate([jnp.full((MOE_TILE,), SPARE_ROW, jnp.int32), src])
    w_sorted = jnp.where(valid, wts.reshape(-1)[assign], 0.0).reshape(MOE_ROWS, 1)
    seg = ((counts + X_ALIGN - 1) // X_ALIGN) * X_ALIGN
    seg_end = jnp.cumsum(seg).astype(jnp.int32)
    seg_begin = seg_end - seg
    x_off = jnp.where(used, pick(tile_is, seg_begin) + tile_in_expert * MOE_TILE, 0).astype(jnp.int32)
    group = jnp.arange(X_ROWS // X_ALIGN, dtype=jnp.int32) * X_ALIGN
    group_is = (group[:, None] >= seg_begin[None, :]) & (group[:, None] < seg_end[None, :])
    xoffset = (group - pick(group_is, seg_begin))[:, None] + jnp.arange(X_ALIGN, dtype=jnp.int32)[None, :]
    xassign = order[jnp.clip(pick(group_is, cstarts)[:, None] + xoffset, 0, N_ASSIGN - 1)]
    gather_row = jnp.where(xoffset < pick(group_is, counts)[:, None], xassign // TOP_K, 0).reshape(X_ROWS)
    plan = (te, first.astype(jnp.int32), next_expert, n_used.reshape(1), src, x_off)
    return plan, gather_row, w_sorted


def _moe(layer, y, ada_l, g, router_w, router_b, w1, b1p, w2, b2):
    h, idx_t, wts_t = _router(y, ada_l, g, router_w, router_b)
    plan, gather_row, w_sorted = _routing_plan(idx_t.T, wts_t.T)
    x_sorted = _dispatch(h, gather_row)
    acc = _experts(layer, x_sorted, w_sorted, plan, w1, b1p, w2, b2)
    if layer == DEPTH - 1:
        return (_combine(y, acc, ada_l, 0, P_TILES), _combine(y, acc, ada_l, P_TILES, N_ROW_TILES - P_TILES))
    return _combine(y, acc, ada_l)


def kernel(x_prompt, x_sample, cache_attn_k, cache_attn_v, state_mlstm_C, state_mlstm_n, state_mlstm_m, c, c_ctx, ada_w, ada_b, norm_mix_g, norm_ffn_g, ab_w_in, ab_w_out, da_qnorm_g, da_knorm_g, da_lambda, da_subnorm_g, ml_conv_w, ml_conv_b, ml_gate_b, ml_headnorm_g, hy_w_in, hy_w_out, hy_conv_w, hy_conv_b, hy_f_w1, hy_f_b1, hy_f_freq1, hy_f_w2, hy_f_b2, hy_f_freq2, hy_f_w3, hy_bias, router_w, router_b, moe_w1, moe_b1, moe_w2, moe_b2):
    y = (x_prompt.reshape(T_P, D_MODEL), x_sample.reshape(T_S, D_MODEL))
    cond = jnp.concatenate([c_ctx[None, :], c, jnp.zeros((COND_ROWS - 1 - DEC_BATCH, D_MODEL), F32)], axis=0)
    ada = _ada_table(cond, ada_w, ada_b)
    b1p = moe_b1.reshape(DEPTH, N_EXPERTS, D_FF, 2).swapaxes(2, 3).reshape(DEPTH, N_EXPERTS, 1, 2 * D_FF)
    b2r = moe_b2.reshape(DEPTH, N_EXPERTS, 1, D_MODEL)
    new_k, new_v, new_c, new_n, new_m = [], [], [], [], []
    for layer in range(DEPTH):
        ada_l = ada[layer]
        if layer % 2 == 0:
            e = layer // 2
            lam_init = 0.8 - 0.6 * math.exp(-0.3 * layer)
            qkv, mqk, mv, mo, mg = _modulated_proj(
                y, ada_l, norm_mix_g[layer], ab_w_in[e], (3 * W_A, 2 * W_B, W_B, W_B, 4 * H_B))
            qg2 = jnp.tile(da_qnorm_g[e], 2).reshape(1, 2 * HD_A)
            kg2 = jnp.tile(da_knorm_g[e], 2).reshape(1, 2 * HD_A)
            sub_g = da_subnorm_g[e].reshape(1, 2 * HD_A)
            oa_p, k_norm = _attention_prompt(qkv, qg2, kg2, da_lambda[e], sub_g, lam_init)
            cos, sin = _rope_tables()
            oa_s = _attention_sample(
                qkv, cache_attn_k[:, e].reshape(DEC_BATCH, PAST_LEN, W_A),
                cache_attn_v[:, e].reshape(DEC_BATCH, PAST_LEN, W_A), cos, sin,
                qg2, kg2, da_lambda[e], sub_g, lam_init)
            ob_p, c_new, n_new, m_new = _mlstm(
                mqk, mv, mo, mg[:T_P], ml_conv_w[e], ml_conv_b[e].reshape(1, 2 * W_B), ml_gate_b[e],
                ml_headnorm_g[e], seq=SEQ, nbatch=BATCH, row_off=0)
            ob_s = _mlstm(
                mqk, mv, mo, mg[T_P:], ml_conv_w[e], ml_conv_b[e].reshape(1, 2 * W_B), ml_gate_b[e],
                ml_headnorm_g[e], seq=DEC_SEQ, nbatch=DEC_BATCH, row_off=T_P // DEC_SEQ,
                ctx=(state_mlstm_C[:, e], state_mlstm_n[:, e], state_mlstm_m[:, e]))
            y = _out_proj_residual([(oa_p, oa_s), (ob_p, ob_s)], y, ada_l, ab_w_out[e], 2)
            new_k.append(k_norm.reshape(BATCH, SEQ, H_A, 2, HD_A))
            new_v.append(qkv[:T_P, 2 * W_A:].reshape(BATCH, SEQ, H_A, 2 * HD_A))
            new_c.append(c_new)
            new_n.append(n_new.reshape(BATCH, 2, H_B, HD_B))
            new_m.append(m_new[..., 0, 0])
        else:
            o = layer // 2
            (zproj,) = _modulated_proj(y, ada_l, norm_mix_g[layer], hy_w_in[o], (HY_PROJ,))
            cores = []
            for seq, nbatch, row_off, td in ((SEQ, BATCH, 0, 512), (DEC_SEQ, DEC_BATCH, T_P // DEC_SEQ, 256)):
                fwd, inv = _dft_mats(seq)
                fwd_bf, inv_bf = fwd.astype(BF16), inv.astype(BF16)
                kspec = _hyena_filter_spectrum(seq, fwd_bf, hy_f_w1[o], hy_f_b1[o], hy_f_freq1[o], hy_f_w2[o],
                                               hy_f_b2[o], hy_f_freq2[o], hy_f_w3[o])
                cores.append(_hyena_core(zproj, hy_conv_w[o], hy_conv_b[o].reshape(1, HY_PROJ), fwd_bf, inv_bf,
                                         kspec, hy_bias[o], seq=seq, nbatch=nbatch, row_off=row_off, td=td))
            y = _out_proj_residual([tuple(cores)], y, ada_l, hy_w_out[o], 2)
        y = _moe(layer, y, ada_l, norm_ffn_g[layer], router_w[layer], router_b[layer],
                 moe_w1, b1p, moe_w2, b2r)
    y_p = y[0].reshape(BATCH, SEQ, D_MODEL)
    y_s = y[1].reshape(DEC_BATCH, DEC_SEQ, D_MODEL)
    return (y_p, y_s, jnp.stack(new_k, axis=1), jnp.stack(new_v, axis=1), jnp.stack(new_c, axis=1),
            jnp.stack(new_n, axis=1), jnp.stack(new_m, axis=1))
```

```python
import functools
import math

import numpy as np
import jax
import jax.numpy as jnp
from jax import lax
from jax.experimental import pallas as pl
from jax.experimental.pallas import tpu as pltpu

D_MODEL = 1024
BATCH = 16
SEQ = 256
DEPTH = 2
DEC_BATCH = 2
DEC_SEQ = 1024
PAST_LEN = 256
GRID_W = 64
W_A = D_MODEL // 2
HD_A = 64
H_A = W_A // (2 * HD_A)
W_B = D_MODEL - W_A
HD_B = 128
H_B = W_B // HD_B
AB_PROJ = 3 * W_A + 4 * W_B + 4 * H_B
ROPE_BASE = 10000.0
CHUNK = 64
HY_ORDER = 2
HY_PROJ = (HY_ORDER + 1) * D_MODEL
HY_BANDS = 8
HY_FH = 64
HY_TARGET = 1e-2
HY_FAST_PCT = 0.3
HY_SLOW_PCT = 1.5
N_EXPERTS = 32
TOP_K = 4
D_FF = D_MODEL
SWIGLU_ALPHA = 1.702
SWIGLU_LIMIT = 7.0
ADA_CHUNKS = 6
EPS = 1e-6
NEG = -1e30
F32 = jnp.float32
BF16 = jnp.bfloat16

T_P = BATCH * SEQ
T_S = DEC_BATCH * DEC_SEQ
T_ALL = T_P + T_S
ROW_TILE = 256
N_ROW_TILES = T_ALL // ROW_TILE
P_TILES = T_P // ROW_TILE
S_TILES_PER_BATCH = DEC_SEQ // ROW_TILE
COND_ROWS = 8
MOE_TILE = 256
N_ASSIGN = T_ALL * TOP_K
MOE_ROWS = N_ASSIGN + N_EXPERTS * MOE_TILE
MOE_TILES = MOE_ROWS // MOE_TILE
X_ALIGN = 16
GATHER_ROWS = 1024
X_ROWS = -(-(N_ASSIGN + N_EXPERTS * X_ALIGN + MOE_TILE) // GATHER_ROWS) * GATHER_ROWS
SPARE_ROW = T_ALL
ACC_ROWS = T_ALL + 8
SCATTER_GROUP = 8
W1_DMA_CHUNKS = 8
W2_DMA_CHUNKS = 4
LANES = 128
ROW_CHUNKS = D_MODEL // LANES
VMEM_LIMIT = 56 * 1024 * 1024
HIGHEST = lax.Precision.HIGHEST


def _cparams(n_axes):
    return pltpu.CompilerParams(dimension_semantics=("arbitrary",) * n_axes,
                                vmem_limit_bytes=VMEM_LIMIT)


def _bdot(a, b):
    return jnp.dot(a.astype(BF16), b.astype(BF16), preferred_element_type=F32)


def _cond_row(i):
    return jnp.where(i < P_TILES, 0, 1 + (i - P_TILES) // S_TILES_PER_BATCH)


def _ada_chunk(ada_ref, row, j):
    return ada_ref[pl.ds(row, 1), j * D_MODEL:(j + 1) * D_MODEL]


def _modulate(x, g, shift, scale):
    ms = jnp.mean(x * x, axis=-1, keepdims=True)
    return (x * lax.rsqrt(ms + EPS) * g) * (1.0 + scale) + shift


def _sigmoid(x):
    return 1.0 / (1.0 + jnp.exp(-x))


def _silu(x):
    return x * _sigmoid(x)


def _log_sigmoid(x):
    return jnp.minimum(x, 0.0) - jnp.log(1.0 + jnp.exp(-jnp.abs(x)))


def _dwconv3(x, w, b):
    n = x.shape[0]
    row = lax.broadcasted_iota(jnp.int32, x.shape, 0)
    prev = jnp.where(row == 0, 0.0, pltpu.roll(x, 1, 0))
    nxt = jnp.where(row == n - 1, 0.0, pltpu.roll(x, n - 1, 0))
    return prev * w[0:1] + x * w[1:2] + nxt * w[2:3] + b


def _ada_kernel(cond_ref, w_ref, b_ref, o_ref):
    c = _silu(cond_ref[...])
    o_ref[...] = _bdot(c, w_ref[...]) + b_ref[...]


def _ada_table(cond, ada_w, ada_b):
    tn = 1536
    return pl.pallas_call(
        _ada_kernel,
        grid=(DEPTH, ADA_CHUNKS * D_MODEL // tn),
        in_specs=[
            pl.BlockSpec((COND_ROWS, D_MODEL), lambda l, j: (0, 0)),
            pl.BlockSpec((None, D_MODEL, tn), lambda l, j: (l, 0, j)),
            pl.BlockSpec((None, 1, tn), lambda l, j: (l, 0, j)),
        ],
        out_specs=pl.BlockSpec((None, COND_ROWS, tn), lambda l, j: (l, 0, j)),
        out_shape=jax.ShapeDtypeStruct((DEPTH, COND_ROWS, ADA_CHUNKS * D_MODEL), F32),
        compiler_params=_cparams(2),
    )(cond, ada_w, ada_b.reshape(DEPTH, 1, ADA_CHUNKS * D_MODEL))


def _stream_specs(y):
    if isinstance(y, tuple):
        return [pl.BlockSpec((ROW_TILE, D_MODEL), lambda i: (jnp.minimum(i, P_TILES - 1), 0)),
                pl.BlockSpec((ROW_TILE, D_MODEL), lambda i: (jnp.maximum(i - P_TILES, 0), 0))], list(y)
    return [pl.BlockSpec((ROW_TILE, D_MODEL), lambda i: (i, 0))], [y]


def _stream_tile(y_refs, i):
    if len(y_refs) == 2:
        return jnp.where(i < P_TILES, y_refs[0][...], y_refs[1][...])
    return y_refs[0][...]


def _proj_kernel(splits, n_y, *refs):
    y_refs = refs[:n_y]
    ada_ref, g_ref, w_ref = refs[n_y:n_y + 3]
    out_refs, wbf_ref = refs[n_y + 3:-1], refs[-1]
    i = pl.program_id(0)

    @pl.when(i == 0)
    def _():
        wbf_ref[...] = w_ref[...].astype(BF16)

    row = _cond_row(i)
    h = _modulate(_stream_tile(y_refs, i), g_ref[...], _ada_chunk(ada_ref, row, 0), _ada_chunk(ada_ref, row, 1))
    h = h.astype(BF16)
    lo = 0
    for o_ref, width in zip(out_refs, splits):
        o_ref[...] = jnp.dot(h, wbf_ref[:, lo:lo + width], preferred_element_type=F32)
        lo += width


def _modulated_proj(y, ada_l, g, w, splits):
    n = w.shape[1]
    y_specs, y_args = _stream_specs(y)
    return pl.pallas_call(
        functools.partial(_proj_kernel, splits, len(y_args)),
        grid=(N_ROW_TILES,),
        in_specs=y_specs + [
            pl.BlockSpec((COND_ROWS, ADA_CHUNKS * D_MODEL), lambda i: (0, 0)),
            pl.BlockSpec((1, D_MODEL), lambda i: (0, 0)),
            pl.BlockSpec((D_MODEL, n), lambda i: (0, 0), pipeline_mode=pl.Buffered(1)),
        ],
        out_specs=[pl.BlockSpec((ROW_TILE, s), lambda i: (i, 0)) for s in splits],
        out_shape=[jax.ShapeDtypeStruct((T_ALL, s), F32) for s in splits],
        scratch_shapes=[pltpu.VMEM((D_MODEL, n), BF16)],
        compiler_params=_cparams(1),
    )(*y_args, ada_l, g.reshape(1, D_MODEL), w)


def _out_proj_kernel(n_in, n_y, gate_chunk, *refs):
    x_refs = refs[:2 * n_in]
    y_refs = refs[2 * n_in:2 * n_in + n_y]
    ada_ref, w_ref, o_ref, wbf_ref = refs[2 * n_in + n_y:]
    i = pl.program_id(0)

    @pl.when(i == 0)
    def _():
        wbf_ref[...] = w_ref[...].astype(BF16)

    acc = None
    lo = 0
    for xp_ref, xs_ref in zip(x_refs[0::2], x_refs[1::2]):
        k = xp_ref.shape[1]
        x = jnp.where(i < P_TILES, xp_ref[...], xs_ref[...])
        part = jnp.dot(x.astype(BF16), wbf_ref[lo:lo + k, :], preferred_element_type=F32)
        acc = part if acc is None else acc + part
        lo += k
    gate = _ada_chunk(ada_ref, _cond_row(i), gate_chunk)
    o_ref[...] = _stream_tile(y_refs, i) + gate * acc


def _out_proj_residual(xs, y, ada_l, w, gate_chunk):
    y_specs, y_args = _stream_specs(y)
    x_specs = []
    for xp, _ in xs:
        x_specs.append(pl.BlockSpec((ROW_TILE, xp.shape[1]), lambda i: (jnp.minimum(i, P_TILES - 1), 0)))
        x_specs.append(pl.BlockSpec((ROW_TILE, xp.shape[1]), lambda i: (jnp.maximum(i - P_TILES, 0), 0)))
    return pl.pallas_call(
        functools.partial(_out_proj_kernel, len(xs), len(y_args), gate_chunk),
        grid=(N_ROW_TILES,),
        in_specs=x_specs + y_specs + [
            pl.BlockSpec((COND_ROWS, ADA_CHUNKS * D_MODEL), lambda i: (0, 0)),
            pl.BlockSpec((D_MODEL, D_MODEL), lambda i: (0, 0), pipeline_mode=pl.Buffered(1)),
        ],
        out_specs=pl.BlockSpec((ROW_TILE, D_MODEL), lambda i: (i, 0)),
        out_shape=jax.ShapeDtypeStruct((T_ALL, D_MODEL), F32),
        scratch_shapes=[pltpu.VMEM((D_MODEL, D_MODEL), BF16)],
        compiler_params=_cparams(1),
    )(*[a for pair in xs for a in pair], *y_args, ada_l, w)


def _subhead_norm(x, g2):
    lane = lax.broadcasted_iota(jnp.int32, x.shape, 1)
    first = lane < HD_A
    xx = x * x
    s0 = jnp.sum(jnp.where(first, xx, 0.0), axis=-1, keepdims=True)
    s1 = jnp.sum(jnp.where(first, 0.0, xx), axis=-1, keepdims=True)
    r = jnp.where(first, lax.rsqrt(s0 / HD_A + EPS), lax.rsqrt(s1 / HD_A + EPS))
    return x * r * g2


def _rope(x, cos, sin):
    quarter = HD_A // 4
    lane = lax.broadcasted_iota(jnp.int32, x.shape, 1)
    lower = (lane % (2 * quarter)) < quarter
    swapped = jnp.where(lower, pltpu.roll(x, 2 * HD_A - quarter, 1), pltpu.roll(x, quarter, 1))
    return x * cos + swapped * sin


def _attn_kernel(lam_init, has_ctx, *refs):
    if has_ctx:
        (q_ref, k_ref, v_ref, ck_ref, cv_ref, cq_ref, sq_ref, ckk_ref, skk_ref,
         qg_ref, kg_ref, lp_ref, sg_ref, o_ref, kall_ref, vall_ref) = refs
    else:
        q_ref, k_ref, v_ref, qg_ref, kg_ref, lp_ref, sg_ref, o_ref, kn_ref, vh_ref = refs
    lp = lp_ref[...]
    lam = (jnp.exp(jnp.sum(lp[0:1] * lp[1:2], axis=-1, keepdims=True))
           - jnp.exp(jnp.sum(lp[2:3] * lp[3:4], axis=-1, keepdims=True)) + lam_init)

    def attend(q, k, v):
        probs = []
        for c in range(2):
            qc = q[:, c * HD_A:(c + 1) * HD_A].astype(BF16)
            kc = k[:, c * HD_A:(c + 1) * HD_A].astype(BF16)
            s = lax.dot_general(qc, kc, (((1,), (1,)), ((), ())), preferred_element_type=F32) * (HD_A ** -0.5)
            e = jnp.exp(s - jnp.max(s, axis=-1, keepdims=True))
            probs.append(e / jnp.sum(e, axis=-1, keepdims=True))
        o = _bdot(probs[0] - lam * probs[1], v)
        ms = jnp.mean(o * o, axis=-1, keepdims=True)
        return (o * lax.rsqrt(ms + EPS) * sg_ref[...]) * (1.0 - lam_init)

    if not has_ctx:
        for h in range(H_A):
            cols = slice(h * 2 * HD_A, (h + 1) * 2 * HD_A)
            k = _subhead_norm(k_ref[:, cols], kg_ref[...])
            for c in range(2):
                kn_ref[pl.ds(2 * h + c, SEQ, stride=2 * H_A), :] = k[:, c * HD_A:(c + 1) * HD_A]
            v = v_ref[:, cols]
            vh_ref[pl.ds(h, SEQ, stride=H_A), :] = v
            o_ref[:, cols] = attend(_subhead_norm(q_ref[:, cols], qg_ref[...]), k, v)
        return

    @pl.when(pl.program_id(2) == 0)
    def _():
        kall_ref[0:PAST_LEN, :] = ck_ref[...].astype(BF16)
        vall_ref[0:PAST_LEN, :] = cv_ref[...].astype(BF16)
        k_new = _rope(_subhead_norm(k_ref[...], kg_ref[...]), ckk_ref[...], skk_ref[...])
        kall_ref[PAST_LEN:, :] = k_new.astype(BF16)
        vall_ref[PAST_LEN:, :] = v_ref[...].astype(BF16)

    q = _rope(_subhead_norm(q_ref[...], qg_ref[...]), cq_ref[...], sq_ref[...])
    o_ref[...] = attend(q, kall_ref[...], vall_ref[...])


def _attention_prompt(qkv, qg2, kg2, lam_p, sub_g, lam_init):
    head = 2 * HD_A
    small = [
        pl.BlockSpec((1, head), lambda b: (0, 0)),
        pl.BlockSpec((1, head), lambda b: (0, 0)),
        pl.BlockSpec((4, HD_A), lambda b: (0, 0)),
        pl.BlockSpec((1, head), lambda b: (0, 0)),
    ]
    return pl.pallas_call(
        functools.partial(_attn_kernel, lam_init, False),
        grid=(BATCH,),
        in_specs=[
            pl.BlockSpec((SEQ, W_A), lambda b: (b, 0)),
            pl.BlockSpec((SEQ, W_A), lambda b: (b, 1)),
            pl.BlockSpec((SEQ, W_A), lambda b: (b, 2)),
        ] + small,
        out_specs=[pl.BlockSpec((SEQ, W_A), lambda b: (b, 0)),
                   pl.BlockSpec((SEQ * 2 * H_A, HD_A), lambda b: (b, 0)),
                   pl.BlockSpec((SEQ * H_A, head), lambda b: (b, 0))],
        out_shape=[jax.ShapeDtypeStruct((T_P, W_A), F32), jax.ShapeDtypeStruct((T_P * 2 * H_A, HD_A), F32),
                   jax.ShapeDtypeStruct((T_P * H_A, head), F32)],
        compiler_params=_cparams(1),
    )(qkv, qkv, qkv, qg2, kg2, lam_p, sub_g)


def _attention_sample(qkv, cache_k, cache_v, cos, sin, qg2, kg2, lam_p, sub_g, lam_init):
    nh = H_A
    head = 2 * HD_A
    tq = ROW_TILE
    nq = DEC_SEQ // tq
    q_off = T_P // tq
    k_off = T_P // DEC_SEQ
    small = [
        pl.BlockSpec((1, head), lambda b, h, i: (0, 0)),
        pl.BlockSpec((1, head), lambda b, h, i: (0, 0)),
        pl.BlockSpec((4, HD_A), lambda b, h, i: (0, 0)),
        pl.BlockSpec((1, head), lambda b, h, i: (0, 0)),
    ]
    return pl.pallas_call(
        functools.partial(_attn_kernel, lam_init, True),
        grid=(DEC_BATCH, nh, nq),
        in_specs=[
            pl.BlockSpec((tq, head), lambda b, h, i: (q_off + b * nq + i, h)),
            pl.BlockSpec((DEC_SEQ, head), lambda b, h, i: (k_off + b, nh + h)),
            pl.BlockSpec((DEC_SEQ, head), lambda b, h, i: (k_off + b, 2 * nh + h)),
            pl.BlockSpec((None, PAST_LEN, head), lambda b, h, i: (b, 0, h)),
            pl.BlockSpec((None, PAST_LEN, head), lambda b, h, i: (b, 0, h)),
            pl.BlockSpec((tq, head), lambda b, h, i: (i, 0)),
            pl.BlockSpec((tq, head), lambda b, h, i: (i, 0)),
            pl.BlockSpec((DEC_SEQ, head), lambda b, h, i: (0, 0)),
            pl.BlockSpec((DEC_SEQ, head), lambda b, h, i: (0, 0)),
        ] + small,
        out_specs=pl.BlockSpec((tq, head), lambda b, h, i: (b * nq + i, h)),
        out_shape=jax.ShapeDtypeStruct((T_S, W_A), F32),
        scratch_shapes=[pltpu.VMEM((PAST_LEN + DEC_SEQ, head), BF16)] * 2,
        compiler_params=_cparams(3),
    )(qkv, qkv, qkv, cache_k, cache_v, cos, sin, cos, sin, qg2, kg2, lam_p, sub_g)


def _rope_tables():
    half = HD_A // 2
    nf = half // 2
    inv = ROPE_BASE ** (-np.arange(nf, dtype=np.float32) / nf)
    pos = np.arange(DEC_SEQ)
    row = (pos // GRID_W).astype(np.float32)
    col = (pos % GRID_W).astype(np.float32)
    ang_r = (row[:, None] * inv).astype(np.float32)
    ang_c = (col[:, None] * inv).astype(np.float32)
    ang = np.concatenate([ang_r, ang_r, ang_c, ang_c], axis=1)
    sign = np.concatenate([-np.ones(nf), np.ones(nf), -np.ones(nf), np.ones(nf)]).astype(np.float32)
    cos = np.cos(ang.astype(np.float64)).astype(np.float32)
    sin = (np.sin(ang.astype(np.float64)) * sign).astype(np.float32)
    return jnp.asarray(np.tile(cos, (1, 2))), jnp.asarray(np.tile(sin, (1, 2)))


def _mlstm_kernel(seq, has_ctx, *refs):
    if has_ctx:
        (q_ref, k_ref, cwq_ref, cwk_ref, cbq_ref, cbk_ref, v_ref, mo_ref, gi_ref, gf_ref,
         gbi_ref, gbf_ref, hn_ref, c0_ref, n0_ref, m0_ref, o_ref,
         qs_ref, ks_ref, hf_ref, hb_ref, cs_ref, rrow_ref, col_ref, wc_ref) = refs
    else:
        (q_ref, k_ref, cwq_ref, cwk_ref, cbq_ref, cbk_ref, v_ref, mo_ref, gi_ref, gf_ref,
         gbi_ref, gbf_ref, hn_ref, o_ref, c_out_ref, n_out_ref, m_out_ref,
         qs_ref, ks_ref, hf_ref, hb_ref, cs_ref, rrow_ref, col_ref, wc_ref) = refs
    nc = seq // CHUNK
    n_chain = 2 * H_B
    chains = [(d, h) for d in range(2) for h in range(H_B)]
    qs_ref[...] = _silu(_dwconv3(q_ref[...], cwq_ref[...], cbq_ref[...])) * (HD_B ** -0.5)
    ks_ref[...] = _silu(_dwconv3(k_ref[...], cwk_ref[...], cbk_ref[...]))

    rows = nc * n_chain
    lane = lax.broadcasted_iota(jnp.int32, (rows, 2 * CHUNK), 1)
    forward = lax.broadcasted_iota(jnp.int32, (rows, 2 * CHUNK), 0) % n_chain < H_B
    valid = lane < CHUNK

    def scan(x, op, fill):
        pre, suf = x, x
        sh = 1
        while sh < CHUNK:
            pre = op(pre, jnp.where(lane >= sh, pltpu.roll(pre, sh, 1), fill))
            suf = op(suf, jnp.where(lane + sh < CHUNK, pltpu.roll(suf, 2 * CHUNK - sh, 1), fill))
            sh *= 2
        return jnp.where(forward, pre, suf)

    gate_i = (gi_ref[...] + gbi_ref[...]).reshape(rows, 2 * CHUNK)
    lf = jnp.where(valid, _log_sigmoid(gf_ref[...] + gbf_ref[...]).reshape(rows, 2 * CHUNK), 0.0)
    b = scan(lf, jnp.add, 0.0)
    cmax = scan(jnp.where(valid, gate_i - b, -jnp.inf), jnp.maximum, -jnp.inf)
    b_last = jnp.sum(lf, axis=1, keepdims=True)
    g = b_last - b + gate_i
    g_max = jnp.max(jnp.where(valid, g, -jnp.inf), axis=1, keepdims=True)
    mm = m0_ref[...] if has_ctx else jnp.zeros((n_chain, 1), F32)
    mm_seq = []
    for p in range(nc):
        mm_seq.append(mm)
        seg = slice(p * n_chain, (p + 1) * n_chain)
        mm = jnp.maximum(b_last[seg] + mm, g_max[seg])
    mm_final = mm
    mm_prev = jnp.concatenate(mm_seq, axis=0)
    mm_next = jnp.concatenate(mm_seq[1:] + [mm_final], axis=0)
    m_t = jnp.maximum(b + mm_prev, b + cmax)
    rrow_ref[...] = (b - gate_i).reshape(nc, n_chain, 2 * CHUNK)
    wc_ref[...] = jnp.exp(b_last + mm_prev - mm_next).reshape(nc, n_chain, 1)
    per_row = [b, m_t, jnp.exp(b + mm_prev - m_t), jnp.exp(-m_t), jnp.exp(g - mm_next)]
    for j, arr in enumerate(per_row):
        by_time = arr.T
        for p in range(nc):
            col_ref[p, :, j * n_chain:(j + 1) * n_chain] = by_time[0:CHUNK, p * n_chain:(p + 1) * n_chain]

    t_idx = lax.broadcasted_iota(jnp.int32, (CHUNK, CHUNK), 0)
    s_idx = lax.broadcasted_iota(jnp.int32, (CHUNK, CHUNK), 1)
    for n, (d, h) in enumerate(chains):
        cs_ref[n] = c0_ref[d, h] if has_ctx else jnp.zeros((HD_B, HD_B), F32)

    def out_step(p, n_states):
        cols = col_ref[p]
        rrows = rrow_ref[p]
        wcs = wc_ref[p]
        new_states = []
        for n, (d, h) in enumerate(chains):
            c = p if d == 0 else nc - 1 - p
            r0 = pl.multiple_of(c * CHUNK, CHUNK)
            hcols = slice(h * HD_B, (h + 1) * HD_B)
            qt = qs_ref[pl.ds(r0, CHUNK), hcols]
            kt = ks_ref[pl.ds(r0, CHUNK), hcols]
            vt = v_ref[pl.ds(r0, CHUNK), hcols]
            b_col, m_t, w_inter, e_inv, w_k = (cols[:, j * n_chain + n:j * n_chain + n + 1] for j in range(5))
            mask = (s_idx <= t_idx) if d == 0 else (s_idx >= t_idx)
            decay = jnp.exp(jnp.where(mask, b_col - rrows[n:n + 1, 0:CHUNK], NEG) - m_t)
            qk = lax.dot_general(qt.astype(BF16), kt.astype(BF16), (((1,), (1,)), ((), ())),
                                 preferred_element_type=F32)
            s = qk * decay
            cm = cs_ref[n]
            nm = n_states[n]
            cq = lax.dot_general(qt.astype(BF16), cm.astype(BF16), (((1,), (1,)), ((), ())),
                                 preferred_element_type=F32)
            num = _bdot(s, vt) + w_inter * cq
            nq = jnp.sum(s, axis=-1, keepdims=True) + w_inter * jnp.sum(qt * nm, axis=-1, keepdims=True)
            hdir_ref = hf_ref if d == 0 else hb_ref
            hdir_ref[pl.ds(r0, CHUNK), hcols] = num / jnp.maximum(jnp.abs(nq), e_inv)
            w_c = wcs[n:n + 1, :]
            vw = (vt * w_k).astype(BF16)
            cs_ref[n] = w_c * cm + lax.dot_general(vw, kt.astype(BF16), (((0,), (0,)), ((), ())),
                                                   preferred_element_type=F32)
            new_states.append(w_c * nm + jnp.sum(kt * w_k, axis=0, keepdims=True))
        return tuple(new_states)

    if has_ctx:
        n_init = tuple(n0_ref[d, h] for d, h in chains)
    else:
        n_init = tuple(jnp.zeros((1, HD_B), F32) for _ in chains)
    n_final = lax.fori_loop(0, nc, out_step, n_init)
    if not has_ctx:
        for n, (d, h) in enumerate(chains):
            c_out_ref[d, h] = cs_ref[n]
            n_out_ref[d, h] = n_final[n]
            m_out_ref[d, h] = jnp.broadcast_to(mm_final[n:n + 1, :], (1, HD_B))

    for h in range(H_B):
        hcols = slice(h * HD_B, (h + 1) * HD_B)
        hh = hf_ref[:, hcols] + hb_ref[:, hcols]
        ms = jnp.mean(hh * hh, axis=-1, keepdims=True)
        o_ref[:, hcols] = (hh * lax.rsqrt(ms + EPS) * hn_ref[:, hcols]) * _sigmoid(mo_ref[:, hcols])


def _mlstm(mqk, mv, mo, mg_stream, conv_w, conv_b, gate_b, hn_g, *, seq, nbatch, row_off, ctx=None):
    nh = H_B
    nc = seq // CHUNK
    has_ctx = ctx is not None
    gt = mg_stream.reshape(nbatch, nc, CHUNK, 2, 2, nh).transpose(0, 1, 3, 4, 5, 2)
    pad = ((0, 0), (0, 0), (0, 0), (0, CHUNK))
    gates = [jnp.pad(jnp.concatenate([gt[:, :, 0, j], gt[:, ::-1, 1, j]], axis=2), pad) for j in range(2)]
    gate_bias = [jnp.concatenate([gate_b[0, j], gate_b[1, j]]).reshape(2 * nh, 1) for j in range(2)]
    blk = lambda col: pl.BlockSpec((seq, W_B), lambda b, col=col: (row_off + b, col))
    gate_blk = pl.BlockSpec((None, nc, 2 * nh, 2 * CHUNK), lambda b: (b, 0, 0, 0))
    in_specs = [
        blk(0), blk(1),
        pl.BlockSpec((3, W_B), lambda b: (0, 0)),
        pl.BlockSpec((3, W_B), lambda b: (0, 1)),
        pl.BlockSpec((1, W_B), lambda b: (0, 0)),
        pl.BlockSpec((1, W_B), lambda b: (0, 1)),
        blk(0), blk(0),
        gate_blk, gate_blk,
        pl.BlockSpec((2 * nh, 1), lambda b: (0, 0)),
        pl.BlockSpec((2 * nh, 1), lambda b: (0, 0)),
        pl.BlockSpec((1, W_B), lambda b: (0, 0)),
    ]
    args = [mqk, mqk, conv_w, conv_w, conv_b, conv_b, mv, mo, gates[0], gates[1],
            gate_bias[0], gate_bias[1], hn_g.reshape(1, W_B)]
    o_spec = pl.BlockSpec((seq, W_B), lambda b: (b, 0))
    o_shape = jax.ShapeDtypeStruct((nbatch * seq, W_B), F32)
    state_blk = lambda rows: pl.BlockSpec((None, 2, nh, rows, HD_B), lambda b: (b, 0, 0, 0, 0))
    if has_ctx:
        c0, n0, m0 = ctx
        in_specs += [state_blk(HD_B), state_blk(1), pl.BlockSpec((None, 2 * nh, 1), lambda b: (b, 0, 0))]
        args += [c0, n0.reshape(nbatch, 2, nh, 1, HD_B), m0.reshape(nbatch, 2 * nh, 1)]
        out_specs, out_shape = o_spec, o_shape
    else:
        out_specs = [o_spec, state_blk(HD_B), state_blk(1), state_blk(1)]
        out_shape = [
            o_shape,
            jax.ShapeDtypeStruct((nbatch, 2, nh, HD_B, HD_B), F32),
            jax.ShapeDtypeStruct((nbatch, 2, nh, 1, HD_B), F32),
            jax.ShapeDtypeStruct((nbatch, 2, nh, 1, HD_B), F32),
        ]
    return pl.pallas_call(
        functools.partial(_mlstm_kernel, seq, has_ctx),
        grid=(nbatch,),
        in_specs=in_specs,
        out_specs=out_specs,
        out_shape=out_shape,
        scratch_shapes=[pltpu.VMEM((seq, W_B), F32)] * 4 + [
            pltpu.VMEM((2 * nh, HD_B, HD_B), F32),
            pltpu.VMEM((nc, 2 * nh, 2 * CHUNK), F32),
            pltpu.VMEM((nc, CHUNK, 5 * 2 * nh), F32),
            pltpu.VMEM((nc, 2 * nh, 1), F32),
        ],
        compiler_params=_cparams(1),
    )(*args)


def _dft_mats(L):
    f = np.arange(L)[:, None]
    j = np.arange(L)[None, :]
    ang = 2.0 * np.pi * ((f * j) % (2 * L)) / (2 * L)
    cm = np.cos(ang)
    sm = np.sin(ang)
    alt = (1.0 - 2.0 * (np.arange(L) % 2))
    fwd_b = -sm
    fwd_b[0, :] = alt
    fwd = np.concatenate([cm, fwd_b], axis=0)
    wgt = np.where(np.arange(L) == 0, 1.0, 2.0)[None, :]
    inv_a = cm.T * wgt
    inv_b = -2.0 * sm.T
    inv_b[:, 0] = alt
    inv = np.concatenate([inv_a, inv_b], axis=1) / (2 * L)
    return jnp.asarray(fwd.astype(np.float32)), jnp.asarray(inv.astype(np.float32))


def _hyena_feats(L):
    t = np.linspace(0.0, 1.0, L, dtype=np.float32)
    wpos = (2.0 * math.pi * np.arange(L, dtype=np.float32) / L).astype(np.float32)
    fb = np.linspace(1e-4, HY_BANDS - 1, HY_BANDS, dtype=np.float32)
    z = (wpos[:, None] * fb).astype(np.float32)
    feats = np.concatenate([t[:, None], np.cos(z), -np.sin(z)], axis=-1).astype(np.float32)
    deltas = np.abs(np.linspace(math.log(HY_TARGET) / HY_SLOW_PCT, math.log(HY_TARGET) / HY_FAST_PCT,
                                D_MODEL, dtype=np.float32))
    decay = np.exp(-t[:, None] * deltas).astype(np.float32)
    return jnp.asarray(feats), jnp.asarray(decay)


def _filter_kernel(L, feats_ref, w1_ref, b1_ref, fr1_ref, w2_ref, b2_ref, fr2_ref, w3f_ref, w3b_ref,
                   decay_ref, fwd_ref, o_ref, hdn_ref):
    @pl.when((pl.program_id(0) == 0) & (pl.program_id(1) == 0))
    def _():
        h1 = jnp.sin(fr1_ref[...] * (jnp.dot(feats_ref[...], w1_ref[...], precision=HIGHEST,
                                             preferred_element_type=F32) + b1_ref[...]))
        hdn_ref[...] = jnp.sin(fr2_ref[...] * (jnp.dot(h1, w2_ref[...], precision=HIGHEST,
                                                       preferred_element_type=F32) + b2_ref[...]))

    hdn = hdn_ref[...]
    decay = decay_ref[...]
    f_fwd = jnp.dot(hdn, w3f_ref[...], precision=HIGHEST, preferred_element_type=F32) * decay
    f_bwd = jnp.dot(hdn, w3b_ref[...], precision=HIGHEST, preferred_element_type=F32) * decay
    row = lax.broadcasted_iota(jnp.int32, f_bwd.shape, 0)
    f_bwd = jnp.where(row == 0, 0.0, f_bwd)
    fwd = fwd_ref[...]
    p = _bdot(fwd, f_fwd)
    q = _bdot(fwd, f_bwd)
    first = row == 0
    o_ref[0:L, :] = p[0:L] + q[0:L]
    o_ref[L:2 * L, :] = p[L:2 * L] + jnp.where(first, q[L:2 * L], -q[L:2 * L])


def _hyena_filter_spectrum(L, fwd_bf, w1, b1, fr1, w2, b2, fr2, w3):
    feats, decay = _hyena_feats(L)
    td = 512
    nd = D_MODEL // td
    emb = feats.shape[1]
    vec = lambda a: a.reshape(1, HY_FH)
    full = lambda shape: pl.BlockSpec(shape, lambda o, j: (0, 0))
    return pl.pallas_call(
        functools.partial(_filter_kernel, L),
        grid=(HY_ORDER, nd),
        in_specs=[
            full((L, emb)), full((emb, HY_FH)), full((1, HY_FH)), full((1, HY_FH)),
            full((HY_FH, HY_FH)), full((1, HY_FH)), full((1, HY_FH)),
            pl.BlockSpec((HY_FH, td), lambda o, j: (0, o * 2 * nd + j)),
            pl.BlockSpec((HY_FH, td), lambda o, j: (0, o * 2 * nd + nd + j)),
            pl.BlockSpec((L, td), lambda o, j: (0, j)),
            full((2 * L, L)),
        ],
        out_specs=pl.BlockSpec((2 * L, td), lambda o, j: (0, o * nd + j)),
        out_shape=jax.ShapeDtypeStruct((2 * L, HY_ORDER * D_MODEL), F32),
        scratch_shapes=[pltpu.VMEM((L, HY_FH), F32)],
        compiler_params=_cparams(2),
    )(feats, w1, vec(b1), vec(fr1), w2, vec(b2), vec(fr2), w3, w3, decay, fwd_bf)


def _spectral_conv(u, fwd, inv, kspec, L):
    uf = jnp.dot(fwd, u.astype(BF16), preferred_element_type=F32)
    ua, ub = uf[0:L], uf[L:2 * L]
    ka, kb = kspec[0:L], kspec[L:2 * L]
    first = lax.broadcasted_iota(jnp.int32, ua.shape, 0) == 0
    ya = ua * ka - jnp.where(first, 0.0, ub * kb)
    yb = jnp.where(first, ub * kb, ua * kb + ub * ka)
    y = jnp.concatenate([ya, yb], axis=0).astype(BF16)
    return jnp.dot(inv, y, preferred_element_type=F32)


def _hyena_kernel(L, zv_ref, z1_ref, z2_ref, cwv_ref, cw1_ref, cw2_ref, cbv_ref, cb1_ref, cb2_ref,
                  fwd_ref, inv_ref, k0_ref, k1_ref, bias0_ref, bias1_ref, o_ref):
    fwd = fwd_ref[...]
    inv = inv_ref[...]
    v = _dwconv3(zv_ref[...], cwv_ref[...], cbv_ref[...])
    x1 = _dwconv3(z1_ref[...], cw1_ref[...], cb1_ref[...])
    x2 = _dwconv3(z2_ref[...], cw2_ref[...], cb2_ref[...])
    z = x1 * (_spectral_conv(v, fwd, inv, k0_ref[...], L) + v * bias0_ref[...])
    o_ref[...] = x2 * (_spectral_conv(z, fwd, inv, k1_ref[...], L) + z * bias1_ref[...])


def _hyena_core(zproj, conv_w, conv_b, fwd_bf, inv_bf, kspec, bias, *, seq, nbatch, row_off, td):
    nd = D_MODEL // td
    zblk = lambda part: pl.BlockSpec((seq, td), lambda b, j, part=part: (row_off + b, part * nd + j))
    cwblk = lambda part: pl.BlockSpec((3, td), lambda b, j, part=part: (0, part * nd + j))
    cbblk = lambda part: pl.BlockSpec((1, td), lambda b, j, part=part: (0, part * nd + j))
    return pl.pallas_call(
        functools.partial(_hyena_kernel, seq),
        grid=(nbatch, nd),
        in_specs=[
            zblk(0), zblk(1), zblk(2), cwblk(0), cwblk(1), cwblk(2), cbblk(0), cbblk(1), cbblk(2),
            pl.BlockSpec((2 * seq, seq), lambda b, j: (0, 0), pipeline_mode=pl.Buffered(1)),
            pl.BlockSpec((seq, 2 * seq), lambda b, j: (0, 0), pipeline_mode=pl.Buffered(1)),
            pl.BlockSpec((2 * seq, td), lambda b, j: (0, j)),
            pl.BlockSpec((2 * seq, td), lambda b, j: (0, nd + j)),
            pl.BlockSpec((None, 1, td), lambda b, j: (0, 0, j)),
            pl.BlockSpec((None, 1, td), lambda b, j: (1, 0, j)),
        ],
        out_specs=pl.BlockSpec((seq, td), lambda b, j: (b, j)),
        out_shape=jax.ShapeDtypeStruct((nbatch * seq, D_MODEL), F32),
        compiler_params=_cparams(2),
    )(zproj, zproj, zproj, conv_w, conv_w, conv_w, conv_b, conv_b, conv_b,
      fwd_bf, inv_bf, kspec, kspec, bias.reshape(HY_ORDER, 1, D_MODEL), bias.reshape(HY_ORDER, 1, D_MODEL))


def _router_kernel(x_ref, ada_ref, g_ref, rw_ref, rb_ref, h_ref, idx_ref, wt_ref):
    i = pl.program_id(0)
    row = _cond_row(i)
    h = _modulate(x_ref[...], g_ref[...], _ada_chunk(ada_ref, row, 3), _ada_chunk(ada_ref, row, 4))
    for j in range(ROW_CHUNKS):
        h_ref[pl.ds(j, ROW_TILE, stride=ROW_CHUNKS), :] = h[:, j * LANES:(j + 1) * LANES]
    logits = lax.dot_general(rw_ref[...], h, (((1,), (1,)), ((), ())), precision=HIGHEST,
                             preferred_element_type=F32) + rb_ref[...]
    expert = lax.broadcasted_iota(jnp.int32, logits.shape, 0)
    slot = lax.broadcasted_iota(jnp.int32, (TOP_K, logits.shape[1]), 0)
    vals = jnp.zeros((TOP_K, logits.shape[1]), F32)
    idxs = jnp.zeros((TOP_K, logits.shape[1]), jnp.int32)
    cur = logits
    for k in range(TOP_K):
        m = jnp.max(cur, axis=0, keepdims=True)
        a = jnp.min(jnp.where(cur == m, expert, N_EXPERTS), axis=0, keepdims=True)
        vals = jnp.where(slot == k, m, vals)
        idxs = jnp.where(slot == k, a, idxs)
        cur = jnp.where(expert == a, -jnp.inf, cur)
    e = jnp.exp(vals - vals[0:1])
    wt_ref[...] = e / jnp.sum(e, axis=0, keepdims=True)
    idx_ref[...] = idxs


def _router(y, ada_l, g, router_w, router_b):
    return pl.pallas_call(
        _router_kernel,
        grid=(N_ROW_TILES,),
        in_specs=[
            pl.BlockSpec((ROW_TILE, D_MODEL), lambda i: (i, 0)),
            pl.BlockSpec((COND_ROWS, ADA_CHUNKS * D_MODEL), lambda i: (0, 0)),
            pl.BlockSpec((1, D_MODEL), lambda i: (0, 0)),
            pl.BlockSpec((N_EXPERTS, D_MODEL), lambda i: (0, 0)),
            pl.BlockSpec((N_EXPERTS, 1), lambda i: (0, 0)),
        ],
        out_specs=[
            pl.BlockSpec((ROW_TILE * ROW_CHUNKS, LANES), lambda i: (i, 0)),
            pl.BlockSpec((TOP_K, ROW_TILE), lambda i: (0, i)),
            pl.BlockSpec((TOP_K, ROW_TILE), lambda i: (0, i)),
        ],
        out_shape=[
            jax.ShapeDtypeStruct((T_ALL * ROW_CHUNKS, LANES), F32),
            jax.ShapeDtypeStruct((TOP_K, T_ALL), jnp.int32),
            jax.ShapeDtypeStruct((TOP_K, T_ALL), F32),
        ],
        compiler_params=_cparams(1),
    )(y, ada_l, g.reshape(1, D_MODEL), router_w.T, router_b.reshape(N_EXPERTS, 1))


def _dispatch_kernel(rows_ref, h_ref, o_ref):
    base = pl.program_id(0) * GATHER_ROWS
    for r in range(GATHER_ROWS):
        t = pl.multiple_of(rows_ref[base + r] * ROW_CHUNKS, ROW_CHUNKS)
        o_ref[r * ROW_CHUNKS:(r + 1) * ROW_CHUNKS, :] = h_ref[pl.ds(t, ROW_CHUNKS), :]


def _dispatch(h_tiles, gather_row):
    grid_spec = pltpu.PrefetchScalarGridSpec(
        num_scalar_prefetch=1,
        grid=(X_ROWS // GATHER_ROWS,),
        in_specs=[pl.BlockSpec((T_ALL * ROW_CHUNKS, LANES), lambda i, rows: (0, 0), pipeline_mode=pl.Buffered(1))],
        out_specs=pl.BlockSpec((GATHER_ROWS * ROW_CHUNKS, LANES), lambda i, rows: (i, 0)),
    )
    return pl.pallas_call(
        _dispatch_kernel,
        grid_spec=grid_spec,
        out_shape=jax.ShapeDtypeStruct((X_ROWS * ROW_CHUNKS, LANES), F32),
        compiler_params=_cparams(1),
    )(gather_row, h_tiles)


def _deinterleave_matrix():
    s = np.zeros((256, 256), np.float32)
    j = np.arange(128)
    s[2 * j, j] = 1.0
    s[2 * j + 1, 128 + j] = 1.0
    return jnp.asarray(s)


def _weight_copies(layer, e, w1_hbm, w2_hbm, w1s_ref, w2s_ref, sem):
    copies = []
    r1 = D_MODEL // W1_DMA_CHUNKS
    for c in range(W1_DMA_CHUNKS):
        copies.append(pltpu.make_async_copy(w1_hbm.at[layer, e, pl.ds(c * r1, r1)],
                                            w1s_ref.at[pl.ds(c * r1, r1)], sem.at[c]))
    r2 = D_FF // W2_DMA_CHUNKS
    for c in range(W2_DMA_CHUNKS):
        copies.append(pltpu.make_async_copy(w2_hbm.at[layer, e, pl.ds(c * r2, r2)],
                                            w2s_ref.at[pl.ds(c * r2, r2)], sem.at[W1_DMA_CHUNKS + c]))
    return copies


def _expert_kernel(layer, te_ref, tf_ref, ne_ref, nu_ref, src_ref, xo_ref, x_ref, b1_ref, b2_ref, wt_ref, s_ref,
                   w1_hbm, w2_hbm, o_hbm, w1s_ref, w2s_ref, w1p_ref, w2p_ref, acc_ref, out_ref, wsem, osem):
    i = pl.program_id(0)
    half = 128
    copies = functools.partial(_weight_copies, layer, w1_hbm=w1_hbm, w2_hbm=w2_hbm,
                               w1s_ref=w1s_ref, w2s_ref=w2s_ref, sem=wsem)

    @pl.when(i == 0)
    def _():
        acc_ref[...] = jnp.zeros_like(acc_ref)
        out_ref[...] = jnp.zeros_like(out_ref)
        for cp in copies(te_ref[0]):
            cp.start()

    @pl.when(tf_ref[i] == 1)
    def _():
        for cp in copies(te_ref[i]):
            cp.wait()
        s = s_ref[...].astype(BF16)
        for c in range(2 * D_FF // 256):
            blk = jnp.dot(w1s_ref[:, c * 256:(c + 1) * 256].astype(BF16), s, preferred_element_type=F32)
            w1p_ref[:, c * half:(c + 1) * half] = blk[:, :half].astype(BF16)
            w1p_ref[:, D_FF + c * half:D_FF + (c + 1) * half] = blk[:, half:].astype(BF16)
        w2p_ref[...] = w2s_ref[...].astype(BF16)

        @pl.when(ne_ref[i] >= 0)
        def _():
            for cp in copies(ne_ref[i]):
                cp.start()

    @pl.when(i <= nu_ref[0])
    def _():
        base = i * MOE_TILE
        prev = (i + 1) % 2
        for r0 in range(0, MOE_TILE, SCATTER_GROUP):
            toks = [pl.multiple_of(src_ref[base + r0 + g] * ROW_CHUNKS, ROW_CHUNKS) for g in range(SCATTER_GROUP)]
            cur = [acc_ref[pl.ds(toks[g], ROW_CHUNKS), :] for g in range(SCATTER_GROUP)]
            add = [out_ref[prev, (r0 + g) * ROW_CHUNKS:(r0 + g + 1) * ROW_CHUNKS, :] for g in range(SCATTER_GROUP)]
            for g in range(SCATTER_GROUP):
                acc_ref[pl.ds(toks[g], ROW_CHUNKS), :] = cur[g] + add[g]
        x = jnp.concatenate([x_ref[pl.ds(j, MOE_TILE, stride=ROW_CHUNKS), :] for j in range(ROW_CHUNKS)], axis=1)
        a = jnp.dot(x.astype(BF16), w1p_ref[...], preferred_element_type=F32) + b1_ref[...]
        glu = jnp.minimum(a[:, :D_FF], SWIGLU_LIMIT)
        lin = jnp.clip(a[:, D_FF:], -SWIGLU_LIMIT, SWIGLU_LIMIT)
        hid = glu * _sigmoid(SWIGLU_ALPHA * glu) * (lin + 1.0)
        out = (jnp.dot(hid.astype(BF16), w2p_ref[...], preferred_element_type=F32) + b2_ref[...]) * wt_ref[...]
        cur_buf = i % 2
        for j in range(ROW_CHUNKS):
            out_ref[cur_buf, pl.ds(j, MOE_TILE, stride=ROW_CHUNKS), :] = out[:, j * LANES:(j + 1) * LANES]

    @pl.when(i == pl.num_programs(0) - 1)
    def _():
        cp = pltpu.make_async_copy(acc_ref.at[pl.ds(0, T_ALL * ROW_CHUNKS)], o_hbm, osem)
        cp.start()
        cp.wait()


def _experts(layer, x_sorted, w_sorted, plan, w1, b1p, w2, b2):
    tile_expert, tile_first, next_expert, n_used, src, x_off = plan
    grid_spec = pltpu.PrefetchScalarGridSpec(
        num_scalar_prefetch=6,
        grid=(MOE_TILES,),
        in_specs=[
            pl.BlockSpec((pl.Element(MOE_TILE * ROW_CHUNKS), pl.Element(LANES)),
                         lambda i, te, tf, ne, nu, src, xo: (pl.multiple_of(xo[i] * ROW_CHUNKS, X_ALIGN * ROW_CHUNKS), 0)),
            pl.BlockSpec((None, None, 1, 2 * D_FF), lambda i, te, *_: (layer, te[i], 0, 0)),
            pl.BlockSpec((None, None, 1, D_MODEL), lambda i, te, *_: (layer, te[i], 0, 0)),
            pl.BlockSpec((MOE_TILE, 1), lambda i, te, *_: (i, 0)),
            pl.BlockSpec((256, 256), lambda i, te, *_: (0, 0)),
            pl.BlockSpec(memory_space=pl.ANY),
            pl.BlockSpec(memory_space=pl.ANY),
        ],
        out_specs=pl.BlockSpec(memory_space=pl.ANY),
        scratch_shapes=[
            pltpu.VMEM((D_MODEL, 2 * D_FF), F32),
            pltpu.VMEM((D_FF, D_MODEL), F32),
            pltpu.VMEM((D_MODEL, 2 * D_FF), BF16),
            pltpu.VMEM((D_FF, D_MODEL), BF16),
            pltpu.VMEM((ACC_ROWS * ROW_CHUNKS, LANES), F32),
            pltpu.VMEM((2, MOE_TILE * ROW_CHUNKS, LANES), F32),
            pltpu.SemaphoreType.DMA((W1_DMA_CHUNKS + W2_DMA_CHUNKS,)),
            pltpu.SemaphoreType.DMA(()),
        ],
    )
    return pl.pallas_call(
        functools.partial(_expert_kernel, layer),
        grid_spec=grid_spec,
        out_shape=jax.ShapeDtypeStruct((T_ALL * ROW_CHUNKS, LANES), F32),
        compiler_params=_cparams(1),
    )(tile_expert, tile_first, next_expert, n_used, src, x_off, x_sorted, b1p, b2, w_sorted,
      _deinterleave_matrix(), w1, w2)


def _combine_kernel(first_tile, y_ref, a_ref, ada_ref, o_ref):
    gate = _ada_chunk(ada_ref, _cond_row(first_tile + pl.program_id(0)), 5)
    acc = jnp.concatenate([a_ref[pl.ds(j, ROW_TILE, stride=ROW_CHUNKS), :] for j in range(ROW_CHUNKS)], axis=1)
    o_ref[...] = y_ref[...] + gate * acc


def _combine(y, acc, ada_l, first_tile=0, n_tiles=N_ROW_TILES):
    return pl.pallas_call(
        functools.partial(_combine_kernel, first_tile),
        grid=(n_tiles,),
        in_specs=[
            pl.BlockSpec((ROW_TILE, D_MODEL), lambda i: (first_tile + i, 0)),
            pl.BlockSpec((ROW_TILE * ROW_CHUNKS, LANES), lambda i: (first_tile + i, 0)),
            pl.BlockSpec((COND_ROWS, ADA_CHUNKS * D_MODEL), lambda i: (0, 0)),
        ],
        out_specs=pl.BlockSpec((ROW_TILE, D_MODEL), lambda i: (i, 0)),
        out_shape=jax.ShapeDtypeStruct((n_tiles * ROW_TILE, D_MODEL), F32),
        compiler_params=_cparams(1),
    )(y, acc, ada_l)


def _routing_plan(idx, wts):
    eid = idx.reshape(-1)
    order = jnp.argsort(eid, stable=True).astype(jnp.int32)
    experts = jnp.arange(N_EXPERTS, dtype=jnp.int32)
    counts = jnp.sum(eid[:, None] == experts[None, :], axis=0).astype(jnp.int32)
    ntiles = (counts + MOE_TILE - 1) // MOE_TILE
    tile_end = jnp.cumsum(ntiles).astype(jnp.int32)
    tile_begin = tile_end - ntiles
    cstarts = (jnp.cumsum(counts) - counts).astype(jnp.int32)
    n_used = tile_end[-1]
    tile = jnp.arange(MOE_TILES, dtype=jnp.int32)
    te = jnp.minimum(jnp.sum(tile[:, None] >= tile_end[None, :], axis=1), N_EXPERTS - 1).astype(jnp.int32)
    used = tile < n_used
    prev = jnp.concatenate([jnp.full((1,), -1, jnp.int32), te[:-1]])
    first = (te != prev) & used

    def pick(onehot, table):
        return jnp.sum(jnp.where(onehot, table[None, :], 0), axis=1).astype(jnp.int32)

    tile_is = te[:, None] == experts[None, :]
    later = (experts[None, :] > experts[:, None]) & (ntiles[None, :] > 0)
    following = jnp.min(jnp.where(later, experts[None, :], N_EXPERTS), axis=1)
    following = jnp.where(following < N_EXPERTS, following, -1)
    next_expert = jnp.where(first, pick(tile_is, following), -1).astype(jnp.int32)
    tile_in_expert = tile - pick(tile_is, tile_begin)
    off = tile_in_expert[:, None] * MOE_TILE + jnp.arange(MOE_TILE, dtype=jnp.int32)[None, :]
    valid = (off < pick(tile_is, counts)[:, None]) & used[:, None]
    assign = order[jnp.clip(pick(tile_is, cstarts)[:, None] + off, 0, N_ASSIGN - 1)]
    token = assign // TOP_K
    src = jnp.where(valid, token, SPARE_ROW).reshape(MOE_ROWS).astype(jnp.int32)
    src = jnp.concatenate([jnp.full((MOE_TILE,), SPARE_ROW, jnp.int32), src])
    w_sorted = jnp.where(valid, wts.reshape(-1)[assign], 0.0).reshape(MOE_ROWS, 1)
    seg = ((counts + X_ALIGN - 1) // X_ALIGN) * X_ALIGN
    seg_end = jnp.cumsum(seg).astype(jnp.int32)
    seg_begin = seg_end - seg
    x_off = jnp.where(used, pick(tile_is, seg_begin) + tile_in_expert * MOE_TILE, 0).astype(jnp.int32)
    group = jnp.arange(X_ROWS // X_ALIGN, dtype=jnp.int32) * X_ALIGN
    group_is = (group[:, None] >= seg_begin[None, :]) & (group[:, None] < seg_end[None, :])
    xoffset = (group - pick(group_is, seg_begin))[:, None] + jnp.arange(X_ALIGN, dtype=jnp.int32)[None, :]
    xassign = order[jnp.clip(pick(group_is, cstarts)[:, None] + xoffset, 0, N_ASSIGN - 1)]
    gather_row = jnp.where(xoffset < pick(group_is, counts)[:, None], xassign // TOP_K, 0).reshape(X_ROWS)
    plan = (te, first.astype(jnp.int32), next_expert, n_used.reshape(1), src, x_off)
    return plan, gather_row, w_sorted


def _moe(layer, y, ada_l, g, router_w, router_b, w1, b1p, w2, b2):
    h, idx_t, wts_t = _router(y, ada_l, g, router_w, router_b)
    plan, gather_row, w_sorted = _routing_plan(idx_t.T, wts_t.T)
    x_sorted = _dispatch(h, gather_row)
    acc = _experts(layer, x_sorted, w_sorted, plan, w1, b1p, w2, b2)
    if layer == DEPTH - 1:
        return (_combine(y, acc, ada_l, 0, P_TILES), _combine(y, acc, ada_l, P_TILES, N_ROW_TILES - P_TILES))
    return _combine(y, acc, ada_l)


def kernel(x_prompt, x_sample, cache_attn_k, cache_attn_v, state_mlstm_C, state_mlstm_n, state_mlstm_m, c, c_ctx, ada_w, ada_b, norm_mix_g, norm_ffn_g, ab_w_in, ab_w_out, da_qnorm_g, da_knorm_g, da_lambda, da_subnorm_g, ml_conv_w, ml_conv_b, ml_gate_b, ml_headnorm_g, hy_w_in, hy_w_out, hy_conv_w, hy_conv_b, hy_f_w1, hy_f_b1, hy_f_freq1, hy_f_w2, hy_f_b2, hy_f_freq2, hy_f_w3, hy_bias, router_w, router_b, moe_w1, moe_b1, moe_w2, moe_b2):
    y = (x_prompt.reshape(T_P, D_MODEL), x_sample.reshape(T_S, D_MODEL))
    cond = jnp.concatenate([c_ctx[None, :], c, jnp.zeros((COND_ROWS - 1 - DEC_BATCH, D_MODEL), F32)], axis=0)
    ada = _ada_table(cond, ada_w, ada_b)
    b1p = moe_b1.reshape(DEPTH, N_EXPERTS, D_FF, 2).swapaxes(2, 3).reshape(DEPTH, N_EXPERTS, 1, 2 * D_FF)
    b2r = moe_b2.reshape(DEPTH, N_EXPERTS, 1, D_MODEL)
    new_k, new_v, new_c, new_n, new_m = [], [], [], [], []
    for layer in range(DEPTH):
        ada_l = ada[layer]
        if layer % 2 == 0:
            e = layer // 2
            lam_init = 0.8 - 0.6 * math.exp(-0.3 * layer)
            qkv, mqk, mv, mo, mg = _modulated_proj(
                y, ada_l, norm_mix_g[layer], ab_w_in[e], (3 * W_A, 2 * W_B, W_B, W_B, 4 * H_B))
            qg2 = jnp.tile(da_qnorm_g[e], 2).reshape(1, 2 * HD_A)
            kg2 = jnp.tile(da_knorm_g[e], 2).reshape(1, 2 * HD_A)
            sub_g = da_subnorm_g[e].reshape(1, 2 * HD_A)
            oa_p, k_norm, v_heads = _attention_prompt(qkv, qg2, kg2, da_lambda[e], sub_g, lam_init)
            cos, sin = _rope_tables()
            oa_s = _attention_sample(
                qkv, cache_attn_k[:, e].reshape(DEC_BATCH, PAST_LEN, W_A),
                cache_attn_v[:, e].reshape(DEC_BATCH, PAST_LEN, W_A), cos, sin,
                qg2, kg2, da_lambda[e], sub_g, lam_init)
            ob_p, c_new, n_new, m_new = _mlstm(
                mqk, mv, mo, mg[:T_P], ml_conv_w[e], ml_conv_b[e].reshape(1, 2 * W_B), ml_gate_b[e],
                ml_headnorm_g[e], seq=SEQ, nbatch=BATCH, row_off=0)
            ob_s = _mlstm(
                mqk, mv, mo, mg[T_P:], ml_conv_w[e], ml_conv_b[e].reshape(1, 2 * W_B), ml_gate_b[e],
                ml_headnorm_g[e], seq=DEC_SEQ, nbatch=DEC_BATCH, row_off=T_P // DEC_SEQ,
                ctx=(state_mlstm_C[:, e], state_mlstm_n[:, e], state_mlstm_m[:, e]))
            y = _out_proj_residual([(oa_p, oa_s), (ob_p, ob_s)], y, ada_l, ab_w_out[e], 2)
            new_k.append(k_norm.reshape(BATCH, SEQ, H_A, 2, HD_A))
            new_v.append(v_heads.reshape(BATCH, SEQ, H_A, 2 * HD_A))
            new_c.append(c_new)
            new_n.append(n_new.reshape(BATCH, 2, H_B, HD_B))
            new_m.append(m_new[..., 0, 0])
        else:
            o = layer // 2
            (zproj,) = _modulated_proj(y, ada_l, norm_mix_g[layer], hy_w_in[o], (HY_PROJ,))
            cores = []
            for seq, nbatch, row_off, td in ((SEQ, BATCH, 0, 512), (DEC_SEQ, DEC_BATCH, T_P // DEC_SEQ, 256)):
                fwd, inv = _dft_mats(seq)
                fwd_bf, inv_bf = fwd.astype(BF16), inv.astype(BF16)
                kspec = _hyena_filter_spectrum(seq, fwd_bf, hy_f_w1[o], hy_f_b1[o], hy_f_freq1[o], hy_f_w2[o],
                                               hy_f_b2[o], hy_f_freq2[o], hy_f_w3[o])
                cores.append(_hyena_core(zproj, hy_conv_w[o], hy_conv_b[o].reshape(1, HY_PROJ), fwd_bf, inv_bf,
                                         kspec, hy_bias[o], seq=seq, nbatch=nbatch, row_off=row_off, td=td))
            y = _out_proj_residual([tuple(cores)], y, ada_l, hy_w_out[o], 2)
        y = _moe(layer, y, ada_l, norm_ffn_g[layer], router_w[layer], router_b[layer],
                 moe_w1, b1p, moe_w2, b2r)
    y_p = y[0].reshape(BATCH, SEQ, D_MODEL)
    y_s = y[1].reshape(DEC_BATCH, DEC_SEQ, D_MODEL)
    return (y_p, y_s, jnp.stack(new_k, axis=1), jnp.stack(new_v, axis=1), jnp.stack(new_c, axis=1),
            jnp.stack(new_n, axis=1), jnp.stack(new_m, axis=1))
```

```python
import functools
import math

import numpy as np
import jax
import jax.numpy as jnp
from jax import lax
from jax.experimental import pallas as pl
from jax.experimental.pallas import tpu as pltpu

D_MODEL = 1024
BATCH = 16
SEQ = 256
DEPTH = 2
DEC_BATCH = 2
DEC_SEQ = 1024
PAST_LEN = 256
GRID_W = 64
W_A = D_MODEL // 2
HD_A = 64
H_A = W_A // (2 * HD_A)
W_B = D_MODEL - W_A
HD_B = 128
H_B = W_B // HD_B
AB_PROJ = 3 * W_A + 4 * W_B + 4 * H_B
ROPE_BASE = 10000.0
CHUNK = 64
HY_ORDER = 2
HY_PROJ = (HY_ORDER + 1) * D_MODEL
HY_BANDS = 8
HY_FH = 64
HY_TARGET = 1e-2
HY_FAST_PCT = 0.3
HY_SLOW_PCT = 1.5
N_EXPERTS = 32
TOP_K = 4
D_FF = D_MODEL
SWIGLU_ALPHA = 1.702
SWIGLU_LIMIT = 7.0
ADA_CHUNKS = 6
EPS = 1e-6
NEG = -1e30
F32 = jnp.float32
BF16 = jnp.bfloat16

T_P = BATCH * SEQ
T_S = DEC_BATCH * DEC_SEQ
T_ALL = T_P + T_S
ROW_TILE = 256
N_ROW_TILES = T_ALL // ROW_TILE
P_TILES = T_P // ROW_TILE
S_TILES_PER_BATCH = DEC_SEQ // ROW_TILE
COND_ROWS = 8
MOE_TILE = 256
N_ASSIGN = T_ALL * TOP_K
MOE_ROWS = N_ASSIGN + N_EXPERTS * MOE_TILE
MOE_TILES = MOE_ROWS // MOE_TILE
X_ALIGN = 16
GATHER_ROWS = 1024
X_ROWS = -(-(N_ASSIGN + N_EXPERTS * X_ALIGN + MOE_TILE) // GATHER_ROWS) * GATHER_ROWS
SPARE_ROW = T_ALL
ACC_ROWS = T_ALL + 8
SCATTER_GROUP = 8
W1_DMA_CHUNKS = 8
W2_DMA_CHUNKS = 4
LANES = 128
ROW_CHUNKS = D_MODEL // LANES
VMEM_LIMIT = 56 * 1024 * 1024
HIGHEST = lax.Precision.HIGHEST


def _cparams(n_axes):
    return pltpu.CompilerParams(dimension_semantics=("arbitrary",) * n_axes,
                                vmem_limit_bytes=VMEM_LIMIT)


def _bdot(a, b):
    return jnp.dot(a.astype(BF16), b.astype(BF16), preferred_element_type=F32)


def _cond_row(i):
    return jnp.where(i < P_TILES, 0, 1 + (i - P_TILES) // S_TILES_PER_BATCH)


def _ada_chunk(ada_ref, row, j):
    return ada_ref[pl.ds(row, 1), j * D_MODEL:(j + 1) * D_MODEL]


def _modulate(x, g, shift, scale):
    ms = jnp.mean(x * x, axis=-1, keepdims=True)
    return (x * lax.rsqrt(ms + EPS) * g) * (1.0 + scale) + shift


def _sigmoid(x):
    return 1.0 / (1.0 + jnp.exp(-x))


def _silu(x):
    return x * _sigmoid(x)


def _log_sigmoid(x):
    return jnp.minimum(x, 0.0) - jnp.log(1.0 + jnp.exp(-jnp.abs(x)))


def _dwconv3(x, w, b):
    n = x.shape[0]
    row = lax.broadcasted_iota(jnp.int32, x.shape, 0)
    prev = jnp.where(row == 0, 0.0, pltpu.roll(x, 1, 0))
    nxt = jnp.where(row == n - 1, 0.0, pltpu.roll(x, n - 1, 0))
    return prev * w[0:1] + x * w[1:2] + nxt * w[2:3] + b


def _ada_kernel(cond_ref, w_ref, b_ref, o_ref):
    c = _silu(cond_ref[...])
    o_ref[...] = _bdot(c, w_ref[...]) + b_ref[...]


def _ada_table(cond, ada_w, ada_b):
    tn = 1536
    return pl.pallas_call(
        _ada_kernel,
        grid=(DEPTH, ADA_CHUNKS * D_MODEL // tn),
        in_specs=[
            pl.BlockSpec((COND_ROWS, D_MODEL), lambda l, j: (0, 0)),
            pl.BlockSpec((None, D_MODEL, tn), lambda l, j: (l, 0, j)),
            pl.BlockSpec((None, 1, tn), lambda l, j: (l, 0, j)),
        ],
        out_specs=pl.BlockSpec((None, COND_ROWS, tn), lambda l, j: (l, 0, j)),
        out_shape=jax.ShapeDtypeStruct((DEPTH, COND_ROWS, ADA_CHUNKS * D_MODEL), F32),
        compiler_params=_cparams(2),
    )(cond, ada_w, ada_b.reshape(DEPTH, 1, ADA_CHUNKS * D_MODEL))


def _stream_specs(y):
    if isinstance(y, tuple):
        return [pl.BlockSpec((ROW_TILE, D_MODEL), lambda i: (jnp.minimum(i, P_TILES - 1), 0)),
                pl.BlockSpec((ROW_TILE, D_MODEL), lambda i: (jnp.maximum(i - P_TILES, 0), 0))], list(y)
    return [pl.BlockSpec((ROW_TILE, D_MODEL), lambda i: (i, 0))], [y]


def _stream_tile(y_refs, i):
    if len(y_refs) == 2:
        return jnp.where(i < P_TILES, y_refs[0][...], y_refs[1][...])
    return y_refs[0][...]


def _proj_kernel(splits, n_y, *refs):
    y_refs = refs[:n_y]
    ada_ref, g_ref, w_ref = refs[n_y:n_y + 3]
    out_refs, wbf_ref = refs[n_y + 3:-1], refs[-1]
    i = pl.program_id(0)

    @pl.when(i == 0)
    def _():
        wbf_ref[...] = w_ref[...].astype(BF16)

    row = _cond_row(i)
    h = _modulate(_stream_tile(y_refs, i), g_ref[...], _ada_chunk(ada_ref, row, 0), _ada_chunk(ada_ref, row, 1))
    h = h.astype(BF16)
    lo = 0
    for o_ref, width in zip(out_refs, splits):
        o_ref[...] = jnp.dot(h, wbf_ref[:, lo:lo + width], preferred_element_type=F32)
        lo += width


def _modulated_proj(y, ada_l, g, w, splits):
    n = w.shape[1]
    y_specs, y_args = _stream_specs(y)
    return pl.pallas_call(
        functools.partial(_proj_kernel, splits, len(y_args)),
        grid=(N_ROW_TILES,),
        in_specs=y_specs + [
            pl.BlockSpec((COND_ROWS, ADA_CHUNKS * D_MODEL), lambda i: (0, 0)),
            pl.BlockSpec((1, D_MODEL), lambda i: (0, 0)),
            pl.BlockSpec((D_MODEL, n), lambda i: (0, 0), pipeline_mode=pl.Buffered(1)),
        ],
        out_specs=[pl.BlockSpec((ROW_TILE, s), lambda i: (i, 0)) for s in splits],
        out_shape=[jax.ShapeDtypeStruct((T_ALL, s), F32) for s in splits],
        scratch_shapes=[pltpu.VMEM((D_MODEL, n), BF16)],
        compiler_params=_cparams(1),
    )(*y_args, ada_l, g.reshape(1, D_MODEL), w)


def _out_proj_kernel(n_in, n_y, gate_chunk, *refs):
    x_refs = refs[:2 * n_in]
    y_refs = refs[2 * n_in:2 * n_in + n_y]
    ada_ref, w_ref, o_ref, wbf_ref = refs[2 * n_in + n_y:]
    i = pl.program_id(0)

    @pl.when(i == 0)
    def _():
        wbf_ref[...] = w_ref[...].astype(BF16)

    acc = None
    lo = 0
    for xp_ref, xs_ref in zip(x_refs[0::2], x_refs[1::2]):
        k = xp_ref.shape[1]
        x = jnp.where(i < P_TILES, xp_ref[...], xs_ref[...])
        part = jnp.dot(x.astype(BF16), wbf_ref[lo:lo + k, :], preferred_element_type=F32)
        acc = part if acc is None else acc + part
        lo += k
    gate = _ada_chunk(ada_ref, _cond_row(i), gate_chunk)
    o_ref[...] = _stream_tile(y_refs, i) + gate * acc


def _out_proj_residual(xs, y, ada_l, w, gate_chunk):
    y_specs, y_args = _stream_specs(y)
    x_specs = []
    for xp, _ in xs:
        x_specs.append(pl.BlockSpec((ROW_TILE, xp.shape[1]), lambda i: (jnp.minimum(i, P_TILES - 1), 0)))
        x_specs.append(pl.BlockSpec((ROW_TILE, xp.shape[1]), lambda i: (jnp.maximum(i - P_TILES, 0), 0)))
    return pl.pallas_call(
        functools.partial(_out_proj_kernel, len(xs), len(y_args), gate_chunk),
        grid=(N_ROW_TILES,),
        in_specs=x_specs + y_specs + [
            pl.BlockSpec((COND_ROWS, ADA_CHUNKS * D_MODEL), lambda i: (0, 0)),
            pl.BlockSpec((D_MODEL, D_MODEL), lambda i: (0, 0), pipeline_mode=pl.Buffered(1)),
        ],
        out_specs=pl.BlockSpec((ROW_TILE, D_MODEL), lambda i: (i, 0)),
        out_shape=jax.ShapeDtypeStruct((T_ALL, D_MODEL), F32),
        scratch_shapes=[pltpu.VMEM((D_MODEL, D_MODEL), BF16)],
        compiler_params=_cparams(1),
    )(*[a for pair in xs for a in pair], *y_args, ada_l, w)


def _subhead_norm(x, g2):
    lane = lax.broadcasted_iota(jnp.int32, x.shape, 1)
    first = lane < HD_A
    xx = x * x
    s0 = jnp.sum(jnp.where(first, xx, 0.0), axis=-1, keepdims=True)
    s1 = jnp.sum(jnp.where(first, 0.0, xx), axis=-1, keepdims=True)
    r = jnp.where(first, lax.rsqrt(s0 / HD_A + EPS), lax.rsqrt(s1 / HD_A + EPS))
    return x * r * g2


def _rope(x, cos, sin):
    quarter = HD_A // 4
    lane = lax.broadcasted_iota(jnp.int32, x.shape, 1)
    lower = (lane % (2 * quarter)) < quarter
    swapped = jnp.where(lower, pltpu.roll(x, 2 * HD_A - quarter, 1), pltpu.roll(x, quarter, 1))
    return x * cos + swapped * sin


def _attn_kernel(lam_init, has_ctx, *refs):
    if has_ctx:
        (q_ref, k_ref, v_ref, ck_ref, cv_ref, cq_ref, sq_ref, ckk_ref, skk_ref,
         qg_ref, kg_ref, lp_ref, sg_ref, o_ref, kall_ref, vall_ref) = refs
    else:
        q_ref, k_ref, v_ref, qg_ref, kg_ref, lp_ref, sg_ref, o_ref, kn_ref, vh_ref = refs
    lp = lp_ref[...]
    lam = (jnp.exp(jnp.sum(lp[0:1] * lp[1:2], axis=-1, keepdims=True))
           - jnp.exp(jnp.sum(lp[2:3] * lp[3:4], axis=-1, keepdims=True)) + lam_init)

    def attend(q, k, v):
        probs = []
        for c in range(2):
            qc = q[:, c * HD_A:(c + 1) * HD_A].astype(BF16)
            kc = k[:, c * HD_A:(c + 1) * HD_A].astype(BF16)
            s = lax.dot_general(qc, kc, (((1,), (1,)), ((), ())), preferred_element_type=F32) * (HD_A ** -0.5)
            e = jnp.exp(s - jnp.max(s, axis=-1, keepdims=True))
            probs.append(e / jnp.sum(e, axis=-1, keepdims=True))
        o = _bdot(probs[0] - lam * probs[1], v)
        ms = jnp.mean(o * o, axis=-1, keepdims=True)
        return (o * lax.rsqrt(ms + EPS) * sg_ref[...]) * (1.0 - lam_init)

    if not has_ctx:
        for h in range(H_A):
            cols = slice(h * 2 * HD_A, (h + 1) * 2 * HD_A)
            k = _subhead_norm(k_ref[:, cols], kg_ref[...])
            for c in range(2):
                kn_ref[pl.ds(2 * h + c, SEQ, stride=2 * H_A), :] = k[:, c * HD_A:(c + 1) * HD_A]
            v = v_ref[:, cols]
            vh_ref[pl.ds(h, SEQ, stride=H_A), :] = v
            o_ref[:, cols] = attend(_subhead_norm(q_ref[:, cols], qg_ref[...]), k, v)
        return

    @pl.when(pl.program_id(2) == 0)
    def _():
        kall_ref[0:PAST_LEN, :] = ck_ref[...].astype(BF16)
        vall_ref[0:PAST_LEN, :] = cv_ref[...].astype(BF16)
        k_new = _rope(_subhead_norm(k_ref[...], kg_ref[...]), ckk_ref[...], skk_ref[...])
        kall_ref[PAST_LEN:, :] = k_new.astype(BF16)
        vall_ref[PAST_LEN:, :] = v_ref[...].astype(BF16)

    q = _rope(_subhead_norm(q_ref[...], qg_ref[...]), cq_ref[...], sq_ref[...])
    o_ref[...] = attend(q, kall_ref[...], vall_ref[...])


def _attention_prompt(qkv, qg2, kg2, lam_p, sub_g, lam_init):
    head = 2 * HD_A
    small = [
        pl.BlockSpec((1, head), lambda b: (0, 0)),
        pl.BlockSpec((1, head), lambda b: (0, 0)),
        pl.BlockSpec((4, HD_A), lambda b: (0, 0)),
        pl.BlockSpec((1, head), lambda b: (0, 0)),
    ]
    return pl.pallas_call(
        functools.partial(_attn_kernel, lam_init, False),
        grid=(BATCH,),
        in_specs=[
            pl.BlockSpec((SEQ, W_A), lambda b: (b, 0)),
            pl.BlockSpec((SEQ, W_A), lambda b: (b, 1)),
            pl.BlockSpec((SEQ, W_A), lambda b: (b, 2)),
        ] + small,
        out_specs=[pl.BlockSpec((SEQ, W_A), lambda b: (b, 0)),
                   pl.BlockSpec((SEQ * 2 * H_A, HD_A), lambda b: (b, 0)),
                   pl.BlockSpec((SEQ * H_A, head), lambda b: (b, 0))],
        out_shape=[jax.ShapeDtypeStruct((T_P, W_A), F32), jax.ShapeDtypeStruct((T_P * 2 * H_A, HD_A), F32),
                   jax.ShapeDtypeStruct((T_P * H_A, head), F32)],
        compiler_params=_cparams(1),
    )(qkv, qkv, qkv, qg2, kg2, lam_p, sub_g)


def _attention_sample(qkv, cache_k, cache_v, cos, sin, qg2, kg2, lam_p, sub_g, lam_init):
    nh = H_A
    head = 2 * HD_A
    tq = ROW_TILE
    nq = DEC_SEQ // tq
    q_off = T_P // tq
    k_off = T_P // DEC_SEQ
    small = [
        pl.BlockSpec((1, head), lambda b, h, i: (0, 0)),
        pl.BlockSpec((1, head), lambda b, h, i: (0, 0)),
        pl.BlockSpec((4, HD_A), lambda b, h, i: (0, 0)),
        pl.BlockSpec((1, head), lambda b, h, i: (0, 0)),
    ]
    return pl.pallas_call(
        functools.partial(_attn_kernel, lam_init, True),
        grid=(DEC_BATCH, nh, nq),
        in_specs=[
            pl.BlockSpec((tq, head), lambda b, h, i: (q_off + b * nq + i, h)),
            pl.BlockSpec((DEC_SEQ, head), lambda b, h, i: (k_off + b, nh + h)),
            pl.BlockSpec((DEC_SEQ, head), lambda b, h, i: (k_off + b, 2 * nh + h)),
            pl.BlockSpec((None, PAST_LEN, head), lambda b, h, i: (b, 0, h)),
            pl.BlockSpec((None, PAST_LEN, head), lambda b, h, i: (b, 0, h)),
            pl.BlockSpec((tq, head), lambda b, h, i: (i, 0)),
            pl.BlockSpec((tq, head), lambda b, h, i: (i, 0)),
            pl.BlockSpec((DEC_SEQ, head), lambda b, h, i: (0, 0)),
            pl.BlockSpec((DEC_SEQ, head), lambda b, h, i: (0, 0)),
        ] + small,
        out_specs=pl.BlockSpec((tq, head), lambda b, h, i: (b * nq + i, h)),
        out_shape=jax.ShapeDtypeStruct((T_S, W_A), F32),
        scratch_shapes=[pltpu.VMEM((PAST_LEN + DEC_SEQ, head), BF16)] * 2,
        compiler_params=_cparams(3),
    )(qkv, qkv, qkv, cache_k, cache_v, cos, sin, cos, sin, qg2, kg2, lam_p, sub_g)


def _rope_tables():
    half = HD_A // 2
    nf = half // 2
    inv = ROPE_BASE ** (-np.arange(nf, dtype=np.float32) / nf)
    pos = np.arange(DEC_SEQ)
    row = (pos // GRID_W).astype(np.float32)
    col = (pos % GRID_W).astype(np.float32)
    ang_r = (row[:, None] * inv).astype(np.float32)
    ang_c = (col[:, None] * inv).astype(np.float32)
    ang = np.concatenate([ang_r, ang_r, ang_c, ang_c], axis=1)
    sign = np.concatenate([-np.ones(nf), np.ones(nf), -np.ones(nf), np.ones(nf)]).astype(np.float32)
    cos = np.cos(ang.astype(np.float64)).astype(np.float32)
    sin = (np.sin(ang.astype(np.float64)) * sign).astype(np.float32)
    return jnp.asarray(np.tile(cos, (1, 2))), jnp.asarray(np.tile(sin, (1, 2)))


def _mlstm_kernel(seq, has_ctx, *refs):
    if has_ctx:
        (q_ref, k_ref, cwq_ref, cwk_ref, cbq_ref, cbk_ref, v_ref, mo_ref, gi_ref, gf_ref,
         gbi_ref, gbf_ref, hn_ref, c0_ref, n0_ref, m0_ref, o_ref,
         qs_ref, ks_ref, hf_ref, hb_ref, cs_ref, rrow_ref, col_ref, wc_ref) = refs
    else:
        (q_ref, k_ref, cwq_ref, cwk_ref, cbq_ref, cbk_ref, v_ref, mo_ref, gi_ref, gf_ref,
         gbi_ref, gbf_ref, hn_ref, o_ref, c_out_ref, n_out_ref, m_out_ref,
         qs_ref, ks_ref, hf_ref, hb_ref, cs_ref, rrow_ref, col_ref, wc_ref) = refs
    nc = seq // CHUNK
    n_chain = 2 * H_B
    chains = [(d, h) for d in range(2) for h in range(H_B)]
    qs_ref[...] = _silu(_dwconv3(q_ref[...], cwq_ref[...], cbq_ref[...])) * (HD_B ** -0.5)
    ks_ref[...] = _silu(_dwconv3(k_ref[...], cwk_ref[...], cbk_ref[...]))

    rows = nc * n_chain
    lane = lax.broadcasted_iota(jnp.int32, (rows, 2 * CHUNK), 1)
    forward = lax.broadcasted_iota(jnp.int32, (rows, 2 * CHUNK), 0) % n_chain < H_B
    valid = lane < CHUNK

    def scan(x, op, fill):
        pre, suf = x, x
        sh = 1
        while sh < CHUNK:
            pre = op(pre, jnp.where(lane >= sh, pltpu.roll(pre, sh, 1), fill))
            suf = op(suf, jnp.where(lane + sh < CHUNK, pltpu.roll(suf, 2 * CHUNK - sh, 1), fill))
            sh *= 2
        return jnp.where(forward, pre, suf)

    gate_i = (gi_ref[...] + gbi_ref[...]).reshape(rows, 2 * CHUNK)
    lf = jnp.where(valid, _log_sigmoid(gf_ref[...] + gbf_ref[...]).reshape(rows, 2 * CHUNK), 0.0)
    b = scan(lf, jnp.add, 0.0)
    cmax = scan(jnp.where(valid, gate_i - b, -jnp.inf), jnp.maximum, -jnp.inf)
    b_last = jnp.sum(lf, axis=1, keepdims=True)
    g = b_last - b + gate_i
    g_max = jnp.max(jnp.where(valid, g, -jnp.inf), axis=1, keepdims=True)
    mm = m0_ref[...] if has_ctx else jnp.zeros((n_chain, 1), F32)
    mm_seq = []
    for p in range(nc):
        mm_seq.append(mm)
        seg = slice(p * n_chain, (p + 1) * n_chain)
        mm = jnp.maximum(b_last[seg] + mm, g_max[seg])
    mm_final = mm
    mm_prev = jnp.concatenate(mm_seq, axis=0)
    mm_next = jnp.concatenate(mm_seq[1:] + [mm_final], axis=0)
    m_t = jnp.maximum(b + mm_prev, b + cmax)
    rrow_ref[...] = (b - gate_i).reshape(nc, n_chain, 2 * CHUNK)
    wc_ref[...] = jnp.exp(b_last + mm_prev - mm_next).reshape(nc, n_chain, 1)
    per_row = [b, m_t, jnp.exp(b + mm_prev - m_t), jnp.exp(-m_t), jnp.exp(g - mm_next)]
    for j, arr in enumerate(per_row):
        by_time = arr.T
        for p in range(nc):
            col_ref[p, :, j * n_chain:(j + 1) * n_chain] = by_time[0:CHUNK, p * n_chain:(p + 1) * n_chain]

    t_idx = lax.broadcasted_iota(jnp.int32, (CHUNK, CHUNK), 0)
    s_idx = lax.broadcasted_iota(jnp.int32, (CHUNK, CHUNK), 1)
    for n, (d, h) in enumerate(chains):
        cs_ref[n] = c0_ref[d, h] if has_ctx else jnp.zeros((HD_B, HD_B), F32)

    def out_step(p, n_states):
        cols = col_ref[p]
        rrows = rrow_ref[p]
        wcs = wc_ref[p]
        new_states = []
        for n, (d, h) in enumerate(chains):
            c = p if d == 0 else nc - 1 - p
            r0 = pl.multiple_of(c * CHUNK, CHUNK)
            hcols = slice(h * HD_B, (h + 1) * HD_B)
            qt = qs_ref[pl.ds(r0, CHUNK), hcols]
            kt = ks_ref[pl.ds(r0, CHUNK), hcols]
            vt = v_ref[pl.ds(r0, CHUNK), hcols]
            b_col, m_t, w_inter, e_inv, w_k = (cols[:, j * n_chain + n:j * n_chain + n + 1] for j in range(5))
            mask = (s_idx <= t_idx) if d == 0 else (s_idx >= t_idx)
            decay = jnp.exp(jnp.where(mask, b_col - rrows[n:n + 1, 0:CHUNK], NEG) - m_t)
            qk = lax.dot_general(qt.astype(BF16), kt.astype(BF16), (((1,), (1,)), ((), ())),
                                 preferred_element_type=F32)
            s = qk * decay
            cm = cs_ref[n]
            nm = n_states[n]
            cq = lax.dot_general(qt.astype(BF16), cm.astype(BF16), (((1,), (1,)), ((), ())),
                                 preferred_element_type=F32)
            num = _bdot(s, vt) + w_inter * cq
            nq = jnp.sum(s, axis=-1, keepdims=True) + w_inter * jnp.sum(qt * nm, axis=-1, keepdims=True)
            hdir_ref = hf_ref if d == 0 else hb_ref
            hdir_ref[pl.ds(r0, CHUNK), hcols] = num / jnp.maximum(jnp.abs(nq), e_inv)
            w_c = wcs[n:n + 1, :]
            vw = (vt * w_k).astype(BF16)
            cs_ref[n] = w_c * cm + lax.dot_general(vw, kt.astype(BF16), (((0,), (0,)), ((), ())),
                                                   preferred_element_type=F32)
            new_states.append(w_c * nm + jnp.sum(kt * w_k, axis=0, keepdims=True))
        return tuple(new_states)

    if has_ctx:
        n_init = tuple(n0_ref[d, h] for d, h in chains)
    else:
        n_init = tuple(jnp.zeros((1, HD_B), F32) for _ in chains)
    n_final = lax.fori_loop(0, nc, out_step, n_init)
    if not has_ctx:
        for n, (d, h) in enumerate(chains):
            c_out_ref[d, h] = cs_ref[n]
            n_out_ref[d, h] = n_final[n]
            m_out_ref[d, h] = jnp.broadcast_to(mm_final[n:n + 1, :], (1, HD_B))

    for h in range(H_B):
        hcols = slice(h * HD_B, (h + 1) * HD_B)
        hh = hf_ref[:, hcols] + hb_ref[:, hcols]
        ms = jnp.mean(hh * hh, axis=-1, keepdims=True)
        o_ref[:, hcols] = (hh * lax.rsqrt(ms + EPS) * hn_ref[:, hcols]) * _sigmoid(mo_ref[:, hcols])


def _mlstm(mqk, mv, mo, mg_stream, conv_w, conv_b, gate_b, hn_g, *, seq, nbatch, row_off, ctx=None):
    nh = H_B
    nc = seq // CHUNK
    has_ctx = ctx is not None
    gt = mg_stream.reshape(nbatch, nc, CHUNK, 2, 2, nh).transpose(0, 1, 3, 4, 5, 2)
    pad = ((0, 0), (0, 0), (0, 0), (0, CHUNK))
    gates = [jnp.pad(jnp.concatenate([gt[:, :, 0, j], gt[:, ::-1, 1, j]], axis=2), pad) for j in range(2)]
    gate_bias = [jnp.concatenate([gate_b[0, j], gate_b[1, j]]).reshape(2 * nh, 1) for j in range(2)]
    blk = lambda col: pl.BlockSpec((seq, W_B), lambda b, col=col: (row_off + b, col))
    gate_blk = pl.BlockSpec((None, nc, 2 * nh, 2 * CHUNK), lambda b: (b, 0, 0, 0))
    in_specs = [
        blk(0), blk(1),
        pl.BlockSpec((3, W_B), lambda b: (0, 0)),
        pl.BlockSpec((3, W_B), lambda b: (0, 1)),
        pl.BlockSpec((1, W_B), lambda b: (0, 0)),
        pl.BlockSpec((1, W_B), lambda b: (0, 1)),
        blk(0), blk(0),
        gate_blk, gate_blk,
        pl.BlockSpec((2 * nh, 1), lambda b: (0, 0)),
        pl.BlockSpec((2 * nh, 1), lambda b: (0, 0)),
        pl.BlockSpec((1, W_B), lambda b: (0, 0)),
    ]
    args = [mqk, mqk, conv_w, conv_w, conv_b, conv_b, mv, mo, gates[0], gates[1],
            gate_bias[0], gate_bias[1], hn_g.reshape(1, W_B)]
    o_spec = pl.BlockSpec((seq, W_B), lambda b: (b, 0))
    o_shape = jax.ShapeDtypeStruct((nbatch * seq, W_B), F32)
    state_blk = lambda rows: pl.BlockSpec((None, 2, nh, rows, HD_B), lambda b: (b, 0, 0, 0, 0))
    if has_ctx:
        c0, n0, m0 = ctx
        in_specs += [state_blk(HD_B), state_blk(1), pl.BlockSpec((None, 2 * nh, 1), lambda b: (b, 0, 0))]
        args += [c0, n0.reshape(nbatch, 2, nh, 1, HD_B), m0.reshape(nbatch, 2 * nh, 1)]
        out_specs, out_shape = o_spec, o_shape
    else:
        out_specs = [o_spec, state_blk(HD_B), state_blk(1), state_blk(1)]
        out_shape = [
            o_shape,
            jax.ShapeDtypeStruct((nbatch, 2, nh, HD_B, HD_B), F32),
            jax.ShapeDtypeStruct((nbatch, 2, nh, 1, HD_B), F32),
            jax.ShapeDtypeStruct((nbatch, 2, nh, 1, HD_B), F32),
        ]
    return pl.pallas_call(
        functools.partial(_mlstm_kernel, seq, has_ctx),
        grid=(nbatch,),
        in_specs=in_specs,
        out_specs=out_specs,
        out_shape=out_shape,
        scratch_shapes=[pltpu.VMEM((seq, W_B), F32)] * 4 + [
            pltpu.VMEM((2 * nh, HD_B, HD_B), F32),
            pltpu.VMEM((nc, 2 * nh, 2 * CHUNK), F32),
            pltpu.VMEM((nc, CHUNK, 5 * 2 * nh), F32),
            pltpu.VMEM((nc, 2 * nh, 1), F32),
        ],
        compiler_params=_cparams(1),
    )(*args)


def _dft_mats(L):
    f = np.arange(L)[:, None]
    j = np.arange(L)[None, :]
    ang = 2.0 * np.pi * ((f * j) % (2 * L)) / (2 * L)
    cm = np.cos(ang)
    sm = np.sin(ang)
    alt = (1.0 - 2.0 * (np.arange(L) % 2))
    fwd_b = -sm
    fwd_b[0, :] = alt
    fwd = np.concatenate([cm, fwd_b], axis=0)
    wgt = np.where(np.arange(L) == 0, 1.0, 2.0)[None, :]
    inv_a = cm.T * wgt
    inv_b = -2.0 * sm.T
    inv_b[:, 0] = alt
    inv = np.concatenate([inv_a, inv_b], axis=1) / (2 * L)
    return jnp.asarray(fwd.astype(np.float32)), jnp.asarray(inv.astype(np.float32))


def _hyena_feats(L):
    t = np.linspace(0.0, 1.0, L, dtype=np.float32)
    wpos = (2.0 * math.pi * np.arange(L, dtype=np.float32) / L).astype(np.float32)
    fb = np.linspace(1e-4, HY_BANDS - 1, HY_BANDS, dtype=np.float32)
    z = (wpos[:, None] * fb).astype(np.float32)
    feats = np.concatenate([t[:, None], np.cos(z), -np.sin(z)], axis=-1).astype(np.float32)
    deltas = np.abs(np.linspace(math.log(HY_TARGET) / HY_SLOW_PCT, math.log(HY_TARGET) / HY_FAST_PCT,
                                D_MODEL, dtype=np.float32))
    decay = np.exp(-t[:, None] * deltas).astype(np.float32)
    return jnp.asarray(feats), jnp.asarray(decay)


def _filter_kernel(L, feats_ref, w1_ref, b1_ref, fr1_ref, w2_ref, b2_ref, fr2_ref, w3f_ref, w3b_ref,
                   decay_ref, fwd_ref, o_ref, hdn_ref):
    @pl.when((pl.program_id(0) == 0) & (pl.program_id(1) == 0))
    def _():
        h1 = jnp.sin(fr1_ref[...] * (jnp.dot(feats_ref[...], w1_ref[...], precision=HIGHEST,
                                             preferred_element_type=F32) + b1_ref[...]))
        hdn_ref[...] = jnp.sin(fr2_ref[...] * (jnp.dot(h1, w2_ref[...], precision=HIGHEST,
                                                       preferred_element_type=F32) + b2_ref[...]))

    hdn = hdn_ref[...]
    decay = decay_ref[...]
    f_fwd = jnp.dot(hdn, w3f_ref[...], precision=HIGHEST, preferred_element_type=F32) * decay
    f_bwd = jnp.dot(hdn, w3b_ref[...], precision=HIGHEST, preferred_element_type=F32) * decay
    row = lax.broadcasted_iota(jnp.int32, f_bwd.shape, 0)
    f_bwd = jnp.where(row == 0, 0.0, f_bwd)
    fwd = fwd_ref[...]
    p = _bdot(fwd, f_fwd)
    q = _bdot(fwd, f_bwd)
    first = row == 0
    o_ref[0:L, :] = p[0:L] + q[0:L]
    o_ref[L:2 * L, :] = p[L:2 * L] + jnp.where(first, q[L:2 * L], -q[L:2 * L])


def _hyena_filter_spectrum(L, fwd_bf, w1, b1, fr1, w2, b2, fr2, w3):
    feats, decay = _hyena_feats(L)
    td = 512
    nd = D_MODEL // td
    emb = feats.shape[1]
    vec = lambda a: a.reshape(1, HY_FH)
    full = lambda shape: pl.BlockSpec(shape, lambda o, j: (0, 0))
    return pl.pallas_call(
        functools.partial(_filter_kernel, L),
        grid=(HY_ORDER, nd),
        in_specs=[
            full((L, emb)), full((emb, HY_FH)), full((1, HY_FH)), full((1, HY_FH)),
            full((HY_FH, HY_FH)), full((1, HY_FH)), full((1, HY_FH)),
            pl.BlockSpec((HY_FH, td), lambda o, j: (0, o * 2 * nd + j)),
            pl.BlockSpec((HY_FH, td), lambda o, j: (0, o * 2 * nd + nd + j)),
            pl.BlockSpec((L, td), lambda o, j: (0, j)),
            full((2 * L, L)),
        ],
        out_specs=pl.BlockSpec((2 * L, td), lambda o, j: (0, o * nd + j)),
        out_shape=jax.ShapeDtypeStruct((2 * L, HY_ORDER * D_MODEL), F32),
        scratch_shapes=[pltpu.VMEM((L, HY_FH), F32)],
        compiler_params=_cparams(2),
    )(feats, w1, vec(b1), vec(fr1), w2, vec(b2), vec(fr2), w3, w3, decay, fwd_bf)


def _spectral_conv(u, fwd, inv, kspec, L):
    uf = jnp.dot(fwd, u.astype(BF16), preferred_element_type=F32)
    ua, ub = uf[0:L], uf[L:2 * L]
    ka, kb = kspec[0:L], kspec[L:2 * L]
    first = lax.broadcasted_iota(jnp.int32, ua.shape, 0) == 0
    ya = ua * ka - jnp.where(first, 0.0, ub * kb)
    yb = jnp.where(first, ub * kb, ua * kb + ub * ka)
    y = jnp.concatenate([ya, yb], axis=0).astype(BF16)
    return jnp.dot(inv, y, preferred_element_type=F32)


def _hyena_kernel(L, zv_ref, z1_ref, z2_ref, cwv_ref, cw1_ref, cw2_ref, cbv_ref, cb1_ref, cb2_ref,
                  fwd_ref, inv_ref, k0_ref, k1_ref, bias0_ref, bias1_ref, o_ref):
    fwd = fwd_ref[...]
    inv = inv_ref[...]
    v = _dwconv3(zv_ref[...], cwv_ref[...], cbv_ref[...])
    x1 = _dwconv3(z1_ref[...], cw1_ref[...], cb1_ref[...])
    x2 = _dwconv3(z2_ref[...], cw2_ref[...], cb2_ref[...])
    z = x1 * (_spectral_conv(v, fwd, inv, k0_ref[...], L) + v * bias0_ref[...])
    o_ref[...] = x2 * (_spectral_conv(z, fwd, inv, k1_ref[...], L) + z * bias1_ref[...])


def _hyena_core(zproj, conv_w, conv_b, fwd_bf, inv_bf, kspec, bias, *, seq, nbatch, row_off, td):
    nd = D_MODEL // td
    zblk = lambda part: pl.BlockSpec((seq, td), lambda b, j, part=part: (row_off + b, part * nd + j))
    cwblk = lambda part: pl.BlockSpec((3, td), lambda b, j, part=part: (0, part * nd + j))
    cbblk = lambda part: pl.BlockSpec((1, td), lambda b, j, part=part: (0, part * nd + j))
    return pl.pallas_call(
        functools.partial(_hyena_kernel, seq),
        grid=(nbatch, nd),
        in_specs=[
            zblk(0), zblk(1), zblk(2), cwblk(0), cwblk(1), cwblk(2), cbblk(0), cbblk(1), cbblk(2),
            pl.BlockSpec((2 * seq, seq), lambda b, j: (0, 0), pipeline_mode=pl.Buffered(1)),
            pl.BlockSpec((seq, 2 * seq), lambda b, j: (0, 0), pipeline_mode=pl.Buffered(1)),
            pl.BlockSpec((2 * seq, td), lambda b, j: (0, j)),
            pl.BlockSpec((2 * seq, td), lambda b, j: (0, nd + j)),
            pl.BlockSpec((None, 1, td), lambda b, j: (0, 0, j)),
            pl.BlockSpec((None, 1, td), lambda b, j: (1, 0, j)),
        ],
        out_specs=pl.BlockSpec((seq, td), lambda b, j: (b, j)),
        out_shape=jax.ShapeDtypeStruct((nbatch * seq, D_MODEL), F32),
        compiler_params=_cparams(2),
    )(zproj, zproj, zproj, conv_w, conv_w, conv_w, conv_b, conv_b, conv_b,
      fwd_bf, inv_bf, kspec, kspec, bias.reshape(HY_ORDER, 1, D_MODEL), bias.reshape(HY_ORDER, 1, D_MODEL))


def _router_kernel(x_ref, ada_ref, g_ref, rw_ref, rb_ref, h_ref, idx_ref, wt_ref):
    i = pl.program_id(0)
    row = _cond_row(i)
    h = _modulate(x_ref[...], g_ref[...], _ada_chunk(ada_ref, row, 3), _ada_chunk(ada_ref, row, 4))
    for j in range(ROW_CHUNKS):
        h_ref[pl.ds(j, ROW_TILE, stride=ROW_CHUNKS), :] = h[:, j * LANES:(j + 1) * LANES]
    logits = lax.dot_general(rw_ref[...], h, (((1,), (1,)), ((), ())), precision=HIGHEST,
                             preferred_element_type=F32) + rb_ref[...]
    expert = lax.broadcasted_iota(jnp.int32, logits.shape, 0)
    slot = lax.broadcasted_iota(jnp.int32, (TOP_K, logits.shape[1]), 0)
    vals = jnp.zeros((TOP_K, logits.shape[1]), F32)
    idxs = jnp.zeros((TOP_K, logits.shape[1]), jnp.int32)
    cur = logits
    for k in range(TOP_K):
        m = jnp.max(cur, axis=0, keepdims=True)
        a = jnp.min(jnp.where(cur == m, expert, N_EXPERTS), axis=0, keepdims=True)
        vals = jnp.where(slot == k, m, vals)
        idxs = jnp.where(slot == k, a, idxs)
        cur = jnp.where(expert == a, -jnp.inf, cur)
    e = jnp.exp(vals - vals[0:1])
    wt_ref[...] = e / jnp.sum(e, axis=0, keepdims=True)
    idx_ref[...] = idxs


def _router(y, ada_l, g, router_w, router_b):
    return pl.pallas_call(
        _router_kernel,
        grid=(N_ROW_TILES,),
        in_specs=[
            pl.BlockSpec((ROW_TILE, D_MODEL), lambda i: (i, 0)),
            pl.BlockSpec((COND_ROWS, ADA_CHUNKS * D_MODEL), lambda i: (0, 0)),
            pl.BlockSpec((1, D_MODEL), lambda i: (0, 0)),
            pl.BlockSpec((N_EXPERTS, D_MODEL), lambda i: (0, 0)),
            pl.BlockSpec((N_EXPERTS, 1), lambda i: (0, 0)),
        ],
        out_specs=[
            pl.BlockSpec((ROW_TILE * ROW_CHUNKS, LANES), lambda i: (i, 0)),
            pl.BlockSpec((TOP_K, ROW_TILE), lambda i: (0, i)),
            pl.BlockSpec((TOP_K, ROW_TILE), lambda i: (0, i)),
        ],
        out_shape=[
            jax.ShapeDtypeStruct((T_ALL * ROW_CHUNKS, LANES), F32),
            jax.ShapeDtypeStruct((TOP_K, T_ALL), jnp.int32),
            jax.ShapeDtypeStruct((TOP_K, T_ALL), F32),
        ],
        compiler_params=_cparams(1),
    )(y, ada_l, g.reshape(1, D_MODEL), router_w.T, router_b.reshape(N_EXPERTS, 1))


def _dispatch_kernel(rows_ref, h_ref, o_ref):
    base = pl.program_id(0) * GATHER_ROWS
    for r in range(GATHER_ROWS):
        t = pl.multiple_of(rows_ref[base + r] * ROW_CHUNKS, ROW_CHUNKS)
        o_ref[r * ROW_CHUNKS:(r + 1) * ROW_CHUNKS, :] = h_ref[pl.ds(t, ROW_CHUNKS), :]


def _dispatch(h_tiles, gather_row):
    grid_spec = pltpu.PrefetchScalarGridSpec(
        num_scalar_prefetch=1,
        grid=(X_ROWS // GATHER_ROWS,),
        in_specs=[pl.BlockSpec((T_ALL * ROW_CHUNKS, LANES), lambda i, rows: (0, 0), pipeline_mode=pl.Buffered(1))],
        out_specs=pl.BlockSpec((GATHER_ROWS * ROW_CHUNKS, LANES), lambda i, rows: (i, 0)),
    )
    return pl.pallas_call(
        _dispatch_kernel,
        grid_spec=grid_spec,
        out_shape=jax.ShapeDtypeStruct((X_ROWS * ROW_CHUNKS, LANES), F32),
        compiler_params=_cparams(1),
    )(gather_row, h_tiles)


def _deinterleave_matrix():
    s = np.zeros((256, 256), np.float32)
    j = np.arange(128)
    s[2 * j, j] = 1.0
    s[2 * j + 1, 128 + j] = 1.0
    return jnp.asarray(s)


def _weight_copies(layer, e, w1_hbm, w2_hbm, w1s_ref, w2s_ref, sem):
    copies = []
    r1 = D_MODEL // W1_DMA_CHUNKS
    for c in range(W1_DMA_CHUNKS):
        copies.append(pltpu.make_async_copy(w1_hbm.at[layer, e, pl.ds(c * r1, r1)],
                                            w1s_ref.at[pl.ds(c * r1, r1)], sem.at[c]))
    r2 = D_FF // W2_DMA_CHUNKS
    for c in range(W2_DMA_CHUNKS):
        copies.append(pltpu.make_async_copy(w2_hbm.at[layer, e, pl.ds(c * r2, r2)],
                                            w2s_ref.at[pl.ds(c * r2, r2)], sem.at[W1_DMA_CHUNKS + c]))
    return copies


def _expert_kernel(layer, te_ref, tf_ref, ne_ref, nu_ref, src_ref, xo_ref, vr_ref, x_ref, b1_ref, b2_ref, wt_ref, s_ref,
                   w1_hbm, w2_hbm, o_hbm, w1s_ref, w2s_ref, w1p_ref, w2p_ref, acc_ref, out_ref, wsem, osem):
    i = pl.program_id(0)
    half = 128
    copies = functools.partial(_weight_copies, layer, w1_hbm=w1_hbm, w2_hbm=w2_hbm,
                               w1s_ref=w1s_ref, w2s_ref=w2s_ref, sem=wsem)

    @pl.when(i == 0)
    def _():
        acc_ref[...] = jnp.zeros_like(acc_ref)
        out_ref[...] = jnp.zeros_like(out_ref)
        for cp in copies(te_ref[0]):
            cp.start()

    @pl.when(tf_ref[i] == 1)
    def _():
        for cp in copies(te_ref[i]):
            cp.wait()
        s = s_ref[...].astype(BF16)
        for c in range(2 * D_FF // 256):
            blk = jnp.dot(w1s_ref[:, c * 256:(c + 1) * 256].astype(BF16), s, preferred_element_type=F32)
            w1p_ref[:, c * half:(c + 1) * half] = blk[:, :half].astype(BF16)
            w1p_ref[:, D_FF + c * half:D_FF + (c + 1) * half] = blk[:, half:].astype(BF16)
        w2p_ref[...] = w2s_ref[...].astype(BF16)

        @pl.when(ne_ref[i] >= 0)
        def _():
            for cp in copies(ne_ref[i]):
                cp.start()

    def tile_step(m):
        base = i * MOE_TILE
        prev = (i + 1) % 2
        for r0 in range(0, MOE_TILE, SCATTER_GROUP):
            toks = [pl.multiple_of(src_ref[base + r0 + g] * ROW_CHUNKS, ROW_CHUNKS) for g in range(SCATTER_GROUP)]
            cur = [acc_ref[pl.ds(toks[g], ROW_CHUNKS), :] for g in range(SCATTER_GROUP)]
            add = [out_ref[prev, (r0 + g) * ROW_CHUNKS:(r0 + g + 1) * ROW_CHUNKS, :] for g in range(SCATTER_GROUP)]
            for g in range(SCATTER_GROUP):
                acc_ref[pl.ds(toks[g], ROW_CHUNKS), :] = cur[g] + add[g]
        x = jnp.concatenate([x_ref[pl.ds(j, m, stride=ROW_CHUNKS), :] for j in range(ROW_CHUNKS)], axis=1)
        a = jnp.dot(x.astype(BF16), w1p_ref[...], preferred_element_type=F32) + b1_ref[...]
        glu = jnp.minimum(a[:, :D_FF], SWIGLU_LIMIT)
        lin = jnp.clip(a[:, D_FF:], -SWIGLU_LIMIT, SWIGLU_LIMIT)
        hid = glu * _sigmoid(SWIGLU_ALPHA * glu) * (lin + 1.0)
        out = (jnp.dot(hid.astype(BF16), w2p_ref[...], preferred_element_type=F32) + b2_ref[...]) * wt_ref[0:m, :]
        cur_buf = i % 2
        for j in range(ROW_CHUNKS):
            out_ref[cur_buf, pl.ds(j, m, stride=ROW_CHUNKS), :] = out[:, j * LANES:(j + 1) * LANES]

    live = i <= nu_ref[0]
    pl.when(live & (vr_ref[i] > MOE_TILE // 2))(functools.partial(tile_step, MOE_TILE))
    pl.when(live & (vr_ref[i] <= MOE_TILE // 2))(functools.partial(tile_step, MOE_TILE // 2))

    @pl.when(i == pl.num_programs(0) - 1)
    def _():
        cp = pltpu.make_async_copy(acc_ref.at[pl.ds(0, T_ALL * ROW_CHUNKS)], o_hbm, osem)
        cp.start()
        cp.wait()


def _experts(layer, x_sorted, w_sorted, plan, w1, b1p, w2, b2):
    tile_expert, tile_first, next_expert, n_used, src, x_off, valid_rows = plan
    grid_spec = pltpu.PrefetchScalarGridSpec(
        num_scalar_prefetch=7,
        grid=(MOE_TILES,),
        in_specs=[
            pl.BlockSpec((pl.Element(MOE_TILE * ROW_CHUNKS), pl.Element(LANES)),
                         lambda i, te, tf, ne, nu, src, xo, vr: (pl.multiple_of(xo[i] * ROW_CHUNKS, X_ALIGN * ROW_CHUNKS), 0)),
            pl.BlockSpec((None, None, 1, 2 * D_FF), lambda i, te, *_: (layer, te[i], 0, 0)),
            pl.BlockSpec((None, None, 1, D_MODEL), lambda i, te, *_: (layer, te[i], 0, 0)),
            pl.BlockSpec((MOE_TILE, 1), lambda i, te, *_: (i, 0)),
            pl.BlockSpec((256, 256), lambda i, te, *_: (0, 0)),
            pl.BlockSpec(memory_space=pl.ANY),
            pl.BlockSpec(memory_space=pl.ANY),
        ],
        out_specs=pl.BlockSpec(memory_space=pl.ANY),
        scratch_shapes=[
            pltpu.VMEM((D_MODEL, 2 * D_FF), F32),
            pltpu.VMEM((D_FF, D_MODEL), F32),
            pltpu.VMEM((D_MODEL, 2 * D_FF), BF16),
            pltpu.VMEM((D_FF, D_MODEL), BF16),
            pltpu.VMEM((ACC_ROWS * ROW_CHUNKS, LANES), F32),
            pltpu.VMEM((2, MOE_TILE * ROW_CHUNKS, LANES), F32),
            pltpu.SemaphoreType.DMA((W1_DMA_CHUNKS + W2_DMA_CHUNKS,)),
            pltpu.SemaphoreType.DMA(()),
        ],
    )
    return pl.pallas_call(
        functools.partial(_expert_kernel, layer),
        grid_spec=grid_spec,
        out_shape=jax.ShapeDtypeStruct((T_ALL * ROW_CHUNKS, LANES), F32),
        compiler_params=_cparams(1),
    )(tile_expert, tile_first, next_expert, n_used, src, x_off, valid_rows, x_sorted, b1p, b2, w_sorted,
      _deinterleave_matrix(), w1, w2)


def _combine_kernel(first_tile, y_ref, a_ref, ada_ref, o_ref):
    gate = _ada_chunk(ada_ref, _cond_row(first_tile + pl.program_id(0)), 5)
    acc = jnp.concatenate([a_ref[pl.ds(j, ROW_TILE, stride=ROW_CHUNKS), :] for j in range(ROW_CHUNKS)], axis=1)
    o_ref[...] = y_ref[...] + gate * acc


def _combine(y, acc, ada_l, first_tile=0, n_tiles=N_ROW_TILES):
    return pl.pallas_call(
        functools.partial(_combine_kernel, first_tile),
        grid=(n_tiles,),
        in_specs=[
            pl.BlockSpec((ROW_TILE, D_MODEL), lambda i: (first_tile + i, 0)),
            pl.BlockSpec((ROW_TILE * ROW_CHUNKS, LANES), lambda i: (first_tile + i, 0)),
            pl.BlockSpec((COND_ROWS, ADA_CHUNKS * D_MODEL), lambda i: (0, 0)),
        ],
        out_specs=pl.BlockSpec((ROW_TILE, D_MODEL), lambda i: (i, 0)),
        out_shape=jax.ShapeDtypeStruct((n_tiles * ROW_TILE, D_MODEL), F32),
        compiler_params=_cparams(1),
    )(y, acc, ada_l)


def _routing_plan(idx, wts):
    eid = idx.reshape(-1)
    order = jnp.argsort(eid, stable=True).astype(jnp.int32)
    experts = jnp.arange(N_EXPERTS, dtype=jnp.int32)
    counts = jnp.sum(eid[:, None] == experts[None, :], axis=0).astype(jnp.int32)
    ntiles = (counts + MOE_TILE - 1) // MOE_TILE
    tile_end = jnp.cumsum(ntiles).astype(jnp.int32)
    tile_begin = tile_end - ntiles
    cstarts = (jnp.cumsum(counts) - counts).astype(jnp.int32)
    n_used = tile_end[-1]
    tile = jnp.arange(MOE_TILES, dtype=jnp.int32)
    te = jnp.minimum(jnp.sum(tile[:, None] >= tile_end[None, :], axis=1), N_EXPERTS - 1).astype(jnp.int32)
    used = tile < n_used
    prev = jnp.concatenate([jnp.full((1,), -1, jnp.int32), te[:-1]])
    first = (te != prev) & used

    def pick(onehot, table):
        return jnp.sum(jnp.where(onehot, table[None, :], 0), axis=1).astype(jnp.int32)

    tile_is = te[:, None] == experts[None, :]
    later = (experts[None, :] > experts[:, None]) & (ntiles[None, :] > 0)
    following = jnp.min(jnp.where(later, experts[None, :], N_EXPERTS), axis=1)
    following = jnp.where(following < N_EXPERTS, following, -1)
    next_expert = jnp.where(first, pick(tile_is, following), -1).astype(jnp.int32)
    tile_in_expert = tile - pick(tile_is, tile_begin)
    off = tile_in_expert[:, None] * MOE_TILE + jnp.arange(MOE_TILE, dtype=jnp.int32)[None, :]
    valid_rows = jnp.where(used, jnp.clip(pick(tile_is, counts) - tile_in_expert * MOE_TILE, 0, MOE_TILE), 0)
    valid = jnp.arange(MOE_TILE, dtype=jnp.int32)[None, :] < valid_rows[:, None]
    assign = order[jnp.clip(pick(tile_is, cstarts)[:, None] + off, 0, N_ASSIGN - 1)]
    token = assign // TOP_K
    src = jnp.where(valid, token, SPARE_ROW).reshape(MOE_ROWS).astype(jnp.int32)
    src = jnp.concatenate([jnp.full((MOE_TILE,), SPARE_ROW, jnp.int32), src])
    w_sorted = jnp.where(valid, wts.reshape(-1)[assign], 0.0).reshape(MOE_ROWS, 1)
    seg = ((counts + X_ALIGN - 1) // X_ALIGN) * X_ALIGN
    seg_end = jnp.cumsum(seg).astype(jnp.int32)
    seg_begin = seg_end - seg
    x_off = jnp.where(used, pick(tile_is, seg_begin) + tile_in_expert * MOE_TILE, 0).astype(jnp.int32)
    group = jnp.arange(X_ROWS // X_ALIGN, dtype=jnp.int32) * X_ALIGN
    group_is = (group[:, None] >= seg_begin[None, :]) & (group[:, None] < seg_end[None, :])
    xoffset = (group - pick(group_is, seg_begin))[:, None] + jnp.arange(X_ALIGN, dtype=jnp.int32)[None, :]
    xassign = order[jnp.clip(pick(group_is, cstarts)[:, None] + xoffset, 0, N_ASSIGN - 1)]
    gather_row = jnp.where(xoffset < pick(group_is, counts)[:, None], xassign // TOP_K, 0).reshape(X_ROWS)
    plan = (te, first.astype(jnp.int32), next_expert, n_used.reshape(1), src, x_off, valid_rows.astype(jnp.int32))
    return plan, gather_row, w_sorted


def _moe(layer, y, ada_l, g, router_w, router_b, w1, b1p, w2, b2):
    h, idx_t, wts_t = _router(y, ada_l, g, router_w, router_b)
    plan, gather_row, w_sorted = _routing_plan(idx_t.T, wts_t.T)
    x_sorted = _dispatch(h, gather_row)
    acc = _experts(layer, x_sorted, w_sorted, plan, w1, b1p, w2, b2)
    if layer == DEPTH - 1:
        return (_combine(y, acc, ada_l, 0, P_TILES), _combine(y, acc, ada_l, P_TILES, N_ROW_TILES - P_TILES))
    return _combine(y, acc, ada_l)


def kernel(x_prompt, x_sample, cache_attn_k, cache_attn_v, state_mlstm_C, state_mlstm_n, state_mlstm_m, c, c_ctx, ada_w, ada_b, norm_mix_g, norm_ffn_g, ab_w_in, ab_w_out, da_qnorm_g, da_knorm_g, da_lambda, da_subnorm_g, ml_conv_w, ml_conv_b, ml_gate_b, ml_headnorm_g, hy_w_in, hy_w_out, hy_conv_w, hy_conv_b, hy_f_w1, hy_f_b1, hy_f_freq1, hy_f_w2, hy_f_b2, hy_f_freq2, hy_f_w3, hy_bias, router_w, router_b, moe_w1, moe_b1, moe_w2, moe_b2):
    y = (x_prompt.reshape(T_P, D_MODEL), x_sample.reshape(T_S, D_MODEL))
    cond = jnp.concatenate([c_ctx[None, :], c, jnp.zeros((COND_ROWS - 1 - DEC_BATCH, D_MODEL), F32)], axis=0)
    ada = _ada_table(cond, ada_w, ada_b)
    b1p = moe_b1.reshape(DEPTH, N_EXPERTS, D_FF, 2).swapaxes(2, 3).reshape(DEPTH, N_EXPERTS, 1, 2 * D_FF)
    b2r = moe_b2.reshape(DEPTH, N_EXPERTS, 1, D_MODEL)
    new_k, new_v, new_c, new_n, new_m = [], [], [], [], []
    for layer in range(DEPTH):
        ada_l = ada[layer]
        if layer % 2 == 0:
            e = layer // 2
            lam_init = 0.8 - 0.6 * math.exp(-0.3 * layer)
            qkv, mqk, mv, mo, mg = _modulated_proj(
                y, ada_l, norm_mix_g[layer], ab_w_in[e], (3 * W_A, 2 * W_B, W_B, W_B, 4 * H_B))
            qg2 = jnp.tile(da_qnorm_g[e], 2).reshape(1, 2 * HD_A)
            kg2 = jnp.tile(da_knorm_g[e], 2).reshape(1, 2 * HD_A)
            sub_g = da_subnorm_g[e].reshape(1, 2 * HD_A)
            oa_p, k_norm, v_heads = _attention_prompt(qkv, qg2, kg2, da_lambda[e], sub_g, lam_init)
            cos, sin = _rope_tables()
            oa_s = _attention_sample(
                qkv, cache_attn_k[:, e].reshape(DEC_BATCH, PAST_LEN, W_A),
                cache_attn_v[:, e].reshape(DEC_BATCH, PAST_LEN, W_A), cos, sin,
                qg2, kg2, da_lambda[e], sub_g, lam_init)
            ob_p, c_new, n_new, m_new = _mlstm(
                mqk, mv, mo, mg[:T_P], ml_conv_w[e], ml_conv_b[e].reshape(1, 2 * W_B), ml_gate_b[e],
                ml_headnorm_g[e], seq=SEQ, nbatch=BATCH, row_off=0)
            ob_s = _mlstm(
                mqk, mv, mo, mg[T_P:], ml_conv_w[e], ml_conv_b[e].reshape(1, 2 * W_B), ml_gate_b[e],
                ml_headnorm_g[e], seq=DEC_SEQ, nbatch=DEC_BATCH, row_off=T_P // DEC_SEQ,
                ctx=(state_mlstm_C[:, e], state_mlstm_n[:, e], state_mlstm_m[:, e]))
            y = _out_proj_residual([(oa_p, oa_s), (ob_p, ob_s)], y, ada_l, ab_w_out[e], 2)
            new_k.append(k_norm.reshape(BATCH, SEQ, H_A, 2, HD_A))
            new_v.append(v_heads.reshape(BATCH, SEQ, H_A, 2 * HD_A))
            new_c.append(c_new)
            new_n.append(n_new.reshape(BATCH, 2, H_B, HD_B))
            new_m.append(m_new[..., 0, 0])
        else:
            o = layer // 2
            (zproj,) = _modulated_proj(y, ada_l, norm_mix_g[layer], hy_w_in[o], (HY_PROJ,))
            cores = []
            for seq, nbatch, row_off, td in ((SEQ, BATCH, 0, 512), (DEC_SEQ, DEC_BATCH, T_P // DEC_SEQ, 256)):
                fwd, inv = _dft_mats(seq)
                fwd_bf, inv_bf = fwd.astype(BF16), inv.astype(BF16)
                kspec = _hyena_filter_spectrum(seq, fwd_bf, hy_f_w1[o], hy_f_b1[o], hy_f_freq1[o], hy_f_w2[o],
                                               hy_f_b2[o], hy_f_freq2[o], hy_f_w3[o])
                cores.append(_hyena_core(zproj, hy_conv_w[o], hy_conv_b[o].reshape(1, HY_PROJ), fwd_bf, inv_bf,
                                         kspec, hy_bias[o], seq=seq, nbatch=nbatch, row_off=row_off, td=td))
            y = _out_proj_residual([tuple(cores)], y, ada_l, hy_w_out[o], 2)
        y = _moe(layer, y, ada_l, norm_ffn_g[layer], router_w[layer], router_b[layer],
                 moe_w1, b1p, moe_w2, b2r)
    y_p = y[0].reshape(BATCH, SEQ, D_MODEL)
    y_s = y[1].reshape(DEC_BATCH, DEC_SEQ, D_MODEL)
    return (y_p, y_s, jnp.stack(new_k, axis=1), jnp.stack(new_v, axis=1), jnp.stack(new_c, axis=1),
            jnp.stack(new_n, axis=1), jnp.stack(new_m, axis=1))
```

```python
import functools
import math

import numpy as np
import jax
import jax.numpy as jnp
from jax import lax
from jax.experimental import pallas as pl
from jax.experimental.pallas import tpu as pltpu

D_MODEL = 1024
BATCH = 16
SEQ = 256
DEPTH = 2
DEC_BATCH = 2
DEC_SEQ = 1024
PAST_LEN = 256
GRID_W = 64
W_A = D_MODEL // 2
HD_A = 64
H_A = W_A // (2 * HD_A)
W_B = D_MODEL - W_A
HD_B = 128
H_B = W_B // HD_B
AB_PROJ = 3 * W_A + 4 * W_B + 4 * H_B
ROPE_BASE = 10000.0
CHUNK = 64
HY_ORDER = 2
HY_PROJ = (HY_ORDER + 1) * D_MODEL
HY_BANDS = 8
HY_FH = 64
HY_TARGET = 1e-2
HY_FAST_PCT = 0.3
HY_SLOW_PCT = 1.5
N_EXPERTS = 32
TOP_K = 4
D_FF = D_MODEL
SWIGLU_ALPHA = 1.702
SWIGLU_LIMIT = 7.0
ADA_CHUNKS = 6
EPS = 1e-6
NEG = -1e30
F32 = jnp.float32
BF16 = jnp.bfloat16

T_P = BATCH * SEQ
T_S = DEC_BATCH * DEC_SEQ
T_ALL = T_P + T_S
ROW_TILE = 256
N_ROW_TILES = T_ALL // ROW_TILE
P_TILES = T_P // ROW_TILE
S_TILES_PER_BATCH = DEC_SEQ // ROW_TILE
COND_ROWS = 8
MOE_TILE = 256
N_ASSIGN = T_ALL * TOP_K
MOE_ROWS = N_ASSIGN + N_EXPERTS * MOE_TILE
MOE_TILES = MOE_ROWS // MOE_TILE
X_ALIGN = 16
MLSTM_GROUP = 2
GATHER_ROWS = 1024
X_ROWS = -(-(N_ASSIGN + N_EXPERTS * X_ALIGN + MOE_TILE) // GATHER_ROWS) * GATHER_ROWS
SPARE_ROW = T_ALL
ACC_ROWS = T_ALL + 8
SCATTER_GROUP = 8
W1_DMA_CHUNKS = 8
W2_DMA_CHUNKS = 4
LANES = 128
ROW_CHUNKS = D_MODEL // LANES
VMEM_LIMIT = 56 * 1024 * 1024
HIGHEST = lax.Precision.HIGHEST


def _cparams(n_axes):
    return pltpu.CompilerParams(dimension_semantics=("arbitrary",) * n_axes,
                                vmem_limit_bytes=VMEM_LIMIT)


def _bdot(a, b):
    return jnp.dot(a.astype(BF16), b.astype(BF16), preferred_element_type=F32)


def _cond_row(i):
    return jnp.where(i < P_TILES, 0, 1 + (i - P_TILES) // S_TILES_PER_BATCH)


def _ada_chunk(ada_ref, row, j):
    return ada_ref[pl.ds(row, 1), j * D_MODEL:(j + 1) * D_MODEL]


def _modulate(x, g, shift, scale):
    ms = jnp.mean(x * x, axis=-1, keepdims=True)
    return (x * lax.rsqrt(ms + EPS) * g) * (1.0 + scale) + shift


def _sigmoid(x):
    return 1.0 / (1.0 + jnp.exp(-x))


def _silu(x):
    return x * _sigmoid(x)


def _log_sigmoid(x):
    return jnp.minimum(x, 0.0) - jnp.log(1.0 + jnp.exp(-jnp.abs(x)))


def _dwconv3(x, w, b, seq=None):
    n = x.shape[0]
    seq = n if seq is None else seq
    pos = lax.broadcasted_iota(jnp.int32, x.shape, 0) % seq
    prev = jnp.where(pos == 0, 0.0, pltpu.roll(x, 1, 0))
    nxt = jnp.where(pos == seq - 1, 0.0, pltpu.roll(x, n - 1, 0))
    return prev * w[0:1] + x * w[1:2] + nxt * w[2:3] + b


def _ada_kernel(cond_ref, w_ref, b_ref, o_ref):
    c = _silu(cond_ref[...])
    o_ref[...] = _bdot(c, w_ref[...]) + b_ref[...]


def _ada_table(cond, ada_w, ada_b):
    tn = 1536
    return pl.pallas_call(
        _ada_kernel,
        grid=(DEPTH, ADA_CHUNKS * D_MODEL // tn),
        in_specs=[
            pl.BlockSpec((COND_ROWS, D_MODEL), lambda l, j: (0, 0)),
            pl.BlockSpec((None, D_MODEL, tn), lambda l, j: (l, 0, j)),
            pl.BlockSpec((None, 1, tn), lambda l, j: (l, 0, j)),
        ],
        out_specs=pl.BlockSpec((None, COND_ROWS, tn), lambda l, j: (l, 0, j)),
        out_shape=jax.ShapeDtypeStruct((DEPTH, COND_ROWS, ADA_CHUNKS * D_MODEL), F32),
        compiler_params=_cparams(2),
    )(cond, ada_w, ada_b.reshape(DEPTH, 1, ADA_CHUNKS * D_MODEL))


def _stream_specs(y):
    if isinstance(y, tuple):
        return [pl.BlockSpec((ROW_TILE, D_MODEL), lambda i: (jnp.minimum(i, P_TILES - 1), 0)),
                pl.BlockSpec((ROW_TILE, D_MODEL), lambda i: (jnp.maximum(i - P_TILES, 0), 0))], list(y)
    return [pl.BlockSpec((ROW_TILE, D_MODEL), lambda i: (i, 0))], [y]


def _stream_tile(y_refs, i):
    if len(y_refs) == 2:
        return jnp.where(i < P_TILES, y_refs[0][...], y_refs[1][...])
    return y_refs[0][...]


def _proj_kernel(splits, n_y, *refs):
    y_refs = refs[:n_y]
    ada_ref, g_ref, w_ref = refs[n_y:n_y + 3]
    out_refs, wbf_ref = refs[n_y + 3:-1], refs[-1]
    i = pl.program_id(0)

    @pl.when(i == 0)
    def _():
        wbf_ref[...] = w_ref[...].astype(BF16)

    row = _cond_row(i)
    h = _modulate(_stream_tile(y_refs, i), g_ref[...], _ada_chunk(ada_ref, row, 0), _ada_chunk(ada_ref, row, 1))
    h = h.astype(BF16)
    lo = 0
    for o_ref, width in zip(out_refs, splits):
        o_ref[...] = jnp.dot(h, wbf_ref[:, lo:lo + width], preferred_element_type=F32)
        lo += width


def _modulated_proj(y, ada_l, g, w, splits):
    n = w.shape[1]
    y_specs, y_args = _stream_specs(y)
    return pl.pallas_call(
        functools.partial(_proj_kernel, splits, len(y_args)),
        grid=(N_ROW_TILES,),
        in_specs=y_specs + [
            pl.BlockSpec((COND_ROWS, ADA_CHUNKS * D_MODEL), lambda i: (0, 0)),
            pl.BlockSpec((1, D_MODEL), lambda i: (0, 0)),
            pl.BlockSpec((D_MODEL, n), lambda i: (0, 0), pipeline_mode=pl.Buffered(1)),
        ],
        out_specs=[pl.BlockSpec((ROW_TILE, s), lambda i: (i, 0)) for s in splits],
        out_shape=[jax.ShapeDtypeStruct((T_ALL, s), F32) for s in splits],
        scratch_shapes=[pltpu.VMEM((D_MODEL, n), BF16)],
        compiler_params=_cparams(1),
    )(*y_args, ada_l, g.reshape(1, D_MODEL), w)


def _out_proj_kernel(n_in, n_y, gate_chunk, *refs):
    x_refs = refs[:2 * n_in]
    y_refs = refs[2 * n_in:2 * n_in + n_y]
    ada_ref, w_ref, o_ref, wbf_ref = refs[2 * n_in + n_y:]
    i = pl.program_id(0)

    @pl.when(i == 0)
    def _():
        wbf_ref[...] = w_ref[...].astype(BF16)

    acc = None
    lo = 0
    for xp_ref, xs_ref in zip(x_refs[0::2], x_refs[1::2]):
        k = xp_ref.shape[1]
        x = jnp.where(i < P_TILES, xp_ref[...], xs_ref[...])
        part = jnp.dot(x.astype(BF16), wbf_ref[lo:lo + k, :], preferred_element_type=F32)
        acc = part if acc is None else acc + part
        lo += k
    gate = _ada_chunk(ada_ref, _cond_row(i), gate_chunk)
    o_ref[...] = _stream_tile(y_refs, i) + gate * acc


def _out_proj_residual(xs, y, ada_l, w, gate_chunk):
    y_specs, y_args = _stream_specs(y)
    x_specs = []
    for xp, _ in xs:
        x_specs.append(pl.BlockSpec((ROW_TILE, xp.shape[1]), lambda i: (jnp.minimum(i, P_TILES - 1), 0)))
        x_specs.append(pl.BlockSpec((ROW_TILE, xp.shape[1]), lambda i: (jnp.maximum(i - P_TILES, 0), 0)))
    return pl.pallas_call(
        functools.partial(_out_proj_kernel, len(xs), len(y_args), gate_chunk),
        grid=(N_ROW_TILES,),
        in_specs=x_specs + y_specs + [
            pl.BlockSpec((COND_ROWS, ADA_CHUNKS * D_MODEL), lambda i: (0, 0)),
            pl.BlockSpec((D_MODEL, D_MODEL), lambda i: (0, 0), pipeline_mode=pl.Buffered(1)),
        ],
        out_specs=pl.BlockSpec((ROW_TILE, D_MODEL), lambda i: (i, 0)),
        out_shape=jax.ShapeDtypeStruct((T_ALL, D_MODEL), F32),
        scratch_shapes=[pltpu.VMEM((D_MODEL, D_MODEL), BF16)],
        compiler_params=_cparams(1),
    )(*[a for pair in xs for a in pair], *y_args, ada_l, w)


def _subhead_norm(x, g2):
    lane = lax.broadcasted_iota(jnp.int32, x.shape, 1)
    first = lane < HD_A
    xx = x * x
    s0 = jnp.sum(jnp.where(first, xx, 0.0), axis=-1, keepdims=True)
    s1 = jnp.sum(jnp.where(first, 0.0, xx), axis=-1, keepdims=True)
    r = jnp.where(first, lax.rsqrt(s0 / HD_A + EPS), lax.rsqrt(s1 / HD_A + EPS))
    return x * r * g2


def _rope(x, cos, sin):
    quarter = HD_A // 4
    lane = lax.broadcasted_iota(jnp.int32, x.shape, 1)
    lower = (lane % (2 * quarter)) < quarter
    swapped = jnp.where(lower, pltpu.roll(x, 2 * HD_A - quarter, 1), pltpu.roll(x, quarter, 1))
    return x * cos + swapped * sin


def _attn_kernel(lam_init, has_ctx, *refs):
    if has_ctx:
        (q_ref, k_ref, v_ref, ck_ref, cv_ref, cq_ref, sq_ref, ckk_ref, skk_ref,
         qg_ref, kg_ref, lp_ref, sg_ref, o_ref, kall_ref, vall_ref) = refs
    else:
        q_ref, k_ref, v_ref, qg_ref, kg_ref, lp_ref, sg_ref, o_ref, kn_ref, vh_ref = refs
    lp = lp_ref[...]
    lam = (jnp.exp(jnp.sum(lp[0:1] * lp[1:2], axis=-1, keepdims=True))
           - jnp.exp(jnp.sum(lp[2:3] * lp[3:4], axis=-1, keepdims=True)) + lam_init)

    def attend(q, k, v):
        probs = []
        for c in range(2):
            qc = q[:, c * HD_A:(c + 1) * HD_A].astype(BF16)
            kc = k[:, c * HD_A:(c + 1) * HD_A].astype(BF16)
            s = lax.dot_general(qc, kc, (((1,), (1,)), ((), ())), preferred_element_type=F32) * (HD_A ** -0.5)
            e = jnp.exp(s - jnp.max(s, axis=-1, keepdims=True))
            probs.append(e / jnp.sum(e, axis=-1, keepdims=True))
        o = _bdot(probs[0] - lam * probs[1], v)
        ms = jnp.mean(o * o, axis=-1, keepdims=True)
        return (o * lax.rsqrt(ms + EPS) * sg_ref[...]) * (1.0 - lam_init)

    if not has_ctx:
        for h in range(H_A):
            cols = slice(h * 2 * HD_A, (h + 1) * 2 * HD_A)
            k = _subhead_norm(k_ref[:, cols], kg_ref[...])
            for c in range(2):
                kn_ref[pl.ds(2 * h + c, SEQ, stride=2 * H_A), :] = k[:, c * HD_A:(c + 1) * HD_A]
            v = v_ref[:, cols]
            vh_ref[pl.ds(h, SEQ, stride=H_A), :] = v
            o_ref[:, cols] = attend(_subhead_norm(q_ref[:, cols], qg_ref[...]), k, v)
        return

    @pl.when(pl.program_id(2) == 0)
    def _():
        kall_ref[0:PAST_LEN, :] = ck_ref[...].astype(BF16)
        vall_ref[0:PAST_LEN, :] = cv_ref[...].astype(BF16)
        k_new = _rope(_subhead_norm(k_ref[...], kg_ref[...]), ckk_ref[...], skk_ref[...])
        kall_ref[PAST_LEN:, :] = k_new.astype(BF16)
        vall_ref[PAST_LEN:, :] = v_ref[...].astype(BF16)

    q = _rope(_subhead_norm(q_ref[...], qg_ref[...]), cq_ref[...], sq_ref[...])
    o_ref[...] = attend(q, kall_ref[...], vall_ref[...])


def _attention_prompt(qkv, qg2, kg2, lam_p, sub_g, lam_init):
    head = 2 * HD_A
    small = [
        pl.BlockSpec((1, head), lambda b: (0, 0)),
        pl.BlockSpec((1, head), lambda b: (0, 0)),
        pl.BlockSpec((4, HD_A), lambda b: (0, 0)),
        pl.BlockSpec((1, head), lambda b: (0, 0)),
    ]
    return pl.pallas_call(
        functools.partial(_attn_kernel, lam_init, False),
        grid=(BATCH,),
        in_specs=[
            pl.BlockSpec((SEQ, W_A), lambda b: (b, 0)),
            pl.BlockSpec((SEQ, W_A), lambda b: (b, 1)),
            pl.BlockSpec((SEQ, W_A), lambda b: (b, 2)),
        ] + small,
        out_specs=[pl.BlockSpec((SEQ, W_A), lambda b: (b, 0)),
                   pl.BlockSpec((SEQ * 2 * H_A, HD_A), lambda b: (b, 0)),
                   pl.BlockSpec((SEQ * H_A, head), lambda b: (b, 0))],
        out_shape=[jax.ShapeDtypeStruct((T_P, W_A), F32), jax.ShapeDtypeStruct((T_P * 2 * H_A, HD_A), F32),
                   jax.ShapeDtypeStruct((T_P * H_A, head), F32)],
        compiler_params=_cparams(1),
    )(qkv, qkv, qkv, qg2, kg2, lam_p, sub_g)


def _attention_sample(qkv, cache_k, cache_v, cos, sin, qg2, kg2, lam_p, sub_g, lam_init):
    nh = H_A
    head = 2 * HD_A
    tq = ROW_TILE
    nq = DEC_SEQ // tq
    q_off = T_P // tq
    k_off = T_P // DEC_SEQ
    small = [
        pl.BlockSpec((1, head), lambda b, h, i: (0, 0)),
        pl.BlockSpec((1, head), lambda b, h, i: (0, 0)),
        pl.BlockSpec((4, HD_A), lambda b, h, i: (0, 0)),
        pl.BlockSpec((1, head), lambda b, h, i: (0, 0)),
    ]
    return pl.pallas_call(
        functools.partial(_attn_kernel, lam_init, True),
        grid=(DEC_BATCH, nh, nq),
        in_specs=[
            pl.BlockSpec((tq, head), lambda b, h, i: (q_off + b * nq + i, h)),
            pl.BlockSpec((DEC_SEQ, head), lambda b, h, i: (k_off + b, nh + h)),
            pl.BlockSpec((DEC_SEQ, head), lambda b, h, i: (k_off + b, 2 * nh + h)),
            pl.BlockSpec((None, PAST_LEN, head), lambda b, h, i: (b, 0, h)),
            pl.BlockSpec((None, PAST_LEN, head), lambda b, h, i: (b, 0, h)),
            pl.BlockSpec((tq, head), lambda b, h, i: (i, 0)),
            pl.BlockSpec((tq, head), lambda b, h, i: (i, 0)),
            pl.BlockSpec((DEC_SEQ, head), lambda b, h, i: (0, 0)),
            pl.BlockSpec((DEC_SEQ, head), lambda b, h, i: (0, 0)),
        ] + small,
        out_specs=pl.BlockSpec((tq, head), lambda b, h, i: (b * nq + i, h)),
        out_shape=jax.ShapeDtypeStruct((T_S, W_A), F32),
        scratch_shapes=[pltpu.VMEM((PAST_LEN + DEC_SEQ, head), BF16)] * 2,
        compiler_params=_cparams(3),
    )(qkv, qkv, qkv, cache_k, cache_v, cos, sin, cos, sin, qg2, kg2, lam_p, sub_g)


def _rope_tables():
    half = HD_A // 2
    nf = half // 2
    inv = ROPE_BASE ** (-np.arange(nf, dtype=np.float32) / nf)
    pos = np.arange(DEC_SEQ)
    row = (pos // GRID_W).astype(np.float32)
    col = (pos % GRID_W).astype(np.float32)
    ang_r = (row[:, None] * inv).astype(np.float32)
    ang_c = (col[:, None] * inv).astype(np.float32)
    ang = np.concatenate([ang_r, ang_r, ang_c, ang_c], axis=1)
    sign = np.concatenate([-np.ones(nf), np.ones(nf), -np.ones(nf), np.ones(nf)]).astype(np.float32)
    cos = np.cos(ang.astype(np.float64)).astype(np.float32)
    sin = (np.sin(ang.astype(np.float64)) * sign).astype(np.float32)
    return jnp.asarray(np.tile(cos, (1, 2))), jnp.asarray(np.tile(sin, (1, 2)))


def _mlstm_kernel(seq, has_ctx, group, *refs):
    if has_ctx:
        (q_ref, k_ref, cwq_ref, cwk_ref, cbq_ref, cbk_ref, v_ref, mo_ref, gi_ref, gf_ref,
         gbi_ref, gbf_ref, hn_ref, c0_ref, n0_ref, m0_ref, o_ref,
         qs_ref, ks_ref, hf_ref, hb_ref, cs_ref, rrow_ref, col_ref, wc_ref) = refs
    else:
        (q_ref, k_ref, cwq_ref, cwk_ref, cbq_ref, cbk_ref, v_ref, mo_ref, gi_ref, gf_ref,
         gbi_ref, gbf_ref, hn_ref, o_ref, c_out_ref, n_out_ref, m_out_ref,
         qs_ref, ks_ref, hf_ref, hb_ref, cs_ref, rrow_ref, col_ref, wc_ref) = refs
    nc = seq // CHUNK
    n_chain = 2 * H_B
    chains = [(sub, d, h) for sub in range(group) for d in range(2) for h in range(H_B)]
    qs_ref[...] = _silu(_dwconv3(q_ref[...], cwq_ref[...], cbq_ref[...], seq)) * (HD_B ** -0.5)
    ks_ref[...] = _silu(_dwconv3(k_ref[...], cwk_ref[...], cbk_ref[...], seq))

    rows = nc * n_chain
    lane = lax.broadcasted_iota(jnp.int32, (rows, 2 * CHUNK), 1)
    forward = lax.broadcasted_iota(jnp.int32, (rows, 2 * CHUNK), 0) % n_chain < H_B
    valid = lane < CHUNK

    def scan(x, op, fill):
        pre, suf = x, x
        sh = 1
        while sh < CHUNK:
            pre = op(pre, jnp.where(lane >= sh, pltpu.roll(pre, sh, 1), fill))
            suf = op(suf, jnp.where(lane + sh < CHUNK, pltpu.roll(suf, 2 * CHUNK - sh, 1), fill))
            sh *= 2
        return jnp.where(forward, pre, suf)

    mm_final = []
    for sub in range(group):
        gate_i = (gi_ref[sub] + gbi_ref[...]).reshape(rows, 2 * CHUNK)
        lf = jnp.where(valid, _log_sigmoid(gf_ref[sub] + gbf_ref[...]).reshape(rows, 2 * CHUNK), 0.0)
        b = scan(lf, jnp.add, 0.0)
        cmax = scan(jnp.where(valid, gate_i - b, -jnp.inf), jnp.maximum, -jnp.inf)
        b_last = jnp.sum(lf, axis=1, keepdims=True)
        g = b_last - b + gate_i
        g_max = jnp.max(jnp.where(valid, g, -jnp.inf), axis=1, keepdims=True)
        mm = m0_ref[sub] if has_ctx else jnp.zeros((n_chain, 1), F32)
        mm_seq = []
        for p in range(nc):
            mm_seq.append(mm)
            seg = slice(p * n_chain, (p + 1) * n_chain)
            mm = jnp.maximum(b_last[seg] + mm, g_max[seg])
        mm_final.append(mm)
        mm_prev = jnp.concatenate(mm_seq, axis=0)
        mm_next = jnp.concatenate(mm_seq[1:] + [mm], axis=0)
        m_t = jnp.maximum(b + mm_prev, b + cmax)
        rrow_ref[sub] = (b - gate_i).reshape(nc, n_chain, 2 * CHUNK)
        wc_ref[sub] = jnp.exp(b_last + mm_prev - mm_next).reshape(nc, n_chain, 1)
        per_row = [b, m_t, jnp.exp(b + mm_prev - m_t), jnp.exp(-m_t), jnp.exp(g - mm_next)]
        for j, arr in enumerate(per_row):
            by_time = arr.T
            for p in range(nc):
                col_ref[sub, p, :, j * n_chain:(j + 1) * n_chain] = by_time[0:CHUNK, p * n_chain:(p + 1) * n_chain]

    t_idx = lax.broadcasted_iota(jnp.int32, (CHUNK, CHUNK), 0)
    s_idx = lax.broadcasted_iota(jnp.int32, (CHUNK, CHUNK), 1)
    for n, (sub, d, h) in enumerate(chains):
        cs_ref[n] = c0_ref[sub, d, h] if has_ctx else jnp.zeros((HD_B, HD_B), F32)

    def out_step(p, n_states):
        new_states = []
        for n, (sub, d, h) in enumerate(chains):
            cols = col_ref[sub, p]
            rrows = rrow_ref[sub, p]
            wcs = wc_ref[sub, p]
            n_loc = d * H_B + h
            c = p if d == 0 else nc - 1 - p
            r0 = pl.multiple_of(sub * seq + c * CHUNK, CHUNK)
            hcols = slice(h * HD_B, (h + 1) * HD_B)
            qt = qs_ref[pl.ds(r0, CHUNK), hcols]
            kt = ks_ref[pl.ds(r0, CHUNK), hcols]
            vt = v_ref[pl.ds(r0, CHUNK), hcols]
            b_col, m_t, w_inter, e_inv, w_k = (cols[:, j * n_chain + n_loc:j * n_chain + n_loc + 1] for j in range(5))
            mask = (s_idx <= t_idx) if d == 0 else (s_idx >= t_idx)
            decay = jnp.exp(jnp.where(mask, b_col - rrows[n_loc:n_loc + 1, 0:CHUNK], NEG) - m_t)
            qk = lax.dot_general(qt.astype(BF16), kt.astype(BF16), (((1,), (1,)), ((), ())),
                                 preferred_element_type=F32)
            s = qk * decay
            cm = cs_ref[n]
            nm = n_states[n]
            cq = lax.dot_general(qt.astype(BF16), cm.astype(BF16), (((1,), (1,)), ((), ())),
                                 preferred_element_type=F32)
            num = _bdot(s, vt) + w_inter * cq
            nq = jnp.sum(s, axis=-1, keepdims=True) + w_inter * jnp.sum(qt * nm, axis=-1, keepdims=True)
            hdir_ref = hf_ref if d == 0 else hb_ref
            hdir_ref[pl.ds(r0, CHUNK), hcols] = num / jnp.maximum(jnp.abs(nq), e_inv)
            w_c = wcs[n_loc:n_loc + 1, :]
            vw = (vt * w_k).astype(BF16)
            cs_ref[n] = w_c * cm + lax.dot_general(vw, kt.astype(BF16), (((0,), (0,)), ((), ())),
                                                   preferred_element_type=F32)
            new_states.append(w_c * nm + jnp.sum(kt * w_k, axis=0, keepdims=True))
        return tuple(new_states)

    if has_ctx:
        n_init = tuple(n0_ref[sub, d, h] for sub, d, h in chains)
    else:
        n_init = tuple(jnp.zeros((1, HD_B), F32) for _ in chains)
    n_final = lax.fori_loop(0, nc, out_step, n_init)
    if not has_ctx:
        for n, (sub, d, h) in enumerate(chains):
            n_loc = d * H_B + h
            c_out_ref[sub, d, h] = cs_ref[n]
            n_out_ref[sub, d, h] = n_final[n]
            m_out_ref[sub, d, h] = jnp.broadcast_to(mm_final[sub][n_loc:n_loc + 1, :], (1, HD_B))

    for h in range(H_B):
        hcols = slice(h * HD_B, (h + 1) * HD_B)
        hh = hf_ref[:, hcols] + hb_ref[:, hcols]
        ms = jnp.mean(hh * hh, axis=-1, keepdims=True)
        o_ref[:, hcols] = (hh * lax.rsqrt(ms + EPS) * hn_ref[:, hcols]) * _sigmoid(mo_ref[:, hcols])


def _mlstm(mqk, mv, mo, mg_stream, conv_w, conv_b, gate_b, hn_g, *, seq, nbatch, row_off, group, ctx=None):
    nh = H_B
    nc = seq // CHUNK
    has_ctx = ctx is not None
    assert nbatch % group == 0
    gt = mg_stream.reshape(nbatch, nc, CHUNK, 2, 2, nh).transpose(0, 1, 3, 4, 5, 2)
    pad = ((0, 0), (0, 0), (0, 0), (0, CHUNK))
    gates = [jnp.pad(jnp.concatenate([gt[:, :, 0, j], gt[:, ::-1, 1, j]], axis=2), pad) for j in range(2)]
    gate_bias = [jnp.concatenate([gate_b[0, j], gate_b[1, j]]).reshape(2 * nh, 1) for j in range(2)]
    blk = lambda col: pl.BlockSpec((group * seq, W_B), lambda b, col=col: (row_off + b, col))
    gate_blk = pl.BlockSpec((group, nc, 2 * nh, 2 * CHUNK), lambda b: (b, 0, 0, 0))
    in_specs = [
        blk(0), blk(1),
        pl.BlockSpec((3, W_B), lambda b: (0, 0)),
        pl.BlockSpec((3, W_B), lambda b: (0, 1)),
        pl.BlockSpec((1, W_B), lambda b: (0, 0)),
        pl.BlockSpec((1, W_B), lambda b: (0, 1)),
        blk(0), blk(0),
        gate_blk, gate_blk,
        pl.BlockSpec((2 * nh, 1), lambda b: (0, 0)),
        pl.BlockSpec((2 * nh, 1), lambda b: (0, 0)),
        pl.BlockSpec((1, W_B), lambda b: (0, 0)),
    ]
    args = [mqk, mqk, conv_w, conv_w, conv_b, conv_b, mv, mo, gates[0], gates[1],
            gate_bias[0], gate_bias[1], hn_g.reshape(1, W_B)]
    o_spec = pl.BlockSpec((group * seq, W_B), lambda b: (b, 0))
    o_shape = jax.ShapeDtypeStruct((nbatch * seq, W_B), F32)
    state_blk = lambda rows: pl.BlockSpec((group, 2, nh, rows, HD_B), lambda b: (b, 0, 0, 0, 0))
    if has_ctx:
        c0, n0, m0 = ctx
        in_specs += [state_blk(HD_B), state_blk(1), pl.BlockSpec((group, 2 * nh, 1), lambda b: (b, 0, 0))]
        args += [c0, n0.reshape(nbatch, 2, nh, 1, HD_B), m0.reshape(nbatch, 2 * nh, 1)]
        out_specs, out_shape = o_spec, o_shape
    else:
        out_specs = [o_spec, state_blk(HD_B), state_blk(1), state_blk(1)]
        out_shape = [
            o_shape,
            jax.ShapeDtypeStruct((nbatch, 2, nh, HD_B, HD_B), F32),
            jax.ShapeDtypeStruct((nbatch, 2, nh, 1, HD_B), F32),
            jax.ShapeDtypeStruct((nbatch, 2, nh, 1, HD_B), F32),
        ]
    return pl.pallas_call(
        functools.partial(_mlstm_kernel, seq, has_ctx, group),
        grid=(nbatch // group,),
        in_specs=in_specs,
        out_specs=out_specs,
        out_shape=out_shape,
        scratch_shapes=[pltpu.VMEM((group * seq, W_B), F32)] * 4 + [
            pltpu.VMEM((group * 2 * nh, HD_B, HD_B), F32),
            pltpu.VMEM((group, nc, 2 * nh, 2 * CHUNK), F32),
            pltpu.VMEM((group, nc, CHUNK, 5 * 2 * nh), F32),
            pltpu.VMEM((group, nc, 2 * nh, 1), F32),
        ],
        compiler_params=_cparams(1),
    )(*args)


def _dft_mats(L):
    f = np.arange(L)[:, None]
    j = np.arange(L)[None, :]
    ang = 2.0 * np.pi * ((f * j) % (2 * L)) / (2 * L)
    cm = np.cos(ang)
    sm = np.sin(ang)
    alt = (1.0 - 2.0 * (np.arange(L) % 2))
    fwd_b = -sm
    fwd_b[0, :] = alt
    fwd = np.concatenate([cm, fwd_b], axis=0)
    wgt = np.where(np.arange(L) == 0, 1.0, 2.0)[None, :]
    inv_a = cm.T * wgt
    inv_b = -2.0 * sm.T
    inv_b[:, 0] = alt
    inv = np.concatenate([inv_a, inv_b], axis=1) / (2 * L)
    return jnp.asarray(fwd.astype(np.float32)), jnp.asarray(inv.astype(np.float32))


def _hyena_feats(L):
    t = np.linspace(0.0, 1.0, L, dtype=np.float32)
    wpos = (2.0 * math.pi * np.arange(L, dtype=np.float32) / L).astype(np.float32)
    fb = np.linspace(1e-4, HY_BANDS - 1, HY_BANDS, dtype=np.float32)
    z = (wpos[:, None] * fb).astype(np.float32)
    feats = np.concatenate([t[:, None], np.cos(z), -np.sin(z)], axis=-1).astype(np.float32)
    deltas = np.abs(np.linspace(math.log(HY_TARGET) / HY_SLOW_PCT, math.log(HY_TARGET) / HY_FAST_PCT,
                                D_MODEL, dtype=np.float32))
    decay = np.exp(-t[:, None] * deltas).astype(np.float32)
    return jnp.asarray(feats), jnp.asarray(decay)


def _filter_kernel(L, feats_ref, w1_ref, b1_ref, fr1_ref, w2_ref, b2_ref, fr2_ref, w3f_ref, w3b_ref,
                   decay_ref, fwd_ref, o_ref, hdn_ref):
    @pl.when((pl.program_id(0) == 0) & (pl.program_id(1) == 0))
    def _():
        h1 = jnp.sin(fr1_ref[...] * (jnp.dot(feats_ref[...], w1_ref[...], precision=HIGHEST,
                                             preferred_element_type=F32) + b1_ref[...]))
        hdn_ref[...] = jnp.sin(fr2_ref[...] * (jnp.dot(h1, w2_ref[...], precision=HIGHEST,
                                                       preferred_element_type=F32) + b2_ref[...]))

    hdn = hdn_ref[...]
    decay = decay_ref[...]
    f_fwd = jnp.dot(hdn, w3f_ref[...], precision=HIGHEST, preferred_element_type=F32) * decay
    f_bwd = jnp.dot(hdn, w3b_ref[...], precision=HIGHEST, preferred_element_type=F32) * decay
    row = lax.broadcasted_iota(jnp.int32, f_bwd.shape, 0)
    f_bwd = jnp.where(row == 0, 0.0, f_bwd)
    f_sum = f_fwd + f_bwd
    o_ref[0:L, :] = _bdot(fwd_ref[0:L, :], f_sum)
    imag = _bdot(fwd_ref[L:2 * L, :], f_fwd - f_bwd)
    nyquist = _bdot(fwd_ref[L:L + 16, :], f_sum)[0:1]
    o_ref[L:2 * L, :] = jnp.where(row == 0, nyquist, imag)


def _hyena_filter_spectrum(L, fwd_bf, w1, b1, fr1, w2, b2, fr2, w3):
    feats, decay = _hyena_feats(L)
    td = 512
    nd = D_MODEL // td
    emb = feats.shape[1]
    vec = lambda a: a.reshape(1, HY_FH)
    full = lambda shape: pl.BlockSpec(shape, lambda o, j: (0, 0))
    return pl.pallas_call(
        functools.partial(_filter_kernel, L),
        grid=(HY_ORDER, nd),
        in_specs=[
            full((L, emb)), full((emb, HY_FH)), full((1, HY_FH)), full((1, HY_FH)),
            full((HY_FH, HY_FH)), full((1, HY_FH)), full((1, HY_FH)),
            pl.BlockSpec((HY_FH, td), lambda o, j: (0, o * 2 * nd + j)),
            pl.BlockSpec((HY_FH, td), lambda o, j: (0, o * 2 * nd + nd + j)),
            pl.BlockSpec((L, td), lambda o, j: (0, j)),
            full((2 * L, L)),
        ],
        out_specs=pl.BlockSpec((2 * L, td), lambda o, j: (0, o * nd + j)),
        out_shape=jax.ShapeDtypeStruct((2 * L, HY_ORDER * D_MODEL), F32),
        scratch_shapes=[pltpu.VMEM((L, HY_FH), F32)],
        compiler_params=_cparams(2),
    )(feats, w1, vec(b1), vec(fr1), w2, vec(b2), vec(fr2), w3, w3, decay, fwd_bf)


def _spectral_conv(u, fwd, inv, kspec, L):
    uf = jnp.dot(fwd, u.astype(BF16), preferred_element_type=F32)
    ua, ub = uf[0:L], uf[L:2 * L]
    ka, kb = kspec[0:L], kspec[L:2 * L]
    first = lax.broadcasted_iota(jnp.int32, ua.shape, 0) == 0
    ya = ua * ka - jnp.where(first, 0.0, ub * kb)
    yb = jnp.where(first, ub * kb, ua * kb + ub * ka)
    y = jnp.concatenate([ya, yb], axis=0).astype(BF16)
    return jnp.dot(inv, y, preferred_element_type=F32)


def _hyena_kernel(L, zv_ref, z1_ref, z2_ref, cwv_ref, cw1_ref, cw2_ref, cbv_ref, cb1_ref, cb2_ref,
                  fwd_ref, inv_ref, k0_ref, k1_ref, bias0_ref, bias1_ref, o_ref):
    fwd = fwd_ref[...]
    inv = inv_ref[...]
    v = _dwconv3(zv_ref[...], cwv_ref[...], cbv_ref[...])
    x1 = _dwconv3(z1_ref[...], cw1_ref[...], cb1_ref[...])
    x2 = _dwconv3(z2_ref[...], cw2_ref[...], cb2_ref[...])
    z = x1 * (_spectral_conv(v, fwd, inv, k0_ref[...], L) + v * bias0_ref[...])
    o_ref[...] = x2 * (_spectral_conv(z, fwd, inv, k1_ref[...], L) + z * bias1_ref[...])


def _hyena_core(zproj, conv_w, conv_b, fwd_bf, inv_bf, kspec, bias, *, seq, nbatch, row_off, td):
    nd = D_MODEL // td
    zblk = lambda part: pl.BlockSpec((seq, td), lambda b, j, part=part: (row_off + b, part * nd + j))
    cwblk = lambda part: pl.BlockSpec((3, td), lambda b, j, part=part: (0, part * nd + j))
    cbblk = lambda part: pl.BlockSpec((1, td), lambda b, j, part=part: (0, part * nd + j))
    return pl.pallas_call(
        functools.partial(_hyena_kernel, seq),
        grid=(nbatch, nd),
        in_specs=[
            zblk(0), zblk(1), zblk(2), cwblk(0), cwblk(1), cwblk(2), cbblk(0), cbblk(1), cbblk(2),
            pl.BlockSpec((2 * seq, seq), lambda b, j: (0, 0), pipeline_mode=pl.Buffered(1)),
            pl.BlockSpec((seq, 2 * seq), lambda b, j: (0, 0), pipeline_mode=pl.Buffered(1)),
            pl.BlockSpec((2 * seq, td), lambda b, j: (0, j)),
            pl.BlockSpec((2 * seq, td), lambda b, j: (0, nd + j)),
            pl.BlockSpec((None, 1, td), lambda b, j: (0, 0, j)),
            pl.BlockSpec((None, 1, td), lambda b, j: (1, 0, j)),
        ],
        out_specs=pl.BlockSpec((seq, td), lambda b, j: (b, j)),
        out_shape=jax.ShapeDtypeStruct((nbatch * seq, D_MODEL), F32),
        compiler_params=_cparams(2),
    )(zproj, zproj, zproj, conv_w, conv_w, conv_w, conv_b, conv_b, conv_b,
      fwd_bf, inv_bf, kspec, kspec, bias.reshape(HY_ORDER, 1, D_MODEL), bias.reshape(HY_ORDER, 1, D_MODEL))


def _router_kernel(x_ref, ada_ref, g_ref, rw_ref, rb_ref, h_ref, idx_ref, wt_ref):
    i = pl.program_id(0)
    row = _cond_row(i)
    h = _modulate(x_ref[...], g_ref[...], _ada_chunk(ada_ref, row, 3), _ada_chunk(ada_ref, row, 4))
    for j in range(ROW_CHUNKS):
        h_ref[pl.ds(j, ROW_TILE, stride=ROW_CHUNKS), :] = h[:, j * LANES:(j + 1) * LANES]
    logits = lax.dot_general(rw_ref[...], h, (((1,), (1,)), ((), ())), precision=HIGHEST,
                             preferred_element_type=F32) + rb_ref[...]
    expert = lax.broadcasted_iota(jnp.int32, logits.shape, 0)
    slot = lax.broadcasted_iota(jnp.int32, (TOP_K, logits.shape[1]), 0)
    vals = jnp.zeros((TOP_K, logits.shape[1]), F32)
    idxs = jnp.zeros((TOP_K, logits.shape[1]), jnp.int32)
    cur = logits
    for k in range(TOP_K):
        m = jnp.max(cur, axis=0, keepdims=True)
        a = jnp.min(jnp.where(cur == m, expert, N_EXPERTS), axis=0, keepdims=True)
        vals = jnp.where(slot == k, m, vals)
        idxs = jnp.where(slot == k, a, idxs)
        cur = jnp.where(expert == a, -jnp.inf, cur)
    e = jnp.exp(vals - vals[0:1])
    wt_ref[...] = e / jnp.sum(e, axis=0, keepdims=True)
    idx_ref[...] = idxs


def _router(y, ada_l, g, router_w, router_b):
    return pl.pallas_call(
        _router_kernel,
        grid=(N_ROW_TILES,),
        in_specs=[
            pl.BlockSpec((ROW_TILE, D_MODEL), lambda i: (i, 0)),
            pl.BlockSpec((COND_ROWS, ADA_CHUNKS * D_MODEL), lambda i: (0, 0)),
            pl.BlockSpec((1, D_MODEL), lambda i: (0, 0)),
            pl.BlockSpec((N_EXPERTS, D_MODEL), lambda i: (0, 0)),
            pl.BlockSpec((N_EXPERTS, 1), lambda i: (0, 0)),
        ],
        out_specs=[
            pl.BlockSpec((ROW_TILE * ROW_CHUNKS, LANES), lambda i: (i, 0)),
            pl.BlockSpec((TOP_K, ROW_TILE), lambda i: (0, i)),
            pl.BlockSpec((TOP_K, ROW_TILE), lambda i: (0, i)),
        ],
        out_shape=[
            jax.ShapeDtypeStruct((T_ALL * ROW_CHUNKS, LANES), F32),
            jax.ShapeDtypeStruct((TOP_K, T_ALL), jnp.int32),
            jax.ShapeDtypeStruct((TOP_K, T_ALL), F32),
        ],
        compiler_params=_cparams(1),
    )(y, ada_l, g.reshape(1, D_MODEL), router_w.T, router_b.reshape(N_EXPERTS, 1))


def _dispatch_kernel(rows_ref, h_ref, o_ref):
    base = pl.program_id(0) * GATHER_ROWS
    for r in range(GATHER_ROWS):
        t = pl.multiple_of(rows_ref[base + r] * ROW_CHUNKS, ROW_CHUNKS)
        o_ref[r * ROW_CHUNKS:(r + 1) * ROW_CHUNKS, :] = h_ref[pl.ds(t, ROW_CHUNKS), :]


def _dispatch(h_tiles, gather_row):
    grid_spec = pltpu.PrefetchScalarGridSpec(
        num_scalar_prefetch=1,
        grid=(X_ROWS // GATHER_ROWS,),
        in_specs=[pl.BlockSpec((T_ALL * ROW_CHUNKS, LANES), lambda i, rows: (0, 0), pipeline_mode=pl.Buffered(1))],
        out_specs=pl.BlockSpec((GATHER_ROWS * ROW_CHUNKS, LANES), lambda i, rows: (i, 0)),
    )
    return pl.pallas_call(
        _dispatch_kernel,
        grid_spec=grid_spec,
        out_shape=jax.ShapeDtypeStruct((X_ROWS * ROW_CHUNKS, LANES), F32),
        compiler_params=_cparams(1),
    )(gather_row, h_tiles)


def _deinterleave_matrix():
    s = np.zeros((256, 256), np.float32)
    j = np.arange(128)
    s[2 * j, j] = 1.0
    s[2 * j + 1, 128 + j] = 1.0
    return jnp.asarray(s)


def _weight_copies(layer, e, w1_hbm, w2_hbm, w1s_ref, w2s_ref, sem):
    copies = []
    r1 = D_MODEL // W1_DMA_CHUNKS
    for c in range(W1_DMA_CHUNKS):
        copies.append(pltpu.make_async_copy(w1_hbm.at[layer, e, pl.ds(c * r1, r1)],
                                            w1s_ref.at[pl.ds(c * r1, r1)], sem.at[c]))
    r2 = D_FF // W2_DMA_CHUNKS
    for c in range(W2_DMA_CHUNKS):
        copies.append(pltpu.make_async_copy(w2_hbm.at[layer, e, pl.ds(c * r2, r2)],
                                            w2s_ref.at[pl.ds(c * r2, r2)], sem.at[W1_DMA_CHUNKS + c]))
    return copies


def _expert_kernel(layer, te_ref, tf_ref, ne_ref, nu_ref, src_ref, xo_ref, vr_ref, x_ref, b1_ref, b2_ref, wt_ref, s_ref,
                   w1_hbm, w2_hbm, o_hbm, w1s_ref, w2s_ref, w1p_ref, w2p_ref, acc_ref, out_ref, wsem, osem):
    i = pl.program_id(0)
    half = 128
    copies = functools.partial(_weight_copies, layer, w1_hbm=w1_hbm, w2_hbm=w2_hbm,
                               w1s_ref=w1s_ref, w2s_ref=w2s_ref, sem=wsem)

    @pl.when(i == 0)
    def _():
        acc_ref[...] = jnp.zeros_like(acc_ref)
        out_ref[...] = jnp.zeros_like(out_ref)
        for cp in copies(te_ref[0]):
            cp.start()

    @pl.when(tf_ref[i] == 1)
    def _():
        for cp in copies(te_ref[i]):
            cp.wait()
        s = s_ref[...].astype(BF16)
        for c in range(2 * D_FF // 256):
            blk = jnp.dot(w1s_ref[:, c * 256:(c + 1) * 256].astype(BF16), s, preferred_element_type=F32)
            w1p_ref[:, c * half:(c + 1) * half] = blk[:, :half].astype(BF16)
            w1p_ref[:, D_FF + c * half:D_FF + (c + 1) * half] = blk[:, half:].astype(BF16)
        w2p_ref[...] = w2s_ref[...].astype(BF16)

        @pl.when(ne_ref[i] >= 0)
        def _():
            for cp in copies(ne_ref[i]):
                cp.start()

    def tile_step(m):
        base = i * MOE_TILE
        prev = (i + 1) % 2
        for r0 in range(0, MOE_TILE, SCATTER_GROUP):
            toks = [pl.multiple_of(src_ref[base + r0 + g] * ROW_CHUNKS, ROW_CHUNKS) for g in range(SCATTER_GROUP)]
            cur = [acc_ref[pl.ds(toks[g], ROW_CHUNKS), :] for g in range(SCATTER_GROUP)]
            add = [out_ref[prev, (r0 + g) * ROW_CHUNKS:(r0 + g + 1) * ROW_CHUNKS, :] for g in range(SCATTER_GROUP)]
            for g in range(SCATTER_GROUP):
                acc_ref[pl.ds(toks[g], ROW_CHUNKS), :] = cur[g] + add[g]
        x = jnp.concatenate([x_ref[pl.ds(j, m, stride=ROW_CHUNKS), :] for j in range(ROW_CHUNKS)], axis=1)
        a = jnp.dot(x.astype(BF16), w1p_ref[...], preferred_element_type=F32) + b1_ref[...]
        glu = jnp.minimum(a[:, :D_FF], SWIGLU_LIMIT)
        lin = jnp.clip(a[:, D_FF:], -SWIGLU_LIMIT, SWIGLU_LIMIT)
        hid = glu * _sigmoid(SWIGLU_ALPHA * glu) * (lin + 1.0)
        out = (jnp.dot(hid.astype(BF16), w2p_ref[...], preferred_element_type=F32) + b2_ref[...]) * wt_ref[0:m, :]
        cur_buf = i % 2
        for j in range(ROW_CHUNKS):
            out_ref[cur_buf, pl.ds(j, m, stride=ROW_CHUNKS), :] = out[:, j * LANES:(j + 1) * LANES]

    live = i <= nu_ref[0]
    pl.when(live & (vr_ref[i] > MOE_TILE // 2))(functools.partial(tile_step, MOE_TILE))
    pl.when(live & (vr_ref[i] <= MOE_TILE // 2))(functools.partial(tile_step, MOE_TILE // 2))

    @pl.when(i == pl.num_programs(0) - 1)
    def _():
        cp = pltpu.make_async_copy(acc_ref.at[pl.ds(0, T_ALL * ROW_CHUNKS)], o_hbm, osem)
        cp.start()
        cp.wait()


def _experts(layer, x_sorted, w_sorted, plan, w1, b1p, w2, b2):
    tile_expert, tile_first, next_expert, n_used, src, x_off, valid_rows = plan
    grid_spec = pltpu.PrefetchScalarGridSpec(
        num_scalar_prefetch=7,
        grid=(MOE_TILES,),
        in_specs=[
            pl.BlockSpec((pl.Element(MOE_TILE * ROW_CHUNKS), pl.Element(LANES)),
                         lambda i, te, tf, ne, nu, src, xo, vr: (pl.multiple_of(xo[i] * ROW_CHUNKS, X_ALIGN * ROW_CHUNKS), 0)),
            pl.BlockSpec((None, None, 1, 2 * D_FF), lambda i, te, *_: (layer, te[i], 0, 0)),
            pl.BlockSpec((None, None, 1, D_MODEL), lambda i, te, *_: (layer, te[i], 0, 0)),
            pl.BlockSpec((MOE_TILE, 1), lambda i, te, *_: (i, 0)),
            pl.BlockSpec((256, 256), lambda i, te, *_: (0, 0)),
            pl.BlockSpec(memory_space=pl.ANY),
            pl.BlockSpec(memory_space=pl.ANY),
        ],
        out_specs=pl.BlockSpec(memory_space=pl.ANY),
        scratch_shapes=[
            pltpu.VMEM((D_MODEL, 2 * D_FF), F32),
            pltpu.VMEM((D_FF, D_MODEL), F32),
            pltpu.VMEM((D_MODEL, 2 * D_FF), BF16),
            pltpu.VMEM((D_FF, D_MODEL), BF16),
            pltpu.VMEM((ACC_ROWS * ROW_CHUNKS, LANES), F32),
            pltpu.VMEM((2, MOE_TILE * ROW_CHUNKS, LANES), F32),
            pltpu.SemaphoreType.DMA((W1_DMA_CHUNKS + W2_DMA_CHUNKS,)),
            pltpu.SemaphoreType.DMA(()),
        ],
    )
    return pl.pallas_call(
        functools.partial(_expert_kernel, layer),
        grid_spec=grid_spec,
        out_shape=jax.ShapeDtypeStruct((T_ALL * ROW_CHUNKS, LANES), F32),
        compiler_params=_cparams(1),
    )(tile_expert, tile_first, next_expert, n_used, src, x_off, valid_rows, x_sorted, b1p, b2, w_sorted,
      _deinterleave_matrix(), w1, w2)


def _combine_kernel(first_tile, y_ref, a_ref, ada_ref, o_ref):
    gate = _ada_chunk(ada_ref, _cond_row(first_tile + pl.program_id(0)), 5)
    acc = jnp.concatenate([a_ref[pl.ds(j, ROW_TILE, stride=ROW_CHUNKS), :] for j in range(ROW_CHUNKS)], axis=1)
    o_ref[...] = y_ref[...] + gate * acc


def _combine(y, acc, ada_l, first_tile=0, n_tiles=N_ROW_TILES):
    return pl.pallas_call(
        functools.partial(_combine_kernel, first_tile),
        grid=(n_tiles,),
        in_specs=[
            pl.BlockSpec((ROW_TILE, D_MODEL), lambda i: (first_tile + i, 0)),
            pl.BlockSpec((ROW_TILE * ROW_CHUNKS, LANES), lambda i: (first_tile + i, 0)),
            pl.BlockSpec((COND_ROWS, ADA_CHUNKS * D_MODEL), lambda i: (0, 0)),
        ],
        out_specs=pl.BlockSpec((ROW_TILE, D_MODEL), lambda i: (i, 0)),
        out_shape=jax.ShapeDtypeStruct((n_tiles * ROW_TILE, D_MODEL), F32),
        compiler_params=_cparams(1),
    )(y, acc, ada_l)


def _routing_plan(idx, wts):
    eid = idx.reshape(-1)
    order = jnp.argsort(eid, stable=True).astype(jnp.int32)
    experts = jnp.arange(N_EXPERTS, dtype=jnp.int32)
    counts = jnp.sum(eid[:, None] == experts[None, :], axis=0).astype(jnp.int32)
    ntiles = (counts + MOE_TILE - 1) // MOE_TILE
    tile_end = jnp.cumsum(ntiles).astype(jnp.int32)
    tile_begin = tile_end - ntiles
    cstarts = (jnp.cumsum(counts) - counts).astype(jnp.int32)
    n_used = tile_end[-1]
    tile = jnp.arange(MOE_TILES, dtype=jnp.int32)
    te = jnp.minimum(jnp.sum(tile[:, None] >= tile_end[None, :], axis=1), N_EXPERTS - 1).astype(jnp.int32)
    used = tile < n_used
    prev = jnp.concatenate([jnp.full((1,), -1, jnp.int32), te[:-1]])
    first = (te != prev) & used

    def pick(onehot, table):
        return jnp.sum(jnp.where(onehot, table[None, :], 0), axis=1).astype(jnp.int32)

    tile_is = te[:, None] == experts[None, :]
    later = (experts[None, :] > experts[:, None]) & (ntiles[None, :] > 0)
    following = jnp.min(jnp.where(later, experts[None, :], N_EXPERTS), axis=1)
    following = jnp.where(following < N_EXPERTS, following, -1)
    next_expert = jnp.where(first, pick(tile_is, following), -1).astype(jnp.int32)
    tile_in_expert = tile - pick(tile_is, tile_begin)
    off = tile_in_expert[:, None] * MOE_TILE + jnp.arange(MOE_TILE, dtype=jnp.int32)[None, :]
    valid_rows = jnp.where(used, jnp.clip(pick(tile_is, counts) - tile_in_expert * MOE_TILE, 0, MOE_TILE), 0)
    valid = jnp.arange(MOE_TILE, dtype=jnp.int32)[None, :] < valid_rows[:, None]
    assign = order[jnp.clip(pick(tile_is, cstarts)[:, None] + off, 0, N_ASSIGN - 1)]
    token = assign // TOP_K
    src = jnp.where(valid, token, SPARE_ROW).reshape(MOE_ROWS).astype(jnp.int32)
    src = jnp.concatenate([jnp.full((MOE_TILE,), SPARE_ROW, jnp.int32), src])
    w_sorted = jnp.where(valid, wts.reshape(-1)[assign], 0.0).reshape(MOE_ROWS, 1)
    seg = ((counts + X_ALIGN - 1) // X_ALIGN) * X_ALIGN
    seg_end = jnp.cumsum(seg).astype(jnp.int32)
    seg_begin = seg_end - seg
    x_off = jnp.where(used, pick(tile_is, seg_begin) + tile_in_expert * MOE_TILE, 0).astype(jnp.int32)
    group = jnp.arange(X_ROWS // X_ALIGN, dtype=jnp.int32) * X_ALIGN
    group_is = (group[:, None] >= seg_begin[None, :]) & (group[:, None] < seg_end[None, :])
    xoffset = (group - pick(group_is, seg_begin))[:, None] + jnp.arange(X_ALIGN, dtype=jnp.int32)[None, :]
    xassign = order[jnp.clip(pick(group_is, cstarts)[:, None] + xoffset, 0, N_ASSIGN - 1)]
    gather_row = jnp.where(xoffset < pick(group_is, counts)[:, None], xassign // TOP_K, 0).reshape(X_ROWS)
    plan = (te, first.astype(jnp.int32), next_expert, n_used.reshape(1), src, x_off, valid_rows.astype(jnp.int32))
    return plan, gather_row, w_sorted


def _moe(layer, y, ada_l, g, router_w, router_b, w1, b1p, w2, b2):
    h, idx_t, wts_t = _router(y, ada_l, g, router_w, router_b)
    plan, gather_row, w_sorted = _routing_plan(idx_t.T, wts_t.T)
    x_sorted = _dispatch(h, gather_row)
    acc = _experts(layer, x_sorted, w_sorted, plan, w1, b1p, w2, b2)
    if layer == DEPTH - 1:
        return (_combine(y, acc, ada_l, 0, P_TILES), _combine(y, acc, ada_l, P_TILES, N_ROW_TILES - P_TILES))
    return _combine(y, acc, ada_l)


def kernel(x_prompt, x_sample, cache_attn_k, cache_attn_v, state_mlstm_C, state_mlstm_n, state_mlstm_m, c, c_ctx, ada_w, ada_b, norm_mix_g, norm_ffn_g, ab_w_in, ab_w_out, da_qnorm_g, da_knorm_g, da_lambda, da_subnorm_g, ml_conv_w, ml_conv_b, ml_gate_b, ml_headnorm_g, hy_w_in, hy_w_out, hy_conv_w, hy_conv_b, hy_f_w1, hy_f_b1, hy_f_freq1, hy_f_w2, hy_f_b2, hy_f_freq2, hy_f_w3, hy_bias, router_w, router_b, moe_w1, moe_b1, moe_w2, moe_b2):
    y = (x_prompt.reshape(T_P, D_MODEL), x_sample.reshape(T_S, D_MODEL))
    cond = jnp.concatenate([c_ctx[None, :], c, jnp.zeros((COND_ROWS - 1 - DEC_BATCH, D_MODEL), F32)], axis=0)
    ada = _ada_table(cond, ada_w, ada_b)
    b1p = moe_b1.reshape(DEPTH, N_EXPERTS, D_FF, 2).swapaxes(2, 3).reshape(DEPTH, N_EXPERTS, 1, 2 * D_FF)
    b2r = moe_b2.reshape(DEPTH, N_EXPERTS, 1, D_MODEL)
    new_k, new_v, new_c, new_n, new_m = [], [], [], [], []
    for layer in range(DEPTH):
        ada_l = ada[layer]
        if layer % 2 == 0:
            e = layer // 2
            lam_init = 0.8 - 0.6 * math.exp(-0.3 * layer)
            qkv, mqk, mv, mo, mg = _modulated_proj(
                y, ada_l, norm_mix_g[layer], ab_w_in[e], (3 * W_A, 2 * W_B, W_B, W_B, 4 * H_B))
            qg2 = jnp.tile(da_qnorm_g[e], 2).reshape(1, 2 * HD_A)
            kg2 = jnp.tile(da_knorm_g[e], 2).reshape(1, 2 * HD_A)
            sub_g = da_subnorm_g[e].reshape(1, 2 * HD_A)
            oa_p, k_norm, v_heads = _attention_prompt(qkv, qg2, kg2, da_lambda[e], sub_g, lam_init)
            cos, sin = _rope_tables()
            oa_s = _attention_sample(
                qkv, cache_attn_k[:, e].reshape(DEC_BATCH, PAST_LEN, W_A),
                cache_attn_v[:, e].reshape(DEC_BATCH, PAST_LEN, W_A), cos, sin,
                qg2, kg2, da_lambda[e], sub_g, lam_init)
            ob_p, c_new, n_new, m_new = _mlstm(
                mqk, mv, mo, mg[:T_P], ml_conv_w[e], ml_conv_b[e].reshape(1, 2 * W_B), ml_gate_b[e],
                ml_headnorm_g[e], seq=SEQ, nbatch=BATCH, row_off=0, group=MLSTM_GROUP)
            ob_s = _mlstm(
                mqk, mv, mo, mg[T_P:], ml_conv_w[e], ml_conv_b[e].reshape(1, 2 * W_B), ml_gate_b[e],
                ml_headnorm_g[e], seq=DEC_SEQ, nbatch=DEC_BATCH, row_off=T_P // DEC_SEQ, group=1,
                ctx=(state_mlstm_C[:, e], state_mlstm_n[:, e], state_mlstm_m[:, e]))
            y = _out_proj_residual([(oa_p, oa_s), (ob_p, ob_s)], y, ada_l, ab_w_out[e], 2)
            new_k.append(k_norm.reshape(BATCH, SEQ, H_A, 2, HD_A))
            new_v.append(v_heads.reshape(BATCH, SEQ, H_A, 2 * HD_A))
            new_c.append(c_new)
            new_n.append(n_new.reshape(BATCH, 2, H_B, HD_B))
            new_m.append(m_new[..., 0, 0])
        else:
            o = layer // 2
            (zproj,) = _modulated_proj(y, ada_l, norm_mix_g[layer], hy_w_in[o], (HY_PROJ,))
            cores = []
            for seq, nbatch, row_off, td in ((SEQ, BATCH, 0, 512), (DEC_SEQ, DEC_BATCH, T_P // DEC_SEQ, 256)):
                fwd, inv = _dft_mats(seq)
                fwd_bf, inv_bf = fwd.astype(BF16), inv.astype(BF16)
                kspec = _hyena_filter_spectrum(seq, fwd_bf, hy_f_w1[o], hy_f_b1[o], hy_f_freq1[o], hy_f_w2[o],
                                               hy_f_b2[o], hy_f_freq2[o], hy_f_w3[o])
                cores.append(_hyena_core(zproj, hy_conv_w[o], hy_conv_b[o].reshape(1, HY_PROJ), fwd_bf, inv_bf,
                                         kspec, hy_bias[o], seq=seq, nbatch=nbatch, row_off=row_off, td=td))
            y = _out_proj_residual([tuple(cores)], y, ada_l, hy_w_out[o], 2)
        y = _moe(layer, y, ada_l, norm_ffn_g[layer], router_w[layer], router_b[layer],
                 moe_w1, b1p, moe_w2, b2r)
    y_p = y[0].reshape(BATCH, SEQ, D_MODEL)
    y_s = y[1].reshape(DEC_BATCH, DEC_SEQ, D_MODEL)
    return (y_p, y_s, jnp.stack(new_k, axis=1), jnp.stack(new_v, axis=1), jnp.stack(new_c, axis=1),
            jnp.stack(new_n, axis=1), jnp.stack(new_m, axis=1))
```

```python
import functools
import math

import numpy as np
import jax
import jax.numpy as jnp
from jax import lax
from jax.experimental import pallas as pl
from jax.experimental.pallas import tpu as pltpu

D_MODEL = 1024
BATCH = 16
SEQ = 256
DEPTH = 2
DEC_BATCH = 2
DEC_SEQ = 1024
PAST_LEN = 256
GRID_W = 64
W_A = D_MODEL // 2
HD_A = 64
H_A = W_A // (2 * HD_A)
W_B = D_MODEL - W_A
HD_B = 128
H_B = W_B // HD_B
AB_PROJ = 3 * W_A + 4 * W_B + 4 * H_B
ROPE_BASE = 10000.0
CHUNK = 64
HY_ORDER = 2
HY_PROJ = (HY_ORDER + 1) * D_MODEL
HY_BANDS = 8
HY_FH = 64
HY_TARGET = 1e-2
HY_FAST_PCT = 0.3
HY_SLOW_PCT = 1.5
N_EXPERTS = 32
TOP_K = 4
D_FF = D_MODEL
SWIGLU_ALPHA = 1.702
SWIGLU_LIMIT = 7.0
ADA_CHUNKS = 6
EPS = 1e-6
NEG = -1e30
F32 = jnp.float32
BF16 = jnp.bfloat16

T_P = BATCH * SEQ
T_S = DEC_BATCH * DEC_SEQ
T_ALL = T_P + T_S
ROW_TILE = 256
N_ROW_TILES = T_ALL // ROW_TILE
P_TILES = T_P // ROW_TILE
S_TILES_PER_BATCH = DEC_SEQ // ROW_TILE
COND_ROWS = 8
MOE_TILE = 256
N_ASSIGN = T_ALL * TOP_K
MOE_ROWS = N_ASSIGN + N_EXPERTS * MOE_TILE
MOE_TILES = MOE_ROWS // MOE_TILE
X_ALIGN = 16
MLSTM_GROUP = 2
GATHER_ROWS = 1024
X_ROWS = -(-(N_ASSIGN + N_EXPERTS * X_ALIGN + MOE_TILE) // GATHER_ROWS) * GATHER_ROWS
SPARE_ROW = T_ALL
ACC_ROWS = T_ALL + 8
SCATTER_GROUP = 8
W1_DMA_CHUNKS = 8
W2_DMA_CHUNKS = 4
LANES = 128
ROW_CHUNKS = D_MODEL // LANES
VMEM_LIMIT = 56 * 1024 * 1024
HIGHEST = lax.Precision.HIGHEST


def _cparams(n_axes):
    return pltpu.CompilerParams(dimension_semantics=("arbitrary",) * n_axes,
                                vmem_limit_bytes=VMEM_LIMIT)


def _bdot(a, b):
    return jnp.dot(a.astype(BF16), b.astype(BF16), preferred_element_type=F32)


def _cond_row(i):
    return jnp.where(i < P_TILES, 0, 1 + (i - P_TILES) // S_TILES_PER_BATCH)


def _ada_chunk(ada_ref, row, j):
    return ada_ref[pl.ds(row, 1), j * D_MODEL:(j + 1) * D_MODEL]


def _modulate(x, g, shift, scale):
    ms = jnp.mean(x * x, axis=-1, keepdims=True)
    return (x * lax.rsqrt(ms + EPS) * g) * (1.0 + scale) + shift


def _sigmoid(x):
    return 1.0 / (1.0 + jnp.exp(-x))


def _silu(x):
    return x * _sigmoid(x)


def _log_sigmoid(x):
    return jnp.minimum(x, 0.0) - jnp.log(1.0 + jnp.exp(-jnp.abs(x)))


def _dwconv3(x, w, b, seq=None):
    n = x.shape[0]
    seq = n if seq is None else seq
    pos = lax.broadcasted_iota(jnp.int32, x.shape, 0) % seq
    prev = jnp.where(pos == 0, 0.0, pltpu.roll(x, 1, 0))
    nxt = jnp.where(pos == seq - 1, 0.0, pltpu.roll(x, n - 1, 0))
    return prev * w[0:1] + x * w[1:2] + nxt * w[2:3] + b


def _ada_kernel(cond_ref, w_ref, b_ref, o_ref):
    c = _silu(cond_ref[...])
    o_ref[...] = _bdot(c, w_ref[...]) + b_ref[...]


def _ada_table(cond, ada_w, ada_b):
    tn = 1536
    return pl.pallas_call(
        _ada_kernel,
        grid=(DEPTH, ADA_CHUNKS * D_MODEL // tn),
        in_specs=[
            pl.BlockSpec((COND_ROWS, D_MODEL), lambda l, j: (0, 0)),
            pl.BlockSpec((None, D_MODEL, tn), lambda l, j: (l, 0, j)),
            pl.BlockSpec((None, 1, tn), lambda l, j: (l, 0, j)),
        ],
        out_specs=pl.BlockSpec((None, COND_ROWS, tn), lambda l, j: (l, 0, j)),
        out_shape=jax.ShapeDtypeStruct((DEPTH, COND_ROWS, ADA_CHUNKS * D_MODEL), F32),
        compiler_params=_cparams(2),
    )(cond, ada_w, ada_b.reshape(DEPTH, 1, ADA_CHUNKS * D_MODEL))


def _stream_specs(y):
    if isinstance(y, tuple):
        return [pl.BlockSpec((ROW_TILE, D_MODEL), lambda i: (jnp.minimum(i, P_TILES - 1), 0)),
                pl.BlockSpec((ROW_TILE, D_MODEL), lambda i: (jnp.maximum(i - P_TILES, 0), 0))], list(y)
    return [pl.BlockSpec((ROW_TILE, D_MODEL), lambda i: (i, 0))], [y]


def _stream_tile(y_refs, i):
    if len(y_refs) == 2:
        return jnp.where(i < P_TILES, y_refs[0][...], y_refs[1][...])
    return y_refs[0][...]


def _proj_kernel(splits, n_y, *refs):
    y_refs = refs[:n_y]
    ada_ref, g_ref, w_ref = refs[n_y:n_y + 3]
    out_refs, wbf_ref = refs[n_y + 3:-1], refs[-1]
    i = pl.program_id(0)

    @pl.when(i == 0)
    def _():
        wbf_ref[...] = w_ref[...].astype(BF16)

    row = _cond_row(i)
    h = _modulate(_stream_tile(y_refs, i), g_ref[...], _ada_chunk(ada_ref, row, 0), _ada_chunk(ada_ref, row, 1))
    h = h.astype(BF16)
    lo = 0
    for o_ref, width in zip(out_refs, splits):
        o_ref[...] = jnp.dot(h, wbf_ref[:, lo:lo + width], preferred_element_type=F32)
        lo += width


def _modulated_proj(y, ada_l, g, w, splits):
    n = w.shape[1]
    y_specs, y_args = _stream_specs(y)
    return pl.pallas_call(
        functools.partial(_proj_kernel, splits, len(y_args)),
        grid=(N_ROW_TILES,),
        in_specs=y_specs + [
            pl.BlockSpec((COND_ROWS, ADA_CHUNKS * D_MODEL), lambda i: (0, 0)),
            pl.BlockSpec((1, D_MODEL), lambda i: (0, 0)),
            pl.BlockSpec((D_MODEL, n), lambda i: (0, 0), pipeline_mode=pl.Buffered(1)),
        ],
        out_specs=[pl.BlockSpec((ROW_TILE, s), lambda i: (i, 0)) for s in splits],
        out_shape=[jax.ShapeDtypeStruct((T_ALL, s), F32) for s in splits],
        scratch_shapes=[pltpu.VMEM((D_MODEL, n), BF16)],
        compiler_params=_cparams(1),
    )(*y_args, ada_l, g.reshape(1, D_MODEL), w)


def _out_proj_kernel(n_in, n_y, gate_chunk, *refs):
    x_refs = refs[:2 * n_in]
    y_refs = refs[2 * n_in:2 * n_in + n_y]
    ada_ref, w_ref, g_ref, rw_ref, rb_ref, o_ref, h_ref, idx_ref, wt_ref, wbf_ref = refs[2 * n_in + n_y:]
    i = pl.program_id(0)

    @pl.when(i == 0)
    def _():
        wbf_ref[...] = w_ref[...].astype(BF16)

    acc = None
    lo = 0
    for xp_ref, xs_ref in zip(x_refs[0::2], x_refs[1::2]):
        k = xp_ref.shape[1]
        x = jnp.where(i < P_TILES, xp_ref[...], xs_ref[...])
        part = jnp.dot(x.astype(BF16), wbf_ref[lo:lo + k, :], preferred_element_type=F32)
        acc = part if acc is None else acc + part
        lo += k
    row = _cond_row(i)
    y_new = _stream_tile(y_refs, i) + _ada_chunk(ada_ref, row, gate_chunk) * acc
    o_ref[...] = y_new
    _route_tile(y_new, row, ada_ref, g_ref, rw_ref, rb_ref, h_ref, idx_ref, wt_ref)


def _out_proj_residual(xs, y, ada_l, w, gate_chunk, ffn_g, router_w, router_b):
    y_specs, y_args = _stream_specs(y)
    x_specs = []
    for xp, _ in xs:
        x_specs.append(pl.BlockSpec((ROW_TILE, xp.shape[1]), lambda i: (jnp.minimum(i, P_TILES - 1), 0)))
        x_specs.append(pl.BlockSpec((ROW_TILE, xp.shape[1]), lambda i: (jnp.maximum(i - P_TILES, 0), 0)))
    y_new, h, idx_t, wts_t = pl.pallas_call(
        functools.partial(_out_proj_kernel, len(xs), len(y_args), gate_chunk),
        grid=(N_ROW_TILES,),
        in_specs=x_specs + y_specs + [
            pl.BlockSpec((COND_ROWS, ADA_CHUNKS * D_MODEL), lambda i: (0, 0)),
            pl.BlockSpec((D_MODEL, D_MODEL), lambda i: (0, 0), pipeline_mode=pl.Buffered(1)),
            pl.BlockSpec((1, D_MODEL), lambda i: (0, 0)),
            pl.BlockSpec((N_EXPERTS, D_MODEL), lambda i: (0, 0)),
            pl.BlockSpec((N_EXPERTS, 1), lambda i: (0, 0)),
        ],
        out_specs=[
            pl.BlockSpec((ROW_TILE, D_MODEL), lambda i: (i, 0)),
            pl.BlockSpec((ROW_TILE * ROW_CHUNKS, LANES), lambda i: (i, 0)),
            pl.BlockSpec((TOP_K, ROW_TILE), lambda i: (0, i)),
            pl.BlockSpec((TOP_K, ROW_TILE), lambda i: (0, i)),
        ],
        out_shape=[
            jax.ShapeDtypeStruct((T_ALL, D_MODEL), F32),
            jax.ShapeDtypeStruct((T_ALL * ROW_CHUNKS, LANES), F32),
            jax.ShapeDtypeStruct((TOP_K, T_ALL), jnp.int32),
            jax.ShapeDtypeStruct((TOP_K, T_ALL), F32),
        ],
        scratch_shapes=[pltpu.VMEM((D_MODEL, D_MODEL), BF16)],
        compiler_params=_cparams(1),
    )(*[a for pair in xs for a in pair], *y_args, ada_l, w,
      ffn_g.reshape(1, D_MODEL), router_w.T, router_b.reshape(N_EXPERTS, 1))
    return y_new, (h, idx_t, wts_t)


def _subhead_norm(x, g2):
    lane = lax.broadcasted_iota(jnp.int32, x.shape, 1)
    first = lane < HD_A
    xx = x * x
    s0 = jnp.sum(jnp.where(first, xx, 0.0), axis=-1, keepdims=True)
    s1 = jnp.sum(jnp.where(first, 0.0, xx), axis=-1, keepdims=True)
    r = jnp.where(first, lax.rsqrt(s0 / HD_A + EPS), lax.rsqrt(s1 / HD_A + EPS))
    return x * r * g2


def _rope(x, cos, sin):
    quarter = HD_A // 4
    lane = lax.broadcasted_iota(jnp.int32, x.shape, 1)
    lower = (lane % (2 * quarter)) < quarter
    swapped = jnp.where(lower, pltpu.roll(x, 2 * HD_A - quarter, 1), pltpu.roll(x, quarter, 1))
    return x * cos + swapped * sin


def _attn_kernel(lam_init, has_ctx, *refs):
    if has_ctx:
        (q_ref, k_ref, v_ref, ck_ref, cv_ref, cq_ref, sq_ref, ckk_ref, skk_ref,
         qg_ref, kg_ref, lp_ref, sg_ref, o_ref, kall_ref, vall_ref) = refs
    else:
        q_ref, k_ref, v_ref, qg_ref, kg_ref, lp_ref, sg_ref, o_ref, kn_ref, vh_ref = refs
    lp = lp_ref[...]
    lam = (jnp.exp(jnp.sum(lp[0:1] * lp[1:2], axis=-1, keepdims=True))
           - jnp.exp(jnp.sum(lp[2:3] * lp[3:4], axis=-1, keepdims=True)) + lam_init)

    def attend(q, k, v):
        probs = []
        for c in range(2):
            qc = q[:, c * HD_A:(c + 1) * HD_A].astype(BF16)
            kc = k[:, c * HD_A:(c + 1) * HD_A].astype(BF16)
            s = lax.dot_general(qc, kc, (((1,), (1,)), ((), ())), preferred_element_type=F32) * (HD_A ** -0.5)
            e = jnp.exp(s - jnp.max(s, axis=-1, keepdims=True))
            probs.append(e / jnp.sum(e, axis=-1, keepdims=True))
        o = _bdot(probs[0] - lam * probs[1], v)
        ms = jnp.mean(o * o, axis=-1, keepdims=True)
        return (o * lax.rsqrt(ms + EPS) * sg_ref[...]) * (1.0 - lam_init)

    if not has_ctx:
        for h in range(H_A):
            cols = slice(h * 2 * HD_A, (h + 1) * 2 * HD_A)
            k = _subhead_norm(k_ref[:, cols], kg_ref[...])
            for c in range(2):
                kn_ref[pl.ds(2 * h + c, SEQ, stride=2 * H_A), :] = k[:, c * HD_A:(c + 1) * HD_A]
            v = v_ref[:, cols]
            vh_ref[pl.ds(h, SEQ, stride=H_A), :] = v
            o_ref[:, cols] = attend(_subhead_norm(q_ref[:, cols], qg_ref[...]), k, v)
        return

    @pl.when(pl.program_id(2) == 0)
    def _():
        kall_ref[0:PAST_LEN, :] = ck_ref[...].astype(BF16)
        vall_ref[0:PAST_LEN, :] = cv_ref[...].astype(BF16)
        k_new = _rope(_subhead_norm(k_ref[...], kg_ref[...]), ckk_ref[...], skk_ref[...])
        kall_ref[PAST_LEN:, :] = k_new.astype(BF16)
        vall_ref[PAST_LEN:, :] = v_ref[...].astype(BF16)

    q = _rope(_subhead_norm(q_ref[...], qg_ref[...]), cq_ref[...], sq_ref[...])
    o_ref[...] = attend(q, kall_ref[...], vall_ref[...])


def _attention_prompt(qkv, qg2, kg2, lam_p, sub_g, lam_init):
    head = 2 * HD_A
    small = [
        pl.BlockSpec((1, head), lambda b: (0, 0)),
        pl.BlockSpec((1, head), lambda b: (0, 0)),
        pl.BlockSpec((4, HD_A), lambda b: (0, 0)),
        pl.BlockSpec((1, head), lambda b: (0, 0)),
    ]
    return pl.pallas_call(
        functools.partial(_attn_kernel, lam_init, False),
        grid=(BATCH,),
        in_specs=[
            pl.BlockSpec((SEQ, W_A), lambda b: (b, 0)),
            pl.BlockSpec((SEQ, W_A), lambda b: (b, 1)),
            pl.BlockSpec((SEQ, W_A), lambda b: (b, 2)),
        ] + small,
        out_specs=[pl.BlockSpec((SEQ, W_A), lambda b: (b, 0)),
                   pl.BlockSpec((SEQ * 2 * H_A, HD_A), lambda b: (b, 0)),
                   pl.BlockSpec((SEQ * H_A, head), lambda b: (b, 0))],
        out_shape=[jax.ShapeDtypeStruct((T_P, W_A), F32), jax.ShapeDtypeStruct((T_P * 2 * H_A, HD_A), F32),
                   jax.ShapeDtypeStruct((T_P * H_A, head), F32)],
        compiler_params=_cparams(1),
    )(qkv, qkv, qkv, qg2, kg2, lam_p, sub_g)


def _attention_sample(qkv, cache_k, cache_v, cos, sin, qg2, kg2, lam_p, sub_g, lam_init):
    nh = H_A
    head = 2 * HD_A
    tq = ROW_TILE
    nq = DEC_SEQ // tq
    q_off = T_P // tq
    k_off = T_P // DEC_SEQ
    small = [
        pl.BlockSpec((1, head), lambda b, h, i: (0, 0)),
        pl.BlockSpec((1, head), lambda b, h, i: (0, 0)),
        pl.BlockSpec((4, HD_A), lambda b, h, i: (0, 0)),
        pl.BlockSpec((1, head), lambda b, h, i: (0, 0)),
    ]
    return pl.pallas_call(
        functools.partial(_attn_kernel, lam_init, True),
        grid=(DEC_BATCH, nh, nq),
        in_specs=[
            pl.BlockSpec((tq, head), lambda b, h, i: (q_off + b * nq + i, h)),
            pl.BlockSpec((DEC_SEQ, head), lambda b, h, i: (k_off + b, nh + h)),
            pl.BlockSpec((DEC_SEQ, head), lambda b, h, i: (k_off + b, 2 * nh + h)),
            pl.BlockSpec((None, PAST_LEN, head), lambda b, h, i: (b, 0, h)),
            pl.BlockSpec((None, PAST_LEN, head), lambda b, h, i: (b, 0, h)),
            pl.BlockSpec((tq, head), lambda b, h, i: (i, 0)),
            pl.BlockSpec((tq, head), lambda b, h, i: (i, 0)),
            pl.BlockSpec((DEC_SEQ, head), lambda b, h, i: (0, 0)),
            pl.BlockSpec((DEC_SEQ, head), lambda b, h, i: (0, 0)),
        ] + small,
        out_specs=pl.BlockSpec((tq, head), lambda b, h, i: (b * nq + i, h)),
        out_shape=jax.ShapeDtypeStruct((T_S, W_A), F32),
        scratch_shapes=[pltpu.VMEM((PAST_LEN + DEC_SEQ, head), BF16)] * 2,
        compiler_params=_cparams(3),
    )(qkv, qkv, qkv, cache_k, cache_v, cos, sin, cos, sin, qg2, kg2, lam_p, sub_g)


def _rope_tables():
    half = HD_A // 2
    nf = half // 2
    inv = ROPE_BASE ** (-np.arange(nf, dtype=np.float32) / nf)
    pos = np.arange(DEC_SEQ)
    row = (pos // GRID_W).astype(np.float32)
    col = (pos % GRID_W).astype(np.float32)
    ang_r = (row[:, None] * inv).astype(np.float32)
    ang_c = (col[:, None] * inv).astype(np.float32)
    ang = np.concatenate([ang_r, ang_r, ang_c, ang_c], axis=1)
    sign = np.concatenate([-np.ones(nf), np.ones(nf), -np.ones(nf), np.ones(nf)]).astype(np.float32)
    cos = np.cos(ang.astype(np.float64)).astype(np.float32)
    sin = (np.sin(ang.astype(np.float64)) * sign).astype(np.float32)
    return jnp.asarray(np.tile(cos, (1, 2))), jnp.asarray(np.tile(sin, (1, 2)))


def _mlstm_kernel(seq, has_ctx, group, *refs):
    if has_ctx:
        (q_ref, k_ref, cwq_ref, cwk_ref, cbq_ref, cbk_ref, v_ref, mo_ref, gi_ref, gf_ref,
         gbi_ref, gbf_ref, hn_ref, c0_ref, n0_ref, m0_ref, o_ref,
         qs_ref, ks_ref, hf_ref, hb_ref, cs_ref, rrow_ref, col_ref, wc_ref) = refs
    else:
        (q_ref, k_ref, cwq_ref, cwk_ref, cbq_ref, cbk_ref, v_ref, mo_ref, gi_ref, gf_ref,
         gbi_ref, gbf_ref, hn_ref, o_ref, c_out_ref, n_out_ref, m_out_ref,
         qs_ref, ks_ref, hf_ref, hb_ref, cs_ref, rrow_ref, col_ref, wc_ref) = refs
    nc = seq // CHUNK
    n_chain = 2 * H_B
    chains = [(sub, d, h) for sub in range(group) for d in range(2) for h in range(H_B)]
    qs_ref[...] = _silu(_dwconv3(q_ref[...], cwq_ref[...], cbq_ref[...], seq)) * (HD_B ** -0.5)
    ks_ref[...] = _silu(_dwconv3(k_ref[...], cwk_ref[...], cbk_ref[...], seq))

    rows = nc * n_chain
    lane = lax.broadcasted_iota(jnp.int32, (rows, 2 * CHUNK), 1)
    forward = lax.broadcasted_iota(jnp.int32, (rows, 2 * CHUNK), 0) % n_chain < H_B
    valid = lane < CHUNK

    def scan(x, op, fill):
        pre, suf = x, x
        sh = 1
        while sh < CHUNK:
            pre = op(pre, jnp.where(lane >= sh, pltpu.roll(pre, sh, 1), fill))
            suf = op(suf, jnp.where(lane + sh < CHUNK, pltpu.roll(suf, 2 * CHUNK - sh, 1), fill))
            sh *= 2
        return jnp.where(forward, pre, suf)

    mm_final = []
    for sub in range(group):
        gate_i = (gi_ref[sub] + gbi_ref[...]).reshape(rows, 2 * CHUNK)
        lf = jnp.where(valid, _log_sigmoid(gf_ref[sub] + gbf_ref[...]).reshape(rows, 2 * CHUNK), 0.0)
        b = scan(lf, jnp.add, 0.0)
        cmax = scan(jnp.where(valid, gate_i - b, -jnp.inf), jnp.maximum, -jnp.inf)
        b_last = jnp.sum(lf, axis=1, keepdims=True)
        g = b_last - b + gate_i
        g_max = jnp.max(jnp.where(valid, g, -jnp.inf), axis=1, keepdims=True)
        mm = m0_ref[sub] if has_ctx else jnp.zeros((n_chain, 1), F32)
        mm_seq = []
        for p in range(nc):
            mm_seq.append(mm)
            seg = slice(p * n_chain, (p + 1) * n_chain)
            mm = jnp.maximum(b_last[seg] + mm, g_max[seg])
        mm_final.append(mm)
        mm_prev = jnp.concatenate(mm_seq, axis=0)
        mm_next = jnp.concatenate(mm_seq[1:] + [mm], axis=0)
        m_t = jnp.maximum(b + mm_prev, b + cmax)
        rrow_ref[sub] = (b - gate_i).reshape(nc, n_chain, 2 * CHUNK)
        wc_ref[sub] = jnp.exp(b_last + mm_prev - mm_next).reshape(nc, n_chain, 1)
        per_row = [b, m_t, jnp.exp(b + mm_prev - m_t), jnp.exp(-m_t), jnp.exp(g - mm_next)]
        for j, arr in enumerate(per_row):
            by_time = arr.T
            for p in range(nc):
                col_ref[sub, p, :, j * n_chain:(j + 1) * n_chain] = by_time[0:CHUNK, p * n_chain:(p + 1) * n_chain]

    t_idx = lax.broadcasted_iota(jnp.int32, (CHUNK, CHUNK), 0)
    s_idx = lax.broadcasted_iota(jnp.int32, (CHUNK, CHUNK), 1)
    for n, (sub, d, h) in enumerate(chains):
        cs_ref[n] = c0_ref[sub, d, h] if has_ctx else jnp.zeros((HD_B, HD_B), F32)

    def out_step(p, n_states):
        new_states = []
        for n, (sub, d, h) in enumerate(chains):
            cols = col_ref[sub, p]
            rrows = rrow_ref[sub, p]
            wcs = wc_ref[sub, p]
            n_loc = d * H_B + h
            c = p if d == 0 else nc - 1 - p
            r0 = pl.multiple_of(sub * seq + c * CHUNK, CHUNK)
            hcols = slice(h * HD_B, (h + 1) * HD_B)
            qt = qs_ref[pl.ds(r0, CHUNK), hcols]
            kt = ks_ref[pl.ds(r0, CHUNK), hcols]
            vt = v_ref[pl.ds(r0, CHUNK), hcols]
            b_col, m_t, w_inter, e_inv, w_k = (cols[:, j * n_chain + n_loc:j * n_chain + n_loc + 1] for j in range(5))
            mask = (s_idx <= t_idx) if d == 0 else (s_idx >= t_idx)
            decay = jnp.exp(jnp.where(mask, b_col - rrows[n_loc:n_loc + 1, 0:CHUNK], NEG) - m_t)
            qk = lax.dot_general(qt.astype(BF16), kt.astype(BF16), (((1,), (1,)), ((), ())),
                                 preferred_element_type=F32)
            s = qk * decay
            cm = cs_ref[n]
            nm = n_states[n]
            cq = lax.dot_general(qt.astype(BF16), cm.astype(BF16), (((1,), (1,)), ((), ())),
                                 preferred_element_type=F32)
            num = _bdot(s, vt) + w_inter * cq
            nq = jnp.sum(s, axis=-1, keepdims=True) + w_inter * jnp.sum(qt * nm, axis=-1, keepdims=True)
            hdir_ref = hf_ref if d == 0 else hb_ref
            hdir_ref[pl.ds(r0, CHUNK), hcols] = num / jnp.maximum(jnp.abs(nq), e_inv)
            w_c = wcs[n_loc:n_loc + 1, :]
            vw = (vt * w_k).astype(BF16)
            cs_ref[n] = w_c * cm + lax.dot_general(vw, kt.astype(BF16), (((0,), (0,)), ((), ())),
                                                   preferred_element_type=F32)
            new_states.append(w_c * nm + jnp.sum(kt * w_k, axis=0, keepdims=True))
        return tuple(new_states)

    if has_ctx:
        n_init = tuple(n0_ref[sub, d, h] for sub, d, h in chains)
    else:
        n_init = tuple(jnp.zeros((1, HD_B), F32) for _ in chains)
    n_final = lax.fori_loop(0, nc, out_step, n_init)
    if not has_ctx:
        for n, (sub, d, h) in enumerate(chains):
            n_loc = d * H_B + h
            c_out_ref[sub, d, h] = cs_ref[n]
            n_out_ref[sub, d, h] = n_final[n]
            m_out_ref[sub, d, h] = jnp.broadcast_to(mm_final[sub][n_loc:n_loc + 1, :], (1, HD_B))

    for h in range(H_B):
        hcols = slice(h * HD_B, (h + 1) * HD_B)
        hh = hf_ref[:, hcols] + hb_ref[:, hcols]
        ms = jnp.mean(hh * hh, axis=-1, keepdims=True)
        o_ref[:, hcols] = (hh * lax.rsqrt(ms + EPS) * hn_ref[:, hcols]) * _sigmoid(mo_ref[:, hcols])


def _mlstm(mqk, mv, mo, mg_stream, conv_w, conv_b, gate_b, hn_g, *, seq, nbatch, row_off, group, ctx=None):
    nh = H_B
    nc = seq // CHUNK
    has_ctx = ctx is not None
    assert nbatch % group == 0
    gt = mg_stream.reshape(nbatch, nc, CHUNK, 2, 2, nh).transpose(0, 1, 3, 4, 5, 2)
    pad = ((0, 0), (0, 0), (0, 0), (0, CHUNK))
    gates = [jnp.pad(jnp.concatenate([gt[:, :, 0, j], gt[:, ::-1, 1, j]], axis=2), pad) for j in range(2)]
    gate_bias = [jnp.concatenate([gate_b[0, j], gate_b[1, j]]).reshape(2 * nh, 1) for j in range(2)]
    blk = lambda col: pl.BlockSpec((group * seq, W_B), lambda b, col=col: (row_off + b, col))
    gate_blk = pl.BlockSpec((group, nc, 2 * nh, 2 * CHUNK), lambda b: (b, 0, 0, 0))
    in_specs = [
        blk(0), blk(1),
        pl.BlockSpec((3, W_B), lambda b: (0, 0)),
        pl.BlockSpec((3, W_B), lambda b: (0, 1)),
        pl.BlockSpec((1, W_B), lambda b: (0, 0)),
        pl.BlockSpec((1, W_B), lambda b: (0, 1)),
        blk(0), blk(0),
        gate_blk, gate_blk,
        pl.BlockSpec((2 * nh, 1), lambda b: (0, 0)),
        pl.BlockSpec((2 * nh, 1), lambda b: (0, 0)),
        pl.BlockSpec((1, W_B), lambda b: (0, 0)),
    ]
    args = [mqk, mqk, conv_w, conv_w, conv_b, conv_b, mv, mo, gates[0], gates[1],
            gate_bias[0], gate_bias[1], hn_g.reshape(1, W_B)]
    o_spec = pl.BlockSpec((group * seq, W_B), lambda b: (b, 0))
    o_shape = jax.ShapeDtypeStruct((nbatch * seq, W_B), F32)
    state_blk = lambda rows: pl.BlockSpec((group, 2, nh, rows, HD_B), lambda b: (b, 0, 0, 0, 0))
    if has_ctx:
        c0, n0, m0 = ctx
        in_specs += [state_blk(HD_B), state_blk(1), pl.BlockSpec((group, 2 * nh, 1), lambda b: (b, 0, 0))]
        args += [c0, n0.reshape(nbatch, 2, nh, 1, HD_B), m0.reshape(nbatch, 2 * nh, 1)]
        out_specs, out_shape = o_spec, o_shape
    else:
        out_specs = [o_spec, state_blk(HD_B), state_blk(1), state_blk(1)]
        out_shape = [
            o_shape,
            jax.ShapeDtypeStruct((nbatch, 2, nh, HD_B, HD_B), F32),
            jax.ShapeDtypeStruct((nbatch, 2, nh, 1, HD_B), F32),
            jax.ShapeDtypeStruct((nbatch, 2, nh, 1, HD_B), F32),
        ]
    return pl.pallas_call(
        functools.partial(_mlstm_kernel, seq, has_ctx, group),
        grid=(nbatch // group,),
        in_specs=in_specs,
        out_specs=out_specs,
        out_shape=out_shape,
        scratch_shapes=[pltpu.VMEM((group * seq, W_B), F32)] * 4 + [
            pltpu.VMEM((group * 2 * nh, HD_B, HD_B), F32),
            pltpu.VMEM((group, nc, 2 * nh, 2 * CHUNK), F32),
            pltpu.VMEM((group, nc, CHUNK, 5 * 2 * nh), F32),
            pltpu.VMEM((group, nc, 2 * nh, 1), F32),
        ],
        compiler_params=_cparams(1),
    )(*args)


def _dft_mats(L):
    f = np.arange(L)[:, None]
    j = np.arange(L)[None, :]
    ang = 2.0 * np.pi * ((f * j) % (2 * L)) / (2 * L)
    cm = np.cos(ang)
    sm = np.sin(ang)
    alt = (1.0 - 2.0 * (np.arange(L) % 2))
    fwd_b = -sm
    fwd_b[0, :] = alt
    fwd = np.concatenate([cm, fwd_b], axis=0)
    wgt = np.where(np.arange(L) == 0, 1.0, 2.0)[None, :]
    inv_a = cm.T * wgt
    inv_b = -2.0 * sm.T
    inv_b[:, 0] = alt
    inv = np.concatenate([inv_a, inv_b], axis=1) / (2 * L)
    return jnp.asarray(fwd.astype(np.float32)), jnp.asarray(inv.astype(np.float32))


def _hyena_feats(L):
    t = np.linspace(0.0, 1.0, L, dtype=np.float32)
    wpos = (2.0 * math.pi * np.arange(L, dtype=np.float32) / L).astype(np.float32)
    fb = np.linspace(1e-4, HY_BANDS - 1, HY_BANDS, dtype=np.float32)
    z = (wpos[:, None] * fb).astype(np.float32)
    feats = np.concatenate([t[:, None], np.cos(z), -np.sin(z)], axis=-1).astype(np.float32)
    deltas = np.abs(np.linspace(math.log(HY_TARGET) / HY_SLOW_PCT, math.log(HY_TARGET) / HY_FAST_PCT,
                                D_MODEL, dtype=np.float32))
    decay = np.exp(-t[:, None] * deltas).astype(np.float32)
    return jnp.asarray(feats), jnp.asarray(decay)


def _filter_kernel(L, feats_ref, w1_ref, b1_ref, fr1_ref, w2_ref, b2_ref, fr2_ref, w3f_ref, w3b_ref,
                   decay_ref, fwd_ref, o_ref, hdn_ref):
    @pl.when((pl.program_id(0) == 0) & (pl.program_id(1) == 0))
    def _():
        h1 = jnp.sin(fr1_ref[...] * (jnp.dot(feats_ref[...], w1_ref[...], precision=HIGHEST,
                                             preferred_element_type=F32) + b1_ref[...]))
        hdn_ref[...] = jnp.sin(fr2_ref[...] * (jnp.dot(h1, w2_ref[...], precision=HIGHEST,
                                                       preferred_element_type=F32) + b2_ref[...]))

    hdn = hdn_ref[...]
    decay = decay_ref[...]
    f_fwd = jnp.dot(hdn, w3f_ref[...], precision=HIGHEST, preferred_element_type=F32) * decay
    f_bwd = jnp.dot(hdn, w3b_ref[...], precision=HIGHEST, preferred_element_type=F32) * decay
    row = lax.broadcasted_iota(jnp.int32, f_bwd.shape, 0)
    f_bwd = jnp.where(row == 0, 0.0, f_bwd)
    f_sum = f_fwd + f_bwd
    o_ref[0:L, :] = _bdot(fwd_ref[0:L, :], f_sum)
    imag = _bdot(fwd_ref[L:2 * L, :], f_fwd - f_bwd)
    nyquist = _bdot(fwd_ref[L:L + 16, :], f_sum)[0:1]
    o_ref[L:2 * L, :] = jnp.where(row == 0, nyquist, imag)


def _hyena_filter_spectrum(L, fwd_bf, w1, b1, fr1, w2, b2, fr2, w3):
    feats, decay = _hyena_feats(L)
    td = 512
    nd = D_MODEL // td
    emb = feats.shape[1]
    vec = lambda a: a.reshape(1, HY_FH)
    full = lambda shape: pl.BlockSpec(shape, lambda o, j: (0, 0))
    return pl.pallas_call(
        functools.partial(_filter_kernel, L),
        grid=(HY_ORDER, nd),
        in_specs=[
            full((L, emb)), full((emb, HY_FH)), full((1, HY_FH)), full((1, HY_FH)),
            full((HY_FH, HY_FH)), full((1, HY_FH)), full((1, HY_FH)),
            pl.BlockSpec((HY_FH, td), lambda o, j: (0, o * 2 * nd + j)),
            pl.BlockSpec((HY_FH, td), lambda o, j: (0, o * 2 * nd + nd + j)),
            pl.BlockSpec((L, td), lambda o, j: (0, j)),
            full((2 * L, L)),
        ],
        out_specs=pl.BlockSpec((2 * L, td), lambda o, j: (0, o * nd + j)),
        out_shape=jax.ShapeDtypeStruct((2 * L, HY_ORDER * D_MODEL), F32),
        scratch_shapes=[pltpu.VMEM((L, HY_FH), F32)],
        compiler_params=_cparams(2),
    )(feats, w1, vec(b1), vec(fr1), w2, vec(b2), vec(fr2), w3, w3, decay, fwd_bf)


def _spectral_conv(u, fwd, inv, kspec, L):
    uf = jnp.dot(fwd, u.astype(BF16), preferred_element_type=F32)
    ua, ub = uf[0:L], uf[L:2 * L]
    ka, kb = kspec[0:L], kspec[L:2 * L]
    first = lax.broadcasted_iota(jnp.int32, ua.shape, 0) == 0
    ya = ua * ka - jnp.where(first, 0.0, ub * kb)
    yb = jnp.where(first, ub * kb, ua * kb + ub * ka)
    y = jnp.concatenate([ya, yb], axis=0).astype(BF16)
    return jnp.dot(inv, y, preferred_element_type=F32)


def _hyena_kernel(L, zv_ref, z1_ref, z2_ref, cwv_ref, cw1_ref, cw2_ref, cbv_ref, cb1_ref, cb2_ref,
                  fwd_ref, inv_ref, k0_ref, k1_ref, bias0_ref, bias1_ref, o_ref):
    fwd = fwd_ref[...]
    inv = inv_ref[...]
    v = _dwconv3(zv_ref[...], cwv_ref[...], cbv_ref[...])
    x1 = _dwconv3(z1_ref[...], cw1_ref[...], cb1_ref[...])
    x2 = _dwconv3(z2_ref[...], cw2_ref[...], cb2_ref[...])
    z = x1 * (_spectral_conv(v, fwd, inv, k0_ref[...], L) + v * bias0_ref[...])
    o_ref[...] = x2 * (_spectral_conv(z, fwd, inv, k1_ref[...], L) + z * bias1_ref[...])


def _hyena_core(zproj, conv_w, conv_b, fwd_bf, inv_bf, kspec, bias, *, seq, nbatch, row_off, td):
    nd = D_MODEL // td
    zblk = lambda part: pl.BlockSpec((seq, td), lambda b, j, part=part: (row_off + b, part * nd + j))
    cwblk = lambda part: pl.BlockSpec((3, td), lambda b, j, part=part: (0, part * nd + j))
    cbblk = lambda part: pl.BlockSpec((1, td), lambda b, j, part=part: (0, part * nd + j))
    return pl.pallas_call(
        functools.partial(_hyena_kernel, seq),
        grid=(nbatch, nd),
        in_specs=[
            zblk(0), zblk(1), zblk(2), cwblk(0), cwblk(1), cwblk(2), cbblk(0), cbblk(1), cbblk(2),
            pl.BlockSpec((2 * seq, seq), lambda b, j: (0, 0), pipeline_mode=pl.Buffered(1)),
            pl.BlockSpec((seq, 2 * seq), lambda b, j: (0, 0), pipeline_mode=pl.Buffered(1)),
            pl.BlockSpec((2 * seq, td), lambda b, j: (0, j)),
            pl.BlockSpec((2 * seq, td), lambda b, j: (0, nd + j)),
            pl.BlockSpec((None, 1, td), lambda b, j: (0, 0, j)),
            pl.BlockSpec((None, 1, td), lambda b, j: (1, 0, j)),
        ],
        out_specs=pl.BlockSpec((seq, td), lambda b, j: (b, j)),
        out_shape=jax.ShapeDtypeStruct((nbatch * seq, D_MODEL), F32),
        compiler_params=_cparams(2),
    )(zproj, zproj, zproj, conv_w, conv_w, conv_w, conv_b, conv_b, conv_b,
      fwd_bf, inv_bf, kspec, kspec, bias.reshape(HY_ORDER, 1, D_MODEL), bias.reshape(HY_ORDER, 1, D_MODEL))


def _route_tile(y_tile, row, ada_ref, g_ref, rw_ref, rb_ref, h_ref, idx_ref, wt_ref):
    h = _modulate(y_tile, g_ref[...], _ada_chunk(ada_ref, row, 3), _ada_chunk(ada_ref, row, 4))
    for j in range(ROW_CHUNKS):
        h_ref[pl.ds(j, ROW_TILE, stride=ROW_CHUNKS), :] = h[:, j * LANES:(j + 1) * LANES]
    logits = lax.dot_general(rw_ref[...], h, (((1,), (1,)), ((), ())), precision=HIGHEST,
                             preferred_element_type=F32) + rb_ref[...]
    expert = lax.broadcasted_iota(jnp.int32, logits.shape, 0)
    slot = lax.broadcasted_iota(jnp.int32, (TOP_K, logits.shape[1]), 0)
    vals = jnp.zeros((TOP_K, logits.shape[1]), F32)
    idxs = jnp.zeros((TOP_K, logits.shape[1]), jnp.int32)
    cur = logits
    for k in range(TOP_K):
        m = jnp.max(cur, axis=0, keepdims=True)
        a = jnp.min(jnp.where(cur == m, expert, N_EXPERTS), axis=0, keepdims=True)
        vals = jnp.where(slot == k, m, vals)
        idxs = jnp.where(slot == k, a, idxs)
        cur = jnp.where(expert == a, -jnp.inf, cur)
    e = jnp.exp(vals - vals[0:1])
    wt_ref[...] = e / jnp.sum(e, axis=0, keepdims=True)
    idx_ref[...] = idxs


def _dispatch_kernel(rows_ref, h_ref, o_ref):
    base = pl.program_id(0) * GATHER_ROWS
    for r in range(GATHER_ROWS):
        t = pl.multiple_of(rows_ref[base + r] * ROW_CHUNKS, ROW_CHUNKS)
        o_ref[r * ROW_CHUNKS:(r + 1) * ROW_CHUNKS, :] = h_ref[pl.ds(t, ROW_CHUNKS), :]


def _dispatch(h_tiles, gather_row):
    grid_spec = pltpu.PrefetchScalarGridSpec(
        num_scalar_prefetch=1,
        grid=(X_ROWS // GATHER_ROWS,),
        in_specs=[pl.BlockSpec((T_ALL * ROW_CHUNKS, LANES), lambda i, rows: (0, 0), pipeline_mode=pl.Buffered(1))],
        out_specs=pl.BlockSpec((GATHER_ROWS * ROW_CHUNKS, LANES), lambda i, rows: (i, 0)),
    )
    return pl.pallas_call(
        _dispatch_kernel,
        grid_spec=grid_spec,
        out_shape=jax.ShapeDtypeStruct((X_ROWS * ROW_CHUNKS, LANES), F32),
        compiler_params=_cparams(1),
    )(gather_row, h_tiles)


def _deinterleave_matrix():
    s = np.zeros((256, 256), np.float32)
    j = np.arange(128)
    s[2 * j, j] = 1.0
    s[2 * j + 1, 128 + j] = 1.0
    return jnp.asarray(s)


def _weight_copies(layer, e, w1_hbm, w2_hbm, w1s_ref, w2s_ref, sem):
    copies = []
    r1 = D_MODEL // W1_DMA_CHUNKS
    for c in range(W1_DMA_CHUNKS):
        copies.append(pltpu.make_async_copy(w1_hbm.at[layer, e, pl.ds(c * r1, r1)],
                                            w1s_ref.at[pl.ds(c * r1, r1)], sem.at[c]))
    r2 = D_FF // W2_DMA_CHUNKS
    for c in range(W2_DMA_CHUNKS):
        copies.append(pltpu.make_async_copy(w2_hbm.at[layer, e, pl.ds(c * r2, r2)],
                                            w2s_ref.at[pl.ds(c * r2, r2)], sem.at[W1_DMA_CHUNKS + c]))
    return copies


def _expert_kernel(layer, te_ref, tf_ref, ne_ref, nu_ref, src_ref, xo_ref, vr_ref, x_ref, b1_ref, b2_ref, wt_ref, s_ref,
                   w1_hbm, w2_hbm, o_hbm, w1s_ref, w2s_ref, w1p_ref, w2p_ref, acc_ref, out_ref, wsem, osem):
    i = pl.program_id(0)
    half = 128
    copies = functools.partial(_weight_copies, layer, w1_hbm=w1_hbm, w2_hbm=w2_hbm,
                               w1s_ref=w1s_ref, w2s_ref=w2s_ref, sem=wsem)

    @pl.when(i == 0)
    def _():
        acc_ref[...] = jnp.zeros_like(acc_ref)
        out_ref[...] = jnp.zeros_like(out_ref)
        for cp in copies(te_ref[0]):
            cp.start()

    @pl.when(tf_ref[i] == 1)
    def _():
        for cp in copies(te_ref[i]):
            cp.wait()
        s = s_ref[...].astype(BF16)
        for c in range(2 * D_FF // 256):
            blk = jnp.dot(w1s_ref[:, c * 256:(c + 1) * 256].astype(BF16), s, preferred_element_type=F32)
            w1p_ref[:, c * half:(c + 1) * half] = blk[:, :half].astype(BF16)
            w1p_ref[:, D_FF + c * half:D_FF + (c + 1) * half] = blk[:, half:].astype(BF16)
        w2p_ref[...] = w2s_ref[...].astype(BF16)

        @pl.when(ne_ref[i] >= 0)
        def _():
            for cp in copies(ne_ref[i]):
                cp.start()

    def tile_step(m):
        base = i * MOE_TILE
        prev = (i + 1) % 2
        for r0 in range(0, MOE_TILE, SCATTER_GROUP):
            toks = [pl.multiple_of(src_ref[base + r0 + g] * ROW_CHUNKS, ROW_CHUNKS) for g in range(SCATTER_GROUP)]
            cur = [acc_ref[pl.ds(toks[g], ROW_CHUNKS), :] for g in range(SCATTER_GROUP)]
            add = [out_ref[prev, (r0 + g) * ROW_CHUNKS:(r0 + g + 1) * ROW_CHUNKS, :] for g in range(SCATTER_GROUP)]
            for g in range(SCATTER_GROUP):
                acc_ref[pl.ds(toks[g], ROW_CHUNKS), :] = cur[g] + add[g]
        x = jnp.concatenate([x_ref[pl.ds(j, m, stride=ROW_CHUNKS), :] for j in range(ROW_CHUNKS)], axis=1)
        a = jnp.dot(x.astype(BF16), w1p_ref[...], preferred_element_type=F32) + b1_ref[...]
        glu = jnp.minimum(a[:, :D_FF], SWIGLU_LIMIT)
        lin = jnp.clip(a[:, D_FF:], -SWIGLU_LIMIT, SWIGLU_LIMIT)
        hid = glu * _sigmoid(SWIGLU_ALPHA * glu) * (lin + 1.0)
        out = (jnp.dot(hid.astype(BF16), w2p_ref[...], preferred_element_type=F32) + b2_ref[...]) * wt_ref[0:m, :]
        cur_buf = i % 2
        for j in range(ROW_CHUNKS):
            out_ref[cur_buf, pl.ds(j, m, stride=ROW_CHUNKS), :] = out[:, j * LANES:(j + 1) * LANES]

    live = i <= nu_ref[0]
    pl.when(live & (vr_ref[i] > MOE_TILE // 2))(functools.partial(tile_step, MOE_TILE))
    pl.when(live & (vr_ref[i] <= MOE_TILE // 2))(functools.partial(tile_step, MOE_TILE // 2))

    @pl.when(i == pl.num_programs(0) - 1)
    def _():
        cp = pltpu.make_async_copy(acc_ref.at[pl.ds(0, T_ALL * ROW_CHUNKS)], o_hbm, osem)
        cp.start()
        cp.wait()


def _experts(layer, x_sorted, w_sorted, plan, w1, b1p, w2, b2):
    tile_expert, tile_first, next_expert, n_used, src, x_off, valid_rows = plan
    grid_spec = pltpu.PrefetchScalarGridSpec(
        num_scalar_prefetch=7,
        grid=(MOE_TILES,),
        in_specs=[
            pl.BlockSpec((pl.Element(MOE_TILE * ROW_CHUNKS), pl.Element(LANES)),
                         lambda i, te, tf, ne, nu, src, xo, vr: (pl.multiple_of(xo[i] * ROW_CHUNKS, X_ALIGN * ROW_CHUNKS), 0)),
            pl.BlockSpec((None, None, 1, 2 * D_FF), lambda i, te, *_: (layer, te[i], 0, 0)),
            pl.BlockSpec((None, None, 1, D_MODEL), lambda i, te, *_: (layer, te[i], 0, 0)),
            pl.BlockSpec((MOE_TILE, 1), lambda i, te, *_: (i, 0)),
            pl.BlockSpec((256, 256), lambda i, te, *_: (0, 0)),
            pl.BlockSpec(memory_space=pl.ANY),
            pl.BlockSpec(memory_space=pl.ANY),
        ],
        out_specs=pl.BlockSpec(memory_space=pl.ANY),
        scratch_shapes=[
            pltpu.VMEM((D_MODEL, 2 * D_FF), F32),
            pltpu.VMEM((D_FF, D_MODEL), F32),
            pltpu.VMEM((D_MODEL, 2 * D_FF), BF16),
            pltpu.VMEM((D_FF, D_MODEL), BF16),
            pltpu.VMEM((ACC_ROWS * ROW_CHUNKS, LANES), F32),
            pltpu.VMEM((2, MOE_TILE * ROW_CHUNKS, LANES), F32),
            pltpu.SemaphoreType.DMA((W1_DMA_CHUNKS + W2_DMA_CHUNKS,)),
            pltpu.SemaphoreType.DMA(()),
        ],
    )
    return pl.pallas_call(
        functools.partial(_expert_kernel, layer),
        grid_spec=grid_spec,
        out_shape=jax.ShapeDtypeStruct((T_ALL * ROW_CHUNKS, LANES), F32),
        compiler_params=_cparams(1),
    )(tile_expert, tile_first, next_expert, n_used, src, x_off, valid_rows, x_sorted, b1p, b2, w_sorted,
      _deinterleave_matrix(), w1, w2)


def _combine_kernel(first_tile, y_ref, a_ref, ada_ref, o_ref):
    gate = _ada_chunk(ada_ref, _cond_row(first_tile + pl.program_id(0)), 5)
    acc = jnp.concatenate([a_ref[pl.ds(j, ROW_TILE, stride=ROW_CHUNKS), :] for j in range(ROW_CHUNKS)], axis=1)
    o_ref[...] = y_ref[...] + gate * acc


def _combine(y, acc, ada_l, first_tile=0, n_tiles=N_ROW_TILES):
    return pl.pallas_call(
        functools.partial(_combine_kernel, first_tile),
        grid=(n_tiles,),
        in_specs=[
            pl.BlockSpec((ROW_TILE, D_MODEL), lambda i: (first_tile + i, 0)),
            pl.BlockSpec((ROW_TILE * ROW_CHUNKS, LANES), lambda i: (first_tile + i, 0)),
            pl.BlockSpec((COND_ROWS, ADA_CHUNKS * D_MODEL), lambda i: (0, 0)),
        ],
        out_specs=pl.BlockSpec((ROW_TILE, D_MODEL), lambda i: (i, 0)),
        out_shape=jax.ShapeDtypeStruct((n_tiles * ROW_TILE, D_MODEL), F32),
        compiler_params=_cparams(1),
    )(y, acc, ada_l)


def _routing_plan(idx, wts):
    eid = idx.reshape(-1)
    order = jnp.argsort(eid, stable=True).astype(jnp.int32)
    experts = jnp.arange(N_EXPERTS, dtype=jnp.int32)
    counts = jnp.sum(eid[:, None] == experts[None, :], axis=0).astype(jnp.int32)
    ntiles = (counts + MOE_TILE - 1) // MOE_TILE
    tile_end = jnp.cumsum(ntiles).astype(jnp.int32)
    tile_begin = tile_end - ntiles
    cstarts = (jnp.cumsum(counts) - counts).astype(jnp.int32)
    n_used = tile_end[-1]
    tile = jnp.arange(MOE_TILES, dtype=jnp.int32)
    te = jnp.minimum(jnp.sum(tile[:, None] >= tile_end[None, :], axis=1), N_EXPERTS - 1).astype(jnp.int32)
    used = tile < n_used
    prev = jnp.concatenate([jnp.full((1,), -1, jnp.int32), te[:-1]])
    first = (te != prev) & used

    def pick(onehot, table):
        return jnp.sum(jnp.where(onehot, table[None, :], 0), axis=1).astype(jnp.int32)

    tile_is = te[:, None] == experts[None, :]
    later = (experts[None, :] > experts[:, None]) & (ntiles[None, :] > 0)
    following = jnp.min(jnp.where(later, experts[None, :], N_EXPERTS), axis=1)
    following = jnp.where(following < N_EXPERTS, following, -1)
    next_expert = jnp.where(first, pick(tile_is, following), -1).astype(jnp.int32)
    tile_in_expert = tile - pick(tile_is, tile_begin)
    off = tile_in_expert[:, None] * MOE_TILE + jnp.arange(MOE_TILE, dtype=jnp.int32)[None, :]
    valid_rows = jnp.where(used, jnp.clip(pick(tile_is, counts) - tile_in_expert * MOE_TILE, 0, MOE_TILE), 0)
    valid = jnp.arange(MOE_TILE, dtype=jnp.int32)[None, :] < valid_rows[:, None]
    assign = order[jnp.clip(pick(tile_is, cstarts)[:, None] + off, 0, N_ASSIGN - 1)]
    token = assign // TOP_K
    src = jnp.where(valid, token, SPARE_ROW).reshape(MOE_ROWS).astype(jnp.int32)
    src = jnp.concatenate([jnp.full((MOE_TILE,), SPARE_ROW, jnp.int32), src])
    w_sorted = jnp.where(valid, wts.reshape(-1)[assign], 0.0).reshape(MOE_ROWS, 1)
    seg = ((counts + X_ALIGN - 1) // X_ALIGN) * X_ALIGN
    seg_end = jnp.cumsum(seg).astype(jnp.int32)
    seg_begin = seg_end - seg
    x_off = jnp.where(used, pick(tile_is, seg_begin) + tile_in_expert * MOE_TILE, 0).astype(jnp.int32)
    group = jnp.arange(X_ROWS // X_ALIGN, dtype=jnp.int32) * X_ALIGN
    group_is = (group[:, None] >= seg_begin[None, :]) & (group[:, None] < seg_end[None, :])
    xoffset = (group - pick(group_is, seg_begin))[:, None] + jnp.arange(X_ALIGN, dtype=jnp.int32)[None, :]
    xassign = order[jnp.clip(pick(group_is, cstarts)[:, None] + xoffset, 0, N_ASSIGN - 1)]
    gather_row = jnp.where(xoffset < pick(group_is, counts)[:, None], xassign // TOP_K, 0).reshape(X_ROWS)
    plan = (te, first.astype(jnp.int32), next_expert, n_used.reshape(1), src, x_off, valid_rows.astype(jnp.int32))
    return plan, gather_row, w_sorted


def _moe(layer, y, routed, ada_l, w1, b1p, w2, b2):
    h, idx_t, wts_t = routed
    plan, gather_row, w_sorted = _routing_plan(idx_t.T, wts_t.T)
    x_sorted = _dispatch(h, gather_row)
    acc = _experts(layer, x_sorted, w_sorted, plan, w1, b1p, w2, b2)
    if layer == DEPTH - 1:
        return (_combine(y, acc, ada_l, 0, P_TILES), _combine(y, acc, ada_l, P_TILES, N_ROW_TILES - P_TILES))
    return _combine(y, acc, ada_l)


def kernel(x_prompt, x_sample, cache_attn_k, cache_attn_v, state_mlstm_C, state_mlstm_n, state_mlstm_m, c, c_ctx, ada_w, ada_b, norm_mix_g, norm_ffn_g, ab_w_in, ab_w_out, da_qnorm_g, da_knorm_g, da_lambda, da_subnorm_g, ml_conv_w, ml_conv_b, ml_gate_b, ml_headnorm_g, hy_w_in, hy_w_out, hy_conv_w, hy_conv_b, hy_f_w1, hy_f_b1, hy_f_freq1, hy_f_w2, hy_f_b2, hy_f_freq2, hy_f_w3, hy_bias, router_w, router_b, moe_w1, moe_b1, moe_w2, moe_b2):
    y = (x_prompt.reshape(T_P, D_MODEL), x_sample.reshape(T_S, D_MODEL))
    cond = jnp.concatenate([c_ctx[None, :], c, jnp.zeros((COND_ROWS - 1 - DEC_BATCH, D_MODEL), F32)], axis=0)
    ada = _ada_table(cond, ada_w, ada_b)
    b1p = moe_b1.reshape(DEPTH, N_EXPERTS, D_FF, 2).swapaxes(2, 3).reshape(DEPTH, N_EXPERTS, 1, 2 * D_FF)
    b2r = moe_b2.reshape(DEPTH, N_EXPERTS, 1, D_MODEL)
    new_k, new_v, new_c, new_n, new_m = [], [], [], [], []
    for layer in range(DEPTH):
        ada_l = ada[layer]
        route_prm = (norm_ffn_g[layer], router_w[layer], router_b[layer])
        if layer % 2 == 0:
            e = layer // 2
            lam_init = 0.8 - 0.6 * math.exp(-0.3 * layer)
            qkv, mqk, mv, mo, mg = _modulated_proj(
                y, ada_l, norm_mix_g[layer], ab_w_in[e], (3 * W_A, 2 * W_B, W_B, W_B, 4 * H_B))
            qg2 = jnp.tile(da_qnorm_g[e], 2).reshape(1, 2 * HD_A)
            kg2 = jnp.tile(da_knorm_g[e], 2).reshape(1, 2 * HD_A)
            sub_g = da_subnorm_g[e].reshape(1, 2 * HD_A)
            oa_p, k_norm, v_heads = _attention_prompt(qkv, qg2, kg2, da_lambda[e], sub_g, lam_init)
            cos, sin = _rope_tables()
            oa_s = _attention_sample(
                qkv, cache_attn_k[:, e].reshape(DEC_BATCH, PAST_LEN, W_A),
                cache_attn_v[:, e].reshape(DEC_BATCH, PAST_LEN, W_A), cos, sin,
                qg2, kg2, da_lambda[e], sub_g, lam_init)
            ob_p, c_new, n_new, m_new = _mlstm(
                mqk, mv, mo, mg[:T_P], ml_conv_w[e], ml_conv_b[e].reshape(1, 2 * W_B), ml_gate_b[e],
                ml_headnorm_g[e], seq=SEQ, nbatch=BATCH, row_off=0, group=MLSTM_GROUP)
            ob_s = _mlstm(
                mqk, mv, mo, mg[T_P:], ml_conv_w[e], ml_conv_b[e].reshape(1, 2 * W_B), ml_gate_b[e],
                ml_headnorm_g[e], seq=DEC_SEQ, nbatch=DEC_BATCH, row_off=T_P // DEC_SEQ, group=1,
                ctx=(state_mlstm_C[:, e], state_mlstm_n[:, e], state_mlstm_m[:, e]))
            y, routed = _out_proj_residual([(oa_p, oa_s), (ob_p, ob_s)], y, ada_l, ab_w_out[e], 2, *route_prm)
            new_k.append(k_norm.reshape(BATCH, SEQ, H_A, 2, HD_A))
            new_v.append(v_heads.reshape(BATCH, SEQ, H_A, 2 * HD_A))
            new_c.append(c_new)
            new_n.append(n_new.reshape(BATCH, 2, H_B, HD_B))
            new_m.append(m_new[..., 0, 0])
        else:
            o = layer // 2
            (zproj,) = _modulated_proj(y, ada_l, norm_mix_g[layer], hy_w_in[o], (HY_PROJ,))
            cores = []
            for seq, nbatch, row_off, td in ((SEQ, BATCH, 0, 512), (DEC_SEQ, DEC_BATCH, T_P // DEC_SEQ, 256)):
                fwd, inv = _dft_mats(seq)
                fwd_bf, inv_bf = fwd.astype(BF16), inv.astype(BF16)
                kspec = _hyena_filter_spectrum(seq, fwd_bf, hy_f_w1[o], hy_f_b1[o], hy_f_freq1[o], hy_f_w2[o],
                                               hy_f_b2[o], hy_f_freq2[o], hy_f_w3[o])
                cores.append(_hyena_core(zproj, hy_conv_w[o], hy_conv_b[o].reshape(1, HY_PROJ), fwd_bf, inv_bf,
                                         kspec, hy_bias[o], seq=seq, nbatch=nbatch, row_off=row_off, td=td))
            y, routed = _out_proj_residual([tuple(cores)], y, ada_l, hy_w_out[o], 2, *route_prm)
        y = _moe(layer, y, routed, ada_l, moe_w1, b1p, moe_w2, b2r)
    y_p = y[0].reshape(BATCH, SEQ, D_MODEL)
    y_s = y[1].reshape(DEC_BATCH, DEC_SEQ, D_MODEL)
    return (y_p, y_s, jnp.stack(new_k, axis=1), jnp.stack(new_v, axis=1), jnp.stack(new_c, axis=1),
            jnp.stack(new_n, axis=1), jnp.stack(new_m, axis=1))
```

```python
import functools
import math

import numpy as np
import jax
import jax.numpy as jnp
from jax import lax
from jax.experimental import pallas as pl
from jax.experimental.pallas import tpu as pltpu

D_MODEL = 1024
BATCH = 16
SEQ = 256
DEPTH = 2
DEC_BATCH = 2
DEC_SEQ = 1024
PAST_LEN = 256
GRID_W = 64
W_A = D_MODEL // 2
HD_A = 64
H_A = W_A // (2 * HD_A)
W_B = D_MODEL - W_A
HD_B = 128
H_B = W_B // HD_B
AB_PROJ = 3 * W_A + 4 * W_B + 4 * H_B
ROPE_BASE = 10000.0
CHUNK = 64
HY_ORDER = 2
HY_PROJ = (HY_ORDER + 1) * D_MODEL
HY_BANDS = 8
HY_FH = 64
HY_TARGET = 1e-2
HY_FAST_PCT = 0.3
HY_SLOW_PCT = 1.5
N_EXPERTS = 32
TOP_K = 4
D_FF = D_MODEL
SWIGLU_ALPHA = 1.702
SWIGLU_LIMIT = 7.0
ADA_CHUNKS = 6
EPS = 1e-6
NEG = -1e30
F32 = jnp.float32
BF16 = jnp.bfloat16

T_P = BATCH * SEQ
T_S = DEC_BATCH * DEC_SEQ
T_ALL = T_P + T_S
ROW_TILE = 256
N_ROW_TILES = T_ALL // ROW_TILE
P_TILES = T_P // ROW_TILE
S_TILES_PER_BATCH = DEC_SEQ // ROW_TILE
COND_ROWS = 8
MOE_TILE = 256
N_ASSIGN = T_ALL * TOP_K
MOE_ROWS = N_ASSIGN + N_EXPERTS * MOE_TILE
MOE_TILES = MOE_ROWS // MOE_TILE
X_ALIGN = 16
MLSTM_GROUP = 2
GATHER_ROWS = 1024
X_ROWS = -(-(N_ASSIGN + N_EXPERTS * X_ALIGN + MOE_TILE) // GATHER_ROWS) * GATHER_ROWS
SPARE_ROW = T_ALL
ACC_ROWS = T_ALL + 8
SCATTER_GROUP = 8
W1_DMA_CHUNKS = 8
W2_DMA_CHUNKS = 4
LANES = 128
ROW_CHUNKS = D_MODEL // LANES
VMEM_LIMIT = 56 * 1024 * 1024
HIGHEST = lax.Precision.HIGHEST


def _cparams(n_axes):
    return pltpu.CompilerParams(dimension_semantics=("arbitrary",) * n_axes,
                                vmem_limit_bytes=VMEM_LIMIT)


def _bdot(a, b):
    return jnp.dot(a.astype(BF16), b.astype(BF16), preferred_element_type=F32)


def _cond_row(i):
    return jnp.where(i < P_TILES, 0, 1 + (i - P_TILES) // S_TILES_PER_BATCH)


def _ada_chunk(ada_ref, row, j):
    return ada_ref[pl.ds(row, 1), j * D_MODEL:(j + 1) * D_MODEL]


def _modulate(x, g, shift, scale):
    ms = jnp.mean(x * x, axis=-1, keepdims=True)
    return (x * lax.rsqrt(ms + EPS) * g) * (1.0 + scale) + shift


def _sigmoid(x):
    return 1.0 / (1.0 + jnp.exp(-x))


def _silu(x):
    return x * _sigmoid(x)


def _log_sigmoid(x):
    return jnp.minimum(x, 0.0) - jnp.log(1.0 + jnp.exp(-jnp.abs(x)))


def _dwconv3(x, w, b, seq=None):
    n = x.shape[0]
    seq = n if seq is None else seq
    pos = lax.broadcasted_iota(jnp.int32, x.shape, 0) % seq
    prev = jnp.where(pos == 0, 0.0, pltpu.roll(x, 1, 0))
    nxt = jnp.where(pos == seq - 1, 0.0, pltpu.roll(x, n - 1, 0))
    return prev * w[0:1] + x * w[1:2] + nxt * w[2:3] + b


def _ada_kernel(cond_ref, w_ref, b_ref, o_ref):
    c = _silu(cond_ref[...])
    o_ref[...] = _bdot(c, w_ref[...]) + b_ref[...]


def _ada_table(cond, ada_w, ada_b):
    tn = 1536
    return pl.pallas_call(
        _ada_kernel,
        grid=(DEPTH, ADA_CHUNKS * D_MODEL // tn),
        in_specs=[
            pl.BlockSpec((COND_ROWS, D_MODEL), lambda l, j: (0, 0)),
            pl.BlockSpec((None, D_MODEL, tn), lambda l, j: (l, 0, j)),
            pl.BlockSpec((None, 1, tn), lambda l, j: (l, 0, j)),
        ],
        out_specs=pl.BlockSpec((None, COND_ROWS, tn), lambda l, j: (l, 0, j)),
        out_shape=jax.ShapeDtypeStruct((DEPTH, COND_ROWS, ADA_CHUNKS * D_MODEL), F32),
        compiler_params=_cparams(2),
    )(cond, ada_w, ada_b.reshape(DEPTH, 1, ADA_CHUNKS * D_MODEL))


def _stream_specs(y):
    if isinstance(y, tuple):
        return [pl.BlockSpec((ROW_TILE, D_MODEL), lambda i: (jnp.minimum(i, P_TILES - 1), 0)),
                pl.BlockSpec((ROW_TILE, D_MODEL), lambda i: (jnp.maximum(i - P_TILES, 0), 0))], list(y)
    return [pl.BlockSpec((ROW_TILE, D_MODEL), lambda i: (i, 0))], [y]


def _token_rows(ref, n):
    return jnp.concatenate([ref[pl.ds(j, n, stride=ROW_CHUNKS), :] for j in range(ROW_CHUNKS)], axis=1)


def _stream_tile(y_refs, i):
    if len(y_refs) == 2:
        return jnp.where(i < P_TILES, y_refs[0][...], y_refs[1][...])
    return y_refs[0][...]


def _proj_kernel(splits, n_y, has_pending, *refs):
    y_refs = refs[:n_y]
    refs = refs[n_y:]
    if has_pending:
        acc_ref, ada_prev_ref = refs[:2]
        refs = refs[2:]
    ada_ref, g_ref, w_ref = refs[:3]
    out_refs, wbf_ref = refs[3:-1], refs[-1]
    i = pl.program_id(0)

    @pl.when(i == 0)
    def _():
        wbf_ref[...] = w_ref[...].astype(BF16)

    row = _cond_row(i)
    y = _stream_tile(y_refs, i)
    if has_pending:
        y = y + _ada_chunk(ada_prev_ref, row, 5) * _token_rows(acc_ref, ROW_TILE)
        out_refs[-1][...] = y
        out_refs = out_refs[:-1]
    h = _modulate(y, g_ref[...], _ada_chunk(ada_ref, row, 0), _ada_chunk(ada_ref, row, 1))
    h = h.astype(BF16)
    lo = 0
    for o_ref, width in zip(out_refs, splits):
        o_ref[...] = jnp.dot(h, wbf_ref[:, lo:lo + width], preferred_element_type=F32)
        lo += width


def _modulated_proj(y, ada_l, g, w, splits, pending=None):
    n = w.shape[1]
    y_specs, y_args = _stream_specs(y)
    widths = tuple(splits)
    if pending is not None:
        y_specs = y_specs + [pl.BlockSpec((ROW_TILE * ROW_CHUNKS, LANES), lambda i: (i, 0)),
                             pl.BlockSpec((COND_ROWS, ADA_CHUNKS * D_MODEL), lambda i: (0, 0))]
        y_args = y_args + list(pending)
        widths = widths + (D_MODEL,)
    return pl.pallas_call(
        functools.partial(_proj_kernel, splits, len(y_args) - (2 if pending is not None else 0), pending is not None),
        grid=(N_ROW_TILES,),
        in_specs=y_specs + [
            pl.BlockSpec((COND_ROWS, ADA_CHUNKS * D_MODEL), lambda i: (0, 0)),
            pl.BlockSpec((1, D_MODEL), lambda i: (0, 0)),
            pl.BlockSpec((D_MODEL, n), lambda i: (0, 0), pipeline_mode=pl.Buffered(1)),
        ],
        out_specs=[pl.BlockSpec((ROW_TILE, s), lambda i: (i, 0)) for s in widths],
        out_shape=[jax.ShapeDtypeStruct((T_ALL, s), F32) for s in widths],
        scratch_shapes=[pltpu.VMEM((D_MODEL, n), BF16)],
        compiler_params=_cparams(1),
    )(*y_args, ada_l, g.reshape(1, D_MODEL), w)


def _out_proj_kernel(n_in, n_y, gate_chunk, *refs):
    x_refs = refs[:2 * n_in]
    y_refs = refs[2 * n_in:2 * n_in + n_y]
    ada_ref, w_ref, g_ref, rw_ref, rb_ref, o_ref, h_ref, idx_ref, wt_ref, wbf_ref = refs[2 * n_in + n_y:]
    i = pl.program_id(0)

    @pl.when(i == 0)
    def _():
        wbf_ref[...] = w_ref[...].astype(BF16)

    acc = None
    lo = 0
    for xp_ref, xs_ref in zip(x_refs[0::2], x_refs[1::2]):
        k = xp_ref.shape[1]
        x = jnp.where(i < P_TILES, xp_ref[...], xs_ref[...])
        part = jnp.dot(x.astype(BF16), wbf_ref[lo:lo + k, :], preferred_element_type=F32)
        acc = part if acc is None else acc + part
        lo += k
    row = _cond_row(i)
    y_new = _stream_tile(y_refs, i) + _ada_chunk(ada_ref, row, gate_chunk) * acc
    o_ref[...] = y_new
    _route_tile(y_new, row, ada_ref, g_ref, rw_ref, rb_ref, h_ref, idx_ref, wt_ref)


def _out_proj_residual(xs, y, ada_l, w, gate_chunk, ffn_g, router_w, router_b):
    y_specs, y_args = _stream_specs(y)
    x_specs = []
    for xp, _ in xs:
        x_specs.append(pl.BlockSpec((ROW_TILE, xp.shape[1]), lambda i: (jnp.minimum(i, P_TILES - 1), 0)))
        x_specs.append(pl.BlockSpec((ROW_TILE, xp.shape[1]), lambda i: (jnp.maximum(i - P_TILES, 0), 0)))
    y_new, h, idx_t, wts_t = pl.pallas_call(
        functools.partial(_out_proj_kernel, len(xs), len(y_args), gate_chunk),
        grid=(N_ROW_TILES,),
        in_specs=x_specs + y_specs + [
            pl.BlockSpec((COND_ROWS, ADA_CHUNKS * D_MODEL), lambda i: (0, 0)),
            pl.BlockSpec((D_MODEL, D_MODEL), lambda i: (0, 0), pipeline_mode=pl.Buffered(1)),
            pl.BlockSpec((1, D_MODEL), lambda i: (0, 0)),
            pl.BlockSpec((N_EXPERTS, D_MODEL), lambda i: (0, 0)),
            pl.BlockSpec((N_EXPERTS, 1), lambda i: (0, 0)),
        ],
        out_specs=[
            pl.BlockSpec((ROW_TILE, D_MODEL), lambda i: (i, 0)),
            pl.BlockSpec((ROW_TILE * ROW_CHUNKS, LANES), lambda i: (i, 0)),
            pl.BlockSpec((TOP_K, ROW_TILE), lambda i: (0, i)),
            pl.BlockSpec((TOP_K, ROW_TILE), lambda i: (0, i)),
        ],
        out_shape=[
            jax.ShapeDtypeStruct((T_ALL, D_MODEL), F32),
            jax.ShapeDtypeStruct((T_ALL * ROW_CHUNKS, LANES), F32),
            jax.ShapeDtypeStruct((TOP_K, T_ALL), jnp.int32),
            jax.ShapeDtypeStruct((TOP_K, T_ALL), F32),
        ],
        scratch_shapes=[pltpu.VMEM((D_MODEL, D_MODEL), BF16)],
        compiler_params=_cparams(1),
    )(*[a for pair in xs for a in pair], *y_args, ada_l, w,
      ffn_g.reshape(1, D_MODEL), router_w.T, router_b.reshape(N_EXPERTS, 1))
    return y_new, (h, idx_t, wts_t)


def _subhead_norm(x, g2):
    lane = lax.broadcasted_iota(jnp.int32, x.shape, 1)
    first = lane < HD_A
    xx = x * x
    s0 = jnp.sum(jnp.where(first, xx, 0.0), axis=-1, keepdims=True)
    s1 = jnp.sum(jnp.where(first, 0.0, xx), axis=-1, keepdims=True)
    r = jnp.where(first, lax.rsqrt(s0 / HD_A + EPS), lax.rsqrt(s1 / HD_A + EPS))
    return x * r * g2


def _rope(x, cos, sin):
    quarter = HD_A // 4
    lane = lax.broadcasted_iota(jnp.int32, x.shape, 1)
    lower = (lane % (2 * quarter)) < quarter
    swapped = jnp.where(lower, pltpu.roll(x, 2 * HD_A - quarter, 1), pltpu.roll(x, quarter, 1))
    return x * cos + swapped * sin


def _attn_kernel(lam_init, has_ctx, *refs):
    if has_ctx:
        (q_ref, k_ref, v_ref, ck_ref, cv_ref, cq_ref, sq_ref, ckk_ref, skk_ref,
         qg_ref, kg_ref, lp_ref, sg_ref, o_ref, kall_ref, vall_ref) = refs
    else:
        q_ref, k_ref, v_ref, qg_ref, kg_ref, lp_ref, sg_ref, o_ref, kn_ref, vh_ref = refs
    lp = lp_ref[...]
    lam = (jnp.exp(jnp.sum(lp[0:1] * lp[1:2], axis=-1, keepdims=True))
           - jnp.exp(jnp.sum(lp[2:3] * lp[3:4], axis=-1, keepdims=True)) + lam_init)

    def attend(q, k, v):
        probs = []
        for c in range(2):
            qc = q[:, c * HD_A:(c + 1) * HD_A].astype(BF16)
            kc = k[:, c * HD_A:(c + 1) * HD_A].astype(BF16)
            s = lax.dot_general(qc, kc, (((1,), (1,)), ((), ())), preferred_element_type=F32) * (HD_A ** -0.5)
            e = jnp.exp(s - jnp.max(s, axis=-1, keepdims=True))
            probs.append(e / jnp.sum(e, axis=-1, keepdims=True))
        o = _bdot(probs[0] - lam * probs[1], v)
        ms = jnp.mean(o * o, axis=-1, keepdims=True)
        return (o * lax.rsqrt(ms + EPS) * sg_ref[...]) * (1.0 - lam_init)

    if not has_ctx:
        for h in range(H_A):
            cols = slice(h * 2 * HD_A, (h + 1) * 2 * HD_A)
            k = _subhead_norm(k_ref[:, cols], kg_ref[...])
            for c in range(2):
                kn_ref[pl.ds(2 * h + c, SEQ, stride=2 * H_A), :] = k[:, c * HD_A:(c + 1) * HD_A]
            v = v_ref[:, cols]
            vh_ref[pl.ds(h, SEQ, stride=H_A), :] = v
            o_ref[:, cols] = attend(_subhead_norm(q_ref[:, cols], qg_ref[...]), k, v)
        return

    @pl.when(pl.program_id(2) == 0)
    def _():
        kall_ref[0:PAST_LEN, :] = ck_ref[...].astype(BF16)
        vall_ref[0:PAST_LEN, :] = cv_ref[...].astype(BF16)
        k_new = _rope(_subhead_norm(k_ref[...], kg_ref[...]), ckk_ref[...], skk_ref[...])
        kall_ref[PAST_LEN:, :] = k_new.astype(BF16)
        vall_ref[PAST_LEN:, :] = v_ref[...].astype(BF16)

    q = _rope(_subhead_norm(q_ref[...], qg_ref[...]), cq_ref[...], sq_ref[...])
    o_ref[...] = attend(q, kall_ref[...], vall_ref[...])


def _attention_prompt(qkv, qg2, kg2, lam_p, sub_g, lam_init):
    head = 2 * HD_A
    small = [
        pl.BlockSpec((1, head), lambda b: (0, 0)),
        pl.BlockSpec((1, head), lambda b: (0, 0)),
        pl.BlockSpec((4, HD_A), lambda b: (0, 0)),
        pl.BlockSpec((1, head), lambda b: (0, 0)),
    ]
    return pl.pallas_call(
        functools.partial(_attn_kernel, lam_init, False),
        grid=(BATCH,),
        in_specs=[
            pl.BlockSpec((SEQ, W_A), lambda b: (b, 0)),
            pl.BlockSpec((SEQ, W_A), lambda b: (b, 1)),
            pl.BlockSpec((SEQ, W_A), lambda b: (b, 2)),
        ] + small,
        out_specs=[pl.BlockSpec((SEQ, W_A), lambda b: (b, 0)),
                   pl.BlockSpec((SEQ * 2 * H_A, HD_A), lambda b: (b, 0)),
                   pl.BlockSpec((SEQ * H_A, head), lambda b: (b, 0))],
        out_shape=[jax.ShapeDtypeStruct((T_P, W_A), F32), jax.ShapeDtypeStruct((T_P * 2 * H_A, HD_A), F32),
                   jax.ShapeDtypeStruct((T_P * H_A, head), F32)],
        compiler_params=_cparams(1),
    )(qkv, qkv, qkv, qg2, kg2, lam_p, sub_g)


def _attention_sample(qkv, cache_k, cache_v, cos, sin, qg2, kg2, lam_p, sub_g, lam_init):
    nh = H_A
    head = 2 * HD_A
    tq = ROW_TILE
    nq = DEC_SEQ // tq
    q_off = T_P // tq
    k_off = T_P // DEC_SEQ
    small = [
        pl.BlockSpec((1, head), lambda b, h, i: (0, 0)),
        pl.BlockSpec((1, head), lambda b, h, i: (0, 0)),
        pl.BlockSpec((4, HD_A), lambda b, h, i: (0, 0)),
        pl.BlockSpec((1, head), lambda b, h, i: (0, 0)),
    ]
    return pl.pallas_call(
        functools.partial(_attn_kernel, lam_init, True),
        grid=(DEC_BATCH, nh, nq),
        in_specs=[
            pl.BlockSpec((tq, head), lambda b, h, i: (q_off + b * nq + i, h)),
            pl.BlockSpec((DEC_SEQ, head), lambda b, h, i: (k_off + b, nh + h)),
            pl.BlockSpec((DEC_SEQ, head), lambda b, h, i: (k_off + b, 2 * nh + h)),
            pl.BlockSpec((None, PAST_LEN, head), lambda b, h, i: (b, 0, h)),
            pl.BlockSpec((None, PAST_LEN, head), lambda b, h, i: (b, 0, h)),
            pl.BlockSpec((tq, head), lambda b, h, i: (i, 0)),
            pl.BlockSpec((tq, head), lambda b, h, i: (i, 0)),
            pl.BlockSpec((DEC_SEQ, head), lambda b, h, i: (0, 0)),
            pl.BlockSpec((DEC_SEQ, head), lambda b, h, i: (0, 0)),
        ] + small,
        out_specs=pl.BlockSpec((tq, head), lambda b, h, i: (b * nq + i, h)),
        out_shape=jax.ShapeDtypeStruct((T_S, W_A), F32),
        scratch_shapes=[pltpu.VMEM((PAST_LEN + DEC_SEQ, head), BF16)] * 2,
        compiler_params=_cparams(3),
    )(qkv, qkv, qkv, cache_k, cache_v, cos, sin, cos, sin, qg2, kg2, lam_p, sub_g)


def _rope_tables():
    half = HD_A // 2
    nf = half // 2
    inv = ROPE_BASE ** (-np.arange(nf, dtype=np.float32) / nf)
    pos = np.arange(DEC_SEQ)
    row = (pos // GRID_W).astype(np.float32)
    col = (pos % GRID_W).astype(np.float32)
    ang_r = (row[:, None] * inv).astype(np.float32)
    ang_c = (col[:, None] * inv).astype(np.float32)
    ang = np.concatenate([ang_r, ang_r, ang_c, ang_c], axis=1)
    sign = np.concatenate([-np.ones(nf), np.ones(nf), -np.ones(nf), np.ones(nf)]).astype(np.float32)
    cos = np.cos(ang.astype(np.float64)).astype(np.float32)
    sin = (np.sin(ang.astype(np.float64)) * sign).astype(np.float32)
    return jnp.asarray(np.tile(cos, (1, 2))), jnp.asarray(np.tile(sin, (1, 2)))


def _mlstm_kernel(seq, has_ctx, group, *refs):
    if has_ctx:
        (q_ref, k_ref, cwq_ref, cwk_ref, cbq_ref, cbk_ref, v_ref, mo_ref, gi_ref, gf_ref,
         gbi_ref, gbf_ref, hn_ref, c0_ref, n0_ref, m0_ref, o_ref,
         qs_ref, ks_ref, hf_ref, hb_ref, cs_ref, rrow_ref, col_ref, wc_ref) = refs
    else:
        (q_ref, k_ref, cwq_ref, cwk_ref, cbq_ref, cbk_ref, v_ref, mo_ref, gi_ref, gf_ref,
         gbi_ref, gbf_ref, hn_ref, o_ref, c_out_ref, n_out_ref, m_out_ref,
         qs_ref, ks_ref, hf_ref, hb_ref, cs_ref, rrow_ref, col_ref, wc_ref) = refs
    nc = seq // CHUNK
    n_chain = 2 * H_B
    chains = [(sub, d, h) for sub in range(group) for d in range(2) for h in range(H_B)]
    qs_ref[...] = _silu(_dwconv3(q_ref[...], cwq_ref[...], cbq_ref[...], seq)) * (HD_B ** -0.5)
    ks_ref[...] = _silu(_dwconv3(k_ref[...], cwk_ref[...], cbk_ref[...], seq))

    rows = nc * n_chain
    lane = lax.broadcasted_iota(jnp.int32, (rows, 2 * CHUNK), 1)
    forward = lax.broadcasted_iota(jnp.int32, (rows, 2 * CHUNK), 0) % n_chain < H_B
    valid = lane < CHUNK

    def scan(x, op, fill):
        pre, suf = x, x
        sh = 1
        while sh < CHUNK:
            pre = op(pre, jnp.where(lane >= sh, pltpu.roll(pre, sh, 1), fill))
            suf = op(suf, jnp.where(lane + sh < CHUNK, pltpu.roll(suf, 2 * CHUNK - sh, 1), fill))
            sh *= 2
        return jnp.where(forward, pre, suf)

    mm_final = []
    for sub in range(group):
        gate_i = (gi_ref[sub] + gbi_ref[...]).reshape(rows, 2 * CHUNK)
        lf = jnp.where(valid, _log_sigmoid(gf_ref[sub] + gbf_ref[...]).reshape(rows, 2 * CHUNK), 0.0)
        b = scan(lf, jnp.add, 0.0)
        cmax = scan(jnp.where(valid, gate_i - b, -jnp.inf), jnp.maximum, -jnp.inf)
        b_last = jnp.sum(lf, axis=1, keepdims=True)
        g = b_last - b + gate_i
        g_max = jnp.max(jnp.where(valid, g, -jnp.inf), axis=1, keepdims=True)
        mm = m0_ref[sub] if has_ctx else jnp.zeros((n_chain, 1), F32)
        mm_seq = []
        for p in range(nc):
            mm_seq.append(mm)
            seg = slice(p * n_chain, (p + 1) * n_chain)
            mm = jnp.maximum(b_last[seg] + mm, g_max[seg])
        mm_final.append(mm)
        mm_prev = jnp.concatenate(mm_seq, axis=0)
        mm_next = jnp.concatenate(mm_seq[1:] + [mm], axis=0)
        m_t = jnp.maximum(b + mm_prev, b + cmax)
        rrow_ref[sub] = (b - gate_i).reshape(nc, n_chain, 2 * CHUNK)
        wc_ref[sub] = jnp.exp(b_last + mm_prev - mm_next).reshape(nc, n_chain, 1)
        per_row = [b, m_t, jnp.exp(b + mm_prev - m_t), jnp.exp(-m_t), jnp.exp(g - mm_next)]
        for j, arr in enumerate(per_row):
            by_time = arr.T
            for p in range(nc):
                col_ref[sub, p, :, j * n_chain:(j + 1) * n_chain] = by_time[0:CHUNK, p * n_chain:(p + 1) * n_chain]

    t_idx = lax.broadcasted_iota(jnp.int32, (CHUNK, CHUNK), 0)
    s_idx = lax.broadcasted_iota(jnp.int32, (CHUNK, CHUNK), 1)
    for n, (sub, d, h) in enumerate(chains):
        cs_ref[n] = c0_ref[sub, d, h] if has_ctx else jnp.zeros((HD_B, HD_B), F32)

    def out_step(p, n_states):
        new_states = []
        for n, (sub, d, h) in enumerate(chains):
            cols = col_ref[sub, p]
            rrows = rrow_ref[sub, p]
            wcs = wc_ref[sub, p]
            n_loc = d * H_B + h
            c = p if d == 0 else nc - 1 - p
            r0 = pl.multiple_of(sub * seq + c * CHUNK, CHUNK)
            hcols = slice(h * HD_B, (h + 1) * HD_B)
            qt = qs_ref[pl.ds(r0, CHUNK), hcols]
            kt = ks_ref[pl.ds(r0, CHUNK), hcols]
            vt = v_ref[pl.ds(r0, CHUNK), hcols]
            b_col, m_t, w_inter, e_inv, w_k = (cols[:, j * n_chain + n_loc:j * n_chain + n_loc + 1] for j in range(5))
            mask = (s_idx <= t_idx) if d == 0 else (s_idx >= t_idx)
            decay = jnp.exp(jnp.where(mask, b_col - rrows[n_loc:n_loc + 1, 0:CHUNK], NEG) - m_t)
            qk = lax.dot_general(qt.astype(BF16), kt.astype(BF16), (((1,), (1,)), ((), ())),
                                 preferred_element_type=F32)
            s = qk * decay
            cm = cs_ref[n]
            nm = n_states[n]
            cq = lax.dot_general(qt.astype(BF16), cm.astype(BF16), (((1,), (1,)), ((), ())),
                                 preferred_element_type=F32)
            num = _bdot(s, vt) + w_inter * cq
            nq = jnp.sum(s, axis=-1, keepdims=True) + w_inter * jnp.sum(qt * nm, axis=-1, keepdims=True)
            hdir_ref = hf_ref if d == 0 else hb_ref
            hdir_ref[pl.ds(r0, CHUNK), hcols] = num / jnp.maximum(jnp.abs(nq), e_inv)
            w_c = wcs[n_loc:n_loc + 1, :]
            vw = (vt * w_k).astype(BF16)
            cs_ref[n] = w_c * cm + lax.dot_general(vw, kt.astype(BF16), (((0,), (0,)), ((), ())),
                                                   preferred_element_type=F32)
            new_states.append(w_c * nm + jnp.sum(kt * w_k, axis=0, keepdims=True))
        return tuple(new_states)

    if has_ctx:
        n_init = tuple(n0_ref[sub, d, h] for sub, d, h in chains)
    else:
        n_init = tuple(jnp.zeros((1, HD_B), F32) for _ in chains)
    n_final = lax.fori_loop(0, nc, out_step, n_init)
    if not has_ctx:
        for n, (sub, d, h) in enumerate(chains):
            n_loc = d * H_B + h
            c_out_ref[sub, d, h] = cs_ref[n]
            n_out_ref[sub, d, h] = n_final[n]
            m_out_ref[sub, d, h] = jnp.broadcast_to(mm_final[sub][n_loc:n_loc + 1, :], (1, HD_B))

    for h in range(H_B):
        hcols = slice(h * HD_B, (h + 1) * HD_B)
        hh = hf_ref[:, hcols] + hb_ref[:, hcols]
        ms = jnp.mean(hh * hh, axis=-1, keepdims=True)
        o_ref[:, hcols] = (hh * lax.rsqrt(ms + EPS) * hn_ref[:, hcols]) * _sigmoid(mo_ref[:, hcols])


def _mlstm(mqk, mv, mo, mg_stream, conv_w, conv_b, gate_b, hn_g, *, seq, nbatch, row_off, group, ctx=None):
    nh = H_B
    nc = seq // CHUNK
    has_ctx = ctx is not None
    assert nbatch % group == 0
    gt = mg_stream.reshape(nbatch, nc, CHUNK, 2, 2, nh).transpose(0, 1, 3, 4, 5, 2)
    pad = ((0, 0), (0, 0), (0, 0), (0, CHUNK))
    gates = [jnp.pad(jnp.concatenate([gt[:, :, 0, j], gt[:, ::-1, 1, j]], axis=2), pad) for j in range(2)]
    gate_bias = [jnp.concatenate([gate_b[0, j], gate_b[1, j]]).reshape(2 * nh, 1) for j in range(2)]
    blk = lambda col: pl.BlockSpec((group * seq, W_B), lambda b, col=col: (row_off + b, col))
    gate_blk = pl.BlockSpec((group, nc, 2 * nh, 2 * CHUNK), lambda b: (b, 0, 0, 0))
    in_specs = [
        blk(0), blk(1),
        pl.BlockSpec((3, W_B), lambda b: (0, 0)),
        pl.BlockSpec((3, W_B), lambda b: (0, 1)),
        pl.BlockSpec((1, W_B), lambda b: (0, 0)),
        pl.BlockSpec((1, W_B), lambda b: (0, 1)),
        blk(0), blk(0),
        gate_blk, gate_blk,
        pl.BlockSpec((2 * nh, 1), lambda b: (0, 0)),
        pl.BlockSpec((2 * nh, 1), lambda b: (0, 0)),
        pl.BlockSpec((1, W_B), lambda b: (0, 0)),
    ]
    args = [mqk, mqk, conv_w, conv_w, conv_b, conv_b, mv, mo, gates[0], gates[1],
            gate_bias[0], gate_bias[1], hn_g.reshape(1, W_B)]
    o_spec = pl.BlockSpec((group * seq, W_B), lambda b: (b, 0))
    o_shape = jax.ShapeDtypeStruct((nbatch * seq, W_B), F32)
    state_blk = lambda rows: pl.BlockSpec((group, 2, nh, rows, HD_B), lambda b: (b, 0, 0, 0, 0))
    if has_ctx:
        c0, n0, m0 = ctx
        in_specs += [state_blk(HD_B), state_blk(1), pl.BlockSpec((group, 2 * nh, 1), lambda b: (b, 0, 0))]
        args += [c0, n0.reshape(nbatch, 2, nh, 1, HD_B), m0.reshape(nbatch, 2 * nh, 1)]
        out_specs, out_shape = o_spec, o_shape
    else:
        out_specs = [o_spec, state_blk(HD_B), state_blk(1), state_blk(1)]
        out_shape = [
            o_shape,
            jax.ShapeDtypeStruct((nbatch, 2, nh, HD_B, HD_B), F32),
            jax.ShapeDtypeStruct((nbatch, 2, nh, 1, HD_B), F32),
            jax.ShapeDtypeStruct((nbatch, 2, nh, 1, HD_B), F32),
        ]
    return pl.pallas_call(
        functools.partial(_mlstm_kernel, seq, has_ctx, group),
        grid=(nbatch // group,),
        in_specs=in_specs,
        out_specs=out_specs,
        out_shape=out_shape,
        scratch_shapes=[pltpu.VMEM((group * seq, W_B), F32)] * 4 + [
            pltpu.VMEM((group * 2 * nh, HD_B, HD_B), F32),
            pltpu.VMEM((group, nc, 2 * nh, 2 * CHUNK), F32),
            pltpu.VMEM((group, nc, CHUNK, 5 * 2 * nh), F32),
            pltpu.VMEM((group, nc, 2 * nh, 1), F32),
        ],
        compiler_params=_cparams(1),
    )(*args)


def _dft_mats(L):
    f = np.arange(L)[:, None]
    j = np.arange(L)[None, :]
    ang = 2.0 * np.pi * ((f * j) % (2 * L)) / (2 * L)
    cm = np.cos(ang)
    sm = np.sin(ang)
    alt = (1.0 - 2.0 * (np.arange(L) % 2))
    fwd_b = -sm
    fwd_b[0, :] = alt
    fwd = np.concatenate([cm, fwd_b], axis=0)
    wgt = np.where(np.arange(L) == 0, 1.0, 2.0)[None, :]
    inv_a = cm.T * wgt
    inv_b = -2.0 * sm.T
    inv_b[:, 0] = alt
    inv = np.concatenate([inv_a, inv_b], axis=1) / (2 * L)
    return jnp.asarray(fwd.astype(np.float32)), jnp.asarray(inv.astype(np.float32))


def _hyena_feats(L):
    t = np.linspace(0.0, 1.0, L, dtype=np.float32)
    wpos = (2.0 * math.pi * np.arange(L, dtype=np.float32) / L).astype(np.float32)
    fb = np.linspace(1e-4, HY_BANDS - 1, HY_BANDS, dtype=np.float32)
    z = (wpos[:, None] * fb).astype(np.float32)
    feats = np.concatenate([t[:, None], np.cos(z), -np.sin(z)], axis=-1).astype(np.float32)
    deltas = np.abs(np.linspace(math.log(HY_TARGET) / HY_SLOW_PCT, math.log(HY_TARGET) / HY_FAST_PCT,
                                D_MODEL, dtype=np.float32))
    decay = np.exp(-t[:, None] * deltas).astype(np.float32)
    return jnp.asarray(feats), jnp.asarray(decay)


def _filter_kernel(L, feats_ref, w1_ref, b1_ref, fr1_ref, w2_ref, b2_ref, fr2_ref, w3f_ref, w3b_ref,
                   decay_ref, fwd_ref, o_ref, hdn_ref):
    @pl.when((pl.program_id(0) == 0) & (pl.program_id(1) == 0))
    def _():
        h1 = jnp.sin(fr1_ref[...] * (jnp.dot(feats_ref[...], w1_ref[...], precision=HIGHEST,
                                             preferred_element_type=F32) + b1_ref[...]))
        hdn_ref[...] = jnp.sin(fr2_ref[...] * (jnp.dot(h1, w2_ref[...], precision=HIGHEST,
                                                       preferred_element_type=F32) + b2_ref[...]))

    hdn = hdn_ref[...]
    decay = decay_ref[...]
    f_fwd = jnp.dot(hdn, w3f_ref[...], precision=HIGHEST, preferred_element_type=F32) * decay
    f_bwd = jnp.dot(hdn, w3b_ref[...], precision=HIGHEST, preferred_element_type=F32) * decay
    row = lax.broadcasted_iota(jnp.int32, f_bwd.shape, 0)
    f_bwd = jnp.where(row == 0, 0.0, f_bwd)
    f_sum = f_fwd + f_bwd
    o_ref[0:L, :] = _bdot(fwd_ref[0:L, :], f_sum)
    imag = _bdot(fwd_ref[L:2 * L, :], f_fwd - f_bwd)
    nyquist = _bdot(fwd_ref[L:L + 16, :], f_sum)[0:1]
    o_ref[L:2 * L, :] = jnp.where(row == 0, nyquist, imag)


def _hyena_filter_spectrum(L, fwd_bf, w1, b1, fr1, w2, b2, fr2, w3):
    feats, decay = _hyena_feats(L)
    td = 512
    nd = D_MODEL // td
    emb = feats.shape[1]
    vec = lambda a: a.reshape(1, HY_FH)
    full = lambda shape: pl.BlockSpec(shape, lambda o, j: (0, 0))
    return pl.pallas_call(
        functools.partial(_filter_kernel, L),
        grid=(HY_ORDER, nd),
        in_specs=[
            full((L, emb)), full((emb, HY_FH)), full((1, HY_FH)), full((1, HY_FH)),
            full((HY_FH, HY_FH)), full((1, HY_FH)), full((1, HY_FH)),
            pl.BlockSpec((HY_FH, td), lambda o, j: (0, o * 2 * nd + j)),
            pl.BlockSpec((HY_FH, td), lambda o, j: (0, o * 2 * nd + nd + j)),
            pl.BlockSpec((L, td), lambda o, j: (0, j)),
            full((2 * L, L)),
        ],
        out_specs=pl.BlockSpec((2 * L, td), lambda o, j: (0, o * nd + j)),
        out_shape=jax.ShapeDtypeStruct((2 * L, HY_ORDER * D_MODEL), F32),
        scratch_shapes=[pltpu.VMEM((L, HY_FH), F32)],
        compiler_params=_cparams(2),
    )(feats, w1, vec(b1), vec(fr1), w2, vec(b2), vec(fr2), w3, w3, decay, fwd_bf)


def _spectral_conv(u, fwd, inv, kspec, L):
    uf = jnp.dot(fwd, u.astype(BF16), preferred_element_type=F32)
    ua, ub = uf[0:L], uf[L:2 * L]
    ka, kb = kspec[0:L], kspec[L:2 * L]
    first = lax.broadcasted_iota(jnp.int32, ua.shape, 0) == 0
    ya = ua * ka - jnp.where(first, 0.0, ub * kb)
    yb = jnp.where(first, ub * kb, ua * kb + ub * ka)
    y = jnp.concatenate([ya, yb], axis=0).astype(BF16)
    return jnp.dot(inv, y, preferred_element_type=F32)


def _hyena_kernel(L, zv_ref, z1_ref, z2_ref, cwv_ref, cw1_ref, cw2_ref, cbv_ref, cb1_ref, cb2_ref,
                  fwd_ref, inv_ref, k0_ref, k1_ref, bias0_ref, bias1_ref, o_ref):
    fwd = fwd_ref[...]
    inv = inv_ref[...]
    v = _dwconv3(zv_ref[...], cwv_ref[...], cbv_ref[...])
    x1 = _dwconv3(z1_ref[...], cw1_ref[...], cb1_ref[...])
    x2 = _dwconv3(z2_ref[...], cw2_ref[...], cb2_ref[...])
    z = x1 * (_spectral_conv(v, fwd, inv, k0_ref[...], L) + v * bias0_ref[...])
    o_ref[...] = x2 * (_spectral_conv(z, fwd, inv, k1_ref[...], L) + z * bias1_ref[...])


def _hyena_core(zproj, conv_w, conv_b, fwd_bf, inv_bf, kspec, bias, *, seq, nbatch, row_off, td):
    nd = D_MODEL // td
    zblk = lambda part: pl.BlockSpec((seq, td), lambda b, j, part=part: (row_off + b, part * nd + j))
    cwblk = lambda part: pl.BlockSpec((3, td), lambda b, j, part=part: (0, part * nd + j))
    cbblk = lambda part: pl.BlockSpec((1, td), lambda b, j, part=part: (0, part * nd + j))
    return pl.pallas_call(
        functools.partial(_hyena_kernel, seq),
        grid=(nbatch, nd),
        in_specs=[
            zblk(0), zblk(1), zblk(2), cwblk(0), cwblk(1), cwblk(2), cbblk(0), cbblk(1), cbblk(2),
            pl.BlockSpec((2 * seq, seq), lambda b, j: (0, 0), pipeline_mode=pl.Buffered(1)),
            pl.BlockSpec((seq, 2 * seq), lambda b, j: (0, 0), pipeline_mode=pl.Buffered(1)),
            pl.BlockSpec((2 * seq, td), lambda b, j: (0, j)),
            pl.BlockSpec((2 * seq, td), lambda b, j: (0, nd + j)),
            pl.BlockSpec((None, 1, td), lambda b, j: (0, 0, j)),
            pl.BlockSpec((None, 1, td), lambda b, j: (1, 0, j)),
        ],
        out_specs=pl.BlockSpec((seq, td), lambda b, j: (b, j)),
        out_shape=jax.ShapeDtypeStruct((nbatch * seq, D_MODEL), F32),
        compiler_params=_cparams(2),
    )(zproj, zproj, zproj, conv_w, conv_w, conv_w, conv_b, conv_b, conv_b,
      fwd_bf, inv_bf, kspec, kspec, bias.reshape(HY_ORDER, 1, D_MODEL), bias.reshape(HY_ORDER, 1, D_MODEL))


def _route_tile(y_tile, row, ada_ref, g_ref, rw_ref, rb_ref, h_ref, idx_ref, wt_ref):
    h = _modulate(y_tile, g_ref[...], _ada_chunk(ada_ref, row, 3), _ada_chunk(ada_ref, row, 4))
    for j in range(ROW_CHUNKS):
        h_ref[pl.ds(j, ROW_TILE, stride=ROW_CHUNKS), :] = h[:, j * LANES:(j + 1) * LANES]
    logits = lax.dot_general(rw_ref[...], h, (((1,), (1,)), ((), ())), precision=HIGHEST,
                             preferred_element_type=F32) + rb_ref[...]
    expert = lax.broadcasted_iota(jnp.int32, logits.shape, 0)
    slot = lax.broadcasted_iota(jnp.int32, (TOP_K, logits.shape[1]), 0)
    vals = jnp.zeros((TOP_K, logits.shape[1]), F32)
    idxs = jnp.zeros((TOP_K, logits.shape[1]), jnp.int32)
    cur = logits
    for k in range(TOP_K):
        m = jnp.max(cur, axis=0, keepdims=True)
        a = jnp.min(jnp.where(cur == m, expert, N_EXPERTS), axis=0, keepdims=True)
        vals = jnp.where(slot == k, m, vals)
        idxs = jnp.where(slot == k, a, idxs)
        cur = jnp.where(expert == a, -jnp.inf, cur)
    e = jnp.exp(vals - vals[0:1])
    wt_ref[...] = e / jnp.sum(e, axis=0, keepdims=True)
    idx_ref[...] = idxs


def _dispatch_kernel(rows_ref, h_ref, o_ref):
    base = pl.program_id(0) * GATHER_ROWS
    for r in range(GATHER_ROWS):
        t = pl.multiple_of(rows_ref[base + r] * ROW_CHUNKS, ROW_CHUNKS)
        o_ref[r * ROW_CHUNKS:(r + 1) * ROW_CHUNKS, :] = h_ref[pl.ds(t, ROW_CHUNKS), :]


def _dispatch(h_tiles, gather_row):
    grid_spec = pltpu.PrefetchScalarGridSpec(
        num_scalar_prefetch=1,
        grid=(X_ROWS // GATHER_ROWS,),
        in_specs=[pl.BlockSpec((T_ALL * ROW_CHUNKS, LANES), lambda i, rows: (0, 0), pipeline_mode=pl.Buffered(1))],
        out_specs=pl.BlockSpec((GATHER_ROWS * ROW_CHUNKS, LANES), lambda i, rows: (i, 0)),
    )
    return pl.pallas_call(
        _dispatch_kernel,
        grid_spec=grid_spec,
        out_shape=jax.ShapeDtypeStruct((X_ROWS * ROW_CHUNKS, LANES), F32),
        compiler_params=_cparams(1),
    )(gather_row, h_tiles)


def _deinterleave_matrix():
    s = np.zeros((256, 256), np.float32)
    j = np.arange(128)
    s[2 * j, j] = 1.0
    s[2 * j + 1, 128 + j] = 1.0
    return jnp.asarray(s)


def _weight_copies(layer, e, w1_hbm, w2_hbm, w1s_ref, w2s_ref, sem):
    copies = []
    r1 = D_MODEL // W1_DMA_CHUNKS
    for c in range(W1_DMA_CHUNKS):
        copies.append(pltpu.make_async_copy(w1_hbm.at[layer, e, pl.ds(c * r1, r1)],
                                            w1s_ref.at[pl.ds(c * r1, r1)], sem.at[c]))
    r2 = D_FF // W2_DMA_CHUNKS
    for c in range(W2_DMA_CHUNKS):
        copies.append(pltpu.make_async_copy(w2_hbm.at[layer, e, pl.ds(c * r2, r2)],
                                            w2s_ref.at[pl.ds(c * r2, r2)], sem.at[W1_DMA_CHUNKS + c]))
    return copies


def _expert_kernel(layer, te_ref, tf_ref, ne_ref, nu_ref, src_ref, xo_ref, vr_ref, x_ref, b1_ref, b2_ref, wt_ref, s_ref,
                   w1_hbm, w2_hbm, o_hbm, w1s_ref, w2s_ref, w1p_ref, w2p_ref, acc_ref, out_ref, wsem, osem):
    i = pl.program_id(0)
    half = 128
    copies = functools.partial(_weight_copies, layer, w1_hbm=w1_hbm, w2_hbm=w2_hbm,
                               w1s_ref=w1s_ref, w2s_ref=w2s_ref, sem=wsem)

    @pl.when(i == 0)
    def _():
        acc_ref[...] = jnp.zeros_like(acc_ref)
        out_ref[...] = jnp.zeros_like(out_ref)
        for cp in copies(te_ref[0]):
            cp.start()

    @pl.when(tf_ref[i] == 1)
    def _():
        for cp in copies(te_ref[i]):
            cp.wait()
        s = s_ref[...].astype(BF16)
        for c in range(2 * D_FF // 256):
            blk = jnp.dot(w1s_ref[:, c * 256:(c + 1) * 256].astype(BF16), s, preferred_element_type=F32)
            w1p_ref[:, c * half:(c + 1) * half] = blk[:, :half].astype(BF16)
            w1p_ref[:, D_FF + c * half:D_FF + (c + 1) * half] = blk[:, half:].astype(BF16)
        w2p_ref[...] = w2s_ref[...].astype(BF16)

        @pl.when(ne_ref[i] >= 0)
        def _():
            for cp in copies(ne_ref[i]):
                cp.start()

    def tile_step(m):
        base = i * MOE_TILE
        prev = (i + 1) % 2
        for r0 in range(0, MOE_TILE, SCATTER_GROUP):
            toks = [pl.multiple_of(src_ref[base + r0 + g] * ROW_CHUNKS, ROW_CHUNKS) for g in range(SCATTER_GROUP)]
            cur = [acc_ref[pl.ds(toks[g], ROW_CHUNKS), :] for g in range(SCATTER_GROUP)]
            add = [out_ref[prev, (r0 + g) * ROW_CHUNKS:(r0 + g + 1) * ROW_CHUNKS, :] for g in range(SCATTER_GROUP)]
            for g in range(SCATTER_GROUP):
                acc_ref[pl.ds(toks[g], ROW_CHUNKS), :] = cur[g] + add[g]
        a = jnp.dot(_token_rows(x_ref, m).astype(BF16), w1p_ref[...], preferred_element_type=F32) + b1_ref[...]
        glu = jnp.minimum(a[:, :D_FF], SWIGLU_LIMIT)
        lin = jnp.clip(a[:, D_FF:], -SWIGLU_LIMIT, SWIGLU_LIMIT)
        hid = glu * _sigmoid(SWIGLU_ALPHA * glu) * (lin + 1.0)
        out = (jnp.dot(hid.astype(BF16), w2p_ref[...], preferred_element_type=F32) + b2_ref[...]) * wt_ref[0:m, :]
        cur_buf = i % 2
        for j in range(ROW_CHUNKS):
            out_ref[cur_buf, pl.ds(j, m, stride=ROW_CHUNKS), :] = out[:, j * LANES:(j + 1) * LANES]

    live = i <= nu_ref[0]
    pl.when(live & (vr_ref[i] > MOE_TILE // 2))(functools.partial(tile_step, MOE_TILE))
    pl.when(live & (vr_ref[i] <= MOE_TILE // 2))(functools.partial(tile_step, MOE_TILE // 2))

    @pl.when(i == pl.num_programs(0) - 1)
    def _():
        cp = pltpu.make_async_copy(acc_ref.at[pl.ds(0, T_ALL * ROW_CHUNKS)], o_hbm, osem)
        cp.start()
        cp.wait()


def _experts(layer, x_sorted, w_sorted, plan, w1, b1p, w2, b2):
    tile_expert, tile_first, next_expert, n_used, src, x_off, valid_rows = plan
    grid_spec = pltpu.PrefetchScalarGridSpec(
        num_scalar_prefetch=7,
        grid=(MOE_TILES,),
        in_specs=[
            pl.BlockSpec((pl.Element(MOE_TILE * ROW_CHUNKS), pl.Element(LANES)),
                         lambda i, te, tf, ne, nu, src, xo, vr: (pl.multiple_of(xo[i] * ROW_CHUNKS, X_ALIGN * ROW_CHUNKS), 0)),
            pl.BlockSpec((None, None, 1, 2 * D_FF), lambda i, te, *_: (layer, te[i], 0, 0)),
            pl.BlockSpec((None, None, 1, D_MODEL), lambda i, te, *_: (layer, te[i], 0, 0)),
            pl.BlockSpec((MOE_TILE, 1), lambda i, te, *_: (i, 0)),
            pl.BlockSpec((256, 256), lambda i, te, *_: (0, 0)),
            pl.BlockSpec(memory_space=pl.ANY),
            pl.BlockSpec(memory_space=pl.ANY),
        ],
        out_specs=pl.BlockSpec(memory_space=pl.ANY),
        scratch_shapes=[
            pltpu.VMEM((D_MODEL, 2 * D_FF), F32),
            pltpu.VMEM((D_FF, D_MODEL), F32),
            pltpu.VMEM((D_MODEL, 2 * D_FF), BF16),
            pltpu.VMEM((D_FF, D_MODEL), BF16),
            pltpu.VMEM((ACC_ROWS * ROW_CHUNKS, LANES), F32),
            pltpu.VMEM((2, MOE_TILE * ROW_CHUNKS, LANES), F32),
            pltpu.SemaphoreType.DMA((W1_DMA_CHUNKS + W2_DMA_CHUNKS,)),
            pltpu.SemaphoreType.DMA(()),
        ],
    )
    return pl.pallas_call(
        functools.partial(_expert_kernel, layer),
        grid_spec=grid_spec,
        out_shape=jax.ShapeDtypeStruct((T_ALL * ROW_CHUNKS, LANES), F32),
        compiler_params=_cparams(1),
    )(tile_expert, tile_first, next_expert, n_used, src, x_off, valid_rows, x_sorted, b1p, b2, w_sorted,
      _deinterleave_matrix(), w1, w2)


def _combine_kernel(first_tile, y_ref, a_ref, ada_ref, o_ref):
    gate = _ada_chunk(ada_ref, _cond_row(first_tile + pl.program_id(0)), 5)
    o_ref[...] = y_ref[...] + gate * _token_rows(a_ref, ROW_TILE)


def _combine(y, acc, ada_l, first_tile=0, n_tiles=N_ROW_TILES):
    return pl.pallas_call(
        functools.partial(_combine_kernel, first_tile),
        grid=(n_tiles,),
        in_specs=[
            pl.BlockSpec((ROW_TILE, D_MODEL), lambda i: (first_tile + i, 0)),
            pl.BlockSpec((ROW_TILE * ROW_CHUNKS, LANES), lambda i: (first_tile + i, 0)),
            pl.BlockSpec((COND_ROWS, ADA_CHUNKS * D_MODEL), lambda i: (0, 0)),
        ],
        out_specs=pl.BlockSpec((ROW_TILE, D_MODEL), lambda i: (i, 0)),
        out_shape=jax.ShapeDtypeStruct((n_tiles * ROW_TILE, D_MODEL), F32),
        compiler_params=_cparams(1),
    )(y, acc, ada_l)


def _routing_plan(idx, wts):
    eid = idx.reshape(-1)
    order = jnp.argsort(eid, stable=True).astype(jnp.int32)
    experts = jnp.arange(N_EXPERTS, dtype=jnp.int32)
    counts = jnp.sum(eid[:, None] == experts[None, :], axis=0).astype(jnp.int32)
    ntiles = (counts + MOE_TILE - 1) // MOE_TILE
    tile_end = jnp.cumsum(ntiles).astype(jnp.int32)
    tile_begin = tile_end - ntiles
    cstarts = (jnp.cumsum(counts) - counts).astype(jnp.int32)
    n_used = tile_end[-1]
    tile = jnp.arange(MOE_TILES, dtype=jnp.int32)
    te = jnp.minimum(jnp.sum(tile[:, None] >= tile_end[None, :], axis=1), N_EXPERTS - 1).astype(jnp.int32)
    used = tile < n_used
    prev = jnp.concatenate([jnp.full((1,), -1, jnp.int32), te[:-1]])
    first = (te != prev) & used

    def pick(onehot, table):
        return jnp.sum(jnp.where(onehot, table[None, :], 0), axis=1).astype(jnp.int32)

    tile_is = te[:, None] == experts[None, :]
    later = (experts[None, :] > experts[:, None]) & (ntiles[None, :] > 0)
    following = jnp.min(jnp.where(later, experts[None, :], N_EXPERTS), axis=1)
    following = jnp.where(following < N_EXPERTS, following, -1)
    next_expert = jnp.where(first, pick(tile_is, following), -1).astype(jnp.int32)
    tile_in_expert = tile - pick(tile_is, tile_begin)
    off = tile_in_expert[:, None] * MOE_TILE + jnp.arange(MOE_TILE, dtype=jnp.int32)[None, :]
    valid_rows = jnp.where(used, jnp.clip(pick(tile_is, counts) - tile_in_expert * MOE_TILE, 0, MOE_TILE), 0)
    valid = jnp.arange(MOE_TILE, dtype=jnp.int32)[None, :] < valid_rows[:, None]
    assign = order[jnp.clip(pick(tile_is, cstarts)[:, None] + off, 0, N_ASSIGN - 1)]
    token = assign // TOP_K
    src = jnp.where(valid, token, SPARE_ROW).reshape(MOE_ROWS).astype(jnp.int32)
    src = jnp.concatenate([jnp.full((MOE_TILE,), SPARE_ROW, jnp.int32), src])
    w_sorted = jnp.where(valid, wts.reshape(-1)[assign], 0.0).reshape(MOE_ROWS, 1)
    seg = ((counts + X_ALIGN - 1) // X_ALIGN) * X_ALIGN
    seg_end = jnp.cumsum(seg).astype(jnp.int32)
    seg_begin = seg_end - seg
    x_off = jnp.where(used, pick(tile_is, seg_begin) + tile_in_expert * MOE_TILE, 0).astype(jnp.int32)
    group = jnp.arange(X_ROWS // X_ALIGN, dtype=jnp.int32) * X_ALIGN
    group_is = (group[:, None] >= seg_begin[None, :]) & (group[:, None] < seg_end[None, :])
    xoffset = (group - pick(group_is, seg_begin))[:, None] + jnp.arange(X_ALIGN, dtype=jnp.int32)[None, :]
    xassign = order[jnp.clip(pick(group_is, cstarts)[:, None] + xoffset, 0, N_ASSIGN - 1)]
    gather_row = jnp.where(xoffset < pick(group_is, counts)[:, None], xassign // TOP_K, 0).reshape(X_ROWS)
    plan = (te, first.astype(jnp.int32), next_expert, n_used.reshape(1), src, x_off, valid_rows.astype(jnp.int32))
    return plan, gather_row, w_sorted


def _moe(layer, y, routed, ada_l, w1, b1p, w2, b2):
    h, idx_t, wts_t = routed
    plan, gather_row, w_sorted = _routing_plan(idx_t.T, wts_t.T)
    x_sorted = _dispatch(h, gather_row)
    acc = _experts(layer, x_sorted, w_sorted, plan, w1, b1p, w2, b2)
    if layer == DEPTH - 1:
        return (_combine(y, acc, ada_l, 0, P_TILES), _combine(y, acc, ada_l, P_TILES, N_ROW_TILES - P_TILES)), None
    return y, (acc, ada_l)


def kernel(x_prompt, x_sample, cache_attn_k, cache_attn_v, state_mlstm_C, state_mlstm_n, state_mlstm_m, c, c_ctx, ada_w, ada_b, norm_mix_g, norm_ffn_g, ab_w_in, ab_w_out, da_qnorm_g, da_knorm_g, da_lambda, da_subnorm_g, ml_conv_w, ml_conv_b, ml_gate_b, ml_headnorm_g, hy_w_in, hy_w_out, hy_conv_w, hy_conv_b, hy_f_w1, hy_f_b1, hy_f_freq1, hy_f_w2, hy_f_b2, hy_f_freq2, hy_f_w3, hy_bias, router_w, router_b, moe_w1, moe_b1, moe_w2, moe_b2):
    y = (x_prompt.reshape(T_P, D_MODEL), x_sample.reshape(T_S, D_MODEL))
    cond = jnp.concatenate([c_ctx[None, :], c, jnp.zeros((COND_ROWS - 1 - DEC_BATCH, D_MODEL), F32)], axis=0)
    ada = _ada_table(cond, ada_w, ada_b)
    b1p = moe_b1.reshape(DEPTH, N_EXPERTS, D_FF, 2).swapaxes(2, 3).reshape(DEPTH, N_EXPERTS, 1, 2 * D_FF)
    b2r = moe_b2.reshape(DEPTH, N_EXPERTS, 1, D_MODEL)
    new_k, new_v, new_c, new_n, new_m = [], [], [], [], []
    pending = None
    for layer in range(DEPTH):
        ada_l = ada[layer]
        route_prm = (norm_ffn_g[layer], router_w[layer], router_b[layer])
        if layer % 2 == 0:
            e = layer // 2
            lam_init = 0.8 - 0.6 * math.exp(-0.3 * layer)
            proj = _modulated_proj(y, ada_l, norm_mix_g[layer], ab_w_in[e],
                                   (3 * W_A, 2 * W_B, W_B, W_B, 4 * H_B), pending)
            if pending is not None:
                y = proj[-1]
            qkv, mqk, mv, mo, mg = proj[:5]
            qg2 = jnp.tile(da_qnorm_g[e], 2).reshape(1, 2 * HD_A)
            kg2 = jnp.tile(da_knorm_g[e], 2).reshape(1, 2 * HD_A)
            sub_g = da_subnorm_g[e].reshape(1, 2 * HD_A)
            oa_p, k_norm, v_heads = _attention_prompt(qkv, qg2, kg2, da_lambda[e], sub_g, lam_init)
            cos, sin = _rope_tables()
            oa_s = _attention_sample(
                qkv, cache_attn_k[:, e].reshape(DEC_BATCH, PAST_LEN, W_A),
                cache_attn_v[:, e].reshape(DEC_BATCH, PAST_LEN, W_A), cos, sin,
                qg2, kg2, da_lambda[e], sub_g, lam_init)
            ob_p, c_new, n_new, m_new = _mlstm(
                mqk, mv, mo, mg[:T_P], ml_conv_w[e], ml_conv_b[e].reshape(1, 2 * W_B), ml_gate_b[e],
                ml_headnorm_g[e], seq=SEQ, nbatch=BATCH, row_off=0, group=MLSTM_GROUP)
            ob_s = _mlstm(
                mqk, mv, mo, mg[T_P:], ml_conv_w[e], ml_conv_b[e].reshape(1, 2 * W_B), ml_gate_b[e],
                ml_headnorm_g[e], seq=DEC_SEQ, nbatch=DEC_BATCH, row_off=T_P // DEC_SEQ, group=1,
                ctx=(state_mlstm_C[:, e], state_mlstm_n[:, e], state_mlstm_m[:, e]))
            y, routed = _out_proj_residual([(oa_p, oa_s), (ob_p, ob_s)], y, ada_l, ab_w_out[e], 2, *route_prm)
            new_k.append(k_norm.reshape(BATCH, SEQ, H_A, 2, HD_A))
            new_v.append(v_heads.reshape(BATCH, SEQ, H_A, 2 * HD_A))
            new_c.append(c_new)
            new_n.append(n_new.reshape(BATCH, 2, H_B, HD_B))
            new_m.append(m_new[..., 0, 0])
        else:
            o = layer // 2
            proj = _modulated_proj(y, ada_l, norm_mix_g[layer], hy_w_in[o], (HY_PROJ,), pending)
            if pending is not None:
                y = proj[-1]
            zproj = proj[0]
            cores = []
            for seq, nbatch, row_off, td in ((SEQ, BATCH, 0, 512), (DEC_SEQ, DEC_BATCH, T_P // DEC_SEQ, 256)):
                fwd, inv = _dft_mats(seq)
                fwd_bf, inv_bf = fwd.astype(BF16), inv.astype(BF16)
                kspec = _hyena_filter_spectrum(seq, fwd_bf, hy_f_w1[o], hy_f_b1[o], hy_f_freq1[o], hy_f_w2[o],
                                               hy_f_b2[o], hy_f_freq2[o], hy_f_w3[o])
                cores.append(_hyena_core(zproj, hy_conv_w[o], hy_conv_b[o].reshape(1, HY_PROJ), fwd_bf, inv_bf,
                                         kspec, hy_bias[o], seq=seq, nbatch=nbatch, row_off=row_off, td=td))
            y, routed = _out_proj_residual([tuple(cores)], y, ada_l, hy_w_out[o], 2, *route_prm)
        y, pending = _moe(layer, y, routed, ada_l, moe_w1, b1p, moe_w2, b2r)
    y_p = y[0].reshape(BATCH, SEQ, D_MODEL)
    y_s = y[1].reshape(DEC_BATCH, DEC_SEQ, D_MODEL)
    return (y_p, y_s, jnp.stack(new_k, axis=1), jnp.stack(new_v, axis=1), jnp.stack(new_c, axis=1),
            jnp.stack(new_n, axis=1), jnp.stack(new_m, axis=1))
```

```python
import functools
import math

import numpy as np
import jax
import jax.numpy as jnp
from jax import lax
from jax.experimental import pallas as pl
from jax.experimental.pallas import tpu as pltpu

D_MODEL = 1024
BATCH = 16
SEQ = 256
DEPTH = 2
DEC_BATCH = 2
DEC_SEQ = 1024
PAST_LEN = 256
GRID_W = 64
W_A = D_MODEL // 2
HD_A = 64
H_A = W_A // (2 * HD_A)
W_B = D_MODEL - W_A
HD_B = 128
H_B = W_B // HD_B
AB_PROJ = 3 * W_A + 4 * W_B + 4 * H_B
ROPE_BASE = 10000.0
CHUNK = 64
HY_ORDER = 2
HY_PROJ = (HY_ORDER + 1) * D_MODEL
HY_BANDS = 8
HY_FH = 64
HY_TARGET = 1e-2
HY_FAST_PCT = 0.3
HY_SLOW_PCT = 1.5
N_EXPERTS = 32
TOP_K = 4
D_FF = D_MODEL
SWIGLU_ALPHA = 1.702
SWIGLU_LIMIT = 7.0
ADA_CHUNKS = 6
EPS = 1e-6
NEG = -1e30
F32 = jnp.float32
BF16 = jnp.bfloat16

T_P = BATCH * SEQ
T_S = DEC_BATCH * DEC_SEQ
T_ALL = T_P + T_S
ROW_TILE = 512
ATTN_Q_TILE = 256
N_ROW_TILES = T_ALL // ROW_TILE
P_TILES = T_P // ROW_TILE
S_TILES_PER_BATCH = DEC_SEQ // ROW_TILE
COND_ROWS = 8
MOE_TILE = 256
N_ASSIGN = T_ALL * TOP_K
MOE_ROWS = N_ASSIGN + N_EXPERTS * MOE_TILE
MOE_TILES = MOE_ROWS // MOE_TILE
X_ALIGN = 16
MLSTM_GROUP = 2
GATHER_ROWS = 1024
X_ROWS = -(-(N_ASSIGN + N_EXPERTS * X_ALIGN + MOE_TILE) // GATHER_ROWS) * GATHER_ROWS
SPARE_ROW = T_ALL
ACC_ROWS = T_ALL + 8
SCATTER_GROUP = 8
W1_DMA_CHUNKS = 8
W2_DMA_CHUNKS = 4
LANES = 128
ROW_CHUNKS = D_MODEL // LANES
VMEM_LIMIT = 56 * 1024 * 1024
HIGHEST = lax.Precision.HIGHEST


def _cparams(n_axes):
    return pltpu.CompilerParams(dimension_semantics=("arbitrary",) * n_axes,
                                vmem_limit_bytes=VMEM_LIMIT)


def _bdot(a, b):
    return jnp.dot(a.astype(BF16), b.astype(BF16), preferred_element_type=F32)


def _cond_row(i):
    return jnp.where(i < P_TILES, 0, 1 + (i - P_TILES) // S_TILES_PER_BATCH)


def _ada_chunk(ada_ref, row, j):
    return ada_ref[pl.ds(row, 1), j * D_MODEL:(j + 1) * D_MODEL]


def _modulate(x, g, shift, scale):
    ms = jnp.mean(x * x, axis=-1, keepdims=True)
    return (x * lax.rsqrt(ms + EPS) * g) * (1.0 + scale) + shift


def _sigmoid(x):
    return 1.0 / (1.0 + jnp.exp(-x))


def _silu(x):
    return x * _sigmoid(x)


def _log_sigmoid(x):
    return jnp.minimum(x, 0.0) - jnp.log(1.0 + jnp.exp(-jnp.abs(x)))


def _dwconv3(x, w, b, seq=None):
    n = x.shape[0]
    seq = n if seq is None else seq
    pos = lax.broadcasted_iota(jnp.int32, x.shape, 0) % seq
    prev = jnp.where(pos == 0, 0.0, pltpu.roll(x, 1, 0))
    nxt = jnp.where(pos == seq - 1, 0.0, pltpu.roll(x, n - 1, 0))
    return prev * w[0:1] + x * w[1:2] + nxt * w[2:3] + b


def _ada_kernel(cond_ref, w_ref, b_ref, o_ref):
    c = _silu(cond_ref[...])
    o_ref[...] = _bdot(c, w_ref[...]) + b_ref[...]


def _ada_table(cond, ada_w, ada_b):
    tn = 1536
    return pl.pallas_call(
        _ada_kernel,
        grid=(DEPTH, ADA_CHUNKS * D_MODEL // tn),
        in_specs=[
            pl.BlockSpec((COND_ROWS, D_MODEL), lambda l, j: (0, 0)),
            pl.BlockSpec((None, D_MODEL, tn), lambda l, j: (l, 0, j)),
            pl.BlockSpec((None, 1, tn), lambda l, j: (l, 0, j)),
        ],
        out_specs=pl.BlockSpec((None, COND_ROWS, tn), lambda l, j: (l, 0, j)),
        out_shape=jax.ShapeDtypeStruct((DEPTH, COND_ROWS, ADA_CHUNKS * D_MODEL), F32),
        compiler_params=_cparams(2),
    )(cond, ada_w, ada_b.reshape(DEPTH, 1, ADA_CHUNKS * D_MODEL))


def _stream_specs(y):
    if isinstance(y, tuple):
        return [pl.BlockSpec((ROW_TILE, D_MODEL), lambda i: (jnp.minimum(i, P_TILES - 1), 0)),
                pl.BlockSpec((ROW_TILE, D_MODEL), lambda i: (jnp.maximum(i - P_TILES, 0), 0))], list(y)
    return [pl.BlockSpec((ROW_TILE, D_MODEL), lambda i: (i, 0))], [y]


def _token_rows(ref, n):
    return jnp.concatenate([ref[pl.ds(j, n, stride=ROW_CHUNKS), :] for j in range(ROW_CHUNKS)], axis=1)


def _stream_tile(y_refs, i):
    if len(y_refs) == 2:
        return jnp.where(i < P_TILES, y_refs[0][...], y_refs[1][...])
    return y_refs[0][...]


def _proj_kernel(splits, n_y, has_pending, *refs):
    y_refs = refs[:n_y]
    refs = refs[n_y:]
    if has_pending:
        acc_ref, ada_prev_ref = refs[:2]
        refs = refs[2:]
    ada_ref, g_ref, w_ref = refs[:3]
    out_refs, wbf_ref = refs[3:-1], refs[-1]
    i = pl.program_id(0)

    @pl.when(i == 0)
    def _():
        wbf_ref[...] = w_ref[...].astype(BF16)

    row = _cond_row(i)
    y = _stream_tile(y_refs, i)
    if has_pending:
        y = y + _ada_chunk(ada_prev_ref, row, 5) * _token_rows(acc_ref, ROW_TILE)
        out_refs[-1][...] = y
        out_refs = out_refs[:-1]
    h = _modulate(y, g_ref[...], _ada_chunk(ada_ref, row, 0), _ada_chunk(ada_ref, row, 1))
    h = h.astype(BF16)
    lo = 0
    for o_ref, width in zip(out_refs, splits):
        o_ref[...] = jnp.dot(h, wbf_ref[:, lo:lo + width], preferred_element_type=F32)
        lo += width


def _modulated_proj(y, ada_l, g, w, splits, pending=None):
    n = w.shape[1]
    y_specs, y_args = _stream_specs(y)
    widths = tuple(splits)
    if pending is not None:
        y_specs = y_specs + [pl.BlockSpec((ROW_TILE * ROW_CHUNKS, LANES), lambda i: (i, 0)),
                             pl.BlockSpec((COND_ROWS, ADA_CHUNKS * D_MODEL), lambda i: (0, 0))]
        y_args = y_args + list(pending)
        widths = widths + (D_MODEL,)
    return pl.pallas_call(
        functools.partial(_proj_kernel, splits, len(y_args) - (2 if pending is not None else 0), pending is not None),
        grid=(N_ROW_TILES,),
        in_specs=y_specs + [
            pl.BlockSpec((COND_ROWS, ADA_CHUNKS * D_MODEL), lambda i: (0, 0)),
            pl.BlockSpec((1, D_MODEL), lambda i: (0, 0)),
            pl.BlockSpec((D_MODEL, n), lambda i: (0, 0), pipeline_mode=pl.Buffered(1)),
        ],
        out_specs=[pl.BlockSpec((ROW_TILE, s), lambda i: (i, 0)) for s in widths],
        out_shape=[jax.ShapeDtypeStruct((T_ALL, s), F32) for s in widths],
        scratch_shapes=[pltpu.VMEM((D_MODEL, n), BF16)],
        compiler_params=_cparams(1),
    )(*y_args, ada_l, g.reshape(1, D_MODEL), w)


def _out_proj_kernel(n_in, n_y, gate_chunk, *refs):
    x_refs = refs[:2 * n_in]
    y_refs = refs[2 * n_in:2 * n_in + n_y]
    ada_ref, w_ref, g_ref, rw_ref, rb_ref, o_ref, h_ref, idx_ref, wt_ref, wbf_ref = refs[2 * n_in + n_y:]
    i = pl.program_id(0)

    @pl.when(i == 0)
    def _():
        wbf_ref[...] = w_ref[...].astype(BF16)

    acc = None
    lo = 0
    for xp_ref, xs_ref in zip(x_refs[0::2], x_refs[1::2]):
        k = xp_ref.shape[1]
        x = jnp.where(i < P_TILES, xp_ref[...], xs_ref[...])
        part = jnp.dot(x.astype(BF16), wbf_ref[lo:lo + k, :], preferred_element_type=F32)
        acc = part if acc is None else acc + part
        lo += k
    row = _cond_row(i)
    y_new = _stream_tile(y_refs, i) + _ada_chunk(ada_ref, row, gate_chunk) * acc
    o_ref[...] = y_new
    _route_tile(y_new, row, ada_ref, g_ref, rw_ref, rb_ref, h_ref, idx_ref, wt_ref)


def _out_proj_residual(xs, y, ada_l, w, gate_chunk, ffn_g, router_w, router_b):
    y_specs, y_args = _stream_specs(y)
    x_specs = []
    for xp, _ in xs:
        x_specs.append(pl.BlockSpec((ROW_TILE, xp.shape[1]), lambda i: (jnp.minimum(i, P_TILES - 1), 0)))
        x_specs.append(pl.BlockSpec((ROW_TILE, xp.shape[1]), lambda i: (jnp.maximum(i - P_TILES, 0), 0)))
    y_new, h, idx_t, wts_t = pl.pallas_call(
        functools.partial(_out_proj_kernel, len(xs), len(y_args), gate_chunk),
        grid=(N_ROW_TILES,),
        in_specs=x_specs + y_specs + [
            pl.BlockSpec((COND_ROWS, ADA_CHUNKS * D_MODEL), lambda i: (0, 0)),
            pl.BlockSpec((D_MODEL, D_MODEL), lambda i: (0, 0), pipeline_mode=pl.Buffered(1)),
            pl.BlockSpec((1, D_MODEL), lambda i: (0, 0)),
            pl.BlockSpec((N_EXPERTS, D_MODEL), lambda i: (0, 0)),
            pl.BlockSpec((N_EXPERTS, 1), lambda i: (0, 0)),
        ],
        out_specs=[
            pl.BlockSpec((ROW_TILE, D_MODEL), lambda i: (i, 0)),
            pl.BlockSpec((ROW_TILE * ROW_CHUNKS, LANES), lambda i: (i, 0)),
            pl.BlockSpec((TOP_K, ROW_TILE), lambda i: (0, i)),
            pl.BlockSpec((TOP_K, ROW_TILE), lambda i: (0, i)),
        ],
        out_shape=[
            jax.ShapeDtypeStruct((T_ALL, D_MODEL), F32),
            jax.ShapeDtypeStruct((T_ALL * ROW_CHUNKS, LANES), F32),
            jax.ShapeDtypeStruct((TOP_K, T_ALL), jnp.int32),
            jax.ShapeDtypeStruct((TOP_K, T_ALL), F32),
        ],
        scratch_shapes=[pltpu.VMEM((D_MODEL, D_MODEL), BF16)],
        compiler_params=_cparams(1),
    )(*[a for pair in xs for a in pair], *y_args, ada_l, w,
      ffn_g.reshape(1, D_MODEL), router_w.T, router_b.reshape(N_EXPERTS, 1))
    return y_new, (h, idx_t, wts_t)


def _subhead_norm(x, g2):
    lane = lax.broadcasted_iota(jnp.int32, x.shape, 1)
    first = lane < HD_A
    xx = x * x
    s0 = jnp.sum(jnp.where(first, xx, 0.0), axis=-1, keepdims=True)
    s1 = jnp.sum(jnp.where(first, 0.0, xx), axis=-1, keepdims=True)
    r = jnp.where(first, lax.rsqrt(s0 / HD_A + EPS), lax.rsqrt(s1 / HD_A + EPS))
    return x * r * g2


def _rope(x, cos, sin):
    quarter = HD_A // 4
    lane = lax.broadcasted_iota(jnp.int32, x.shape, 1)
    lower = (lane % (2 * quarter)) < quarter
    swapped = jnp.where(lower, pltpu.roll(x, 2 * HD_A - quarter, 1), pltpu.roll(x, quarter, 1))
    return x * cos + swapped * sin


def _attn_kernel(lam_init, has_ctx, *refs):
    if has_ctx:
        (q_ref, k_ref, v_ref, ck_ref, cv_ref, cq_ref, sq_ref, ckk_ref, skk_ref,
         qg_ref, kg_ref, lp_ref, sg_ref, o_ref, kall_ref, vall_ref) = refs
    else:
        q_ref, k_ref, v_ref, qg_ref, kg_ref, lp_ref, sg_ref, o_ref, kn_ref, vh_ref = refs
    lp = lp_ref[...]
    lam = (jnp.exp(jnp.sum(lp[0:1] * lp[1:2], axis=-1, keepdims=True))
           - jnp.exp(jnp.sum(lp[2:3] * lp[3:4], axis=-1, keepdims=True)) + lam_init)

    def attend(q, k, v):
        probs = []
        for c in range(2):
            qc = q[:, c * HD_A:(c + 1) * HD_A].astype(BF16)
            kc = k[:, c * HD_A:(c + 1) * HD_A].astype(BF16)
            s = lax.dot_general(qc, kc, (((1,), (1,)), ((), ())), preferred_element_type=F32) * (HD_A ** -0.5)
            e = jnp.exp(s - jnp.max(s, axis=-1, keepdims=True))
            probs.append(e / jnp.sum(e, axis=-1, keepdims=True))
        o = _bdot(probs[0] - lam * probs[1], v)
        ms = jnp.mean(o * o, axis=-1, keepdims=True)
        return (o * lax.rsqrt(ms + EPS) * sg_ref[...]) * (1.0 - lam_init)

    if not has_ctx:
        for h in range(H_A):
            cols = slice(h * 2 * HD_A, (h + 1) * 2 * HD_A)
            k = _subhead_norm(k_ref[:, cols], kg_ref[...])
            for c in range(2):
                kn_ref[pl.ds(2 * h + c, SEQ, stride=2 * H_A), :] = k[:, c * HD_A:(c + 1) * HD_A]
            v = v_ref[:, cols]
            vh_ref[pl.ds(h, SEQ, stride=H_A), :] = v
            o_ref[:, cols] = attend(_subhead_norm(q_ref[:, cols], qg_ref[...]), k, v)
        return

    @pl.when(pl.program_id(2) == 0)
    def _():
        kall_ref[0:PAST_LEN, :] = ck_ref[...].astype(BF16)
        vall_ref[0:PAST_LEN, :] = cv_ref[...].astype(BF16)
        k_new = _rope(_subhead_norm(k_ref[...], kg_ref[...]), ckk_ref[...], skk_ref[...])
        kall_ref[PAST_LEN:, :] = k_new.astype(BF16)
        vall_ref[PAST_LEN:, :] = v_ref[...].astype(BF16)

    q = _rope(_subhead_norm(q_ref[...], qg_ref[...]), cq_ref[...], sq_ref[...])
    o_ref[...] = attend(q, kall_ref[...], vall_ref[...])


def _attention_prompt(qkv, qg2, kg2, lam_p, sub_g, lam_init):
    head = 2 * HD_A
    small = [
        pl.BlockSpec((1, head), lambda b: (0, 0)),
        pl.BlockSpec((1, head), lambda b: (0, 0)),
        pl.BlockSpec((4, HD_A), lambda b: (0, 0)),
        pl.BlockSpec((1, head), lambda b: (0, 0)),
    ]
    return pl.pallas_call(
        functools.partial(_attn_kernel, lam_init, False),
        grid=(BATCH,),
        in_specs=[
            pl.BlockSpec((SEQ, W_A), lambda b: (b, 0)),
            pl.BlockSpec((SEQ, W_A), lambda b: (b, 1)),
            pl.BlockSpec((SEQ, W_A), lambda b: (b, 2)),
        ] + small,
        out_specs=[pl.BlockSpec((SEQ, W_A), lambda b: (b, 0)),
                   pl.BlockSpec((SEQ * 2 * H_A, HD_A), lambda b: (b, 0)),
                   pl.BlockSpec((SEQ * H_A, head), lambda b: (b, 0))],
        out_shape=[jax.ShapeDtypeStruct((T_P, W_A), F32), jax.ShapeDtypeStruct((T_P * 2 * H_A, HD_A), F32),
                   jax.ShapeDtypeStruct((T_P * H_A, head), F32)],
        compiler_params=_cparams(1),
    )(qkv, qkv, qkv, qg2, kg2, lam_p, sub_g)


def _attention_sample(qkv, cache_k, cache_v, cos, sin, qg2, kg2, lam_p, sub_g, lam_init):
    nh = H_A
    head = 2 * HD_A
    tq = ATTN_Q_TILE
    nq = DEC_SEQ // tq
    q_off = T_P // tq
    k_off = T_P // DEC_SEQ
    small = [
        pl.BlockSpec((1, head), lambda b, h, i: (0, 0)),
        pl.BlockSpec((1, head), lambda b, h, i: (0, 0)),
        pl.BlockSpec((4, HD_A), lambda b, h, i: (0, 0)),
        pl.BlockSpec((1, head), lambda b, h, i: (0, 0)),
    ]
    return pl.pallas_call(
        functools.partial(_attn_kernel, lam_init, True),
        grid=(DEC_BATCH, nh, nq),
        in_specs=[
            pl.BlockSpec((tq, head), lambda b, h, i: (q_off + b * nq + i, h)),
            pl.BlockSpec((DEC_SEQ, head), lambda b, h, i: (k_off + b, nh + h)),
            pl.BlockSpec((DEC_SEQ, head), lambda b, h, i: (k_off + b, 2 * nh + h)),
            pl.BlockSpec((None, PAST_LEN, head), lambda b, h, i: (b, 0, h)),
            pl.BlockSpec((None, PAST_LEN, head), lambda b, h, i: (b, 0, h)),
            pl.BlockSpec((tq, head), lambda b, h, i: (i, 0)),
            pl.BlockSpec((tq, head), lambda b, h, i: (i, 0)),
            pl.BlockSpec((DEC_SEQ, head), lambda b, h, i: (0, 0)),
            pl.BlockSpec((DEC_SEQ, head), lambda b, h, i: (0, 0)),
        ] + small,
        out_specs=pl.BlockSpec((tq, head), lambda b, h, i: (b * nq + i, h)),
        out_shape=jax.ShapeDtypeStruct((T_S, W_A), F32),
        scratch_shapes=[pltpu.VMEM((PAST_LEN + DEC_SEQ, head), BF16)] * 2,
        compiler_params=_cparams(3),
    )(qkv, qkv, qkv, cache_k, cache_v, cos, sin, cos, sin, qg2, kg2, lam_p, sub_g)


def _rope_tables():
    half = HD_A // 2
    nf = half // 2
    inv = ROPE_BASE ** (-np.arange(nf, dtype=np.float32) / nf)
    pos = np.arange(DEC_SEQ)
    row = (pos // GRID_W).astype(np.float32)
    col = (pos % GRID_W).astype(np.float32)
    ang_r = (row[:, None] * inv).astype(np.float32)
    ang_c = (col[:, None] * inv).astype(np.float32)
    ang = np.concatenate([ang_r, ang_r, ang_c, ang_c], axis=1)
    sign = np.concatenate([-np.ones(nf), np.ones(nf), -np.ones(nf), np.ones(nf)]).astype(np.float32)
    cos = np.cos(ang.astype(np.float64)).astype(np.float32)
    sin = (np.sin(ang.astype(np.float64)) * sign).astype(np.float32)
    return jnp.asarray(np.tile(cos, (1, 2))), jnp.asarray(np.tile(sin, (1, 2)))


def _mlstm_kernel(seq, has_ctx, group, *refs):
    if has_ctx:
        (q_ref, k_ref, cwq_ref, cwk_ref, cbq_ref, cbk_ref, v_ref, mo_ref, gi_ref, gf_ref,
         gbi_ref, gbf_ref, hn_ref, c0_ref, n0_ref, m0_ref, o_ref,
         qs_ref, ks_ref, hf_ref, hb_ref, cs_ref, rrow_ref, col_ref, wc_ref) = refs
    else:
        (q_ref, k_ref, cwq_ref, cwk_ref, cbq_ref, cbk_ref, v_ref, mo_ref, gi_ref, gf_ref,
         gbi_ref, gbf_ref, hn_ref, o_ref, c_out_ref, n_out_ref, m_out_ref,
         qs_ref, ks_ref, hf_ref, hb_ref, cs_ref, rrow_ref, col_ref, wc_ref) = refs
    nc = seq // CHUNK
    n_chain = 2 * H_B
    chains = [(sub, d, h) for sub in range(group) for d in range(2) for h in range(H_B)]
    qs_ref[...] = _silu(_dwconv3(q_ref[...], cwq_ref[...], cbq_ref[...], seq)) * (HD_B ** -0.5)
    ks_ref[...] = _silu(_dwconv3(k_ref[...], cwk_ref[...], cbk_ref[...], seq))

    rows = nc * n_chain
    lane = lax.broadcasted_iota(jnp.int32, (rows, 2 * CHUNK), 1)
    forward = lax.broadcasted_iota(jnp.int32, (rows, 2 * CHUNK), 0) % n_chain < H_B
    valid = lane < CHUNK

    def scan(x, op, fill):
        pre, suf = x, x
        sh = 1
        while sh < CHUNK:
            pre = op(pre, jnp.where(lane >= sh, pltpu.roll(pre, sh, 1), fill))
            suf = op(suf, jnp.where(lane + sh < CHUNK, pltpu.roll(suf, 2 * CHUNK - sh, 1), fill))
            sh *= 2
        return jnp.where(forward, pre, suf)

    mm_final = []
    for sub in range(group):
        gate_i = (gi_ref[sub] + gbi_ref[...]).reshape(rows, 2 * CHUNK)
        lf = jnp.where(valid, _log_sigmoid(gf_ref[sub] + gbf_ref[...]).reshape(rows, 2 * CHUNK), 0.0)
        b = scan(lf, jnp.add, 0.0)
        cmax = scan(jnp.where(valid, gate_i - b, -jnp.inf), jnp.maximum, -jnp.inf)
        b_last = jnp.sum(lf, axis=1, keepdims=True)
        g = b_last - b + gate_i
        g_max = jnp.max(jnp.where(valid, g, -jnp.inf), axis=1, keepdims=True)
        mm = m0_ref[sub] if has_ctx else jnp.zeros((n_chain, 1), F32)
        mm_seq = []
        for p in range(nc):
            mm_seq.append(mm)
            seg = slice(p * n_chain, (p + 1) * n_chain)
            mm = jnp.maximum(b_last[seg] + mm, g_max[seg])
        mm_final.append(mm)
        mm_prev = jnp.concatenate(mm_seq, axis=0)
        mm_next = jnp.concatenate(mm_seq[1:] + [mm], axis=0)
        m_t = jnp.maximum(b + mm_prev, b + cmax)
        rrow_ref[sub] = (b - gate_i).reshape(nc, n_chain, 2 * CHUNK)
        wc_ref[sub] = jnp.exp(b_last + mm_prev - mm_next).reshape(nc, n_chain, 1)
        per_row = [b, m_t, jnp.exp(b + mm_prev - m_t), jnp.exp(-m_t), jnp.exp(g - mm_next)]
        for j, arr in enumerate(per_row):
            by_time = arr.T
            for p in range(nc):
                col_ref[sub, p, :, j * n_chain:(j + 1) * n_chain] = by_time[0:CHUNK, p * n_chain:(p + 1) * n_chain]

    t_idx = lax.broadcasted_iota(jnp.int32, (CHUNK, CHUNK), 0)
    s_idx = lax.broadcasted_iota(jnp.int32, (CHUNK, CHUNK), 1)
    for n, (sub, d, h) in enumerate(chains):
        cs_ref[n] = c0_ref[sub, d, h] if has_ctx else jnp.zeros((HD_B, HD_B), F32)

    def out_step(p, n_states):
        new_states = []
        for n, (sub, d, h) in enumerate(chains):
            cols = col_ref[sub, p]
            rrows = rrow_ref[sub, p]
            wcs = wc_ref[sub, p]
            n_loc = d * H_B + h
            c = p if d == 0 else nc - 1 - p
            r0 = pl.multiple_of(sub * seq + c * CHUNK, CHUNK)
            hcols = slice(h * HD_B, (h + 1) * HD_B)
            qt = qs_ref[pl.ds(r0, CHUNK), hcols]
            kt = ks_ref[pl.ds(r0, CHUNK), hcols]
            vt = v_ref[pl.ds(r0, CHUNK), hcols]
            b_col, m_t, w_inter, e_inv, w_k = (cols[:, j * n_chain + n_loc:j * n_chain + n_loc + 1] for j in range(5))
            mask = (s_idx <= t_idx) if d == 0 else (s_idx >= t_idx)
            decay = jnp.exp(jnp.where(mask, b_col - rrows[n_loc:n_loc + 1, 0:CHUNK], NEG) - m_t)
            qk = lax.dot_general(qt.astype(BF16), kt.astype(BF16), (((1,), (1,)), ((), ())),
                                 preferred_element_type=F32)
            s = qk * decay
            cm = cs_ref[n]
            nm = n_states[n]
            cq = lax.dot_general(qt.astype(BF16), cm.astype(BF16), (((1,), (1,)), ((), ())),
                                 preferred_element_type=F32)
            num = _bdot(s, vt) + w_inter * cq
            nq = jnp.sum(s, axis=-1, keepdims=True) + w_inter * jnp.sum(qt * nm, axis=-1, keepdims=True)
            hdir_ref = hf_ref if d == 0 else hb_ref
            hdir_ref[pl.ds(r0, CHUNK), hcols] = num / jnp.maximum(jnp.abs(nq), e_inv)
            w_c = wcs[n_loc:n_loc + 1, :]
            vw = (vt * w_k).astype(BF16)
            cs_ref[n] = w_c * cm + lax.dot_general(vw, kt.astype(BF16), (((0,), (0,)), ((), ())),
                                                   preferred_element_type=F32)
            new_states.append(w_c * nm + jnp.sum(kt * w_k, axis=0, keepdims=True))
        return tuple(new_states)

    if has_ctx:
        n_init = tuple(n0_ref[sub, d, h] for sub, d, h in chains)
    else:
        n_init = tuple(jnp.zeros((1, HD_B), F32) for _ in chains)
    n_final = lax.fori_loop(0, nc, out_step, n_init)
    if not has_ctx:
        for n, (sub, d, h) in enumerate(chains):
            n_loc = d * H_B + h
            c_out_ref[sub, d, h] = cs_ref[n]
            n_out_ref[sub, d, h] = n_final[n]
            m_out_ref[sub, d, h] = jnp.broadcast_to(mm_final[sub][n_loc:n_loc + 1, :], (1, HD_B))

    for h in range(H_B):
        hcols = slice(h * HD_B, (h + 1) * HD_B)
        hh = hf_ref[:, hcols] + hb_ref[:, hcols]
        ms = jnp.mean(hh * hh, axis=-1, keepdims=True)
        o_ref[:, hcols] = (hh * lax.rsqrt(ms + EPS) * hn_ref[:, hcols]) * _sigmoid(mo_ref[:, hcols])


def _mlstm(mqk, mv, mo, mg_stream, conv_w, conv_b, gate_b, hn_g, *, seq, nbatch, row_off, group, ctx=None):
    nh = H_B
    nc = seq // CHUNK
    has_ctx = ctx is not None
    assert nbatch % group == 0
    gt = mg_stream.reshape(nbatch, nc, CHUNK, 2, 2, nh).transpose(0, 1, 3, 4, 5, 2)
    pad = ((0, 0), (0, 0), (0, 0), (0, CHUNK))
    gates = [jnp.pad(jnp.concatenate([gt[:, :, 0, j], gt[:, ::-1, 1, j]], axis=2), pad) for j in range(2)]
    gate_bias = [jnp.concatenate([gate_b[0, j], gate_b[1, j]]).reshape(2 * nh, 1) for j in range(2)]
    blk = lambda col: pl.BlockSpec((group * seq, W_B), lambda b, col=col: (row_off + b, col))
    gate_blk = pl.BlockSpec((group, nc, 2 * nh, 2 * CHUNK), lambda b: (b, 0, 0, 0))
    in_specs = [
        blk(0), blk(1),
        pl.BlockSpec((3, W_B), lambda b: (0, 0)),
        pl.BlockSpec((3, W_B), lambda b: (0, 1)),
        pl.BlockSpec((1, W_B), lambda b: (0, 0)),
        pl.BlockSpec((1, W_B), lambda b: (0, 1)),
        blk(0), blk(0),
        gate_blk, gate_blk,
        pl.BlockSpec((2 * nh, 1), lambda b: (0, 0)),
        pl.BlockSpec((2 * nh, 1), lambda b: (0, 0)),
        pl.BlockSpec((1, W_B), lambda b: (0, 0)),
    ]
    args = [mqk, mqk, conv_w, conv_w, conv_b, conv_b, mv, mo, gates[0], gates[1],
            gate_bias[0], gate_bias[1], hn_g.reshape(1, W_B)]
    o_spec = pl.BlockSpec((group * seq, W_B), lambda b: (b, 0))
    o_shape = jax.ShapeDtypeStruct((nbatch * seq, W_B), F32)
    state_blk = lambda rows: pl.BlockSpec((group, 2, nh, rows, HD_B), lambda b: (b, 0, 0, 0, 0))
    if has_ctx:
        c0, n0, m0 = ctx
        in_specs += [state_blk(HD_B), state_blk(1), pl.BlockSpec((group, 2 * nh, 1), lambda b: (b, 0, 0))]
        args += [c0, n0.reshape(nbatch, 2, nh, 1, HD_B), m0.reshape(nbatch, 2 * nh, 1)]
        out_specs, out_shape = o_spec, o_shape
    else:
        out_specs = [o_spec, state_blk(HD_B), state_blk(1), state_blk(1)]
        out_shape = [
            o_shape,
            jax.ShapeDtypeStruct((nbatch, 2, nh, HD_B, HD_B), F32),
            jax.ShapeDtypeStruct((nbatch, 2, nh, 1, HD_B), F32),
            jax.ShapeDtypeStruct((nbatch, 2, nh, 1, HD_B), F32),
        ]
    return pl.pallas_call(
        functools.partial(_mlstm_kernel, seq, has_ctx, group),
        grid=(nbatch // group,),
        in_specs=in_specs,
        out_specs=out_specs,
        out_shape=out_shape,
        scratch_shapes=[pltpu.VMEM((group * seq, W_B), F32)] * 4 + [
            pltpu.VMEM((group * 2 * nh, HD_B, HD_B), F32),
            pltpu.VMEM((group, nc, 2 * nh, 2 * CHUNK), F32),
            pltpu.VMEM((group, nc, CHUNK, 5 * 2 * nh), F32),
            pltpu.VMEM((group, nc, 2 * nh, 1), F32),
        ],
        compiler_params=_cparams(1),
    )(*args)


def _dft_mats(L):
    f = np.arange(L)[:, None]
    j = np.arange(L)[None, :]
    ang = 2.0 * np.pi * ((f * j) % (2 * L)) / (2 * L)
    cm = np.cos(ang)
    sm = np.sin(ang)
    alt = (1.0 - 2.0 * (np.arange(L) % 2))
    fwd_b = -sm
    fwd_b[0, :] = alt
    fwd = np.concatenate([cm, fwd_b], axis=0)
    wgt = np.where(np.arange(L) == 0, 1.0, 2.0)[None, :]
    inv_a = cm.T * wgt
    inv_b = -2.0 * sm.T
    inv_b[:, 0] = alt
    inv = np.concatenate([inv_a, inv_b], axis=1) / (2 * L)
    return jnp.asarray(fwd.astype(np.float32)), jnp.asarray(inv.astype(np.float32))


def _hyena_feats(L):
    t = np.linspace(0.0, 1.0, L, dtype=np.float32)
    wpos = (2.0 * math.pi * np.arange(L, dtype=np.float32) / L).astype(np.float32)
    fb = np.linspace(1e-4, HY_BANDS - 1, HY_BANDS, dtype=np.float32)
    z = (wpos[:, None] * fb).astype(np.float32)
    feats = np.concatenate([t[:, None], np.cos(z), -np.sin(z)], axis=-1).astype(np.float32)
    deltas = np.abs(np.linspace(math.log(HY_TARGET) / HY_SLOW_PCT, math.log(HY_TARGET) / HY_FAST_PCT,
                                D_MODEL, dtype=np.float32))
    decay = np.exp(-t[:, None] * deltas).astype(np.float32)
    return jnp.asarray(feats), jnp.asarray(decay)


def _filter_kernel(L, feats_ref, w1_ref, b1_ref, fr1_ref, w2_ref, b2_ref, fr2_ref, w3f_ref, w3b_ref,
                   decay_ref, fwd_ref, o_ref, hdn_ref):
    @pl.when((pl.program_id(0) == 0) & (pl.program_id(1) == 0))
    def _():
        h1 = jnp.sin(fr1_ref[...] * (jnp.dot(feats_ref[...], w1_ref[...], precision=HIGHEST,
                                             preferred_element_type=F32) + b1_ref[...]))
        hdn_ref[...] = jnp.sin(fr2_ref[...] * (jnp.dot(h1, w2_ref[...], precision=HIGHEST,
                                                       preferred_element_type=F32) + b2_ref[...]))

    hdn = hdn_ref[...]
    decay = decay_ref[...]
    f_fwd = jnp.dot(hdn, w3f_ref[...], precision=HIGHEST, preferred_element_type=F32) * decay
    f_bwd = jnp.dot(hdn, w3b_ref[...], precision=HIGHEST, preferred_element_type=F32) * decay
    row = lax.broadcasted_iota(jnp.int32, f_bwd.shape, 0)
    f_bwd = jnp.where(row == 0, 0.0, f_bwd)
    f_sum = f_fwd + f_bwd
    o_ref[0:L, :] = _bdot(fwd_ref[0:L, :], f_sum)
    imag = _bdot(fwd_ref[L:2 * L, :], f_fwd - f_bwd)
    nyquist = _bdot(fwd_ref[L:L + 16, :], f_sum)[0:1]
    o_ref[L:2 * L, :] = jnp.where(row == 0, nyquist, imag)


def _hyena_filter_spectrum(L, fwd_bf, w1, b1, fr1, w2, b2, fr2, w3):
    feats, decay = _hyena_feats(L)
    td = 512
    nd = D_MODEL // td
    emb = feats.shape[1]
    vec = lambda a: a.reshape(1, HY_FH)
    full = lambda shape: pl.BlockSpec(shape, lambda o, j: (0, 0))
    return pl.pallas_call(
        functools.partial(_filter_kernel, L),
        grid=(HY_ORDER, nd),
        in_specs=[
            full((L, emb)), full((emb, HY_FH)), full((1, HY_FH)), full((1, HY_FH)),
            full((HY_FH, HY_FH)), full((1, HY_FH)), full((1, HY_FH)),
            pl.BlockSpec((HY_FH, td), lambda o, j: (0, o * 2 * nd + j)),
            pl.BlockSpec((HY_FH, td), lambda o, j: (0, o * 2 * nd + nd + j)),
            pl.BlockSpec((L, td), lambda o, j: (0, j)),
            full((2 * L, L)),
        ],
        out_specs=pl.BlockSpec((2 * L, td), lambda o, j: (0, o * nd + j)),
        out_shape=jax.ShapeDtypeStruct((2 * L, HY_ORDER * D_MODEL), F32),
        scratch_shapes=[pltpu.VMEM((L, HY_FH), F32)],
        compiler_params=_cparams(2),
    )(feats, w1, vec(b1), vec(fr1), w2, vec(b2), vec(fr2), w3, w3, decay, fwd_bf)


def _spectral_conv(u, fwd, inv, kspec, L):
    uf = jnp.dot(fwd, u.astype(BF16), preferred_element_type=F32)
    ua, ub = uf[0:L], uf[L:2 * L]
    ka, kb = kspec[0:L], kspec[L:2 * L]
    first = lax.broadcasted_iota(jnp.int32, ua.shape, 0) == 0
    ya = ua * ka - jnp.where(first, 0.0, ub * kb)
    yb = jnp.where(first, ub * kb, ua * kb + ub * ka)
    y = jnp.concatenate([ya, yb], axis=0).astype(BF16)
    return jnp.dot(inv, y, preferred_element_type=F32)


def _hyena_kernel(L, zv_ref, z1_ref, z2_ref, cwv_ref, cw1_ref, cw2_ref, cbv_ref, cb1_ref, cb2_ref,
                  fwd_ref, inv_ref, k0_ref, k1_ref, bias0_ref, bias1_ref, o_ref):
    fwd = fwd_ref[...]
    inv = inv_ref[...]
    v = _dwconv3(zv_ref[...], cwv_ref[...], cbv_ref[...])
    x1 = _dwconv3(z1_ref[...], cw1_ref[...], cb1_ref[...])
    x2 = _dwconv3(z2_ref[...], cw2_ref[...], cb2_ref[...])
    z = x1 * (_spectral_conv(v, fwd, inv, k0_ref[...], L) + v * bias0_ref[...])
    o_ref[...] = x2 * (_spectral_conv(z, fwd, inv, k1_ref[...], L) + z * bias1_ref[...])


def _hyena_core(zproj, conv_w, conv_b, fwd_bf, inv_bf, kspec, bias, *, seq, nbatch, row_off, td):
    nd = D_MODEL // td
    zblk = lambda part: pl.BlockSpec((seq, td), lambda b, j, part=part: (row_off + b, part * nd + j))
    cwblk = lambda part: pl.BlockSpec((3, td), lambda b, j, part=part: (0, part * nd + j))
    cbblk = lambda part: pl.BlockSpec((1, td), lambda b, j, part=part: (0, part * nd + j))
    return pl.pallas_call(
        functools.partial(_hyena_kernel, seq),
        grid=(nbatch, nd),
        in_specs=[
            zblk(0), zblk(1), zblk(2), cwblk(0), cwblk(1), cwblk(2), cbblk(0), cbblk(1), cbblk(2),
            pl.BlockSpec((2 * seq, seq), lambda b, j: (0, 0), pipeline_mode=pl.Buffered(1)),
            pl.BlockSpec((seq, 2 * seq), lambda b, j: (0, 0), pipeline_mode=pl.Buffered(1)),
            pl.BlockSpec((2 * seq, td), lambda b, j: (0, j)),
            pl.BlockSpec((2 * seq, td), lambda b, j: (0, nd + j)),
            pl.BlockSpec((None, 1, td), lambda b, j: (0, 0, j)),
            pl.BlockSpec((None, 1, td), lambda b, j: (1, 0, j)),
        ],
        out_specs=pl.BlockSpec((seq, td), lambda b, j: (b, j)),
        out_shape=jax.ShapeDtypeStruct((nbatch * seq, D_MODEL), F32),
        compiler_params=_cparams(2),
    )(zproj, zproj, zproj, conv_w, conv_w, conv_w, conv_b, conv_b, conv_b,
      fwd_bf, inv_bf, kspec, kspec, bias.reshape(HY_ORDER, 1, D_MODEL), bias.reshape(HY_ORDER, 1, D_MODEL))


def _route_tile(y_tile, row, ada_ref, g_ref, rw_ref, rb_ref, h_ref, idx_ref, wt_ref):
    h = _modulate(y_tile, g_ref[...], _ada_chunk(ada_ref, row, 3), _ada_chunk(ada_ref, row, 4))
    for j in range(ROW_CHUNKS):
        h_ref[pl.ds(j, ROW_TILE, stride=ROW_CHUNKS), :] = h[:, j * LANES:(j + 1) * LANES]
    logits = lax.dot_general(rw_ref[...], h, (((1,), (1,)), ((), ())), precision=HIGHEST,
                             preferred_element_type=F32) + rb_ref[...]
    expert = lax.broadcasted_iota(jnp.int32, logits.shape, 0)
    slot = lax.broadcasted_iota(jnp.int32, (TOP_K, logits.shape[1]), 0)
    vals = jnp.zeros((TOP_K, logits.shape[1]), F32)
    idxs = jnp.zeros((TOP_K, logits.shape[1]), jnp.int32)
    cur = logits
    for k in range(TOP_K):
        m = jnp.max(cur, axis=0, keepdims=True)
        a = jnp.min(jnp.where(cur == m, expert, N_EXPERTS), axis=0, keepdims=True)
        vals = jnp.where(slot == k, m, vals)
        idxs = jnp.where(slot == k, a, idxs)
        cur = jnp.where(expert == a, -jnp.inf, cur)
    e = jnp.exp(vals - vals[0:1])
    wt_ref[...] = e / jnp.sum(e, axis=0, keepdims=True)
    idx_ref[...] = idxs


def _dispatch_kernel(rows_ref, h_ref, o_ref):
    base = pl.program_id(0) * GATHER_ROWS
    for r in range(GATHER_ROWS):
        t = pl.multiple_of(rows_ref[base + r] * ROW_CHUNKS, ROW_CHUNKS)
        o_ref[r * ROW_CHUNKS:(r + 1) * ROW_CHUNKS, :] = h_ref[pl.ds(t, ROW_CHUNKS), :]


def _dispatch(h_tiles, gather_row):
    grid_spec = pltpu.PrefetchScalarGridSpec(
        num_scalar_prefetch=1,
        grid=(X_ROWS // GATHER_ROWS,),
        in_specs=[pl.BlockSpec((T_ALL * ROW_CHUNKS, LANES), lambda i, rows: (0, 0), pipeline_mode=pl.Buffered(1))],
        out_specs=pl.BlockSpec((GATHER_ROWS * ROW_CHUNKS, LANES), lambda i, rows: (i, 0)),
    )
    return pl.pallas_call(
        _dispatch_kernel,
        grid_spec=grid_spec,
        out_shape=jax.ShapeDtypeStruct((X_ROWS * ROW_CHUNKS, LANES), F32),
        compiler_params=_cparams(1),
    )(gather_row, h_tiles)


def _deinterleave_matrix():
    s = np.zeros((256, 256), np.float32)
    j = np.arange(128)
    s[2 * j, j] = 1.0
    s[2 * j + 1, 128 + j] = 1.0
    return jnp.asarray(s)


def _weight_copies(layer, e, w1_hbm, w2_hbm, w1s_ref, w2s_ref, sem):
    copies = []
    r1 = D_MODEL // W1_DMA_CHUNKS
    for c in range(W1_DMA_CHUNKS):
        copies.append(pltpu.make_async_copy(w1_hbm.at[layer, e, pl.ds(c * r1, r1)],
                                            w1s_ref.at[pl.ds(c * r1, r1)], sem.at[c]))
    r2 = D_FF // W2_DMA_CHUNKS
    for c in range(W2_DMA_CHUNKS):
        copies.append(pltpu.make_async_copy(w2_hbm.at[layer, e, pl.ds(c * r2, r2)],
                                            w2s_ref.at[pl.ds(c * r2, r2)], sem.at[W1_DMA_CHUNKS + c]))
    return copies


def _expert_kernel(layer, te_ref, tf_ref, ne_ref, nu_ref, src_ref, xo_ref, vr_ref, x_ref, b1_ref, b2_ref, wt_ref, s_ref,
                   w1_hbm, w2_hbm, o_hbm, w1s_ref, w2s_ref, w1p_ref, w2p_ref, acc_ref, out_ref, wsem, osem):
    i = pl.program_id(0)
    half = 128
    copies = functools.partial(_weight_copies, layer, w1_hbm=w1_hbm, w2_hbm=w2_hbm,
                               w1s_ref=w1s_ref, w2s_ref=w2s_ref, sem=wsem)

    @pl.when(i == 0)
    def _():
        acc_ref[...] = jnp.zeros_like(acc_ref)
        out_ref[...] = jnp.zeros_like(out_ref)
        for cp in copies(te_ref[0]):
            cp.start()

    @pl.when(tf_ref[i] == 1)
    def _():
        for cp in copies(te_ref[i]):
            cp.wait()
        s = s_ref[...].astype(BF16)
        for c in range(2 * D_FF // 256):
            blk = jnp.dot(w1s_ref[:, c * 256:(c + 1) * 256].astype(BF16), s, preferred_element_type=F32)
            w1p_ref[:, c * half:(c + 1) * half] = blk[:, :half].astype(BF16)
            w1p_ref[:, D_FF + c * half:D_FF + (c + 1) * half] = blk[:, half:].astype(BF16)
        w2p_ref[...] = w2s_ref[...].astype(BF16)

        @pl.when(ne_ref[i] >= 0)
        def _():
            for cp in copies(ne_ref[i]):
                cp.start()

    def tile_step(m):
        base = i * MOE_TILE
        prev = (i + 1) % 2
        for r0 in range(0, MOE_TILE, SCATTER_GROUP):
            toks = [pl.multiple_of(src_ref[base + r0 + g] * ROW_CHUNKS, ROW_CHUNKS) for g in range(SCATTER_GROUP)]
            cur = [acc_ref[pl.ds(toks[g], ROW_CHUNKS), :] for g in range(SCATTER_GROUP)]
            add = [out_ref[prev, (r0 + g) * ROW_CHUNKS:(r0 + g + 1) * ROW_CHUNKS, :] for g in range(SCATTER_GROUP)]
            for g in range(SCATTER_GROUP):
                acc_ref[pl.ds(toks[g], ROW_CHUNKS), :] = cur[g] + add[g]
        a = jnp.dot(_token_rows(x_ref, m).astype(BF16), w1p_ref[...], preferred_element_type=F32) + b1_ref[...]
        glu = jnp.minimum(a[:, :D_FF], SWIGLU_LIMIT)
        lin = jnp.clip(a[:, D_FF:], -SWIGLU_LIMIT, SWIGLU_LIMIT)
        hid = glu * _sigmoid(SWIGLU_ALPHA * glu) * (lin + 1.0)
        out = (jnp.dot(hid.astype(BF16), w2p_ref[...], preferred_element_type=F32) + b2_ref[...]) * wt_ref[0:m, :]
        cur_buf = i % 2
        for j in range(ROW_CHUNKS):
            out_ref[cur_buf, pl.ds(j, m, stride=ROW_CHUNKS), :] = out[:, j * LANES:(j + 1) * LANES]

    live = i <= nu_ref[0]
    pl.when(live & (vr_ref[i] > MOE_TILE // 2))(functools.partial(tile_step, MOE_TILE))
    pl.when(live & (vr_ref[i] <= MOE_TILE // 2))(functools.partial(tile_step, MOE_TILE // 2))

    @pl.when(i == pl.num_programs(0) - 1)
    def _():
        cp = pltpu.make_async_copy(acc_ref.at[pl.ds(0, T_ALL * ROW_CHUNKS)], o_hbm, osem)
        cp.start()
        cp.wait()


def _experts(layer, x_sorted, w_sorted, plan, w1, b1p, w2, b2):
    tile_expert, tile_first, next_expert, n_used, src, x_off, valid_rows = plan
    grid_spec = pltpu.PrefetchScalarGridSpec(
        num_scalar_prefetch=7,
        grid=(MOE_TILES,),
        in_specs=[
            pl.BlockSpec((pl.Element(MOE_TILE * ROW_CHUNKS), pl.Element(LANES)),
                         lambda i, te, tf, ne, nu, src, xo, vr: (pl.multiple_of(xo[i] * ROW_CHUNKS, X_ALIGN * ROW_CHUNKS), 0)),
            pl.BlockSpec((None, None, 1, 2 * D_FF), lambda i, te, *_: (layer, te[i], 0, 0)),
            pl.BlockSpec((None, None, 1, D_MODEL), lambda i, te, *_: (layer, te[i], 0, 0)),
            pl.BlockSpec((MOE_TILE, 1), lambda i, te, *_: (i, 0)),
            pl.BlockSpec((256, 256), lambda i, te, *_: (0, 0)),
            pl.BlockSpec(memory_space=pl.ANY),
            pl.BlockSpec(memory_space=pl.ANY),
        ],
        out_specs=pl.BlockSpec(memory_space=pl.ANY),
        scratch_shapes=[
            pltpu.VMEM((D_MODEL, 2 * D_FF), F32),
            pltpu.VMEM((D_FF, D_MODEL), F32),
            pltpu.VMEM((D_MODEL, 2 * D_FF), BF16),
            pltpu.VMEM((D_FF, D_MODEL), BF16),
            pltpu.VMEM((ACC_ROWS * ROW_CHUNKS, LANES), F32),
            pltpu.VMEM((2, MOE_TILE * ROW_CHUNKS, LANES), F32),
            pltpu.SemaphoreType.DMA((W1_DMA_CHUNKS + W2_DMA_CHUNKS,)),
            pltpu.SemaphoreType.DMA(()),
        ],
    )
    return pl.pallas_call(
        functools.partial(_expert_kernel, layer),
        grid_spec=grid_spec,
        out_shape=jax.ShapeDtypeStruct((T_ALL * ROW_CHUNKS, LANES), F32),
        compiler_params=_cparams(1),
    )(tile_expert, tile_first, next_expert, n_used, src, x_off, valid_rows, x_sorted, b1p, b2, w_sorted,
      _deinterleave_matrix(), w1, w2)


def _combine_kernel(first_tile, y_ref, a_ref, ada_ref, o_ref):
    gate = _ada_chunk(ada_ref, _cond_row(first_tile + pl.program_id(0)), 5)
    o_ref[...] = y_ref[...] + gate * _token_rows(a_ref, ROW_TILE)


def _combine(y, acc, ada_l, first_tile=0, n_tiles=N_ROW_TILES):
    return pl.pallas_call(
        functools.partial(_combine_kernel, first_tile),
        grid=(n_tiles,),
        in_specs=[
            pl.BlockSpec((ROW_TILE, D_MODEL), lambda i: (first_tile + i, 0)),
            pl.BlockSpec((ROW_TILE * ROW_CHUNKS, LANES), lambda i: (first_tile + i, 0)),
            pl.BlockSpec((COND_ROWS, ADA_CHUNKS * D_MODEL), lambda i: (0, 0)),
        ],
        out_specs=pl.BlockSpec((ROW_TILE, D_MODEL), lambda i: (i, 0)),
        out_shape=jax.ShapeDtypeStruct((n_tiles * ROW_TILE, D_MODEL), F32),
        compiler_params=_cparams(1),
    )(y, acc, ada_l)


def _routing_plan(idx, wts):
    eid = idx.reshape(-1)
    order = jnp.argsort(eid, stable=True).astype(jnp.int32)
    experts = jnp.arange(N_EXPERTS, dtype=jnp.int32)
    counts = jnp.sum(eid[:, None] == experts[None, :], axis=0).astype(jnp.int32)
    ntiles = (counts + MOE_TILE - 1) // MOE_TILE
    tile_end = jnp.cumsum(ntiles).astype(jnp.int32)
    tile_begin = tile_end - ntiles
    cstarts = (jnp.cumsum(counts) - counts).astype(jnp.int32)
    n_used = tile_end[-1]
    tile = jnp.arange(MOE_TILES, dtype=jnp.int32)
    te = jnp.minimum(jnp.sum(tile[:, None] >= tile_end[None, :], axis=1), N_EXPERTS - 1).astype(jnp.int32)
    used = tile < n_used
    prev = jnp.concatenate([jnp.full((1,), -1, jnp.int32), te[:-1]])
    first = (te != prev) & used

    def pick(onehot, table):
        return jnp.sum(jnp.where(onehot, table[None, :], 0), axis=1).astype(jnp.int32)

    tile_is = te[:, None] == experts[None, :]
    later = (experts[None, :] > experts[:, None]) & (ntiles[None, :] > 0)
    following = jnp.min(jnp.where(later, experts[None, :], N_EXPERTS), axis=1)
    following = jnp.where(following < N_EXPERTS, following, -1)
    next_expert = jnp.where(first, pick(tile_is, following), -1).astype(jnp.int32)
    tile_in_expert = tile - pick(tile_is, tile_begin)
    off = tile_in_expert[:, None] * MOE_TILE + jnp.arange(MOE_TILE, dtype=jnp.int32)[None, :]
    valid_rows = jnp.where(used, jnp.clip(pick(tile_is, counts) - tile_in_expert * MOE_TILE, 0, MOE_TILE), 0)
    valid = jnp.arange(MOE_TILE, dtype=jnp.int32)[None, :] < valid_rows[:, None]
    assign = order[jnp.clip(pick(tile_is, cstarts)[:, None] + off, 0, N_ASSIGN - 1)]
    token = assign // TOP_K
    src = jnp.where(valid, token, SPARE_ROW).reshape(MOE_ROWS).astype(jnp.int32)
    src = jnp.concatenate([jnp.full((MOE_TILE,), SPARE_ROW, jnp.int32), src])
    w_sorted = jnp.where(valid, wts.reshape(-1)[assign], 0.0).reshape(MOE_ROWS, 1)
    seg = ((counts + X_ALIGN - 1) // X_ALIGN) * X_ALIGN
    seg_end = jnp.cumsum(seg).astype(jnp.int32)
    seg_begin = seg_end - seg
    x_off = jnp.where(used, pick(tile_is, seg_begin) + tile_in_expert * MOE_TILE, 0).astype(jnp.int32)
    group = jnp.arange(X_ROWS // X_ALIGN, dtype=jnp.int32) * X_ALIGN
    group_is = (group[:, None] >= seg_begin[None, :]) & (group[:, None] < seg_end[None, :])
    xoffset = (group - pick(group_is, seg_begin))[:, None] + jnp.arange(X_ALIGN, dtype=jnp.int32)[None, :]
    xassign = order[jnp.clip(pick(group_is, cstarts)[:, None] + xoffset, 0, N_ASSIGN - 1)]
    gather_row = jnp.where(xoffset < pick(group_is, counts)[:, None], xassign // TOP_K, 0).reshape(X_ROWS)
    plan = (te, first.astype(jnp.int32), next_expert, n_used.reshape(1), src, x_off, valid_rows.astype(jnp.int32))
    return plan, gather_row, w_sorted


def _moe(layer, y, routed, ada_l, w1, b1p, w2, b2):
    h, idx_t, wts_t = routed
    plan, gather_row, w_sorted = _routing_plan(idx_t.T, wts_t.T)
    x_sorted = _dispatch(h, gather_row)
    acc = _experts(layer, x_sorted, w_sorted, plan, w1, b1p, w2, b2)
    if layer == DEPTH - 1:
        return (_combine(y, acc, ada_l, 0, P_TILES), _combine(y, acc, ada_l, P_TILES, N_ROW_TILES - P_TILES)), None
    return y, (acc, ada_l)


def kernel(x_prompt, x_sample, cache_attn_k, cache_attn_v, state_mlstm_C, state_mlstm_n, state_mlstm_m, c, c_ctx, ada_w, ada_b, norm_mix_g, norm_ffn_g, ab_w_in, ab_w_out, da_qnorm_g, da_knorm_g, da_lambda, da_subnorm_g, ml_conv_w, ml_conv_b, ml_gate_b, ml_headnorm_g, hy_w_in, hy_w_out, hy_conv_w, hy_conv_b, hy_f_w1, hy_f_b1, hy_f_freq1, hy_f_w2, hy_f_b2, hy_f_freq2, hy_f_w3, hy_bias, router_w, router_b, moe_w1, moe_b1, moe_w2, moe_b2):
    y = (x_prompt.reshape(T_P, D_MODEL), x_sample.reshape(T_S, D_MODEL))
    cond = jnp.concatenate([c_ctx[None, :], c, jnp.zeros((COND_ROWS - 1 - DEC_BATCH, D_MODEL), F32)], axis=0)
    ada = _ada_table(cond, ada_w, ada_b)
    b1p = moe_b1.reshape(DEPTH, N_EXPERTS, D_FF, 2).swapaxes(2, 3).reshape(DEPTH, N_EXPERTS, 1, 2 * D_FF)
    b2r = moe_b2.reshape(DEPTH, N_EXPERTS, 1, D_MODEL)
    new_k, new_v, new_c, new_n, new_m = [], [], [], [], []
    pending = None
    for layer in range(DEPTH):
        ada_l = ada[layer]
        route_prm = (norm_ffn_g[layer], router_w[layer], router_b[layer])
        if layer % 2 == 0:
            e = layer // 2
            lam_init = 0.8 - 0.6 * math.exp(-0.3 * layer)
            proj = _modulated_proj(y, ada_l, norm_mix_g[layer], ab_w_in[e],
                                   (3 * W_A, 2 * W_B, W_B, W_B, 4 * H_B), pending)
            if pending is not None:
                y = proj[-1]
            qkv, mqk, mv, mo, mg = proj[:5]
            qg2 = jnp.tile(da_qnorm_g[e], 2).reshape(1, 2 * HD_A)
            kg2 = jnp.tile(da_knorm_g[e], 2).reshape(1, 2 * HD_A)
            sub_g = da_subnorm_g[e].reshape(1, 2 * HD_A)
            oa_p, k_norm, v_heads = _attention_prompt(qkv, qg2, kg2, da_lambda[e], sub_g, lam_init)
            cos, sin = _rope_tables()
            oa_s = _attention_sample(
                qkv, cache_attn_k[:, e].reshape(DEC_BATCH, PAST_LEN, W_A),
                cache_attn_v[:, e].reshape(DEC_BATCH, PAST_LEN, W_A), cos, sin,
                qg2, kg2, da_lambda[e], sub_g, lam_init)
            ob_p, c_new, n_new, m_new = _mlstm(
                mqk, mv, mo, mg[:T_P], ml_conv_w[e], ml_conv_b[e].reshape(1, 2 * W_B), ml_gate_b[e],
                ml_headnorm_g[e], seq=SEQ, nbatch=BATCH, row_off=0, group=MLSTM_GROUP)
            ob_s = _mlstm(
                mqk, mv, mo, mg[T_P:], ml_conv_w[e], ml_conv_b[e].reshape(1, 2 * W_B), ml_gate_b[e],
                ml_headnorm_g[e], seq=DEC_SEQ, nbatch=DEC_BATCH, row_off=T_P // DEC_SEQ, group=1,
                ctx=(state_mlstm_C[:, e], state_mlstm_n[:, e], state_mlstm_m[:, e]))
            y, routed = _out_proj_residual([(oa_p, oa_s), (ob_p, ob_s)], y, ada_l, ab_w_out[e], 2, *route_prm)
            new_k.append(k_norm.reshape(BATCH, SEQ, H_A, 2, HD_A))
            new_v.append(v_heads.reshape(BATCH, SEQ, H_A, 2 * HD_A))
            new_c.append(c_new)
            new_n.append(n_new.reshape(BATCH, 2, H_B, HD_B))
            new_m.append(m_new[..., 0, 0])
        else:
            o = layer // 2
            proj = _modulated_proj(y, ada_l, norm_mix_g[layer], hy_w_in[o], (HY_PROJ,), pending)
            if pending is not None:
                y = proj[-1]
            zproj = proj[0]
            cores = []
            for seq, nbatch, row_off, td in ((SEQ, BATCH, 0, 512), (DEC_SEQ, DEC_BATCH, T_P // DEC_SEQ, 256)):
                fwd, inv = _dft_mats(seq)
                fwd_bf, inv_bf = fwd.astype(BF16), inv.astype(BF16)
                kspec = _hyena_filter_spectrum(seq, fwd_bf, hy_f_w1[o], hy_f_b1[o], hy_f_freq1[o], hy_f_w2[o],
                                               hy_f_b2[o], hy_f_freq2[o], hy_f_w3[o])
                cores.append(_hyena_core(zproj, hy_conv_w[o], hy_conv_b[o].reshape(1, HY_PROJ), fwd_bf, inv_bf,
                                         kspec, hy_bias[o], seq=seq, nbatch=nbatch, row_off=row_off, td=td))
            y, routed = _out_proj_residual([tuple(cores)], y, ada_l, hy_w_out[o], 2, *route_prm)
        y, pending = _moe(layer, y, routed, ada_l, moe_w1, b1p, moe_w2, b2r)
    y_p = y[0].reshape(BATCH, SEQ, D_MODEL)
    y_s = y[1].reshape(DEC_BATCH, DEC_SEQ, D_MODEL)
    return (y_p, y_s, jnp.stack(new_k, axis=1), jnp.stack(new_v, axis=1), jnp.stack(new_c, axis=1),
            jnp.stack(new_n, axis=1), jnp.stack(new_m, axis=1))
```

```python
import functools
import math

import numpy as np
import jax
import jax.numpy as jnp
from jax import lax
from jax.experimental import pallas as pl
from jax.experimental.pallas import tpu as pltpu

D_MODEL = 1024
BATCH = 16
SEQ = 256
DEPTH = 2
DEC_BATCH = 2
DEC_SEQ = 1024
PAST_LEN = 256
GRID_W = 64
W_A = D_MODEL // 2
HD_A = 64
H_A = W_A // (2 * HD_A)
W_B = D_MODEL - W_A
HD_B = 128
H_B = W_B // HD_B
AB_PROJ = 3 * W_A + 4 * W_B + 4 * H_B
ROPE_BASE = 10000.0
CHUNK = 64
HY_ORDER = 2
HY_PROJ = (HY_ORDER + 1) * D_MODEL
HY_BANDS = 8
HY_FH = 64
HY_TARGET = 1e-2
HY_FAST_PCT = 0.3
HY_SLOW_PCT = 1.5
N_EXPERTS = 32
TOP_K = 4
D_FF = D_MODEL
SWIGLU_ALPHA = 1.702
SWIGLU_LIMIT = 7.0
ADA_CHUNKS = 6
EPS = 1e-6
NEG = -1e30
F32 = jnp.float32
BF16 = jnp.bfloat16

T_P = BATCH * SEQ
T_S = DEC_BATCH * DEC_SEQ
T_ALL = T_P + T_S
ROW_TILE = 512
ATTN_Q_TILE = 256
N_ROW_TILES = T_ALL // ROW_TILE
P_TILES = T_P // ROW_TILE
S_TILES_PER_BATCH = DEC_SEQ // ROW_TILE
COND_ROWS = 8
MOE_TILE = 256
N_ASSIGN = T_ALL * TOP_K
MOE_ROWS = N_ASSIGN + N_EXPERTS * MOE_TILE
MOE_TILES = MOE_ROWS // MOE_TILE
X_ALIGN = 16
MLSTM_GROUP = 2
GATHER_ROWS = 1024
X_ROWS = -(-(N_ASSIGN + N_EXPERTS * X_ALIGN + MOE_TILE) // GATHER_ROWS) * GATHER_ROWS
SPARE_ROW = T_ALL
ACC_ROWS = T_ALL + 8
SCATTER_GROUP = 8
W1_DMA_CHUNKS = 8
W2_DMA_CHUNKS = 4
LANES = 128
ROW_CHUNKS = D_MODEL // LANES
VMEM_LIMIT = 56 * 1024 * 1024
HIGHEST = lax.Precision.HIGHEST


def _cparams(n_axes):
    return pltpu.CompilerParams(dimension_semantics=("arbitrary",) * n_axes,
                                vmem_limit_bytes=VMEM_LIMIT)


def _bdot(a, b):
    return jnp.dot(a.astype(BF16), b.astype(BF16), preferred_element_type=F32)


def _cond_row(i):
    return jnp.where(i < P_TILES, 0, 1 + (i - P_TILES) // S_TILES_PER_BATCH)


def _ada_chunk(ada_ref, row, j):
    return ada_ref[pl.ds(row, 1), j * D_MODEL:(j + 1) * D_MODEL]


def _modulate(x, g, shift, scale):
    ms = jnp.mean(x * x, axis=-1, keepdims=True)
    return (x * lax.rsqrt(ms + EPS) * g) * (1.0 + scale) + shift


def _sigmoid(x):
    return 1.0 / (1.0 + jnp.exp(-x))


def _silu(x):
    return x * _sigmoid(x)


def _log_sigmoid(x):
    return jnp.minimum(x, 0.0) - jnp.log(1.0 + jnp.exp(-jnp.abs(x)))


def _dwconv3(x, w, b, seq=None):
    n = x.shape[0]
    seq = n if seq is None else seq
    pos = lax.broadcasted_iota(jnp.int32, x.shape, 0) % seq
    prev = jnp.where(pos == 0, 0.0, pltpu.roll(x, 1, 0))
    nxt = jnp.where(pos == seq - 1, 0.0, pltpu.roll(x, n - 1, 0))
    return prev * w[0:1] + x * w[1:2] + nxt * w[2:3] + b


def _ada_kernel(cond_ref, w_ref, b_ref, o_ref):
    c = _silu(cond_ref[...])
    o_ref[...] = _bdot(c, w_ref[...]) + b_ref[...]


def _ada_table(cond, ada_w, ada_b):
    tn = 1536
    return pl.pallas_call(
        _ada_kernel,
        grid=(DEPTH, ADA_CHUNKS * D_MODEL // tn),
        in_specs=[
            pl.BlockSpec((COND_ROWS, D_MODEL), lambda l, j: (0, 0)),
            pl.BlockSpec((None, D_MODEL, tn), lambda l, j: (l, 0, j)),
            pl.BlockSpec((None, 1, tn), lambda l, j: (l, 0, j)),
        ],
        out_specs=pl.BlockSpec((None, COND_ROWS, tn), lambda l, j: (l, 0, j)),
        out_shape=jax.ShapeDtypeStruct((DEPTH, COND_ROWS, ADA_CHUNKS * D_MODEL), F32),
        compiler_params=_cparams(2),
    )(cond, ada_w, ada_b.reshape(DEPTH, 1, ADA_CHUNKS * D_MODEL))


def _stream_specs(y):
    if isinstance(y, tuple):
        return [pl.BlockSpec((ROW_TILE, D_MODEL), lambda i: (jnp.minimum(i, P_TILES - 1), 0)),
                pl.BlockSpec((ROW_TILE, D_MODEL), lambda i: (jnp.maximum(i - P_TILES, 0), 0))], list(y)
    return [pl.BlockSpec((ROW_TILE, D_MODEL), lambda i: (i, 0))], [y]


def _token_rows(ref, n):
    return jnp.concatenate([ref[pl.ds(j, n, stride=ROW_CHUNKS), :] for j in range(ROW_CHUNKS)], axis=1)


def _stream_tile(y_refs, i):
    if len(y_refs) == 2:
        return jnp.where(i < P_TILES, y_refs[0][...], y_refs[1][...])
    return y_refs[0][...]


def _proj_kernel(splits, n_y, has_pending, *refs):
    y_refs = refs[:n_y]
    refs = refs[n_y:]
    if has_pending:
        acc_ref, ada_prev_ref = refs[:2]
        refs = refs[2:]
    ada_ref, g_ref, w_ref = refs[:3]
    out_refs, wbf_ref = refs[3:-1], refs[-1]
    i = pl.program_id(0)

    @pl.when(i == 0)
    def _():
        wbf_ref[...] = w_ref[...].astype(BF16)

    row = _cond_row(i)
    y = _stream_tile(y_refs, i)
    if has_pending:
        y = y + _ada_chunk(ada_prev_ref, row, 5) * _token_rows(acc_ref, ROW_TILE)
        out_refs[-1][...] = y
        out_refs = out_refs[:-1]
    h = _modulate(y, g_ref[...], _ada_chunk(ada_ref, row, 0), _ada_chunk(ada_ref, row, 1))
    h = h.astype(BF16)
    lo = 0
    for o_ref, width in zip(out_refs, splits):
        o_ref[...] = jnp.dot(h, wbf_ref[:, lo:lo + width], preferred_element_type=F32)
        lo += width


def _modulated_proj(y, ada_l, g, w, splits, pending=None):
    n = w.shape[1]
    y_specs, y_args = _stream_specs(y)
    widths = tuple(splits)
    if pending is not None:
        y_specs = y_specs + [pl.BlockSpec((ROW_TILE * ROW_CHUNKS, LANES), lambda i: (i, 0)),
                             pl.BlockSpec((COND_ROWS, ADA_CHUNKS * D_MODEL), lambda i: (0, 0))]
        y_args = y_args + list(pending)
        widths = widths + (D_MODEL,)
    return pl.pallas_call(
        functools.partial(_proj_kernel, splits, len(y_args) - (2 if pending is not None else 0), pending is not None),
        grid=(N_ROW_TILES,),
        in_specs=y_specs + [
            pl.BlockSpec((COND_ROWS, ADA_CHUNKS * D_MODEL), lambda i: (0, 0)),
            pl.BlockSpec((1, D_MODEL), lambda i: (0, 0)),
            pl.BlockSpec((D_MODEL, n), lambda i: (0, 0), pipeline_mode=pl.Buffered(1)),
        ],
        out_specs=[pl.BlockSpec((ROW_TILE, s), lambda i: (i, 0)) for s in widths],
        out_shape=[jax.ShapeDtypeStruct((T_ALL, s), F32) for s in widths],
        scratch_shapes=[pltpu.VMEM((D_MODEL, n), BF16)],
        compiler_params=_cparams(1),
    )(*y_args, ada_l, g.reshape(1, D_MODEL), w)


def _out_proj_kernel(n_in, n_y, gate_chunk, *refs):
    x_refs = refs[:2 * n_in]
    y_refs = refs[2 * n_in:2 * n_in + n_y]
    ada_ref, w_ref, g_ref, rw_ref, rb_ref, o_ref, h_ref, idx_ref, wt_ref, wbf_ref = refs[2 * n_in + n_y:]
    i = pl.program_id(0)

    @pl.when(i == 0)
    def _():
        wbf_ref[...] = w_ref[...].astype(BF16)

    acc = None
    lo = 0
    for xp_ref, xs_ref in zip(x_refs[0::2], x_refs[1::2]):
        k = xp_ref.shape[1]
        x = jnp.where(i < P_TILES, xp_ref[...], xs_ref[...])
        part = jnp.dot(x.astype(BF16), wbf_ref[lo:lo + k, :], preferred_element_type=F32)
        acc = part if acc is None else acc + part
        lo += k
    row = _cond_row(i)
    y_new = _stream_tile(y_refs, i) + _ada_chunk(ada_ref, row, gate_chunk) * acc
    o_ref[...] = y_new
    _route_tile(y_new, row, ada_ref, g_ref, rw_ref, rb_ref, h_ref, idx_ref, wt_ref)


def _out_proj_residual(xs, y, ada_l, w, gate_chunk, ffn_g, router_w, router_b):
    y_specs, y_args = _stream_specs(y)
    x_specs = []
    for xp, _ in xs:
        x_specs.append(pl.BlockSpec((ROW_TILE, xp.shape[1]), lambda i: (jnp.minimum(i, P_TILES - 1), 0)))
        x_specs.append(pl.BlockSpec((ROW_TILE, xp.shape[1]), lambda i: (jnp.maximum(i - P_TILES, 0), 0)))
    y_new, h, idx_t, wts_t = pl.pallas_call(
        functools.partial(_out_proj_kernel, len(xs), len(y_args), gate_chunk),
        grid=(N_ROW_TILES,),
        in_specs=x_specs + y_specs + [
            pl.BlockSpec((COND_ROWS, ADA_CHUNKS * D_MODEL), lambda i: (0, 0)),
            pl.BlockSpec((D_MODEL, D_MODEL), lambda i: (0, 0), pipeline_mode=pl.Buffered(1)),
            pl.BlockSpec((1, D_MODEL), lambda i: (0, 0)),
            pl.BlockSpec((N_EXPERTS, D_MODEL), lambda i: (0, 0)),
            pl.BlockSpec((N_EXPERTS, 1), lambda i: (0, 0)),
        ],
        out_specs=[
            pl.BlockSpec((ROW_TILE, D_MODEL), lambda i: (i, 0)),
            pl.BlockSpec((ROW_TILE * ROW_CHUNKS, LANES), lambda i: (i, 0)),
            pl.BlockSpec((TOP_K, ROW_TILE), lambda i: (0, i)),
            pl.BlockSpec((TOP_K, ROW_TILE), lambda i: (0, i)),
        ],
        out_shape=[
            jax.ShapeDtypeStruct((T_ALL, D_MODEL), F32),
            jax.ShapeDtypeStruct((T_ALL * ROW_CHUNKS, LANES), F32),
            jax.ShapeDtypeStruct((TOP_K, T_ALL), jnp.int32),
            jax.ShapeDtypeStruct((TOP_K, T_ALL), F32),
        ],
        scratch_shapes=[pltpu.VMEM((D_MODEL, D_MODEL), BF16)],
        compiler_params=_cparams(1),
    )(*[a for pair in xs for a in pair], *y_args, ada_l, w,
      ffn_g.reshape(1, D_MODEL), router_w.T, router_b.reshape(N_EXPERTS, 1))
    return y_new, (h, idx_t, wts_t)


def _subhead_norm(x, g2):
    lane = lax.broadcasted_iota(jnp.int32, x.shape, 1)
    first = lane < HD_A
    xx = x * x
    s0 = jnp.sum(jnp.where(first, xx, 0.0), axis=-1, keepdims=True)
    s1 = jnp.sum(jnp.where(first, 0.0, xx), axis=-1, keepdims=True)
    r = jnp.where(first, lax.rsqrt(s0 / HD_A + EPS), lax.rsqrt(s1 / HD_A + EPS))
    return x * r * g2


def _rope(x, cos, sin):
    quarter = HD_A // 4
    lane = lax.broadcasted_iota(jnp.int32, x.shape, 1)
    lower = (lane % (2 * quarter)) < quarter
    swapped = jnp.where(lower, pltpu.roll(x, 2 * HD_A - quarter, 1), pltpu.roll(x, quarter, 1))
    return x * cos + swapped * sin


def _attn_kernel(lam_init, has_ctx, *refs):
    if has_ctx:
        (q_ref, k_ref, v_ref, ck_ref, cv_ref, cq_ref, sq_ref, ckk_ref, skk_ref,
         qg_ref, kg_ref, lp_ref, sg_ref, o_ref, kall_ref, vall_ref) = refs
    else:
        q_ref, k_ref, v_ref, qg_ref, kg_ref, lp_ref, sg_ref, o_ref, kn_ref, vh_ref = refs
    lp = lp_ref[...]
    lam = (jnp.exp(jnp.sum(lp[0:1] * lp[1:2], axis=-1, keepdims=True))
           - jnp.exp(jnp.sum(lp[2:3] * lp[3:4], axis=-1, keepdims=True)) + lam_init)

    def attend(q, k, v):
        probs = []
        for c in range(2):
            qc = q[:, c * HD_A:(c + 1) * HD_A].astype(BF16)
            kc = k[:, c * HD_A:(c + 1) * HD_A].astype(BF16)
            s = lax.dot_general(qc, kc, (((1,), (1,)), ((), ())), preferred_element_type=F32) * (HD_A ** -0.5)
            e = jnp.exp(s - jnp.max(s, axis=-1, keepdims=True))
            probs.append(e / jnp.sum(e, axis=-1, keepdims=True))
        o = _bdot(probs[0] - lam * probs[1], v)
        ms = jnp.mean(o * o, axis=-1, keepdims=True)
        return (o * lax.rsqrt(ms + EPS) * sg_ref[...]) * (1.0 - lam_init)

    if not has_ctx:
        for h in range(H_A):
            cols = slice(h * 2 * HD_A, (h + 1) * 2 * HD_A)
            k = _subhead_norm(k_ref[:, cols], kg_ref[...])
            for c in range(2):
                kn_ref[pl.ds(2 * h + c, SEQ, stride=2 * H_A), :] = k[:, c * HD_A:(c + 1) * HD_A]
            v = v_ref[:, cols]
            vh_ref[pl.ds(h, SEQ, stride=H_A), :] = v
            o_ref[:, cols] = attend(_subhead_norm(q_ref[:, cols], qg_ref[...]), k, v)
        return

    @pl.when(pl.program_id(2) == 0)
    def _():
        kall_ref[0:PAST_LEN, :] = ck_ref[...].astype(BF16)
        vall_ref[0:PAST_LEN, :] = cv_ref[...].astype(BF16)
        k_new = _rope(_subhead_norm(k_ref[...], kg_ref[...]), ckk_ref[...], skk_ref[...])
        kall_ref[PAST_LEN:, :] = k_new.astype(BF16)
        vall_ref[PAST_LEN:, :] = v_ref[...].astype(BF16)

    q = _rope(_subhead_norm(q_ref[...], qg_ref[...]), cq_ref[...], sq_ref[...])
    o_ref[...] = attend(q, kall_ref[...], vall_ref[...])


def _attention_prompt(qkv, qg2, kg2, lam_p, sub_g, lam_init):
    head = 2 * HD_A
    small = [
        pl.BlockSpec((1, head), lambda b: (0, 0)),
        pl.BlockSpec((1, head), lambda b: (0, 0)),
        pl.BlockSpec((4, HD_A), lambda b: (0, 0)),
        pl.BlockSpec((1, head), lambda b: (0, 0)),
    ]
    return pl.pallas_call(
        functools.partial(_attn_kernel, lam_init, False),
        grid=(BATCH,),
        in_specs=[
            pl.BlockSpec((SEQ, W_A), lambda b: (b, 0)),
            pl.BlockSpec((SEQ, W_A), lambda b: (b, 1)),
            pl.BlockSpec((SEQ, W_A), lambda b: (b, 2)),
        ] + small,
        out_specs=[pl.BlockSpec((SEQ, W_A), lambda b: (b, 0)),
                   pl.BlockSpec((SEQ * 2 * H_A, HD_A), lambda b: (b, 0)),
                   pl.BlockSpec((SEQ * H_A, head), lambda b: (b, 0))],
        out_shape=[jax.ShapeDtypeStruct((T_P, W_A), F32), jax.ShapeDtypeStruct((T_P * 2 * H_A, HD_A), F32),
                   jax.ShapeDtypeStruct((T_P * H_A, head), F32)],
        compiler_params=_cparams(1),
    )(qkv, qkv, qkv, qg2, kg2, lam_p, sub_g)


def _attention_sample(qkv, cache_k, cache_v, cos, sin, qg2, kg2, lam_p, sub_g, lam_init):
    nh = H_A
    head = 2 * HD_A
    tq = ATTN_Q_TILE
    nq = DEC_SEQ // tq
    q_off = T_P // tq
    k_off = T_P // DEC_SEQ
    small = [
        pl.BlockSpec((1, head), lambda b, h, i: (0, 0)),
        pl.BlockSpec((1, head), lambda b, h, i: (0, 0)),
        pl.BlockSpec((4, HD_A), lambda b, h, i: (0, 0)),
        pl.BlockSpec((1, head), lambda b, h, i: (0, 0)),
    ]
    return pl.pallas_call(
        functools.partial(_attn_kernel, lam_init, True),
        grid=(DEC_BATCH, nh, nq),
        in_specs=[
            pl.BlockSpec((tq, head), lambda b, h, i: (q_off + b * nq + i, h)),
            pl.BlockSpec((DEC_SEQ, head), lambda b, h, i: (k_off + b, nh + h)),
            pl.BlockSpec((DEC_SEQ, head), lambda b, h, i: (k_off + b, 2 * nh + h)),
            pl.BlockSpec((None, PAST_LEN, head), lambda b, h, i: (b, 0, h)),
            pl.BlockSpec((None, PAST_LEN, head), lambda b, h, i: (b, 0, h)),
            pl.BlockSpec((tq, head), lambda b, h, i: (i, 0)),
            pl.BlockSpec((tq, head), lambda b, h, i: (i, 0)),
            pl.BlockSpec((DEC_SEQ, head), lambda b, h, i: (0, 0)),
            pl.BlockSpec((DEC_SEQ, head), lambda b, h, i: (0, 0)),
        ] + small,
        out_specs=pl.BlockSpec((tq, head), lambda b, h, i: (b * nq + i, h)),
        out_shape=jax.ShapeDtypeStruct((T_S, W_A), F32),
        scratch_shapes=[pltpu.VMEM((PAST_LEN + DEC_SEQ, head), BF16)] * 2,
        compiler_params=_cparams(3),
    )(qkv, qkv, qkv, cache_k, cache_v, cos, sin, cos, sin, qg2, kg2, lam_p, sub_g)


def _rope_tables():
    half = HD_A // 2
    nf = half // 2
    inv = ROPE_BASE ** (-np.arange(nf, dtype=np.float32) / nf)
    pos = np.arange(DEC_SEQ)
    row = (pos // GRID_W).astype(np.float32)
    col = (pos % GRID_W).astype(np.float32)
    ang_r = (row[:, None] * inv).astype(np.float32)
    ang_c = (col[:, None] * inv).astype(np.float32)
    ang = np.concatenate([ang_r, ang_r, ang_c, ang_c], axis=1)
    sign = np.concatenate([-np.ones(nf), np.ones(nf), -np.ones(nf), np.ones(nf)]).astype(np.float32)
    cos = np.cos(ang.astype(np.float64)).astype(np.float32)
    sin = (np.sin(ang.astype(np.float64)) * sign).astype(np.float32)
    return jnp.asarray(np.tile(cos, (1, 2))), jnp.asarray(np.tile(sin, (1, 2)))


def _mlstm_kernel(seq, has_ctx, group, *refs):
    if has_ctx:
        (q_ref, k_ref, cwq_ref, cwk_ref, cbq_ref, cbk_ref, v_ref, mo_ref, gi_ref, gf_ref,
         gbi_ref, gbf_ref, hn_ref, c0_ref, n0_ref, m0_ref, o_ref,
         qs_ref, ks_ref, hf_ref, hb_ref, cs_ref, rrow_ref, col_ref, wc_ref) = refs
    else:
        (q_ref, k_ref, cwq_ref, cwk_ref, cbq_ref, cbk_ref, v_ref, mo_ref, gi_ref, gf_ref,
         gbi_ref, gbf_ref, hn_ref, o_ref, c_out_ref, n_out_ref, m_out_ref,
         qs_ref, ks_ref, hf_ref, hb_ref, cs_ref, rrow_ref, col_ref, wc_ref) = refs
    nc = seq // CHUNK
    n_chain = 2 * H_B
    chains = [(sub, d, h) for sub in range(group) for d in range(2) for h in range(H_B)]
    qs_ref[...] = _silu(_dwconv3(q_ref[...], cwq_ref[...], cbq_ref[...], seq)) * (HD_B ** -0.5)
    ks_ref[...] = _silu(_dwconv3(k_ref[...], cwk_ref[...], cbk_ref[...], seq))

    rows = nc * n_chain
    lane = lax.broadcasted_iota(jnp.int32, (rows, 2 * CHUNK), 1)
    forward = lax.broadcasted_iota(jnp.int32, (rows, 2 * CHUNK), 0) % n_chain < H_B
    valid = lane < CHUNK

    def scan(x, op, fill):
        pre, suf = x, x
        sh = 1
        while sh < CHUNK:
            pre = op(pre, jnp.where(lane >= sh, pltpu.roll(pre, sh, 1), fill))
            suf = op(suf, jnp.where(lane + sh < CHUNK, pltpu.roll(suf, 2 * CHUNK - sh, 1), fill))
            sh *= 2
        return jnp.where(forward, pre, suf)

    mm_final = []
    for sub in range(group):
        gate_i = (gi_ref[sub] + gbi_ref[...]).reshape(rows, 2 * CHUNK)
        lf = jnp.where(valid, _log_sigmoid(gf_ref[sub] + gbf_ref[...]).reshape(rows, 2 * CHUNK), 0.0)
        b = scan(lf, jnp.add, 0.0)
        cmax = scan(jnp.where(valid, gate_i - b, -jnp.inf), jnp.maximum, -jnp.inf)
        b_last = jnp.sum(lf, axis=1, keepdims=True)
        g = b_last - b + gate_i
        g_max = jnp.max(jnp.where(valid, g, -jnp.inf), axis=1, keepdims=True)
        mm = m0_ref[sub] if has_ctx else jnp.zeros((n_chain, 1), F32)
        mm_seq = []
        for p in range(nc):
            mm_seq.append(mm)
            seg = slice(p * n_chain, (p + 1) * n_chain)
            mm = jnp.maximum(b_last[seg] + mm, g_max[seg])
        mm_final.append(mm)
        mm_prev = jnp.concatenate(mm_seq, axis=0)
        mm_next = jnp.concatenate(mm_seq[1:] + [mm], axis=0)
        m_t = jnp.maximum(b + mm_prev, b + cmax)
        rrow_ref[sub] = (b - gate_i).reshape(nc, n_chain, 2 * CHUNK)
        wc_ref[sub] = jnp.exp(b_last + mm_prev - mm_next).reshape(nc, n_chain, 1)
        per_row = [b, m_t, jnp.exp(b + mm_prev - m_t), jnp.exp(-m_t), jnp.exp(g - mm_next)]
        for j, arr in enumerate(per_row):
            by_time = arr.T
            for p in range(nc):
                col_ref[sub, p, :, j * n_chain:(j + 1) * n_chain] = by_time[0:CHUNK, p * n_chain:(p + 1) * n_chain]

    t_idx = lax.broadcasted_iota(jnp.int32, (CHUNK, CHUNK), 0)
    s_idx = lax.broadcasted_iota(jnp.int32, (CHUNK, CHUNK), 1)
    for n, (sub, d, h) in enumerate(chains):
        cs_ref[n] = c0_ref[sub, d, h] if has_ctx else jnp.zeros((HD_B, HD_B), F32)

    def out_step(p, n_states):
        new_states = []
        for n, (sub, d, h) in enumerate(chains):
            cols = col_ref[sub, p]
            rrows = rrow_ref[sub, p]
            wcs = wc_ref[sub, p]
            n_loc = d * H_B + h
            c = p if d == 0 else nc - 1 - p
            r0 = pl.multiple_of(sub * seq + c * CHUNK, CHUNK)
            hcols = slice(h * HD_B, (h + 1) * HD_B)
            qt = qs_ref[pl.ds(r0, CHUNK), hcols]
            kt = ks_ref[pl.ds(r0, CHUNK), hcols]
            vt = v_ref[pl.ds(r0, CHUNK), hcols]
            b_col, m_t, w_inter, e_inv, w_k = (cols[:, j * n_chain + n_loc:j * n_chain + n_loc + 1] for j in range(5))
            mask = (s_idx <= t_idx) if d == 0 else (s_idx >= t_idx)
            decay = jnp.exp(jnp.where(mask, b_col - rrows[n_loc:n_loc + 1, 0:CHUNK], NEG) - m_t)
            qk = lax.dot_general(qt.astype(BF16), kt.astype(BF16), (((1,), (1,)), ((), ())),
                                 preferred_element_type=F32)
            s = qk * decay
            cm = cs_ref[n]
            nm = n_states[n]
            cq = lax.dot_general(qt.astype(BF16), cm.astype(BF16), (((1,), (1,)), ((), ())),
                                 preferred_element_type=F32)
            num = _bdot(s, vt) + w_inter * cq
            nq = jnp.sum(s, axis=-1, keepdims=True) + w_inter * jnp.sum(qt * nm, axis=-1, keepdims=True)
            hdir_ref = hf_ref if d == 0 else hb_ref
            hdir_ref[pl.ds(r0, CHUNK), hcols] = num / jnp.maximum(jnp.abs(nq), e_inv)
            w_c = wcs[n_loc:n_loc + 1, :]
            vw = (vt * w_k).astype(BF16)
            cs_ref[n] = w_c * cm + lax.dot_general(vw, kt.astype(BF16), (((0,), (0,)), ((), ())),
                                                   preferred_element_type=F32)
            new_states.append(w_c * nm + jnp.sum(kt * w_k, axis=0, keepdims=True))
        return tuple(new_states)

    if has_ctx:
        n_init = tuple(n0_ref[sub, d, h] for sub, d, h in chains)
    else:
        n_init = tuple(jnp.zeros((1, HD_B), F32) for _ in chains)
    n_final = lax.fori_loop(0, nc, out_step, n_init)
    if not has_ctx:
        for n, (sub, d, h) in enumerate(chains):
            n_loc = d * H_B + h
            c_out_ref[sub, d, h] = cs_ref[n]
            n_out_ref[sub, d, h] = n_final[n]
            m_out_ref[sub, d, h] = jnp.broadcast_to(mm_final[sub][n_loc:n_loc + 1, :], (1, HD_B))

    for h in range(H_B):
        hcols = slice(h * HD_B, (h + 1) * HD_B)
        hh = hf_ref[:, hcols] + hb_ref[:, hcols]
        ms = jnp.mean(hh * hh, axis=-1, keepdims=True)
        o_ref[:, hcols] = (hh * lax.rsqrt(ms + EPS) * hn_ref[:, hcols]) * _sigmoid(mo_ref[:, hcols])


def _mlstm(mqk, mv, mo, mg_stream, conv_w, conv_b, gate_b, hn_g, *, seq, nbatch, row_off, group, ctx=None):
    nh = H_B
    nc = seq // CHUNK
    has_ctx = ctx is not None
    assert nbatch % group == 0
    gt = mg_stream.reshape(nbatch, nc, CHUNK, 2, 2, nh).transpose(0, 1, 3, 4, 5, 2)
    pad = ((0, 0), (0, 0), (0, 0), (0, CHUNK))
    gates = [jnp.pad(jnp.concatenate([gt[:, :, 0, j], gt[:, ::-1, 1, j]], axis=2), pad) for j in range(2)]
    gate_bias = [jnp.concatenate([gate_b[0, j], gate_b[1, j]]).reshape(2 * nh, 1) for j in range(2)]
    blk = lambda col: pl.BlockSpec((group * seq, W_B), lambda b, col=col: (row_off + b, col))
    gate_blk = pl.BlockSpec((group, nc, 2 * nh, 2 * CHUNK), lambda b: (b, 0, 0, 0))
    in_specs = [
        blk(0), blk(1),
        pl.BlockSpec((3, W_B), lambda b: (0, 0)),
        pl.BlockSpec((3, W_B), lambda b: (0, 1)),
        pl.BlockSpec((1, W_B), lambda b: (0, 0)),
        pl.BlockSpec((1, W_B), lambda b: (0, 1)),
        blk(0), blk(0),
        gate_blk, gate_blk,
        pl.BlockSpec((2 * nh, 1), lambda b: (0, 0)),
        pl.BlockSpec((2 * nh, 1), lambda b: (0, 0)),
        pl.BlockSpec((1, W_B), lambda b: (0, 0)),
    ]
    args = [mqk, mqk, conv_w, conv_w, conv_b, conv_b, mv, mo, gates[0], gates[1],
            gate_bias[0], gate_bias[1], hn_g.reshape(1, W_B)]
    o_spec = pl.BlockSpec((group * seq, W_B), lambda b: (b, 0))
    o_shape = jax.ShapeDtypeStruct((nbatch * seq, W_B), F32)
    state_blk = lambda rows: pl.BlockSpec((group, 2, nh, rows, HD_B), lambda b: (b, 0, 0, 0, 0))
    if has_ctx:
        c0, n0, m0 = ctx
        in_specs += [state_blk(HD_B), state_blk(1), pl.BlockSpec((group, 2 * nh, 1), lambda b: (b, 0, 0))]
        args += [c0, n0.reshape(nbatch, 2, nh, 1, HD_B), m0.reshape(nbatch, 2 * nh, 1)]
        out_specs, out_shape = o_spec, o_shape
    else:
        out_specs = [o_spec, state_blk(HD_B), state_blk(1), state_blk(1)]
        out_shape = [
            o_shape,
            jax.ShapeDtypeStruct((nbatch, 2, nh, HD_B, HD_B), F32),
            jax.ShapeDtypeStruct((nbatch, 2, nh, 1, HD_B), F32),
            jax.ShapeDtypeStruct((nbatch, 2, nh, 1, HD_B), F32),
        ]
    return pl.pallas_call(
        functools.partial(_mlstm_kernel, seq, has_ctx, group),
        grid=(nbatch // group,),
        in_specs=in_specs,
        out_specs=out_specs,
        out_shape=out_shape,
        scratch_shapes=[pltpu.VMEM((group * seq, W_B), F32)] * 4 + [
            pltpu.VMEM((group * 2 * nh, HD_B, HD_B), F32),
            pltpu.VMEM((group, nc, 2 * nh, 2 * CHUNK), F32),
            pltpu.VMEM((group, nc, CHUNK, 5 * 2 * nh), F32),
            pltpu.VMEM((group, nc, 2 * nh, 1), F32),
        ],
        compiler_params=_cparams(1),
    )(*args)


def _dft_mats(L):
    f = np.arange(L)[:, None]
    j = np.arange(L)[None, :]
    ang = 2.0 * np.pi * ((f * j) % (2 * L)) / (2 * L)
    cm = np.cos(ang)
    sm = np.sin(ang)
    alt = (1.0 - 2.0 * (np.arange(L) % 2))
    fwd_b = -sm
    fwd_b[0, :] = alt
    fwd = np.concatenate([cm, fwd_b], axis=0)
    wgt = np.where(np.arange(L) == 0, 1.0, 2.0)[None, :]
    inv_a = cm.T * wgt
    inv_b = -2.0 * sm.T
    inv_b[:, 0] = alt
    inv = np.concatenate([inv_a, inv_b], axis=1) / (2 * L)
    return jnp.asarray(fwd.astype(np.float32)), jnp.asarray(inv.astype(np.float32))


def _hyena_feats(L):
    t = np.linspace(0.0, 1.0, L, dtype=np.float32)
    wpos = (2.0 * math.pi * np.arange(L, dtype=np.float32) / L).astype(np.float32)
    fb = np.linspace(1e-4, HY_BANDS - 1, HY_BANDS, dtype=np.float32)
    z = (wpos[:, None] * fb).astype(np.float32)
    feats = np.concatenate([t[:, None], np.cos(z), -np.sin(z)], axis=-1).astype(np.float32)
    deltas = np.abs(np.linspace(math.log(HY_TARGET) / HY_SLOW_PCT, math.log(HY_TARGET) / HY_FAST_PCT,
                                D_MODEL, dtype=np.float32))
    decay = np.exp(-t[:, None] * deltas).astype(np.float32)
    return jnp.asarray(feats), jnp.asarray(decay)


def _filter_kernel(L, feats_ref, w1_ref, b1_ref, fr1_ref, w2_ref, b2_ref, fr2_ref, w3f_ref, w3b_ref,
                   decay_ref, fwd_ref, o_ref, hdn_ref):
    @pl.when((pl.program_id(0) == 0) & (pl.program_id(1) == 0))
    def _():
        h1 = jnp.sin(fr1_ref[...] * (jnp.dot(feats_ref[...], w1_ref[...], precision=HIGHEST,
                                             preferred_element_type=F32) + b1_ref[...]))
        hdn_ref[...] = jnp.sin(fr2_ref[...] * (jnp.dot(h1, w2_ref[...], precision=HIGHEST,
                                                       preferred_element_type=F32) + b2_ref[...]))

    hdn = hdn_ref[...]
    decay = decay_ref[...]
    f_fwd = jnp.dot(hdn, w3f_ref[...], precision=HIGHEST, preferred_element_type=F32) * decay
    f_bwd = jnp.dot(hdn, w3b_ref[...], precision=HIGHEST, preferred_element_type=F32) * decay
    row = lax.broadcasted_iota(jnp.int32, f_bwd.shape, 0)
    f_bwd = jnp.where(row == 0, 0.0, f_bwd)
    f_sum = f_fwd + f_bwd
    o_ref[0:L, :] = _bdot(fwd_ref[0:L, :], f_sum)
    imag = _bdot(fwd_ref[L:2 * L, :], f_fwd - f_bwd)
    nyquist = _bdot(fwd_ref[L:L + 16, :], f_sum)[0:1]
    o_ref[L:2 * L, :] = jnp.where(row == 0, nyquist, imag)


def _hyena_filter_spectrum(L, fwd_bf, w1, b1, fr1, w2, b2, fr2, w3):
    feats, decay = _hyena_feats(L)
    td = 512
    nd = D_MODEL // td
    emb = feats.shape[1]
    vec = lambda a: a.reshape(1, HY_FH)
    full = lambda shape: pl.BlockSpec(shape, lambda o, j: (0, 0))
    return pl.pallas_call(
        functools.partial(_filter_kernel, L),
        grid=(HY_ORDER, nd),
        in_specs=[
            full((L, emb)), full((emb, HY_FH)), full((1, HY_FH)), full((1, HY_FH)),
            full((HY_FH, HY_FH)), full((1, HY_FH)), full((1, HY_FH)),
            pl.BlockSpec((HY_FH, td), lambda o, j: (0, o * 2 * nd + j)),
            pl.BlockSpec((HY_FH, td), lambda o, j: (0, o * 2 * nd + nd + j)),
            pl.BlockSpec((L, td), lambda o, j: (0, j)),
            full((2 * L, L)),
        ],
        out_specs=pl.BlockSpec((2 * L, td), lambda o, j: (0, o * nd + j)),
        out_shape=jax.ShapeDtypeStruct((2 * L, HY_ORDER * D_MODEL), F32),
        scratch_shapes=[pltpu.VMEM((L, HY_FH), F32)],
        compiler_params=_cparams(2),
    )(feats, w1, vec(b1), vec(fr1), w2, vec(b2), vec(fr2), w3, w3, decay, fwd_bf)


def _spectral_conv(u, fwd, inv, kspec, L):
    uf = jnp.dot(fwd, u.astype(BF16), preferred_element_type=F32)
    ua, ub = uf[0:L], uf[L:2 * L]
    ka, kb = kspec[0:L], kspec[L:2 * L]
    first = lax.broadcasted_iota(jnp.int32, ua.shape, 0) == 0
    ya = ua * ka - jnp.where(first, 0.0, ub * kb)
    yb = jnp.where(first, ub * kb, ua * kb + ub * ka)
    y = jnp.concatenate([ya, yb], axis=0).astype(BF16)
    return jnp.dot(inv, y, preferred_element_type=F32)


def _hyena_kernel(L, zv_ref, z1_ref, z2_ref, cwv_ref, cw1_ref, cw2_ref, cbv_ref, cb1_ref, cb2_ref,
                  fwd_ref, inv_ref, k0_ref, k1_ref, bias0_ref, bias1_ref, o_ref):
    fwd = fwd_ref[...]
    inv = inv_ref[...]
    v = _dwconv3(zv_ref[...], cwv_ref[...], cbv_ref[...])
    x1 = _dwconv3(z1_ref[...], cw1_ref[...], cb1_ref[...])
    x2 = _dwconv3(z2_ref[...], cw2_ref[...], cb2_ref[...])
    z = x1 * (_spectral_conv(v, fwd, inv, k0_ref[...], L) + v * bias0_ref[...])
    o_ref[...] = x2 * (_spectral_conv(z, fwd, inv, k1_ref[...], L) + z * bias1_ref[...])


def _hyena_core(zproj, conv_w, conv_b, fwd_bf, inv_bf, kspec, bias, *, seq, nbatch, row_off, td):
    nd = D_MODEL // td
    zblk = lambda part: pl.BlockSpec((seq, td), lambda b, j, part=part: (row_off + b, part * nd + j))
    cwblk = lambda part: pl.BlockSpec((3, td), lambda b, j, part=part: (0, part * nd + j))
    cbblk = lambda part: pl.BlockSpec((1, td), lambda b, j, part=part: (0, part * nd + j))
    return pl.pallas_call(
        functools.partial(_hyena_kernel, seq),
        grid=(nbatch, nd),
        in_specs=[
            zblk(0), zblk(1), zblk(2), cwblk(0), cwblk(1), cwblk(2), cbblk(0), cbblk(1), cbblk(2),
            pl.BlockSpec((2 * seq, seq), lambda b, j: (0, 0), pipeline_mode=pl.Buffered(1)),
            pl.BlockSpec((seq, 2 * seq), lambda b, j: (0, 0), pipeline_mode=pl.Buffered(1)),
            pl.BlockSpec((2 * seq, td), lambda b, j: (0, j)),
            pl.BlockSpec((2 * seq, td), lambda b, j: (0, nd + j)),
            pl.BlockSpec((None, 1, td), lambda b, j: (0, 0, j)),
            pl.BlockSpec((None, 1, td), lambda b, j: (1, 0, j)),
        ],
        out_specs=pl.BlockSpec((seq, td), lambda b, j: (b, j)),
        out_shape=jax.ShapeDtypeStruct((nbatch * seq, D_MODEL), F32),
        compiler_params=_cparams(2),
    )(zproj, zproj, zproj, conv_w, conv_w, conv_w, conv_b, conv_b, conv_b,
      fwd_bf, inv_bf, kspec, kspec, bias.reshape(HY_ORDER, 1, D_MODEL), bias.reshape(HY_ORDER, 1, D_MODEL))


def _route_tile(y_tile, row, ada_ref, g_ref, rw_ref, rb_ref, h_ref, idx_ref, wt_ref):
    h = _modulate(y_tile, g_ref[...], _ada_chunk(ada_ref, row, 3), _ada_chunk(ada_ref, row, 4))
    for j in range(ROW_CHUNKS):
        h_ref[pl.ds(j, ROW_TILE, stride=ROW_CHUNKS), :] = h[:, j * LANES:(j + 1) * LANES]
    logits = lax.dot_general(rw_ref[...], h, (((1,), (1,)), ((), ())), precision=HIGHEST,
                             preferred_element_type=F32) + rb_ref[...]
    expert = lax.broadcasted_iota(jnp.int32, logits.shape, 0)
    slot = lax.broadcasted_iota(jnp.int32, (TOP_K, logits.shape[1]), 0)
    vals = jnp.zeros((TOP_K, logits.shape[1]), F32)
    idxs = jnp.zeros((TOP_K, logits.shape[1]), jnp.int32)
    cur = logits
    for k in range(TOP_K):
        m = jnp.max(cur, axis=0, keepdims=True)
        a = jnp.min(jnp.where(cur == m, expert, N_EXPERTS), axis=0, keepdims=True)
        vals = jnp.where(slot == k, m, vals)
        idxs = jnp.where(slot == k, a, idxs)
        cur = jnp.where(expert == a, -jnp.inf, cur)
    e = jnp.exp(vals - vals[0:1])
    wt_ref[...] = e / jnp.sum(e, axis=0, keepdims=True)
    idx_ref[...] = idxs


def _dispatch_kernel(rows_ref, h_ref, o_ref):
    base = pl.program_id(0) * GATHER_ROWS
    for r in range(GATHER_ROWS):
        t = pl.multiple_of(rows_ref[base + r] * ROW_CHUNKS, ROW_CHUNKS)
        o_ref[r * ROW_CHUNKS:(r + 1) * ROW_CHUNKS, :] = h_ref[pl.ds(t, ROW_CHUNKS), :]


def _dispatch(h_tiles, gather_row):
    grid_spec = pltpu.PrefetchScalarGridSpec(
        num_scalar_prefetch=1,
        grid=(X_ROWS // GATHER_ROWS,),
        in_specs=[pl.BlockSpec((T_ALL * ROW_CHUNKS, LANES), lambda i, rows: (0, 0), pipeline_mode=pl.Buffered(1))],
        out_specs=pl.BlockSpec((GATHER_ROWS * ROW_CHUNKS, LANES), lambda i, rows: (i, 0)),
    )
    return pl.pallas_call(
        _dispatch_kernel,
        grid_spec=grid_spec,
        out_shape=jax.ShapeDtypeStruct((X_ROWS * ROW_CHUNKS, LANES), F32),
        compiler_params=_cparams(1),
    )(gather_row, h_tiles)


def _deinterleave_matrix():
    s = np.zeros((256, 256), np.float32)
    j = np.arange(128)
    s[2 * j, j] = 1.0
    s[2 * j + 1, 128 + j] = 1.0
    return jnp.asarray(s)


def _weight_copies(layer, e, w1_hbm, w2_hbm, w1s_ref, w2s_ref, sem):
    copies = []
    r1 = D_MODEL // W1_DMA_CHUNKS
    for c in range(W1_DMA_CHUNKS):
        copies.append(pltpu.make_async_copy(w1_hbm.at[layer, e, pl.ds(c * r1, r1)],
                                            w1s_ref.at[pl.ds(c * r1, r1)], sem.at[c]))
    r2 = D_FF // W2_DMA_CHUNKS
    for c in range(W2_DMA_CHUNKS):
        copies.append(pltpu.make_async_copy(w2_hbm.at[layer, e, pl.ds(c * r2, r2)],
                                            w2s_ref.at[pl.ds(c * r2, r2)], sem.at[W1_DMA_CHUNKS + c]))
    return copies


def _expert_kernel(layer, te_ref, tf_ref, ne_ref, nu_ref, src_ref, xo_ref, vr_ref, x_ref, b1_ref, b2_ref, wt_ref, s_ref,
                   w1_hbm, w2_hbm, o_hbm, w1s_ref, w2s_ref, w1p_ref, w2p_ref, acc_ref, out_ref, wsem, osem):
    i = pl.program_id(0)
    half = 128
    copies = functools.partial(_weight_copies, layer, w1_hbm=w1_hbm, w2_hbm=w2_hbm,
                               w1s_ref=w1s_ref, w2s_ref=w2s_ref, sem=wsem)

    @pl.when(i == 0)
    def _():
        acc_ref[...] = jnp.zeros_like(acc_ref)
        out_ref[...] = jnp.zeros_like(out_ref)
        for cp in copies(te_ref[0]):
            cp.start()

    @pl.when(tf_ref[i] == 1)
    def _():
        for cp in copies(te_ref[i]):
            cp.wait()
        s = s_ref[...].astype(BF16)
        for c in range(2 * D_FF // 256):
            blk = jnp.dot(w1s_ref[:, c * 256:(c + 1) * 256].astype(BF16), s, preferred_element_type=F32)
            w1p_ref[:, c * half:(c + 1) * half] = blk[:, :half].astype(BF16)
            w1p_ref[:, D_FF + c * half:D_FF + (c + 1) * half] = blk[:, half:].astype(BF16)
        w2p_ref[...] = w2s_ref[...].astype(BF16)

        @pl.when(ne_ref[i] >= 0)
        def _():
            for cp in copies(ne_ref[i]):
                cp.start()

    def tile_step(m):
        base = i * MOE_TILE
        prev = (i + 1) % 2
        for r0 in range(0, MOE_TILE, SCATTER_GROUP):
            toks = [pl.multiple_of(src_ref[base + r0 + g] * ROW_CHUNKS, ROW_CHUNKS) for g in range(SCATTER_GROUP)]
            cur = [acc_ref[pl.ds(toks[g], ROW_CHUNKS), :] for g in range(SCATTER_GROUP)]
            add = [out_ref[prev, (r0 + g) * ROW_CHUNKS:(r0 + g + 1) * ROW_CHUNKS, :] for g in range(SCATTER_GROUP)]
            for g in range(SCATTER_GROUP):
                acc_ref[pl.ds(toks[g], ROW_CHUNKS), :] = cur[g] + add[g]
        a = jnp.dot(_token_rows(x_ref, m).astype(BF16), w1p_ref[...], preferred_element_type=F32) + b1_ref[...]
        glu = jnp.minimum(a[:, :D_FF], SWIGLU_LIMIT)
        lin = jnp.clip(a[:, D_FF:], -SWIGLU_LIMIT, SWIGLU_LIMIT)
        hid = glu * _sigmoid(SWIGLU_ALPHA * glu) * (lin + 1.0)
        out = (jnp.dot(hid.astype(BF16), w2p_ref[...], preferred_element_type=F32) + b2_ref[...]) * wt_ref[0:m, :]
        cur_buf = i % 2
        for j in range(ROW_CHUNKS):
            out_ref[cur_buf, pl.ds(j, m, stride=ROW_CHUNKS), :] = out[:, j * LANES:(j + 1) * LANES]

    live = i <= nu_ref[0]
    pl.when(live & (vr_ref[i] > MOE_TILE // 2))(functools.partial(tile_step, MOE_TILE))
    pl.when(live & (vr_ref[i] <= MOE_TILE // 2))(functools.partial(tile_step, MOE_TILE // 2))

    @pl.when(i == pl.num_programs(0) - 1)
    def _():
        cp = pltpu.make_async_copy(acc_ref.at[pl.ds(0, T_ALL * ROW_CHUNKS)], o_hbm, osem)
        cp.start()
        cp.wait()


def _experts(layer, x_sorted, w_sorted, plan, w1, b1p, w2, b2):
    tile_expert, tile_first, next_expert, n_used, src, x_off, valid_rows = plan
    grid_spec = pltpu.PrefetchScalarGridSpec(
        num_scalar_prefetch=7,
        grid=(MOE_TILES,),
        in_specs=[
            pl.BlockSpec((pl.Element(MOE_TILE * ROW_CHUNKS), pl.Element(LANES)),
                         lambda i, te, tf, ne, nu, src, xo, vr: (pl.multiple_of(xo[i] * ROW_CHUNKS, X_ALIGN * ROW_CHUNKS), 0)),
            pl.BlockSpec((None, None, 1, 2 * D_FF), lambda i, te, *_: (layer, te[i], 0, 0)),
            pl.BlockSpec((None, None, 1, D_MODEL), lambda i, te, *_: (layer, te[i], 0, 0)),
            pl.BlockSpec((MOE_TILE, 1), lambda i, te, *_: (i, 0)),
            pl.BlockSpec((256, 256), lambda i, te, *_: (0, 0)),
            pl.BlockSpec(memory_space=pl.ANY),
            pl.BlockSpec(memory_space=pl.ANY),
        ],
        out_specs=pl.BlockSpec(memory_space=pl.ANY),
        scratch_shapes=[
            pltpu.VMEM((D_MODEL, 2 * D_FF), F32),
            pltpu.VMEM((D_FF, D_MODEL), F32),
            pltpu.VMEM((D_MODEL, 2 * D_FF), BF16),
            pltpu.VMEM((D_FF, D_MODEL), BF16),
            pltpu.VMEM((ACC_ROWS * ROW_CHUNKS, LANES), F32),
            pltpu.VMEM((2, MOE_TILE * ROW_CHUNKS, LANES), F32),
            pltpu.SemaphoreType.DMA((W1_DMA_CHUNKS + W2_DMA_CHUNKS,)),
            pltpu.SemaphoreType.DMA(()),
        ],
    )
    return pl.pallas_call(
        functools.partial(_expert_kernel, layer),
        grid_spec=grid_spec,
        out_shape=jax.ShapeDtypeStruct((T_ALL * ROW_CHUNKS, LANES), F32),
        compiler_params=_cparams(1),
    )(tile_expert, tile_first, next_expert, n_used, src, x_off, valid_rows, x_sorted, b1p, b2, w_sorted,
      _deinterleave_matrix(), w1, w2)


def _combine_kernel(first_tile, y_ref, a_ref, ada_ref, o_ref):
    gate = _ada_chunk(ada_ref, _cond_row(first_tile + pl.program_id(0)), 5)
    o_ref[...] = y_ref[...] + gate * _token_rows(a_ref, ROW_TILE)


def _combine(y, acc, ada_l, first_tile=0, n_tiles=N_ROW_TILES):
    return pl.pallas_call(
        functools.partial(_combine_kernel, first_tile),
        grid=(n_tiles,),
        in_specs=[
            pl.BlockSpec((ROW_TILE, D_MODEL), lambda i: (first_tile + i, 0)),
            pl.BlockSpec((ROW_TILE * ROW_CHUNKS, LANES), lambda i: (first_tile + i, 0)),
            pl.BlockSpec((COND_ROWS, ADA_CHUNKS * D_MODEL), lambda i: (0, 0)),
        ],
        out_specs=pl.BlockSpec((ROW_TILE, D_MODEL), lambda i: (i, 0)),
        out_shape=jax.ShapeDtypeStruct((n_tiles * ROW_TILE, D_MODEL), F32),
        compiler_params=_cparams(1),
    )(y, acc, ada_l)


def _routing_plan(idx, wts):
    eid = idx.reshape(-1)
    order = jnp.argsort(eid, stable=True).astype(jnp.int32)
    experts = jnp.arange(N_EXPERTS, dtype=jnp.int32)
    counts = jnp.sum(eid[:, None] == experts[None, :], axis=0).astype(jnp.int32)
    ntiles = (counts + MOE_TILE - 1) // MOE_TILE
    tile_end = jnp.cumsum(ntiles).astype(jnp.int32)
    tile_begin = tile_end - ntiles
    cstarts = (jnp.cumsum(counts) - counts).astype(jnp.int32)
    n_used = tile_end[-1]
    tile = jnp.arange(MOE_TILES, dtype=jnp.int32)
    te = jnp.minimum(jnp.sum(tile[:, None] >= tile_end[None, :], axis=1), N_EXPERTS - 1).astype(jnp.int32)
    used = tile < n_used
    prev = jnp.concatenate([jnp.full((1,), -1, jnp.int32), te[:-1]])
    first = (te != prev) & used

    def pick(onehot, table):
        return jnp.sum(jnp.where(onehot, table[None, :], 0), axis=1).astype(jnp.int32)

    tile_is = te[:, None] == experts[None, :]
    later = (experts[None, :] > experts[:, None]) & (ntiles[None, :] > 0)
    following = jnp.min(jnp.where(later, experts[None, :], N_EXPERTS), axis=1)
    following = jnp.where(following < N_EXPERTS, following, -1)
    next_expert = jnp.where(first, pick(tile_is, following), -1).astype(jnp.int32)
    tile_in_expert = tile - pick(tile_is, tile_begin)
    off = tile_in_expert[:, None] * MOE_TILE + jnp.arange(MOE_TILE, dtype=jnp.int32)[None, :]
    valid_rows = jnp.where(used, jnp.clip(pick(tile_is, counts) - tile_in_expert * MOE_TILE, 0, MOE_TILE), 0)
    valid = jnp.arange(MOE_TILE, dtype=jnp.int32)[None, :] < valid_rows[:, None]
    assign = order[jnp.clip(pick(tile_is, cstarts)[:, None] + off, 0, N_ASSIGN - 1)]
    token = assign // TOP_K
    src = jnp.where(valid, token, SPARE_ROW).reshape(MOE_ROWS).astype(jnp.int32)
    src = jnp.concatenate([jnp.full((MOE_TILE,), SPARE_ROW, jnp.int32), src])
    w_sorted = jnp.where(valid, wts.reshape(-1)[assign], 0.0).reshape(MOE_ROWS, 1)
    seg = ((counts + X_ALIGN - 1) // X_ALIGN) * X_ALIGN
    seg_end = jnp.cumsum(seg).astype(jnp.int32)
    seg_begin = seg_end - seg
    x_off = jnp.where(used, pick(tile_is, seg_begin) + tile_in_expert * MOE_TILE, 0).astype(jnp.int32)
    group = jnp.arange(X_ROWS // X_ALIGN, dtype=jnp.int32) * X_ALIGN
    group_is = (group[:, None] >= seg_begin[None, :]) & (group[:, None] < seg_end[None, :])
    xoffset = (group - pick(group_is, seg_begin))[:, None] + jnp.arange(X_ALIGN, dtype=jnp.int32)[None, :]
    xassign = order[jnp.clip(pick(group_is, cstarts)[:, None] + xoffset, 0, N_ASSIGN - 1)]
    gather_row = jnp.where(xoffset < pick(group_is, counts)[:, None], xassign // TOP_K, 0).reshape(X_ROWS)
    plan = (te, first.astype(jnp.int32), next_expert, n_used.reshape(1), src, x_off, valid_rows.astype(jnp.int32))
    return plan, gather_row, w_sorted


def _moe(layer, y, routed, ada_l, w1, b1p, w2, b2):
    h, idx_t, wts_t = routed
    plan, gather_row, w_sorted = _routing_plan(idx_t.T, wts_t.T)
    x_sorted = _dispatch(h, gather_row)
    acc = _experts(layer, x_sorted, w_sorted, plan, w1, b1p, w2, b2)
    if layer == DEPTH - 1:
        return (_combine(y, acc, ada_l, 0, P_TILES), _combine(y, acc, ada_l, P_TILES, N_ROW_TILES - P_TILES)), None
    return y, (acc, ada_l)


def kernel(x_prompt, x_sample, cache_attn_k, cache_attn_v, state_mlstm_C, state_mlstm_n, state_mlstm_m, c, c_ctx, ada_w, ada_b, norm_mix_g, norm_ffn_g, ab_w_in, ab_w_out, da_qnorm_g, da_knorm_g, da_lambda, da_subnorm_g, ml_conv_w, ml_conv_b, ml_gate_b, ml_headnorm_g, hy_w_in, hy_w_out, hy_conv_w, hy_conv_b, hy_f_w1, hy_f_b1, hy_f_freq1, hy_f_w2, hy_f_b2, hy_f_freq2, hy_f_w3, hy_bias, router_w, router_b, moe_w1, moe_b1, moe_w2, moe_b2):
    y = (x_prompt.reshape(T_P, D_MODEL), x_sample.reshape(T_S, D_MODEL))
    cond = jnp.concatenate([c_ctx[None, :], c, jnp.zeros((COND_ROWS - 1 - DEC_BATCH, D_MODEL), F32)], axis=0)
    ada = _ada_table(cond, ada_w, ada_b)
    b1p = moe_b1.reshape(DEPTH, N_EXPERTS, D_FF, 2).swapaxes(2, 3).reshape(DEPTH, N_EXPERTS, 1, 2 * D_FF)
    b2r = moe_b2.reshape(DEPTH, N_EXPERTS, 1, D_MODEL)
    new_k, new_v, new_c, new_n, new_m = [], [], [], [], []
    pending = None
    for layer in range(DEPTH):
        ada_l = ada[layer]
        route_prm = (norm_ffn_g[layer], router_w[layer], router_b[layer])
        if layer % 2 == 0:
            e = layer // 2
            lam_init = 0.8 - 0.6 * math.exp(-0.3 * layer)
            proj = _modulated_proj(y, ada_l, norm_mix_g[layer], ab_w_in[e],
                                   (3 * W_A, 2 * W_B, W_B, W_B, 4 * H_B), pending)
            if pending is not None:
                y = proj[-1]
            qkv, mqk, mv, mo, mg = proj[:5]
            qg2 = jnp.tile(da_qnorm_g[e], 2).reshape(1, 2 * HD_A)
            kg2 = jnp.tile(da_knorm_g[e], 2).reshape(1, 2 * HD_A)
            sub_g = da_subnorm_g[e].reshape(1, 2 * HD_A)
            oa_p, k_norm, v_heads = _attention_prompt(qkv, qg2, kg2, da_lambda[e], sub_g, lam_init)
            cos, sin = _rope_tables()
            oa_s = _attention_sample(
                qkv, cache_attn_k[:, e].reshape(DEC_BATCH, PAST_LEN, W_A),
                cache_attn_v[:, e].reshape(DEC_BATCH, PAST_LEN, W_A), cos, sin,
                qg2, kg2, da_lambda[e], sub_g, lam_init)
            ob_p, c_new, n_new, m_new = _mlstm(
                mqk, mv, mo, mg[:T_P], ml_conv_w[e], ml_conv_b[e].reshape(1, 2 * W_B), ml_gate_b[e],
                ml_headnorm_g[e], seq=SEQ, nbatch=BATCH, row_off=0, group=MLSTM_GROUP)
            ob_s = _mlstm(
                mqk, mv, mo, mg[T_P:], ml_conv_w[e], ml_conv_b[e].reshape(1, 2 * W_B), ml_gate_b[e],
                ml_headnorm_g[e], seq=DEC_SEQ, nbatch=DEC_BATCH, row_off=T_P // DEC_SEQ, group=1,
                ctx=(state_mlstm_C[:, e], state_mlstm_n[:, e], state_mlstm_m[:, e]))
            y, routed = _out_proj_residual([(oa_p, oa_s), (ob_p, ob_s)], y, ada_l, ab_w_out[e], 2, *route_prm)
            new_k.append(k_norm.reshape(BATCH, SEQ, H_A, 2, HD_A))
            new_v.append(v_heads.reshape(BATCH, SEQ, H_A, 2 * HD_A))
            new_c.append(c_new)
            new_n.append(n_new.reshape(BATCH, 2, H_B, HD_B))
            new_m.append(m_new[..., 0, 0])
        else:
            o = layer // 2
            proj = _modulated_proj(y, ada_l, norm_mix_g[layer], hy_w_in[o], (HY_PROJ,), pending)
            if pending is not None:
                y = proj[-1]
            zproj = proj[0]
            cores = []
            for seq, nbatch, row_off, td in ((SEQ, BATCH, 0, D_MODEL), (DEC_SEQ, DEC_BATCH, T_P // DEC_SEQ, 256)):
                fwd, inv = _dft_mats(seq)
                fwd_bf, inv_bf = fwd.astype(BF16), inv.astype(BF16)
                kspec = _hyena_filter_spectrum(seq, fwd_bf, hy_f_w1[o], hy_f_b1[o], hy_f_freq1[o], hy_f_w2[o],
                                               hy_f_b2[o], hy_f_freq2[o], hy_f_w3[o])
                cores.append(_hyena_core(zproj, hy_conv_w[o], hy_conv_b[o].reshape(1, HY_PROJ), fwd_bf, inv_bf,
                                         kspec, hy_bias[o], seq=seq, nbatch=nbatch, row_off=row_off, td=td))
            y, routed = _out_proj_residual([tuple(cores)], y, ada_l, hy_w_out[o], 2, *route_prm)
        y, pending = _moe(layer, y, routed, ada_l, moe_w1, b1p, moe_w2, b2r)
    y_p = y[0].reshape(BATCH, SEQ, D_MODEL)
    y_s = y[1].reshape(DEC_BATCH, DEC_SEQ, D_MODEL)
    return (y_p, y_s, jnp.stack(new_k, axis=1), jnp.stack(new_v, axis=1), jnp.stack(new_c, axis=1),
            jnp.stack(new_n, axis=1), jnp.stack(new_m, axis=1))
```

```python
import functools
import math

import numpy as np
import jax
import jax.numpy as jnp
from jax import lax
from jax.experimental import pallas as pl
from jax.experimental.pallas import tpu as pltpu

D_MODEL = 1024
BATCH = 16
SEQ = 256
DEPTH = 2
DEC_BATCH = 2
DEC_SEQ = 1024
PAST_LEN = 256
GRID_W = 64
W_A = D_MODEL // 2
HD_A = 64
H_A = W_A // (2 * HD_A)
W_B = D_MODEL - W_A
HD_B = 128
H_B = W_B // HD_B
AB_PROJ = 3 * W_A + 4 * W_B + 4 * H_B
ROPE_BASE = 10000.0
CHUNK = 64
HY_ORDER = 2
HY_PROJ = (HY_ORDER + 1) * D_MODEL
HY_BANDS = 8
HY_FH = 64
HY_TARGET = 1e-2
HY_FAST_PCT = 0.3
HY_SLOW_PCT = 1.5
N_EXPERTS = 32
TOP_K = 4
D_FF = D_MODEL
SWIGLU_ALPHA = 1.702
SWIGLU_LIMIT = 7.0
ADA_CHUNKS = 6
EPS = 1e-6
NEG = -1e30
F32 = jnp.float32
BF16 = jnp.bfloat16

T_P = BATCH * SEQ
T_S = DEC_BATCH * DEC_SEQ
T_ALL = T_P + T_S
ROW_TILE = 512
ATTN_Q_TILE = 256
N_ROW_TILES = T_ALL // ROW_TILE
P_TILES = T_P // ROW_TILE
S_TILES_PER_BATCH = DEC_SEQ // ROW_TILE
COND_ROWS = 8
MOE_TILE = 256
N_ASSIGN = T_ALL * TOP_K
MOE_ROWS = N_ASSIGN + N_EXPERTS * MOE_TILE
MOE_TILES = MOE_ROWS // MOE_TILE
X_ALIGN = 16
MLSTM_GROUP = 2
GATHER_ROWS = 1024
X_ROWS = -(-(N_ASSIGN + N_EXPERTS * X_ALIGN + MOE_TILE) // GATHER_ROWS) * GATHER_ROWS
SPARE_ROW = T_ALL
ACC_ROWS = T_ALL + 8
SCATTER_GROUP = 8
W1_DMA_CHUNKS = 8
W2_DMA_CHUNKS = 4
LANES = 128
ROW_CHUNKS = D_MODEL // LANES
VMEM_LIMIT = 56 * 1024 * 1024
HIGHEST = lax.Precision.HIGHEST


def _cparams(n_axes):
    return pltpu.CompilerParams(dimension_semantics=("arbitrary",) * n_axes,
                                vmem_limit_bytes=VMEM_LIMIT)


def _bdot(a, b):
    return jnp.dot(a.astype(BF16), b.astype(BF16), preferred_element_type=F32)


def _cond_row(i):
    return jnp.where(i < P_TILES, 0, 1 + (i - P_TILES) // S_TILES_PER_BATCH)


def _ada_chunk(ada_ref, row, j):
    return ada_ref[pl.ds(row, 1), j * D_MODEL:(j + 1) * D_MODEL]


def _modulate(x, g, shift, scale):
    ms = jnp.mean(x * x, axis=-1, keepdims=True)
    return (x * lax.rsqrt(ms + EPS) * g) * (1.0 + scale) + shift


def _sigmoid(x):
    return 1.0 / (1.0 + jnp.exp(-x))


def _silu(x):
    return x * _sigmoid(x)


def _log_sigmoid(x):
    return jnp.minimum(x, 0.0) - jnp.log(1.0 + jnp.exp(-jnp.abs(x)))


def _dwconv3(x, w, b, seq=None):
    n = x.shape[0]
    seq = n if seq is None else seq
    pos = lax.broadcasted_iota(jnp.int32, x.shape, 0) % seq
    prev = jnp.where(pos == 0, 0.0, pltpu.roll(x, 1, 0))
    nxt = jnp.where(pos == seq - 1, 0.0, pltpu.roll(x, n - 1, 0))
    return prev * w[0:1] + x * w[1:2] + nxt * w[2:3] + b


def _ada_kernel(cond_ref, w_ref, b_ref, o_ref):
    c = _silu(cond_ref[...])
    o_ref[...] = _bdot(c, w_ref[...]) + b_ref[...]


def _ada_table(cond, ada_w, ada_b):
    tn = 1536
    return pl.pallas_call(
        _ada_kernel,
        grid=(DEPTH, ADA_CHUNKS * D_MODEL // tn),
        in_specs=[
            pl.BlockSpec((COND_ROWS, D_MODEL), lambda l, j: (0, 0)),
            pl.BlockSpec((None, D_MODEL, tn), lambda l, j: (l, 0, j)),
            pl.BlockSpec((None, 1, tn), lambda l, j: (l, 0, j)),
        ],
        out_specs=pl.BlockSpec((None, COND_ROWS, tn), lambda l, j: (l, 0, j)),
        out_shape=jax.ShapeDtypeStruct((DEPTH, COND_ROWS, ADA_CHUNKS * D_MODEL), F32),
        compiler_params=_cparams(2),
    )(cond, ada_w, ada_b.reshape(DEPTH, 1, ADA_CHUNKS * D_MODEL))


def _stream_specs(y):
    if isinstance(y, tuple):
        return [pl.BlockSpec((ROW_TILE, D_MODEL), lambda i: (jnp.minimum(i, P_TILES - 1), 0)),
                pl.BlockSpec((ROW_TILE, D_MODEL), lambda i: (jnp.maximum(i - P_TILES, 0), 0))], list(y)
    return [pl.BlockSpec((ROW_TILE, D_MODEL), lambda i: (i, 0))], [y]


def _token_rows(ref, n):
    return jnp.concatenate([ref[pl.ds(j, n, stride=ROW_CHUNKS), :] for j in range(ROW_CHUNKS)], axis=1)


def _stream_tile(y_refs, i):
    if len(y_refs) == 2:
        return jnp.where(i < P_TILES, y_refs[0][...], y_refs[1][...])
    return y_refs[0][...]


def _proj_kernel(splits, n_y, has_pending, *refs):
    y_refs = refs[:n_y]
    refs = refs[n_y:]
    if has_pending:
        acc_ref, ada_prev_ref = refs[:2]
        refs = refs[2:]
    ada_ref, g_ref, w_ref = refs[:3]
    out_refs, wbf_ref = refs[3:-1], refs[-1]
    i = pl.program_id(0)

    @pl.when(i == 0)
    def _():
        wbf_ref[...] = w_ref[...].astype(BF16)

    row = _cond_row(i)
    y = _stream_tile(y_refs, i)
    if has_pending:
        y = y + _ada_chunk(ada_prev_ref, row, 5) * _token_rows(acc_ref, ROW_TILE)
        out_refs[-1][...] = y
        out_refs = out_refs[:-1]
    h = _modulate(y, g_ref[...], _ada_chunk(ada_ref, row, 0), _ada_chunk(ada_ref, row, 1))
    h = h.astype(BF16)
    lo = 0
    for o_ref, width in zip(out_refs, splits):
        o_ref[...] = jnp.dot(h, wbf_ref[:, lo:lo + width], preferred_element_type=F32)
        lo += width


def _modulated_proj(y, ada_l, g, w, splits, pending=None):
    n = w.shape[1]
    y_specs, y_args = _stream_specs(y)
    widths = tuple(splits)
    if pending is not None:
        y_specs = y_specs + [pl.BlockSpec((ROW_TILE * ROW_CHUNKS, LANES), lambda i: (i, 0)),
                             pl.BlockSpec((COND_ROWS, ADA_CHUNKS * D_MODEL), lambda i: (0, 0))]
        y_args = y_args + list(pending)
        widths = widths + (D_MODEL,)
    return pl.pallas_call(
        functools.partial(_proj_kernel, splits, len(y_args) - (2 if pending is not None else 0), pending is not None),
        grid=(N_ROW_TILES,),
        in_specs=y_specs + [
            pl.BlockSpec((COND_ROWS, ADA_CHUNKS * D_MODEL), lambda i: (0, 0)),
            pl.BlockSpec((1, D_MODEL), lambda i: (0, 0)),
            pl.BlockSpec((D_MODEL, n), lambda i: (0, 0), pipeline_mode=pl.Buffered(1)),
        ],
        out_specs=[pl.BlockSpec((ROW_TILE, s), lambda i: (i, 0)) for s in widths],
        out_shape=[jax.ShapeDtypeStruct((T_ALL, s), F32) for s in widths],
        scratch_shapes=[pltpu.VMEM((D_MODEL, n), BF16)],
        compiler_params=_cparams(1),
    )(*y_args, ada_l, g.reshape(1, D_MODEL), w)


def _out_proj_kernel(n_in, n_y, gate_chunk, *refs):
    x_refs = refs[:2 * n_in]
    y_refs = refs[2 * n_in:2 * n_in + n_y]
    ada_ref, w_ref, g_ref, rw_ref, rb_ref, o_ref, h_ref, idx_ref, wt_ref, wbf_ref = refs[2 * n_in + n_y:]
    i = pl.program_id(0)

    @pl.when(i == 0)
    def _():
        wbf_ref[...] = w_ref[...].astype(BF16)

    acc = None
    lo = 0
    for xp_ref, xs_ref in zip(x_refs[0::2], x_refs[1::2]):
        k = xp_ref.shape[1]
        x = jnp.where(i < P_TILES, xp_ref[...], xs_ref[...])
        part = jnp.dot(x.astype(BF16), wbf_ref[lo:lo + k, :], preferred_element_type=F32)
        acc = part if acc is None else acc + part
        lo += k
    row = _cond_row(i)
    y_new = _stream_tile(y_refs, i) + _ada_chunk(ada_ref, row, gate_chunk) * acc
    o_ref[...] = y_new
    _route_tile(y_new, row, ada_ref, g_ref, rw_ref, rb_ref, h_ref, idx_ref, wt_ref)


def _out_proj_residual(xs, y, ada_l, w, gate_chunk, ffn_g, router_w, router_b):
    y_specs, y_args = _stream_specs(y)
    x_specs = []
    for xp, _ in xs:
        x_specs.append(pl.BlockSpec((ROW_TILE, xp.shape[1]), lambda i: (jnp.minimum(i, P_TILES - 1), 0)))
        x_specs.append(pl.BlockSpec((ROW_TILE, xp.shape[1]), lambda i: (jnp.maximum(i - P_TILES, 0), 0)))
    y_new, h, idx_t, wts_t = pl.pallas_call(
        functools.partial(_out_proj_kernel, len(xs), len(y_args), gate_chunk),
        grid=(N_ROW_TILES,),
        in_specs=x_specs + y_specs + [
            pl.BlockSpec((COND_ROWS, ADA_CHUNKS * D_MODEL), lambda i: (0, 0)),
            pl.BlockSpec((D_MODEL, D_MODEL), lambda i: (0, 0), pipeline_mode=pl.Buffered(1)),
            pl.BlockSpec((1, D_MODEL), lambda i: (0, 0)),
            pl.BlockSpec((N_EXPERTS, D_MODEL), lambda i: (0, 0)),
            pl.BlockSpec((N_EXPERTS, 1), lambda i: (0, 0)),
        ],
        out_specs=[
            pl.BlockSpec((ROW_TILE, D_MODEL), lambda i: (i, 0)),
            pl.BlockSpec((ROW_TILE * ROW_CHUNKS, LANES), lambda i: (i, 0)),
            pl.BlockSpec((TOP_K, ROW_TILE), lambda i: (0, i)),
            pl.BlockSpec((TOP_K, ROW_TILE), lambda i: (0, i)),
        ],
        out_shape=[
            jax.ShapeDtypeStruct((T_ALL, D_MODEL), F32),
            jax.ShapeDtypeStruct((T_ALL * ROW_CHUNKS, LANES), F32),
            jax.ShapeDtypeStruct((TOP_K, T_ALL), jnp.int32),
            jax.ShapeDtypeStruct((TOP_K, T_ALL), F32),
        ],
        scratch_shapes=[pltpu.VMEM((D_MODEL, D_MODEL), BF16)],
        compiler_params=_cparams(1),
    )(*[a for pair in xs for a in pair], *y_args, ada_l, w,
      ffn_g.reshape(1, D_MODEL), router_w.T, router_b.reshape(N_EXPERTS, 1))
    return y_new, (h, idx_t, wts_t)


def _subhead_norm(x, g2):
    lane = lax.broadcasted_iota(jnp.int32, x.shape, 1)
    first = lane < HD_A
    xx = x * x
    s0 = jnp.sum(jnp.where(first, xx, 0.0), axis=-1, keepdims=True)
    s1 = jnp.sum(jnp.where(first, 0.0, xx), axis=-1, keepdims=True)
    r = jnp.where(first, lax.rsqrt(s0 / HD_A + EPS), lax.rsqrt(s1 / HD_A + EPS))
    return x * r * g2


def _rope(x, cos, sin):
    quarter = HD_A // 4
    lane = lax.broadcasted_iota(jnp.int32, x.shape, 1)
    lower = (lane % (2 * quarter)) < quarter
    swapped = jnp.where(lower, pltpu.roll(x, 2 * HD_A - quarter, 1), pltpu.roll(x, quarter, 1))
    return x * cos + swapped * sin


def _attn_kernel(lam_init, has_ctx, *refs):
    if has_ctx:
        (q_ref, k_ref, v_ref, ck_ref, cv_ref, cq_ref, sq_ref, ckk_ref, skk_ref,
         qg_ref, kg_ref, lp_ref, sg_ref, o_ref, kall_ref, vall_ref) = refs
    else:
        q_ref, k_ref, v_ref, qg_ref, kg_ref, lp_ref, sg_ref, o_ref, kn_ref, vh_ref = refs
    lp = lp_ref[...]
    lam = (jnp.exp(jnp.sum(lp[0:1] * lp[1:2], axis=-1, keepdims=True))
           - jnp.exp(jnp.sum(lp[2:3] * lp[3:4], axis=-1, keepdims=True)) + lam_init)

    def attend(q, k, v):
        probs = []
        for c in range(2):
            qc = q[:, c * HD_A:(c + 1) * HD_A].astype(BF16)
            kc = k[:, c * HD_A:(c + 1) * HD_A].astype(BF16)
            s = lax.dot_general(qc, kc, (((1,), (1,)), ((), ())), preferred_element_type=F32) * (HD_A ** -0.5)
            e = jnp.exp(s - jnp.max(s, axis=-1, keepdims=True))
            probs.append(e / jnp.sum(e, axis=-1, keepdims=True))
        o = _bdot(probs[0] - lam * probs[1], v)
        ms = jnp.mean(o * o, axis=-1, keepdims=True)
        return (o * lax.rsqrt(ms + EPS) * sg_ref[...]) * (1.0 - lam_init)

    if not has_ctx:
        for h in range(H_A):
            cols = slice(h * 2 * HD_A, (h + 1) * 2 * HD_A)
            k = _subhead_norm(k_ref[:, cols], kg_ref[...])
            for c in range(2):
                kn_ref[pl.ds(2 * h + c, SEQ, stride=2 * H_A), :] = k[:, c * HD_A:(c + 1) * HD_A]
            v = v_ref[:, cols]
            vh_ref[pl.ds(h, SEQ, stride=H_A), :] = v
            o_ref[:, cols] = attend(_subhead_norm(q_ref[:, cols], qg_ref[...]), k, v)
        return

    @pl.when(pl.program_id(2) == 0)
    def _():
        kall_ref[0:PAST_LEN, :] = ck_ref[...].astype(BF16)
        vall_ref[0:PAST_LEN, :] = cv_ref[...].astype(BF16)
        k_new = _rope(_subhead_norm(k_ref[...], kg_ref[...]), ckk_ref[...], skk_ref[...])
        kall_ref[PAST_LEN:, :] = k_new.astype(BF16)
        vall_ref[PAST_LEN:, :] = v_ref[...].astype(BF16)

    q = _rope(_subhead_norm(q_ref[...], qg_ref[...]), cq_ref[...], sq_ref[...])
    o_ref[...] = attend(q, kall_ref[...], vall_ref[...])


def _attention_prompt(qkv, qg2, kg2, lam_p, sub_g, lam_init):
    head = 2 * HD_A
    small = [
        pl.BlockSpec((1, head), lambda b: (0, 0)),
        pl.BlockSpec((1, head), lambda b: (0, 0)),
        pl.BlockSpec((4, HD_A), lambda b: (0, 0)),
        pl.BlockSpec((1, head), lambda b: (0, 0)),
    ]
    return pl.pallas_call(
        functools.partial(_attn_kernel, lam_init, False),
        grid=(BATCH,),
        in_specs=[
            pl.BlockSpec((SEQ, W_A), lambda b: (b, 0)),
            pl.BlockSpec((SEQ, W_A), lambda b: (b, 1)),
            pl.BlockSpec((SEQ, W_A), lambda b: (b, 2)),
        ] + small,
        out_specs=[pl.BlockSpec((SEQ, W_A), lambda b: (b, 0)),
                   pl.BlockSpec((SEQ * 2 * H_A, HD_A), lambda b: (b, 0)),
                   pl.BlockSpec((SEQ * H_A, head), lambda b: (b, 0))],
        out_shape=[jax.ShapeDtypeStruct((T_P, W_A), F32), jax.ShapeDtypeStruct((T_P * 2 * H_A, HD_A), F32),
                   jax.ShapeDtypeStruct((T_P * H_A, head), F32)],
        compiler_params=_cparams(1),
    )(qkv, qkv, qkv, qg2, kg2, lam_p, sub_g)


def _attention_sample(qkv, cache_k, cache_v, cos, sin, qg2, kg2, lam_p, sub_g, lam_init):
    nh = H_A
    head = 2 * HD_A
    tq = ATTN_Q_TILE
    nq = DEC_SEQ // tq
    q_off = T_P // tq
    k_off = T_P // DEC_SEQ
    small = [
        pl.BlockSpec((1, head), lambda b, h, i: (0, 0)),
        pl.BlockSpec((1, head), lambda b, h, i: (0, 0)),
        pl.BlockSpec((4, HD_A), lambda b, h, i: (0, 0)),
        pl.BlockSpec((1, head), lambda b, h, i: (0, 0)),
    ]
    return pl.pallas_call(
        functools.partial(_attn_kernel, lam_init, True),
        grid=(DEC_BATCH, nh, nq),
        in_specs=[
            pl.BlockSpec((tq, head), lambda b, h, i: (q_off + b * nq + i, h)),
            pl.BlockSpec((DEC_SEQ, head), lambda b, h, i: (k_off + b, nh + h)),
            pl.BlockSpec((DEC_SEQ, head), lambda b, h, i: (k_off + b, 2 * nh + h)),
            pl.BlockSpec((None, PAST_LEN, head), lambda b, h, i: (b, 0, h)),
            pl.BlockSpec((None, PAST_LEN, head), lambda b, h, i: (b, 0, h)),
            pl.BlockSpec((tq, head), lambda b, h, i: (i, 0)),
            pl.BlockSpec((tq, head), lambda b, h, i: (i, 0)),
            pl.BlockSpec((DEC_SEQ, head), lambda b, h, i: (0, 0)),
            pl.BlockSpec((DEC_SEQ, head), lambda b, h, i: (0, 0)),
        ] + small,
        out_specs=pl.BlockSpec((tq, head), lambda b, h, i: (b * nq + i, h)),
        out_shape=jax.ShapeDtypeStruct((T_S, W_A), F32),
        scratch_shapes=[pltpu.VMEM((PAST_LEN + DEC_SEQ, head), BF16)] * 2,
        compiler_params=_cparams(3),
    )(qkv, qkv, qkv, cache_k, cache_v, cos, sin, cos, sin, qg2, kg2, lam_p, sub_g)


def _rope_tables():
    half = HD_A // 2
    nf = half // 2
    inv = ROPE_BASE ** (-np.arange(nf, dtype=np.float32) / nf)
    pos = np.arange(DEC_SEQ)
    row = (pos // GRID_W).astype(np.float32)
    col = (pos % GRID_W).astype(np.float32)
    ang_r = (row[:, None] * inv).astype(np.float32)
    ang_c = (col[:, None] * inv).astype(np.float32)
    ang = np.concatenate([ang_r, ang_r, ang_c, ang_c], axis=1)
    sign = np.concatenate([-np.ones(nf), np.ones(nf), -np.ones(nf), np.ones(nf)]).astype(np.float32)
    cos = np.cos(ang.astype(np.float64)).astype(np.float32)
    sin = (np.sin(ang.astype(np.float64)) * sign).astype(np.float32)
    return jnp.asarray(np.tile(cos, (1, 2))), jnp.asarray(np.tile(sin, (1, 2)))


def _mlstm_kernel(seq, has_ctx, group, *refs):
    if has_ctx:
        (q_ref, k_ref, cwq_ref, cwk_ref, cbq_ref, cbk_ref, v_ref, mo_ref, gi_ref, gf_ref,
         gbi_ref, gbf_ref, hn_ref, c0_ref, n0_ref, m0_ref, o_ref,
         qs_ref, ks_ref, hf_ref, hb_ref, cs_ref, rrow_ref, col_ref, wc_ref) = refs
    else:
        (q_ref, k_ref, cwq_ref, cwk_ref, cbq_ref, cbk_ref, v_ref, mo_ref, gi_ref, gf_ref,
         gbi_ref, gbf_ref, hn_ref, o_ref, c_out_ref, n_out_ref, m_out_ref,
         qs_ref, ks_ref, hf_ref, hb_ref, cs_ref, rrow_ref, col_ref, wc_ref) = refs
    nc = seq // CHUNK
    n_chain = 2 * H_B
    chains = [(sub, d, h) for sub in range(group) for d in range(2) for h in range(H_B)]
    qs_ref[...] = _silu(_dwconv3(q_ref[...], cwq_ref[...], cbq_ref[...], seq)) * (HD_B ** -0.5)
    ks_ref[...] = _silu(_dwconv3(k_ref[...], cwk_ref[...], cbk_ref[...], seq))

    rows = nc * n_chain
    lane = lax.broadcasted_iota(jnp.int32, (rows, 2 * CHUNK), 1)
    forward = lax.broadcasted_iota(jnp.int32, (rows, 2 * CHUNK), 0) % n_chain < H_B
    valid = lane < CHUNK

    def scan(x, op, fill):
        pre, suf = x, x
        sh = 1
        while sh < CHUNK:
            pre = op(pre, jnp.where(lane >= sh, pltpu.roll(pre, sh, 1), fill))
            suf = op(suf, jnp.where(lane + sh < CHUNK, pltpu.roll(suf, 2 * CHUNK - sh, 1), fill))
            sh *= 2
        return jnp.where(forward, pre, suf)

    mm_final = []
    for sub in range(group):
        gate_i = (gi_ref[sub] + gbi_ref[...]).reshape(rows, 2 * CHUNK)
        lf = jnp.where(valid, _log_sigmoid(gf_ref[sub] + gbf_ref[...]).reshape(rows, 2 * CHUNK), 0.0)
        b = scan(lf, jnp.add, 0.0)
        cmax = scan(jnp.where(valid, gate_i - b, -jnp.inf), jnp.maximum, -jnp.inf)
        b_last = jnp.sum(lf, axis=1, keepdims=True)
        g = b_last - b + gate_i
        g_max = jnp.max(jnp.where(valid, g, -jnp.inf), axis=1, keepdims=True)
        mm = m0_ref[sub] if has_ctx else jnp.zeros((n_chain, 1), F32)
        mm_seq = []
        for p in range(nc):
            mm_seq.append(mm)
            seg = slice(p * n_chain, (p + 1) * n_chain)
            mm = jnp.maximum(b_last[seg] + mm, g_max[seg])
        mm_final.append(mm)
        mm_prev = jnp.concatenate(mm_seq, axis=0)
        mm_next = jnp.concatenate(mm_seq[1:] + [mm], axis=0)
        m_t = jnp.maximum(b + mm_prev, b + cmax)
        rrow_ref[sub] = (b - gate_i).reshape(nc, n_chain, 2 * CHUNK)
        wc_ref[sub] = jnp.exp(b_last + mm_prev - mm_next).reshape(nc, n_chain, 1)
        per_row = [b, m_t, jnp.exp(b + mm_prev - m_t), jnp.exp(-m_t), jnp.exp(g - mm_next)]
        for j, arr in enumerate(per_row):
            by_time = arr.T
            for p in range(nc):
                col_ref[sub, p, :, j * n_chain:(j + 1) * n_chain] = by_time[0:CHUNK, p * n_chain:(p + 1) * n_chain]

    t_idx = lax.broadcasted_iota(jnp.int32, (CHUNK, CHUNK), 0)
    s_idx = lax.broadcasted_iota(jnp.int32, (CHUNK, CHUNK), 1)
    for n, (sub, d, h) in enumerate(chains):
        cs_ref[n] = c0_ref[sub, d, h] if has_ctx else jnp.zeros((HD_B, HD_B), F32)

    def out_step(p, n_states):
        new_states = []
        for n, (sub, d, h) in enumerate(chains):
            cols = col_ref[sub, p]
            rrows = rrow_ref[sub, p]
            wcs = wc_ref[sub, p]
            n_loc = d * H_B + h
            c = p if d == 0 else nc - 1 - p
            r0 = pl.multiple_of(sub * seq + c * CHUNK, CHUNK)
            hcols = slice(h * HD_B, (h + 1) * HD_B)
            qt = qs_ref[pl.ds(r0, CHUNK), hcols]
            kt = ks_ref[pl.ds(r0, CHUNK), hcols]
            vt = v_ref[pl.ds(r0, CHUNK), hcols]
            b_col, m_t, w_inter, e_inv, w_k = (cols[:, j * n_chain + n_loc:j * n_chain + n_loc + 1] for j in range(5))
            mask = (s_idx <= t_idx) if d == 0 else (s_idx >= t_idx)
            decay = jnp.exp(jnp.where(mask, b_col - rrows[n_loc:n_loc + 1, 0:CHUNK], NEG) - m_t)
            qk = lax.dot_general(qt.astype(BF16), kt.astype(BF16), (((1,), (1,)), ((), ())),
                                 preferred_element_type=F32)
            s = qk * decay
            cm = cs_ref[n]
            nm = n_states[n]
            cq = lax.dot_general(qt.astype(BF16), cm.astype(BF16), (((1,), (1,)), ((), ())),
                                 preferred_element_type=F32)
            num = _bdot(s, vt) + w_inter * cq
            nq = jnp.sum(s, axis=-1, keepdims=True) + w_inter * jnp.sum(qt * nm, axis=-1, keepdims=True)
            hdir_ref = hf_ref if d == 0 else hb_ref
            hdir_ref[pl.ds(r0, CHUNK), hcols] = num / jnp.maximum(jnp.abs(nq), e_inv)
            w_c = wcs[n_loc:n_loc + 1, :]
            vw = (vt * w_k).astype(BF16)
            cs_ref[n] = w_c * cm + lax.dot_general(vw, kt.astype(BF16), (((0,), (0,)), ((), ())),
                                                   preferred_element_type=F32)
            new_states.append(w_c * nm + jnp.sum(kt * w_k, axis=0, keepdims=True))
        return tuple(new_states)

    if has_ctx:
        n_init = tuple(n0_ref[sub, d, h] for sub, d, h in chains)
    else:
        n_init = tuple(jnp.zeros((1, HD_B), F32) for _ in chains)
    n_final = lax.fori_loop(0, nc, out_step, n_init)
    if not has_ctx:
        for n, (sub, d, h) in enumerate(chains):
            n_loc = d * H_B + h
            c_out_ref[sub, d, h] = cs_ref[n]
            n_out_ref[sub, d, h] = n_final[n]
            m_out_ref[sub, d, h] = jnp.broadcast_to(mm_final[sub][n_loc:n_loc + 1, :], (1, HD_B))

    for h in range(H_B):
        hcols = slice(h * HD_B, (h + 1) * HD_B)
        hh = hf_ref[:, hcols] + hb_ref[:, hcols]
        ms = jnp.mean(hh * hh, axis=-1, keepdims=True)
        o_ref[:, hcols] = (hh * lax.rsqrt(ms + EPS) * hn_ref[:, hcols]) * _sigmoid(mo_ref[:, hcols])


def _mlstm(mqk, mv, mo, mg_stream, conv_w, conv_b, gate_b, hn_g, *, seq, nbatch, row_off, group, ctx=None):
    nh = H_B
    nc = seq // CHUNK
    has_ctx = ctx is not None
    assert nbatch % group == 0
    gt = mg_stream.reshape(nbatch, nc, CHUNK, 2, 2, nh).transpose(0, 1, 3, 4, 5, 2)
    pad = ((0, 0), (0, 0), (0, 0), (0, CHUNK))
    gates = [jnp.pad(jnp.concatenate([gt[:, :, 0, j], gt[:, ::-1, 1, j]], axis=2), pad) for j in range(2)]
    gate_bias = [jnp.concatenate([gate_b[0, j], gate_b[1, j]]).reshape(2 * nh, 1) for j in range(2)]
    blk = lambda col: pl.BlockSpec((group * seq, W_B), lambda b, col=col: (row_off + b, col))
    gate_blk = pl.BlockSpec((group, nc, 2 * nh, 2 * CHUNK), lambda b: (b, 0, 0, 0))
    in_specs = [
        blk(0), blk(1),
        pl.BlockSpec((3, W_B), lambda b: (0, 0)),
        pl.BlockSpec((3, W_B), lambda b: (0, 1)),
        pl.BlockSpec((1, W_B), lambda b: (0, 0)),
        pl.BlockSpec((1, W_B), lambda b: (0, 1)),
        blk(0), blk(0),
        gate_blk, gate_blk,
        pl.BlockSpec((2 * nh, 1), lambda b: (0, 0)),
        pl.BlockSpec((2 * nh, 1), lambda b: (0, 0)),
        pl.BlockSpec((1, W_B), lambda b: (0, 0)),
    ]
    args = [mqk, mqk, conv_w, conv_w, conv_b, conv_b, mv, mo, gates[0], gates[1],
            gate_bias[0], gate_bias[1], hn_g.reshape(1, W_B)]
    o_spec = pl.BlockSpec((group * seq, W_B), lambda b: (b, 0))
    o_shape = jax.ShapeDtypeStruct((nbatch * seq, W_B), F32)
    state_blk = lambda rows: pl.BlockSpec((group, 2, nh, rows, HD_B), lambda b: (b, 0, 0, 0, 0))
    if has_ctx:
        c0, n0, m0 = ctx
        in_specs += [state_blk(HD_B), state_blk(1), pl.BlockSpec((group, 2 * nh, 1), lambda b: (b, 0, 0))]
        args += [c0, n0.reshape(nbatch, 2, nh, 1, HD_B), m0.reshape(nbatch, 2 * nh, 1)]
        out_specs, out_shape = o_spec, o_shape
    else:
        out_specs = [o_spec, state_blk(HD_B), state_blk(1), state_blk(1)]
        out_shape = [
            o_shape,
            jax.ShapeDtypeStruct((nbatch, 2, nh, HD_B, HD_B), F32),
            jax.ShapeDtypeStruct((nbatch, 2, nh, 1, HD_B), F32),
            jax.ShapeDtypeStruct((nbatch, 2, nh, 1, HD_B), F32),
        ]
    return pl.pallas_call(
        functools.partial(_mlstm_kernel, seq, has_ctx, group),
        grid=(nbatch // group,),
        in_specs=in_specs,
        out_specs=out_specs,
        out_shape=out_shape,
        scratch_shapes=[pltpu.VMEM((group * seq, W_B), F32)] * 4 + [
            pltpu.VMEM((group * 2 * nh, HD_B, HD_B), F32),
            pltpu.VMEM((group, nc, 2 * nh, 2 * CHUNK), F32),
            pltpu.VMEM((group, nc, CHUNK, 5 * 2 * nh), F32),
            pltpu.VMEM((group, nc, 2 * nh, 1), F32),
        ],
        compiler_params=_cparams(1),
    )(*args)


def _dft_mats(L):
    f = np.arange(L)[:, None]
    j = np.arange(L)[None, :]
    ang = 2.0 * np.pi * ((f * j) % (2 * L)) / (2 * L)
    cm = np.cos(ang)
    sm = np.sin(ang)
    alt = (1.0 - 2.0 * (np.arange(L) % 2))
    fwd_b = -sm
    fwd_b[0, :] = alt
    fwd = np.concatenate([cm, fwd_b], axis=0)
    wgt = np.where(np.arange(L) == 0, 1.0, 2.0)[None, :]
    inv_a = cm.T * wgt
    inv_b = -2.0 * sm.T
    inv_b[:, 0] = alt
    inv = np.concatenate([inv_a, inv_b], axis=1) / (2 * L)
    return jnp.asarray(fwd.astype(np.float32)), jnp.asarray(inv.astype(np.float32))


def _hyena_feats(L):
    t = np.linspace(0.0, 1.0, L, dtype=np.float32)
    wpos = (2.0 * math.pi * np.arange(L, dtype=np.float32) / L).astype(np.float32)
    fb = np.linspace(1e-4, HY_BANDS - 1, HY_BANDS, dtype=np.float32)
    z = (wpos[:, None] * fb).astype(np.float32)
    feats = np.concatenate([t[:, None], np.cos(z), -np.sin(z)], axis=-1).astype(np.float32)
    deltas = np.abs(np.linspace(math.log(HY_TARGET) / HY_SLOW_PCT, math.log(HY_TARGET) / HY_FAST_PCT,
                                D_MODEL, dtype=np.float32))
    decay = np.exp(-t[:, None] * deltas).astype(np.float32)
    return jnp.asarray(feats), jnp.asarray(decay)


def _filter_kernel(L, feats_ref, w1_ref, b1_ref, fr1_ref, w2_ref, b2_ref, fr2_ref, w3f_ref, w3b_ref,
                   decay_ref, fwd_ref, o_ref, hdn_ref):
    @pl.when((pl.program_id(0) == 0) & (pl.program_id(1) == 0))
    def _():
        h1 = jnp.sin(fr1_ref[...] * (jnp.dot(feats_ref[...], w1_ref[...], precision=HIGHEST,
                                             preferred_element_type=F32) + b1_ref[...]))
        hdn_ref[...] = jnp.sin(fr2_ref[...] * (jnp.dot(h1, w2_ref[...], precision=HIGHEST,
                                                       preferred_element_type=F32) + b2_ref[...]))

    hdn = hdn_ref[...]
    decay = decay_ref[...]
    f_fwd = jnp.dot(hdn, w3f_ref[...], precision=HIGHEST, preferred_element_type=F32) * decay
    f_bwd = jnp.dot(hdn, w3b_ref[...], precision=HIGHEST, preferred_element_type=F32) * decay
    row = lax.broadcasted_iota(jnp.int32, f_bwd.shape, 0)
    f_bwd = jnp.where(row == 0, 0.0, f_bwd)
    f_sum = f_fwd + f_bwd
    o_ref[0:L, :] = _bdot(fwd_ref[0:L, :], f_sum)
    imag = _bdot(fwd_ref[L:2 * L, :], f_fwd - f_bwd)
    nyquist = _bdot(fwd_ref[L:L + 16, :], f_sum)[0:1]
    o_ref[L:2 * L, :] = jnp.where(row == 0, nyquist, imag)


def _hyena_filter_spectrum(L, fwd_bf, w1, b1, fr1, w2, b2, fr2, w3):
    feats, decay = _hyena_feats(L)
    td = 512
    nd = D_MODEL // td
    emb = feats.shape[1]
    vec = lambda a: a.reshape(1, HY_FH)
    full = lambda shape: pl.BlockSpec(shape, lambda o, j: (0, 0))
    return pl.pallas_call(
        functools.partial(_filter_kernel, L),
        grid=(HY_ORDER, nd),
        in_specs=[
            full((L, emb)), full((emb, HY_FH)), full((1, HY_FH)), full((1, HY_FH)),
            full((HY_FH, HY_FH)), full((1, HY_FH)), full((1, HY_FH)),
            pl.BlockSpec((HY_FH, td), lambda o, j: (0, o * 2 * nd + j)),
            pl.BlockSpec((HY_FH, td), lambda o, j: (0, o * 2 * nd + nd + j)),
            pl.BlockSpec((L, td), lambda o, j: (0, j)),
            full((2 * L, L)),
        ],
        out_specs=pl.BlockSpec((2 * L, td), lambda o, j: (0, o * nd + j)),
        out_shape=jax.ShapeDtypeStruct((2 * L, HY_ORDER * D_MODEL), F32),
        scratch_shapes=[pltpu.VMEM((L, HY_FH), F32)],
        compiler_params=_cparams(2),
    )(feats, w1, vec(b1), vec(fr1), w2, vec(b2), vec(fr2), w3, w3, decay, fwd_bf)


def _spectral_conv(u, fwd, inv, kspec, L):
    uf = jnp.dot(fwd, u.astype(BF16), preferred_element_type=F32)
    ua, ub = uf[0:L], uf[L:2 * L]
    ka, kb = kspec[0:L], kspec[L:2 * L]
    first = lax.broadcasted_iota(jnp.int32, ua.shape, 0) == 0
    ya = ua * ka - jnp.where(first, 0.0, ub * kb)
    yb = jnp.where(first, ub * kb, ua * kb + ub * ka)
    y = jnp.concatenate([ya, yb], axis=0).astype(BF16)
    return jnp.dot(inv, y, preferred_element_type=F32)


def _hyena_kernel(L, zv_ref, z1_ref, z2_ref, cwv_ref, cw1_ref, cw2_ref, cbv_ref, cb1_ref, cb2_ref,
                  fwd_ref, inv_ref, k0_ref, k1_ref, bias0_ref, bias1_ref, o_ref):
    fwd = fwd_ref[...]
    inv = inv_ref[...]
    v = _dwconv3(zv_ref[...], cwv_ref[...], cbv_ref[...])
    x1 = _dwconv3(z1_ref[...], cw1_ref[...], cb1_ref[...])
    x2 = _dwconv3(z2_ref[...], cw2_ref[...], cb2_ref[...])
    z = x1 * (_spectral_conv(v, fwd, inv, k0_ref[...], L) + v * bias0_ref[...])
    o_ref[...] = x2 * (_spectral_conv(z, fwd, inv, k1_ref[...], L) + z * bias1_ref[...])


def _hyena_core(zproj, conv_w, conv_b, fwd_bf, inv_bf, kspec, bias, *, seq, nbatch, row_off, td):
    nd = D_MODEL // td
    zblk = lambda part: pl.BlockSpec((seq, td), lambda b, j, part=part: (row_off + b, part * nd + j))
    cwblk = lambda part: pl.BlockSpec((3, td), lambda b, j, part=part: (0, part * nd + j))
    cbblk = lambda part: pl.BlockSpec((1, td), lambda b, j, part=part: (0, part * nd + j))
    return pl.pallas_call(
        functools.partial(_hyena_kernel, seq),
        grid=(nbatch, nd),
        in_specs=[
            zblk(0), zblk(1), zblk(2), cwblk(0), cwblk(1), cwblk(2), cbblk(0), cbblk(1), cbblk(2),
            pl.BlockSpec((2 * seq, seq), lambda b, j: (0, 0), pipeline_mode=pl.Buffered(1)),
            pl.BlockSpec((seq, 2 * seq), lambda b, j: (0, 0), pipeline_mode=pl.Buffered(1)),
            pl.BlockSpec((2 * seq, td), lambda b, j: (0, j)),
            pl.BlockSpec((2 * seq, td), lambda b, j: (0, nd + j)),
            pl.BlockSpec((None, 1, td), lambda b, j: (0, 0, j)),
            pl.BlockSpec((None, 1, td), lambda b, j: (1, 0, j)),
        ],
        out_specs=pl.BlockSpec((seq, td), lambda b, j: (b, j)),
        out_shape=jax.ShapeDtypeStruct((nbatch * seq, D_MODEL), F32),
        compiler_params=_cparams(2),
    )(zproj, zproj, zproj, conv_w, conv_w, conv_w, conv_b, conv_b, conv_b,
      fwd_bf, inv_bf, kspec, kspec, bias.reshape(HY_ORDER, 1, D_MODEL), bias.reshape(HY_ORDER, 1, D_MODEL))


def _route_tile(y_tile, row, ada_ref, g_ref, rw_ref, rb_ref, h_ref, idx_ref, wt_ref):
    h = _modulate(y_tile, g_ref[...], _ada_chunk(ada_ref, row, 3), _ada_chunk(ada_ref, row, 4))
    for j in range(ROW_CHUNKS):
        h_ref[pl.ds(j, ROW_TILE, stride=ROW_CHUNKS), :] = h[:, j * LANES:(j + 1) * LANES]
    logits = lax.dot_general(rw_ref[...], h, (((1,), (1,)), ((), ())), precision=HIGHEST,
                             preferred_element_type=F32) + rb_ref[...]
    expert = lax.broadcasted_iota(jnp.int32, logits.shape, 0)
    slot = lax.broadcasted_iota(jnp.int32, (TOP_K, logits.shape[1]), 0)
    vals = jnp.zeros((TOP_K, logits.shape[1]), F32)
    idxs = jnp.zeros((TOP_K, logits.shape[1]), jnp.int32)
    cur = logits
    for k in range(TOP_K):
        m = jnp.max(cur, axis=0, keepdims=True)
        a = jnp.min(jnp.where(cur == m, expert, N_EXPERTS), axis=0, keepdims=True)
        vals = jnp.where(slot == k, m, vals)
        idxs = jnp.where(slot == k, a, idxs)
        cur = jnp.where(expert == a, -jnp.inf, cur)
    e = jnp.exp(vals - vals[0:1])
    wt_ref[...] = e / jnp.sum(e, axis=0, keepdims=True)
    idx_ref[...] = idxs


def _dispatch_kernel(rows_ref, h_ref, o_ref):
    base = pl.program_id(0) * GATHER_ROWS
    for r in range(GATHER_ROWS):
        t = pl.multiple_of(rows_ref[base + r], ROW_CHUNKS)
        o_ref[r * ROW_CHUNKS:(r + 1) * ROW_CHUNKS, :] = h_ref[pl.ds(t, ROW_CHUNKS), :]


def _dispatch(h_tiles, gather_row):
    grid_spec = pltpu.PrefetchScalarGridSpec(
        num_scalar_prefetch=1,
        grid=(X_ROWS // GATHER_ROWS,),
        in_specs=[pl.BlockSpec((T_ALL * ROW_CHUNKS, LANES), lambda i, rows: (0, 0), pipeline_mode=pl.Buffered(1))],
        out_specs=pl.BlockSpec((GATHER_ROWS * ROW_CHUNKS, LANES), lambda i, rows: (i, 0)),
    )
    return pl.pallas_call(
        _dispatch_kernel,
        grid_spec=grid_spec,
        out_shape=jax.ShapeDtypeStruct((X_ROWS * ROW_CHUNKS, LANES), F32),
        compiler_params=_cparams(1),
    )(gather_row * ROW_CHUNKS, h_tiles)


def _deinterleave_matrix():
    s = np.zeros((256, 256), np.float32)
    j = np.arange(128)
    s[2 * j, j] = 1.0
    s[2 * j + 1, 128 + j] = 1.0
    return jnp.asarray(s)


def _weight_copies(layer, e, w1_hbm, w2_hbm, w1s_ref, w2s_ref, sem):
    copies = []
    r1 = D_MODEL // W1_DMA_CHUNKS
    for c in range(W1_DMA_CHUNKS):
        copies.append(pltpu.make_async_copy(w1_hbm.at[layer, e, pl.ds(c * r1, r1)],
                                            w1s_ref.at[pl.ds(c * r1, r1)], sem.at[c]))
    r2 = D_FF // W2_DMA_CHUNKS
    for c in range(W2_DMA_CHUNKS):
        copies.append(pltpu.make_async_copy(w2_hbm.at[layer, e, pl.ds(c * r2, r2)],
                                            w2s_ref.at[pl.ds(c * r2, r2)], sem.at[W1_DMA_CHUNKS + c]))
    return copies


def _expert_kernel(layer, te_ref, tf_ref, ne_ref, nu_ref, src_ref, xo_ref, vr_ref, x_ref, b1_ref, b2_ref, wt_ref, s_ref,
                   w1_hbm, w2_hbm, o_hbm, w1s_ref, w2s_ref, w1p_ref, w2p_ref, acc_ref, out_ref, wsem, osem):
    i = pl.program_id(0)
    half = 128
    copies = functools.partial(_weight_copies, layer, w1_hbm=w1_hbm, w2_hbm=w2_hbm,
                               w1s_ref=w1s_ref, w2s_ref=w2s_ref, sem=wsem)

    @pl.when(i == 0)
    def _():
        acc_ref[...] = jnp.zeros_like(acc_ref)
        out_ref[...] = jnp.zeros_like(out_ref)
        for cp in copies(te_ref[0]):
            cp.start()

    @pl.when(tf_ref[i] == 1)
    def _():
        for cp in copies(te_ref[i]):
            cp.wait()
        s = s_ref[...].astype(BF16)
        for c in range(2 * D_FF // 256):
            blk = jnp.dot(w1s_ref[:, c * 256:(c + 1) * 256].astype(BF16), s, preferred_element_type=F32)
            w1p_ref[:, c * half:(c + 1) * half] = blk[:, :half].astype(BF16)
            w1p_ref[:, D_FF + c * half:D_FF + (c + 1) * half] = blk[:, half:].astype(BF16)
        w2p_ref[...] = w2s_ref[...].astype(BF16)

        @pl.when(ne_ref[i] >= 0)
        def _():
            for cp in copies(ne_ref[i]):
                cp.start()

    def tile_step(m):
        base = i * MOE_TILE
        prev = (i + 1) % 2
        for r0 in range(0, MOE_TILE, SCATTER_GROUP):
            toks = [pl.multiple_of(src_ref[base + r0 + g], ROW_CHUNKS) for g in range(SCATTER_GROUP)]
            cur = [acc_ref[pl.ds(toks[g], ROW_CHUNKS), :] for g in range(SCATTER_GROUP)]
            add = [out_ref[prev, (r0 + g) * ROW_CHUNKS:(r0 + g + 1) * ROW_CHUNKS, :] for g in range(SCATTER_GROUP)]
            for g in range(SCATTER_GROUP):
                acc_ref[pl.ds(toks[g], ROW_CHUNKS), :] = cur[g] + add[g]
        a = jnp.dot(_token_rows(x_ref, m).astype(BF16), w1p_ref[...], preferred_element_type=F32) + b1_ref[...]
        glu = jnp.minimum(a[:, :D_FF], SWIGLU_LIMIT)
        lin = jnp.clip(a[:, D_FF:], -SWIGLU_LIMIT, SWIGLU_LIMIT)
        hid = glu * _sigmoid(SWIGLU_ALPHA * glu) * (lin + 1.0)
        out = (jnp.dot(hid.astype(BF16), w2p_ref[...], preferred_element_type=F32) + b2_ref[...]) * wt_ref[0:m, :]
        cur_buf = i % 2
        for j in range(ROW_CHUNKS):
            out_ref[cur_buf, pl.ds(j, m, stride=ROW_CHUNKS), :] = out[:, j * LANES:(j + 1) * LANES]

    live = i <= nu_ref[0]
    pl.when(live & (vr_ref[i] > MOE_TILE // 2))(functools.partial(tile_step, MOE_TILE))
    pl.when(live & (vr_ref[i] <= MOE_TILE // 2))(functools.partial(tile_step, MOE_TILE // 2))

    @pl.when(i == pl.num_programs(0) - 1)
    def _():
        cp = pltpu.make_async_copy(acc_ref.at[pl.ds(0, T_ALL * ROW_CHUNKS)], o_hbm, osem)
        cp.start()
        cp.wait()


def _experts(layer, x_sorted, w_sorted, plan, w1, b1p, w2, b2):
    tile_expert, tile_first, next_expert, n_used, src, x_off, valid_rows = plan
    grid_spec = pltpu.PrefetchScalarGridSpec(
        num_scalar_prefetch=7,
        grid=(MOE_TILES,),
        in_specs=[
            pl.BlockSpec((pl.Element(MOE_TILE * ROW_CHUNKS), pl.Element(LANES)),
                         lambda i, te, tf, ne, nu, src, xo, vr: (pl.multiple_of(xo[i] * ROW_CHUNKS, X_ALIGN * ROW_CHUNKS), 0)),
            pl.BlockSpec((None, None, 1, 2 * D_FF), lambda i, te, *_: (layer, te[i], 0, 0)),
            pl.BlockSpec((None, None, 1, D_MODEL), lambda i, te, *_: (layer, te[i], 0, 0)),
            pl.BlockSpec((MOE_TILE, 1), lambda i, te, *_: (i, 0)),
            pl.BlockSpec((256, 256), lambda i, te, *_: (0, 0)),
            pl.BlockSpec(memory_space=pl.ANY),
            pl.BlockSpec(memory_space=pl.ANY),
        ],
        out_specs=pl.BlockSpec(memory_space=pl.ANY),
        scratch_shapes=[
            pltpu.VMEM((D_MODEL, 2 * D_FF), F32),
            pltpu.VMEM((D_FF, D_MODEL), F32),
            pltpu.VMEM((D_MODEL, 2 * D_FF), BF16),
            pltpu.VMEM((D_FF, D_MODEL), BF16),
            pltpu.VMEM((ACC_ROWS * ROW_CHUNKS, LANES), F32),
            pltpu.VMEM((2, MOE_TILE * ROW_CHUNKS, LANES), F32),
            pltpu.SemaphoreType.DMA((W1_DMA_CHUNKS + W2_DMA_CHUNKS,)),
            pltpu.SemaphoreType.DMA(()),
        ],
    )
    return pl.pallas_call(
        functools.partial(_expert_kernel, layer),
        grid_spec=grid_spec,
        out_shape=jax.ShapeDtypeStruct((T_ALL * ROW_CHUNKS, LANES), F32),
        compiler_params=_cparams(1),
    )(tile_expert, tile_first, next_expert, n_used, src, x_off, valid_rows, x_sorted, b1p, b2, w_sorted,
      _deinterleave_matrix(), w1, w2)


def _combine_kernel(first_tile, y_ref, a_ref, ada_ref, o_ref):
    gate = _ada_chunk(ada_ref, _cond_row(first_tile + pl.program_id(0)), 5)
    o_ref[...] = y_ref[...] + gate * _token_rows(a_ref, ROW_TILE)


def _combine(y, acc, ada_l, first_tile=0, n_tiles=N_ROW_TILES):
    return pl.pallas_call(
        functools.partial(_combine_kernel, first_tile),
        grid=(n_tiles,),
        in_specs=[
            pl.BlockSpec((ROW_TILE, D_MODEL), lambda i: (first_tile + i, 0)),
            pl.BlockSpec((ROW_TILE * ROW_CHUNKS, LANES), lambda i: (first_tile + i, 0)),
            pl.BlockSpec((COND_ROWS, ADA_CHUNKS * D_MODEL), lambda i: (0, 0)),
        ],
        out_specs=pl.BlockSpec((ROW_TILE, D_MODEL), lambda i: (i, 0)),
        out_shape=jax.ShapeDtypeStruct((n_tiles * ROW_TILE, D_MODEL), F32),
        compiler_params=_cparams(1),
    )(y, acc, ada_l)


def _routing_plan(idx, wts):
    eid = idx.reshape(-1)
    order = jnp.argsort(eid, stable=True).astype(jnp.int32)
    experts = jnp.arange(N_EXPERTS, dtype=jnp.int32)
    counts = jnp.sum(eid[:, None] == experts[None, :], axis=0).astype(jnp.int32)
    ntiles = (counts + MOE_TILE - 1) // MOE_TILE
    tile_end = jnp.cumsum(ntiles).astype(jnp.int32)
    tile_begin = tile_end - ntiles
    cstarts = (jnp.cumsum(counts) - counts).astype(jnp.int32)
    n_used = tile_end[-1]
    tile = jnp.arange(MOE_TILES, dtype=jnp.int32)
    te = jnp.minimum(jnp.sum(tile[:, None] >= tile_end[None, :], axis=1), N_EXPERTS - 1).astype(jnp.int32)
    used = tile < n_used
    prev = jnp.concatenate([jnp.full((1,), -1, jnp.int32), te[:-1]])
    first = (te != prev) & used

    def pick(onehot, table):
        return jnp.sum(jnp.where(onehot, table[None, :], 0), axis=1).astype(jnp.int32)

    tile_is = te[:, None] == experts[None, :]
    later = (experts[None, :] > experts[:, None]) & (ntiles[None, :] > 0)
    following = jnp.min(jnp.where(later, experts[None, :], N_EXPERTS), axis=1)
    following = jnp.where(following < N_EXPERTS, following, -1)
    next_expert = jnp.where(first, pick(tile_is, following), -1).astype(jnp.int32)
    tile_in_expert = tile - pick(tile_is, tile_begin)
    off = tile_in_expert[:, None] * MOE_TILE + jnp.arange(MOE_TILE, dtype=jnp.int32)[None, :]
    valid_rows = jnp.where(used, jnp.clip(pick(tile_is, counts) - tile_in_expert * MOE_TILE, 0, MOE_TILE), 0)
    valid = jnp.arange(MOE_TILE, dtype=jnp.int32)[None, :] < valid_rows[:, None]
    assign = order[jnp.clip(pick(tile_is, cstarts)[:, None] + off, 0, N_ASSIGN - 1)]
    token = assign // TOP_K
    src = jnp.where(valid, token, SPARE_ROW).reshape(MOE_ROWS).astype(jnp.int32)
    src = jnp.concatenate([jnp.full((MOE_TILE,), SPARE_ROW, jnp.int32), src]) * ROW_CHUNKS
    w_sorted = jnp.where(valid, wts.reshape(-1)[assign], 0.0).reshape(MOE_ROWS, 1)
    seg = ((counts + X_ALIGN - 1) // X_ALIGN) * X_ALIGN
    seg_end = jnp.cumsum(seg).astype(jnp.int32)
    seg_begin = seg_end - seg
    x_off = jnp.where(used, pick(tile_is, seg_begin) + tile_in_expert * MOE_TILE, 0).astype(jnp.int32)
    group = jnp.arange(X_ROWS // X_ALIGN, dtype=jnp.int32) * X_ALIGN
    group_is = (group[:, None] >= seg_begin[None, :]) & (group[:, None] < seg_end[None, :])
    xoffset = (group - pick(group_is, seg_begin))[:, None] + jnp.arange(X_ALIGN, dtype=jnp.int32)[None, :]
    xassign = order[jnp.clip(pick(group_is, cstarts)[:, None] + xoffset, 0, N_ASSIGN - 1)]
    gather_row = jnp.where(xoffset < pick(group_is, counts)[:, None], xassign // TOP_K, 0).reshape(X_ROWS)
    plan = (te, first.astype(jnp.int32), next_expert, n_used.reshape(1), src, x_off, valid_rows.astype(jnp.int32))
    return plan, gather_row, w_sorted


def _moe(layer, y, routed, ada_l, w1, b1p, w2, b2):
    h, idx_t, wts_t = routed
    plan, gather_row, w_sorted = _routing_plan(idx_t.T, wts_t.T)
    x_sorted = _dispatch(h, gather_row)
    acc = _experts(layer, x_sorted, w_sorted, plan, w1, b1p, w2, b2)
    if layer == DEPTH - 1:
        return (_combine(y, acc, ada_l, 0, P_TILES), _combine(y, acc, ada_l, P_TILES, N_ROW_TILES - P_TILES)), None
    return y, (acc, ada_l)


def kernel(x_prompt, x_sample, cache_attn_k, cache_attn_v, state_mlstm_C, state_mlstm_n, state_mlstm_m, c, c_ctx, ada_w, ada_b, norm_mix_g, norm_ffn_g, ab_w_in, ab_w_out, da_qnorm_g, da_knorm_g, da_lambda, da_subnorm_g, ml_conv_w, ml_conv_b, ml_gate_b, ml_headnorm_g, hy_w_in, hy_w_out, hy_conv_w, hy_conv_b, hy_f_w1, hy_f_b1, hy_f_freq1, hy_f_w2, hy_f_b2, hy_f_freq2, hy_f_w3, hy_bias, router_w, router_b, moe_w1, moe_b1, moe_w2, moe_b2):
    y = (x_prompt.reshape(T_P, D_MODEL), x_sample.reshape(T_S, D_MODEL))
    cond = jnp.concatenate([c_ctx[None, :], c, jnp.zeros((COND_ROWS - 1 - DEC_BATCH, D_MODEL), F32)], axis=0)
    ada = _ada_table(cond, ada_w, ada_b)
    b1p = moe_b1.reshape(DEPTH, N_EXPERTS, D_FF, 2).swapaxes(2, 3).reshape(DEPTH, N_EXPERTS, 1, 2 * D_FF)
    b2r = moe_b2.reshape(DEPTH, N_EXPERTS, 1, D_MODEL)
    new_k, new_v, new_c, new_n, new_m = [], [], [], [], []
    pending = None
    for layer in range(DEPTH):
        ada_l = ada[layer]
        route_prm = (norm_ffn_g[layer], router_w[layer], router_b[layer])
        if layer % 2 == 0:
            e = layer // 2
            lam_init = 0.8 - 0.6 * math.exp(-0.3 * layer)
            proj = _modulated_proj(y, ada_l, norm_mix_g[layer], ab_w_in[e],
                                   (3 * W_A, 2 * W_B, W_B, W_B, 4 * H_B), pending)
            if pending is not None:
                y = proj[-1]
            qkv, mqk, mv, mo, mg = proj[:5]
            qg2 = jnp.tile(da_qnorm_g[e], 2).reshape(1, 2 * HD_A)
            kg2 = jnp.tile(da_knorm_g[e], 2).reshape(1, 2 * HD_A)
            sub_g = da_subnorm_g[e].reshape(1, 2 * HD_A)
            oa_p, k_norm, v_heads = _attention_prompt(qkv, qg2, kg2, da_lambda[e], sub_g, lam_init)
            cos, sin = _rope_tables()
            oa_s = _attention_sample(
                qkv, cache_attn_k[:, e].reshape(DEC_BATCH, PAST_LEN, W_A),
                cache_attn_v[:, e].reshape(DEC_BATCH, PAST_LEN, W_A), cos, sin,
                qg2, kg2, da_lambda[e], sub_g, lam_init)
            ob_p, c_new, n_new, m_new = _mlstm(
                mqk, mv, mo, mg[:T_P], ml_conv_w[e], ml_conv_b[e].reshape(1, 2 * W_B), ml_gate_b[e],
                ml_headnorm_g[e], seq=SEQ, nbatch=BATCH, row_off=0, group=MLSTM_GROUP)
            ob_s = _mlstm(
                mqk, mv, mo, mg[T_P:], ml_conv_w[e], ml_conv_b[e].reshape(1, 2 * W_B), ml_gate_b[e],
                ml_headnorm_g[e], seq=DEC_SEQ, nbatch=DEC_BATCH, row_off=T_P // DEC_SEQ, group=1,
                ctx=(state_mlstm_C[:, e], state_mlstm_n[:, e], state_mlstm_m[:, e]))
            y, routed = _out_proj_residual([(oa_p, oa_s), (ob_p, ob_s)], y, ada_l, ab_w_out[e], 2, *route_prm)
            new_k.append(k_norm.reshape(BATCH, SEQ, H_A, 2, HD_A))
            new_v.append(v_heads.reshape(BATCH, SEQ, H_A, 2 * HD_A))
            new_c.append(c_new)
            new_n.append(n_new.reshape(BATCH, 2, H_B, HD_B))
            new_m.append(m_new[..., 0, 0])
        else:
            o = layer // 2
            proj = _modulated_proj(y, ada_l, norm_mix_g[layer], hy_w_in[o], (HY_PROJ,), pending)
            if pending is not None:
                y = proj[-1]
            zproj = proj[0]
            cores = []
            for seq, nbatch, row_off, td in ((SEQ, BATCH, 0, D_MODEL), (DEC_SEQ, DEC_BATCH, T_P // DEC_SEQ, 256)):
                fwd, inv = _dft_mats(seq)
                fwd_bf, inv_bf = fwd.astype(BF16), inv.astype(BF16)
                kspec = _hyena_filter_spectrum(seq, fwd_bf, hy_f_w1[o], hy_f_b1[o], hy_f_freq1[o], hy_f_w2[o],
                                               hy_f_b2[o], hy_f_freq2[o], hy_f_w3[o])
                cores.append(_hyena_core(zproj, hy_conv_w[o], hy_conv_b[o].reshape(1, HY_PROJ), fwd_bf, inv_bf,
                                         kspec, hy_bias[o], seq=seq, nbatch=nbatch, row_off=row_off, td=td))
            y, routed = _out_proj_residual([tuple(cores)], y, ada_l, hy_w_out[o], 2, *route_prm)
        y, pending = _moe(layer, y, routed, ada_l, moe_w1, b1p, moe_w2, b2r)
    y_p = y[0].reshape(BATCH, SEQ, D_MODEL)
    y_s = y[1].reshape(DEC_BATCH, DEC_SEQ, D_MODEL)
    return (y_p, y_s, jnp.stack(new_k, axis=1), jnp.stack(new_v, axis=1), jnp.stack(new_c, axis=1),
            jnp.stack(new_n, axis=1), jnp.stack(new_m, axis=1))
```
